```python
import math
import jax, jax.numpy as jnp
from jax import lax
import numpy as np

D_MODEL = 1024
BATCH = 8
SEQ = 4096
DEPTH = 2

HEAD_DIM = 64
N_HEADS_A = D_MODEL // (2 * HEAD_DIM)
N_HEADS_B = D_MODEL // (2 * HEAD_DIM)
N_KV_B = max(1, N_HEADS_B // 4)
GROUP_B = N_HEADS_B // N_KV_B
MIX_WIDTH = (N_HEADS_A + N_HEADS_B) * HEAD_DIM
DILATED_CONFIGS = ((128, 1), (512, 4), (2048, 16))
SWA_RADIUS = 128
N_BUCKETS = 32
MAX_DISTANCE = 1024
D_FF = ((8 * D_MODEL // 3 + 127) // 128) * 128
PLE_DIM = 256
EPS = 1e-6
QBLOCK = 128
NEG = -1e30

QKV_WIDTHS = (N_HEADS_A * HEAD_DIM, N_HEADS_A * HEAD_DIM, N_HEADS_A * HEAD_DIM,
              N_HEADS_B * HEAD_DIM, N_KV_B * HEAD_DIM, N_KV_B * HEAD_DIM)
QKV_WIDTH = sum(QKV_WIDTHS)

kernel_name = "hybrid_dilated_swa_macaron_encoder"


def rms_norm(x, g):
    xf = x.astype(jnp.float32)
    y = xf * lax.rsqrt(jnp.mean(xf * xf, axis=-1, keepdims=True) + EPS) * g.astype(jnp.float32)
    return y.astype(x.dtype)


def swiglu(h, w_in, w_out):
    gate, up = jnp.split(h @ w_in, 2, axis=-1)
    return (jax.nn.silu(gate) * up) @ w_out


def t5_bucket(rel):
    half = N_BUCKETS // 2
    max_exact = half // 2
    ret = jnp.where(rel > 0, half, 0)
    n = jnp.abs(rel)
    nf = jnp.maximum(n, 1).astype(jnp.float32)
    large = max_exact + (jnp.log(nf / max_exact) / math.log(MAX_DISTANCE / max_exact)
                         * (half - max_exact)).astype(jnp.int32)
    large = jnp.minimum(large, half - 1)
    return ret + jnp.where(n < max_exact, n, large)


def banded_attention(q, k, v, radius, dilation, bias_heads, sink=None):
    N, Hk, G, L, E = q.shape
    bq = math.gcd(L, QBLOCK)
    nb = L // bq
    W = bq + 2 * radius
    pad = ((0, 0), (0, 0), (radius, radius), (0, 0))
    idx = (jnp.arange(nb) * bq)[:, None] + jnp.arange(W)[None, :]
    kb = jnp.pad(k, pad)[:, :, idx]
    vb = jnp.pad(v, pad)[:, :, idx]
    qb = q.reshape(N, Hk, G, nb, bq, E)
    logits = jnp.einsum('nhgbqe,nhbke->nhgbqk', qb, kb,
                        preferred_element_type=jnp.float32) * (E ** -0.5)
    rel = jnp.arange(W)[None, :] - radius - jnp.arange(bq)[:, None]
    bias = bias_heads[t5_bucket(rel * dilation)].astype(jnp.float32)
    bias = jnp.transpose(bias.reshape(bq, W, Hk, G), (2, 3, 0, 1))[:, :, None]
    in_band = jnp.abs(rel) <= radius
    in_seq = (idx >= radius) & (idx < radius + L)
    valid = in_band[None] & in_seq[:, None, :]
    logits = jnp.where(valid, logits + bias, NEG)
    m = jnp.max(logits, axis=-1)
    if sink is not None:
        sink_b = sink.astype(jnp.float32)[None, :, :, None, None]
        m = jnp.maximum(m, sink_b)
    pr = jnp.exp(logits - m[..., None])
    denom = jnp.sum(pr, axis=-1)
    if sink is not None:
        denom = denom + jnp.exp(sink_b - m)
    out = jnp.einsum('nhgbqk,nhbke->nhgbqe', pr.astype(v.dtype), vb,
                     preferred_element_type=jnp.float32) / denom[..., None]
    lse = m + jnp.log(denom)
    return out.reshape(N, Hk, G, L, E).astype(q.dtype), lse.reshape(N, Hk, G, L)


def dilated_attention(q, k, v, bias_heads):
    B, H, S, E = q.shape
    outs, lses = [], []
    for window, d in DILATED_CONFIGS:
        L = S // d
        def split(t):
            return t.reshape(B, H, L, d, E).transpose(0, 3, 1, 2, 4).reshape(B * d, H, L, E)
        o, lse = banded_attention(split(q)[:, :, None], split(k), split(v),
                                  window // (2 * d), d, bias_heads)
        outs.append(o[:, :, 0].reshape(B, d, H, L, E).transpose(0, 2, 3, 1, 4).reshape(B, H, S, E))
        lses.append(lse[:, :, 0].reshape(B, d, H, L).transpose(0, 2, 3, 1).reshape(B, H, S))
    wts = jax.nn.softmax(jnp.stack(lses), axis=0)
    return jnp.einsum('cbhs,cbhse->bhse', wts, jnp.stack(outs).astype(jnp.float32)).astype(q.dtype)


def _fwd_setup_inputs(seed: int = 0) -> dict:
    key = jax.random.key(seed)
    ks = jax.random.split(key, 20)
    f32 = jnp.float32

    def nrm(k, shape):
        return jax.random.normal(k, shape, f32)

    def w(k, shape, fan_in):
        return nrm(k, shape) * fan_in ** -0.5

    def gain(k, shape):
        return 1.0 + 0.05 * nrm(k, shape)

    return {
        "x": nrm(ks[0], (BATCH, SEQ, D_MODEL)),
        "p": nrm(ks[1], (DEPTH, BATCH, SEQ, PLE_DIM)),
        "rel_bias": 0.5 * nrm(ks[2], (N_BUCKETS, N_HEADS_A + N_HEADS_B)),
        "norm_ffn1": gain(ks[3], (DEPTH, D_MODEL)),
        "ffn1_w_in": w(ks[4], (DEPTH, D_MODEL, 2 * D_FF), D_MODEL),
        "ffn1_w_out": w(ks[5], (DEPTH, D_FF, D_MODEL), D_FF),
        "norm_mix": gain(ks[6], (DEPTH, D_MODEL)),
        "w_qkv": w(ks[7], (DEPTH, D_MODEL, QKV_WIDTH), D_MODEL),
        "q_norm_a": gain(ks[8], (DEPTH, HEAD_DIM)),
        "k_norm_a": gain(ks[9], (DEPTH, HEAD_DIM)),
        "q_norm_b": gain(ks[10], (DEPTH, HEAD_DIM)),
        "k_norm_b": gain(ks[11], (DEPTH, HEAD_DIM)),
        "sink_b": 0.5 * nrm(ks[12], (DEPTH, N_HEADS_B)),
        "w_o": w(ks[13], (DEPTH, MIX_WIDTH, D_MODEL), MIX_WIDTH),
        "norm_ffn2": gain(ks[14], (DEPTH, D_MODEL)),
        "ffn2_w_in": w(ks[15], (DEPTH, D_MODEL, 2 * D_FF), D_MODEL),
        "ffn2_w_out": w(ks[16], (DEPTH, D_FF, D_MODEL), D_FF),
        "norm_ple": gain(ks[17], (DEPTH, D_MODEL)),
        "w_ple_gate": w(ks[18], (DEPTH, D_MODEL, D_MODEL), D_MODEL),
        "w_ple_proj": w(ks[19], (DEPTH, PLE_DIM, D_MODEL), PLE_DIM),
    }


def _fwd_reference(x, p, rel_bias, norm_ffn1, ffn1_w_in, ffn1_w_out, norm_mix, w_qkv,
              q_norm_a, k_norm_a, q_norm_b, k_norm_b, sink_b, w_o, norm_ffn2,
              ffn2_w_in, ffn2_w_out, norm_ple, w_ple_gate, w_ple_proj):
    B, S, _ = x.shape
    split_at = [int(c) for c in np.cumsum(QKV_WIDTHS)[:-1]]
    bias_a = rel_bias[:, :N_HEADS_A]
    bias_b = rel_bias[:, N_HEADS_A:]

    def heads(t, n):
        return t.reshape(B, S, n, HEAD_DIM).transpose(0, 2, 1, 3)

    for i in range(DEPTH):
        x = x + 0.5 * swiglu(rms_norm(x, norm_ffn1[i]), ffn1_w_in[i], ffn1_w_out[i])

        h = rms_norm(x, norm_mix[i])
        qa, ka, va, qb, kb, vb = jnp.split(h @ w_qkv[i], split_at, axis=-1)

        qa = rms_norm(heads(qa, N_HEADS_A), q_norm_a[i])
        ka = rms_norm(heads(ka, N_HEADS_A), k_norm_a[i])
        oa = dilated_attention(qa, ka, heads(va, N_HEADS_A), bias_a)

        qb = rms_norm(heads(qb, N_HEADS_B), q_norm_b[i]).reshape(B, N_KV_B, GROUP_B, S, HEAD_DIM)
        kb = rms_norm(heads(kb, N_KV_B), k_norm_b[i])
        ob, _ = banded_attention(qb, kb, heads(vb, N_KV_B), SWA_RADIUS, 1, bias_b,
                                 sink_b[i].reshape(N_KV_B, GROUP_B))
        ob = ob.reshape(B, N_HEADS_B, S, HEAD_DIM)

        o = jnp.concatenate([oa, ob], axis=1).transpose(0, 2, 1, 3).reshape(B, S, MIX_WIDTH)
        x = x + o @ w_o[i]

        x = x + 0.5 * swiglu(rms_norm(x, norm_ffn2[i]), ffn2_w_in[i], ffn2_w_out[i])

        gate = jax.nn.sigmoid(rms_norm(x, norm_ple[i]) @ w_ple_gate[i])
        x = x + gate * (p[i] @ w_ple_proj[i])
    return x


import jax as _jax
import jax.numpy as _jnp

TWIN_FORMAT = 'train_step'
FWD_PARAMS = ['x', 'p', 'rel_bias', 'norm_ffn1', 'ffn1_w_in', 'ffn1_w_out', 'norm_mix', 'w_qkv', 'q_norm_a', 'k_norm_a', 'q_norm_b', 'k_norm_b', 'sink_b', 'w_o', 'norm_ffn2', 'ffn2_w_in', 'ffn2_w_out', 'norm_ple', 'w_ple_gate', 'w_ple_proj']
TWIN_WEIGHTS = ['rel_bias', 'norm_ffn1', 'ffn1_w_in', 'ffn1_w_out', 'norm_mix', 'w_qkv', 'q_norm_a', 'k_norm_a', 'q_norm_b', 'k_norm_b', 'sink_b', 'w_o', 'norm_ffn2', 'ffn2_w_in', 'ffn2_w_out', 'norm_ple', 'w_ple_gate', 'w_ple_proj']
TWIN_DIFF_INPUT = 'x'
TWIN_INPUTS = ['x', 'p', 'rel_bias', 'norm_ffn1', 'ffn1_w_in', 'ffn1_w_out', 'norm_mix', 'w_qkv', 'q_norm_a', 'k_norm_a', 'q_norm_b', 'k_norm_b', 'sink_b', 'w_o', 'norm_ffn2', 'ffn2_w_in', 'ffn2_w_out', 'norm_ple', 'w_ple_gate', 'w_ple_proj', 'loss_target', 'm_rel_bias', 'm_norm_ffn1', 'm_ffn1_w_in', 'm_ffn1_w_out', 'm_norm_mix', 'm_w_qkv', 'm_q_norm_a', 'm_k_norm_a', 'm_q_norm_b', 'm_k_norm_b', 'm_sink_b', 'm_w_o', 'm_norm_ffn2', 'm_ffn2_w_in', 'm_ffn2_w_out', 'm_norm_ple', 'm_w_ple_gate', 'm_w_ple_proj', 'v_rel_bias', 'v_norm_ffn1', 'v_ffn1_w_in', 'v_ffn1_w_out', 'v_norm_mix', 'v_w_qkv', 'v_q_norm_a', 'v_k_norm_a', 'v_q_norm_b', 'v_k_norm_b', 'v_sink_b', 'v_w_o', 'v_norm_ffn2', 'v_ffn2_w_in', 'v_ffn2_w_out', 'v_norm_ple', 'v_w_ple_gate', 'v_w_ple_proj']
TWIN_OUTPUTS = ['loss', 'grad_x', 'grad_rel_bias', 'grad_norm_ffn1', 'grad_ffn1_w_in', 'grad_ffn1_w_out', 'grad_norm_mix', 'grad_w_qkv', 'grad_q_norm_a', 'grad_k_norm_a', 'grad_q_norm_b', 'grad_k_norm_b', 'grad_sink_b', 'grad_w_o', 'grad_norm_ffn2', 'grad_ffn2_w_in', 'grad_ffn2_w_out', 'grad_norm_ple', 'grad_w_ple_gate', 'grad_w_ple_proj', 'delta_rel_bias', 'delta_norm_ffn1', 'delta_ffn1_w_in', 'delta_ffn1_w_out', 'delta_norm_mix', 'delta_w_qkv', 'delta_q_norm_a', 'delta_k_norm_a', 'delta_q_norm_b', 'delta_k_norm_b', 'delta_sink_b', 'delta_w_o', 'delta_norm_ffn2', 'delta_ffn2_w_in', 'delta_ffn2_w_out', 'delta_norm_ple', 'delta_w_ple_gate', 'delta_w_ple_proj', 'new_m_rel_bias', 'new_m_norm_ffn1', 'new_m_ffn1_w_in', 'new_m_ffn1_w_out', 'new_m_norm_mix', 'new_m_w_qkv', 'new_m_q_norm_a', 'new_m_k_norm_a', 'new_m_q_norm_b', 'new_m_k_norm_b', 'new_m_sink_b', 'new_m_w_o', 'new_m_norm_ffn2', 'new_m_ffn2_w_in', 'new_m_ffn2_w_out', 'new_m_norm_ple', 'new_m_w_ple_gate', 'new_m_w_ple_proj', 'new_v_rel_bias', 'new_v_norm_ffn1', 'new_v_ffn1_w_in', 'new_v_ffn1_w_out', 'new_v_norm_mix', 'new_v_w_qkv', 'new_v_q_norm_a', 'new_v_k_norm_a', 'new_v_q_norm_b', 'new_v_k_norm_b', 'new_v_sink_b', 'new_v_w_o', 'new_v_norm_ffn2', 'new_v_ffn2_w_in', 'new_v_ffn2_w_out', 'new_v_norm_ple', 'new_v_w_ple_gate', 'new_v_w_ple_proj']
TWIN_LEAF_KINDS = {'loss': 'loss', 'grad_x': 'grad_x', 'grad_rel_bias': 'grad_w', 'grad_norm_ffn1': 'grad_w', 'grad_ffn1_w_in': 'grad_w', 'grad_ffn1_w_out': 'grad_w', 'grad_norm_mix': 'grad_w', 'grad_w_qkv': 'grad_w', 'grad_q_norm_a': 'grad_w', 'grad_k_norm_a': 'grad_w', 'grad_q_norm_b': 'grad_w', 'grad_k_norm_b': 'grad_w', 'grad_sink_b': 'grad_w', 'grad_w_o': 'grad_w', 'grad_norm_ffn2': 'grad_w', 'grad_ffn2_w_in': 'grad_w', 'grad_ffn2_w_out': 'grad_w', 'grad_norm_ple': 'grad_w', 'grad_w_ple_gate': 'grad_w', 'grad_w_ple_proj': 'grad_w', 'delta_rel_bias': 'delta_w', 'delta_norm_ffn1': 'delta_w', 'delta_ffn1_w_in': 'delta_w', 'delta_ffn1_w_out': 'delta_w', 'delta_norm_mix': 'delta_w', 'delta_w_qkv': 'delta_w', 'delta_q_norm_a': 'delta_w', 'delta_k_norm_a': 'delta_w', 'delta_q_norm_b': 'delta_w', 'delta_k_norm_b': 'delta_w', 'delta_sink_b': 'delta_w', 'delta_w_o': 'delta_w', 'delta_norm_ffn2': 'delta_w', 'delta_ffn2_w_in': 'delta_w', 'delta_ffn2_w_out': 'delta_w', 'delta_norm_ple': 'delta_w', 'delta_w_ple_gate': 'delta_w', 'delta_w_ple_proj': 'delta_w', 'new_m_rel_bias': 'new_m', 'new_m_norm_ffn1': 'new_m', 'new_m_ffn1_w_in': 'new_m', 'new_m_ffn1_w_out': 'new_m', 'new_m_norm_mix': 'new_m', 'new_m_w_qkv': 'new_m', 'new_m_q_norm_a': 'new_m', 'new_m_k_norm_a': 'new_m', 'new_m_q_norm_b': 'new_m', 'new_m_k_norm_b': 'new_m', 'new_m_sink_b': 'new_m', 'new_m_w_o': 'new_m', 'new_m_norm_ffn2': 'new_m', 'new_m_ffn2_w_in': 'new_m', 'new_m_ffn2_w_out': 'new_m', 'new_m_norm_ple': 'new_m', 'new_m_w_ple_gate': 'new_m', 'new_m_w_ple_proj': 'new_m', 'new_v_rel_bias': 'new_v', 'new_v_norm_ffn1': 'new_v', 'new_v_ffn1_w_in': 'new_v', 'new_v_ffn1_w_out': 'new_v', 'new_v_norm_mix': 'new_v', 'new_v_w_qkv': 'new_v', 'new_v_q_norm_a': 'new_v', 'new_v_k_norm_a': 'new_v', 'new_v_q_norm_b': 'new_v', 'new_v_k_norm_b': 'new_v', 'new_v_sink_b': 'new_v', 'new_v_w_o': 'new_v', 'new_v_norm_ffn2': 'new_v', 'new_v_ffn2_w_in': 'new_v', 'new_v_ffn2_w_out': 'new_v', 'new_v_norm_ple': 'new_v', 'new_v_w_ple_gate': 'new_v', 'new_v_w_ple_proj': 'new_v'}


def _forward(args):
    return _fwd_reference(*[args[k] for k in FWD_PARAMS])


def _output_shape():
    def fwd():
        inp = _fwd_setup_inputs(0)
        return _fwd_reference(*[inp[k] for k in FWD_PARAMS])
    out = _jax.eval_shape(fwd)
    return out.shape, out.dtype

N_MICROBATCH = 1
ADAM_LR = 0.001
ADAM_B1 = 0.9
ADAM_B2 = 0.999
ADAM_EPS = 1e-08
ADAM_WD = 0.01
ADAM_STEP = 10
PER_EXAMPLE_BATCH_AXIS = {'x': 0, 'p': 1, 'loss_target': 0}
SHARED_INPUTS = []
_WEIGHT_DTYPES = {'rel_bias': _jnp.float32, 'norm_ffn1': _jnp.float32, 'ffn1_w_in': _jnp.float32, 'ffn1_w_out': _jnp.float32, 'norm_mix': _jnp.float32, 'w_qkv': _jnp.float32, 'q_norm_a': _jnp.float32, 'k_norm_a': _jnp.float32, 'q_norm_b': _jnp.float32, 'k_norm_b': _jnp.float32, 'sink_b': _jnp.float32, 'w_o': _jnp.float32, 'norm_ffn2': _jnp.float32, 'ffn2_w_in': _jnp.float32, 'ffn2_w_out': _jnp.float32, 'norm_ple': _jnp.float32, 'w_ple_gate': _jnp.float32, 'w_ple_proj': _jnp.float32}
MOMENT_SCALE = {'rel_bias': 1.250477e+00, 'norm_ffn1': 6.177859e+00, 'ffn1_w_in': 8.371150e-02, 'ffn1_w_out': 1.470273e-01, 'norm_mix': 3.682550e-01, 'w_qkv': 7.231630e-02, 'q_norm_a': 2.329568e+00, 'k_norm_a': 2.329281e+00, 'q_norm_b': 2.489800e+00, 'k_norm_b': 2.474215e+00, 'sink_b': 5.815535e-02, 'w_o': 7.235080e-02, 'norm_ffn2': 6.161272e+00, 'ffn2_w_in': 7.941979e-02, 'ffn2_w_out': 1.417045e-01, 'norm_ple': 9.870003e-01, 'w_ple_gate': 8.771179e-02, 'w_ple_proj': 5.121498e-01}


def _to_microbatches(a, axis):
    t = _jnp.moveaxis(a, axis, 0)
    t = t.reshape((N_MICROBATCH, t.shape[0] // N_MICROBATCH) + t.shape[1:])
    return _jnp.moveaxis(t, 1, axis + 1)


def setup_inputs(seed: int = 0) -> dict:
    inp = _fwd_setup_inputs(seed)
    key = _jax.random.fold_in(_jax.random.key(seed), 7919)
    shape, _ = _output_shape()
    out = dict(inp)
    out["loss_target"] = _jax.random.normal(_jax.random.fold_in(key, 0), shape, _jnp.float32)
    for i, name in enumerate(TWIN_WEIGHTS):
        w = inp[name].astype(_jnp.float32)
        if MOMENT_SCALE is None:
            s = _jnp.sqrt(_jnp.mean(_jnp.square(w)) + 1e-30)
        else:
            s = MOMENT_SCALE[name]
        km, kv = _jax.random.split(_jax.random.fold_in(key, i + 1))
        out[name] = w
        out["m_" + name] = s * _jax.random.normal(km, w.shape, _jnp.float32)
        out["v_" + name] = (s * s) * _jax.random.uniform(kv, w.shape, _jnp.float32, 0.5, 1.5)
    if N_MICROBATCH > 1:
        for name, axis in PER_EXAMPLE_BATCH_AXIS.items():
            out[name] = _to_microbatches(out[name], axis)
    return {'x': out['x'], 'p': out['p'], 'rel_bias': out['rel_bias'], 'norm_ffn1': out['norm_ffn1'], 'ffn1_w_in': out['ffn1_w_in'], 'ffn1_w_out': out['ffn1_w_out'], 'norm_mix': out['norm_mix'], 'w_qkv': out['w_qkv'], 'q_norm_a': out['q_norm_a'], 'k_norm_a': out['k_norm_a'], 'q_norm_b': out['q_norm_b'], 'k_norm_b': out['k_norm_b'], 'sink_b': out['sink_b'], 'w_o': out['w_o'], 'norm_ffn2': out['norm_ffn2'], 'ffn2_w_in': out['ffn2_w_in'], 'ffn2_w_out': out['ffn2_w_out'], 'norm_ple': out['norm_ple'], 'w_ple_gate': out['w_ple_gate'], 'w_ple_proj': out['w_ple_proj'], 'loss_target': out['loss_target'], 'm_rel_bias': out['m_rel_bias'], 'm_norm_ffn1': out['m_norm_ffn1'], 'm_ffn1_w_in': out['m_ffn1_w_in'], 'm_ffn1_w_out': out['m_ffn1_w_out'], 'm_norm_mix': out['m_norm_mix'], 'm_w_qkv': out['m_w_qkv'], 'm_q_norm_a': out['m_q_norm_a'], 'm_k_norm_a': out['m_k_norm_a'], 'm_q_norm_b': out['m_q_norm_b'], 'm_k_norm_b': out['m_k_norm_b'], 'm_sink_b': out['m_sink_b'], 'm_w_o': out['m_w_o'], 'm_norm_ffn2': out['m_norm_ffn2'], 'm_ffn2_w_in': out['m_ffn2_w_in'], 'm_ffn2_w_out': out['m_ffn2_w_out'], 'm_norm_ple': out['m_norm_ple'], 'm_w_ple_gate': out['m_w_ple_gate'], 'm_w_ple_proj': out['m_w_ple_proj'], 'v_rel_bias': out['v_rel_bias'], 'v_norm_ffn1': out['v_norm_ffn1'], 'v_ffn1_w_in': out['v_ffn1_w_in'], 'v_ffn1_w_out': out['v_ffn1_w_out'], 'v_norm_mix': out['v_norm_mix'], 'v_w_qkv': out['v_w_qkv'], 'v_q_norm_a': out['v_q_norm_a'], 'v_k_norm_a': out['v_k_norm_a'], 'v_q_norm_b': out['v_q_norm_b'], 'v_k_norm_b': out['v_k_norm_b'], 'v_sink_b': out['v_sink_b'], 'v_w_o': out['v_w_o'], 'v_norm_ffn2': out['v_norm_ffn2'], 'v_ffn2_w_in': out['v_ffn2_w_in'], 'v_ffn2_w_out': out['v_ffn2_w_out'], 'v_norm_ple': out['v_norm_ple'], 'v_w_ple_gate': out['v_w_ple_gate'], 'v_w_ple_proj': out['v_w_ple_proj']}


def _loss(weights, diff, rest, loss_target):
    with _jax.named_scope("forward"):
        args = {**rest, TWIN_DIFF_INPUT: diff, **{k: w.astype(_WEIGHT_DTYPES[k]) for k, w in weights.items()}}
        y = _forward(args)
    with _jax.named_scope("loss_head"):
        err = _jnp.square(y.astype(_jnp.float32) - loss_target)
        return 0.5 * _jnp.sum(_jnp.mean(err, axis=-1)) if err.ndim else 0.5 * err


def _adamw(w, g, m, v):
    m = ADAM_B1 * m + (1.0 - ADAM_B1) * g
    v = ADAM_B2 * v + (1.0 - ADAM_B2) * _jnp.square(g)
    m_hat = m / (1.0 - ADAM_B1 ** ADAM_STEP)
    v_hat = v / (1.0 - ADAM_B2 ** ADAM_STEP)
    delta = -ADAM_LR * (m_hat / (_jnp.sqrt(v_hat) + ADAM_EPS) + ADAM_WD * w)
    return delta, m, v


def reference(x, p, rel_bias, norm_ffn1, ffn1_w_in, ffn1_w_out, norm_mix, w_qkv, q_norm_a, k_norm_a, q_norm_b, k_norm_b, sink_b, w_o, norm_ffn2, ffn2_w_in, ffn2_w_out, norm_ple, w_ple_gate, w_ple_proj, loss_target, m_rel_bias, m_norm_ffn1, m_ffn1_w_in, m_ffn1_w_out, m_norm_mix, m_w_qkv, m_q_norm_a, m_k_norm_a, m_q_norm_b, m_k_norm_b, m_sink_b, m_w_o, m_norm_ffn2, m_ffn2_w_in, m_ffn2_w_out, m_norm_ple, m_w_ple_gate, m_w_ple_proj, v_rel_bias, v_norm_ffn1, v_ffn1_w_in, v_ffn1_w_out, v_norm_mix, v_w_qkv, v_q_norm_a, v_k_norm_a, v_q_norm_b, v_k_norm_b, v_sink_b, v_w_o, v_norm_ffn2, v_ffn2_w_in, v_ffn2_w_out, v_norm_ple, v_w_ple_gate, v_w_ple_proj):
    given = dict(x=x, p=p, rel_bias=rel_bias, norm_ffn1=norm_ffn1, ffn1_w_in=ffn1_w_in, ffn1_w_out=ffn1_w_out, norm_mix=norm_mix, w_qkv=w_qkv, q_norm_a=q_norm_a, k_norm_a=k_norm_a, q_norm_b=q_norm_b, k_norm_b=k_norm_b, sink_b=sink_b, w_o=w_o, norm_ffn2=norm_ffn2, ffn2_w_in=ffn2_w_in, ffn2_w_out=ffn2_w_out, norm_ple=norm_ple, w_ple_gate=w_ple_gate, w_ple_proj=w_ple_proj, loss_target=loss_target, m_rel_bias=m_rel_bias, m_norm_ffn1=m_norm_ffn1, m_ffn1_w_in=m_ffn1_w_in, m_ffn1_w_out=m_ffn1_w_out, m_norm_mix=m_norm_mix, m_w_qkv=m_w_qkv, m_q_norm_a=m_q_norm_a, m_k_norm_a=m_k_norm_a, m_q_norm_b=m_q_norm_b, m_k_norm_b=m_k_norm_b, m_sink_b=m_sink_b, m_w_o=m_w_o, m_norm_ffn2=m_norm_ffn2, m_ffn2_w_in=m_ffn2_w_in, m_ffn2_w_out=m_ffn2_w_out, m_norm_ple=m_norm_ple, m_w_ple_gate=m_w_ple_gate, m_w_ple_proj=m_w_ple_proj, v_rel_bias=v_rel_bias, v_norm_ffn1=v_norm_ffn1, v_ffn1_w_in=v_ffn1_w_in, v_ffn1_w_out=v_ffn1_w_out, v_norm_mix=v_norm_mix, v_w_qkv=v_w_qkv, v_q_norm_a=v_q_norm_a, v_k_norm_a=v_k_norm_a, v_q_norm_b=v_q_norm_b, v_k_norm_b=v_k_norm_b, v_sink_b=v_sink_b, v_w_o=v_w_o, v_norm_ffn2=v_norm_ffn2, v_ffn2_w_in=v_ffn2_w_in, v_ffn2_w_out=v_ffn2_w_out, v_norm_ple=v_norm_ple, v_w_ple_gate=v_w_ple_gate, v_w_ple_proj=v_w_ple_proj)
    weights = {n: given[n] for n in TWIN_WEIGHTS}
    shared = {n: given[n] for n in SHARED_INPUTS}
    per_example = {n: given[n] for n in ['x', 'p']}
    grad_fn = _jax.value_and_grad(_loss, argnums=(0, 1))

    def one_microbatch(ex, loss_target):
        ex = dict(ex)
        diff = ex.pop(TWIN_DIFF_INPUT)
        return grad_fn(weights, diff, {**shared, **ex}, loss_target)

    if N_MICROBATCH == 1:
        loss, (grad_w, grad_x) = one_microbatch(per_example, given["loss_target"])
    else:
        def body(carry, xs):
            loss_sum, grad_sum = carry
            l_k, (gw_k, gx_k) = one_microbatch(xs[0], xs[1])
            with _jax.named_scope("update"):
                return (loss_sum + l_k, _jax.tree.map(_jnp.add, grad_sum, gw_k)), gx_k

        init = (_jnp.zeros((), _jnp.float32), _jax.tree.map(_jnp.zeros_like, weights))
        (loss, grad_w), grad_x = _jax.lax.scan(body, init, (per_example, given["loss_target"]))
    with _jax.named_scope("update"):
        delta_w, new_m, new_v = {}, {}, {}
        for n in TWIN_WEIGHTS:
            delta_w[n], new_m[n], new_v[n] = _adamw(weights[n], grad_w[n], given["m_" + n], given["v_" + n])
    return (loss, grad_x, *[grad_w[n] for n in TWIN_WEIGHTS], *[delta_w[n] for n in TWIN_WEIGHTS],
            *[new_m[n] for n in TWIN_WEIGHTS], *[new_v[n] for n in TWIN_WEIGHTS])
```

```python
import functools
import math

import numpy as np
import jax
import jax.numpy as jnp
from jax import lax
from jax.experimental import pallas as pl
from jax.experimental.pallas import tpu as pltpu

F32 = jnp.float32
BF16 = jnp.bfloat16
MESH = pl.DeviceIdType.MESH

HEAD_DIM = 64
LANES = 128
QBLOCK = 128
N_BUCKETS = 32
MAX_DISTANCE = 1024
DILATED_CONFIGS = ((128, 1), (512, 4), (2048, 16))
SWA_RADIUS = 128
GROUP_B = 4
EPS = 1e-6
NEG = -1e30
Q_SCALE = HEAD_DIM ** -0.5
ADAM_LR, ADAM_B1, ADAM_B2, ADAM_EPS, ADAM_WD, ADAM_STEP = 0.001, 0.9, 0.999, 1e-08, 0.01, 10
VMEM_LIMIT = 56 * 2 ** 20
N_CHIPS = 4
N_DEV = 8

BIG = ("ffn1_w_in", "ffn1_w_out", "w_qkv", "w_o", "ffn2_w_in", "ffn2_w_out", "w_ple_gate", "w_ple_proj")
SMALL = ("rel_bias", "norm_ffn1", "norm_mix", "q_norm_a", "k_norm_a", "q_norm_b", "k_norm_b", "sink_b",
         "norm_ffn2", "norm_ple")
WEIGHTS = ("rel_bias", "norm_ffn1", "ffn1_w_in", "ffn1_w_out", "norm_mix", "w_qkv", "q_norm_a", "k_norm_a",
           "q_norm_b", "k_norm_b", "sink_b", "w_o", "norm_ffn2", "ffn2_w_in", "ffn2_w_out", "norm_ple",
           "w_ple_gate", "w_ple_proj")


def _params(n_grid):
    return pltpu.CompilerParams(dimension_semantics=("arbitrary",) * n_grid, vmem_limit_bytes=VMEM_LIMIT)


def _pick_tm(rows, cap):
    t = (min(cap, rows) // 16) * 16
    while t >= 16:
        if rows % t == 0:
            return t
        t -= 16
    return rows


def _dot(a, b):
    return jnp.dot(a, b, preferred_element_type=F32)


def _dot_nt(a, b):
    return lax.dot_general(a, b, (((1,), (1,)), ((), ())), preferred_element_type=F32)


def _dot_tn(a, b):
    return lax.dot_general(a, b, (((0,), (0,)), ((), ())), preferred_element_type=F32)


def _sigmoid(z):
    return 1.0 / (1.0 + jnp.exp(-z))


def _lo_lanes(shape):
    return lax.broadcasted_iota(jnp.int32, shape, len(shape) - 1) % LANES < HEAD_DIM


def _seg_sum(blk):
    lo = _lo_lanes(blk.shape)
    s_lo = jnp.sum(jnp.where(lo, blk, 0.0), axis=1, keepdims=True)
    s_hi = jnp.sum(jnp.where(lo, 0.0, blk), axis=1, keepdims=True)
    return jnp.where(lo, s_lo, s_hi)


def _rms_bwd_tile(x, g, dh):
    r = lax.rsqrt(jnp.mean(x * x, axis=-1, keepdims=True) + EPS)
    xh = x * r
    dyg = dh * g
    dx = r * (dyg - xh * jnp.mean(dyg * xh, axis=-1, keepdims=True))
    return dx, jnp.sum(dh * xh, axis=0, keepdims=True)


def _ew(name, fn, ins, out_defs, cap=512):
    rows = ins[0].shape[0]
    tm = _pick_tm(rows, cap)
    n_in = len(ins)

    def body(*refs):
        vals = fn(*[r[...] for r in refs[:n_in]])
        if not isinstance(vals, tuple):
            vals = (vals,)
        for r, v in zip(refs[n_in:], vals):
            r[...] = v.astype(r.dtype)

    return pl.pallas_call(
        body, name=name, grid=(rows // tm,),
        in_specs=[pl.BlockSpec((tm, a.shape[1]), lambda i: (i, 0)) for a in ins],
        out_specs=[pl.BlockSpec((tm, c), lambda i: (i, 0)) for c, _ in out_defs],
        out_shape=[jax.ShapeDtypeStruct((rows, c), dt) for c, dt in out_defs],
        compiler_params=_params(1))(*ins)


def _rms_fwd(x, g):
    T, D = x.shape
    tm = _pick_tm(T, 512)

    def body(x_ref, g_ref, h_ref):
        xv = x_ref[...]
        r = lax.rsqrt(jnp.mean(xv * xv, axis=-1, keepdims=True) + EPS)
        h_ref[...] = (xv * r * g_ref[...]).astype(BF16)

    return pl.pallas_call(
        body, name="rms_fwd", grid=(T // tm,),
        in_specs=[pl.BlockSpec((tm, D), lambda i: (i, 0)), pl.BlockSpec((1, D), lambda i: (0, 0))],
        out_specs=pl.BlockSpec((tm, D), lambda i: (i, 0)),
        out_shape=jax.ShapeDtypeStruct((T, D), BF16), compiler_params=_params(1))(x, g)


def _ffn_up(h, win, l):
    T, D = h.shape
    wc = win.shape[3]
    tm = _pick_tm(T, 512)

    def body(h_ref, wg_ref, wu_ref, gate_ref, up_ref, act_ref):
        hv = h_ref[...]
        gte = _dot(hv, wg_ref[...])
        u = _dot(hv, wu_ref[...])
        gate_ref[...] = gte.astype(BF16)
        up_ref[...] = u.astype(BF16)
        act_ref[...] = (gte * _sigmoid(gte) * u).astype(BF16)

    out = jax.ShapeDtypeStruct((T, 2 * wc), BF16)
    ospec = pl.BlockSpec((tm, wc), lambda j, i: (i, j))
    return pl.pallas_call(
        body, name="ffn_up", grid=(2, T // tm),
        in_specs=[pl.BlockSpec((tm, D), lambda j, i: (i, 0)),
                  pl.BlockSpec((None, None, D, wc), lambda j, i: (l, j, 0, 0)),
                  pl.BlockSpec((None, None, D, wc), lambda j, i: (l, j + 2, 0, 0))],
        out_specs=[ospec, ospec, ospec], out_shape=[out, out, out], compiler_params=_params(2))(h, win, win)


def _mm_res(name, res, a, w, l, scale):
    T, K = a.shape
    N = w.shape[2]
    tm = _pick_tm(T, 512)

    def body(r_ref, a_ref, w_ref, o_ref):
        o_ref[...] = r_ref[...] + scale * _dot(a_ref[...].astype(BF16), w_ref[...])

    return pl.pallas_call(
        body, name=name, grid=(T // tm,),
        in_specs=[pl.BlockSpec((tm, N), lambda i: (i, 0)), pl.BlockSpec((tm, K), lambda i: (i, 0)),
                  pl.BlockSpec((None, K, N), lambda i: (l, 0, 0))],
        out_specs=pl.BlockSpec((tm, N), lambda i: (i, 0)),
        out_shape=jax.ShapeDtypeStruct((T, N), F32), compiler_params=_params(1))(res, a, w)


def _mm_plain(name, a, w, l):
    T, K = a.shape
    N = w.shape[2]
    tm = _pick_tm(T, 512)

    def body(a_ref, w_ref, o_ref):
        o_ref[...] = _dot(a_ref[...], w_ref[...])

    return pl.pallas_call(
        body, name=name, grid=(T // tm,),
        in_specs=[pl.BlockSpec((tm, K), lambda i: (i, 0)), pl.BlockSpec((None, K, N), lambda i: (l, 0, 0))],
        out_specs=pl.BlockSpec((tm, N), lambda i: (i, 0)),
        out_shape=jax.ShapeDtypeStruct((T, N), F32), compiler_params=_params(1))(a, w)


def _mm_tn(name, a, a_w, b, b_w, n_slots, a_by_slot, b_by_slot, scale, tm_cap=512):
    T = a.shape[0]
    tm = _pick_tm(T, tm_cap)
    nt = T // tm

    def body(a_ref, b_ref, o_ref, acc):
        i = pl.program_id(1)

        @pl.when(i == 0)
        def _():
            acc[...] = jnp.zeros_like(acc)

        acc[...] += _dot_tn(a_ref[...].astype(BF16), b_ref[...].astype(BF16))

        @pl.when(i == nt - 1)
        def _():
            o_ref[...] = (acc[...] * scale).astype(BF16)

    return pl.pallas_call(
        body, name=name, grid=(n_slots, nt),
        in_specs=[pl.BlockSpec((tm, a_w), (lambda s, i: (i, s)) if a_by_slot else (lambda s, i: (i, 0))),
                  pl.BlockSpec((tm, b_w), (lambda s, i: (i, s)) if b_by_slot else (lambda s, i: (i, 0)))],
        out_specs=pl.BlockSpec((None, a_w, b_w), lambda s, i: (s, 0, 0)),
        out_shape=jax.ShapeDtypeStruct((n_slots, a_w, b_w), BF16),
        scratch_shapes=[pltpu.VMEM((a_w, b_w), F32)], compiler_params=_params(2))(a, b)


def _ffn_bwd_a(dx, wout, l, gate, up):
    T, D = dx.shape
    F = gate.shape[1]
    tm = _pick_tm(T, 256)

    def body(dx_ref, w_ref, g_ref, u_ref, o_ref):
        dact = 0.5 * _dot_nt(dx_ref[...].astype(BF16), w_ref[...])
        gte = g_ref[...].astype(F32)
        u = u_ref[...].astype(F32)
        sg = _sigmoid(gte)
        silu = gte * sg
        o_ref[:, :F] = (dact * u * (sg + silu * (1.0 - sg))).astype(BF16)
        o_ref[:, F:] = (dact * silu).astype(BF16)

    return pl.pallas_call(
        body, name="ffn_bwd_a", grid=(T // tm,),
        in_specs=[pl.BlockSpec((tm, D), lambda i: (i, 0)), pl.BlockSpec((None, F, D), lambda i: (l, 0, 0)),
                  pl.BlockSpec((tm, F), lambda i: (i, 0)), pl.BlockSpec((tm, F), lambda i: (i, 0))],
        out_specs=pl.BlockSpec((tm, 2 * F), lambda i: (i, 0)),
        out_shape=jax.ShapeDtypeStruct((T, 2 * F), BF16), compiler_params=_params(1))(dx, wout, gate, up)


def _bwd_into_norm(name, d, d_w, w, l, n_slots, x, g, dx_in):
    T, D = x.shape
    tm = _pick_tm(T, 256)

    def body(*refs):
        d_refs, w_refs = refs[:n_slots], refs[n_slots:2 * n_slots]
        x_ref, g_ref, dxi_ref, dx_ref, dg_ref = refs[2 * n_slots:]
        dh = _dot_nt(d_refs[0][...], w_refs[0][...])
        for s in range(1, n_slots):
            dh = dh + _dot_nt(d_refs[s][...], w_refs[s][...])
        dxn, dg = _rms_bwd_tile(x_ref[...], g_ref[...], dh)
        dx_ref[...] = dxi_ref[...] + dxn

        @pl.when(pl.program_id(0) == 0)
        def _():
            dg_ref[...] = jnp.zeros_like(dg_ref)

        dg_ref[...] += dg

    row = pl.BlockSpec((tm, D), lambda i: (i, 0))
    vec = pl.BlockSpec((1, D), lambda i: (0, 0))
    return pl.pallas_call(
        body, name=name, grid=(T // tm,),
        in_specs=[pl.BlockSpec((tm, d_w), lambda i, s=s: (i, s)) for s in range(n_slots)]
        + [pl.BlockSpec((None, None, D, d_w), lambda i, s=s: (l, s, 0, 0)) for s in range(n_slots)]
        + [row, vec, row],
        out_specs=[row, vec],
        out_shape=[jax.ShapeDtypeStruct((T, D), F32), jax.ShapeDtypeStruct((1, D), F32)],
        compiler_params=_params(1))(*([d] * n_slots + [w] * n_slots + [x, g, dx_in]))


def _ple_fwd(x, hn, p, wg, wp, l):
    T, D = x.shape
    P = p.shape[1]
    tm = _pick_tm(T, 256)

    def body(x_ref, hn_ref, p_ref, wg_ref, wp_ref, xo_ref, gate_ref, pp_ref):
        gate = _sigmoid(_dot(hn_ref[...], wg_ref[...]))
        pp = _dot(p_ref[...].astype(BF16), wp_ref[...])
        gate_ref[...] = gate
        pp_ref[...] = pp
        xo_ref[...] = x_ref[...] + gate * pp

    row = pl.BlockSpec((tm, D), lambda i: (i, 0))
    out = jax.ShapeDtypeStruct((T, D), F32)
    return pl.pallas_call(
        body, name="ple_fwd", grid=(T // tm,),
        in_specs=[row, row, pl.BlockSpec((tm, P), lambda i: (i, 0)),
                  pl.BlockSpec((None, D, D), lambda i: (l, 0, 0)), pl.BlockSpec((None, P, D), lambda i: (l, 0, 0))],
        out_specs=[row, row, row], out_shape=[out, out, out], compiler_params=_params(1))(x, hn, p, wg, wp)


def _ple_bwd(dx, gate, pp, hn, p, x, g, wg, l):
    T, D = x.shape
    P = p.shape[1]
    tm = _pick_tm(T, 256)
    nt = T // tm

    def body(dx_ref, gate_ref, pp_ref, hn_ref, p_ref, x_ref, g_ref, wg_ref,
             dxo_ref, dg_ref, dwg_ref, dwp_ref, acc_g, acc_p):
        i = pl.program_id(0)

        @pl.when(i == 0)
        def _():
            acc_g[...] = jnp.zeros_like(acc_g)
            acc_p[...] = jnp.zeros_like(acc_p)
            dg_ref[...] = jnp.zeros_like(dg_ref)

        dxv = dx_ref[...]
        gate = gate_ref[...]
        dz = (dxv * pp_ref[...] * gate * (1.0 - gate)).astype(BF16)
        dpp = (dxv * gate).astype(BF16)
        acc_g[...] += _dot_tn(hn_ref[...], dz)
        acc_p[...] += _dot_tn(p_ref[...].astype(BF16), dpp)
        dxn, dg = _rms_bwd_tile(x_ref[...], g_ref[...], _dot_nt(dz, wg_ref[...]))
        dxo_ref[...] = dxv + dxn
        dg_ref[...] += dg

        @pl.when(i == nt - 1)
        def _():
            dwg_ref[...] = acc_g[...].astype(BF16)
            dwp_ref[...] = acc_p[...].astype(BF16)

    row = pl.BlockSpec((tm, D), lambda i: (i, 0))
    vec = pl.BlockSpec((1, D), lambda i: (0, 0))
    return pl.pallas_call(
        body, name="ple_bwd", grid=(nt,),
        in_specs=[row, row, row, row, pl.BlockSpec((tm, P), lambda i: (i, 0)), row, vec,
                  pl.BlockSpec((None, D, D), lambda i: (l, 0, 0))],
        out_specs=[row, vec, pl.BlockSpec((D, D), lambda i: (0, 0)), pl.BlockSpec((P, D), lambda i: (0, 0))],
        out_shape=[jax.ShapeDtypeStruct((T, D), F32), jax.ShapeDtypeStruct((1, D), F32),
                   jax.ShapeDtypeStruct((D, D), BF16), jax.ShapeDtypeStruct((P, D), BF16)],
        scratch_shapes=[pltpu.VMEM((D, D), F32), pltpu.VMEM((P, D), F32)],
        compiler_params=_params(1))(dx, gate, pp, hn, p, x, g, wg)


def _loss_fwd_bwd(y, tgt):
    T, D = y.shape
    tm = _pick_tm(T, 512)

    def body(y_ref, t_ref, dy_ref, loss_ref):
        e = y_ref[...] - t_ref[...]
        dy_ref[...] = e / D

        @pl.when(pl.program_id(0) == 0)
        def _():
            loss_ref[...] = jnp.zeros_like(loss_ref)

        loss_ref[...] += 0.5 * jnp.sum(jnp.mean(e * e, axis=-1, keepdims=True), axis=0, keepdims=True)

    row = pl.BlockSpec((tm, D), lambda i: (i, 0))
    return pl.pallas_call(
        body, name="loss", grid=(T // tm,), in_specs=[row, row],
        out_specs=[row, pl.BlockSpec((8, LANES), lambda i: (0, 0))],
        out_shape=[jax.ShapeDtypeStruct((T, D), F32), jax.ShapeDtypeStruct((8, LANES), F32)],
        compiler_params=_params(1))(y, tgt)


def _qkv_layout(D):
    n_a = D // (2 * LANES)
    n_b = D // (2 * LANES)
    n_kv = max(1, (2 * n_b) // GROUP_B) * HEAD_DIM // LANES
    return n_a, n_b, n_kv


def _dup_half(xv, half):
    rolled = pltpu.roll(xv, HEAD_DIM, 1)
    lo = _lo_lanes(xv.shape)
    return jnp.where(lo, xv, rolled) if half == 0 else jnp.where(lo, rolled, xv)


def _qkv_post(raw, gains):
    T, W = raw.shape
    n_a, n_b, n_kv = _qkv_layout(W * 4 // 9)
    tm = _pick_tm(T, 256)
    o_qb = 3 * n_a

    def norm(xv, gv, scale):
        ms = _seg_sum(xv * xv) * (1.0 / HEAD_DIM)
        return xv * lax.rsqrt(ms + EPS) * gv * scale

    def body(raw_ref, g_ref, a_ref, b_ref):
        def blk(cb):
            return raw_ref[:, cb * LANES:(cb + 1) * LANES]

        def gn(cb):
            return g_ref[:, cb * LANES:(cb + 1) * LANES]

        for cb in range(n_a):
            a_ref[:, cb * LANES:(cb + 1) * LANES] = norm(blk(cb), gn(cb), Q_SCALE).astype(BF16)
            cbk = n_a + cb
            a_ref[:, cbk * LANES:(cbk + 1) * LANES] = norm(blk(cbk), gn(cbk), 1.0).astype(BF16)
            cbv = 2 * n_a + cb
            a_ref[:, cbv * LANES:(cbv + 1) * LANES] = blk(cbv).astype(BF16)
        for cb in range(n_b):
            src = o_qb + cb
            b_ref[:, cb * LANES:(cb + 1) * LANES] = norm(blk(src), gn(src), Q_SCALE).astype(BF16)
        for e in range(n_b):
            kvh = (2 * e) // GROUP_B
            ck = o_qb + n_b + kvh // 2
            cv = ck + n_kv
            kn = norm(blk(ck), gn(ck), 1.0)
            b_ref[:, (n_b + e) * LANES:(n_b + e + 1) * LANES] = _dup_half(kn, kvh % 2).astype(BF16)
            b_ref[:, (2 * n_b + e) * LANES:(2 * n_b + e + 1) * LANES] = _dup_half(blk(cv), kvh % 2).astype(BF16)

    wa, wb = 3 * n_a * LANES, 3 * n_b * LANES
    return pl.pallas_call(
        body, name="qkv_post", grid=(T // tm,),
        in_specs=[pl.BlockSpec((tm, W), lambda i: (i, 0)), pl.BlockSpec((1, W), lambda i: (0, 0))],
        out_specs=[pl.BlockSpec((tm, wa), lambda i: (i, 0)), pl.BlockSpec((tm, wb), lambda i: (i, 0))],
        out_shape=[jax.ShapeDtypeStruct((T, wa), BF16), jax.ShapeDtypeStruct((T, wb), BF16)],
        compiler_params=_params(1))(raw, gains)


def _qkv_post_bwd(raw, gains, d_a, d_b):
    T, W = raw.shape
    n_a, n_b, n_kv = _qkv_layout(W * 4 // 9)
    tm = _pick_tm(T, 256)
    n_br = len(d_a)
    o_qb = 3 * n_a
    flat = [t for br in d_a for t in br] + list(d_b)

    def body(*refs):
        raw_ref, g_ref = refs[0], refs[1]
        da = refs[2:2 + 3 * n_br]
        dbq, dbk, dbv = refs[2 + 3 * n_br:5 + 3 * n_br]
        o_ref, dg_ref = refs[5 + 3 * n_br:]

        @pl.when(pl.program_id(0) == 0)
        def _():
            dg_ref[...] = jnp.zeros_like(dg_ref)

        def cols(ref, cb):
            return ref[:, cb * LANES:(cb + 1) * LANES]

        def sum_a(which, cb):
            acc = cols(da[which], cb)
            for br in range(1, n_br):
                acc = acc + cols(da[3 * br + which], cb)
            return acc

        def norm_bwd(cb, dy, scale):
            xv = cols(raw_ref, cb)
            gv = cols(g_ref, cb)
            r = lax.rsqrt(_seg_sum(xv * xv) * (1.0 / HEAD_DIM) + EPS)
            xh = xv * r
            dys = dy * scale
            dyg = dys * gv
            dxv = r * (dyg - xh * (_seg_sum(dyg * xh) * (1.0 / HEAD_DIM)))
            o_ref[:, cb * LANES:(cb + 1) * LANES] = dxv.astype(BF16)
            dg_ref[:, cb * LANES:(cb + 1) * LANES] += jnp.sum(dys * xh, axis=0, keepdims=True)

        def fold(ref, kv_blk):
            halves = []
            for half in range(2):
                kvh = 2 * kv_blk + half
                blocks = [e for e in range(n_b) if (2 * e) // GROUP_B == kvh]
                s = cols(ref, blocks[0])
                for e in blocks[1:]:
                    s = s + cols(ref, e)
                halves.append(s + pltpu.roll(s, HEAD_DIM, 1))
            return jnp.where(_lo_lanes(halves[0].shape), halves[0], halves[1])

        for cb in range(n_a):
            norm_bwd(cb, sum_a(0, cb), Q_SCALE)
            norm_bwd(n_a + cb, sum_a(1, cb), 1.0)
            cbv = 2 * n_a + cb
            o_ref[:, cbv * LANES:(cbv + 1) * LANES] = sum_a(2, cb).astype(BF16)
        for cb in range(n_b):
            norm_bwd(o_qb + cb, cols(dbq, cb), Q_SCALE)
        for kb in range(n_kv):
            ck = o_qb + n_b + kb
            cv = ck + n_kv
            norm_bwd(ck, fold(dbk, kb), 1.0)
            o_ref[:, cv * LANES:(cv + 1) * LANES] = fold(dbv, kb).astype(BF16)

    hw_a, hw_b = n_a * LANES, n_b * LANES
    return pl.pallas_call(
        body, name="qkv_post_bwd", grid=(T // tm,),
        in_specs=[pl.BlockSpec((tm, W), lambda i: (i, 0)), pl.BlockSpec((1, W), lambda i: (0, 0))]
        + [pl.BlockSpec((tm, hw_a), lambda i: (i, 0))] * (3 * n_br) + [pl.BlockSpec((tm, hw_b), lambda i: (i, 0))] * 3,
        out_specs=[pl.BlockSpec((tm, W), lambda i: (i, 0)), pl.BlockSpec((1, W), lambda i: (0, 0))],
        out_shape=[jax.ShapeDtypeStruct((T, W), BF16), jax.ShapeDtypeStruct((1, W), F32)],
        compiler_params=_params(1))(raw, gains, *flat)


def _t5_bucket_np(rel):
    half = N_BUCKETS // 2
    max_exact = half // 2
    ret = np.where(rel > 0, half, 0)
    n = np.abs(rel)
    nf = np.maximum(n, 1).astype(np.float32)
    large = max_exact + (np.log(nf / np.float32(max_exact)) / np.float32(math.log(MAX_DISTANCE / max_exact))
                         * np.float32(half - max_exact)).astype(np.int32)
    large = np.minimum(large, half - 1)
    return ret + np.where(n < max_exact, n, large)


def _bucket_maps():
    q = np.arange(QBLOCK)[:, None]
    kk = np.arange(3 * QBLOCK)[None, :]
    rel = kk - QBLOCK - q
    maps = []
    for radius, dil in [(w // (2 * d), d) for w, d in DILATED_CONFIGS] + [(SWA_RADIUS, 1)]:
        assert radius <= QBLOCK
        maps.append(np.where(np.abs(rel) <= radius, _t5_bucket_np(rel * dil), -1))
    return np.stack(maps).astype(np.int32)


def _bias_build(rel_bias, bmaps):
    n_sets = bmaps.shape[0]
    n_heads = rel_bias.shape[1] // 2

    def body(rb_ref, bm_ref, o_ref):
        s, h = pl.program_id(0), pl.program_id(1)
        col = jnp.where(s == n_sets - 1, n_heads + h, h)
        bm = bm_ref[...]

        def step(n, acc):
            return jnp.where(bm == n, rb_ref[n, col], acc)

        o_ref[...] = lax.fori_loop(0, N_BUCKETS, step, jnp.where(bm < 0, NEG, 0.0).astype(F32))

    return pl.pallas_call(
        body, name="bias_build", grid=(n_sets, n_heads),
        in_specs=[pl.BlockSpec(memory_space=pltpu.SMEM),
                  pl.BlockSpec((None, QBLOCK, 3 * QBLOCK), lambda s, h: (s, 0, 0))],
        out_specs=pl.BlockSpec((None, None, QBLOCK, 3 * QBLOCK), lambda s, h: (s, h, 0, 0)),
        out_shape=jax.ShapeDtypeStruct((n_sets, n_heads, QBLOCK, 3 * QBLOCK), F32),
        compiler_params=_params(2))(rel_bias, bmaps)


def _bias_grad(dtiles, bmaps):
    n_sets = bmaps.shape[0]
    n_heads = dtiles[0].shape[1]
    n_l = len(dtiles)

    def body(*refs):
        bm_ref, o_ref = refs[0], refs[1 + n_l]
        s, h = pl.program_id(0), pl.program_id(1)
        col = jnp.where(s == n_sets - 1, n_heads + h, h)

        @pl.when((s == 0) & (h == 0))
        def _():
            o_ref[...] = jnp.zeros_like(o_ref)

        d = refs[1][...]
        for r in refs[2:1 + n_l]:
            d = d + r[...]
        bm = bm_ref[...]
        rows = lax.broadcasted_iota(jnp.int32, o_ref.shape, 0)
        lanes = lax.broadcasted_iota(jnp.int32, o_ref.shape, 1)

        def step(n, acc):
            val = jnp.sum(jnp.where(bm == n, d, 0.0))
            return acc + jnp.where((rows == n) & (lanes == col), val, 0.0)

        o_ref[...] += lax.fori_loop(0, N_BUCKETS, step, jnp.zeros(o_ref.shape, F32))

    tile = pl.BlockSpec((None, None, QBLOCK, 3 * QBLOCK), lambda s, h: (s, h, 0, 0))
    return pl.pallas_call(
        body, name="bias_grad", grid=(n_sets, n_heads),
        in_specs=[pl.BlockSpec((None, QBLOCK, 3 * QBLOCK), lambda s, h: (s, 0, 0))] + [tile] * n_l,
        out_specs=pl.BlockSpec((N_BUCKETS, LANES), lambda s, h: (0, 0)),
        out_shape=jax.ShapeDtypeStruct((N_BUCKETS, LANES), F32), compiler_params=_params(2))(bmaps, *dtiles)


def _window_specs(n_blocks, col_base):
    def spec(off):
        return pl.BlockSpec((QBLOCK, LANES),
                            lambda g, b: (jnp.clip(b + off, 0, n_blocks - 1), col_base + g))
    return [spec(-1), spec(0), spec(1)]


def _window_valid(b, nb_sub):
    col = lax.broadcasted_iota(jnp.int32, (QBLOCK, 3 * QBLOCK), 1)
    first = (b % nb_sub) == 0
    last = (b % nb_sub) == nb_sub - 1
    return ((col >= QBLOCK) | jnp.logical_not(first)) & ((col < 2 * QBLOCK) | jnp.logical_not(last))


def _attn_fwd(qkv, bias, sink, sub_len):
    T = qkv.shape[0]
    hw = qkv.shape[1] // 3
    ng = hw // LANES
    n_blocks = T // QBLOCK
    nb_sub = sub_len // QBLOCK

    def body(sink_ref, q_ref, kp, kc, kn, vp, vc, vn, bias_ref, o_ref, lse_ref):
        g, b = pl.program_id(0), pl.program_id(1)
        valid = _window_valid(b, nb_sub)
        q = q_ref[...]
        k = jnp.concatenate([kp[...], kc[...], kn[...]], axis=0)
        v = jnp.concatenate([vp[...], vc[...], vn[...]], axis=0)
        lo = _lo_lanes(q.shape)
        outs, lses = [], []
        for hh in range(2):
            hm = lo if hh == 0 else jnp.logical_not(lo)
            s = _dot_nt(jnp.where(hm, q, jnp.zeros_like(q)), k) + bias_ref[hh]
            s = jnp.where(valid, s, NEG)
            snk = sink_ref[2 * g + hh]
            m = jnp.maximum(jnp.max(s, axis=1, keepdims=True), snk)
            p = jnp.exp(s - m)
            den = jnp.sum(p, axis=1, keepdims=True) + jnp.exp(snk - m)
            outs.append(_dot(p.astype(BF16), v) / den)
            lses.append(m + jnp.log(den))
        o_ref[...] = jnp.where(lo, outs[0], outs[1])
        lse_ref[...] = jnp.where(lo, lses[0], lses[1])

    blk = pl.BlockSpec((QBLOCK, LANES), lambda g, b: (b, g))
    out = jax.ShapeDtypeStruct((T, hw), F32)
    return pl.pallas_call(
        body, name="attn_fwd", grid=(ng, n_blocks),
        in_specs=[pl.BlockSpec(memory_space=pltpu.SMEM), blk] + _window_specs(n_blocks, ng)
        + _window_specs(n_blocks, 2 * ng) + [pl.BlockSpec((2, QBLOCK, 3 * QBLOCK), lambda g, b: (g, 0, 0))],
        out_specs=[blk, blk], out_shape=[out, out], compiler_params=_params(2))(
            sink, qkv, qkv, qkv, qkv, qkv, qkv, qkv, bias)


def _attn_bwd(qkv, bias, sink, sub_len, do, lse, dd, col_base):
    T = qkv.shape[0]
    hw = qkv.shape[1] // 3
    ng = hw // LANES
    n_blocks = T // QBLOCK
    nb_sub = sub_len // QBLOCK

    def body(sink_ref, q_ref, kp, kc, kn, vp, vc, vn, bias_ref, do_ref, lse_ref, dd_ref,
             dq_ref, dk_ref, dv_ref, dt_ref, ds_ref):
        g, b = pl.program_id(0), pl.program_id(1)

        @pl.when(b == 0)
        def _():
            dk_ref[...] = jnp.zeros_like(dk_ref)
            dv_ref[...] = jnp.zeros_like(dv_ref)
            dt_ref[...] = jnp.zeros_like(dt_ref)
            ds_ref[...] = jnp.zeros_like(ds_ref)

        valid = _window_valid(b, nb_sub)
        q = q_ref[...]
        k = jnp.concatenate([kp[...], kc[...], kn[...]], axis=0)
        v = jnp.concatenate([vp[...], vc[...], vn[...]], axis=0)
        dob = do_ref[...].astype(BF16)
        lo = _lo_lanes(q.shape)
        dqs = []
        dkw = jnp.zeros((3 * QBLOCK, LANES), F32)
        dvw = jnp.zeros((3 * QBLOCK, LANES), F32)
        for hh in range(2):
            hm = lo if hh == 0 else jnp.logical_not(lo)
            qm = jnp.where(hm, q, jnp.zeros_like(q))
            dom = jnp.where(hm, dob, jnp.zeros_like(dob))
            s = _dot_nt(qm, k) + bias_ref[hh]
            s = jnp.where(valid, s, NEG)
            lse_h = lse_ref[:, hh * HEAD_DIM:hh * HEAD_DIM + 1]
            dd_h = dd_ref[:, hh * HEAD_DIM:hh * HEAD_DIM + 1]
            p = jnp.exp(s - lse_h)
            ds = p * (_dot_nt(dom, v) - dd_h)
            dsb = ds.astype(BF16)
            dqs.append(_dot(dsb, k))
            dkw = dkw + _dot_tn(dsb, qm)
            dvw = dvw + _dot_tn(p.astype(BF16), dom)
            dt_ref[hh] += ds
        dq_ref[...] = jnp.where(lo, dqs[0], dqs[1])
        snk = jnp.where(lo, sink_ref[2 * g], sink_ref[2 * g + 1])
        ds_ref[0:1, :] += jnp.sum(-jnp.exp(snk - lse_ref[...]) * dd_ref[...], axis=0, keepdims=True)
        for w, blk_i in enumerate((jnp.maximum(b - 1, 0), b, jnp.minimum(b + 1, n_blocks - 1))):
            r0 = pl.multiple_of(blk_i * QBLOCK, QBLOCK)
            dk_ref[pl.ds(r0, QBLOCK), :] += dkw[w * QBLOCK:(w + 1) * QBLOCK]
            dv_ref[pl.ds(r0, QBLOCK), :] += dvw[w * QBLOCK:(w + 1) * QBLOCK]

    blk = pl.BlockSpec((QBLOCK, LANES), lambda g, b: (b, g))
    stat = pl.BlockSpec((QBLOCK, LANES), lambda g, b: (b, col_base + g))
    colacc = pl.BlockSpec((T, LANES), lambda g, b: (0, g))
    tile = pl.BlockSpec((2, QBLOCK, 3 * QBLOCK), lambda g, b: (g, 0, 0))
    full = jax.ShapeDtypeStruct((T, hw), F32)
    return pl.pallas_call(
        body, name="attn_bwd", grid=(ng, n_blocks),
        in_specs=[pl.BlockSpec(memory_space=pltpu.SMEM), blk] + _window_specs(n_blocks, ng)
        + _window_specs(n_blocks, 2 * ng) + [tile, stat, stat, stat],
        out_specs=[blk, colacc, colacc, tile, pl.BlockSpec((None, 8, LANES), lambda g, b: (g, 0, 0))],
        out_shape=[full, full, full, jax.ShapeDtypeStruct((2 * ng, QBLOCK, 3 * QBLOCK), F32),
                   jax.ShapeDtypeStruct((ng, 8, LANES), F32)],
        compiler_params=_params(2))(sink, qkv, qkv, qkv, qkv, qkv, qkv, qkv, bias, do, lse, dd)


def _merge(o_list, lse_list, o_b, lse_b):
    n = len(o_list)
    hwa, hwb = o_list[0].shape[1], o_b.shape[1]

    def fn(*vals):
        os_, ls_ = vals[:n], vals[n:2 * n]
        ob, lb = vals[2 * n], vals[2 * n + 1]
        m = ls_[0]
        for t in ls_[1:]:
            m = jnp.maximum(m, t)
        ws = [jnp.exp(t - m) for t in ls_]
        z = ws[0]
        for t in ws[1:]:
            z = z + t
        acc = ws[0] * os_[0]
        for wv, ov in zip(ws[1:], os_[1:]):
            acc = acc + wv * ov
        return (jnp.concatenate([acc / z, ob], axis=1), jnp.concatenate([m + jnp.log(z), lb], axis=1))

    return _ew("merge", fn, list(o_list) + list(lse_list) + [o_b, lse_b], [(hwa + hwb, F32), (hwa + hwb, F32)], cap=256)


def _mix_bwd_in(dx, wo, l, o_full):
    T, D = dx.shape
    hw = o_full.shape[1]
    tm = _pick_tm(T, 256)

    def body(dx_ref, w_ref, o_ref, do_ref, dd_ref):
        dov = _dot_nt(dx_ref[...].astype(BF16), w_ref[...])
        do_ref[...] = dov
        prod = dov * o_ref[...]
        for cb in range(hw // LANES):
            dd_ref[:, cb * LANES:(cb + 1) * LANES] = _seg_sum(prod[:, cb * LANES:(cb + 1) * LANES])

    row = pl.BlockSpec((tm, hw), lambda i: (i, 0))
    out = jax.ShapeDtypeStruct((T, hw), F32)
    return pl.pallas_call(
        body, name="mix_bwd_in", grid=(T // tm,),
        in_specs=[pl.BlockSpec((tm, D), lambda i: (i, 0)), pl.BlockSpec((None, hw, D), lambda i: (l, 0, 0)), row],
        out_specs=[row, row], out_shape=[out, out], compiler_params=_params(1))(dx, wo, o_full)


def _perm(a, d):
    if d == 1:
        return a
    T, C = a.shape
    return a.reshape(T // d, d, C).transpose(1, 0, 2).reshape(T, C)


def _unperm(a, d):
    if d == 1:
        return a
    T, C = a.shape
    return a.reshape(d, T // d, C).transpose(1, 0, 2).reshape(T, C)


def _mesh_pos():
    return lax.axis_index("x"), lax.axis_index("y"), lax.axis_index("c")


def _other_chips(x, y):
    return [(1 - x, y), (x, 1 - y), (1 - x, 1 - y)]


HBM_SPEC = pl.BlockSpec(memory_space=pltpu.HBM)


def _half_rows(rows, cc):
    hr = rows // 2
    return pl.ds(pl.multiple_of(cc * hr, 16), hr)


def _allgather_weights(shards):
    n = len(shards)

    def body(*refs):
        ins, outs = refs[:n], refs[n:2 * n]
        ici_s, ici_r, d2d_s, d2d_r, loc = refs[2 * n:]
        x, y, c = _mesh_pos()
        my = 2 * x + y
        chips = _other_chips(x, y)

        def landing(w, slot, cc):
            return outs[w].at[:, slot, _half_rows(shards[w].shape[1], cc), :]

        def ici(k, w):
            px, py = chips[k]
            return pltpu.make_async_remote_copy(
                src_ref=ins[w].at[:, _half_rows(shards[w].shape[1], c), :], dst_ref=landing(w, my, c),
                send_sem=ici_s.at[k * n + w], recv_sem=ici_r.at[k * n + w], device_id=(px, py, c), device_id_type=MESH)

        def ici_landed(k, w):
            px, py = chips[k]
            return pltpu.make_async_remote_copy(
                src_ref=landing(w, 2 * px + py, c), dst_ref=landing(w, 2 * px + py, c),
                send_sem=ici_s.at[k * n + w], recv_sem=ici_r.at[k * n + w], device_id=(px, py, c), device_id_type=MESH)

        def passed(k, w, cc):
            px, py = chips[k]
            return pltpu.make_async_remote_copy(
                src_ref=landing(w, 2 * px + py, cc), dst_ref=landing(w, 2 * px + py, cc),
                send_sem=d2d_s.at[k * n + w], recv_sem=d2d_r.at[k * n + w], device_id=(x, y, 1 - c), device_id_type=MESH)

        local = [pltpu.make_async_copy(ins[w], outs[w].at[:, my], loc.at[w]) for w in range(n)]
        for cp in local:
            cp.start()
        for w in range(n):
            for k in range(3):
                ici(k, w).start()
        for w in range(n):
            for k in range(3):
                ici_landed(k, w).wait_recv()
                passed(k, w, c).start()
        for w in range(n):
            for k in range(3):
                passed(k, w, 1 - c).wait_recv()
        for w in range(n):
            for k in range(3):
                ici(k, w).wait_send()
                passed(k, w, c).wait_send()
        for cp in local:
            cp.wait()

    return pl.pallas_call(
        body, name="allgather_weights",
        in_specs=[HBM_SPEC] * n, out_specs=[HBM_SPEC] * n,
        out_shape=[jax.ShapeDtypeStruct((s.shape[0], N_CHIPS) + s.shape[1:], s.dtype) for s in shards],
        scratch_shapes=[pltpu.SemaphoreType.DMA((3 * n,)), pltpu.SemaphoreType.DMA((3 * n,)),
                        pltpu.SemaphoreType.DMA((3 * n,)), pltpu.SemaphoreType.DMA((3 * n,)),
                        pltpu.SemaphoreType.DMA((n,))],
    )(*shards)


def _rs_pair_exchange(grads):
    n = len(grads)

    def body(*refs):
        ins, keep, recv = refs[:n], refs[n:2 * n], refs[2 * n:3 * n]
        send_s, recv_s, loc = refs[3 * n:]
        x, y, c = _mesh_pos()

        def rows(t, cc):
            return ins[t].at[:, _half_rows(grads[t].shape[1], cc), :]

        local = [pltpu.make_async_copy(rows(t, c), keep[t], loc.at[t]) for t in range(n)]
        remote = [pltpu.make_async_remote_copy(
            src_ref=rows(t, 1 - c), dst_ref=recv[t], send_sem=send_s.at[t], recv_sem=recv_s.at[t],
            device_id=(x, y, 1 - c), device_id_type=MESH) for t in range(n)]
        for cp in local + remote:
            cp.start()
        for cp in remote:
            cp.wait_recv()
        for cp in remote:
            cp.wait_send()
        for cp in local:
            cp.wait()

    half = [jax.ShapeDtypeStruct((g.shape[0], g.shape[1] // 2, g.shape[2]), g.dtype) for g in grads]
    res = pl.pallas_call(
        body, name="rs_pair_exchange", in_specs=[HBM_SPEC] * n, out_specs=[HBM_SPEC] * (2 * n),
        out_shape=half + half,
        scratch_shapes=[pltpu.SemaphoreType.DMA((n,)), pltpu.SemaphoreType.DMA((n,)), pltpu.SemaphoreType.DMA((n,))],
    )(*grads)
    return res[:n], res[n:]


def _rs_chip_exchange(parts):
    n = len(parts)

    def body(*refs):
        ins, own, recv = refs[:n], refs[n:2 * n], refs[2 * n:3 * n]
        send_s, recv_s, loc = refs[3 * n:]
        x, y, c = _mesh_pos()
        my = 2 * x + y
        chips = _other_chips(x, y)
        local = [pltpu.make_async_copy(ins[t].at[my], own[t], loc.at[t]) for t in range(n)]
        remote = []
        for t in range(n):
            for k, (px, py) in enumerate(chips):
                remote.append(pltpu.make_async_remote_copy(
                    src_ref=ins[t].at[2 * px + py], dst_ref=recv[t].at[k],
                    send_sem=send_s.at[3 * t + k], recv_sem=recv_s.at[3 * t + k],
                    device_id=(px, py, c), device_id_type=MESH))
        for cp in local + remote:
            cp.start()
        for cp in remote:
            cp.wait_recv()
        for cp in remote:
            cp.wait_send()
        for cp in local:
            cp.wait()

    res = pl.pallas_call(
        body, name="rs_chip_exchange", in_specs=[HBM_SPEC] * n, out_specs=[HBM_SPEC] * (2 * n),
        out_shape=[jax.ShapeDtypeStruct(p.shape[1:], p.dtype) for p in parts]
        + [jax.ShapeDtypeStruct((3,) + p.shape[1:], p.dtype) for p in parts],
        scratch_shapes=[pltpu.SemaphoreType.DMA((3 * n,)), pltpu.SemaphoreType.DMA((3 * n,)),
                        pltpu.SemaphoreType.DMA((n,))],
    )(*parts)
    return res[:n], res[n:]


def _rs_pair_share(halves, n_layers):
    n = len(halves)
    n_w = n // n_layers

    def body(*refs):
        ins, outs = refs[:n], refs[n:n + n_w]
        send_s, recv_s, loc = refs[n + n_w:]
        x, y, c = _mesh_pos()

        def place(t):
            w, l = divmod(t, n_layers)
            return outs[w].at[l, _half_rows(2 * halves[t].shape[0], c), :]

        local = [pltpu.make_async_copy(ins[t], place(t), loc.at[t]) for t in range(n)]
        remote = [pltpu.make_async_remote_copy(
            src_ref=ins[t], dst_ref=place(t), send_sem=send_s.at[t], recv_sem=recv_s.at[t],
            device_id=(x, y, 1 - c), device_id_type=MESH) for t in range(n)]
        for cp in local + remote:
            cp.start()
        for t in range(n):
            w, l = divmod(t, n_layers)
            there = outs[w].at[l, _half_rows(2 * halves[t].shape[0], 1 - c), :]
            pltpu.make_async_remote_copy(
                src_ref=there, dst_ref=there, send_sem=send_s.at[t], recv_sem=recv_s.at[t],
                device_id=(x, y, 1 - c), device_id_type=MESH).wait_recv()
        for cp in remote:
            cp.wait_send()
        for cp in local:
            cp.wait()

    return pl.pallas_call(
        body, name="rs_pair_share", in_specs=[HBM_SPEC] * n, out_specs=[HBM_SPEC] * n_w,
        out_shape=[jax.ShapeDtypeStruct((n_layers, 2 * halves[w * n_layers].shape[0], halves[w * n_layers].shape[1]), F32)
                   for w in range(n_w)],
        scratch_shapes=[pltpu.SemaphoreType.DMA((n,)), pltpu.SemaphoreType.DMA((n,)), pltpu.SemaphoreType.DMA((n,))],
    )(*halves)


def _allreduce_small(v):
    rows = v.shape[0]

    def body(v_ref, o_ref, buf, send_s, recv_s):
        x, y, c = _mesh_pos()
        me = 4 * x + 2 * y + c
        buf[me] = v_ref[...]
        copies = []
        for r in range(1, N_DEV):
            px = 1 - x if r & 4 else x
            py = 1 - y if r & 2 else y
            pc = 1 - c if r & 1 else c
            send = pltpu.make_async_remote_copy(
                src_ref=v_ref, dst_ref=buf.at[me], send_sem=send_s.at[r - 1], recv_sem=recv_s.at[r - 1],
                device_id=(px, py, pc), device_id_type=MESH)
            peer_slot = buf.at[4 * px + 2 * py + pc]
            landed = pltpu.make_async_remote_copy(
                src_ref=peer_slot, dst_ref=peer_slot, send_sem=send_s.at[r - 1], recv_sem=recv_s.at[r - 1],
                device_id=(px, py, pc), device_id_type=MESH)
            copies.append((send, landed))
        for send, _ in copies:
            send.start()
        for _, landed in copies:
            landed.wait_recv()
        for send, _ in copies:
            send.wait_send()
        acc = buf[0]
        for j in range(1, N_DEV):
            acc = acc + buf[j]
        o_ref[...] = acc

    vm = pl.BlockSpec(memory_space=pltpu.VMEM)
    return pl.pallas_call(
        body, name="allreduce_small", in_specs=[vm], out_specs=vm,
        out_shape=jax.ShapeDtypeStruct((rows, LANES), F32),
        scratch_shapes=[pltpu.VMEM((N_DEV, rows, LANES), F32), pltpu.SemaphoreType.DMA((N_DEV - 1,)),
                        pltpu.SemaphoreType.DMA((N_DEV - 1,))],
    )(v)


def _adamw_fn(w, g, m, v):
    m2 = ADAM_B1 * m + (1.0 - ADAM_B1) * g
    v2 = ADAM_B2 * v + (1.0 - ADAM_B2) * (g * g)
    m_hat = m2 / (1.0 - ADAM_B1 ** ADAM_STEP)
    v_hat = v2 / (1.0 - ADAM_B2 ** ADAM_STEP)
    delta = -ADAM_LR * (m_hat / (jnp.sqrt(v_hat) + ADAM_EPS) + ADAM_WD * w)
    return g, delta, m2, v2


def _adamw(w, g, m, v):
    c = w.shape[1]
    return _ew("adamw", _adamw_fn, [w, g, m, v], [(c, F32)] * 4, cap=128)


def _pack_small(parts):
    out = []
    for a in parts:
        flat = a.reshape(-1)
        n = -(-flat.shape[0] // (8 * LANES)) * 8 * LANES
        out.append(jnp.pad(flat, (0, n - flat.shape[0])).reshape(-1, LANES))
    return jnp.concatenate(out, axis=0)


def _unpack_small(packed, like):
    out, r = [], 0
    for a in like:
        size = int(np.prod(a.shape))
        rows = -(-size // (8 * LANES)) * 8
        out.append(packed[r:r + rows].reshape(-1)[:size].reshape(a.shape))
        r += rows
    return out


def _ffn_forward(x, g, win, wout, l):
    h = _rms_fwd(x, g)
    gate, up, act = _ffn_up(h, win, l)
    return _mm_res("ffn_down", x, act, wout, l, 0.5), (x, h, gate, up, act)


def _ffn_backward(dx, saved, g, win, wout, l):
    x, h, gate, up, act = saved
    D = x.shape[1]
    wc = win.shape[3]
    dgu = _ffn_bwd_a(dx, wout, l, gate, up)
    dwout = _mm_tn("dw_ffn_out", act, wc, dx, D, 2, True, False, 0.5)
    dwin = _mm_tn("dw_ffn_in", h, D, dgu, wc, 4, False, True, 1.0)
    dx_in, dg = _bwd_into_norm("ffn_bwd_b", dgu, wc, win, l, 4, x, g, dx)
    return dx_in, dg, dwin, dwout.reshape(N_CHIPS, -1, D)


def kernel(x, p, rel_bias, norm_ffn1, ffn1_w_in, ffn1_w_out, norm_mix, w_qkv, q_norm_a, k_norm_a, q_norm_b, k_norm_b, sink_b, w_o, norm_ffn2, ffn2_w_in, ffn2_w_out, norm_ple, w_ple_gate, w_ple_proj, loss_target, m_rel_bias, m_norm_ffn1, m_ffn1_w_in, m_ffn1_w_out, m_norm_mix, m_w_qkv, m_q_norm_a, m_k_norm_a, m_q_norm_b, m_k_norm_b, m_sink_b, m_w_o, m_norm_ffn2, m_ffn2_w_in, m_ffn2_w_out, m_norm_ple, m_w_ple_gate, m_w_ple_proj, v_rel_bias, v_norm_ffn1, v_ffn1_w_in, v_ffn1_w_out, v_norm_mix, v_w_qkv, v_q_norm_a, v_k_norm_a, v_q_norm_b, v_k_norm_b, v_sink_b, v_w_o, v_norm_ffn2, v_ffn2_w_in, v_ffn2_w_out, v_norm_ple, v_w_ple_gate, v_w_ple_proj):
    given = dict(locals())
    T, D = x.shape[1], x.shape[2]
    NL = norm_ffn1.shape[0]
    x0 = x.reshape(T, D)
    tgt = loss_target.reshape(T, D)
    n_a, n_b, n_kv = _qkv_layout(D)
    hwa = n_a * LANES

    shards = []
    for name in BIG:
        wsh = given[name]
        w2 = wsh.reshape(-1, wsh.shape[-1])
        shards.append(_ew("cast_bf16", lambda t: t, [w2], [(w2.shape[1], BF16)])[0].reshape(wsh.shape))
    gathered = dict(zip(BIG, _allgather_weights(shards)))

    def by_rows(a):
        return a.reshape(NL, -1, a.shape[-1])

    def by_cols(a):
        return a.transpose(0, 2, 1, 3).reshape(NL, a.shape[2], -1)

    win1, win2 = gathered["ffn1_w_in"], gathered["ffn2_w_in"]
    wout1, wout2 = by_rows(gathered["ffn1_w_out"]), by_rows(gathered["ffn2_w_out"])
    wqkv = by_cols(gathered["w_qkv"])
    wqkv4 = wqkv.reshape(NL, 1, D, -1)
    wo = by_rows(gathered["w_o"])
    wpg = by_rows(gathered["w_ple_gate"])
    wpp = by_cols(gathered["w_ple_proj"])
    QW = wqkv.shape[2]

    bmaps = jnp.asarray(_bucket_maps())
    bias = _bias_build(rel_bias, bmaps)
    n_heads = bias.shape[1]
    no_sink = jnp.full((n_heads,), NEG, F32)
    dils = [d for _, d in DILATED_CONFIGS]

    def gains_row(l):
        ones = jnp.ones((hwa,), F32)
        return jnp.concatenate([
            jnp.tile(q_norm_a[l], 2 * n_a), jnp.tile(k_norm_a[l], 2 * n_a), ones,
            jnp.tile(q_norm_b[l], 2 * n_b), jnp.tile(k_norm_b[l], 2 * n_kv), jnp.ones((n_kv * LANES,), F32)]).reshape(1, QW)

    saved = []
    xc = x0
    for l in range(NL):
        s = {}
        xc, s["ffn1"] = _ffn_forward(xc, norm_ffn1[l:l + 1], win1, wout1, l)
        s["x1"] = xc
        h2 = _rms_fwd(xc, norm_mix[l:l + 1])
        raw = _mm_plain("qkv_proj", h2, wqkv, l)
        qkv_a, qkv_b = _qkv_post(raw, gains_row(l))
        s["h2"], s["raw"], s["qkv_b"] = h2, raw, qkv_b
        s["qkv_a"] = [_perm(qkv_a, d) for d in dils]
        o_br, lse_br = [], []
        for ci, d in enumerate(dils):
            o_c, lse_c = _attn_fwd(s["qkv_a"][ci], bias[ci], no_sink, T // d)
            o_br.append(_unperm(o_c, d))
            lse_br.append(_unperm(lse_c, d))
        o_b, lse_b = _attn_fwd(qkv_b, bias[len(dils)], sink_b[l], T)
        s["o"], s["lse"] = _merge(o_br, lse_br, o_b, lse_b)
        xc = _mm_res("attn_out", xc, s["o"], wo, l, 1.0)
        xc, s["ffn2"] = _ffn_forward(xc, norm_ffn2[l:l + 1], win2, wout2, l)
        s["x3"] = xc
        s["hn"] = _rms_fwd(xc, norm_ple[l:l + 1])
        s["p"] = p[l].reshape(T, -1)
        xc, s["gate"], s["pp"] = _ple_fwd(xc, s["hn"], s["p"], wpg, wpp, l)
        saved.append(s)

    dx, loss_blk = _loss_fwd_bwd(xc, tgt)
    loss = lax.psum(loss_blk[0, 0], ("x", "y", "c"))

    gw = {name: [None] * NL for name in BIG}
    gs = {name: [None] * NL for name in SMALL if name != "rel_bias"}
    dtiles = []
    for l in reversed(range(NL)):
        s = saved[l]
        dx, gs["norm_ple"][l], dwg, dwp = _ple_bwd(dx, s["gate"], s["pp"], s["hn"], s["p"], s["x3"],
                                                   norm_ple[l:l + 1], wpg, l)
        gw["w_ple_gate"][l] = dwg.reshape(N_CHIPS, -1, D)
        gw["w_ple_proj"][l] = dwp.reshape(dwp.shape[0], N_CHIPS, -1).transpose(1, 0, 2)
        dx, gs["norm_ffn2"][l], gw["ffn2_w_in"][l], gw["ffn2_w_out"][l] = _ffn_backward(
            dx, s["ffn2"], norm_ffn2[l:l + 1], win2, wout2, l)
        do, dd = _mix_bwd_in(dx, wo, l, s["o"])
        gw["w_o"][l] = _mm_tn("dw_o", s["o"], D, dx, D, 1, False, False, 1.0).reshape(N_CHIPS, -1, D)
        d_a, dt_l = [], []
        for ci, d in enumerate(dils):
            if d == 1:
                do_c, lse_c, dd_c = do, s["lse"], dd
            else:
                do_c, lse_c, dd_c = (_perm(t[:, :hwa], d) for t in (do, s["lse"], dd))
            dq, dk, dv, dt, _ = _attn_bwd(s["qkv_a"][ci], bias[ci], no_sink, T // d, do_c, lse_c, dd_c, 0)
            d_a.append(tuple(_unperm(t, d) for t in (dq, dk, dv)))
            dt_l.append(dt)
        dq, dk, dv, dt, dsink = _attn_bwd(s["qkv_b"], bias[len(dils)], sink_b[l], T, do, s["lse"], dd, n_a)
        dt_l.append(dt)
        dtiles.append(jnp.stack(dt_l))
        gs["sink_b"][l] = dsink[:, 0, ::HEAD_DIM].reshape(-1)
        draw, dgains = _qkv_post_bwd(s["raw"], gains_row(l), d_a, (dq, dk, dv))
        dgv = dgains.reshape(-1, HEAD_DIM)
        gs["q_norm_a"][l] = dgv[:2 * n_a].sum(0)
        gs["k_norm_a"][l] = dgv[2 * n_a:4 * n_a].sum(0)
        gs["q_norm_b"][l] = dgv[6 * n_a:6 * n_a + 2 * n_b].sum(0)
        gs["k_norm_b"][l] = dgv[6 * n_a + 2 * n_b:6 * n_a + 2 * n_b + 2 * n_kv].sum(0)
        dwqkv = _mm_tn("dw_qkv", s["h2"], D, draw, QW, 1, False, False, 1.0)
        gw["w_qkv"][l] = dwqkv.reshape(D, N_CHIPS, -1).transpose(1, 0, 2)
        dx, gs["norm_mix"][l] = _bwd_into_norm("qkv_bwd_b", draw, QW, wqkv4, l, 1, s["x1"], norm_mix[l:l + 1], dx)
        dx, gs["norm_ffn1"][l], gw["ffn1_w_in"][l], gw["ffn1_w_out"][l] = _ffn_backward(
            dx, s["ffn1"], norm_ffn1[l:l + 1], win1, wout1, l)
    grad_x = dx.reshape(x.shape)
    d_rel_bias = _bias_grad(dtiles[::-1], bmaps)[:, :rel_bias.shape[1]]

    items = [gw[name][l] for name in BIG for l in range(NL)]
    keep, got = _rs_pair_exchange(items)
    part = []
    for kp, gt in zip(keep, got):
        a2, b2 = kp.reshape(-1, kp.shape[-1]), gt.reshape(-1, gt.shape[-1])
        part.append(_ew("rs_add_pair", lambda u, w_: u.astype(F32) + w_.astype(F32), [a2, b2],
                        [(a2.shape[1], BF16)])[0].reshape(kp.shape))
    own, recv = _rs_chip_exchange(part)
    halves = []
    for o_, r_ in zip(own, recv):
        halves.append(_ew("rs_add_chips",
                          lambda a0, a1, a2, a3: ((a0.astype(F32) + a1.astype(F32)) + a2.astype(F32)) + a3.astype(F32),
                          [o_, r_[0], r_[1], r_[2]], [(o_.shape[1], F32)])[0])
    g_full = dict(zip(BIG, _rs_pair_share(halves, NL)))

    out = {}
    for name in BIG:
        shp = given[name].shape
        flat = lambda a: a.reshape(-1, shp[-1])
        res = _adamw(flat(given[name]), flat(g_full[name]), flat(given["m_" + name]), flat(given["v_" + name]))
        out[name] = [r.reshape(shp) for r in res]

    small_g = [d_rel_bias] + [jnp.stack([t.reshape(-1) for t in gs[name]]) for name in SMALL[1:]]
    g_sum = _allreduce_small(_pack_small(small_g))
    res = _adamw(_pack_small([given[n] for n in SMALL]), g_sum, _pack_small([given["m_" + n] for n in SMALL]),
                 _pack_small([given["v_" + n] for n in SMALL]))
    like = [given[n] for n in SMALL]
    unpacked = [_unpack_small(r, like) for r in res]
    for i, name in enumerate(SMALL):
        out[name] = [u[i] for u in unpacked]

    return (loss, grad_x, *[out[n][0] for n in WEIGHTS], *[out[n][1] for n in WEIGHTS],
            *[out[n][2] for n in WEIGHTS], *[out[n][3] for n in WEIGHTS])
```

```python
import functools
import math

import numpy as np
import jax
import jax.numpy as jnp
from jax import lax
from jax.experimental import pallas as pl
from jax.experimental.pallas import tpu as pltpu

F32 = jnp.float32
BF16 = jnp.bfloat16
MESH = pl.DeviceIdType.MESH

HEAD_DIM = 64
LANES = 128
QBLOCK = 128
N_BUCKETS = 32
MAX_DISTANCE = 1024
DILATED_CONFIGS = ((128, 1), (512, 4), (2048, 16))
SWA_RADIUS = 128
GROUP_B = 4
EPS = 1e-6
NEG = -1e30
Q_SCALE = HEAD_DIM ** -0.5
ADAM_LR, ADAM_B1, ADAM_B2, ADAM_EPS, ADAM_WD, ADAM_STEP = 0.001, 0.9, 0.999, 1e-08, 0.01, 10
VMEM_LIMIT = 56 * 2 ** 20
N_CHIPS = 4
N_DEV = 8

BIG = ("ffn1_w_in", "ffn1_w_out", "w_qkv", "w_o", "ffn2_w_in", "ffn2_w_out", "w_ple_gate", "w_ple_proj")
SMALL = ("rel_bias", "norm_ffn1", "norm_mix", "q_norm_a", "k_norm_a", "q_norm_b", "k_norm_b", "sink_b",
         "norm_ffn2", "norm_ple")
WEIGHTS = ("rel_bias", "norm_ffn1", "ffn1_w_in", "ffn1_w_out", "norm_mix", "w_qkv", "q_norm_a", "k_norm_a",
           "q_norm_b", "k_norm_b", "sink_b", "w_o", "norm_ffn2", "ffn2_w_in", "ffn2_w_out", "norm_ple",
           "w_ple_gate", "w_ple_proj")


def _params(n_grid):
    return pltpu.CompilerParams(dimension_semantics=("arbitrary",) * n_grid, vmem_limit_bytes=VMEM_LIMIT)


def _pick_tm(rows, cap):
    t = (min(cap, rows) // 16) * 16
    while t >= 16:
        if rows % t == 0:
            return t
        t -= 16
    return rows


def _dot(a, b):
    return jnp.dot(a, b, preferred_element_type=F32)


def _dot_nt(a, b):
    return lax.dot_general(a, b, (((1,), (1,)), ((), ())), preferred_element_type=F32)


def _dot_tn(a, b):
    return lax.dot_general(a, b, (((0,), (0,)), ((), ())), preferred_element_type=F32)


def _sigmoid(z):
    return 1.0 / (1.0 + jnp.exp(-z))


def _lo_lanes(shape):
    return lax.broadcasted_iota(jnp.int32, shape, len(shape) - 1) % LANES < HEAD_DIM


def _seg_sum(blk):
    lo = _lo_lanes(blk.shape)
    s_lo = jnp.sum(jnp.where(lo, blk, 0.0), axis=1, keepdims=True)
    s_hi = jnp.sum(jnp.where(lo, 0.0, blk), axis=1, keepdims=True)
    return jnp.where(lo, s_lo, s_hi)


def _rms_bwd_tile(x, g, dh):
    r = lax.rsqrt(jnp.mean(x * x, axis=-1, keepdims=True) + EPS)
    xh = x * r
    dyg = dh * g
    dx = r * (dyg - xh * jnp.mean(dyg * xh, axis=-1, keepdims=True))
    return dx, jnp.sum(dh * xh, axis=0, keepdims=True)


def _ew(name, fn, ins, out_defs, cap=512):
    rows = ins[0].shape[0]
    tm = _pick_tm(rows, cap)
    n_in = len(ins)

    def body(*refs):
        vals = fn(*[r[...] for r in refs[:n_in]])
        if not isinstance(vals, tuple):
            vals = (vals,)
        for r, v in zip(refs[n_in:], vals):
            r[...] = v.astype(r.dtype)

    return pl.pallas_call(
        body, name=name, grid=(rows // tm,),
        in_specs=[pl.BlockSpec((tm, a.shape[1]), lambda i: (i, 0)) for a in ins],
        out_specs=[pl.BlockSpec((tm, c), lambda i: (i, 0)) for c, _ in out_defs],
        out_shape=[jax.ShapeDtypeStruct((rows, c), dt) for c, dt in out_defs],
        compiler_params=_params(1))(*ins)


def _rms_fwd(x, g):
    T, D = x.shape
    tm = _pick_tm(T, 512)

    def body(x_ref, g_ref, h_ref):
        xv = x_ref[...]
        r = lax.rsqrt(jnp.mean(xv * xv, axis=-1, keepdims=True) + EPS)
        h_ref[...] = (xv * r * g_ref[...]).astype(BF16)

    return pl.pallas_call(
        body, name="rms_fwd", grid=(T // tm,),
        in_specs=[pl.BlockSpec((tm, D), lambda i: (i, 0)), pl.BlockSpec((1, D), lambda i: (0, 0))],
        out_specs=pl.BlockSpec((tm, D), lambda i: (i, 0)),
        out_shape=jax.ShapeDtypeStruct((T, D), BF16), compiler_params=_params(1))(x, g)


def _ffn_up(h, win, l):
    T, D = h.shape
    wc = win.shape[3]
    tm = _pick_tm(T, 512)

    def body(h_ref, wg_ref, wu_ref, gate_ref, up_ref, act_ref):
        hv = h_ref[...]
        gte = _dot(hv, wg_ref[...])
        u = _dot(hv, wu_ref[...])
        gate_ref[...] = gte.astype(BF16)
        up_ref[...] = u.astype(BF16)
        act_ref[...] = (gte * _sigmoid(gte) * u).astype(BF16)

    out = jax.ShapeDtypeStruct((T, 2 * wc), BF16)
    ospec = pl.BlockSpec((tm, wc), lambda j, i: (i, j))
    return pl.pallas_call(
        body, name="ffn_up", grid=(2, T // tm),
        in_specs=[pl.BlockSpec((tm, D), lambda j, i: (i, 0)),
                  pl.BlockSpec((None, None, D, wc), lambda j, i: (l, j, 0, 0)),
                  pl.BlockSpec((None, None, D, wc), lambda j, i: (l, j + 2, 0, 0))],
        out_specs=[ospec, ospec, ospec], out_shape=[out, out, out], compiler_params=_params(2))(h, win, win)


def _mm_res(name, res, a_list, w, l, scale):
    T, N = res.shape
    n = len(a_list)
    widths = [a.shape[1] for a in a_list]
    tm = _pick_tm(T, 512)

    def body(*refs):
        r_ref, a_refs, w_refs, o_ref = refs[0], refs[1:1 + n], refs[1 + n:1 + 2 * n], refs[1 + 2 * n]
        acc = _dot(a_refs[0][...].astype(BF16), w_refs[0][...])
        for a_ref, w_ref in zip(a_refs[1:], w_refs[1:]):
            acc = acc + _dot(a_ref[...].astype(BF16), w_ref[...])
        o_ref[...] = r_ref[...] + scale * acc

    w_specs, off = [], 0
    for k in widths:
        w_specs.append(pl.BlockSpec((None, k, N), lambda i, blk=off // k: (l, blk, 0)))
        off += k
    return pl.pallas_call(
        body, name=name, grid=(T // tm,),
        in_specs=[pl.BlockSpec((tm, N), lambda i: (i, 0))]
        + [pl.BlockSpec((tm, k), lambda i: (i, 0)) for k in widths] + w_specs,
        out_specs=pl.BlockSpec((tm, N), lambda i: (i, 0)),
        out_shape=jax.ShapeDtypeStruct((T, N), F32), compiler_params=_params(1))(res, *a_list, *([w] * n))


def _mm_plain(name, a, w, l):
    T, K = a.shape
    N = w.shape[2]
    tm = _pick_tm(T, 512)

    def body(a_ref, w_ref, o_ref):
        o_ref[...] = _dot(a_ref[...], w_ref[...])

    return pl.pallas_call(
        body, name=name, grid=(T // tm,),
        in_specs=[pl.BlockSpec((tm, K), lambda i: (i, 0)), pl.BlockSpec((None, K, N), lambda i: (l, 0, 0))],
        out_specs=pl.BlockSpec((tm, N), lambda i: (i, 0)),
        out_shape=jax.ShapeDtypeStruct((T, N), F32), compiler_params=_params(1))(a, w)


def _mm_tn(name, a, a_w, b, b_w, n_slots, a_by_slot, b_by_slot, scale, tm_cap=512):
    T = a.shape[0]
    tm = _pick_tm(T, tm_cap)
    nt = T // tm

    def body(a_ref, b_ref, o_ref, acc):
        i = pl.program_id(1)

        @pl.when(i == 0)
        def _():
            acc[...] = jnp.zeros_like(acc)

        acc[...] += _dot_tn(a_ref[...].astype(BF16), b_ref[...].astype(BF16))

        @pl.when(i == nt - 1)
        def _():
            o_ref[...] = (acc[...] * scale).astype(BF16)

    return pl.pallas_call(
        body, name=name, grid=(n_slots, nt),
        in_specs=[pl.BlockSpec((tm, a_w), (lambda s, i: (i, s)) if a_by_slot else (lambda s, i: (i, 0))),
                  pl.BlockSpec((tm, b_w), (lambda s, i: (i, s)) if b_by_slot else (lambda s, i: (i, 0)))],
        out_specs=pl.BlockSpec((None, a_w, b_w), lambda s, i: (s, 0, 0)),
        out_shape=jax.ShapeDtypeStruct((n_slots, a_w, b_w), BF16),
        scratch_shapes=[pltpu.VMEM((a_w, b_w), F32)], compiler_params=_params(2))(a, b)


def _ffn_bwd_a(dx, wout, l, gate, up):
    T, D = dx.shape
    F = gate.shape[1]
    tm = _pick_tm(T, 256)

    def body(dx_ref, w_ref, g_ref, u_ref, o_ref):
        dact = 0.5 * _dot_nt(dx_ref[...].astype(BF16), w_ref[...])
        gte = g_ref[...].astype(F32)
        u = u_ref[...].astype(F32)
        sg = _sigmoid(gte)
        silu = gte * sg
        o_ref[:, :F] = (dact * u * (sg + silu * (1.0 - sg))).astype(BF16)
        o_ref[:, F:] = (dact * silu).astype(BF16)

    return pl.pallas_call(
        body, name="ffn_bwd_a", grid=(T // tm,),
        in_specs=[pl.BlockSpec((tm, D), lambda i: (i, 0)), pl.BlockSpec((None, F, D), lambda i: (l, 0, 0)),
                  pl.BlockSpec((tm, F), lambda i: (i, 0)), pl.BlockSpec((tm, F), lambda i: (i, 0))],
        out_specs=pl.BlockSpec((tm, 2 * F), lambda i: (i, 0)),
        out_shape=jax.ShapeDtypeStruct((T, 2 * F), BF16), compiler_params=_params(1))(dx, wout, gate, up)


def _bwd_into_norm(name, d, d_w, w, l, n_slots, x, g, dx_in):
    T, D = x.shape
    tm = _pick_tm(T, 256)

    def body(*refs):
        d_refs, w_refs = refs[:n_slots], refs[n_slots:2 * n_slots]
        x_ref, g_ref, dxi_ref, dx_ref, dg_ref = refs[2 * n_slots:]
        dh = _dot_nt(d_refs[0][...], w_refs[0][...])
        for s in range(1, n_slots):
            dh = dh + _dot_nt(d_refs[s][...], w_refs[s][...])
        dxn, dg = _rms_bwd_tile(x_ref[...], g_ref[...], dh)
        dx_ref[...] = dxi_ref[...] + dxn

        @pl.when(pl.program_id(0) == 0)
        def _():
            dg_ref[...] = jnp.zeros_like(dg_ref)

        dg_ref[...] += dg

    row = pl.BlockSpec((tm, D), lambda i: (i, 0))
    vec = pl.BlockSpec((1, D), lambda i: (0, 0))
    return pl.pallas_call(
        body, name=name, grid=(T // tm,),
        in_specs=[pl.BlockSpec((tm, d_w), lambda i, s=s: (i, s)) for s in range(n_slots)]
        + [pl.BlockSpec((None, None, D, d_w), lambda i, s=s: (l, s, 0, 0)) for s in range(n_slots)]
        + [row, vec, row],
        out_specs=[row, vec],
        out_shape=[jax.ShapeDtypeStruct((T, D), F32), jax.ShapeDtypeStruct((1, D), F32)],
        compiler_params=_params(1))(*([d] * n_slots + [w] * n_slots + [x, g, dx_in]))


def _ple_fwd(x, hn, p, wg, wp, l):
    T, D = x.shape
    P = p.shape[1]
    tm = _pick_tm(T, 256)

    def body(x_ref, hn_ref, p_ref, wg_ref, wp_ref, xo_ref, gate_ref, pp_ref):
        gate = _sigmoid(_dot(hn_ref[...], wg_ref[...]))
        pp = _dot(p_ref[...].astype(BF16), wp_ref[...])
        gate_ref[...] = gate
        pp_ref[...] = pp
        xo_ref[...] = x_ref[...] + gate * pp

    row = pl.BlockSpec((tm, D), lambda i: (i, 0))
    out = jax.ShapeDtypeStruct((T, D), F32)
    return pl.pallas_call(
        body, name="ple_fwd", grid=(T // tm,),
        in_specs=[row, row, pl.BlockSpec((tm, P), lambda i: (i, 0)),
                  pl.BlockSpec((None, D, D), lambda i: (l, 0, 0)), pl.BlockSpec((None, P, D), lambda i: (l, 0, 0))],
        out_specs=[row, row, row], out_shape=[out, out, out], compiler_params=_params(1))(x, hn, p, wg, wp)


def _ple_bwd(dx, gate, pp, hn, p, x, g, wg, l):
    T, D = x.shape
    P = p.shape[1]
    tm = _pick_tm(T, 256)
    nt = T // tm

    def body(dx_ref, gate_ref, pp_ref, hn_ref, p_ref, x_ref, g_ref, wg_ref,
             dxo_ref, dg_ref, dwg_ref, dwp_ref, acc_g, acc_p):
        i = pl.program_id(0)

        @pl.when(i == 0)
        def _():
            acc_g[...] = jnp.zeros_like(acc_g)
            acc_p[...] = jnp.zeros_like(acc_p)
            dg_ref[...] = jnp.zeros_like(dg_ref)

        dxv = dx_ref[...]
        gate = gate_ref[...]
        dz = (dxv * pp_ref[...] * gate * (1.0 - gate)).astype(BF16)
        dpp = (dxv * gate).astype(BF16)
        acc_g[...] += _dot_tn(hn_ref[...], dz)
        acc_p[...] += _dot_tn(p_ref[...].astype(BF16), dpp)
        dxn, dg = _rms_bwd_tile(x_ref[...], g_ref[...], _dot_nt(dz, wg_ref[...]))
        dxo_ref[...] = dxv + dxn
        dg_ref[...] += dg

        @pl.when(i == nt - 1)
        def _():
            dwg_ref[...] = acc_g[...].astype(BF16)
            dwp_ref[...] = acc_p[...].astype(BF16)

    row = pl.BlockSpec((tm, D), lambda i: (i, 0))
    vec = pl.BlockSpec((1, D), lambda i: (0, 0))
    return pl.pallas_call(
        body, name="ple_bwd", grid=(nt,),
        in_specs=[row, row, row, row, pl.BlockSpec((tm, P), lambda i: (i, 0)), row, vec,
                  pl.BlockSpec((None, D, D), lambda i: (l, 0, 0))],
        out_specs=[row, vec, pl.BlockSpec((D, D), lambda i: (0, 0)), pl.BlockSpec((P, D), lambda i: (0, 0))],
        out_shape=[jax.ShapeDtypeStruct((T, D), F32), jax.ShapeDtypeStruct((1, D), F32),
                   jax.ShapeDtypeStruct((D, D), BF16), jax.ShapeDtypeStruct((P, D), BF16)],
        scratch_shapes=[pltpu.VMEM((D, D), F32), pltpu.VMEM((P, D), F32)],
        compiler_params=_params(1))(dx, gate, pp, hn, p, x, g, wg)


def _loss_fwd_bwd(y, tgt):
    T, D = y.shape
    tm = _pick_tm(T, 512)

    def body(y_ref, t_ref, dy_ref, loss_ref):
        e = y_ref[...] - t_ref[...]
        dy_ref[...] = e / D

        @pl.when(pl.program_id(0) == 0)
        def _():
            loss_ref[...] = jnp.zeros_like(loss_ref)

        loss_ref[...] += 0.5 * jnp.sum(jnp.mean(e * e, axis=-1, keepdims=True), axis=0, keepdims=True)

    row = pl.BlockSpec((tm, D), lambda i: (i, 0))
    return pl.pallas_call(
        body, name="loss", grid=(T // tm,), in_specs=[row, row],
        out_specs=[row, pl.BlockSpec((8, LANES), lambda i: (0, 0))],
        out_shape=[jax.ShapeDtypeStruct((T, D), F32), jax.ShapeDtypeStruct((8, LANES), F32)],
        compiler_params=_params(1))(y, tgt)


def _qkv_layout(D):
    n_a = D // (2 * LANES)
    n_b = D // (2 * LANES)
    n_kv = max(1, (2 * n_b) // GROUP_B) * HEAD_DIM // LANES
    return n_a, n_b, n_kv


def _dup_half(xv, half):
    rolled = pltpu.roll(xv, HEAD_DIM, 1)
    lo = _lo_lanes(xv.shape)
    return jnp.where(lo, xv, rolled) if half == 0 else jnp.where(lo, rolled, xv)


def _qkv_post(raw, gains):
    T, W = raw.shape
    n_a, n_b, n_kv = _qkv_layout(W * 4 // 9)
    tm = _pick_tm(T, 256)
    o_qb = 3 * n_a

    def norm(xv, gv, scale):
        ms = _seg_sum(xv * xv) * (1.0 / HEAD_DIM)
        return xv * lax.rsqrt(ms + EPS) * gv * scale

    def body(raw_ref, g_ref, a_ref, b_ref):
        def blk(cb):
            return raw_ref[:, cb * LANES:(cb + 1) * LANES]

        def gn(cb):
            return g_ref[:, cb * LANES:(cb + 1) * LANES]

        for cb in range(n_a):
            a_ref[:, cb * LANES:(cb + 1) * LANES] = norm(blk(cb), gn(cb), Q_SCALE)
            cbk = n_a + cb
            a_ref[:, cbk * LANES:(cbk + 1) * LANES] = norm(blk(cbk), gn(cbk), 1.0)
            cbv = 2 * n_a + cb
            a_ref[:, cbv * LANES:(cbv + 1) * LANES] = blk(cbv)
        for cb in range(n_b):
            src = o_qb + cb
            b_ref[:, cb * LANES:(cb + 1) * LANES] = norm(blk(src), gn(src), Q_SCALE)
        for e in range(n_b):
            kvh = (2 * e) // GROUP_B
            ck = o_qb + n_b + kvh // 2
            cv = ck + n_kv
            kn = norm(blk(ck), gn(ck), 1.0)
            b_ref[:, (n_b + e) * LANES:(n_b + e + 1) * LANES] = _dup_half(kn, kvh % 2)
            b_ref[:, (2 * n_b + e) * LANES:(2 * n_b + e + 1) * LANES] = _dup_half(blk(cv), kvh % 2)

    wa, wb = 3 * n_a * LANES, 3 * n_b * LANES
    return pl.pallas_call(
        body, name="qkv_post", grid=(T // tm,),
        in_specs=[pl.BlockSpec((tm, W), lambda i: (i, 0)), pl.BlockSpec((1, W), lambda i: (0, 0))],
        out_specs=[pl.BlockSpec((tm, wa), lambda i: (i, 0)), pl.BlockSpec((tm, wb), lambda i: (i, 0))],
        out_shape=[jax.ShapeDtypeStruct((T, wa), F32), jax.ShapeDtypeStruct((T, wb), F32)],
        compiler_params=_params(1))(raw, gains)


def _qkv_post_bwd(raw, gains, d_a, d_b):
    T, W = raw.shape
    n_a, n_b, n_kv = _qkv_layout(W * 4 // 9)
    tm = _pick_tm(T, 256)
    o_qb = 3 * n_a

    def body(raw_ref, g_ref, daq, dak, dav, dbq, dbk, dbv, o_ref, dg_ref):
        @pl.when(pl.program_id(0) == 0)
        def _():
            dg_ref[...] = jnp.zeros_like(dg_ref)

        def cols(ref, cb):
            return ref[:, cb * LANES:(cb + 1) * LANES]

        def norm_bwd(cb, dy, scale):
            xv = cols(raw_ref, cb)
            gv = cols(g_ref, cb)
            r = lax.rsqrt(_seg_sum(xv * xv) * (1.0 / HEAD_DIM) + EPS)
            xh = xv * r
            dys = dy * scale
            dyg = dys * gv
            dxv = r * (dyg - xh * (_seg_sum(dyg * xh) * (1.0 / HEAD_DIM)))
            o_ref[:, cb * LANES:(cb + 1) * LANES] = dxv.astype(BF16)
            dg_ref[:, cb * LANES:(cb + 1) * LANES] += jnp.sum(dys * xh, axis=0, keepdims=True)

        def fold(ref, kv_blk):
            halves = []
            for half in range(2):
                kvh = 2 * kv_blk + half
                blocks = [e for e in range(n_b) if (2 * e) // GROUP_B == kvh]
                s = cols(ref, blocks[0])
                for e in blocks[1:]:
                    s = s + cols(ref, e)
                halves.append(s + pltpu.roll(s, HEAD_DIM, 1))
            return jnp.where(_lo_lanes(halves[0].shape), halves[0], halves[1])

        for cb in range(n_a):
            norm_bwd(cb, cols(daq, cb), Q_SCALE)
            norm_bwd(n_a + cb, cols(dak, cb), 1.0)
            cbv = 2 * n_a + cb
            o_ref[:, cbv * LANES:(cbv + 1) * LANES] = cols(dav, cb).astype(BF16)
        for cb in range(n_b):
            norm_bwd(o_qb + cb, cols(dbq, cb), Q_SCALE)
        for kb in range(n_kv):
            ck = o_qb + n_b + kb
            cv = ck + n_kv
            norm_bwd(ck, fold(dbk, kb), 1.0)
            o_ref[:, cv * LANES:(cv + 1) * LANES] = fold(dbv, kb).astype(BF16)

    hw_a, hw_b = n_a * LANES, n_b * LANES
    return pl.pallas_call(
        body, name="qkv_post_bwd", grid=(T // tm,),
        in_specs=[pl.BlockSpec((tm, W), lambda i: (i, 0)), pl.BlockSpec((1, W), lambda i: (0, 0))]
        + [pl.BlockSpec((tm, hw_a), lambda i: (i, 0))] * 3 + [pl.BlockSpec((tm, hw_b), lambda i: (i, 0))] * 3,
        out_specs=[pl.BlockSpec((tm, W), lambda i: (i, 0)), pl.BlockSpec((1, W), lambda i: (0, 0))],
        out_shape=[jax.ShapeDtypeStruct((T, W), BF16), jax.ShapeDtypeStruct((1, W), F32)],
        compiler_params=_params(1))(raw, gains, *d_a, *d_b)


def _t5_bucket_np(rel):
    half = N_BUCKETS // 2
    max_exact = half // 2
    ret = np.where(rel > 0, half, 0)
    n = np.abs(rel)
    nf = np.maximum(n, 1).astype(np.float32)
    large = max_exact + (np.log(nf / np.float32(max_exact)) / np.float32(math.log(MAX_DISTANCE / max_exact))
                         * np.float32(half - max_exact)).astype(np.int32)
    large = np.minimum(large, half - 1)
    return ret + np.where(n < max_exact, n, large)


def _window_pad(radius):
    assert radius <= QBLOCK
    return HEAD_DIM if radius <= HEAD_DIM else QBLOCK


def _bucket_maps(configs):
    pad = _window_pad(configs[0][0])
    q = np.arange(QBLOCK)[:, None]
    kk = np.arange(QBLOCK + 2 * pad)[None, :]
    rel = kk - pad - q
    maps = [np.where(np.abs(rel) <= radius, _t5_bucket_np(rel * dil), -1) for radius, dil in configs]
    return np.stack(maps).astype(np.int32)


def _bias_build(rel_bias, bmaps, col0):
    n_sets, _, W = bmaps.shape
    n_heads = rel_bias.shape[1] // 2

    def body(rb_ref, bm_ref, o_ref):
        h = pl.program_id(1)
        bm = bm_ref[...]

        def step(n, acc):
            return jnp.where(bm == n, rb_ref[n, col0 + h], acc)

        o_ref[...] = lax.fori_loop(0, N_BUCKETS, step, jnp.where(bm < 0, NEG, 0.0).astype(F32))

    return pl.pallas_call(
        body, name="bias_build", grid=(n_sets, n_heads),
        in_specs=[pl.BlockSpec(memory_space=pltpu.SMEM), pl.BlockSpec((None, QBLOCK, W), lambda s, h: (s, 0, 0))],
        out_specs=pl.BlockSpec((None, None, QBLOCK, W), lambda s, h: (s, h, 0, 0)),
        out_shape=jax.ShapeDtypeStruct((n_sets, n_heads, QBLOCK, W), F32),
        compiler_params=_params(2))(rel_bias, bmaps)


def _bias_grad(dtiles, bmaps, col0):
    n_sets, _, W = bmaps.shape
    n_heads = dtiles[0].shape[1]
    n_l = len(dtiles)

    def body(*refs):
        bm_ref, o_ref = refs[0], refs[1 + n_l]
        s, h = pl.program_id(0), pl.program_id(1)

        @pl.when((s == 0) & (h == 0))
        def _():
            o_ref[...] = jnp.zeros_like(o_ref)

        d = refs[1][...]
        for r in refs[2:1 + n_l]:
            d = d + r[...]
        bm = bm_ref[...]
        rows = lax.broadcasted_iota(jnp.int32, o_ref.shape, 0)
        lanes = lax.broadcasted_iota(jnp.int32, o_ref.shape, 1)

        def step(n, acc):
            val = jnp.sum(jnp.where(bm == n, d, 0.0))
            return acc + jnp.where((rows == n) & (lanes == col0 + h), val, 0.0)

        o_ref[...] += lax.fori_loop(0, N_BUCKETS, step, jnp.zeros(o_ref.shape, F32))

    tile = pl.BlockSpec((None, None, QBLOCK, W), lambda s, h: (s, h, 0, 0))
    return pl.pallas_call(
        body, name="bias_grad", grid=(n_sets, n_heads),
        in_specs=[pl.BlockSpec((None, QBLOCK, W), lambda s, h: (s, 0, 0))] + [tile] * n_l,
        out_specs=pl.BlockSpec((N_BUCKETS, LANES), lambda s, h: (0, 0)),
        out_shape=jax.ShapeDtypeStruct((N_BUCKETS, LANES), F32), compiler_params=_params(2))(bmaps, *dtiles)


def _rows(l_start, n, d, r):
    if d == 1:
        return pl.ds(pl.multiple_of(l_start, 8), n)
    return pl.ds(l_start * d + r, n, stride=d)


def _block_geometry(b, nb_sub, pad):
    r, lb = b // nb_sub, b % nb_sub
    l0 = lb * QBLOCK
    lp = jnp.maximum(l0 - pad, 0)
    ln = jnp.minimum(l0 + QBLOCK, nb_sub * QBLOCK - pad)
    col = lax.broadcasted_iota(jnp.int32, (QBLOCK, QBLOCK + 2 * pad), 1)
    valid = ((col >= pad) | (lb != 0)) & ((col < pad + QBLOCK) | (lb != nb_sub - 1))
    return r, l0, lp, ln, valid


def _window(ref, l0, lp, ln, pad, d, r):
    return jnp.concatenate([ref[_rows(lp, pad, d, r), :], ref[_rows(l0, QBLOCK, d, r), :],
                            ref[_rows(ln, pad, d, r), :]], axis=0)


def _attn_fwd(qkv, bias, sink, dils, pad):
    T = qkv.shape[0]
    hw = qkv.shape[1] // 3
    ng = hw // LANES
    n_br = len(dils)
    n_blocks = T // QBLOCK
    W = QBLOCK + 2 * pad
    chunk = 256

    def body(sink_ref, q_ref, k_ref, v_ref, bias_ref, o_ref, lse_ref, *scratch):
        g = pl.program_id(0)
        lo = _lo_lanes((QBLOCK, LANES))
        for c, d in enumerate(dils):
            nb_sub = n_blocks // d
            o_dst = scratch[0].at[c] if n_br > 1 else o_ref
            l_dst = scratch[1].at[c] if n_br > 1 else lse_ref

            def block(b, carry, c=c, d=d, nb_sub=nb_sub, o_dst=o_dst, l_dst=l_dst):
                r, l0, lp, ln, valid = _block_geometry(b, nb_sub, pad)
                q = q_ref[_rows(l0, QBLOCK, d, r), :].astype(BF16)
                k = _window(k_ref, l0, lp, ln, pad, d, r).astype(BF16)
                v = _window(v_ref, l0, lp, ln, pad, d, r).astype(BF16)
                outs, lses = [], []
                for hh in range(2):
                    hm = lo if hh == 0 else jnp.logical_not(lo)
                    s = _dot_nt(jnp.where(hm, q, jnp.zeros_like(q)), k) + bias_ref[c, hh]
                    s = jnp.where(valid, s, NEG)
                    snk = sink_ref[2 * g + hh]
                    m = jnp.maximum(jnp.max(s, axis=1, keepdims=True), snk)
                    p = jnp.exp(s - m)
                    den = jnp.sum(p, axis=1, keepdims=True) + jnp.exp(snk - m)
                    outs.append(_dot(p.astype(BF16), v) / den)
                    lses.append(m + jnp.log(den))
                o_dst[_rows(l0, QBLOCK, d, r), :] = jnp.where(lo, outs[0], outs[1])
                l_dst[_rows(l0, QBLOCK, d, r), :] = jnp.where(lo, lses[0], lses[1])
                return carry

            lax.fori_loop(0, n_blocks, block, 0)

        if n_br > 1:
            def merge(i, carry):
                rs = pl.ds(pl.multiple_of(i * chunk, chunk), chunk)
                ls = [scratch[1][c, rs, :] for c in range(n_br)]
                m = ls[0]
                for t in ls[1:]:
                    m = jnp.maximum(m, t)
                ws = [jnp.exp(t - m) for t in ls]
                z = ws[0]
                acc = ws[0] * scratch[0][0, rs, :]
                for c in range(1, n_br):
                    z = z + ws[c]
                    acc = acc + ws[c] * scratch[0][c, rs, :]
                o_ref[rs, :] = acc / z
                lse_ref[rs, :] = m + jnp.log(z)
                return carry

            lax.fori_loop(0, T // chunk, merge, 0)

    def col(base):
        return pl.BlockSpec((T, LANES), lambda g: (0, base + g))

    out = jax.ShapeDtypeStruct((T, hw), F32)
    scratch = [pltpu.VMEM((n_br, T, LANES), F32)] * 2 if n_br > 1 else []
    return pl.pallas_call(
        body, name="attn_fwd", grid=(ng,),
        in_specs=[pl.BlockSpec(memory_space=pltpu.SMEM), col(0), col(ng), col(2 * ng),
                  pl.BlockSpec((n_br, 2, QBLOCK, W), lambda g: (0, g, 0, 0))],
        out_specs=[col(0), col(0)], out_shape=[out, out], scratch_shapes=scratch,
        compiler_params=_params(1))(sink, qkv, qkv, qkv, bias)


def _attn_bwd(qkv, bias, sink, dils, pad, do, lse, dd, col_base):
    T = qkv.shape[0]
    hw = qkv.shape[1] // 3
    ng = hw // LANES
    n_br = len(dils)
    n_blocks = T // QBLOCK
    W = QBLOCK + 2 * pad

    def body(sink_ref, q_ref, k_ref, v_ref, bias_ref, do_ref, lse_ref, dd_ref,
             dq_ref, dk_ref, dv_ref, dt_ref, ds_ref):
        g = pl.program_id(0)
        dq_ref[...] = jnp.zeros_like(dq_ref)
        dk_ref[...] = jnp.zeros_like(dk_ref)
        dv_ref[...] = jnp.zeros_like(dv_ref)
        dt_ref[...] = jnp.zeros_like(dt_ref)
        ds_ref[...] = jnp.zeros_like(ds_ref)
        lo = _lo_lanes((QBLOCK, LANES))
        snk = jnp.where(lo, sink_ref[2 * g], sink_ref[2 * g + 1])
        for c, d in enumerate(dils):
            nb_sub = n_blocks // d

            def block(b, carry, c=c, d=d, nb_sub=nb_sub):
                r, l0, lp, ln, valid = _block_geometry(b, nb_sub, pad)
                rows_q = _rows(l0, QBLOCK, d, r)
                q = q_ref[rows_q, :].astype(BF16)
                k = _window(k_ref, l0, lp, ln, pad, d, r).astype(BF16)
                v = _window(v_ref, l0, lp, ln, pad, d, r).astype(BF16)
                dob = do_ref[rows_q, :].astype(BF16)
                lse_b = lse_ref[rows_q, :]
                dd_b = dd_ref[rows_q, :]
                dqs = []
                dkw = jnp.zeros((W, LANES), F32)
                dvw = jnp.zeros((W, LANES), F32)
                for hh in range(2):
                    hm = lo if hh == 0 else jnp.logical_not(lo)
                    qm = jnp.where(hm, q, jnp.zeros_like(q))
                    dom = jnp.where(hm, dob, jnp.zeros_like(dob))
                    s = _dot_nt(qm, k) + bias_ref[c, hh]
                    s = jnp.where(valid, s, NEG)
                    p = jnp.exp(s - lse_b[:, hh * HEAD_DIM:hh * HEAD_DIM + 1])
                    ds = p * (_dot_nt(dom, v) - dd_b[:, hh * HEAD_DIM:hh * HEAD_DIM + 1])
                    dsb = ds.astype(BF16)
                    dqs.append(_dot(dsb, k))
                    dkw = dkw + _dot_tn(dsb, qm)
                    dvw = dvw + _dot_tn(p.astype(BF16), dom)
                    dt_ref[c, hh] += ds
                dq_ref[rows_q, :] += jnp.where(lo, dqs[0], dqs[1])
                ds_ref[0:1, :] += jnp.sum(-jnp.exp(snk - lse_b) * dd_b, axis=0, keepdims=True)
                for part, (start, n) in zip((0, pad, pad + QBLOCK), ((lp, pad), (l0, QBLOCK), (ln, pad))):
                    dk_ref[_rows(start, n, d, r), :] += dkw[part:part + n]
                    dv_ref[_rows(start, n, d, r), :] += dvw[part:part + n]
                return carry

            lax.fori_loop(0, n_blocks, block, 0)

    def col(base):
        return pl.BlockSpec((T, LANES), lambda g: (0, base + g))

    tile = pl.BlockSpec((n_br, 2, QBLOCK, W), lambda g: (0, g, 0, 0))
    full = jax.ShapeDtypeStruct((T, hw), F32)
    return pl.pallas_call(
        body, name="attn_bwd", grid=(ng,),
        in_specs=[pl.BlockSpec(memory_space=pltpu.SMEM), col(0), col(ng), col(2 * ng), tile,
                  col(col_base), col(0), col(col_base)],
        out_specs=[col(0), col(0), col(0), tile, pl.BlockSpec((None, 8, LANES), lambda g: (g, 0, 0))],
        out_shape=[full, full, full, jax.ShapeDtypeStruct((n_br, 2 * ng, QBLOCK, W), F32),
                   jax.ShapeDtypeStruct((ng, 8, LANES), F32)],
        compiler_params=_params(1))(sink, qkv, qkv, qkv, bias, do, lse, dd)


def _mix_bwd_in(dx, wo, l, o_list):
    T, D = dx.shape
    widths = [o.shape[1] for o in o_list]
    hw = sum(widths)
    n = len(o_list)
    tm = _pick_tm(T, 256)

    def body(*refs):
        dx_ref, w_ref, o_refs, do_ref, dd_ref = refs[0], refs[1], refs[2:2 + n], refs[2 + n], refs[3 + n]
        dov = _dot_nt(dx_ref[...].astype(BF16), w_ref[...])
        do_ref[...] = dov
        off = 0
        for o_ref, k in zip(o_refs, widths):
            prod = dov[:, off:off + k] * o_ref[...]
            for cb in range(k // LANES):
                dd_ref[:, off + cb * LANES:off + (cb + 1) * LANES] = _seg_sum(prod[:, cb * LANES:(cb + 1) * LANES])
            off += k

    row = pl.BlockSpec((tm, hw), lambda i: (i, 0))
    out = jax.ShapeDtypeStruct((T, hw), F32)
    return pl.pallas_call(
        body, name="mix_bwd_in", grid=(T // tm,),
        in_specs=[pl.BlockSpec((tm, D), lambda i: (i, 0)), pl.BlockSpec((None, hw, D), lambda i: (l, 0, 0))]
        + [pl.BlockSpec((tm, k), lambda i: (i, 0)) for k in widths],
        out_specs=[row, row], out_shape=[out, out], compiler_params=_params(1))(dx, wo, *o_list)


def _mesh_pos():
    return lax.axis_index("x"), lax.axis_index("y"), lax.axis_index("c")


def _my_chip():
    return 2 * lax.axis_index("x") + lax.axis_index("y")


def _other_chips(x, y):
    return [(1 - x, y), (x, 1 - y), (1 - x, 1 - y)]


HBM_SPEC = pl.BlockSpec(memory_space=pltpu.HBM)


def _half_rows(rows, cc):
    hr = rows // 2
    return pl.ds(pl.multiple_of(cc * hr, 16), hr)


def _cast_into_slot(w):
    NL, R, C = w.shape
    tm = _pick_tm(R, 512)

    def body(w_ref, o_ref):
        o_ref[...] = w_ref[...].astype(BF16)

    return pl.pallas_call(
        body, name="cast_into_slot", grid=(NL, R // tm),
        in_specs=[pl.BlockSpec((None, tm, C), lambda l, i: (l, i, 0))],
        out_specs=pl.BlockSpec((None, None, tm, C), lambda l, i: (l, _my_chip(), i, 0)),
        out_shape=jax.ShapeDtypeStruct((NL, N_CHIPS, R, C), BF16), compiler_params=_params(2))(w)


def _allgather_weights(slotted):
    n = len(slotted)

    def body(*refs):
        bufs = refs[n:2 * n]
        ici_s, ici_r, d2d_s, d2d_r = refs[2 * n:]
        x, y, c = _mesh_pos()
        my = 2 * x + y
        chips = _other_chips(x, y)

        def part(w, slot, cc):
            return bufs[w].at[:, slot, _half_rows(slotted[w].shape[2], cc), :]

        def ici(k, w, slot):
            px, py = chips[k]
            return pltpu.make_async_remote_copy(
                src_ref=part(w, slot, c), dst_ref=part(w, slot, c),
                send_sem=ici_s.at[k * n + w], recv_sem=ici_r.at[k * n + w], device_id=(px, py, c), device_id_type=MESH)

        def passed(k, w, cc):
            px, py = chips[k]
            return pltpu.make_async_remote_copy(
                src_ref=part(w, 2 * px + py, cc), dst_ref=part(w, 2 * px + py, cc),
                send_sem=d2d_s.at[k * n + w], recv_sem=d2d_r.at[k * n + w], device_id=(x, y, 1 - c), device_id_type=MESH)

        for w in range(n):
            for k in range(3):
                ici(k, w, my).start()
        for w in range(n):
            for k, (px, py) in enumerate(chips):
                ici(k, w, 2 * px + py).wait_recv()
                passed(k, w, c).start()
        for w in range(n):
            for k in range(3):
                passed(k, w, 1 - c).wait_recv()
        for w in range(n):
            for k in range(3):
                ici(k, w, my).wait_send()
                passed(k, w, c).wait_send()

    return pl.pallas_call(
        body, name="allgather_weights",
        in_specs=[HBM_SPEC] * n, out_specs=[HBM_SPEC] * n,
        out_shape=[jax.ShapeDtypeStruct(s.shape, s.dtype) for s in slotted],
        input_output_aliases={w: w for w in range(n)},
        scratch_shapes=[pltpu.SemaphoreType.DMA((3 * n,)), pltpu.SemaphoreType.DMA((3 * n,)),
                        pltpu.SemaphoreType.DMA((3 * n,)), pltpu.SemaphoreType.DMA((3 * n,))],
    )(*slotted)


def _rs_pair_exchange(grads):
    n = len(grads)

    def body(*refs):
        ins, recv = refs[:n], refs[n:2 * n]
        send_s, recv_s = refs[2 * n:]
        x, y, c = _mesh_pos()
        remote = [pltpu.make_async_remote_copy(
            src_ref=ins[t].at[:, _half_rows(grads[t].shape[1], 1 - c), :], dst_ref=recv[t],
            send_sem=send_s.at[t], recv_sem=recv_s.at[t], device_id=(x, y, 1 - c), device_id_type=MESH)
            for t in range(n)]
        for cp in remote:
            cp.start()
        for cp in remote:
            cp.wait_recv()
        for cp in remote:
            cp.wait_send()

    return pl.pallas_call(
        body, name="rs_pair_exchange", in_specs=[HBM_SPEC] * n, out_specs=[HBM_SPEC] * n,
        out_shape=[jax.ShapeDtypeStruct((g.shape[0], g.shape[1] // 2, g.shape[2]), g.dtype) for g in grads],
        scratch_shapes=[pltpu.SemaphoreType.DMA((n,)), pltpu.SemaphoreType.DMA((n,))],
    )(*grads)


def _rs_add_pair(grad, recv):
    n_slot, hr, C = recv.shape
    tm = _pick_tm(hr, 256)
    nb = hr // tm

    def body(a_ref, b_ref, o_ref):
        o_ref[...] = (a_ref[...].astype(F32) + b_ref[...].astype(F32)).astype(BF16)

    blk = pl.BlockSpec((None, tm, C), lambda k, i: (k, i, 0))
    return pl.pallas_call(
        body, name="rs_add_pair", grid=(n_slot, nb),
        in_specs=[pl.BlockSpec((None, tm, C), lambda k, i: (k, lax.axis_index("c") * nb + i, 0)), blk],
        out_specs=blk, out_shape=jax.ShapeDtypeStruct(recv.shape, BF16), compiler_params=_params(2))(grad, recv)


def _rs_chip_exchange(parts):
    n = len(parts)

    def body(*refs):
        ins, recv = refs[:n], refs[n:2 * n]
        send_s, recv_s = refs[2 * n:]
        x, y, c = _mesh_pos()
        remote = []
        for t in range(n):
            for k, (px, py) in enumerate(_other_chips(x, y)):
                remote.append(pltpu.make_async_remote_copy(
                    src_ref=ins[t].at[2 * px + py], dst_ref=recv[t].at[k],
                    send_sem=send_s.at[3 * t + k], recv_sem=recv_s.at[3 * t + k],
                    device_id=(px, py, c), device_id_type=MESH))
        for cp in remote:
            cp.start()
        for cp in remote:
            cp.wait_recv()
        for cp in remote:
            cp.wait_send()

    return pl.pallas_call(
        body, name="rs_chip_exchange", in_specs=[HBM_SPEC] * n, out_specs=[HBM_SPEC] * n,
        out_shape=[jax.ShapeDtypeStruct((3,) + p.shape[1:], p.dtype) for p in parts],
        scratch_shapes=[pltpu.SemaphoreType.DMA((3 * n,)), pltpu.SemaphoreType.DMA((3 * n,))],
    )(*parts)


def _rs_add_chips(part, recv):
    _, hr, C = part.shape
    tm = _pick_tm(hr, 256)
    nb = hr // tm

    def body(a_ref, r0, r1, r2, o_ref):
        o_ref[...] = ((a_ref[...].astype(F32) + r0[...].astype(F32)) + r1[...].astype(F32)) + r2[...].astype(F32)

    def rel(k):
        return pl.BlockSpec((None, tm, C), lambda i: (k, i, 0))

    return pl.pallas_call(
        body, name="rs_add_chips", grid=(nb,),
        in_specs=[pl.BlockSpec((None, tm, C), lambda i: (_my_chip(), i, 0)), rel(0), rel(1), rel(2)],
        out_specs=pl.BlockSpec((tm, C), lambda i: (lax.axis_index("c") * nb + i, 0)),
        out_shape=jax.ShapeDtypeStruct((2 * hr, C), F32), compiler_params=_params(1))(part, recv, recv, recv)


def _rs_pair_share(halves):
    n = len(halves)

    def body(*refs):
        bufs = refs[n:2 * n]
        send_s, recv_s = refs[2 * n:]
        x, y, c = _mesh_pos()

        def half(t, cc):
            return bufs[t].at[_half_rows(halves[t].shape[0], cc), :]

        sent = [pltpu.make_async_remote_copy(
            src_ref=half(t, c), dst_ref=half(t, c), send_sem=send_s.at[t], recv_sem=recv_s.at[t],
            device_id=(x, y, 1 - c), device_id_type=MESH) for t in range(n)]
        landed = [pltpu.make_async_remote_copy(
            src_ref=half(t, 1 - c), dst_ref=half(t, 1 - c), send_sem=send_s.at[t], recv_sem=recv_s.at[t],
            device_id=(x, y, 1 - c), device_id_type=MESH) for t in range(n)]
        for cp in sent:
            cp.start()
        for cp in landed:
            cp.wait_recv()
        for cp in sent:
            cp.wait_send()

    return pl.pallas_call(
        body, name="rs_pair_share", in_specs=[HBM_SPEC] * n, out_specs=[HBM_SPEC] * n,
        out_shape=[jax.ShapeDtypeStruct(h.shape, h.dtype) for h in halves],
        input_output_aliases={t: t for t in range(n)},
        scratch_shapes=[pltpu.SemaphoreType.DMA((n,)), pltpu.SemaphoreType.DMA((n,))],
    )(*halves)


def _allreduce_small(v):
    rows = v.shape[0]

    def body(v_ref, o_ref, buf, send_s, recv_s):
        x, y, c = _mesh_pos()
        me = 4 * x + 2 * y + c
        buf[me] = v_ref[...]
        copies = []
        for r in range(1, N_DEV):
            px = 1 - x if r & 4 else x
            py = 1 - y if r & 2 else y
            pc = 1 - c if r & 1 else c
            send = pltpu.make_async_remote_copy(
                src_ref=v_ref, dst_ref=buf.at[me], send_sem=send_s.at[r - 1], recv_sem=recv_s.at[r - 1],
                device_id=(px, py, pc), device_id_type=MESH)
            peer_slot = buf.at[4 * px + 2 * py + pc]
            landed = pltpu.make_async_remote_copy(
                src_ref=peer_slot, dst_ref=peer_slot, send_sem=send_s.at[r - 1], recv_sem=recv_s.at[r - 1],
                device_id=(px, py, pc), device_id_type=MESH)
            copies.append((send, landed))
        for send, _ in copies:
            send.start()
        for _, landed in copies:
            landed.wait_recv()
        for send, _ in copies:
            send.wait_send()
        acc = buf[0]
        for j in range(1, N_DEV):
            acc = acc + buf[j]
        o_ref[...] = acc

    vm = pl.BlockSpec(memory_space=pltpu.VMEM)
    return pl.pallas_call(
        body, name="allreduce_small", in_specs=[vm], out_specs=vm,
        out_shape=jax.ShapeDtypeStruct((rows, LANES), F32),
        scratch_shapes=[pltpu.VMEM((N_DEV, rows, LANES), F32), pltpu.SemaphoreType.DMA((N_DEV - 1,)),
                        pltpu.SemaphoreType.DMA((N_DEV - 1,))],
    )(v)


def _adamw_fn(w, g, m, v):
    m2 = ADAM_B1 * m + (1.0 - ADAM_B1) * g
    v2 = ADAM_B2 * v + (1.0 - ADAM_B2) * (g * g)
    m_hat = m2 / (1.0 - ADAM_B1 ** ADAM_STEP)
    v_hat = v2 / (1.0 - ADAM_B2 ** ADAM_STEP)
    delta = -ADAM_LR * (m_hat / (jnp.sqrt(v_hat) + ADAM_EPS) + ADAM_WD * w)
    return g, delta, m2, v2


def _adamw_layer(w, g, m, v, l, prev):
    NL, R, C = w.shape
    tm = _pick_tm(R, 128)
    n_prev = 0 if prev is None else 4

    def body(w_ref, g_ref, m_ref, v_ref, *rest):
        outs = rest[n_prev:]
        for o_ref, val in zip(outs, _adamw_fn(w_ref[...], g_ref[...], m_ref[...], v_ref[...])):
            o_ref[...] = val

    lay = pl.BlockSpec((None, tm, C), lambda i: (l, i, 0))
    shape = jax.ShapeDtypeStruct((NL, R, C), F32)
    return pl.pallas_call(
        body, name="adamw", grid=(R // tm,),
        in_specs=[lay, pl.BlockSpec((tm, C), lambda i: (i, 0)), lay, lay] + [pl.BlockSpec(memory_space=pl.ANY)] * n_prev,
        out_specs=[lay] * 4, out_shape=[shape] * 4,
        input_output_aliases={4 + j: j for j in range(n_prev)},
        compiler_params=_params(1))(w, g, m, v, *(prev or []))


def _pack_small(parts):
    out = []
    for a in parts:
        flat = a.reshape(-1)
        n = -(-flat.shape[0] // (8 * LANES)) * 8 * LANES
        out.append(jnp.pad(flat, (0, n - flat.shape[0])).reshape(-1, LANES))
    return jnp.concatenate(out, axis=0)


def _unpack_small(packed, like):
    out, r = [], 0
    for a in like:
        size = int(np.prod(a.shape))
        rows = -(-size // (8 * LANES)) * 8
        out.append(packed[r:r + rows].reshape(-1)[:size].reshape(a.shape))
        r += rows
    return out


def _ffn_forward(x, g, win, wout, l):
    h = _rms_fwd(x, g)
    gate, up, act = _ffn_up(h, win, l)
    return _mm_res("ffn_down", x, [act], wout, l, 0.5), (x, h, gate, up, act)


def _ffn_backward(dx, saved, g, win, wout, l):
    x, h, gate, up, act = saved
    D = x.shape[1]
    wc = win.shape[3]
    dgu = _ffn_bwd_a(dx, wout, l, gate, up)
    dwout = _mm_tn("dw_ffn_out", act, wc, dx, D, 2, True, False, 0.5)
    dwin = _mm_tn("dw_ffn_in", h, D, dgu, wc, 4, False, True, 1.0)
    dx_in, dg = _bwd_into_norm("ffn_bwd_b", dgu, wc, win, l, 4, x, g, dx)
    return dx_in, dg, dwin, dwout.reshape(N_CHIPS, -1, D)


def kernel(x, p, rel_bias, norm_ffn1, ffn1_w_in, ffn1_w_out, norm_mix, w_qkv, q_norm_a, k_norm_a, q_norm_b, k_norm_b, sink_b, w_o, norm_ffn2, ffn2_w_in, ffn2_w_out, norm_ple, w_ple_gate, w_ple_proj, loss_target, m_rel_bias, m_norm_ffn1, m_ffn1_w_in, m_ffn1_w_out, m_norm_mix, m_w_qkv, m_q_norm_a, m_k_norm_a, m_q_norm_b, m_k_norm_b, m_sink_b, m_w_o, m_norm_ffn2, m_ffn2_w_in, m_ffn2_w_out, m_norm_ple, m_w_ple_gate, m_w_ple_proj, v_rel_bias, v_norm_ffn1, v_ffn1_w_in, v_ffn1_w_out, v_norm_mix, v_w_qkv, v_q_norm_a, v_k_norm_a, v_q_norm_b, v_k_norm_b, v_sink_b, v_w_o, v_norm_ffn2, v_ffn2_w_in, v_ffn2_w_out, v_norm_ple, v_w_ple_gate, v_w_ple_proj):
    given = dict(locals())
    T, D = x.shape[1], x.shape[2]
    NL = norm_ffn1.shape[0]
    x0 = x.reshape(T, D)
    tgt = loss_target.reshape(T, D)
    n_a, n_b, n_kv = _qkv_layout(D)

    gathered = dict(zip(BIG, _allgather_weights([_cast_into_slot(given[name]) for name in BIG])))

    def by_rows(a):
        return a.reshape(NL, -1, a.shape[-1])

    def by_cols(a):
        return a.transpose(0, 2, 1, 3).reshape(NL, a.shape[2], -1)

    win1, win2 = gathered["ffn1_w_in"], gathered["ffn2_w_in"]
    wout1, wout2 = by_rows(gathered["ffn1_w_out"]), by_rows(gathered["ffn2_w_out"])
    wqkv = by_cols(gathered["w_qkv"])
    wqkv4 = wqkv.reshape(NL, 1, D, -1)
    wo = by_rows(gathered["w_o"])
    wpg = by_rows(gathered["w_ple_gate"])
    wpp = by_cols(gathered["w_ple_proj"])
    QW = wqkv.shape[2]

    dils = tuple(d for _, d in DILATED_CONFIGS)
    cfg_a = [(w // (2 * d), d) for w, d in DILATED_CONFIGS]
    pad_a, pad_b = _window_pad(cfg_a[0][0]), _window_pad(SWA_RADIUS)
    bmaps_a, bmaps_b = jnp.asarray(_bucket_maps(cfg_a)), jnp.asarray(_bucket_maps([(SWA_RADIUS, 1)]))
    n_heads = rel_bias.shape[1] // 2
    bias_a = _bias_build(rel_bias, bmaps_a, 0)
    bias_b = _bias_build(rel_bias, bmaps_b, n_heads)
    no_sink = jnp.full((n_heads,), NEG, F32)

    def gains_row(l):
        ones = jnp.ones((n_a * LANES,), F32)
        return jnp.concatenate([
            jnp.tile(q_norm_a[l], 2 * n_a), jnp.tile(k_norm_a[l], 2 * n_a), ones,
            jnp.tile(q_norm_b[l], 2 * n_b), jnp.tile(k_norm_b[l], 2 * n_kv), jnp.ones((n_kv * LANES,), F32)]).reshape(1, QW)

    saved = []
    xc = x0
    for l in range(NL):
        s = {}
        xc, s["ffn1"] = _ffn_forward(xc, norm_ffn1[l:l + 1], win1, wout1, l)
        s["x1"] = xc
        h2 = _rms_fwd(xc, norm_mix[l:l + 1])
        raw = _mm_plain("qkv_proj", h2, wqkv, l)
        s["h2"], s["raw"] = h2, raw
        s["qkv_a"], s["qkv_b"] = _qkv_post(raw, gains_row(l))
        s["o_a"], s["lse_a"] = _attn_fwd(s["qkv_a"], bias_a, no_sink, dils, pad_a)
        s["o_b"], s["lse_b"] = _attn_fwd(s["qkv_b"], bias_b, sink_b[l], (1,), pad_b)
        xc = _mm_res("attn_out", xc, [s["o_a"], s["o_b"]], wo, l, 1.0)
        xc, s["ffn2"] = _ffn_forward(xc, norm_ffn2[l:l + 1], win2, wout2, l)
        s["x3"] = xc
        s["hn"] = _rms_fwd(xc, norm_ple[l:l + 1])
        s["p"] = p[l].reshape(T, -1)
        xc, s["gate"], s["pp"] = _ple_fwd(xc, s["hn"], s["p"], wpg, wpp, l)
        saved.append(s)

    dx, loss_blk = _loss_fwd_bwd(xc, tgt)
    loss = lax.psum(loss_blk[0, 0], ("x", "y", "c"))

    gw = {name: [None] * NL for name in BIG}
    gs = {name: [None] * NL for name in SMALL if name != "rel_bias"}
    dt_a, dt_b = [], []
    for l in reversed(range(NL)):
        s = saved[l]
        dx, gs["norm_ple"][l], dwg, dwp = _ple_bwd(dx, s["gate"], s["pp"], s["hn"], s["p"], s["x3"],
                                                   norm_ple[l:l + 1], wpg, l)
        gw["w_ple_gate"][l] = dwg.reshape(N_CHIPS, -1, D)
        gw["w_ple_proj"][l] = dwp.reshape(dwp.shape[0], N_CHIPS, -1).transpose(1, 0, 2)
        dx, gs["norm_ffn2"][l], gw["ffn2_w_in"][l], gw["ffn2_w_out"][l] = _ffn_backward(
            dx, s["ffn2"], norm_ffn2[l:l + 1], win2, wout2, l)
        do, dd = _mix_bwd_in(dx, wo, l, [s["o_a"], s["o_b"]])
        hwa = s["o_a"].shape[1]
        gw["w_o"][l] = jnp.concatenate([
            _mm_tn("dw_o", o_, o_.shape[1], dx, D, 1, False, False, 1.0).reshape(-1, D // N_CHIPS, D)
            for o_ in (s["o_a"], s["o_b"])], axis=0)
        dqa, dka, dva, dt, _ = _attn_bwd(s["qkv_a"], bias_a, no_sink, dils, pad_a, do, s["lse_a"], dd, 0)
        dt_a.append(dt)
        dqb, dkb, dvb, dt, dsink = _attn_bwd(s["qkv_b"], bias_b, sink_b[l], (1,), pad_b, do, s["lse_b"], dd,
                                             hwa // LANES)
        dt_b.append(dt)
        gs["sink_b"][l] = dsink[:, 0, ::HEAD_DIM].reshape(-1)
        draw, dgains = _qkv_post_bwd(s["raw"], gains_row(l), (dqa, dka, dva), (dqb, dkb, dvb))
        dgv = dgains.reshape(-1, HEAD_DIM)
        gs["q_norm_a"][l] = dgv[:2 * n_a].sum(0)
        gs["k_norm_a"][l] = dgv[2 * n_a:4 * n_a].sum(0)
        gs["q_norm_b"][l] = dgv[6 * n_a:6 * n_a + 2 * n_b].sum(0)
        gs["k_norm_b"][l] = dgv[6 * n_a + 2 * n_b:6 * n_a + 2 * n_b + 2 * n_kv].sum(0)
        dwqkv = _mm_tn("dw_qkv", s["h2"], D, draw, QW, 1, False, False, 1.0)
        gw["w_qkv"][l] = dwqkv.reshape(D, N_CHIPS, -1).transpose(1, 0, 2)
        dx, gs["norm_mix"][l] = _bwd_into_norm("qkv_bwd_b", draw, QW, wqkv4, l, 1, s["x1"], norm_mix[l:l + 1], dx)
        dx, gs["norm_ffn1"][l], gw["ffn1_w_in"][l], gw["ffn1_w_out"][l] = _ffn_backward(
            dx, s["ffn1"], norm_ffn1[l:l + 1], win1, wout1, l)
    grad_x = dx.reshape(x.shape)
    d_rel_bias = (_bias_grad(dt_a, bmaps_a, 0) + _bias_grad(dt_b, bmaps_b, n_heads))[:, :rel_bias.shape[1]]

    items = [gw[name][l] for name in BIG for l in range(NL)]
    got = _rs_pair_exchange(items)
    part = [_rs_add_pair(g_, r_) for g_, r_ in zip(items, got)]
    recv = _rs_chip_exchange(part)
    g_full = _rs_pair_share([_rs_add_chips(p_, r_) for p_, r_ in zip(part, recv)])

    out = {}
    for i, name in enumerate(BIG):
        res = None
        for l in range(NL):
            res = _adamw_layer(given[name], g_full[i * NL + l], given["m_" + name], given["v_" + name], l, res)
        out[name] = res

    small_g = [d_rel_bias] + [jnp.stack([t.reshape(-1) for t in gs[name]]) for name in SMALL[1:]]
    g_sum = _allreduce_small(_pack_small(small_g))
    res = _ew("adamw_small", _adamw_fn,
              [_pack_small([given[n] for n in SMALL]), g_sum, _pack_small([given["m_" + n] for n in SMALL]),
               _pack_small([given["v_" + n] for n in SMALL])], [(LANES, F32)] * 4)
    like = [given[n] for n in SMALL]
    unpacked = [_unpack_small(r, like) for r in res]
    for i, name in enumerate(SMALL):
        out[name] = [u[i] for u in unpacked]

    return (loss, grad_x, *[out[n][0] for n in WEIGHTS], *[out[n][1] for n in WEIGHTS],
            *[out[n][2] for n in WEIGHTS], *[out[n][3] for n in WEIGHTS])
```

```python
import functools
import math

import numpy as np
import jax
import jax.numpy as jnp
from jax import lax
from jax.experimental import pallas as pl
from jax.experimental.pallas import tpu as pltpu

F32 = jnp.float32
BF16 = jnp.bfloat16
MESH = pl.DeviceIdType.MESH

HEAD_DIM = 64
LANES = 128
QBLOCK = 128
FWD_UNROLL, BWD_UNROLL = 4, 4
N_BUCKETS = 32
MAX_DISTANCE = 1024
DILATED_CONFIGS = ((128, 1), (512, 4), (2048, 16))
SWA_RADIUS = 128
GROUP_B = 4
EPS = 1e-6
NEG = -1e30
Q_SCALE = HEAD_DIM ** -0.5
ADAM_LR, ADAM_B1, ADAM_B2, ADAM_EPS, ADAM_WD, ADAM_STEP = 0.001, 0.9, 0.999, 1e-08, 0.01, 10
VMEM_LIMIT = 56 * 2 ** 20
N_CHIPS = 4
N_DEV = 8

BIG = ("ffn1_w_in", "ffn1_w_out", "w_qkv", "w_o", "ffn2_w_in", "ffn2_w_out", "w_ple_gate", "w_ple_proj")
SMALL = ("rel_bias", "norm_ffn1", "norm_mix", "q_norm_a", "k_norm_a", "q_norm_b", "k_norm_b", "sink_b",
         "norm_ffn2", "norm_ple")
WEIGHTS = ("rel_bias", "norm_ffn1", "ffn1_w_in", "ffn1_w_out", "norm_mix", "w_qkv", "q_norm_a", "k_norm_a",
           "q_norm_b", "k_norm_b", "sink_b", "w_o", "norm_ffn2", "ffn2_w_in", "ffn2_w_out", "norm_ple",
           "w_ple_gate", "w_ple_proj")


def _params(n_grid):
    return pltpu.CompilerParams(dimension_semantics=("arbitrary",) * n_grid, vmem_limit_bytes=VMEM_LIMIT)


def _pick_tm(rows, cap):
    t = (min(cap, rows) // 16) * 16
    while t >= 16:
        if rows % t == 0:
            return t
        t -= 16
    return rows


def _dot(a, b):
    return jnp.dot(a, b, preferred_element_type=F32)


def _dot_nt(a, b):
    return lax.dot_general(a, b, (((1,), (1,)), ((), ())), preferred_element_type=F32)


def _dot_tn(a, b):
    return lax.dot_general(a, b, (((0,), (0,)), ((), ())), preferred_element_type=F32)


def _sigmoid(z):
    return 1.0 / (1.0 + jnp.exp(-z))


def _lo_lanes(shape):
    return lax.broadcasted_iota(jnp.int32, shape, len(shape) - 1) % LANES < HEAD_DIM


def _seg_sum(blk):
    lo = _lo_lanes(blk.shape)
    s_lo = jnp.sum(jnp.where(lo, blk, 0.0), axis=1, keepdims=True)
    s_hi = jnp.sum(jnp.where(lo, 0.0, blk), axis=1, keepdims=True)
    return jnp.where(lo, s_lo, s_hi)


def _rms_bwd_tile(x, g, dh):
    r = lax.rsqrt(jnp.mean(x * x, axis=-1, keepdims=True) + EPS)
    xh = x * r
    dyg = dh * g
    dx = r * (dyg - xh * jnp.mean(dyg * xh, axis=-1, keepdims=True))
    return dx, jnp.sum(dh * xh, axis=0, keepdims=True)


def _ew(name, fn, ins, out_defs, cap=512):
    rows = ins[0].shape[0]
    tm = _pick_tm(rows, cap)
    n_in = len(ins)

    def body(*refs):
        vals = fn(*[r[...] for r in refs[:n_in]])
        if not isinstance(vals, tuple):
            vals = (vals,)
        for r, v in zip(refs[n_in:], vals):
            r[...] = v.astype(r.dtype)

    return pl.pallas_call(
        body, name=name, grid=(rows // tm,),
        in_specs=[pl.BlockSpec((tm, a.shape[1]), lambda i: (i, 0)) for a in ins],
        out_specs=[pl.BlockSpec((tm, c), lambda i: (i, 0)) for c, _ in out_defs],
        out_shape=[jax.ShapeDtypeStruct((rows, c), dt) for c, dt in out_defs],
        compiler_params=_params(1))(*ins)


def _rms_fwd(x, g):
    T, D = x.shape
    tm = _pick_tm(T, 512)

    def body(x_ref, g_ref, h_ref):
        xv = x_ref[...]
        r = lax.rsqrt(jnp.mean(xv * xv, axis=-1, keepdims=True) + EPS)
        h_ref[...] = (xv * r * g_ref[...]).astype(BF16)

    return pl.pallas_call(
        body, name="rms_fwd", grid=(T // tm,),
        in_specs=[pl.BlockSpec((tm, D), lambda i: (i, 0)), pl.BlockSpec((1, D), lambda i: (0, 0))],
        out_specs=pl.BlockSpec((tm, D), lambda i: (i, 0)),
        out_shape=jax.ShapeDtypeStruct((T, D), BF16), compiler_params=_params(1))(x, g)


def _ffn_up(h, win, l):
    T, D = h.shape
    wc = win.shape[3]
    tm = _pick_tm(T, 512)

    def body(h_ref, wg_ref, wu_ref, gate_ref, up_ref, act_ref):
        hv = h_ref[...]
        gte = _dot(hv, wg_ref[...])
        u = _dot(hv, wu_ref[...])
        gate_ref[...] = gte.astype(BF16)
        up_ref[...] = u.astype(BF16)
        act_ref[...] = (gte * _sigmoid(gte) * u).astype(BF16)

    out = jax.ShapeDtypeStruct((T, 2 * wc), BF16)
    ospec = pl.BlockSpec((tm, wc), lambda j, i: (i, j))
    return pl.pallas_call(
        body, name="ffn_up", grid=(2, T // tm),
        in_specs=[pl.BlockSpec((tm, D), lambda j, i: (i, 0)),
                  pl.BlockSpec((None, None, D, wc), lambda j, i: (l, j, 0, 0)),
                  pl.BlockSpec((None, None, D, wc), lambda j, i: (l, j + 2, 0, 0))],
        out_specs=[ospec, ospec, ospec], out_shape=[out, out, out], compiler_params=_params(2))(h, win, win)


def _mm_res(name, res, a_list, w, l, scale):
    T, N = res.shape
    n = len(a_list)
    widths = [a.shape[1] for a in a_list]
    tm = _pick_tm(T, 512)

    def body(*refs):
        r_ref, a_refs, w_refs, o_ref = refs[0], refs[1:1 + n], refs[1 + n:1 + 2 * n], refs[1 + 2 * n]
        acc = _dot(a_refs[0][...].astype(BF16), w_refs[0][...])
        for a_ref, w_ref in zip(a_refs[1:], w_refs[1:]):
            acc = acc + _dot(a_ref[...].astype(BF16), w_ref[...])
        o_ref[...] = r_ref[...] + scale * acc

    w_specs, off = [], 0
    for k in widths:
        w_specs.append(pl.BlockSpec((None, k, N), lambda i, blk=off // k: (l, blk, 0)))
        off += k
    return pl.pallas_call(
        body, name=name, grid=(T // tm,),
        in_specs=[pl.BlockSpec((tm, N), lambda i: (i, 0))]
        + [pl.BlockSpec((tm, k), lambda i: (i, 0)) for k in widths] + w_specs,
        out_specs=pl.BlockSpec((tm, N), lambda i: (i, 0)),
        out_shape=jax.ShapeDtypeStruct((T, N), F32), compiler_params=_params(1))(res, *a_list, *([w] * n))


def _mm_plain(name, a, w, l):
    T, K = a.shape
    N = w.shape[2]
    tm = _pick_tm(T, 512)

    def body(a_ref, w_ref, o_ref):
        o_ref[...] = _dot(a_ref[...], w_ref[...])

    return pl.pallas_call(
        body, name=name, grid=(T // tm,),
        in_specs=[pl.BlockSpec((tm, K), lambda i: (i, 0)), pl.BlockSpec((None, K, N), lambda i: (l, 0, 0))],
        out_specs=pl.BlockSpec((tm, N), lambda i: (i, 0)),
        out_shape=jax.ShapeDtypeStruct((T, N), F32), compiler_params=_params(1))(a, w)


def _mm_tn(name, a, a_w, b, b_w, n_slots, a_by_slot, b_by_slot, scale, tm_cap=512):
    T = a.shape[0]
    tm = _pick_tm(T, tm_cap)
    nt = T // tm

    def body(a_ref, b_ref, o_ref, acc):
        i = pl.program_id(1)

        @pl.when(i == 0)
        def _():
            acc[...] = jnp.zeros_like(acc)

        acc[...] += _dot_tn(a_ref[...].astype(BF16), b_ref[...].astype(BF16))

        @pl.when(i == nt - 1)
        def _():
            o_ref[...] = (acc[...] * scale).astype(BF16)

    return pl.pallas_call(
        body, name=name, grid=(n_slots, nt),
        in_specs=[pl.BlockSpec((tm, a_w), (lambda s, i: (i, s)) if a_by_slot else (lambda s, i: (i, 0))),
                  pl.BlockSpec((tm, b_w), (lambda s, i: (i, s)) if b_by_slot else (lambda s, i: (i, 0)))],
        out_specs=pl.BlockSpec((None, a_w, b_w), lambda s, i: (s, 0, 0)),
        out_shape=jax.ShapeDtypeStruct((n_slots, a_w, b_w), BF16),
        scratch_shapes=[pltpu.VMEM((a_w, b_w), F32)], compiler_params=_params(2))(a, b)


def _ffn_bwd_a(dx, wout, l, gate, up):
    T, D = dx.shape
    F = gate.shape[1]
    tm = _pick_tm(T, 256)

    def body(dx_ref, w_ref, g_ref, u_ref, o_ref):
        dact = 0.5 * _dot_nt(dx_ref[...].astype(BF16), w_ref[...])
        gte = g_ref[...].astype(F32)
        u = u_ref[...].astype(F32)
        sg = _sigmoid(gte)
        silu = gte * sg
        o_ref[:, :F] = (dact * u * (sg + silu * (1.0 - sg))).astype(BF16)
        o_ref[:, F:] = (dact * silu).astype(BF16)

    return pl.pallas_call(
        body, name="ffn_bwd_a", grid=(T // tm,),
        in_specs=[pl.BlockSpec((tm, D), lambda i: (i, 0)), pl.BlockSpec((None, F, D), lambda i: (l, 0, 0)),
                  pl.BlockSpec((tm, F), lambda i: (i, 0)), pl.BlockSpec((tm, F), lambda i: (i, 0))],
        out_specs=pl.BlockSpec((tm, 2 * F), lambda i: (i, 0)),
        out_shape=jax.ShapeDtypeStruct((T, 2 * F), BF16), compiler_params=_params(1))(dx, wout, gate, up)


def _bwd_into_norm(name, d, d_w, w, l, n_slots, x, g, dx_in):
    T, D = x.shape
    tm = _pick_tm(T, 256)

    def body(*refs):
        d_refs, w_refs = refs[:n_slots], refs[n_slots:2 * n_slots]
        x_ref, g_ref, dxi_ref, dx_ref, dg_ref = refs[2 * n_slots:]
        dh = _dot_nt(d_refs[0][...], w_refs[0][...])
        for s in range(1, n_slots):
            dh = dh + _dot_nt(d_refs[s][...], w_refs[s][...])
        dxn, dg = _rms_bwd_tile(x_ref[...], g_ref[...], dh)
        dx_ref[...] = dxi_ref[...] + dxn

        @pl.when(pl.program_id(0) == 0)
        def _():
            dg_ref[...] = jnp.zeros_like(dg_ref)

        dg_ref[...] += dg

    row = pl.BlockSpec((tm, D), lambda i: (i, 0))
    vec = pl.BlockSpec((1, D), lambda i: (0, 0))
    return pl.pallas_call(
        body, name=name, grid=(T // tm,),
        in_specs=[pl.BlockSpec((tm, d_w), lambda i, s=s: (i, s)) for s in range(n_slots)]
        + [pl.BlockSpec((None, None, D, d_w), lambda i, s=s: (l, s, 0, 0)) for s in range(n_slots)]
        + [row, vec, row],
        out_specs=[row, vec],
        out_shape=[jax.ShapeDtypeStruct((T, D), F32), jax.ShapeDtypeStruct((1, D), F32)],
        compiler_params=_params(1))(*([d] * n_slots + [w] * n_slots + [x, g, dx_in]))


def _ple_fwd(x, hn, p, wg, wp, l):
    T, D = x.shape
    P = p.shape[1]
    tm = _pick_tm(T, 256)

    def body(x_ref, hn_ref, p_ref, wg_ref, wp_ref, xo_ref, gate_ref, pp_ref):
        gate = _sigmoid(_dot(hn_ref[...], wg_ref[...]))
        pp = _dot(p_ref[...].astype(BF16), wp_ref[...])
        gate_ref[...] = gate
        pp_ref[...] = pp
        xo_ref[...] = x_ref[...] + gate * pp

    row = pl.BlockSpec((tm, D), lambda i: (i, 0))
    out = jax.ShapeDtypeStruct((T, D), F32)
    return pl.pallas_call(
        body, name="ple_fwd", grid=(T // tm,),
        in_specs=[row, row, pl.BlockSpec((tm, P), lambda i: (i, 0)),
                  pl.BlockSpec((None, D, D), lambda i: (l, 0, 0)), pl.BlockSpec((None, P, D), lambda i: (l, 0, 0))],
        out_specs=[row, row, row], out_shape=[out, out, out], compiler_params=_params(1))(x, hn, p, wg, wp)


def _ple_bwd(dx, gate, pp, hn, p, x, g, wg, l):
    T, D = x.shape
    P = p.shape[1]
    tm = _pick_tm(T, 256)
    nt = T // tm

    def body(dx_ref, gate_ref, pp_ref, hn_ref, p_ref, x_ref, g_ref, wg_ref,
             dxo_ref, dg_ref, dwg_ref, dwp_ref, acc_g, acc_p):
        i = pl.program_id(0)

        @pl.when(i == 0)
        def _():
            acc_g[...] = jnp.zeros_like(acc_g)
            acc_p[...] = jnp.zeros_like(acc_p)
            dg_ref[...] = jnp.zeros_like(dg_ref)

        dxv = dx_ref[...]
        gate = gate_ref[...]
        dz = (dxv * pp_ref[...] * gate * (1.0 - gate)).astype(BF16)
        dpp = (dxv * gate).astype(BF16)
        acc_g[...] += _dot_tn(hn_ref[...], dz)
        acc_p[...] += _dot_tn(p_ref[...].astype(BF16), dpp)
        dxn, dg = _rms_bwd_tile(x_ref[...], g_ref[...], _dot_nt(dz, wg_ref[...]))
        dxo_ref[...] = dxv + dxn
        dg_ref[...] += dg

        @pl.when(i == nt - 1)
        def _():
            dwg_ref[...] = acc_g[...].astype(BF16)
            dwp_ref[...] = acc_p[...].astype(BF16)

    row = pl.BlockSpec((tm, D), lambda i: (i, 0))
    vec = pl.BlockSpec((1, D), lambda i: (0, 0))
    return pl.pallas_call(
        body, name="ple_bwd", grid=(nt,),
        in_specs=[row, row, row, row, pl.BlockSpec((tm, P), lambda i: (i, 0)), row, vec,
                  pl.BlockSpec((None, D, D), lambda i: (l, 0, 0))],
        out_specs=[row, vec, pl.BlockSpec((D, D), lambda i: (0, 0)), pl.BlockSpec((P, D), lambda i: (0, 0))],
        out_shape=[jax.ShapeDtypeStruct((T, D), F32), jax.ShapeDtypeStruct((1, D), F32),
                   jax.ShapeDtypeStruct((D, D), BF16), jax.ShapeDtypeStruct((P, D), BF16)],
        scratch_shapes=[pltpu.VMEM((D, D), F32), pltpu.VMEM((P, D), F32)],
        compiler_params=_params(1))(dx, gate, pp, hn, p, x, g, wg)


def _loss_fwd_bwd(y, tgt):
    T, D = y.shape
    tm = _pick_tm(T, 512)

    def body(y_ref, t_ref, dy_ref, loss_ref):
        e = y_ref[...] - t_ref[...]
        dy_ref[...] = e / D

        @pl.when(pl.program_id(0) == 0)
        def _():
            loss_ref[...] = jnp.zeros_like(loss_ref)

        loss_ref[...] += 0.5 * jnp.sum(jnp.mean(e * e, axis=-1, keepdims=True), axis=0, keepdims=True)

    row = pl.BlockSpec((tm, D), lambda i: (i, 0))
    return pl.pallas_call(
        body, name="loss", grid=(T // tm,), in_specs=[row, row],
        out_specs=[row, pl.BlockSpec((8, LANES), lambda i: (0, 0))],
        out_shape=[jax.ShapeDtypeStruct((T, D), F32), jax.ShapeDtypeStruct((8, LANES), F32)],
        compiler_params=_params(1))(y, tgt)


def _qkv_layout(D):
    n_a = D // (2 * LANES)
    n_b = D // (2 * LANES)
    n_kv = max(1, (2 * n_b) // GROUP_B) * HEAD_DIM // LANES
    return n_a, n_b, n_kv


def _dup_half(xv, half):
    rolled = pltpu.roll(xv, HEAD_DIM, 1)
    lo = _lo_lanes(xv.shape)
    return jnp.where(lo, xv, rolled) if half == 0 else jnp.where(lo, rolled, xv)


def _qkv_post(raw, gains):
    T, W = raw.shape
    n_a, n_b, n_kv = _qkv_layout(W * 4 // 9)
    tm = _pick_tm(T, 256)
    o_qb = 3 * n_a

    def norm(xv, gv, scale):
        ms = _seg_sum(xv * xv) * (1.0 / HEAD_DIM)
        return xv * lax.rsqrt(ms + EPS) * gv * scale

    def body(raw_ref, g_ref, a_ref, b_ref):
        def blk(cb):
            return raw_ref[:, cb * LANES:(cb + 1) * LANES]

        def gn(cb):
            return g_ref[:, cb * LANES:(cb + 1) * LANES]

        for cb in range(n_a):
            a_ref[:, cb * LANES:(cb + 1) * LANES] = norm(blk(cb), gn(cb), Q_SCALE)
            cbk = n_a + cb
            a_ref[:, cbk * LANES:(cbk + 1) * LANES] = norm(blk(cbk), gn(cbk), 1.0)
            cbv = 2 * n_a + cb
            a_ref[:, cbv * LANES:(cbv + 1) * LANES] = blk(cbv)
        for cb in range(n_b):
            src = o_qb + cb
            b_ref[:, cb * LANES:(cb + 1) * LANES] = norm(blk(src), gn(src), Q_SCALE)
        for e in range(n_b):
            kvh = (2 * e) // GROUP_B
            ck = o_qb + n_b + kvh // 2
            cv = ck + n_kv
            kn = norm(blk(ck), gn(ck), 1.0)
            b_ref[:, (n_b + e) * LANES:(n_b + e + 1) * LANES] = _dup_half(kn, kvh % 2)
            b_ref[:, (2 * n_b + e) * LANES:(2 * n_b + e + 1) * LANES] = _dup_half(blk(cv), kvh % 2)

    wa, wb = 3 * n_a * LANES, 3 * n_b * LANES
    return pl.pallas_call(
        body, name="qkv_post", grid=(T // tm,),
        in_specs=[pl.BlockSpec((tm, W), lambda i: (i, 0)), pl.BlockSpec((1, W), lambda i: (0, 0))],
        out_specs=[pl.BlockSpec((tm, wa), lambda i: (i, 0)), pl.BlockSpec((tm, wb), lambda i: (i, 0))],
        out_shape=[jax.ShapeDtypeStruct((T, wa), F32), jax.ShapeDtypeStruct((T, wb), F32)],
        compiler_params=_params(1))(raw, gains)


def _qkv_post_bwd(raw, gains, d_a, d_b):
    T, W = raw.shape
    n_a, n_b, n_kv = _qkv_layout(W * 4 // 9)
    tm = _pick_tm(T, 256)
    o_qb = 3 * n_a

    def body(raw_ref, g_ref, daq, dak, dav, dbq, dbk, dbv, o_ref, dg_ref):
        @pl.when(pl.program_id(0) == 0)
        def _():
            dg_ref[...] = jnp.zeros_like(dg_ref)

        def cols(ref, cb):
            return ref[:, cb * LANES:(cb + 1) * LANES]

        def norm_bwd(cb, dy, scale):
            xv = cols(raw_ref, cb)
            gv = cols(g_ref, cb)
            r = lax.rsqrt(_seg_sum(xv * xv) * (1.0 / HEAD_DIM) + EPS)
            xh = xv * r
            dys = dy * scale
            dyg = dys * gv
            dxv = r * (dyg - xh * (_seg_sum(dyg * xh) * (1.0 / HEAD_DIM)))
            o_ref[:, cb * LANES:(cb + 1) * LANES] = dxv.astype(BF16)
            dg_ref[:, cb * LANES:(cb + 1) * LANES] += jnp.sum(dys * xh, axis=0, keepdims=True)

        def fold(ref, kv_blk):
            halves = []
            for half in range(2):
                kvh = 2 * kv_blk + half
                blocks = [e for e in range(n_b) if (2 * e) // GROUP_B == kvh]
                s = cols(ref, blocks[0])
                for e in blocks[1:]:
                    s = s + cols(ref, e)
                halves.append(s + pltpu.roll(s, HEAD_DIM, 1))
            return jnp.where(_lo_lanes(halves[0].shape), halves[0], halves[1])

        for cb in range(n_a):
            norm_bwd(cb, cols(daq, cb), Q_SCALE)
            norm_bwd(n_a + cb, cols(dak, cb), 1.0)
            cbv = 2 * n_a + cb
            o_ref[:, cbv * LANES:(cbv + 1) * LANES] = cols(dav, cb).astype(BF16)
        for cb in range(n_b):
            norm_bwd(o_qb + cb, cols(dbq, cb), Q_SCALE)
        for kb in range(n_kv):
            ck = o_qb + n_b + kb
            cv = ck + n_kv
            norm_bwd(ck, fold(dbk, kb), 1.0)
            o_ref[:, cv * LANES:(cv + 1) * LANES] = fold(dbv, kb).astype(BF16)

    hw_a, hw_b = n_a * LANES, n_b * LANES
    return pl.pallas_call(
        body, name="qkv_post_bwd", grid=(T // tm,),
        in_specs=[pl.BlockSpec((tm, W), lambda i: (i, 0)), pl.BlockSpec((1, W), lambda i: (0, 0))]
        + [pl.BlockSpec((tm, hw_a), lambda i: (i, 0))] * 3 + [pl.BlockSpec((tm, hw_b), lambda i: (i, 0))] * 3,
        out_specs=[pl.BlockSpec((tm, W), lambda i: (i, 0)), pl.BlockSpec((1, W), lambda i: (0, 0))],
        out_shape=[jax.ShapeDtypeStruct((T, W), BF16), jax.ShapeDtypeStruct((1, W), F32)],
        compiler_params=_params(1))(raw, gains, *d_a, *d_b)


def _t5_bucket_np(rel):
    half = N_BUCKETS // 2
    max_exact = half // 2
    ret = np.where(rel > 0, half, 0)
    n = np.abs(rel)
    nf = np.maximum(n, 1).astype(np.float32)
    large = max_exact + (np.log(nf / np.float32(max_exact)) / np.float32(math.log(MAX_DISTANCE / max_exact))
                         * np.float32(half - max_exact)).astype(np.int32)
    large = np.minimum(large, half - 1)
    return ret + np.where(n < max_exact, n, large)


def _window_pad(radius):
    assert radius <= QBLOCK
    return HEAD_DIM if radius <= HEAD_DIM else QBLOCK


def _bucket_maps(configs):
    pad = _window_pad(configs[0][0])
    q = np.arange(QBLOCK)[:, None]
    kk = np.arange(QBLOCK + 2 * pad)[None, :]
    rel = kk - pad - q
    maps = [np.where(np.abs(rel) <= radius, _t5_bucket_np(rel * dil), -1) for radius, dil in configs]
    return np.stack(maps).astype(np.int32)


def _bias_build(rel_bias, bmaps, col0):
    n_sets, _, W = bmaps.shape
    n_heads = rel_bias.shape[1] // 2

    def body(rb_ref, bm_ref, o_ref):
        h = pl.program_id(1)
        bm = bm_ref[...]

        def step(n, acc):
            return jnp.where(bm == n, rb_ref[n, col0 + h], acc)

        o_ref[...] = lax.fori_loop(0, N_BUCKETS, step, jnp.where(bm < 0, NEG, 0.0).astype(F32))

    return pl.pallas_call(
        body, name="bias_build", grid=(n_sets, n_heads),
        in_specs=[pl.BlockSpec(memory_space=pltpu.SMEM), pl.BlockSpec((None, QBLOCK, W), lambda s, h: (s, 0, 0))],
        out_specs=pl.BlockSpec((None, None, QBLOCK, W), lambda s, h: (s, h, 0, 0)),
        out_shape=jax.ShapeDtypeStruct((n_sets, n_heads, QBLOCK, W), F32),
        compiler_params=_params(2))(rel_bias, bmaps)


def _bias_grad(dtiles, bmaps, col0):
    n_sets, _, W = bmaps.shape
    n_heads = dtiles[0].shape[1]
    n_l = len(dtiles)

    def body(*refs):
        bm_ref, o_ref = refs[0], refs[1 + n_l]
        s, h = pl.program_id(0), pl.program_id(1)

        @pl.when((s == 0) & (h == 0))
        def _():
            o_ref[...] = jnp.zeros_like(o_ref)

        d = refs[1][...]
        for r in refs[2:1 + n_l]:
            d = d + r[...]
        bm = bm_ref[...]
        rows = lax.broadcasted_iota(jnp.int32, o_ref.shape, 0)
        lanes = lax.broadcasted_iota(jnp.int32, o_ref.shape, 1)

        def step(n, acc):
            val = jnp.sum(jnp.where(bm == n, d, 0.0))
            return acc + jnp.where((rows == n) & (lanes == col0 + h), val, 0.0)

        o_ref[...] += lax.fori_loop(0, N_BUCKETS, step, jnp.zeros(o_ref.shape, F32))

    tile = pl.BlockSpec((None, None, QBLOCK, W), lambda s, h: (s, h, 0, 0))
    return pl.pallas_call(
        body, name="bias_grad", grid=(n_sets, n_heads),
        in_specs=[pl.BlockSpec((None, QBLOCK, W), lambda s, h: (s, 0, 0))] + [tile] * n_l,
        out_specs=pl.BlockSpec((N_BUCKETS, LANES), lambda s, h: (0, 0)),
        out_shape=jax.ShapeDtypeStruct((N_BUCKETS, LANES), F32), compiler_params=_params(2))(bmaps, *dtiles)


def _rows(l_start, n, d, r):
    if d == 1:
        return pl.ds(pl.multiple_of(l_start, 8), n)
    return pl.ds(l_start * d + r, n, stride=d)


def _block_geometry(b, nb_sub, pad):
    r, lb = b // nb_sub, b % nb_sub
    l0 = lb * QBLOCK
    lp = jnp.maximum(l0 - pad, 0)
    ln = jnp.minimum(l0 + QBLOCK, nb_sub * QBLOCK - pad)
    col = lax.broadcasted_iota(jnp.int32, (QBLOCK, QBLOCK + 2 * pad), 1)
    valid = ((col >= pad) | (lb != 0)) & ((col < pad + QBLOCK) | (lb != nb_sub - 1))
    return r, l0, lp, ln, valid


def _window(ref, l0, lp, ln, pad, d, r):
    return jnp.concatenate([ref[_rows(lp, pad, d, r), :], ref[_rows(l0, QBLOCK, d, r), :],
                            ref[_rows(ln, pad, d, r), :]], axis=0)


def _attn_fwd(qkv, bias, sink, dils, pad):
    T = qkv.shape[0]
    hw = qkv.shape[1] // 3
    ng = hw // LANES
    n_br = len(dils)
    n_blocks = T // QBLOCK
    W = QBLOCK + 2 * pad
    chunk = 256

    def body(sink_ref, q_ref, k_ref, v_ref, bias_ref, o_ref, lse_ref, *scratch):
        g = pl.program_id(0)
        lo = _lo_lanes((QBLOCK, LANES))
        for c, d in enumerate(dils):
            nb_sub = n_blocks // d
            o_dst = scratch[0].at[c] if n_br > 1 else o_ref
            l_dst = scratch[1].at[c] if n_br > 1 else lse_ref

            def block(b, carry, c=c, d=d, nb_sub=nb_sub, o_dst=o_dst, l_dst=l_dst):
                r, l0, lp, ln, valid = _block_geometry(b, nb_sub, pad)
                q = q_ref[_rows(l0, QBLOCK, d, r), :].astype(BF16)
                k = _window(k_ref, l0, lp, ln, pad, d, r).astype(BF16)
                v = _window(v_ref, l0, lp, ln, pad, d, r).astype(BF16)
                outs, lses = [], []
                for hh in range(2):
                    hm = lo if hh == 0 else jnp.logical_not(lo)
                    s = _dot_nt(jnp.where(hm, q, jnp.zeros_like(q)), k) + bias_ref[c, hh]
                    s = jnp.where(valid, s, NEG)
                    snk = sink_ref[2 * g + hh]
                    m = jnp.maximum(jnp.max(s, axis=1, keepdims=True), snk)
                    p = jnp.exp(s - m)
                    den = jnp.sum(p, axis=1, keepdims=True) + jnp.exp(snk - m)
                    outs.append(_dot(p.astype(BF16), v) / den)
                    lses.append(m + jnp.log(den))
                o_dst[_rows(l0, QBLOCK, d, r), :] = jnp.where(lo, outs[0], outs[1])
                l_dst[_rows(l0, QBLOCK, d, r), :] = jnp.where(lo, lses[0], lses[1])
                return carry

            lax.fori_loop(0, n_blocks, block, 0, unroll=FWD_UNROLL)

        if n_br > 1:
            def merge(i, carry):
                rs = pl.ds(pl.multiple_of(i * chunk, chunk), chunk)
                ls = [scratch[1][c, rs, :] for c in range(n_br)]
                m = ls[0]
                for t in ls[1:]:
                    m = jnp.maximum(m, t)
                ws = [jnp.exp(t - m) for t in ls]
                z = ws[0]
                acc = ws[0] * scratch[0][0, rs, :]
                for c in range(1, n_br):
                    z = z + ws[c]
                    acc = acc + ws[c] * scratch[0][c, rs, :]
                o_ref[rs, :] = acc / z
                lse_ref[rs, :] = m + jnp.log(z)
                return carry

            lax.fori_loop(0, T // chunk, merge, 0)

    def col(base):
        return pl.BlockSpec((T, LANES), lambda g: (0, base + g))

    out = jax.ShapeDtypeStruct((T, hw), F32)
    scratch = [pltpu.VMEM((n_br, T, LANES), F32)] * 2 if n_br > 1 else []
    return pl.pallas_call(
        body, name="attn_fwd", grid=(ng,),
        in_specs=[pl.BlockSpec(memory_space=pltpu.SMEM), col(0), col(ng), col(2 * ng),
                  pl.BlockSpec((n_br, 2, QBLOCK, W), lambda g: (0, g, 0, 0))],
        out_specs=[col(0), col(0)], out_shape=[out, out], scratch_shapes=scratch,
        compiler_params=_params(1))(sink, qkv, qkv, qkv, bias)


def _attn_bwd(qkv, bias, sink, dils, pad, do, lse, dd, col_base):
    T = qkv.shape[0]
    hw = qkv.shape[1] // 3
    ng = hw // LANES
    n_br = len(dils)
    n_blocks = T // QBLOCK
    W = QBLOCK + 2 * pad

    def body(sink_ref, q_ref, k_ref, v_ref, bias_ref, do_ref, lse_ref, dd_ref,
             dq_ref, dk_ref, dv_ref, dt_ref, ds_ref):
        g = pl.program_id(0)
        dq_ref[...] = jnp.zeros_like(dq_ref)
        dk_ref[...] = jnp.zeros_like(dk_ref)
        dv_ref[...] = jnp.zeros_like(dv_ref)
        dt_ref[...] = jnp.zeros_like(dt_ref)
        ds_ref[...] = jnp.zeros_like(ds_ref)
        lo = _lo_lanes((QBLOCK, LANES))
        snk = jnp.where(lo, sink_ref[2 * g], sink_ref[2 * g + 1])
        for c, d in enumerate(dils):
            nb_sub = n_blocks // d

            def block(b, carry, c=c, d=d, nb_sub=nb_sub):
                r, l0, lp, ln, valid = _block_geometry(b, nb_sub, pad)
                rows_q = _rows(l0, QBLOCK, d, r)
                q = q_ref[rows_q, :].astype(BF16)
                k = _window(k_ref, l0, lp, ln, pad, d, r).astype(BF16)
                v = _window(v_ref, l0, lp, ln, pad, d, r).astype(BF16)
                dob = do_ref[rows_q, :].astype(BF16)
                lse_b = lse_ref[rows_q, :]
                dd_b = dd_ref[rows_q, :]
                dqs = []
                dkw = jnp.zeros((W, LANES), F32)
                dvw = jnp.zeros((W, LANES), F32)
                for hh in range(2):
                    hm = lo if hh == 0 else jnp.logical_not(lo)
                    qm = jnp.where(hm, q, jnp.zeros_like(q))
                    dom = jnp.where(hm, dob, jnp.zeros_like(dob))
                    s = _dot_nt(qm, k) + bias_ref[c, hh]
                    s = jnp.where(valid, s, NEG)
                    p = jnp.exp(s - lse_b[:, hh * HEAD_DIM:hh * HEAD_DIM + 1])
                    ds = p * (_dot_nt(dom, v) - dd_b[:, hh * HEAD_DIM:hh * HEAD_DIM + 1])
                    dsb = ds.astype(BF16)
                    dqs.append(_dot(dsb, k))
                    dkw = dkw + _dot_tn(dsb, qm)
                    dvw = dvw + _dot_tn(p.astype(BF16), dom)
                    dt_ref[c, hh] += ds
                dq_ref[rows_q, :] += jnp.where(lo, dqs[0], dqs[1])
                ds_ref[0:1, :] += jnp.sum(-jnp.exp(snk - lse_b) * dd_b, axis=0, keepdims=True)
                for part, (start, n) in zip((0, pad, pad + QBLOCK), ((lp, pad), (l0, QBLOCK), (ln, pad))):
                    dk_ref[_rows(start, n, d, r), :] += dkw[part:part + n]
                    dv_ref[_rows(start, n, d, r), :] += dvw[part:part + n]
                return carry

            lax.fori_loop(0, n_blocks, block, 0, unroll=BWD_UNROLL)

    def col(base):
        return pl.BlockSpec((T, LANES), lambda g: (0, base + g))

    tile =pl.BlockSpec((n_br, 2, QBLOCK, W), lambda g: (0, g, 0, 0))
    full = jax.ShapeDtypeStruct((T, hw), F32)
    return pl.pallas_call(
        body, name="attn_bwd", grid=(ng,),
        in_specs=[pl.BlockSpec(memory_space=pltpu.SMEM), col(0), col(ng), col(2 * ng), tile,
                  col(col_base), col(0), col(col_base)],
        out_specs=[col(0), col(0), col(0), tile, pl.BlockSpec((None, 8, LANES), lambda g: (g, 0, 0))],
        out_shape=[full, full, full, jax.ShapeDtypeStruct((n_br, 2 * ng, QBLOCK, W), F32),
                   jax.ShapeDtypeStruct((ng, 8, LANES), F32)],
        compiler_params=_params(1))(sink, qkv, qkv, qkv, bias, do, lse, dd)


def _mix_bwd_in(dx, wo, l, o_list):
    T, D = dx.shape
    widths = [o.shape[1] for o in o_list]
    hw = sum(widths)
    n = len(o_list)
    tm = _pick_tm(T, 256)

    def body(*refs):
        dx_ref, w_ref, o_refs, do_ref, dd_ref = refs[0], refs[1], refs[2:2 + n], refs[2 + n], refs[3 + n]
        dov = _dot_nt(dx_ref[...].astype(BF16), w_ref[...])
        do_ref[...] = dov
        off = 0
        for o_ref, k in zip(o_refs, widths):
            prod = dov[:, off:off + k] * o_ref[...]
            for cb in range(k // LANES):
                dd_ref[:, off + cb * LANES:off + (cb + 1) * LANES] = _seg_sum(prod[:, cb * LANES:(cb + 1) * LANES])
            off += k

    row = pl.BlockSpec((tm, hw), lambda i: (i, 0))
    out = jax.ShapeDtypeStruct((T, hw), F32)
    return pl.pallas_call(
        body, name="mix_bwd_in", grid=(T // tm,),
        in_specs=[pl.BlockSpec((tm, D), lambda i: (i, 0)), pl.BlockSpec((None, hw, D), lambda i: (l, 0, 0))]
        + [pl.BlockSpec((tm, k), lambda i: (i, 0)) for k in widths],
        out_specs=[row, row], out_shape=[out, out], compiler_params=_params(1))(dx, wo, *o_list)


def _mesh_pos():
    return lax.axis_index("x"), lax.axis_index("y"), lax.axis_index("c")


def _my_chip():
    return 2 * lax.axis_index("x") + lax.axis_index("y")


def _other_chips(x, y):
    return [(1 - x, y), (x, 1 - y), (1 - x, 1 - y)]


HBM_SPEC = pl.BlockSpec(memory_space=pltpu.HBM)


def _half_rows(rows, cc):
    hr = rows // 2
    return pl.ds(pl.multiple_of(cc * hr, 16), hr)


def _cast_into_slot(w):
    NL, R, C = w.shape
    tm = _pick_tm(R, 512)

    def body(w_ref, o_ref):
        o_ref[...] = w_ref[...].astype(BF16)

    return pl.pallas_call(
        body, name="cast_into_slot", grid=(NL, R // tm),
        in_specs=[pl.BlockSpec((None, tm, C), lambda l, i: (l, i, 0))],
        out_specs=pl.BlockSpec((None, None, tm, C), lambda l, i: (l, _my_chip(), i, 0)),
        out_shape=jax.ShapeDtypeStruct((NL, N_CHIPS, R, C), BF16), compiler_params=_params(2))(w)


def _allgather_weights(slotted):
    n = len(slotted)

    def body(*refs):
        bufs = refs[n:2 * n]
        ici_s, ici_r, d2d_s, d2d_r = refs[2 * n:]
        x, y, c = _mesh_pos()
        my = 2 * x + y
        chips = _other_chips(x, y)

        def part(w, slot, cc):
            return bufs[w].at[:, slot, _half_rows(slotted[w].shape[2], cc), :]

        def ici(k, w, slot):
            px, py = chips[k]
            return pltpu.make_async_remote_copy(
                src_ref=part(w, slot, c), dst_ref=part(w, slot, c),
                send_sem=ici_s.at[k * n + w], recv_sem=ici_r.at[k * n + w], device_id=(px, py, c), device_id_type=MESH)

        def passed(k, w, cc):
            px, py = chips[k]
            return pltpu.make_async_remote_copy(
                src_ref=part(w, 2 * px + py, cc), dst_ref=part(w, 2 * px + py, cc),
                send_sem=d2d_s.at[k * n + w], recv_sem=d2d_r.at[k * n + w], device_id=(x, y, 1 - c), device_id_type=MESH)

        for w in range(n):
            for k in range(3):
                ici(k, w, my).start()
        for w in range(n):
            for k, (px, py) in enumerate(chips):
                ici(k, w, 2 * px + py).wait_recv()
                passed(k, w, c).start()
        for w in range(n):
            for k in range(3):
                passed(k, w, 1 - c).wait_recv()
        for w in range(n):
            for k in range(3):
                ici(k, w, my).wait_send()
                passed(k, w, c).wait_send()

    return pl.pallas_call(
        body, name="allgather_weights",
        in_specs=[HBM_SPEC] * n, out_specs=[HBM_SPEC] * n,
        out_shape=[jax.ShapeDtypeStruct(s.shape, s.dtype) for s in slotted],
        input_output_aliases={w: w for w in range(n)},
        scratch_shapes=[pltpu.SemaphoreType.DMA((3 * n,)), pltpu.SemaphoreType.DMA((3 * n,)),
                        pltpu.SemaphoreType.DMA((3 * n,)), pltpu.SemaphoreType.DMA((3 * n,))],
    )(*slotted)


def _rs_pair_exchange(grads):
    n = len(grads)

    def body(*refs):
        ins, recv = refs[:n], refs[n:2 * n]
        send_s, recv_s = refs[2 * n:]
        x, y, c = _mesh_pos()
        remote = [pltpu.make_async_remote_copy(
            src_ref=ins[t].at[:, _half_rows(grads[t].shape[1], 1 - c), :], dst_ref=recv[t],
            send_sem=send_s.at[t], recv_sem=recv_s.at[t], device_id=(x, y, 1 - c), device_id_type=MESH)
            for t in range(n)]
        for cp in remote:
            cp.start()
        for cp in remote:
            cp.wait_recv()
        for cp in remote:
            cp.wait_send()

    return pl.pallas_call(
        body, name="rs_pair_exchange", in_specs=[HBM_SPEC] * n, out_specs=[HBM_SPEC] * n,
        out_shape=[jax.ShapeDtypeStruct((g.shape[0], g.shape[1] // 2, g.shape[2]), g.dtype) for g in grads],
        scratch_shapes=[pltpu.SemaphoreType.DMA((n,)), pltpu.SemaphoreType.DMA((n,))],
    )(*grads)


def _rs_add_pair(grad, recv):
    n_slot, hr, C = recv.shape
    tm = _pick_tm(hr, 256)
    nb = hr // tm

    def body(a_ref, b_ref, o_ref):
        o_ref[...] = (a_ref[...].astype(F32) + b_ref[...].astype(F32)).astype(BF16)

    blk = pl.BlockSpec((None, tm, C), lambda k, i: (k, i, 0))
    return pl.pallas_call(
        body, name="rs_add_pair", grid=(n_slot, nb),
        in_specs=[pl.BlockSpec((None, tm, C), lambda k, i: (k, lax.axis_index("c") * nb + i, 0)), blk],
        out_specs=blk, out_shape=jax.ShapeDtypeStruct(recv.shape, BF16), compiler_params=_params(2))(grad, recv)


def _rs_chip_exchange(parts):
    n = len(parts)

    def body(*refs):
        ins, recv = refs[:n], refs[n:2 * n]
        send_s, recv_s = refs[2 * n:]
        x, y, c = _mesh_pos()
        remote = []
        for t in range(n):
            for k, (px, py) in enumerate(_other_chips(x, y)):
                remote.append(pltpu.make_async_remote_copy(
                    src_ref=ins[t].at[2 * px + py], dst_ref=recv[t].at[k],
                    send_sem=send_s.at[3 * t + k], recv_sem=recv_s.at[3 * t + k],
                    device_id=(px, py, c), device_id_type=MESH))
        for cp in remote:
            cp.start()
        for cp in remote:
            cp.wait_recv()
        for cp in remote:
            cp.wait_send()

    return pl.pallas_call(
        body, name="rs_chip_exchange", in_specs=[HBM_SPEC] * n, out_specs=[HBM_SPEC] * n,
        out_shape=[jax.ShapeDtypeStruct((3,) + p.shape[1:], p.dtype) for p in parts],
        scratch_shapes=[pltpu.SemaphoreType.DMA((3 * n,)), pltpu.SemaphoreType.DMA((3 * n,))],
    )(*parts)


def _rs_add_chips(part, recv):
    _, hr, C = part.shape
    tm = _pick_tm(hr, 256)
    nb = hr // tm

    def body(a_ref, r0, r1, r2, o_ref):
        o_ref[...] = ((a_ref[...].astype(F32) + r0[...].astype(F32)) + r1[...].astype(F32)) + r2[...].astype(F32)

    def rel(k):
        return pl.BlockSpec((None, tm, C), lambda i: (k, i, 0))

    return pl.pallas_call(
        body, name="rs_add_chips", grid=(nb,),
        in_specs=[pl.BlockSpec((None, tm, C), lambda i: (_my_chip(), i, 0)), rel(0), rel(1), rel(2)],
        out_specs=pl.BlockSpec((tm, C), lambda i: (lax.axis_index("c") * nb + i, 0)),
        out_shape=jax.ShapeDtypeStruct((2 * hr, C), F32), compiler_params=_params(1))(part, recv, recv, recv)


def _rs_pair_share(halves):
    n = len(halves)

    def body(*refs):
        bufs = refs[n:2 * n]
        send_s, recv_s = refs[2 * n:]
        x, y, c = _mesh_pos()

        def half(t, cc):
            return bufs[t].at[_half_rows(halves[t].shape[0], cc), :]

        sent = [pltpu.make_async_remote_copy(
            src_ref=half(t, c), dst_ref=half(t, c), send_sem=send_s.at[t], recv_sem=recv_s.at[t],
            device_id=(x, y, 1 - c), device_id_type=MESH) for t in range(n)]
        landed = [pltpu.make_async_remote_copy(
            src_ref=half(t, 1 - c), dst_ref=half(t, 1 - c), send_sem=send_s.at[t], recv_sem=recv_s.at[t],
            device_id=(x, y, 1 - c), device_id_type=MESH) for t in range(n)]
        for cp in sent:
            cp.start()
        for cp in landed:
            cp.wait_recv()
        for cp in sent:
            cp.wait_send()

    return pl.pallas_call(
        body, name="rs_pair_share", in_specs=[HBM_SPEC] * n, out_specs=[HBM_SPEC] * n,
        out_shape=[jax.ShapeDtypeStruct(h.shape, h.dtype) for h in halves],
        input_output_aliases={t: t for t in range(n)},
        scratch_shapes=[pltpu.SemaphoreType.DMA((n,)), pltpu.SemaphoreType.DMA((n,))],
    )(*halves)


def _allreduce_small(v):
    rows = v.shape[0]

    def body(v_ref, o_ref, buf, send_s, recv_s):
        x, y, c = _mesh_pos()
        me = 4 * x + 2 * y + c
        buf[me] = v_ref[...]
        copies = []
        for r in range(1, N_DEV):
            px = 1 - x if r & 4 else x
            py = 1 - y if r & 2 else y
            pc = 1 - c if r & 1 else c
            send = pltpu.make_async_remote_copy(
                src_ref=v_ref, dst_ref=buf.at[me], send_sem=send_s.at[r - 1], recv_sem=recv_s.at[r - 1],
                device_id=(px, py, pc), device_id_type=MESH)
            peer_slot = buf.at[4 * px + 2 * py + pc]
            landed = pltpu.make_async_remote_copy(
                src_ref=peer_slot, dst_ref=peer_slot, send_sem=send_s.at[r - 1], recv_sem=recv_s.at[r - 1],
                device_id=(px, py, pc), device_id_type=MESH)
            copies.append((send, landed))
        for send, _ in copies:
            send.start()
        for _, landed in copies:
            landed.wait_recv()
        for send, _ in copies:
            send.wait_send()
        acc = buf[0]
        for j in range(1, N_DEV):
            acc = acc + buf[j]
        o_ref[...] = acc

    vm = pl.BlockSpec(memory_space=pltpu.VMEM)
    return pl.pallas_call(
        body, name="allreduce_small", in_specs=[vm], out_specs=vm,
        out_shape=jax.ShapeDtypeStruct((rows, LANES), F32),
        scratch_shapes=[pltpu.VMEM((N_DEV, rows, LANES), F32), pltpu.SemaphoreType.DMA((N_DEV - 1,)),
                        pltpu.SemaphoreType.DMA((N_DEV - 1,))],
    )(v)


def _adamw_fn(w, g, m, v):
    m2 = ADAM_B1 * m + (1.0 - ADAM_B1) * g
    v2 = ADAM_B2 * v + (1.0 - ADAM_B2) * (g * g)
    m_hat = m2 / (1.0 - ADAM_B1 ** ADAM_STEP)
    v_hat = v2 / (1.0 - ADAM_B2 ** ADAM_STEP)
    delta = -ADAM_LR * (m_hat / (jnp.sqrt(v_hat) + ADAM_EPS) + ADAM_WD * w)
    return g, delta, m2, v2


def _adamw_layer(w, g, m, v, l, prev):
    NL, R, C = w.shape
    tm = _pick_tm(R, 128)
    n_prev = 0 if prev is None else 4

    def body(w_ref, g_ref, m_ref, v_ref, *rest):
        outs = rest[n_prev:]
        for o_ref, val in zip(outs, _adamw_fn(w_ref[...], g_ref[...], m_ref[...], v_ref[...])):
            o_ref[...] = val

    lay = pl.BlockSpec((None, tm, C), lambda i: (l, i, 0))
    shape = jax.ShapeDtypeStruct((NL, R, C), F32)
    return pl.pallas_call(
        body, name="adamw", grid=(R // tm,),
        in_specs=[lay, pl.BlockSpec((tm, C), lambda i: (i, 0)), lay, lay] + [pl.BlockSpec(memory_space=pl.ANY)] * n_prev,
        out_specs=[lay] * 4, out_shape=[shape] * 4,
        input_output_aliases={4 + j: j for j in range(n_prev)},
        compiler_params=_params(1))(w, g, m, v, *(prev or []))


def _pack_small(parts):
    out = []
    for a in parts:
        flat = a.reshape(-1)
        n = -(-flat.shape[0] // (8 * LANES)) * 8 * LANES
        out.append(jnp.pad(flat, (0, n - flat.shape[0])).reshape(-1, LANES))
    return jnp.concatenate(out, axis=0)


def _unpack_small(packed, like):
    out, r = [], 0
    for a in like:
        size = int(np.prod(a.shape))
        rows = -(-size // (8 * LANES)) * 8
        out.append(packed[r:r + rows].reshape(-1)[:size].reshape(a.shape))
        r += rows
    return out


def _ffn_forward(x, g, win, wout, l):
    h = _rms_fwd(x, g)
    gate, up, act = _ffn_up(h, win, l)
    return _mm_res("ffn_down", x, [act], wout, l, 0.5), (x, h, gate, up, act)


def _ffn_backward(dx, saved, g, win, wout, l):
    x, h, gate, up, act = saved
    D = x.shape[1]
    wc = win.shape[3]
    dgu = _ffn_bwd_a(dx, wout, l, gate, up)
    dwout = _mm_tn("dw_ffn_out", act, wc, dx, D, 2, True, False, 0.5)
    dwin = _mm_tn("dw_ffn_in", h, D, dgu, wc, 4, False, True, 1.0)
    dx_in, dg = _bwd_into_norm("ffn_bwd_b", dgu, wc, win, l, 4, x, g, dx)
    return dx_in, dg, dwin, dwout.reshape(N_CHIPS, -1, D)


def kernel(x, p, rel_bias, norm_ffn1, ffn1_w_in, ffn1_w_out, norm_mix, w_qkv, q_norm_a, k_norm_a, q_norm_b, k_norm_b, sink_b, w_o, norm_ffn2, ffn2_w_in, ffn2_w_out, norm_ple, w_ple_gate, w_ple_proj, loss_target, m_rel_bias, m_norm_ffn1, m_ffn1_w_in, m_ffn1_w_out, m_norm_mix, m_w_qkv, m_q_norm_a, m_k_norm_a, m_q_norm_b, m_k_norm_b, m_sink_b, m_w_o, m_norm_ffn2, m_ffn2_w_in, m_ffn2_w_out, m_norm_ple, m_w_ple_gate, m_w_ple_proj, v_rel_bias, v_norm_ffn1, v_ffn1_w_in, v_ffn1_w_out, v_norm_mix, v_w_qkv, v_q_norm_a, v_k_norm_a, v_q_norm_b, v_k_norm_b, v_sink_b, v_w_o, v_norm_ffn2, v_ffn2_w_in, v_ffn2_w_out, v_norm_ple, v_w_ple_gate, v_w_ple_proj):
    given = dict(locals())
    T, D = x.shape[1], x.shape[2]
    NL = norm_ffn1.shape[0]
    x0 = x.reshape(T, D)
    tgt = loss_target.reshape(T, D)
    n_a, n_b, n_kv = _qkv_layout(D)

    gathered = dict(zip(BIG, _allgather_weights([_cast_into_slot(given[name]) for name in BIG])))

    def by_rows(a):
        return a.reshape(NL, -1, a.shape[-1])

    def by_cols(a):
        return a.transpose(0, 2, 1, 3).reshape(NL, a.shape[2], -1)

    win1, win2 = gathered["ffn1_w_in"], gathered["ffn2_w_in"]
    wout1, wout2 = by_rows(gathered["ffn1_w_out"]), by_rows(gathered["ffn2_w_out"])
    wqkv = by_cols(gathered["w_qkv"])
    wqkv4 = wqkv.reshape(NL, 1, D, -1)
    wo = by_rows(gathered["w_o"])
    wpg = by_rows(gathered["w_ple_gate"])
    wpp = by_cols(gathered["w_ple_proj"])
    QW = wqkv.shape[2]

    dils = tuple(d for _, d in DILATED_CONFIGS)
    cfg_a = [(w // (2 * d), d) for w, d in DILATED_CONFIGS]
    pad_a, pad_b = _window_pad(cfg_a[0][0]), _window_pad(SWA_RADIUS)
    bmaps_a, bmaps_b = jnp.asarray(_bucket_maps(cfg_a)), jnp.asarray(_bucket_maps([(SWA_RADIUS, 1)]))
    n_heads = rel_bias.shape[1] // 2
    bias_a = _bias_build(rel_bias, bmaps_a, 0)
    bias_b = _bias_build(rel_bias, bmaps_b, n_heads)
    no_sink = jnp.full((n_heads,), NEG, F32)

    def gains_row(l):
        ones = jnp.ones((n_a * LANES,), F32)
        return jnp.concatenate([
            jnp.tile(q_norm_a[l], 2 * n_a), jnp.tile(k_norm_a[l], 2 * n_a), ones,
            jnp.tile(q_norm_b[l], 2 * n_b), jnp.tile(k_norm_b[l], 2 * n_kv), jnp.ones((n_kv * LANES,), F32)]).reshape(1, QW)

    saved = []
    xc = x0
    for l in range(NL):
        s = {}
        xc, s["ffn1"] = _ffn_forward(xc, norm_ffn1[l:l + 1], win1, wout1, l)
        s["x1"] = xc
        h2 = _rms_fwd(xc, norm_mix[l:l + 1])
        raw = _mm_plain("qkv_proj", h2, wqkv, l)
        s["h2"], s["raw"] = h2, raw
        s["qkv_a"], s["qkv_b"] = _qkv_post(raw, gains_row(l))
        s["o_a"], s["lse_a"] = _attn_fwd(s["qkv_a"], bias_a, no_sink, dils, pad_a)
        s["o_b"], s["lse_b"] = _attn_fwd(s["qkv_b"], bias_b, sink_b[l], (1,), pad_b)
        xc = _mm_res("attn_out", xc, [s["o_a"], s["o_b"]], wo, l, 1.0)
        xc, s["ffn2"] = _ffn_forward(xc, norm_ffn2[l:l + 1], win2, wout2, l)
        s["x3"] = xc
        s["hn"] = _rms_fwd(xc, norm_ple[l:l + 1])
        s["p"] = p[l].reshape(T, -1)
        xc, s["gate"], s["pp"] = _ple_fwd(xc, s["hn"], s["p"], wpg, wpp, l)
        saved.append(s)

    dx, loss_blk = _loss_fwd_bwd(xc, tgt)
    loss = lax.psum(loss_blk[0, 0], ("x", "y", "c"))

    gw = {name: [None] * NL for name in BIG}
    gs = {name: [None] * NL for name in SMALL if name != "rel_bias"}
    dt_a, dt_b = [], []
    for l in reversed(range(NL)):
        s = saved[l]
        dx, gs["norm_ple"][l], dwg, dwp = _ple_bwd(dx, s["gate"], s["pp"], s["hn"], s["p"], s["x3"],
                                                   norm_ple[l:l + 1], wpg, l)
        gw["w_ple_gate"][l] = dwg.reshape(N_CHIPS, -1, D)
        gw["w_ple_proj"][l] = dwp.reshape(dwp.shape[0], N_CHIPS, -1).transpose(1, 0, 2)
        dx, gs["norm_ffn2"][l], gw["ffn2_w_in"][l], gw["ffn2_w_out"][l] = _ffn_backward(
            dx, s["ffn2"], norm_ffn2[l:l + 1], win2, wout2, l)
        do, dd = _mix_bwd_in(dx, wo, l, [s["o_a"], s["o_b"]])
        hwa = s["o_a"].shape[1]
        gw["w_o"][l] = jnp.concatenate([
            _mm_tn("dw_o", o_, o_.shape[1], dx, D, 1, False, False, 1.0).reshape(-1, D // N_CHIPS, D)
            for o_ in (s["o_a"], s["o_b"])], axis=0)
        dqa, dka, dva, dt, _ = _attn_bwd(s["qkv_a"], bias_a, no_sink, dils, pad_a, do, s["lse_a"], dd, 0)
        dt_a.append(dt)
        dqb, dkb, dvb, dt, dsink = _attn_bwd(s["qkv_b"], bias_b, sink_b[l], (1,), pad_b, do, s["lse_b"], dd,
                                             hwa // LANES)
        dt_b.append(dt)
        gs["sink_b"][l] = dsink[:, 0, ::HEAD_DIM].reshape(-1)
        draw, dgains = _qkv_post_bwd(s["raw"], gains_row(l), (dqa, dka, dva), (dqb, dkb, dvb))
        dgv = dgains.reshape(-1, HEAD_DIM)
        gs["q_norm_a"][l] = dgv[:2 * n_a].sum(0)
        gs["k_norm_a"][l] = dgv[2 * n_a:4 * n_a].sum(0)
        gs["q_norm_b"][l] = dgv[6 * n_a:6 * n_a + 2 * n_b].sum(0)
        gs["k_norm_b"][l] = dgv[6 * n_a + 2 * n_b:6 * n_a + 2 * n_b + 2 * n_kv].sum(0)
        dwqkv = _mm_tn("dw_qkv", s["h2"], D, draw, QW, 1, False, False, 1.0)
        gw["w_qkv"][l] = dwqkv.reshape(D, N_CHIPS, -1).transpose(1, 0, 2)
        dx, gs["norm_mix"][l] = _bwd_into_norm("qkv_bwd_b", draw, QW, wqkv4, l, 1, s["x1"], norm_mix[l:l + 1], dx)
        dx, gs["norm_ffn1"][l], gw["ffn1_w_in"][l], gw["ffn1_w_out"][l] = _ffn_backward(
            dx, s["ffn1"], norm_ffn1[l:l + 1], win1, wout1, l)
    grad_x = dx.reshape(x.shape)
    d_rel_bias = (_bias_grad(dt_a, bmaps_a, 0) + _bias_grad(dt_b, bmaps_b, n_heads))[:, :rel_bias.shape[1]]

    items = [gw[name][l] for name in BIG for l in range(NL)]
    got = _rs_pair_exchange(items)
    part = [_rs_add_pair(g_, r_) for g_, r_ in zip(items, got)]
    recv = _rs_chip_exchange(part)
    g_full = _rs_pair_share([_rs_add_chips(p_, r_) for p_, r_ in zip(part, recv)])

    out = {}
    for i, name in enumerate(BIG):
        res = None
        for l in range(NL):
            res = _adamw_layer(given[name], g_full[i * NL + l], given["m_" + name], given["v_" + name], l, res)
        out[name] = res

    small_g = [d_rel_bias] + [jnp.stack([t.reshape(-1) for t in gs[name]]) for name in SMALL[1:]]
    g_sum = _allreduce_small(_pack_small(small_g))
    res = _ew("adamw_small", _adamw_fn,
              [_pack_small([given[n] for n in SMALL]), g_sum, _pack_small([given["m_" + n] for n in SMALL]),
               _pack_small([given["v_" + n] for n in SMALL])], [(LANES, F32)] * 4)
    like = [given[n] for n in SMALL]
    unpacked = [_unpack_small(r, like) for r in res]
    for i, name in enumerate(SMALL):
        out[name] = [u[i] for u in unpacked]

    return (loss, grad_x, *[out[n][0] for n in WEIGHTS], *[out[n][1] for n in WEIGHTS],
            *[out[n][2] for n in WEIGHTS], *[out[n][3] for n in WEIGHTS])
```

```python
import functools
import math

import numpy as np
import jax
import jax.numpy as jnp
from jax import lax
from jax.experimental import pallas as pl
from jax.experimental.pallas import tpu as pltpu

F32 = jnp.float32
BF16 = jnp.bfloat16
MESH = pl.DeviceIdType.MESH

HEAD_DIM = 64
LANES = 128
QBLOCK = 128
FWD_UNROLL, BWD_UNROLL = 4, 4
N_BUCKETS = 32
MAX_DISTANCE = 1024
DILATED_CONFIGS = ((128, 1), (512, 4), (2048, 16))
SWA_RADIUS = 128
GROUP_B = 4
EPS = 1e-6
NEG = -1e30
Q_SCALE = HEAD_DIM ** -0.5
ADAM_LR, ADAM_B1, ADAM_B2, ADAM_EPS, ADAM_WD, ADAM_STEP = 0.001, 0.9, 0.999, 1e-08, 0.01, 10
VMEM_LIMIT = 56 * 2 ** 20
N_CHIPS = 4
N_DEV = 8

BIG = ("ffn1_w_in", "ffn1_w_out", "w_qkv", "w_o", "ffn2_w_in", "ffn2_w_out", "w_ple_gate", "w_ple_proj")
SMALL = ("rel_bias", "norm_ffn1", "norm_mix", "q_norm_a", "k_norm_a", "q_norm_b", "k_norm_b", "sink_b",
         "norm_ffn2", "norm_ple")
WEIGHTS = ("rel_bias", "norm_ffn1", "ffn1_w_in", "ffn1_w_out", "norm_mix", "w_qkv", "q_norm_a", "k_norm_a",
           "q_norm_b", "k_norm_b", "sink_b", "w_o", "norm_ffn2", "ffn2_w_in", "ffn2_w_out", "norm_ple",
           "w_ple_gate", "w_ple_proj")


HBM_SPEC = pl.BlockSpec(memory_space=pltpu.HBM)
ANY_SPEC = pl.BlockSpec(memory_space=pl.ANY)
SEM_SPEC = pl.BlockSpec(memory_space=pltpu.SEMAPHORE)


def _params(n_grid):
    return pltpu.CompilerParams(dimension_semantics=("arbitrary",) * n_grid, vmem_limit_bytes=VMEM_LIMIT)


def _pick_tm(rows, cap):
    t = (min(cap, rows) // 16) * 16
    while t >= 16:
        if rows % t == 0:
            return t
        t -= 16
    return rows


def _dot(a, b):
    return jnp.dot(a, b, preferred_element_type=F32)


def _dot_nt(a, b):
    return lax.dot_general(a, b, (((1,), (1,)), ((), ())), preferred_element_type=F32)


def _dot_tn(a, b):
    return lax.dot_general(a, b, (((0,), (0,)), ((), ())), preferred_element_type=F32)


def _sigmoid(z):
    return 1.0 / (1.0 + jnp.exp(-z))


def _lo_lanes(shape):
    return lax.broadcasted_iota(jnp.int32, shape, len(shape) - 1) % LANES < HEAD_DIM


def _seg_sum(blk):
    lo = _lo_lanes(blk.shape)
    s_lo = jnp.sum(jnp.where(lo, blk, 0.0), axis=1, keepdims=True)
    s_hi = jnp.sum(jnp.where(lo, 0.0, blk), axis=1, keepdims=True)
    return jnp.where(lo, s_lo, s_hi)


def _rms_bwd_tile(x, g, dh):
    r = lax.rsqrt(jnp.mean(x * x, axis=-1, keepdims=True) + EPS)
    xh = x * r
    dyg = dh * g
    dx = r * (dyg - xh * jnp.mean(dyg * xh, axis=-1, keepdims=True))
    return dx, jnp.sum(dh * xh, axis=0, keepdims=True)


def _ew(name, fn, ins, out_defs, cap=512):
    rows = ins[0].shape[0]
    tm = _pick_tm(rows, cap)
    n_in = len(ins)

    def body(*refs):
        vals = fn(*[r[...] for r in refs[:n_in]])
        if not isinstance(vals, tuple):
            vals = (vals,)
        for r, v in zip(refs[n_in:], vals):
            r[...] = v.astype(r.dtype)

    return pl.pallas_call(
        body, name=name, grid=(rows // tm,),
        in_specs=[pl.BlockSpec((tm, a.shape[1]), lambda i: (i, 0)) for a in ins],
        out_specs=[pl.BlockSpec((tm, c), lambda i: (i, 0)) for c, _ in out_defs],
        out_shape=[jax.ShapeDtypeStruct((rows, c), dt) for c, dt in out_defs],
        compiler_params=_params(1))(*ins)


def _rms_fwd(x, g):
    T, D = x.shape
    tm = _pick_tm(T, 512)

    def body(x_ref, g_ref, h_ref):
        xv = x_ref[...]
        r = lax.rsqrt(jnp.mean(xv * xv, axis=-1, keepdims=True) + EPS)
        h_ref[...] = (xv * r * g_ref[...]).astype(BF16)

    return pl.pallas_call(
        body, name="rms_fwd", grid=(T // tm,),
        in_specs=[pl.BlockSpec((tm, D), lambda i: (i, 0)), pl.BlockSpec((1, D), lambda i: (0, 0))],
        out_specs=pl.BlockSpec((tm, D), lambda i: (i, 0)),
        out_shape=jax.ShapeDtypeStruct((T, D), BF16), compiler_params=_params(1))(x, g)


def _ffn_up(h, win):
    T, D = h.shape
    wc = win.shape[2]
    tm = _pick_tm(T, 512)

    def body(h_ref, wg_ref, wu_ref, gate_ref, up_ref, act_ref):
        hv = h_ref[...]
        gte = _dot(hv, wg_ref[...])
        u = _dot(hv, wu_ref[...])
        gate_ref[...] = gte.astype(BF16)
        up_ref[...] = u.astype(BF16)
        act_ref[...] = (gte * _sigmoid(gte) * u).astype(BF16)

    out = jax.ShapeDtypeStruct((T, 2 * wc), BF16)
    ospec = pl.BlockSpec((tm, wc), lambda j, i: (i, j))
    return pl.pallas_call(
        body, name="ffn_up", grid=(2, T // tm),
        in_specs=[pl.BlockSpec((tm, D), lambda j, i: (i, 0)),
                  pl.BlockSpec((None, D, wc), lambda j, i: (j, 0, 0)),
                  pl.BlockSpec((None, D, wc), lambda j, i: (j + 2, 0, 0))],
        out_specs=[ospec, ospec, ospec], out_shape=[out, out, out], compiler_params=_params(2))(h, win, win)


def _mm_res(name, res, a_list, w, scale):
    T, N = res.shape
    n = len(a_list)
    widths = [a.shape[1] for a in a_list]
    tm = _pick_tm(T, 512)

    def body(*refs):
        r_ref, a_refs, w_refs, o_ref = refs[0], refs[1:1 + n], refs[1 + n:1 + 2 * n], refs[1 + 2 * n]
        acc = _dot(a_refs[0][...].astype(BF16), w_refs[0][...])
        for a_ref, w_ref in zip(a_refs[1:], w_refs[1:]):
            acc = acc + _dot(a_ref[...].astype(BF16), w_ref[...])
        o_ref[...] = r_ref[...] + scale * acc

    w_specs, off = [], 0
    for k in widths:
        w_specs.append(pl.BlockSpec((k, N), lambda i, blk=off // k: (blk, 0)))
        off += k
    return pl.pallas_call(
        body, name=name, grid=(T // tm,),
        in_specs=[pl.BlockSpec((tm, N), lambda i: (i, 0))]
        + [pl.BlockSpec((tm, k), lambda i: (i, 0)) for k in widths] + w_specs,
        out_specs=pl.BlockSpec((tm, N), lambda i: (i, 0)),
        out_shape=jax.ShapeDtypeStruct((T, N), F32), compiler_params=_params(1))(res, *a_list, *([w] * n))


def _mm_plain(name, a, w):
    T, K = a.shape
    N = w.shape[1]
    tm = _pick_tm(T, 512)

    def body(a_ref, w_ref, o_ref):
        o_ref[...] = _dot(a_ref[...], w_ref[...])

    return pl.pallas_call(
        body, name=name, grid=(T // tm,),
        in_specs=[pl.BlockSpec((tm, K), lambda i: (i, 0)), pl.BlockSpec((K, N), lambda i: (0, 0))],
        out_specs=pl.BlockSpec((tm, N), lambda i: (i, 0)),
        out_shape=jax.ShapeDtypeStruct((T, N), F32), compiler_params=_params(1))(a, w)


def _mm_tn(name, a, a_w, b, b_w, n_slots, a_by_slot, b_by_slot, scale, tm_cap=512):
    T = a.shape[0]
    tm = _pick_tm(T, tm_cap)
    nt = T // tm

    def body(a_ref, b_ref, o_ref, acc):
        i = pl.program_id(1)

        @pl.when(i == 0)
        def _():
            acc[...] = jnp.zeros_like(acc)

        acc[...] += _dot_tn(a_ref[...].astype(BF16), b_ref[...].astype(BF16))

        @pl.when(i == nt - 1)
        def _():
            o_ref[...] = (acc[...] * scale).astype(BF16)

    return pl.pallas_call(
        body, name=name, grid=(n_slots, nt),
        in_specs=[pl.BlockSpec((tm, a_w), (lambda s, i: (i, s)) if a_by_slot else (lambda s, i: (i, 0))),
                  pl.BlockSpec((tm, b_w), (lambda s, i: (i, s)) if b_by_slot else (lambda s, i: (i, 0)))],
        out_specs=pl.BlockSpec((None, a_w, b_w), lambda s, i: (s, 0, 0)),
        out_shape=jax.ShapeDtypeStruct((n_slots, a_w, b_w), BF16),
        scratch_shapes=[pltpu.VMEM((a_w, b_w), F32)], compiler_params=_params(2))(a, b)


def _ffn_bwd_a(dx, wout, gate, up, after):
    T, D = dx.shape
    F = gate.shape[1]
    tm = _pick_tm(T, 256)

    def body(dx_ref, w_ref, g_ref, u_ref, after_ref, o_ref):
        dact = 0.5 * _dot_nt(dx_ref[...].astype(BF16), w_ref[...])
        gte = g_ref[...].astype(F32)
        u = u_ref[...].astype(F32)
        sg = _sigmoid(gte)
        silu = gte * sg
        o_ref[:, :F] = (dact * u * (sg + silu * (1.0 - sg))).astype(BF16)
        o_ref[:, F:] = (dact * silu).astype(BF16)

    return pl.pallas_call(
        body, name="ffn_bwd_a", grid=(T // tm,),
        in_specs=[pl.BlockSpec((tm, D), lambda i: (i, 0)), pl.BlockSpec((F, D), lambda i: (0, 0)),
                  pl.BlockSpec((tm, F), lambda i: (i, 0)), pl.BlockSpec((tm, F), lambda i: (i, 0)), ANY_SPEC],
        out_specs=pl.BlockSpec((tm, 2 * F), lambda i: (i, 0)),
        out_shape=jax.ShapeDtypeStruct((T, 2 * F), BF16), compiler_params=_params(1))(dx, wout, gate, up, after)


def _bwd_into_norm(name, d, d_w, w, n_slots, x, g, dx_in):
    T, D = x.shape
    tm = _pick_tm(T, 256)

    def body(*refs):
        d_refs, w_refs = refs[:n_slots], refs[n_slots:2 * n_slots]
        x_ref, g_ref, dxi_ref, dx_ref, dg_ref = refs[2 * n_slots:]
        dh = _dot_nt(d_refs[0][...], w_refs[0][...])
        for s in range(1, n_slots):
            dh = dh + _dot_nt(d_refs[s][...], w_refs[s][...])
        dxn, dg = _rms_bwd_tile(x_ref[...], g_ref[...], dh)
        dx_ref[...] = dxi_ref[...] + dxn

        @pl.when(pl.program_id(0) == 0)
        def _():
            dg_ref[...] = jnp.zeros_like(dg_ref)

        dg_ref[...] += dg

    row = pl.BlockSpec((tm, D), lambda i: (i, 0))
    vec = pl.BlockSpec((1, D), lambda i: (0, 0))
    return pl.pallas_call(
        body, name=name, grid=(T // tm,),
        in_specs=[pl.BlockSpec((tm, d_w), lambda i, s=s: (i, s)) for s in range(n_slots)]
        + [pl.BlockSpec((None, D, d_w), lambda i, s=s: (s, 0, 0)) for s in range(n_slots)]
        + [row, vec, row],
        out_specs=[row, vec],
        out_shape=[jax.ShapeDtypeStruct((T, D), F32), jax.ShapeDtypeStruct((1, D), F32)],
        compiler_params=_params(1))(*([d] * n_slots + [w] * n_slots + [x, g, dx_in]))


def _ple_fwd(x, hn, p, wg, wp):
    T, D = x.shape
    P = p.shape[1]
    tm = _pick_tm(T, 256)

    def body(x_ref, hn_ref, p_ref, wg_ref, wp_ref, xo_ref, gate_ref, pp_ref):
        gate = _sigmoid(_dot(hn_ref[...], wg_ref[...]))
        pp = _dot(p_ref[...].astype(BF16), wp_ref[...])
        gate_ref[...] = gate
        pp_ref[...] = pp
        xo_ref[...] = x_ref[...] + gate * pp

    row = pl.BlockSpec((tm, D), lambda i: (i, 0))
    out = jax.ShapeDtypeStruct((T, D), F32)
    return pl.pallas_call(
        body, name="ple_fwd", grid=(T // tm,),
        in_specs=[row, row, pl.BlockSpec((tm, P), lambda i: (i, 0)),
                  pl.BlockSpec((D, D), lambda i: (0, 0)), pl.BlockSpec((P, D), lambda i: (0, 0))],
        out_specs=[row, row, row], out_shape=[out, out, out], compiler_params=_params(1))(x, hn, p, wg, wp)


def _ple_bwd(dx, gate, pp, hn, p, x, g, wg, after):
    T, D = x.shape
    P = p.shape[1]
    tm = _pick_tm(T, 256)
    nt = T // tm

    def body(dx_ref, gate_ref, pp_ref, hn_ref, p_ref, x_ref, g_ref, wg_ref, after_ref,
             dxo_ref, dg_ref, dwg_ref, dwp_ref, acc_g, acc_p):
        i = pl.program_id(0)

        @pl.when(i == 0)
        def _():
            acc_g[...] = jnp.zeros_like(acc_g)
            acc_p[...] = jnp.zeros_like(acc_p)
            dg_ref[...] = jnp.zeros_like(dg_ref)

        dxv = dx_ref[...]
        gate = gate_ref[...]
        dz = (dxv * pp_ref[...] * gate * (1.0 - gate)).astype(BF16)
        dpp = (dxv * gate).astype(BF16)
        acc_g[...] += _dot_tn(hn_ref[...], dz)
        acc_p[...] += _dot_tn(p_ref[...].astype(BF16), dpp)
        dxn, dg = _rms_bwd_tile(x_ref[...], g_ref[...], _dot_nt(dz, wg_ref[...]))
        dxo_ref[...] = dxv + dxn
        dg_ref[...] += dg

        @pl.when(i == nt - 1)
        def _():
            dwg_ref[...] = acc_g[...].astype(BF16)
            dwp_ref[...] = acc_p[...].astype(BF16)

    row = pl.BlockSpec((tm, D), lambda i: (i, 0))
    vec = pl.BlockSpec((1, D), lambda i: (0, 0))
    return pl.pallas_call(
        body, name="ple_bwd", grid=(nt,),
        in_specs=[row, row, row, row, pl.BlockSpec((tm, P), lambda i: (i, 0)), row, vec,
                  pl.BlockSpec((D, D), lambda i: (0, 0)), ANY_SPEC],
        out_specs=[row, vec, pl.BlockSpec((D, D), lambda i: (0, 0)), pl.BlockSpec((P, D), lambda i: (0, 0))],
        out_shape=[jax.ShapeDtypeStruct((T, D), F32), jax.ShapeDtypeStruct((1, D), F32),
                   jax.ShapeDtypeStruct((D, D), BF16), jax.ShapeDtypeStruct((P, D), BF16)],
        scratch_shapes=[pltpu.VMEM((D, D), F32), pltpu.VMEM((P, D), F32)],
        compiler_params=_params(1))(dx, gate, pp, hn, p, x, g, wg, after)


def _loss_fwd_bwd(y, tgt):
    T, D = y.shape
    tm = _pick_tm(T, 512)

    def body(y_ref, t_ref, dy_ref, loss_ref):
        e = y_ref[...] - t_ref[...]
        dy_ref[...] = e / D

        @pl.when(pl.program_id(0) == 0)
        def _():
            loss_ref[...] = jnp.zeros_like(loss_ref)

        loss_ref[...] += 0.5 * jnp.sum(jnp.mean(e * e, axis=-1, keepdims=True), axis=0, keepdims=True)

    row = pl.BlockSpec((tm, D), lambda i: (i, 0))
    return pl.pallas_call(
        body, name="loss", grid=(T // tm,), in_specs=[row, row],
        out_specs=[row, pl.BlockSpec((8, LANES), lambda i: (0, 0))],
        out_shape=[jax.ShapeDtypeStruct((T, D), F32), jax.ShapeDtypeStruct((8, LANES), F32)],
        compiler_params=_params(1))(y, tgt)


def _qkv_layout(D):
    n_a = D // (2 * LANES)
    n_b = D // (2 * LANES)
    n_kv = max(1, (2 * n_b) // GROUP_B) * HEAD_DIM // LANES
    return n_a, n_b, n_kv


def _dup_half(xv, half):
    rolled = pltpu.roll(xv, HEAD_DIM, 1)
    lo = _lo_lanes(xv.shape)
    return jnp.where(lo, xv, rolled) if half == 0 else jnp.where(lo, rolled, xv)


def _qkv_post(raw, gains):
    T, W = raw.shape
    n_a, n_b, n_kv = _qkv_layout(W * 4 // 9)
    tm = _pick_tm(T, 256)
    o_qb = 3 * n_a

    def norm(xv, gv, scale):
        ms = _seg_sum(xv * xv) * (1.0 / HEAD_DIM)
        return xv * lax.rsqrt(ms + EPS) * gv * scale

    def body(raw_ref, g_ref, a_ref, b_ref):
        def blk(cb):
            return raw_ref[:, cb * LANES:(cb + 1) * LANES]

        def gn(cb):
            return g_ref[:, cb * LANES:(cb + 1) * LANES]

        for cb in range(n_a):
            a_ref[:, cb * LANES:(cb + 1) * LANES] = norm(blk(cb), gn(cb), Q_SCALE)
            cbk = n_a + cb
            a_ref[:, cbk * LANES:(cbk + 1) * LANES] = norm(blk(cbk), gn(cbk), 1.0)
            cbv = 2 * n_a + cb
            a_ref[:, cbv * LANES:(cbv + 1) * LANES] = blk(cbv)
        for cb in range(n_b):
            src = o_qb + cb
            b_ref[:, cb * LANES:(cb + 1) * LANES] = norm(blk(src), gn(src), Q_SCALE)
        for e in range(n_b):
            kvh = (2 * e) // GROUP_B
            ck = o_qb + n_b + kvh // 2
            cv = ck + n_kv
            kn = norm(blk(ck), gn(ck), 1.0)
            b_ref[:, (n_b + e) * LANES:(n_b + e + 1) * LANES] = _dup_half(kn, kvh % 2)
            b_ref[:, (2 * n_b + e) * LANES:(2 * n_b + e + 1) * LANES] = _dup_half(blk(cv), kvh % 2)

    wa, wb = 3 * n_a * LANES, 3 * n_b * LANES
    return pl.pallas_call(
        body, name="qkv_post", grid=(T // tm,),
        in_specs=[pl.BlockSpec((tm, W), lambda i: (i, 0)), pl.BlockSpec((1, W), lambda i: (0, 0))],
        out_specs=[pl.BlockSpec((tm, wa), lambda i: (i, 0)), pl.BlockSpec((tm, wb), lambda i: (i, 0))],
        out_shape=[jax.ShapeDtypeStruct((T, wa), F32), jax.ShapeDtypeStruct((T, wb), F32)],
        compiler_params=_params(1))(raw, gains)


def _qkv_post_bwd(raw, gains, d_a, d_b):
    T, W = raw.shape
    n_a, n_b, n_kv = _qkv_layout(W * 4 // 9)
    tm = _pick_tm(T, 256)
    o_qb = 3 * n_a

    def body(raw_ref, g_ref, daq, dak, dav, dbq, dbk, dbv, o_ref, dg_ref):
        @pl.when(pl.program_id(0) == 0)
        def _():
            dg_ref[...] = jnp.zeros_like(dg_ref)

        def cols(ref, cb):
            return ref[:, cb * LANES:(cb + 1) * LANES]

        def norm_bwd(cb, dy, scale):
            xv = cols(raw_ref, cb)
            gv = cols(g_ref, cb)
            r = lax.rsqrt(_seg_sum(xv * xv) * (1.0 / HEAD_DIM) + EPS)
            xh = xv * r
            dys = dy * scale
            dyg = dys * gv
            dxv = r * (dyg - xh * (_seg_sum(dyg * xh) * (1.0 / HEAD_DIM)))
            o_ref[:, cb * LANES:(cb + 1) * LANES] = dxv.astype(BF16)
            dg_ref[:, cb * LANES:(cb + 1) * LANES] += jnp.sum(dys * xh, axis=0, keepdims=True)

        def fold(ref, kv_blk):
            halves = []
            for half in range(2):
                kvh = 2 * kv_blk + half
                blocks = [e for e in range(n_b) if (2 * e) // GROUP_B == kvh]
                s = cols(ref, blocks[0])
                for e in blocks[1:]:
                    s = s + cols(ref, e)
                halves.append(s + pltpu.roll(s, HEAD_DIM, 1))
            return jnp.where(_lo_lanes(halves[0].shape), halves[0], halves[1])

        for cb in range(n_a):
            norm_bwd(cb, cols(daq, cb), Q_SCALE)
            norm_bwd(n_a + cb, cols(dak, cb), 1.0)
            cbv = 2 * n_a + cb
            o_ref[:, cbv * LANES:(cbv + 1) * LANES] = cols(dav, cb).astype(BF16)
        for cb in range(n_b):
            norm_bwd(o_qb + cb, cols(dbq, cb), Q_SCALE)
        for kb in range(n_kv):
            ck = o_qb + n_b + kb
            cv = ck + n_kv
            norm_bwd(ck, fold(dbk, kb), 1.0)
            o_ref[:, cv * LANES:(cv + 1) * LANES] = fold(dbv, kb).astype(BF16)

    hw_a, hw_b = n_a * LANES, n_b * LANES
    return pl.pallas_call(
        body, name="qkv_post_bwd", grid=(T // tm,),
        in_specs=[pl.BlockSpec((tm, W), lambda i: (i, 0)), pl.BlockSpec((1, W), lambda i: (0, 0))]
        + [pl.BlockSpec((tm, hw_a), lambda i: (i, 0))] * 3 + [pl.BlockSpec((tm, hw_b), lambda i: (i, 0))] * 3,
        out_specs=[pl.BlockSpec((tm, W), lambda i: (i, 0)), pl.BlockSpec((1, W), lambda i: (0, 0))],
        out_shape=[jax.ShapeDtypeStruct((T, W), BF16), jax.ShapeDtypeStruct((1, W), F32)],
        compiler_params=_params(1))(raw, gains, *d_a, *d_b)


def _t5_bucket_np(rel):
    half = N_BUCKETS // 2
    max_exact = half // 2
    ret = np.where(rel > 0, half, 0)
    n = np.abs(rel)
    nf = np.maximum(n, 1).astype(np.float32)
    large = max_exact + (np.log(nf / np.float32(max_exact)) / np.float32(math.log(MAX_DISTANCE / max_exact))
                         * np.float32(half - max_exact)).astype(np.int32)
    large = np.minimum(large, half - 1)
    return ret + np.where(n < max_exact, n, large)


def _window_pad(radius):
    assert radius <= QBLOCK
    return HEAD_DIM if radius <= HEAD_DIM else QBLOCK


def _bucket_maps(configs):
    pad = _window_pad(configs[0][0])
    q = np.arange(QBLOCK)[:, None]
    kk = np.arange(QBLOCK + 2 * pad)[None, :]
    rel = kk - pad - q
    maps = [np.where(np.abs(rel) <= radius, _t5_bucket_np(rel * dil), -1) for radius, dil in configs]
    return np.stack(maps).astype(np.int32)


def _bias_build(rel_bias, bmaps, col0):
    n_sets, _, W = bmaps.shape
    n_heads = rel_bias.shape[1] // 2

    def body(rb_ref, bm_ref, o_ref):
        h = pl.program_id(1)
        bm = bm_ref[...]

        def step(n, acc):
            return jnp.where(bm == n, rb_ref[n, col0 + h], acc)

        o_ref[...] = lax.fori_loop(0, N_BUCKETS, step, jnp.where(bm < 0, NEG, 0.0).astype(F32))

    return pl.pallas_call(
        body, name="bias_build", grid=(n_sets, n_heads),
        in_specs=[pl.BlockSpec(memory_space=pltpu.SMEM), pl.BlockSpec((None, QBLOCK, W), lambda s, h: (s, 0, 0))],
        out_specs=pl.BlockSpec((None, None, QBLOCK, W), lambda s, h: (s, h, 0, 0)),
        out_shape=jax.ShapeDtypeStruct((n_sets, n_heads, QBLOCK, W), F32),
        compiler_params=_params(2))(rel_bias, bmaps)


def _bias_grad(dtiles, bmaps, col0):
    n_sets, _, W = bmaps.shape
    n_heads = dtiles[0].shape[1]
    n_l = len(dtiles)

    def body(*refs):
        bm_ref, o_ref = refs[0], refs[1 + n_l]
        s, h = pl.program_id(0), pl.program_id(1)

        @pl.when((s == 0) & (h == 0))
        def _():
            o_ref[...] = jnp.zeros_like(o_ref)

        d = refs[1][...]
        for r in refs[2:1 + n_l]:
            d = d + r[...]
        bm = bm_ref[...]
        rows = lax.broadcasted_iota(jnp.int32, o_ref.shape, 0)
        lanes = lax.broadcasted_iota(jnp.int32, o_ref.shape, 1)

        def step(n, acc):
            val = jnp.sum(jnp.where(bm == n, d, 0.0))
            return acc + jnp.where((rows == n) & (lanes == col0 + h), val, 0.0)

        o_ref[...] += lax.fori_loop(0, N_BUCKETS, step, jnp.zeros(o_ref.shape, F32))

    tile = pl.BlockSpec((None, None, QBLOCK, W), lambda s, h: (s, h, 0, 0))
    return pl.pallas_call(
        body, name="bias_grad", grid=(n_sets, n_heads),
        in_specs=[pl.BlockSpec((None, QBLOCK, W), lambda s, h: (s, 0, 0))] + [tile] * n_l,
        out_specs=pl.BlockSpec((N_BUCKETS, LANES), lambda s, h: (0, 0)),
        out_shape=jax.ShapeDtypeStruct((N_BUCKETS, LANES), F32), compiler_params=_params(2))(bmaps, *dtiles)


def _rows(l_start, n, d, r):
    if d == 1:
        return pl.ds(pl.multiple_of(l_start, 8), n)
    return pl.ds(l_start * d + r, n, stride=d)


def _block_geometry(b, nb_sub, pad):
    r, lb = b // nb_sub, b % nb_sub
    l0 = lb * QBLOCK
    lp = jnp.maximum(l0 - pad, 0)
    ln = jnp.minimum(l0 + QBLOCK, nb_sub * QBLOCK - pad)
    col = lax.broadcasted_iota(jnp.int32, (QBLOCK, QBLOCK + 2 * pad), 1)
    valid = ((col >= pad) | (lb != 0)) & ((col < pad + QBLOCK) | (lb != nb_sub - 1))
    return r, l0, lp, ln, valid


def _window(ref, l0, lp, ln, pad, d, r):
    return jnp.concatenate([ref[_rows(lp, pad, d, r), :], ref[_rows(l0, QBLOCK, d, r), :],
                            ref[_rows(ln, pad, d, r), :]], axis=0)


def _attn_fwd(qkv, bias, sink, dils, pad):
    T = qkv.shape[0]
    hw = qkv.shape[1] // 3
    ng = hw // LANES
    n_br = len(dils)
    n_blocks = T // QBLOCK
    W = QBLOCK + 2 * pad
    chunk = 256

    def body(sink_ref, q_ref, k_ref, v_ref, bias_ref, o_ref, lse_ref, *scratch):
        g = pl.program_id(0)
        lo = _lo_lanes((QBLOCK, LANES))
        for c, d in enumerate(dils):
            nb_sub = n_blocks // d
            o_dst = scratch[0].at[c] if n_br > 1 else o_ref
            l_dst = scratch[1].at[c] if n_br > 1 else lse_ref

            def block(b, carry, c=c, d=d, nb_sub=nb_sub, o_dst=o_dst, l_dst=l_dst):
                r, l0, lp, ln, valid = _block_geometry(b, nb_sub, pad)
                q = q_ref[_rows(l0, QBLOCK, d, r), :].astype(BF16)
                k = _window(k_ref, l0, lp, ln, pad, d, r).astype(BF16)
                v = _window(v_ref, l0, lp, ln, pad, d, r).astype(BF16)
                outs, lses = [], []
                for hh in range(2):
                    hm = lo if hh == 0 else jnp.logical_not(lo)
                    s = _dot_nt(jnp.where(hm, q, jnp.zeros_like(q)), k) + bias_ref[c, hh]
                    s = jnp.where(valid, s, NEG)
                    snk = sink_ref[2 * g + hh]
                    m = jnp.maximum(jnp.max(s, axis=1, keepdims=True), snk)
                    p = jnp.exp(s - m)
                    den = jnp.sum(p, axis=1, keepdims=True) + jnp.exp(snk - m)
                    outs.append(_dot(p.astype(BF16), v) / den)
                    lses.append(m + jnp.log(den))
                o_dst[_rows(l0, QBLOCK, d, r), :] = jnp.where(lo, outs[0], outs[1])
                l_dst[_rows(l0, QBLOCK, d, r), :] = jnp.where(lo, lses[0], lses[1])
                return carry

            lax.fori_loop(0, n_blocks, block, 0, unroll=FWD_UNROLL)

        if n_br > 1:
            def merge(i, carry):
                rs = pl.ds(pl.multiple_of(i * chunk, chunk), chunk)
                ls = [scratch[1][c, rs, :] for c in range(n_br)]
                m = ls[0]
                for t in ls[1:]:
                    m = jnp.maximum(m, t)
                ws = [jnp.exp(t - m) for t in ls]
                z = ws[0]
                acc = ws[0] * scratch[0][0, rs, :]
                for c in range(1, n_br):
                    z = z + ws[c]
                    acc = acc + ws[c] * scratch[0][c, rs, :]
                o_ref[rs, :] = acc / z
                lse_ref[rs, :] = m + jnp.log(z)
                return carry

            lax.fori_loop(0, T // chunk, merge, 0)

    def col(base):
        return pl.BlockSpec((T, LANES), lambda g: (0, base + g))

    out = jax.ShapeDtypeStruct((T, hw), F32)
    scratch = [pltpu.VMEM((n_br, T, LANES), F32)] * 2 if n_br > 1 else []
    return pl.pallas_call(
        body, name="attn_fwd", grid=(ng,),
        in_specs=[pl.BlockSpec(memory_space=pltpu.SMEM), col(0), col(ng), col(2 * ng),
                  pl.BlockSpec((n_br, 2, QBLOCK, W), lambda g: (0, g, 0, 0))],
        out_specs=[col(0), col(0)], out_shape=[out, out], scratch_shapes=scratch,
        compiler_params=_params(1))(sink, qkv, qkv, qkv, bias)


def _attn_bwd(qkv, bias, sink, dils, pad, do, lse, dd, col_base):
    T = qkv.shape[0]
    hw = qkv.shape[1] // 3
    ng = hw // LANES
    n_br = len(dils)
    n_blocks = T // QBLOCK
    W = QBLOCK + 2 * pad

    def body(sink_ref, q_ref, k_ref, v_ref, bias_ref, do_ref, lse_ref, dd_ref,
             dq_ref, dk_ref, dv_ref, dt_ref, ds_ref):
        g = pl.program_id(0)
        dq_ref[...] = jnp.zeros_like(dq_ref)
        dk_ref[...] = jnp.zeros_like(dk_ref)
        dv_ref[...] = jnp.zeros_like(dv_ref)
        dt_ref[...] = jnp.zeros_like(dt_ref)
        ds_ref[...] = jnp.zeros_like(ds_ref)
        lo = _lo_lanes((QBLOCK, LANES))
        snk = jnp.where(lo, sink_ref[2 * g], sink_ref[2 * g + 1])
        for c, d in enumerate(dils):
            nb_sub = n_blocks // d

            def block(b, carry, c=c, d=d, nb_sub=nb_sub):
                r, l0, lp, ln, valid = _block_geometry(b, nb_sub, pad)
                rows_q = _rows(l0, QBLOCK, d, r)
                q = q_ref[rows_q, :].astype(BF16)
                k = _window(k_ref, l0, lp, ln, pad, d, r).astype(BF16)
                v = _window(v_ref, l0, lp, ln, pad, d, r).astype(BF16)
                dob = do_ref[rows_q, :].astype(BF16)
                lse_b = lse_ref[rows_q, :]
                dd_b = dd_ref[rows_q, :]
                dqs = []
                dkw = jnp.zeros((W, LANES), F32)
                dvw = jnp.zeros((W, LANES), F32)
                for hh in range(2):
                    hm = lo if hh == 0 else jnp.logical_not(lo)
                    qm = jnp.where(hm, q, jnp.zeros_like(q))
                    dom = jnp.where(hm, dob, jnp.zeros_like(dob))
                    s = _dot_nt(qm, k) + bias_ref[c, hh]
                    s = jnp.where(valid, s, NEG)
                    p = jnp.exp(s - lse_b[:, hh * HEAD_DIM:hh * HEAD_DIM + 1])
                    ds = p * (_dot_nt(dom, v) - dd_b[:, hh * HEAD_DIM:hh * HEAD_DIM + 1])
                    dsb = ds.astype(BF16)
                    dqs.append(_dot(dsb, k))
                    dkw = dkw + _dot_tn(dsb, qm)
                    dvw = dvw + _dot_tn(p.astype(BF16), dom)
                    dt_ref[c, hh] += ds
                dq_ref[rows_q, :] += jnp.where(lo, dqs[0], dqs[1])
                ds_ref[0:1, :] += jnp.sum(-jnp.exp(snk - lse_b) * dd_b, axis=0, keepdims=True)
                for part, (start, n) in zip((0, pad, pad + QBLOCK), ((lp, pad), (l0, QBLOCK), (ln, pad))):
                    dk_ref[_rows(start, n, d, r), :] += dkw[part:part + n]
                    dv_ref[_rows(start, n, d, r), :] += dvw[part:part + n]
                return carry

            lax.fori_loop(0, n_blocks, block, 0, unroll=BWD_UNROLL)

    def col(base):
        return pl.BlockSpec((T, LANES), lambda g: (0, base + g))

    tile =pl.BlockSpec((n_br, 2, QBLOCK, W), lambda g: (0, g, 0, 0))
    full = jax.ShapeDtypeStruct((T, hw), F32)
    return pl.pallas_call(
        body, name="attn_bwd", grid=(ng,),
        in_specs=[pl.BlockSpec(memory_space=pltpu.SMEM), col(0), col(ng), col(2 * ng), tile,
                  col(col_base), col(0), col(col_base)],
        out_specs=[col(0), col(0), col(0), tile, pl.BlockSpec((None, 8, LANES), lambda g: (g, 0, 0))],
        out_shape=[full, full, full, jax.ShapeDtypeStruct((n_br, 2 * ng, QBLOCK, W), F32),
                   jax.ShapeDtypeStruct((ng, 8, LANES), F32)],
        compiler_params=_params(1))(sink, qkv, qkv, qkv, bias, do, lse, dd)


def _mix_bwd_in(dx, wo, o_list):
    T, D = dx.shape
    widths = [o.shape[1] for o in o_list]
    hw = sum(widths)
    n = len(o_list)
    tm = _pick_tm(T, 256)

    def body(*refs):
        dx_ref, w_ref, o_refs, do_ref, dd_ref = refs[0], refs[1], refs[2:2 + n], refs[2 + n], refs[3 + n]
        dov = _dot_nt(dx_ref[...].astype(BF16), w_ref[...])
        do_ref[...] = dov
        off = 0
        for o_ref, k in zip(o_refs, widths):
            prod = dov[:, off:off + k] * o_ref[...]
            for cb in range(k // LANES):
                dd_ref[:, off + cb * LANES:off + (cb + 1) * LANES] = _seg_sum(prod[:, cb * LANES:(cb + 1) * LANES])
            off += k

    row = pl.BlockSpec((tm, hw), lambda i: (i, 0))
    out = jax.ShapeDtypeStruct((T, hw), F32)
    return pl.pallas_call(
        body, name="mix_bwd_in", grid=(T // tm,),
        in_specs=[pl.BlockSpec((tm, D), lambda i: (i, 0)), pl.BlockSpec((hw, D), lambda i: (0, 0))]
        + [pl.BlockSpec((tm, k), lambda i: (i, 0)) for k in widths],
        out_specs=[row, row], out_shape=[out, out], compiler_params=_params(1))(dx, wo, *o_list)


def _mesh_pos():
    return lax.axis_index("x"), lax.axis_index("y"), lax.axis_index("c")


def _my_chip():
    return 2 * lax.axis_index("x") + lax.axis_index("y")


def _other_chips(x, y):
    return [(1 - x, y), (x, 1 - y), (1 - x, 1 - y)]


def _half_rows(rows, cc):
    hr = rows // 2
    return pl.ds(pl.multiple_of(cc * hr, 16), hr)


def _cast_into_slot(w, l):
    _, R, C = w.shape
    tm = _pick_tm(R, 512)

    def body(w_ref, o_ref):
        o_ref[...] = w_ref[...].astype(BF16)

    return pl.pallas_call(
        body, name="cast_into_slot", grid=(R // tm,),
        in_specs=[pl.BlockSpec((None, tm, C), lambda i: (l, i, 0))],
        out_specs=pl.BlockSpec((None, tm, C), lambda i: (_my_chip(), i, 0)),
        out_shape=jax.ShapeDtypeStruct((N_CHIPS, R, C), BF16), compiler_params=_params(1))(w)


def _split_start(name, arrays, make_copies, n_sem, after):
    n = len(arrays)
    n_in = n + (0 if after is None else 1)

    def body(*refs):
        send_s, recv_s = refs[n_in], refs[n_in + 1]
        token = refs[n_in + 2 + n]
        for send, _ in make_copies(refs[:n], send_s, recv_s):
            send.start()
        token[...] = jnp.zeros_like(token)

    res = pl.pallas_call(
        body, name=name,
        out_shape=(pltpu.SemaphoreType.DMA((n_sem,)), pltpu.SemaphoreType.DMA((n_sem,)),
                   *[pltpu.HBM(a.shape, a.dtype) for a in arrays], jax.ShapeDtypeStruct((8, LANES), F32)),
        in_specs=[HBM_SPEC] * n + [ANY_SPEC] * (n_in - n),
        out_specs=(SEM_SPEC, SEM_SPEC, *([HBM_SPEC] * n), pl.BlockSpec(memory_space=pltpu.VMEM)),
        input_output_aliases={i: 2 + i for i in range(n)},
        compiler_params=pltpu.CompilerParams(has_side_effects=pltpu.SideEffectType.DATAFLOW_SIDE_EFFECTING),
    )(*[pltpu.with_memory_space_constraint(a, pltpu.HBM) for a in arrays], *([] if after is None else [after]))
    return res[0], res[1], list(res[2:2 + n]), res[2 + n]


def _split_wait(name, send_s, recv_s, arrays, make_copies, after):
    n = len(arrays)

    def body(*refs):
        for send, landed in make_copies(refs[:n], refs[n], refs[n + 1]):
            send.wait_send()
            landed.wait_recv()

    return list(pl.pallas_call(
        body, name=name, out_shape=[pltpu.HBM(a.shape, a.dtype) for a in arrays],
        in_specs=[HBM_SPEC] * n + [SEM_SPEC, SEM_SPEC, ANY_SPEC], out_specs=[HBM_SPEC] * n,
        input_output_aliases={i: i for i in range(n)},
        compiler_params=pltpu.CompilerParams(has_side_effects=pltpu.SideEffectType.DATAFLOW_SIDE_EFFECTING),
    )(*arrays, send_s, recv_s, after))


def _gather_copies(shapes):
    n = len(shapes)

    def make(refs, send_s, recv_s):
        x, y, c = _mesh_pos()
        my = 2 * x + y
        copies = []
        for w in range(n):
            for k, (px, py) in enumerate(_other_chips(x, y)):
                def part(slot, w=w):
                    return refs[w].at[slot, _half_rows(shapes[w][1], c), :]
                sems = dict(send_sem=send_s.at[k * n + w], recv_sem=recv_s.at[k * n + w],
                            device_id=(px, py, c), device_id_type=MESH)
                copies.append((pltpu.make_async_remote_copy(src_ref=part(my), dst_ref=part(my), **sems),
                               pltpu.make_async_remote_copy(src_ref=part(2 * px + py), dst_ref=part(2 * px + py), **sems)))
        return copies

    return make


def _pair_forward(bufs):
    n = len(bufs)

    def body(*refs):
        arr = refs[n:2 * n]
        send_s, recv_s = refs[2 * n:]
        x, y, c = _mesh_pos()
        sent, landed = [], []
        for w in range(n):
            for k, (px, py) in enumerate(_other_chips(x, y)):
                def part(cc, w=w, slot=2 * px + py):
                    return arr[w].at[slot, _half_rows(bufs[w].shape[1], cc), :]
                sems = dict(send_sem=send_s.at[k * n + w], recv_sem=recv_s.at[k * n + w],
                            device_id=(x, y, 1 - c), device_id_type=MESH)
                sent.append(pltpu.make_async_remote_copy(src_ref=part(c), dst_ref=part(c), **sems))
                landed.append(pltpu.make_async_remote_copy(src_ref=part(1 - c), dst_ref=part(1 - c), **sems))
        for cp in sent:
            cp.start()
        for cp in landed:
            cp.wait_recv()
        for cp in sent:
            cp.wait_send()

    return list(pl.pallas_call(
        body, name="ag_pair_forward", in_specs=[HBM_SPEC] * n, out_specs=[HBM_SPEC] * n,
        out_shape=[jax.ShapeDtypeStruct(b.shape, b.dtype) for b in bufs],
        input_output_aliases={w: w for w in range(n)},
        scratch_shapes=[pltpu.SemaphoreType.DMA((3 * n,)), pltpu.SemaphoreType.DMA((3 * n,))],
    )(*bufs))


def _rs_pair_exchange(grads):
    n = len(grads)

    def body(*refs):
        ins, recv = refs[:n], refs[n:2 * n]
        send_s, recv_s = refs[2 * n:]
        x, y, c = _mesh_pos()
        remote = [pltpu.make_async_remote_copy(
            src_ref=ins[t].at[:, _half_rows(grads[t].shape[1], 1 - c), :], dst_ref=recv[t],
            send_sem=send_s.at[t], recv_sem=recv_s.at[t], device_id=(x, y, 1 - c), device_id_type=MESH)
            for t in range(n)]
        for cp in remote:
            cp.start()
        for cp in remote:
            cp.wait_recv()
        for cp in remote:
            cp.wait_send()

    return pl.pallas_call(
        body, name="rs_pair_exchange", in_specs=[HBM_SPEC] * n, out_specs=[HBM_SPEC] * n,
        out_shape=[jax.ShapeDtypeStruct((g.shape[0], g.shape[1] // 2, g.shape[2]), g.dtype) for g in grads],
        scratch_shapes=[pltpu.SemaphoreType.DMA((n,)), pltpu.SemaphoreType.DMA((n,))],
    )(*grads)


def _rs_add_pair(grad, recv):
    n_slot, hr, C = recv.shape
    tm = _pick_tm(hr, 256)
    nb = hr // tm

    def body(a_ref, b_ref, o_ref):
        o_ref[...] = (a_ref[...].astype(F32) + b_ref[...].astype(F32)).astype(BF16)

    blk = pl.BlockSpec((None, tm, C), lambda k, i: (k, i, 0))
    return pl.pallas_call(
        body, name="rs_add_pair", grid=(n_slot, nb),
        in_specs=[pl.BlockSpec((None, tm, C), lambda k, i: (k, lax.axis_index("c") * nb + i, 0)), blk],
        out_specs=blk, out_shape=jax.ShapeDtypeStruct(recv.shape, BF16), compiler_params=_params(2))(grad, recv)


def _scatter_copies(n):
    def make(refs, send_s, recv_s):
        x, y, c = _mesh_pos()
        copies = []
        for t in range(n):
            for k, (px, py) in enumerate(_other_chips(x, y)):
                sems = dict(send_sem=send_s.at[3 * t + k], recv_sem=recv_s.at[3 * t + k],
                            device_id=(px, py, c), device_id_type=MESH)
                land = refs[n + t].at[k]
                copies.append((pltpu.make_async_remote_copy(src_ref=refs[t].at[2 * px + py], dst_ref=land, **sems),
                               pltpu.make_async_remote_copy(src_ref=land, dst_ref=land, **sems)))
        return copies

    return make


def _rs_add_chips(part, recv):
    _, hr, C = part.shape
    tm = _pick_tm(hr, 256)
    nb = hr // tm

    def body(a_ref, r0, r1, r2, o_ref):
        o_ref[...] = ((a_ref[...].astype(F32) + r0[...].astype(F32)) + r1[...].astype(F32)) + r2[...].astype(F32)

    def rel(k):
        return pl.BlockSpec((None, tm, C), lambda i: (k, i, 0))

    return pl.pallas_call(
        body, name="rs_add_chips", grid=(nb,),
        in_specs=[pl.BlockSpec((None, tm, C), lambda i: (_my_chip(), i, 0)), rel(0), rel(1), rel(2)],
        out_specs=pl.BlockSpec((tm, C), lambda i: (lax.axis_index("c") * nb + i, 0)),
        out_shape=jax.ShapeDtypeStruct((2 * hr, C), F32), compiler_params=_params(1))(part, recv, recv, recv)


def _rs_pair_share(halves):
    n = len(halves)

    def body(*refs):
        bufs = refs[n:2 * n]
        send_s, recv_s = refs[2 * n:]
        x, y, c = _mesh_pos()

        def half(t, cc):
            return bufs[t].at[_half_rows(halves[t].shape[0], cc), :]

        sent = [pltpu.make_async_remote_copy(
            src_ref=half(t, c), dst_ref=half(t, c), send_sem=send_s.at[t], recv_sem=recv_s.at[t],
            device_id=(x, y, 1 - c), device_id_type=MESH) for t in range(n)]
        landed = [pltpu.make_async_remote_copy(
            src_ref=half(t, 1 - c), dst_ref=half(t, 1 - c), send_sem=send_s.at[t], recv_sem=recv_s.at[t],
            device_id=(x, y, 1 - c), device_id_type=MESH) for t in range(n)]
        for cp in sent:
            cp.start()
        for cp in landed:
            cp.wait_recv()
        for cp in sent:
            cp.wait_send()

    return pl.pallas_call(
        body, name="rs_pair_share", in_specs=[HBM_SPEC] * n, out_specs=[HBM_SPEC] * n,
        out_shape=[jax.ShapeDtypeStruct(h.shape, h.dtype) for h in halves],
        input_output_aliases={t: t for t in range(n)},
        scratch_shapes=[pltpu.SemaphoreType.DMA((n,)), pltpu.SemaphoreType.DMA((n,))],
    )(*halves)


def _allreduce_small(v):
    rows = v.shape[0]

    def body(v_ref, o_ref, buf, send_s, recv_s):
        x, y, c = _mesh_pos()
        me = 4 * x + 2 * y + c
        buf[me] = v_ref[...]
        copies = []
        for r in range(1, N_DEV):
            px = 1 - x if r & 4 else x
            py = 1 - y if r & 2 else y
            pc = 1 - c if r & 1 else c
            send = pltpu.make_async_remote_copy(
                src_ref=v_ref, dst_ref=buf.at[me], send_sem=send_s.at[r - 1], recv_sem=recv_s.at[r - 1],
                device_id=(px, py, pc), device_id_type=MESH)
            peer_slot = buf.at[4 * px + 2 * py + pc]
            landed = pltpu.make_async_remote_copy(
                src_ref=peer_slot, dst_ref=peer_slot, send_sem=send_s.at[r - 1], recv_sem=recv_s.at[r - 1],
                device_id=(px, py, pc), device_id_type=MESH)
            copies.append((send, landed))
        for send, _ in copies:
            send.start()
        for _, landed in copies:
            landed.wait_recv()
        for send, _ in copies:
            send.wait_send()
        acc = buf[0]
        for j in range(1, N_DEV):
            acc = acc + buf[j]
        o_ref[...] = acc

    vm = pl.BlockSpec(memory_space=pltpu.VMEM)
    return pl.pallas_call(
        body, name="allreduce_small", in_specs=[vm], out_specs=vm,
        out_shape=jax.ShapeDtypeStruct((rows, LANES), F32),
        scratch_shapes=[pltpu.VMEM((N_DEV, rows, LANES), F32), pltpu.SemaphoreType.DMA((N_DEV - 1,)),
                        pltpu.SemaphoreType.DMA((N_DEV - 1,))],
    )(v)


def _adamw_fn(w, g, m, v):
    m2 = ADAM_B1 * m + (1.0 - ADAM_B1) * g
    v2 = ADAM_B2 * v + (1.0 - ADAM_B2) * (g * g)
    m_hat = m2 / (1.0 - ADAM_B1 ** ADAM_STEP)
    v_hat = v2 / (1.0 - ADAM_B2 ** ADAM_STEP)
    delta = -ADAM_LR * (m_hat / (jnp.sqrt(v_hat) + ADAM_EPS) + ADAM_WD * w)
    return g, delta, m2, v2


def _adamw_layer(w, g, m, v, l, prev):
    NL, R, C = w.shape
    tm = _pick_tm(R, 128)
    n_prev = 0 if prev is None else 4

    def body(w_ref, g_ref, m_ref, v_ref, *rest):
        outs = rest[n_prev:]
        for o_ref, val in zip(outs, _adamw_fn(w_ref[...], g_ref[...], m_ref[...], v_ref[...])):
            o_ref[...] = val

    lay = pl.BlockSpec((None, tm, C), lambda i: (l, i, 0))
    shape = jax.ShapeDtypeStruct((NL, R, C), F32)
    return pl.pallas_call(
        body, name="adamw", grid=(R // tm,),
        in_specs=[lay, pl.BlockSpec((tm, C), lambda i: (i, 0)), lay, lay] + [pl.BlockSpec(memory_space=pl.ANY)] * n_prev,
        out_specs=[lay] * 4, out_shape=[shape] * 4,
        input_output_aliases={4 + j: j for j in range(n_prev)},
        compiler_params=_params(1))(w, g, m, v, *(prev or []))


def _pack_small(parts):
    out = []
    for a in parts:
        flat = a.reshape(-1)
        n = -(-flat.shape[0] // (8 * LANES)) * 8 * LANES
        out.append(jnp.pad(flat, (0, n - flat.shape[0])).reshape(-1, LANES))
    return jnp.concatenate(out, axis=0)


def _unpack_small(packed, like):
    out, r = [], 0
    for a in like:
        size = int(np.prod(a.shape))
        rows = -(-size // (8 * LANES)) * 8
        out.append(packed[r:r + rows].reshape(-1)[:size].reshape(a.shape))
        r += rows
    return out


def _ffn_forward(x, g, win, wout):
    h = _rms_fwd(x, g)
    gate, up, act = _ffn_up(h, win)
    return _mm_res("ffn_down", x, [act], wout, 0.5), (x, h, gate, up, act)


def _ffn_backward(dx, saved, g, win, wout, after):
    x, h, gate, up, act = saved
    D = x.shape[1]
    wc = win.shape[2]
    dgu = _ffn_bwd_a(dx, wout, gate, up, after)
    dwout = _mm_tn("dw_ffn_out", act, wc, dx, D, 2, True, False, 0.5)
    dwin = _mm_tn("dw_ffn_in", h, D, dgu, wc, 4, False, True, 1.0)
    dx_in, dg = _bwd_into_norm("ffn_bwd_b", dgu, wc, win, 4, x, g, dx)
    return dx_in, dg, dwin, dwout.reshape(N_CHIPS, -1, D)


GROUP_FFN1 = ("ffn1_w_in", "ffn1_w_out")
GROUP_REST = ("w_qkv", "w_o", "ffn2_w_in", "ffn2_w_out", "w_ple_gate", "w_ple_proj")


def _gather_start(tag, slotted, after):
    return _split_start("ag_start_" + tag, slotted, _gather_copies([a.shape for a in slotted]), 3 * len(slotted), after)


def _gather_finish(tag, started, after):
    send_s, recv_s, arrays, _ = started
    return _pair_forward(_split_wait("ag_wait_" + tag, send_s, recv_s, arrays,
                                     _gather_copies([a.shape for a in arrays]), after))


def _scatter_start(tag, grads):
    n = len(grads)
    part = [_rs_add_pair(g_, r_) for g_, r_ in zip(grads, _rs_pair_exchange(grads))]
    land = [lax.empty((3,) + p_.shape[1:], p_.dtype) for p_ in part]
    return _split_start("rs_start_" + tag, part + land, _scatter_copies(n), 3 * n, None)


def _scatter_finish(tag, started, after):
    send_s, recv_s, arrays, _ = started
    n = len(arrays) // 2
    arrays = _split_wait("rs_wait_" + tag, send_s, recv_s, arrays, _scatter_copies(n), after)
    return _rs_pair_share([_rs_add_chips(p_, r_) for p_, r_ in zip(arrays[:n], arrays[n:])])


def kernel(x, p, rel_bias, norm_ffn1, ffn1_w_in, ffn1_w_out, norm_mix, w_qkv, q_norm_a, k_norm_a, q_norm_b, k_norm_b, sink_b, w_o, norm_ffn2, ffn2_w_in, ffn2_w_out, norm_ple, w_ple_gate, w_ple_proj, loss_target, m_rel_bias, m_norm_ffn1, m_ffn1_w_in, m_ffn1_w_out, m_norm_mix, m_w_qkv, m_q_norm_a, m_k_norm_a, m_q_norm_b, m_k_norm_b, m_sink_b, m_w_o, m_norm_ffn2, m_ffn2_w_in, m_ffn2_w_out, m_norm_ple, m_w_ple_gate, m_w_ple_proj, v_rel_bias, v_norm_ffn1, v_ffn1_w_in, v_ffn1_w_out, v_norm_mix, v_w_qkv, v_q_norm_a, v_k_norm_a, v_q_norm_b, v_k_norm_b, v_sink_b, v_w_o, v_norm_ffn2, v_ffn2_w_in, v_ffn2_w_out, v_norm_ple, v_w_ple_gate, v_w_ple_proj):
    given = dict(locals())
    T, D = x.shape[1], x.shape[2]
    NL = norm_ffn1.shape[0]
    x0 = x.reshape(T, D)
    tgt = loss_target.reshape(T, D)
    n_a, n_b, n_kv = _qkv_layout(D)

    assert NL == 2
    slot = [{name: _cast_into_slot(given[name], l) for name in BIG} for l in range(NL)]
    ag_a = _gather_start("a", [slot[0][n] for n in GROUP_FFN1], None)
    ag_b = _gather_start("b", [slot[0][n] for n in GROUP_REST], ag_a[3])
    ag_1 = _gather_start("1", [slot[1][n] for n in BIG], ag_b[3])

    def by_rows(a):
        return a.reshape(-1, a.shape[-1])

    def by_cols(a):
        return a.transpose(1, 0, 2).reshape(a.shape[1], -1)

    QW = N_CHIPS * w_qkv.shape[2]

    dils = tuple(d for _, d in DILATED_CONFIGS)
    cfg_a = [(w // (2 * d), d) for w, d in DILATED_CONFIGS]
    pad_a, pad_b = _window_pad(cfg_a[0][0]), _window_pad(SWA_RADIUS)
    bmaps_a, bmaps_b = jnp.asarray(_bucket_maps(cfg_a)), jnp.asarray(_bucket_maps([(SWA_RADIUS, 1)]))
    n_heads = rel_bias.shape[1] // 2
    bias_a = _bias_build(rel_bias, bmaps_a, 0)
    bias_b = _bias_build(rel_bias, bmaps_b, n_heads)
    no_sink = jnp.full((n_heads,), NEG, F32)

    def gains_row(l):
        ones = jnp.ones((n_a * LANES,), F32)
        return jnp.concatenate([
            jnp.tile(q_norm_a[l], 2 * n_a), jnp.tile(k_norm_a[l], 2 * n_a), ones,
            jnp.tile(q_norm_b[l], 2 * n_b), jnp.tile(k_norm_b[l], 2 * n_kv), jnp.ones((n_kv * LANES,), F32)]).reshape(1, QW)

    saved, weights = [], []
    xc = x0
    for l in range(NL):
        s, w = {}, {}
        if l == 0:
            w.update(zip(GROUP_FFN1, _gather_finish("a", ag_a, ag_1[3])))
        else:
            w.update(zip(BIG, _gather_finish("1", ag_1, xc)))
        w["ffn1_w_out"] = by_rows(w["ffn1_w_out"])
        xc, s["ffn1"] = _ffn_forward(xc, norm_ffn1[l:l + 1], w["ffn1_w_in"], w["ffn1_w_out"])
        s["x1"] = xc
        if l == 0:
            w.update(zip(GROUP_REST, _gather_finish("b", ag_b, xc)))
        w["w_qkv"] = by_cols(w["w_qkv"])
        w["w_qkv4"] = w["w_qkv"].reshape(1, D, QW)
        w["w_ple_proj"] = by_cols(w["w_ple_proj"])
        for name in ("w_o", "ffn2_w_out", "w_ple_gate"):
            w[name] = by_rows(w[name])
        h2 = _rms_fwd(xc, norm_mix[l:l + 1])
        raw = _mm_plain("qkv_proj", h2, w["w_qkv"])
        s["h2"], s["raw"] = h2, raw
        s["qkv_a"], s["qkv_b"] = _qkv_post(raw, gains_row(l))
        s["o_a"], s["lse_a"] = _attn_fwd(s["qkv_a"], bias_a, no_sink, dils, pad_a)
        s["o_b"], s["lse_b"] = _attn_fwd(s["qkv_b"], bias_b, sink_b[l], (1,), pad_b)
        xc = _mm_res("attn_out", xc, [s["o_a"], s["o_b"]], w["w_o"], 1.0)
        xc, s["ffn2"] = _ffn_forward(xc, norm_ffn2[l:l + 1], w["ffn2_w_in"], w["ffn2_w_out"])
        s["x3"] = xc
        s["hn"] = _rms_fwd(xc, norm_ple[l:l + 1])
        s["p"] = p[l].reshape(T, -1)
        xc, s["gate"], s["pp"] = _ple_fwd(xc, s["hn"], s["p"], w["w_ple_gate"], w["w_ple_proj"])
        saved.append(s)
        weights.append(w)

    dx, loss_blk = _loss_fwd_bwd(xc, tgt)
    loss = lax.psum(loss_blk[0, 0], ("x", "y", "c"))

    gs = {name: [None] * NL for name in SMALL if name != "rel_bias"}
    dt_a, dt_b = [], []
    g_full = {}
    rs_1 = rs_0a = None
    for l in reversed(range(NL)):
        s, w, gw = saved[l], weights[l], {}
        dx, gs["norm_ple"][l], dwg, dwp = _ple_bwd(dx, s["gate"], s["pp"], s["hn"], s["p"], s["x3"],
                                                   norm_ple[l:l + 1], w["w_ple_gate"],
                                                   loss_blk if rs_1 is None else rs_1[3])
        gw["w_ple_gate"] = dwg.reshape(N_CHIPS, -1, D)
        gw["w_ple_proj"] = dwp.reshape(dwp.shape[0], N_CHIPS, -1).transpose(1, 0, 2)
        dx, gs["norm_ffn2"][l], gw["ffn2_w_in"], gw["ffn2_w_out"] = _ffn_backward(
            dx, s["ffn2"], norm_ffn2[l:l + 1], w["ffn2_w_in"], w["ffn2_w_out"], dx)
        do, dd = _mix_bwd_in(dx, w["w_o"], [s["o_a"], s["o_b"]])
        hwa = s["o_a"].shape[1]
        gw["w_o"] = jnp.concatenate([
            _mm_tn("dw_o", o_, o_.shape[1], dx, D, 1, False, False, 1.0).reshape(-1, D // N_CHIPS, D)
            for o_ in (s["o_a"], s["o_b"])], axis=0)
        dqa, dka, dva, dt, _ = _attn_bwd(s["qkv_a"], bias_a, no_sink, dils, pad_a, do, s["lse_a"], dd, 0)
        dt_a.append(dt)
        dqb, dkb, dvb, dt, dsink = _attn_bwd(s["qkv_b"], bias_b, sink_b[l], (1,), pad_b, do, s["lse_b"], dd,
                                             hwa // LANES)
        dt_b.append(dt)
        gs["sink_b"][l] = dsink[:, 0, ::HEAD_DIM].reshape(-1)
        draw, dgains = _qkv_post_bwd(s["raw"], gains_row(l), (dqa, dka, dva), (dqb, dkb, dvb))
        dgv = dgains.reshape(-1, HEAD_DIM)
        gs["q_norm_a"][l] = dgv[:2 * n_a].sum(0)
        gs["k_norm_a"][l] = dgv[2 * n_a:4 * n_a].sum(0)
        gs["q_norm_b"][l] = dgv[6 * n_a:6 * n_a + 2 * n_b].sum(0)
        gs["k_norm_b"][l] = dgv[6 * n_a + 2 * n_b:6 * n_a + 2 * n_b + 2 * n_kv].sum(0)
        dwqkv = _mm_tn("dw_qkv", s["h2"], D, draw, QW, 1, False, False, 1.0)
        gw["w_qkv"] = dwqkv.reshape(D, N_CHIPS, -1).transpose(1, 0, 2)
        dx, gs["norm_mix"][l] = _bwd_into_norm("qkv_bwd_b", draw, QW, w["w_qkv4"], 1, s["x1"], norm_mix[l:l + 1], dx)
        if l > 0:
            dx, gs["norm_ffn1"][l], gw["ffn1_w_in"], gw["ffn1_w_out"] = _ffn_backward(
                dx, s["ffn1"], norm_ffn1[l:l + 1], w["ffn1_w_in"], w["ffn1_w_out"], dx)
            rs_1 = _scatter_start("1", [gw[n] for n in BIG])
        else:
            rs_0a = _scatter_start("0a", [gw[n] for n in GROUP_REST])
            g_full.update(zip([(n, 1) for n in BIG], _scatter_finish("1", rs_1, rs_0a[3])))
            dx, gs["norm_ffn1"][l], gw["ffn1_w_in"], gw["ffn1_w_out"] = _ffn_backward(
                dx, s["ffn1"], norm_ffn1[l:l + 1], w["ffn1_w_in"], w["ffn1_w_out"], rs_0a[3])
            rs_0b = _scatter_start("0b", [gw[n] for n in GROUP_FFN1])
            done_0a = _scatter_finish("0a", rs_0a, rs_0b[3])
            g_full.update(zip([(n, 0) for n in GROUP_REST], done_0a))
            g_full.update(zip([(n, 0) for n in GROUP_FFN1], _scatter_finish("0b", rs_0b, done_0a[0])))
    grad_x = dx.reshape(x.shape)
    d_rel_bias = (_bias_grad(dt_a, bmaps_a, 0) + _bias_grad(dt_b, bmaps_b, n_heads))[:, :rel_bias.shape[1]]

    out = {}
    for name in BIG:
        res = None
        for l in reversed(range(NL)):
            res = _adamw_layer(given[name], g_full[(name, l)], given["m_" + name], given["v_" + name], l, res)
        out[name] = res

    small_g = [d_rel_bias] + [jnp.stack([t.reshape(-1) for t in gs[name]]) for name in SMALL[1:]]
    g_sum = _allreduce_small(_pack_small(small_g))
    res = _ew("adamw_small", _adamw_fn,
              [_pack_small([given[n] for n in SMALL]), g_sum, _pack_small([given["m_" + n] for n in SMALL]),
               _pack_small([given["v_" + n] for n in SMALL])], [(LANES, F32)] * 4)
    like = [given[n] for n in SMALL]
    unpacked = [_unpack_small(r, like) for r in res]
    for i, name in enumerate(SMALL):
        out[name] = [u[i] for u in unpacked]

    return (loss, grad_x, *[out[n][0] for n in WEIGHTS], *[out[n][1] for n in WEIGHTS],
            *[out[n][2] for n in WEIGHTS], *[out[n][3] for n in WEIGHTS])
```

```python
import functools
import math

import numpy as np
import jax
import jax.numpy as jnp
from jax import lax
from jax.experimental import pallas as pl
from jax.experimental.pallas import tpu as pltpu

F32 = jnp.float32
BF16 = jnp.bfloat16
MESH = pl.DeviceIdType.MESH

HEAD_DIM = 64
LANES = 128
QBLOCK = 128
FWD_UNROLL, BWD_UNROLL = 4, 4
N_BUCKETS = 32
MAX_DISTANCE = 1024
DILATED_CONFIGS = ((128, 1), (512, 4), (2048, 16))
SWA_RADIUS = 128
GROUP_B = 4
EPS = 1e-6
NEG = -1e30
Q_SCALE = HEAD_DIM ** -0.5
ADAM_LR, ADAM_B1, ADAM_B2, ADAM_EPS, ADAM_WD, ADAM_STEP = 0.001, 0.9, 0.999, 1e-08, 0.01, 10
VMEM_LIMIT = 56 * 2 ** 20
N_CHIPS = 4
N_DEV = 8

BIG = ("ffn1_w_in", "ffn1_w_out", "w_qkv", "w_o", "ffn2_w_in", "ffn2_w_out", "w_ple_gate", "w_ple_proj")
SMALL = ("rel_bias", "norm_ffn1", "norm_mix", "q_norm_a", "k_norm_a", "q_norm_b", "k_norm_b", "sink_b",
         "norm_ffn2", "norm_ple")
WEIGHTS = ("rel_bias", "norm_ffn1", "ffn1_w_in", "ffn1_w_out", "norm_mix", "w_qkv", "q_norm_a", "k_norm_a",
           "q_norm_b", "k_norm_b", "sink_b", "w_o", "norm_ffn2", "ffn2_w_in", "ffn2_w_out", "norm_ple",
           "w_ple_gate", "w_ple_proj")


HBM_SPEC = pl.BlockSpec(memory_space=pltpu.HBM)
ANY_SPEC = pl.BlockSpec(memory_space=pl.ANY)
SEM_SPEC = pl.BlockSpec(memory_space=pltpu.SEMAPHORE)


def _params(n_grid):
    return pltpu.CompilerParams(dimension_semantics=("arbitrary",) * n_grid, vmem_limit_bytes=VMEM_LIMIT)


def _pick_tm(rows, cap):
    t = (min(cap, rows) // 16) * 16
    while t >= 16:
        if rows % t == 0:
            return t
        t -= 16
    return rows


def _dot(a, b):
    return jnp.dot(a, b, preferred_element_type=F32)


def _dot_nt(a, b):
    return lax.dot_general(a, b, (((1,), (1,)), ((), ())), preferred_element_type=F32)


def _dot_tn(a, b):
    return lax.dot_general(a, b, (((0,), (0,)), ((), ())), preferred_element_type=F32)


def _sigmoid(z):
    return 1.0 / (1.0 + jnp.exp(-z))


def _lo_lanes(shape):
    return lax.broadcasted_iota(jnp.int32, shape, len(shape) - 1) % LANES < HEAD_DIM


def _seg_sum(blk):
    lo = _lo_lanes(blk.shape)
    s_lo = jnp.sum(jnp.where(lo, blk, 0.0), axis=1, keepdims=True)
    s_hi = jnp.sum(jnp.where(lo, 0.0, blk), axis=1, keepdims=True)
    return jnp.where(lo, s_lo, s_hi)


def _rms_bwd_tile(x, g, dh):
    r = lax.rsqrt(jnp.mean(x * x, axis=-1, keepdims=True) + EPS)
    xh = x * r
    dyg = dh * g
    dx = r * (dyg - xh * jnp.mean(dyg * xh, axis=-1, keepdims=True))
    return dx, jnp.sum(dh * xh, axis=0, keepdims=True)


def _ew(name, fn, ins, out_defs, cap=512):
    rows = ins[0].shape[0]
    tm = _pick_tm(rows, cap)
    n_in = len(ins)

    def body(*refs):
        vals = fn(*[r[...] for r in refs[:n_in]])
        if not isinstance(vals, tuple):
            vals = (vals,)
        for r, v in zip(refs[n_in:], vals):
            r[...] = v.astype(r.dtype)

    return pl.pallas_call(
        body, name=name, grid=(rows // tm,),
        in_specs=[pl.BlockSpec((tm, a.shape[1]), lambda i: (i, 0)) for a in ins],
        out_specs=[pl.BlockSpec((tm, c), lambda i: (i, 0)) for c, _ in out_defs],
        out_shape=[jax.ShapeDtypeStruct((rows, c), dt) for c, dt in out_defs],
        compiler_params=_params(1))(*ins)


def _rms_fwd(x, g):
    T, D = x.shape
    tm = _pick_tm(T, 512)

    def body(x_ref, g_ref, h_ref):
        xv = x_ref[...]
        r = lax.rsqrt(jnp.mean(xv * xv, axis=-1, keepdims=True) + EPS)
        h_ref[...] = (xv * r * g_ref[...]).astype(BF16)

    return pl.pallas_call(
        body, name="rms_fwd", grid=(T // tm,),
        in_specs=[pl.BlockSpec((tm, D), lambda i: (i, 0)), pl.BlockSpec((1, D), lambda i: (0, 0))],
        out_specs=pl.BlockSpec((tm, D), lambda i: (i, 0)),
        out_shape=jax.ShapeDtypeStruct((T, D), BF16), compiler_params=_params(1))(x, g)


def _ffn_up(h, win):
    T, D = h.shape
    wc = win.shape[2]
    tm = _pick_tm(T, 512)

    def body(h_ref, wg_ref, wu_ref, silu_ref, dsilu_ref, up_ref, act_ref):
        hv = h_ref[...]
        gte = _dot(hv, wg_ref[...])
        u = _dot(hv, wu_ref[...])
        sg = _sigmoid(gte)
        silu = gte * sg
        silu_ref[...] = silu.astype(BF16)
        dsilu_ref[...] = (sg + silu * (1.0 - sg)).astype(BF16)
        up_ref[...] = u.astype(BF16)
        act_ref[...] = (silu * u).astype(BF16)

    out = jax.ShapeDtypeStruct((T, 2 * wc), BF16)
    ospec = pl.BlockSpec((tm, wc), lambda j, i: (i, j))
    return pl.pallas_call(
        body, name="ffn_up", grid=(2, T // tm),
        in_specs=[pl.BlockSpec((tm, D), lambda j, i: (i, 0)),
                  pl.BlockSpec((None, D, wc), lambda j, i: (j, 0, 0)),
                  pl.BlockSpec((None, D, wc), lambda j, i: (j + 2, 0, 0))],
        out_specs=[ospec] * 4, out_shape=[out] * 4, compiler_params=_params(2))(h, win, win)


def _mm_res(name, res, a_list, w, scale):
    T, N = res.shape
    n = len(a_list)
    widths = [a.shape[1] for a in a_list]
    tm = _pick_tm(T, 512)

    def body(*refs):
        r_ref, a_refs, w_refs, o_ref = refs[0], refs[1:1 + n], refs[1 + n:1 + 2 * n], refs[1 + 2 * n]
        acc = _dot(a_refs[0][...].astype(BF16), w_refs[0][...])
        for a_ref, w_ref in zip(a_refs[1:], w_refs[1:]):
            acc = acc + _dot(a_ref[...].astype(BF16), w_ref[...])
        o_ref[...] = r_ref[...] + scale * acc

    w_specs, off = [], 0
    for k in widths:
        w_specs.append(pl.BlockSpec((k, N), lambda i, blk=off // k: (blk, 0)))
        off += k
    return pl.pallas_call(
        body, name=name, grid=(T // tm,),
        in_specs=[pl.BlockSpec((tm, N), lambda i: (i, 0))]
        + [pl.BlockSpec((tm, k), lambda i: (i, 0)) for k in widths] + w_specs,
        out_specs=pl.BlockSpec((tm, N), lambda i: (i, 0)),
        out_shape=jax.ShapeDtypeStruct((T, N), F32), compiler_params=_params(1))(res, *a_list, *([w] * n))


def _mm_plain(name, a, w):
    T, K = a.shape
    N = w.shape[1]
    tm = _pick_tm(T, 512)

    def body(a_ref, w_ref, o_ref):
        o_ref[...] = _dot(a_ref[...], w_ref[...])

    return pl.pallas_call(
        body, name=name, grid=(T // tm,),
        in_specs=[pl.BlockSpec((tm, K), lambda i: (i, 0)), pl.BlockSpec((K, N), lambda i: (0, 0))],
        out_specs=pl.BlockSpec((tm, N), lambda i: (i, 0)),
        out_shape=jax.ShapeDtypeStruct((T, N), F32), compiler_params=_params(1))(a, w)


def _mm_tn(name, a, a_w, b, b_w, n_slots, a_by_slot, b_by_slot, scale, tm_cap=512):
    T = a.shape[0]
    tm = _pick_tm(T, tm_cap)
    nt = T // tm

    def body(a_ref, b_ref, o_ref, acc):
        i = pl.program_id(1)

        @pl.when(i == 0)
        def _():
            acc[...] = jnp.zeros_like(acc)

        acc[...] += _dot_tn(a_ref[...].astype(BF16), b_ref[...].astype(BF16))

        @pl.when(i == nt - 1)
        def _():
            o_ref[...] = (acc[...] * scale).astype(BF16)

    return pl.pallas_call(
        body, name=name, grid=(n_slots, nt),
        in_specs=[pl.BlockSpec((tm, a_w), (lambda s, i: (i, s)) if a_by_slot else (lambda s, i: (i, 0))),
                  pl.BlockSpec((tm, b_w), (lambda s, i: (i, s)) if b_by_slot else (lambda s, i: (i, 0)))],
        out_specs=pl.BlockSpec((None, a_w, b_w), lambda s, i: (s, 0, 0)),
        out_shape=jax.ShapeDtypeStruct((n_slots, a_w, b_w), BF16),
        scratch_shapes=[pltpu.VMEM((a_w, b_w), F32)], compiler_params=_params(2))(a, b)


def _ffn_bwd_a(dx, wout, silu, dsilu, up, after):
    T, D = dx.shape
    F = up.shape[1]
    tm = _pick_tm(T, 256)

    def body(dx_ref, w_ref, s_ref, ds_ref, u_ref, after_ref, o_ref):
        dact = 0.5 * _dot_nt(dx_ref[...].astype(BF16), w_ref[...])
        o_ref[:, :F] = (dact * u_ref[...].astype(F32) * ds_ref[...].astype(F32)).astype(BF16)
        o_ref[:, F:] = (dact * s_ref[...].astype(F32)).astype(BF16)

    act_spec = pl.BlockSpec((tm, F), lambda i: (i, 0))
    return pl.pallas_call(
        body, name="ffn_bwd_a", grid=(T // tm,),
        in_specs=[pl.BlockSpec((tm, D), lambda i: (i, 0)), pl.BlockSpec((F, D), lambda i: (0, 0)),
                  act_spec, act_spec, act_spec, ANY_SPEC],
        out_specs=pl.BlockSpec((tm, 2 * F), lambda i: (i, 0)),
        out_shape=jax.ShapeDtypeStruct((T, 2 * F), BF16),
        compiler_params=_params(1))(dx, wout, silu, dsilu, up, after)


def _bwd_into_norm(name, d, d_w, w, n_slots, x, g, dx_in):
    T, D = x.shape
    tm = _pick_tm(T, 256)

    def body(*refs):
        d_refs, w_refs = refs[:n_slots], refs[n_slots:2 * n_slots]
        x_ref, g_ref, dxi_ref, dx_ref, dg_ref = refs[2 * n_slots:]
        dh = _dot_nt(d_refs[0][...], w_refs[0][...])
        for s in range(1, n_slots):
            dh = dh + _dot_nt(d_refs[s][...], w_refs[s][...])
        dxn, dg = _rms_bwd_tile(x_ref[...], g_ref[...], dh)
        dx_ref[...] = dxi_ref[...] + dxn

        @pl.when(pl.program_id(0) == 0)
        def _():
            dg_ref[...] = jnp.zeros_like(dg_ref)

        dg_ref[...] += dg

    row = pl.BlockSpec((tm, D), lambda i: (i, 0))
    vec = pl.BlockSpec((1, D), lambda i: (0, 0))
    return pl.pallas_call(
        body, name=name, grid=(T // tm,),
        in_specs=[pl.BlockSpec((tm, d_w), lambda i, s=s: (i, s)) for s in range(n_slots)]
        + [pl.BlockSpec((None, D, d_w), lambda i, s=s: (s, 0, 0)) for s in range(n_slots)]
        + [row, vec, row],
        out_specs=[row, vec],
        out_shape=[jax.ShapeDtypeStruct((T, D), F32), jax.ShapeDtypeStruct((1, D), F32)],
        compiler_params=_params(1))(*([d] * n_slots + [w] * n_slots + [x, g, dx_in]))


def _ple_fwd(x, hn, p, wg, wp):
    T, D = x.shape
    P = p.shape[1]
    tm = _pick_tm(T, 256)

    def body(x_ref, hn_ref, p_ref, wg_ref, wp_ref, xo_ref, gate_ref, pp_ref):
        gate = _sigmoid(_dot(hn_ref[...], wg_ref[...]))
        pp = _dot(p_ref[...].astype(BF16), wp_ref[...])
        gate_ref[...] = gate
        pp_ref[...] = pp
        xo_ref[...] = x_ref[...] + gate * pp

    row = pl.BlockSpec((tm, D), lambda i: (i, 0))
    out = jax.ShapeDtypeStruct((T, D), F32)
    return pl.pallas_call(
        body, name="ple_fwd", grid=(T // tm,),
        in_specs=[row, row, pl.BlockSpec((tm, P), lambda i: (i, 0)),
                  pl.BlockSpec((D, D), lambda i: (0, 0)), pl.BlockSpec((P, D), lambda i: (0, 0))],
        out_specs=[row, row, row], out_shape=[out, out, out], compiler_params=_params(1))(x, hn, p, wg, wp)


def _ple_bwd(dx, gate, pp, hn, p, x, g, wg, after):
    T, D = x.shape
    P = p.shape[1]
    tm = _pick_tm(T, 256)
    nt = T // tm

    def body(dx_ref, gate_ref, pp_ref, hn_ref, p_ref, x_ref, g_ref, wg_ref, after_ref,
             dxo_ref, dg_ref, dwg_ref, dwp_ref, acc_g, acc_p):
        i = pl.program_id(0)

        @pl.when(i == 0)
        def _():
            acc_g[...] = jnp.zeros_like(acc_g)
            acc_p[...] = jnp.zeros_like(acc_p)
            dg_ref[...] = jnp.zeros_like(dg_ref)

        dxv = dx_ref[...]
        gate = gate_ref[...]
        dz = (dxv * pp_ref[...] * gate * (1.0 - gate)).astype(BF16)
        dpp = (dxv * gate).astype(BF16)
        acc_g[...] += _dot_tn(hn_ref[...], dz)
        acc_p[...] += _dot_tn(p_ref[...].astype(BF16), dpp)
        dxn, dg = _rms_bwd_tile(x_ref[...], g_ref[...], _dot_nt(dz, wg_ref[...]))
        dxo_ref[...] = dxv + dxn
        dg_ref[...] += dg

        @pl.when(i == nt - 1)
        def _():
            dwg_ref[...] = acc_g[...].astype(BF16)
            dwp_ref[...] = acc_p[...].astype(BF16)

    row = pl.BlockSpec((tm, D), lambda i: (i, 0))
    vec = pl.BlockSpec((1, D), lambda i: (0, 0))
    return pl.pallas_call(
        body, name="ple_bwd", grid=(nt,),
        in_specs=[row, row, row, row, pl.BlockSpec((tm, P), lambda i: (i, 0)), row, vec,
                  pl.BlockSpec((D, D), lambda i: (0, 0)), ANY_SPEC],
        out_specs=[row, vec, pl.BlockSpec((D, D), lambda i: (0, 0)), pl.BlockSpec((P, D), lambda i: (0, 0))],
        out_shape=[jax.ShapeDtypeStruct((T, D), F32), jax.ShapeDtypeStruct((1, D), F32),
                   jax.ShapeDtypeStruct((D, D), BF16), jax.ShapeDtypeStruct((P, D), BF16)],
        scratch_shapes=[pltpu.VMEM((D, D), F32), pltpu.VMEM((P, D), F32)],
        compiler_params=_params(1))(dx, gate, pp, hn, p, x, g, wg, after)


def _loss_fwd_bwd(y, tgt):
    T, D = y.shape
    tm = _pick_tm(T, 512)

    def body(y_ref, t_ref, dy_ref, loss_ref):
        e = y_ref[...] - t_ref[...]
        dy_ref[...] = e / D

        @pl.when(pl.program_id(0) == 0)
        def _():
            loss_ref[...] = jnp.zeros_like(loss_ref)

        loss_ref[...] += 0.5 * jnp.sum(jnp.mean(e * e, axis=-1, keepdims=True), axis=0, keepdims=True)

    row = pl.BlockSpec((tm, D), lambda i: (i, 0))
    return pl.pallas_call(
        body, name="loss", grid=(T // tm,), in_specs=[row, row],
        out_specs=[row, pl.BlockSpec((8, LANES), lambda i: (0, 0))],
        out_shape=[jax.ShapeDtypeStruct((T, D), F32), jax.ShapeDtypeStruct((8, LANES), F32)],
        compiler_params=_params(1))(y, tgt)


def _qkv_layout(D):
    n_a = D // (2 * LANES)
    n_b = D // (2 * LANES)
    n_kv = max(1, (2 * n_b) // GROUP_B) * HEAD_DIM // LANES
    return n_a, n_b, n_kv


def _dup_half(xv, half):
    rolled = pltpu.roll(xv, HEAD_DIM, 1)
    lo = _lo_lanes(xv.shape)
    return jnp.where(lo, xv, rolled) if half == 0 else jnp.where(lo, rolled, xv)


def _qkv_post(raw, gains):
    T, W = raw.shape
    n_a, n_b, n_kv = _qkv_layout(W * 4 // 9)
    tm = _pick_tm(T, 256)
    o_qb = 3 * n_a

    def norm(xv, gv, scale):
        ms = _seg_sum(xv * xv) * (1.0 / HEAD_DIM)
        return xv * lax.rsqrt(ms + EPS) * gv * scale

    def body(raw_ref, g_ref, a_ref, b_ref):
        def blk(cb):
            return raw_ref[:, cb * LANES:(cb + 1) * LANES]

        def gn(cb):
            return g_ref[:, cb * LANES:(cb + 1) * LANES]

        for cb in range(n_a):
            a_ref[:, cb * LANES:(cb + 1) * LANES] = norm(blk(cb), gn(cb), Q_SCALE)
            cbk = n_a + cb
            a_ref[:, cbk * LANES:(cbk + 1) * LANES] = norm(blk(cbk), gn(cbk), 1.0)
            cbv = 2 * n_a + cb
            a_ref[:, cbv * LANES:(cbv + 1) * LANES] = blk(cbv)
        for cb in range(n_b):
            src = o_qb + cb
            b_ref[:, cb * LANES:(cb + 1) * LANES] = norm(blk(src), gn(src), Q_SCALE)
        for e in range(n_b):
            kvh = (2 * e) // GROUP_B
            ck = o_qb + n_b + kvh // 2
            cv = ck + n_kv
            kn = norm(blk(ck), gn(ck), 1.0)
            b_ref[:, (n_b + e) * LANES:(n_b + e + 1) * LANES] = _dup_half(kn, kvh % 2)
            b_ref[:, (2 * n_b + e) * LANES:(2 * n_b + e + 1) * LANES] = _dup_half(blk(cv), kvh % 2)

    wa, wb = 3 * n_a * LANES, 3 * n_b * LANES
    return pl.pallas_call(
        body, name="qkv_post", grid=(T // tm,),
        in_specs=[pl.BlockSpec((tm, W), lambda i: (i, 0)), pl.BlockSpec((1, W), lambda i: (0, 0))],
        out_specs=[pl.BlockSpec((tm, wa), lambda i: (i, 0)), pl.BlockSpec((tm, wb), lambda i: (i, 0))],
        out_shape=[jax.ShapeDtypeStruct((T, wa), F32), jax.ShapeDtypeStruct((T, wb), F32)],
        compiler_params=_params(1))(raw, gains)


def _qkv_post_bwd(raw, gains, d_a, d_b):
    T, W = raw.shape
    n_a, n_b, n_kv = _qkv_layout(W * 4 // 9)
    tm = _pick_tm(T, 256)
    o_qb = 3 * n_a

    def body(raw_ref, g_ref, daq, dak, dav, dbq, dbk, dbv, o_ref, dg_ref):
        @pl.when(pl.program_id(0) == 0)
        def _():
            dg_ref[...] = jnp.zeros_like(dg_ref)

        def cols(ref, cb):
            return ref[:, cb * LANES:(cb + 1) * LANES]

        def norm_bwd(cb, dy, scale):
            xv = cols(raw_ref, cb)
            gv = cols(g_ref, cb)
            r = lax.rsqrt(_seg_sum(xv * xv) * (1.0 / HEAD_DIM) + EPS)
            xh = xv * r
            dys = dy * scale
            dyg = dys * gv
            dxv = r * (dyg - xh * (_seg_sum(dyg * xh) * (1.0 / HEAD_DIM)))
            o_ref[:, cb * LANES:(cb + 1) * LANES] = dxv.astype(BF16)
            dg_ref[:, cb * LANES:(cb + 1) * LANES] += jnp.sum(dys * xh, axis=0, keepdims=True)

        def fold(ref, kv_blk):
            halves = []
            for half in range(2):
                kvh = 2 * kv_blk + half
                blocks = [e for e in range(n_b) if (2 * e) // GROUP_B == kvh]
                s = cols(ref, blocks[0])
                for e in blocks[1:]:
                    s = s + cols(ref, e)
                halves.append(s + pltpu.roll(s, HEAD_DIM, 1))
            return jnp.where(_lo_lanes(halves[0].shape), halves[0], halves[1])

        for cb in range(n_a):
            norm_bwd(cb, cols(daq, cb), Q_SCALE)
            norm_bwd(n_a + cb, cols(dak, cb), 1.0)
            cbv = 2 * n_a + cb
            o_ref[:, cbv * LANES:(cbv + 1) * LANES] = cols(dav, cb).astype(BF16)
        for cb in range(n_b):
            norm_bwd(o_qb + cb, cols(dbq, cb), Q_SCALE)
        for kb in range(n_kv):
            ck = o_qb + n_b + kb
            cv = ck + n_kv
            norm_bwd(ck, fold(dbk, kb), 1.0)
            o_ref[:, cv * LANES:(cv + 1) * LANES] = fold(dbv, kb).astype(BF16)

    hw_a, hw_b = n_a * LANES, n_b * LANES
    return pl.pallas_call(
        body, name="qkv_post_bwd", grid=(T // tm,),
        in_specs=[pl.BlockSpec((tm, W), lambda i: (i, 0)), pl.BlockSpec((1, W), lambda i: (0, 0))]
        + [pl.BlockSpec((tm, hw_a), lambda i: (i, 0))] * 3 + [pl.BlockSpec((tm, hw_b), lambda i: (i, 0))] * 3,
        out_specs=[pl.BlockSpec((tm, W), lambda i: (i, 0)), pl.BlockSpec((1, W), lambda i: (0, 0))],
        out_shape=[jax.ShapeDtypeStruct((T, W), BF16), jax.ShapeDtypeStruct((1, W), F32)],
        compiler_params=_params(1))(raw, gains, *d_a, *d_b)


def _t5_bucket_np(rel):
    half = N_BUCKETS // 2
    max_exact = half // 2
    ret = np.where(rel > 0, half, 0)
    n = np.abs(rel)
    nf = np.maximum(n, 1).astype(np.float32)
    large = max_exact + (np.log(nf / np.float32(max_exact)) / np.float32(math.log(MAX_DISTANCE / max_exact))
                         * np.float32(half - max_exact)).astype(np.int32)
    large = np.minimum(large, half - 1)
    return ret + np.where(n < max_exact, n, large)


def _window_pad(radius):
    assert radius <= QBLOCK
    return HEAD_DIM if radius <= HEAD_DIM else QBLOCK


def _bucket_maps(configs):
    pad = _window_pad(configs[0][0])
    q = np.arange(QBLOCK)[:, None]
    kk = np.arange(QBLOCK + 2 * pad)[None, :]
    rel = kk - pad - q
    maps = [np.where(np.abs(rel) <= radius, _t5_bucket_np(rel * dil), -1) for radius, dil in configs]
    return np.stack(maps).astype(np.int32)


def _bias_build(rel_bias, bmaps, col0):
    n_sets, _, W = bmaps.shape
    n_heads = rel_bias.shape[1] // 2

    def body(rb_ref, bm_ref, o_ref):
        h = pl.program_id(1)
        bm = bm_ref[...]

        def step(n, acc):
            return jnp.where(bm == n, rb_ref[n, col0 + h], acc)

        o_ref[...] = lax.fori_loop(0, N_BUCKETS, step, jnp.where(bm < 0, NEG, 0.0).astype(F32))

    return pl.pallas_call(
        body, name="bias_build", grid=(n_sets, n_heads),
        in_specs=[pl.BlockSpec(memory_space=pltpu.SMEM), pl.BlockSpec((None, QBLOCK, W), lambda s, h: (s, 0, 0))],
        out_specs=pl.BlockSpec((None, None, QBLOCK, W), lambda s, h: (s, h, 0, 0)),
        out_shape=jax.ShapeDtypeStruct((n_sets, n_heads, QBLOCK, W), F32),
        compiler_params=_params(2))(rel_bias, bmaps)


def _bias_grad(dtiles, bmaps, col0):
    n_sets, _, W = bmaps.shape
    n_heads = dtiles[0].shape[1]
    n_l = len(dtiles)

    def body(*refs):
        bm_ref, o_ref = refs[0], refs[1 + n_l]
        s, h = pl.program_id(0), pl.program_id(1)

        @pl.when((s == 0) & (h == 0))
        def _():
            o_ref[...] = jnp.zeros_like(o_ref)

        d = refs[1][...]
        for r in refs[2:1 + n_l]:
            d = d + r[...]
        acc8 = d[0:8, :]
        for a in range(1, QBLOCK // 8):
            acc8 = acc8 + pltpu.roll(d[8 * a:8 * a + 8, :], W - 8 * a, 1)
        per_offset = acc8[0:1, :]
        for b in range(1, 8):
            per_offset = per_offset + pltpu.roll(acc8[b:b + 1, :], W - b, 1)
        bm = bm_ref[0:1, :]
        rows = lax.broadcasted_iota(jnp.int32, o_ref.shape, 0)
        lanes = lax.broadcasted_iota(jnp.int32, o_ref.shape, 1)

        def step(n, acc):
            val = jnp.sum(jnp.where(bm == n, per_offset, 0.0))
            return acc + jnp.where((rows == n) & (lanes == col0 + h), val, 0.0)

        o_ref[...] += lax.fori_loop(0, N_BUCKETS, step, jnp.zeros(o_ref.shape, F32))

    tile = pl.BlockSpec((None, None, QBLOCK, W), lambda s, h: (s, h, 0, 0))
    return pl.pallas_call(
        body, name="bias_grad", grid=(n_sets, n_heads),
        in_specs=[pl.BlockSpec((None, QBLOCK, W), lambda s, h: (s, 0, 0))] + [tile] * n_l,
        out_specs=pl.BlockSpec((N_BUCKETS, LANES), lambda s, h: (0, 0)),
        out_shape=jax.ShapeDtypeStruct((N_BUCKETS, LANES), F32), compiler_params=_params(2))(bmaps, *dtiles)


def _rows(l_start, n, d, r):
    if d == 1:
        return pl.ds(pl.multiple_of(l_start, 8), n)
    return pl.ds(l_start * d + r, n, stride=d)


def _stack_heads(xv, lo):
    z = jnp.zeros_like(xv)
    return jnp.concatenate([jnp.where(lo, xv, z), jnp.where(lo, z, xv)], axis=0)


def _unstack_heads(xv, lo):
    return jnp.where(lo, xv[:QBLOCK], xv[QBLOCK:])


def _per_head_rows(v0, v1):
    if jnp.ndim(v0) == 0:
        return jnp.where(lax.broadcasted_iota(jnp.int32, (2 * QBLOCK, 1), 0) < QBLOCK, v0, v1)
    return jnp.concatenate([v0, v1], axis=0)


def _block_geometry(b, nb_sub, pad):
    r, lb = b // nb_sub, b % nb_sub
    l0 = lb * QBLOCK
    lp = jnp.maximum(l0 - pad, 0)
    ln = jnp.minimum(l0 + QBLOCK, nb_sub * QBLOCK - pad)
    col = lax.broadcasted_iota(jnp.int32, (2 * QBLOCK, QBLOCK + 2 * pad), 1)
    valid = ((col >= pad) | (lb != 0)) & ((col < pad + QBLOCK) | (lb != nb_sub - 1))
    return r, l0, lp, ln, valid


def _window(ref, l0, lp, ln, pad, d, r):
    return jnp.concatenate([ref[_rows(lp, pad, d, r), :], ref[_rows(l0, QBLOCK, d, r), :],
                            ref[_rows(ln, pad, d, r), :]], axis=0)


def _attn_fwd(qkv, bias, sink, dils, pad):
    T = qkv.shape[0]
    hw = qkv.shape[1] // 3
    ng = hw // LANES
    n_br = len(dils)
    n_blocks = T // QBLOCK
    W = QBLOCK + 2 * pad
    chunk = 256

    def body(sink_ref, q_ref, k_ref, v_ref, bias_ref, o_ref, lse_ref, *scratch):
        g = pl.program_id(0)
        lo = _lo_lanes((QBLOCK, LANES))
        snk = _per_head_rows(sink_ref[2 * g], sink_ref[2 * g + 1])
        for c, d in enumerate(dils):
            nb_sub = n_blocks // d
            o_dst = scratch[0].at[c] if n_br > 1 else o_ref
            l_dst = scratch[1].at[c] if n_br > 1 else lse_ref

            def block(b, carry, c=c, d=d, nb_sub=nb_sub, o_dst=o_dst, l_dst=l_dst):
                r, l0, lp, ln, valid = _block_geometry(b, nb_sub, pad)
                q = _stack_heads(q_ref[_rows(l0, QBLOCK, d, r), :].astype(BF16), lo)
                k = _window(k_ref, l0, lp, ln, pad, d, r).astype(BF16)
                v = _window(v_ref, l0, lp, ln, pad, d, r).astype(BF16)
                s = jnp.where(valid, _dot_nt(q, k) + bias_ref[c], NEG)
                m = jnp.maximum(jnp.max(s, axis=1, keepdims=True), snk)
                p = jnp.exp(s - m)
                den = jnp.sum(p, axis=1, keepdims=True) + jnp.exp(snk - m)
                o_dst[_rows(l0, QBLOCK, d, r), :] = _unstack_heads(_dot(p.astype(BF16), v) / den, lo)
                l_dst[_rows(l0, QBLOCK, d, r), :] = _unstack_heads(
                    jnp.broadcast_to(m + jnp.log(den), (2 * QBLOCK, LANES)), lo)
                return carry

            lax.fori_loop(0, n_blocks, block, 0, unroll=FWD_UNROLL)

        if n_br > 1:
            def merge(i, carry):
                rs = pl.ds(pl.multiple_of(i * chunk, chunk), chunk)
                ls = [scratch[1][c, rs, :] for c in range(n_br)]
                m = ls[0]
                for t in ls[1:]:
                    m = jnp.maximum(m, t)
                ws = [jnp.exp(t - m) for t in ls]
                z = ws[0]
                acc = ws[0] * scratch[0][0, rs, :]
                for c in range(1, n_br):
                    z = z + ws[c]
                    acc = acc + ws[c] * scratch[0][c, rs, :]
                o_ref[rs, :] = acc / z
                lse_ref[rs, :] = m + jnp.log(z)
                return carry

            lax.fori_loop(0, T // chunk, merge, 0)

    def col(base):
        return pl.BlockSpec((T, LANES), lambda g: (0, base + g))

    out = jax.ShapeDtypeStruct((T, hw), F32)
    scratch = [pltpu.VMEM((n_br, T, LANES), F32)] * 2 if n_br > 1 else []
    return pl.pallas_call(
        body, name="attn_fwd", grid=(ng,),
        in_specs=[pl.BlockSpec(memory_space=pltpu.SMEM), col(0), col(ng), col(2 * ng),
                  pl.BlockSpec((n_br, 2 * QBLOCK, W), lambda g: (0, g, 0))],
        out_specs=[col(0), col(0)], out_shape=[out, out], scratch_shapes=scratch,
        compiler_params=_params(1))(sink, qkv, qkv, qkv, bias.reshape(n_br, -1, W))


def _attn_bwd(qkv, bias, sink, dils, pad, do, lse, dd, col_base):
    T = qkv.shape[0]
    hw = qkv.shape[1] // 3
    ng = hw // LANES
    n_br = len(dils)
    n_blocks = T // QBLOCK
    W = QBLOCK + 2 * pad

    def body(sink_ref, q_ref, k_ref, v_ref, bias_ref, do_ref, lse_ref, dd_ref,
             dq_ref, dk_ref, dv_ref, dt_ref, ds_ref):
        g = pl.program_id(0)
        dq_ref[...] = jnp.zeros_like(dq_ref)
        dk_ref[...] = jnp.zeros_like(dk_ref)
        dv_ref[...] = jnp.zeros_like(dv_ref)
        dt_ref[...] = jnp.zeros_like(dt_ref)
        ds_ref[...] = jnp.zeros_like(ds_ref)
        lo = _lo_lanes((QBLOCK, LANES))
        snk = jnp.where(lo, sink_ref[2 * g], sink_ref[2 * g + 1])
        for c, d in enumerate(dils):
            nb_sub = n_blocks // d

            def block(b, carry, c=c, d=d, nb_sub=nb_sub):
                r, l0, lp, ln, valid = _block_geometry(b, nb_sub, pad)
                rows_q = _rows(l0, QBLOCK, d, r)
                q = _stack_heads(q_ref[rows_q, :].astype(BF16), lo)
                k = _window(k_ref, l0, lp, ln, pad, d, r).astype(BF16)
                v = _window(v_ref, l0, lp, ln, pad, d, r).astype(BF16)
                dob = _stack_heads(do_ref[rows_q, :].astype(BF16), lo)
                lse_b = lse_ref[rows_q, :]
                dd_b = dd_ref[rows_q, :]
                s = jnp.where(valid, _dot_nt(q, k) + bias_ref[c], NEG)
                p = jnp.exp(s - _per_head_rows(lse_b[:, 0:1], lse_b[:, HEAD_DIM:HEAD_DIM + 1]))
                ds = p * (_dot_nt(dob, v) - _per_head_rows(dd_b[:, 0:1], dd_b[:, HEAD_DIM:HEAD_DIM + 1]))
                dsb = ds.astype(BF16)
                dkw = _dot_tn(dsb, q)
                dvw = _dot_tn(p.astype(BF16), dob)
                dt_ref[c] += ds
                dq_ref[rows_q, :] += _unstack_heads(_dot(dsb, k), lo)
                ds_ref[0:1, :] += jnp.sum(-jnp.exp(snk - lse_b) * dd_b, axis=0, keepdims=True)
                for part, (start, n) in zip((0, pad, pad + QBLOCK), ((lp, pad), (l0, QBLOCK), (ln, pad))):
                    dk_ref[_rows(start, n, d, r), :] += dkw[part:part + n]
                    dv_ref[_rows(start, n, d, r), :] += dvw[part:part + n]
                return carry

            lax.fori_loop(0, n_blocks, block, 0, unroll=BWD_UNROLL)

    def col(base):
        return pl.BlockSpec((T, LANES), lambda g: (0, base + g))

    tile = pl.BlockSpec((n_br, 2 * QBLOCK, W), lambda g: (0, g, 0))
    full = jax.ShapeDtypeStruct((T, hw), F32)
    dq, dk, dv, dt, dsink = pl.pallas_call(
        body, name="attn_bwd", grid=(ng,),
        in_specs=[pl.BlockSpec(memory_space=pltpu.SMEM), col(0), col(ng), col(2 * ng), tile,
                  col(col_base), col(0), col(col_base)],
        out_specs=[col(0), col(0), col(0), tile, pl.BlockSpec((None, 8, LANES), lambda g: (g, 0, 0))],
        out_shape=[full, full, full, jax.ShapeDtypeStruct((n_br, 2 * ng * QBLOCK, W), F32),
                   jax.ShapeDtypeStruct((ng, 8, LANES), F32)],
        compiler_params=_params(1))(sink, qkv, qkv, qkv, bias.reshape(n_br, -1, W), do, lse, dd)
    return dq, dk, dv, dt.reshape(n_br, 2 * ng, QBLOCK, W), dsink


def _mix_bwd_in(dx, wo, o_list):
    T, D = dx.shape
    widths = [o.shape[1] for o in o_list]
    hw = sum(widths)
    n = len(o_list)
    tm = _pick_tm(T, 256)

    def body(*refs):
        dx_ref, w_ref, o_refs, do_ref, dd_ref = refs[0], refs[1], refs[2:2 + n], refs[2 + n], refs[3 + n]
        dov = _dot_nt(dx_ref[...].astype(BF16), w_ref[...])
        do_ref[...] = dov
        off = 0
        for o_ref, k in zip(o_refs, widths):
            prod = dov[:, off:off + k] * o_ref[...]
            for cb in range(k // LANES):
                dd_ref[:, off + cb * LANES:off + (cb + 1) * LANES] = _seg_sum(prod[:, cb * LANES:(cb + 1) * LANES])
            off += k

    row = pl.BlockSpec((tm, hw), lambda i: (i, 0))
    out = jax.ShapeDtypeStruct((T, hw), F32)
    return pl.pallas_call(
        body, name="mix_bwd_in", grid=(T // tm,),
        in_specs=[pl.BlockSpec((tm, D), lambda i: (i, 0)), pl.BlockSpec((hw, D), lambda i: (0, 0))]
        + [pl.BlockSpec((tm, k), lambda i: (i, 0)) for k in widths],
        out_specs=[row, row], out_shape=[out, out], compiler_params=_params(1))(dx, wo, *o_list)


def _mesh_pos():
    return lax.axis_index("x"), lax.axis_index("y"), lax.axis_index("c")


def _my_chip():
    return 2 * lax.axis_index("x") + lax.axis_index("y")


def _other_chips(x, y):
    return [(1 - x, y), (x, 1 - y), (1 - x, 1 - y)]


def _half_rows(rows, cc):
    hr = rows // 2
    return pl.ds(pl.multiple_of(cc * hr, 16), hr)


def _cast_into_slot(w, l):
    _, R, C = w.shape
    tm = _pick_tm(R, 512)

    def body(w_ref, o_ref):
        o_ref[...] = w_ref[...].astype(BF16)

    return pl.pallas_call(
        body, name="cast_into_slot", grid=(R // tm,),
        in_specs=[pl.BlockSpec((None, tm, C), lambda i: (l, i, 0))],
        out_specs=pl.BlockSpec((None, tm, C), lambda i: (_my_chip(), i, 0)),
        out_shape=jax.ShapeDtypeStruct((N_CHIPS, R, C), BF16), compiler_params=_params(1))(w)


def _split_start(name, arrays, make_copies, n_sem, after):
    n = len(arrays)
    n_in = n + (0 if after is None else 1)

    def body(*refs):
        send_s, recv_s = refs[n_in], refs[n_in + 1]
        token = refs[n_in + 2 + n]
        for send, _ in make_copies(refs[:n], send_s, recv_s):
            send.start()
        token[...] = jnp.zeros_like(token)

    res = pl.pallas_call(
        body, name=name,
        out_shape=(pltpu.SemaphoreType.DMA((n_sem,)), pltpu.SemaphoreType.DMA((n_sem,)),
                   *[pltpu.HBM(a.shape, a.dtype) for a in arrays], jax.ShapeDtypeStruct((8, LANES), F32)),
        in_specs=[HBM_SPEC] * n + [ANY_SPEC] * (n_in - n),
        out_specs=(SEM_SPEC, SEM_SPEC, *([HBM_SPEC] * n), pl.BlockSpec(memory_space=pltpu.VMEM)),
        input_output_aliases={i: 2 + i for i in range(n)},
        compiler_params=pltpu.CompilerParams(has_side_effects=pltpu.SideEffectType.DATAFLOW_SIDE_EFFECTING),
    )(*[pltpu.with_memory_space_constraint(a, pltpu.HBM) for a in arrays], *([] if after is None else [after]))
    return res[0], res[1], list(res[2:2 + n]), res[2 + n]


def _split_wait(name, send_s, recv_s, arrays, make_copies, after):
    n = len(arrays)

    def body(*refs):
        for send, landed in make_copies(refs[:n], refs[n], refs[n + 1]):
            send.wait_send()
            landed.wait_recv()

    return list(pl.pallas_call(
        body, name=name, out_shape=[pltpu.HBM(a.shape, a.dtype) for a in arrays],
        in_specs=[HBM_SPEC] * n + [SEM_SPEC, SEM_SPEC, ANY_SPEC], out_specs=[HBM_SPEC] * n,
        input_output_aliases={i: i for i in range(n)},
        compiler_params=pltpu.CompilerParams(has_side_effects=pltpu.SideEffectType.DATAFLOW_SIDE_EFFECTING),
    )(*arrays, send_s, recv_s, after))


def _gather_copies(shapes):
    n = len(shapes)

    def make(refs, send_s, recv_s):
        x, y, c = _mesh_pos()
        my = 2 * x + y
        copies = []
        for w in range(n):
            for k, (px, py) in enumerate(_other_chips(x, y)):
                def part(slot, w=w):
                    return refs[w].at[slot, _half_rows(shapes[w][1], c), :]
                sems = dict(send_sem=send_s.at[k * n + w], recv_sem=recv_s.at[k * n + w],
                            device_id=(px, py, c), device_id_type=MESH)
                copies.append((pltpu.make_async_remote_copy(src_ref=part(my), dst_ref=part(my), **sems),
                               pltpu.make_async_remote_copy(src_ref=part(2 * px + py), dst_ref=part(2 * px + py), **sems)))
        return copies

    return make


def _pair_forward(bufs):
    n = len(bufs)

    def body(*refs):
        arr = refs[n:2 * n]
        send_s, recv_s = refs[2 * n:]
        x, y, c = _mesh_pos()
        sent, landed = [], []
        for w in range(n):
            for k, (px, py) in enumerate(_other_chips(x, y)):
                def part(cc, w=w, slot=2 * px + py):
                    return arr[w].at[slot, _half_rows(bufs[w].shape[1], cc), :]
                sems = dict(send_sem=send_s.at[k * n + w], recv_sem=recv_s.at[k * n + w],
                            device_id=(x, y, 1 - c), device_id_type=MESH)
                sent.append(pltpu.make_async_remote_copy(src_ref=part(c), dst_ref=part(c), **sems))
                landed.append(pltpu.make_async_remote_copy(src_ref=part(1 - c), dst_ref=part(1 - c), **sems))
        for cp in sent:
            cp.start()
        for cp in landed:
            cp.wait_recv()
        for cp in sent:
            cp.wait_send()

    return list(pl.pallas_call(
        body, name="ag_pair_forward", in_specs=[HBM_SPEC] * n, out_specs=[HBM_SPEC] * n,
        out_shape=[jax.ShapeDtypeStruct(b.shape, b.dtype) for b in bufs],
        input_output_aliases={w: w for w in range(n)},
        scratch_shapes=[pltpu.SemaphoreType.DMA((3 * n,)), pltpu.SemaphoreType.DMA((3 * n,))],
    )(*bufs))


def _rs_pair_exchange(grads):
    n = len(grads)

    def body(*refs):
        ins, recv = refs[:n], refs[n:2 * n]
        send_s, recv_s = refs[2 * n:]
        x, y, c = _mesh_pos()
        remote = [pltpu.make_async_remote_copy(
            src_ref=ins[t].at[:, _half_rows(grads[t].shape[1], 1 - c), :], dst_ref=recv[t],
            send_sem=send_s.at[t], recv_sem=recv_s.at[t], device_id=(x, y, 1 - c), device_id_type=MESH)
            for t in range(n)]
        for cp in remote:
            cp.start()
        for cp in remote:
            cp.wait_recv()
        for cp in remote:
            cp.wait_send()

    return pl.pallas_call(
        body, name="rs_pair_exchange", in_specs=[HBM_SPEC] * n, out_specs=[HBM_SPEC] * n,
        out_shape=[jax.ShapeDtypeStruct((g.shape[0], g.shape[1] // 2, g.shape[2]), g.dtype) for g in grads],
        scratch_shapes=[pltpu.SemaphoreType.DMA((n,)), pltpu.SemaphoreType.DMA((n,))],
    )(*grads)


def _rs_add_pair(grad, recv):
    n_slot, hr, C = recv.shape
    tm = _pick_tm(hr, 256)
    nb = hr // tm

    def body(a_ref, b_ref, o_ref):
        o_ref[...] = (a_ref[...].astype(F32) + b_ref[...].astype(F32)).astype(BF16)

    blk = pl.BlockSpec((None, tm, C), lambda k, i: (k, i, 0))
    return pl.pallas_call(
        body, name="rs_add_pair", grid=(n_slot, nb),
        in_specs=[pl.BlockSpec((None, tm, C), lambda k, i: (k, lax.axis_index("c") * nb + i, 0)), blk],
        out_specs=blk, out_shape=jax.ShapeDtypeStruct(recv.shape, BF16), compiler_params=_params(2))(grad, recv)


def _scatter_copies(n):
    def make(refs, send_s, recv_s):
        x, y, c = _mesh_pos()
        copies = []
        for t in range(n):
            for k, (px, py) in enumerate(_other_chips(x, y)):
                sems = dict(send_sem=send_s.at[3 * t + k], recv_sem=recv_s.at[3 * t + k],
                            device_id=(px, py, c), device_id_type=MESH)
                land = refs[n + t].at[k]
                copies.append((pltpu.make_async_remote_copy(src_ref=refs[t].at[2 * px + py], dst_ref=land, **sems),
                               pltpu.make_async_remote_copy(src_ref=land, dst_ref=land, **sems)))
        return copies

    return make


def _rs_add_chips(part, recv):
    _, hr, C = part.shape
    tm = _pick_tm(hr, 256)
    nb = hr // tm

    def body(a_ref, r0, r1, r2, o_ref):
        o_ref[...] = ((a_ref[...].astype(F32) + r0[...].astype(F32)) + r1[...].astype(F32)) + r2[...].astype(F32)

    def rel(k):
        return pl.BlockSpec((None, tm, C), lambda i: (k, i, 0))

    return pl.pallas_call(
        body, name="rs_add_chips", grid=(nb,),
        in_specs=[pl.BlockSpec((None, tm, C), lambda i: (_my_chip(), i, 0)), rel(0), rel(1), rel(2)],
        out_specs=pl.BlockSpec((tm, C), lambda i: (lax.axis_index("c") * nb + i, 0)),
        out_shape=jax.ShapeDtypeStruct((2 * hr, C), F32), compiler_params=_params(1))(part, recv, recv, recv)


def _rs_pair_share(halves):
    n = len(halves)

    def body(*refs):
        bufs = refs[n:2 * n]
        send_s, recv_s = refs[2 * n:]
        x, y, c = _mesh_pos()

        def half(t, cc):
            return bufs[t].at[_half_rows(halves[t].shape[0], cc), :]

        sent = [pltpu.make_async_remote_copy(
            src_ref=half(t, c), dst_ref=half(t, c), send_sem=send_s.at[t], recv_sem=recv_s.at[t],
            device_id=(x, y, 1 - c), device_id_type=MESH) for t in range(n)]
        landed = [pltpu.make_async_remote_copy(
            src_ref=half(t, 1 - c), dst_ref=half(t, 1 - c), send_sem=send_s.at[t], recv_sem=recv_s.at[t],
            device_id=(x, y, 1 - c), device_id_type=MESH) for t in range(n)]
        for cp in sent:
            cp.start()
        for cp in landed:
            cp.wait_recv()
        for cp in sent:
            cp.wait_send()

    return pl.pallas_call(
        body, name="rs_pair_share", in_specs=[HBM_SPEC] * n, out_specs=[HBM_SPEC] * n,
        out_shape=[jax.ShapeDtypeStruct(h.shape, h.dtype) for h in halves],
        input_output_aliases={t: t for t in range(n)},
        scratch_shapes=[pltpu.SemaphoreType.DMA((n,)), pltpu.SemaphoreType.DMA((n,))],
    )(*halves)


def _allreduce_small(v):
    rows = v.shape[0]

    def body(v_ref, o_ref, buf, send_s, recv_s):
        x, y, c = _mesh_pos()
        me = 4 * x + 2 * y + c
        buf[me] = v_ref[...]
        copies = []
        for r in range(1, N_DEV):
            px = 1 - x if r & 4 else x
            py = 1 - y if r & 2 else y
            pc = 1 - c if r & 1 else c
            send = pltpu.make_async_remote_copy(
                src_ref=v_ref, dst_ref=buf.at[me], send_sem=send_s.at[r - 1], recv_sem=recv_s.at[r - 1],
                device_id=(px, py, pc), device_id_type=MESH)
            peer_slot = buf.at[4 * px + 2 * py + pc]
            landed = pltpu.make_async_remote_copy(
                src_ref=peer_slot, dst_ref=peer_slot, send_sem=send_s.at[r - 1], recv_sem=recv_s.at[r - 1],
                device_id=(px, py, pc), device_id_type=MESH)
            copies.append((send, landed))
        for send, _ in copies:
            send.start()
        for _, landed in copies:
            landed.wait_recv()
        for send, _ in copies:
            send.wait_send()
        acc = buf[0]
        for j in range(1, N_DEV):
            acc = acc + buf[j]
        o_ref[...] = acc

    vm = pl.BlockSpec(memory_space=pltpu.VMEM)
    return pl.pallas_call(
        body, name="allreduce_small", in_specs=[vm], out_specs=vm,
        out_shape=jax.ShapeDtypeStruct((rows, LANES), F32),
        scratch_shapes=[pltpu.VMEM((N_DEV, rows, LANES), F32), pltpu.SemaphoreType.DMA((N_DEV - 1,)),
                        pltpu.SemaphoreType.DMA((N_DEV - 1,))],
    )(v)


def _adamw_fn(w, g, m, v):
    m2 = ADAM_B1 * m + (1.0 - ADAM_B1) * g
    v2 = ADAM_B2 * v + (1.0 - ADAM_B2) * (g * g)
    m_hat = m2 / (1.0 - ADAM_B1 ** ADAM_STEP)
    v_hat = v2 / (1.0 - ADAM_B2 ** ADAM_STEP)
    delta = -ADAM_LR * (m_hat / (jnp.sqrt(v_hat) + ADAM_EPS) + ADAM_WD * w)
    return g, delta, m2, v2


def _adamw_layer(w, g, m, v, l, prev):
    NL, R, C = w.shape
    tm = _pick_tm(R, 128)
    n_prev = 0 if prev is None else 4

    def body(w_ref, g_ref, m_ref, v_ref, *rest):
        outs = rest[n_prev:]
        for o_ref, val in zip(outs, _adamw_fn(w_ref[...], g_ref[...], m_ref[...], v_ref[...])):
            o_ref[...] = val

    lay = pl.BlockSpec((None, tm, C), lambda i: (l, i, 0))
    shape = jax.ShapeDtypeStruct((NL, R, C), F32)
    return pl.pallas_call(
        body, name="adamw", grid=(R // tm,),
        in_specs=[lay, pl.BlockSpec((tm, C), lambda i: (i, 0)), lay, lay] + [pl.BlockSpec(memory_space=pl.ANY)] * n_prev,
        out_specs=[lay] * 4, out_shape=[shape] * 4,
        input_output_aliases={4 + j: j for j in range(n_prev)},
        compiler_params=_params(1))(w, g, m, v, *(prev or []))


def _pack_small(parts):
    out = []
    for a in parts:
        flat = a.reshape(-1)
        n = -(-flat.shape[0] // (8 * LANES)) * 8 * LANES
        out.append(jnp.pad(flat, (0, n - flat.shape[0])).reshape(-1, LANES))
    return jnp.concatenate(out, axis=0)


def _unpack_small(packed, like):
    out, r = [], 0
    for a in like:
        size = int(np.prod(a.shape))
        rows = -(-size // (8 * LANES)) * 8
        out.append(packed[r:r + rows].reshape(-1)[:size].reshape(a.shape))
        r += rows
    return out


def _ffn_forward(x, g, win, wout):
    h = _rms_fwd(x, g)
    silu, dsilu, up, act = _ffn_up(h, win)
    return _mm_res("ffn_down", x, [act], wout, 0.5), (x, h, silu, dsilu, up, act)


def _ffn_backward(dx, saved, g, win, wout, after):
    x, h, silu, dsilu, up, act = saved
    D = x.shape[1]
    wc = win.shape[2]
    dgu = _ffn_bwd_a(dx, wout, silu, dsilu, up, after)
    dwout = _mm_tn("dw_ffn_out", act, wc, dx, D, 2, True, False, 0.5)
    dwin = _mm_tn("dw_ffn_in", h, D, dgu, wc, 4, False, True, 1.0)
    dx_in, dg = _bwd_into_norm("ffn_bwd_b", dgu, wc, win, 4, x, g, dx)
    return dx_in, dg, dwin, dwout.reshape(N_CHIPS, -1, D)


GROUP_FFN1 = ("ffn1_w_in", "ffn1_w_out")
GROUP_REST = ("w_qkv", "w_o", "ffn2_w_in", "ffn2_w_out", "w_ple_gate", "w_ple_proj")


def _gather_start(tag, slotted, after):
    return _split_start("ag_start_" + tag, slotted, _gather_copies([a.shape for a in slotted]), 3 * len(slotted), after)


def _gather_finish(tag, started, after):
    send_s, recv_s, arrays, _ = started
    return _pair_forward(_split_wait("ag_wait_" + tag, send_s, recv_s, arrays,
                                     _gather_copies([a.shape for a in arrays]), after))


def _scatter_start(tag, grads):
    n = len(grads)
    part = [_rs_add_pair(g_, r_) for g_, r_ in zip(grads, _rs_pair_exchange(grads))]
    land = [lax.empty((3,) + p_.shape[1:], p_.dtype) for p_ in part]
    return _split_start("rs_start_" + tag, part + land, _scatter_copies(n), 3 * n, None)


def _scatter_finish(tag, started, after):
    send_s, recv_s, arrays, _ = started
    n = len(arrays) // 2
    arrays = _split_wait("rs_wait_" + tag, send_s, recv_s, arrays, _scatter_copies(n), after)
    return _rs_pair_share([_rs_add_chips(p_, r_) for p_, r_ in zip(arrays[:n], arrays[n:])])


def kernel(x, p, rel_bias, norm_ffn1, ffn1_w_in, ffn1_w_out, norm_mix, w_qkv, q_norm_a, k_norm_a, q_norm_b, k_norm_b, sink_b, w_o, norm_ffn2, ffn2_w_in, ffn2_w_out, norm_ple, w_ple_gate, w_ple_proj, loss_target, m_rel_bias, m_norm_ffn1, m_ffn1_w_in, m_ffn1_w_out, m_norm_mix, m_w_qkv, m_q_norm_a, m_k_norm_a, m_q_norm_b, m_k_norm_b, m_sink_b, m_w_o, m_norm_ffn2, m_ffn2_w_in, m_ffn2_w_out, m_norm_ple, m_w_ple_gate, m_w_ple_proj, v_rel_bias, v_norm_ffn1, v_ffn1_w_in, v_ffn1_w_out, v_norm_mix, v_w_qkv, v_q_norm_a, v_k_norm_a, v_q_norm_b, v_k_norm_b, v_sink_b, v_w_o, v_norm_ffn2, v_ffn2_w_in, v_ffn2_w_out, v_norm_ple, v_w_ple_gate, v_w_ple_proj):
    given = dict(locals())
    T, D = x.shape[1], x.shape[2]
    NL = norm_ffn1.shape[0]
    x0 = x.reshape(T, D)
    tgt = loss_target.reshape(T, D)
    n_a, n_b, n_kv = _qkv_layout(D)

    assert NL == 2
    slot = [{name: _cast_into_slot(given[name], l) for name in BIG} for l in range(NL)]
    ag_a = _gather_start("a", [slot[0][n] for n in GROUP_FFN1], None)
    ag_b = _gather_start("b", [slot[0][n] for n in GROUP_REST], ag_a[3])
    ag_1 = _gather_start("1", [slot[1][n] for n in BIG], ag_b[3])

    def by_rows(a):
        return a.reshape(-1, a.shape[-1])

    def by_cols(a):
        return a.transpose(1, 0, 2).reshape(a.shape[1], -1)

    QW = N_CHIPS * w_qkv.shape[2]

    dils = tuple(d for _, d in DILATED_CONFIGS)
    cfg_a = [(w // (2 * d), d) for w, d in DILATED_CONFIGS]
    pad_a, pad_b = _window_pad(cfg_a[0][0]), _window_pad(SWA_RADIUS)
    bmaps_a, bmaps_b = jnp.asarray(_bucket_maps(cfg_a)), jnp.asarray(_bucket_maps([(SWA_RADIUS, 1)]))
    n_heads = rel_bias.shape[1] // 2
    bias_a = _bias_build(rel_bias, bmaps_a, 0)
    bias_b = _bias_build(rel_bias, bmaps_b, n_heads)
    no_sink = jnp.full((n_heads,), NEG, F32)

    def gains_row(l):
        ones = jnp.ones((n_a * LANES,), F32)
        return jnp.concatenate([
            jnp.tile(q_norm_a[l], 2 * n_a), jnp.tile(k_norm_a[l], 2 * n_a), ones,
            jnp.tile(q_norm_b[l], 2 * n_b), jnp.tile(k_norm_b[l], 2 * n_kv), jnp.ones((n_kv * LANES,), F32)]).reshape(1, QW)

    saved, weights = [], []
    xc = x0
    for l in range(NL):
        s, w = {}, {}
        if l == 0:
            w.update(zip(GROUP_FFN1, _gather_finish("a", ag_a, ag_1[3])))
        else:
            w.update(zip(BIG, _gather_finish("1", ag_1, xc)))
        w["ffn1_w_out"] = by_rows(w["ffn1_w_out"])
        xc, s["ffn1"] = _ffn_forward(xc, norm_ffn1[l:l + 1], w["ffn1_w_in"], w["ffn1_w_out"])
        s["x1"] = xc
        if l == 0:
            w.update(zip(GROUP_REST, _gather_finish("b", ag_b, xc)))
        w["w_qkv"] = by_cols(w["w_qkv"])
        w["w_qkv4"] = w["w_qkv"].reshape(1, D, QW)
        w["w_ple_proj"] = by_cols(w["w_ple_proj"])
        for name in ("w_o", "ffn2_w_out", "w_ple_gate"):
            w[name] = by_rows(w[name])
        h2 = _rms_fwd(xc, norm_mix[l:l + 1])
        raw = _mm_plain("qkv_proj", h2, w["w_qkv"])
        s["h2"], s["raw"] = h2, raw
        s["qkv_a"], s["qkv_b"] = _qkv_post(raw, gains_row(l))
        s["o_a"], s["lse_a"] = _attn_fwd(s["qkv_a"], bias_a, no_sink, dils, pad_a)
        s["o_b"], s["lse_b"] = _attn_fwd(s["qkv_b"], bias_b, sink_b[l], (1,), pad_b)
        xc = _mm_res("attn_out", xc, [s["o_a"], s["o_b"]], w["w_o"], 1.0)
        xc, s["ffn2"] = _ffn_forward(xc, norm_ffn2[l:l + 1], w["ffn2_w_in"], w["ffn2_w_out"])
        s["x3"] = xc
        s["hn"] = _rms_fwd(xc, norm_ple[l:l + 1])
        s["p"] = p[l].reshape(T, -1)
        xc, s["gate"], s["pp"] = _ple_fwd(xc, s["hn"], s["p"], w["w_ple_gate"], w["w_ple_proj"])
        saved.append(s)
        weights.append(w)

    dx, loss_blk = _loss_fwd_bwd(xc, tgt)
    loss = lax.psum(loss_blk[0, 0], ("x", "y", "c"))

    gs = {name: [None] * NL for name in SMALL if name != "rel_bias"}
    dt_a, dt_b = [], []
    g_full = {}
    rs_1 = rs_0a = None
    for l in reversed(range(NL)):
        s, w, gw = saved[l], weights[l], {}
        dx, gs["norm_ple"][l], dwg, dwp = _ple_bwd(dx, s["gate"], s["pp"], s["hn"], s["p"], s["x3"],
                                                   norm_ple[l:l + 1], w["w_ple_gate"],
                                                   loss_blk if rs_1 is None else rs_1[3])
        gw["w_ple_gate"] = dwg.reshape(N_CHIPS, -1, D)
        gw["w_ple_proj"] = dwp.reshape(dwp.shape[0], N_CHIPS, -1).transpose(1, 0, 2)
        dx, gs["norm_ffn2"][l], gw["ffn2_w_in"], gw["ffn2_w_out"] = _ffn_backward(
            dx, s["ffn2"], norm_ffn2[l:l + 1], w["ffn2_w_in"], w["ffn2_w_out"], dx)
        do, dd = _mix_bwd_in(dx, w["w_o"], [s["o_a"], s["o_b"]])
        hwa = s["o_a"].shape[1]
        gw["w_o"] = jnp.concatenate([
            _mm_tn("dw_o", o_, o_.shape[1], dx, D, 1, False, False, 1.0).reshape(-1, D // N_CHIPS, D)
            for o_ in (s["o_a"], s["o_b"])], axis=0)
        dqa, dka, dva, dt, _ = _attn_bwd(s["qkv_a"], bias_a, no_sink, dils, pad_a, do, s["lse_a"], dd, 0)
        dt_a.append(dt)
        dqb, dkb, dvb, dt, dsink = _attn_bwd(s["qkv_b"], bias_b, sink_b[l], (1,), pad_b, do, s["lse_b"], dd,
                                             hwa // LANES)
        dt_b.append(dt)
        gs["sink_b"][l] = dsink[:, 0, ::HEAD_DIM].reshape(-1)
        draw, dgains = _qkv_post_bwd(s["raw"], gains_row(l), (dqa, dka, dva), (dqb, dkb, dvb))
        dgv = dgains.reshape(-1, HEAD_DIM)
        gs["q_norm_a"][l] = dgv[:2 * n_a].sum(0)
        gs["k_norm_a"][l] = dgv[2 * n_a:4 * n_a].sum(0)
        gs["q_norm_b"][l] = dgv[6 * n_a:6 * n_a + 2 * n_b].sum(0)
        gs["k_norm_b"][l] = dgv[6 * n_a + 2 * n_b:6 * n_a + 2 * n_b + 2 * n_kv].sum(0)
        dwqkv = _mm_tn("dw_qkv", s["h2"], D, draw, QW, 1, False, False, 1.0)
        gw["w_qkv"] = dwqkv.reshape(D, N_CHIPS, -1).transpose(1, 0, 2)
        dx, gs["norm_mix"][l] = _bwd_into_norm("qkv_bwd_b", draw, QW, w["w_qkv4"], 1, s["x1"], norm_mix[l:l + 1], dx)
        if l > 0:
            dx, gs["norm_ffn1"][l], gw["ffn1_w_in"], gw["ffn1_w_out"] = _ffn_backward(
                dx, s["ffn1"], norm_ffn1[l:l + 1], w["ffn1_w_in"], w["ffn1_w_out"], dx)
            rs_1 = _scatter_start("1", [gw[n] for n in BIG])
        else:
            rs_0a = _scatter_start("0a", [gw[n] for n in GROUP_REST])
            g_full.update(zip([(n, 1) for n in BIG], _scatter_finish("1", rs_1, rs_0a[3])))
            dx, gs["norm_ffn1"][l], gw["ffn1_w_in"], gw["ffn1_w_out"] = _ffn_backward(
                dx, s["ffn1"], norm_ffn1[l:l + 1], w["ffn1_w_in"], w["ffn1_w_out"], rs_0a[3])
            rs_0b = _scatter_start("0b", [gw[n] for n in GROUP_FFN1])
            done_0a = _scatter_finish("0a", rs_0a, rs_0b[3])
            g_full.update(zip([(n, 0) for n in GROUP_REST], done_0a))
            g_full.update(zip([(n, 0) for n in GROUP_FFN1], _scatter_finish("0b", rs_0b, done_0a[0])))
    grad_x = dx.reshape(x.shape)
    d_rel_bias = (_bias_grad(dt_a, bmaps_a, 0) + _bias_grad(dt_b, bmaps_b, n_heads))[:, :rel_bias.shape[1]]

    out = {}
    for name in BIG:
        res = None
        for l in reversed(range(NL)):
            res = _adamw_layer(given[name], g_full[(name, l)], given["m_" + name], given["v_" + name], l, res)
        out[name] = res

    small_g = [d_rel_bias] + [jnp.stack([t.reshape(-1) for t in gs[name]]) for name in SMALL[1:]]
    g_sum = _allreduce_small(_pack_small(small_g))
    res = _ew("adamw_small", _adamw_fn,
              [_pack_small([given[n] for n in SMALL]), g_sum, _pack_small([given["m_" + n] for n in SMALL]),
               _pack_small([given["v_" + n] for n in SMALL])], [(LANES, F32)] * 4)
    like = [given[n] for n in SMALL]
    unpacked = [_unpack_small(r, like) for r in res]
    for i, name in enumerate(SMALL):
        out[name] = [u[i] for u in unpacked]

    return (loss, grad_x, *[out[n][0] for n in WEIGHTS], *[out[n][1] for n in WEIGHTS],
            *[out[n][2] for n in WEIGHTS], *[out[n][3] for n in WEIGHTS])
```

```python
import functools
import math

import numpy as np
import jax
import jax.numpy as jnp
from jax import lax
from jax.experimental import pallas as pl
from jax.experimental.pallas import tpu as pltpu

F32 = jnp.float32
BF16 = jnp.bfloat16
MESH = pl.DeviceIdType.MESH

HEAD_DIM = 64
LANES = 128
QBLOCK = 128
FWD_UNROLL, BWD_UNROLL = 4, 4
N_BUCKETS = 32
MAX_DISTANCE = 1024
DILATED_CONFIGS = ((128, 1), (512, 4), (2048, 16))
SWA_RADIUS = 128
GROUP_B = 4
EPS = 1e-6
NEG = -1e30
Q_SCALE = HEAD_DIM ** -0.5
ADAM_LR, ADAM_B1, ADAM_B2, ADAM_EPS, ADAM_WD, ADAM_STEP = 0.001, 0.9, 0.999, 1e-08, 0.01, 10
VMEM_LIMIT = 56 * 2 ** 20
N_CHIPS = 4
N_DEV = 8

BIG = ("ffn1_w_in", "ffn1_w_out", "w_qkv", "w_o", "ffn2_w_in", "ffn2_w_out", "w_ple_gate", "w_ple_proj")
SMALL = ("rel_bias", "norm_ffn1", "norm_mix", "q_norm_a", "k_norm_a", "q_norm_b", "k_norm_b", "sink_b",
         "norm_ffn2", "norm_ple")
WEIGHTS = ("rel_bias", "norm_ffn1", "ffn1_w_in", "ffn1_w_out", "norm_mix", "w_qkv", "q_norm_a", "k_norm_a",
           "q_norm_b", "k_norm_b", "sink_b", "w_o", "norm_ffn2", "ffn2_w_in", "ffn2_w_out", "norm_ple",
           "w_ple_gate", "w_ple_proj")


HBM_SPEC = pl.BlockSpec(memory_space=pltpu.HBM)
ANY_SPEC = pl.BlockSpec(memory_space=pl.ANY)
SEM_SPEC = pl.BlockSpec(memory_space=pltpu.SEMAPHORE)


def _params(n_grid):
    return pltpu.CompilerParams(dimension_semantics=("arbitrary",) * n_grid, vmem_limit_bytes=VMEM_LIMIT)


def _pick_tm(rows, cap):
    t = (min(cap, rows) // 16) * 16
    while t >= 16:
        if rows % t == 0:
            return t
        t -= 16
    return rows


def _dot(a, b):
    return jnp.dot(a, b, preferred_element_type=F32)


def _dot_nt(a, b):
    return lax.dot_general(a, b, (((1,), (1,)), ((), ())), preferred_element_type=F32)


def _dot_tn(a, b):
    return lax.dot_general(a, b, (((0,), (0,)), ((), ())), preferred_element_type=F32)


def _sigmoid(z):
    return 1.0 / (1.0 + jnp.exp(-z))


def _lo_lanes(shape):
    return lax.broadcasted_iota(jnp.int32, shape, len(shape) - 1) % LANES < HEAD_DIM


def _seg_sum(blk):
    lo = _lo_lanes(blk.shape)
    s_lo = jnp.sum(jnp.where(lo, blk, 0.0), axis=1, keepdims=True)
    s_hi = jnp.sum(jnp.where(lo, 0.0, blk), axis=1, keepdims=True)
    return jnp.where(lo, s_lo, s_hi)


def _rms_bwd_tile(x, g, dh):
    r = lax.rsqrt(jnp.mean(x * x, axis=-1, keepdims=True) + EPS)
    xh = x * r
    dyg = dh * g
    dx = r * (dyg - xh * jnp.mean(dyg * xh, axis=-1, keepdims=True))
    return dx, jnp.sum(dh * xh, axis=0, keepdims=True)


def _ew(name, fn, ins, out_defs, cap=512):
    rows = ins[0].shape[0]
    tm = _pick_tm(rows, cap)
    n_in = len(ins)

    def body(*refs):
        vals = fn(*[r[...] for r in refs[:n_in]])
        if not isinstance(vals, tuple):
            vals = (vals,)
        for r, v in zip(refs[n_in:], vals):
            r[...] = v.astype(r.dtype)

    return pl.pallas_call(
        body, name=name, grid=(rows // tm,),
        in_specs=[pl.BlockSpec((tm, a.shape[1]), lambda i: (i, 0)) for a in ins],
        out_specs=[pl.BlockSpec((tm, c), lambda i: (i, 0)) for c, _ in out_defs],
        out_shape=[jax.ShapeDtypeStruct((rows, c), dt) for c, dt in out_defs],
        compiler_params=_params(1))(*ins)


def _rms_tile(xv, gv):
    r = lax.rsqrt(jnp.mean(xv * xv, axis=-1, keepdims=True) + EPS)
    return (xv * r * gv).astype(BF16)


def _ffn_up(x, g, win):
    T, D = x.shape
    wc = win.shape[2]
    tm = _pick_tm(T, 512)

    def body(x_ref, g_ref, wg_ref, wu_ref, h_ref, silu_ref, dgate_ref, act_ref):
        hv = _rms_tile(x_ref[...], g_ref[...])
        h_ref[...] = hv
        gte = _dot(hv, wg_ref[...])
        u = _dot(hv, wu_ref[...])
        sg = _sigmoid(gte)
        silu = gte * sg
        silu_ref[...] = silu.astype(BF16)
        dgate_ref[...] = ((sg + silu * (1.0 - sg)) * u).astype(BF16)
        act_ref[...] = (silu * u).astype(BF16)

    out = jax.ShapeDtypeStruct((T, 2 * wc), BF16)
    ospec = pl.BlockSpec((tm, wc), lambda j, i: (i, j))
    nt = T // tm
    h_spec = pl.BlockSpec((tm, D), lambda j, i: (jnp.where(j == 0, i, nt), 0))
    return pl.pallas_call(
        body, name="ffn_up", grid=(2, nt),
        in_specs=[pl.BlockSpec((tm, D), lambda j, i: (i, 0)), pl.BlockSpec((1, D), lambda j, i: (0, 0)),
                  pl.BlockSpec((None, D, wc), lambda j, i: (j, 0, 0)),
                  pl.BlockSpec((None, D, wc), lambda j, i: (j + 2, 0, 0))],
        out_specs=[h_spec] + [ospec] * 3, out_shape=[jax.ShapeDtypeStruct((T + tm, D), BF16)] + [out] * 3,
        compiler_params=_params(2))(x, g, win, win)


def _mm_res(name, res, a_list, w, scale):
    T, N = res.shape
    n = len(a_list)
    widths = [a.shape[1] for a in a_list]
    tm = _pick_tm(T, 512)

    def body(*refs):
        r_ref, a_refs, w_refs, o_ref = refs[0], refs[1:1 + n], refs[1 + n:1 + 2 * n], refs[1 + 2 * n]
        acc = _dot(a_refs[0][...].astype(BF16), w_refs[0][...])
        for a_ref, w_ref in zip(a_refs[1:], w_refs[1:]):
            acc = acc + _dot(a_ref[...].astype(BF16), w_ref[...])
        o_ref[...] = r_ref[...] + scale * acc

    w_specs, off = [], 0
    for k in widths:
        w_specs.append(pl.BlockSpec((k, N), lambda i, blk=off // k: (blk, 0)))
        off += k
    return pl.pallas_call(
        body, name=name, grid=(T // tm,),
        in_specs=[pl.BlockSpec((tm, N), lambda i: (i, 0))]
        + [pl.BlockSpec((tm, k), lambda i: (i, 0)) for k in widths] + w_specs,
        out_specs=pl.BlockSpec((tm, N), lambda i: (i, 0)),
        out_shape=jax.ShapeDtypeStruct((T, N), F32), compiler_params=_params(1))(res, *a_list, *([w] * n))


def _norm_proj(name, x, g, w):
    T, K = x.shape
    N = w.shape[1]
    tm = _pick_tm(T, 512)

    def body(x_ref, g_ref, w_ref, h_ref, o_ref):
        hv = _rms_tile(x_ref[...], g_ref[...])
        h_ref[...] = hv
        o_ref[...] = _dot(hv, w_ref[...])

    row = pl.BlockSpec((tm, K), lambda i: (i, 0))
    return pl.pallas_call(
        body, name=name, grid=(T // tm,),
        in_specs=[row, pl.BlockSpec((1, K), lambda i: (0, 0)), pl.BlockSpec((K, N), lambda i: (0, 0))],
        out_specs=[row, pl.BlockSpec((tm, N), lambda i: (i, 0))],
        out_shape=[jax.ShapeDtypeStruct((T, K), BF16), jax.ShapeDtypeStruct((T, N), F32)],
        compiler_params=_params(1))(x, g, w)


def _mm_tn(name, a, a_w, b, b_w, n_slots, a_by_slot, b_by_slot, scale, tm_cap=512):
    T = b.shape[0]
    tm = _pick_tm(T, tm_cap)
    nt = T // tm

    def body(a_ref, b_ref, o_ref, acc):
        i = pl.program_id(1)

        @pl.when(i == 0)
        def _():
            acc[...] = jnp.zeros_like(acc)

        acc[...] += _dot_tn(a_ref[...].astype(BF16), b_ref[...].astype(BF16))

        @pl.when(i == nt - 1)
        def _():
            o_ref[...] = (acc[...] * scale).astype(BF16)

    return pl.pallas_call(
        body, name=name, grid=(n_slots, nt),
        in_specs=[pl.BlockSpec((tm, a_w), (lambda s, i: (i, s)) if a_by_slot else (lambda s, i: (i, 0))),
                  pl.BlockSpec((tm, b_w), (lambda s, i: (i, s)) if b_by_slot else (lambda s, i: (i, 0)))],
        out_specs=pl.BlockSpec((None, a_w, b_w), lambda s, i: (s, 0, 0)),
        out_shape=jax.ShapeDtypeStruct((n_slots, a_w, b_w), BF16),
        scratch_shapes=[pltpu.VMEM((a_w, b_w), F32)], compiler_params=_params(2))(a, b)


def _ffn_bwd_a(dx, wout, silu, dgate, after):
    T, D = dx.shape
    F = silu.shape[1]
    tm = _pick_tm(T, 256)

    def body(dx_ref, w_ref, s_ref, dg_ref, after_ref, o_ref):
        dact = 0.5 * _dot_nt(dx_ref[...].astype(BF16), w_ref[...])
        o_ref[:, :F] = (dact * dg_ref[...].astype(F32)).astype(BF16)
        o_ref[:, F:] = (dact * s_ref[...].astype(F32)).astype(BF16)

    act_spec = pl.BlockSpec((tm, F), lambda i: (i, 0))
    return pl.pallas_call(
        body, name="ffn_bwd_a", grid=(T // tm,),
        in_specs=[pl.BlockSpec((tm, D), lambda i: (i, 0)), pl.BlockSpec((F, D), lambda i: (0, 0)),
                  act_spec, act_spec, ANY_SPEC],
        out_specs=pl.BlockSpec((tm, 2 * F), lambda i: (i, 0)),
        out_shape=jax.ShapeDtypeStruct((T, 2 * F), BF16),
        compiler_params=_params(1))(dx, wout, silu, dgate, after)


def _bwd_into_norm(name, d, d_w, w, n_slots, x, g, dx_in):
    T, D = x.shape
    tm = _pick_tm(T, 256)

    def body(*refs):
        d_refs, w_refs = refs[:n_slots], refs[n_slots:2 * n_slots]
        x_ref, g_ref, dxi_ref, dx_ref, dg_ref = refs[2 * n_slots:]
        dh = _dot_nt(d_refs[0][...], w_refs[0][...])
        for s in range(1, n_slots):
            dh = dh + _dot_nt(d_refs[s][...], w_refs[s][...])
        dxn, dg = _rms_bwd_tile(x_ref[...], g_ref[...], dh)
        dx_ref[...] = dxi_ref[...] + dxn

        @pl.when(pl.program_id(0) == 0)
        def _():
            dg_ref[...] = jnp.zeros_like(dg_ref)

        dg_ref[...] += dg

    row = pl.BlockSpec((tm, D), lambda i: (i, 0))
    vec = pl.BlockSpec((1, D), lambda i: (0, 0))
    return pl.pallas_call(
        body, name=name, grid=(T // tm,),
        in_specs=[pl.BlockSpec((tm, d_w), lambda i, s=s: (i, s)) for s in range(n_slots)]
        + [pl.BlockSpec((None, D, d_w), lambda i, s=s: (s, 0, 0)) for s in range(n_slots)]
        + [row, vec, row],
        out_specs=[row, vec],
        out_shape=[jax.ShapeDtypeStruct((T, D), F32), jax.ShapeDtypeStruct((1, D), F32)],
        compiler_params=_params(1))(*([d] * n_slots + [w] * n_slots + [x, g, dx_in]))


def _ple_fwd(x, g, p, wg, wp):
    T, D = x.shape
    P = p.shape[1]
    tm = _pick_tm(T, 256)

    def body(x_ref, g_ref, p_ref, wg_ref, wp_ref, hn_ref, xo_ref, gate_ref, pp_ref):
        xv = x_ref[...]
        hn = _rms_tile(xv, g_ref[...])
        hn_ref[...] = hn
        gate = _sigmoid(_dot(hn, wg_ref[...]))
        pp = _dot(p_ref[...].astype(BF16), wp_ref[...])
        gate_ref[...] = gate
        pp_ref[...] = pp
        xo_ref[...] = xv + gate * pp

    row = pl.BlockSpec((tm, D), lambda i: (i, 0))
    out = jax.ShapeDtypeStruct((T, D), F32)
    return pl.pallas_call(
        body, name="ple_fwd", grid=(T // tm,),
        in_specs=[row, pl.BlockSpec((1, D), lambda i: (0, 0)), pl.BlockSpec((tm, P), lambda i: (i, 0)),
                  pl.BlockSpec((D, D), lambda i: (0, 0)), pl.BlockSpec((P, D), lambda i: (0, 0))],
        out_specs=[row, row, row, row], out_shape=[jax.ShapeDtypeStruct((T, D), BF16), out, out, out],
        compiler_params=_params(1))(x, g, p, wg, wp)


def _ple_bwd(dx, gate, pp, hn, p, x, g, wg, after):
    T, D = x.shape
    P = p.shape[1]
    tm = _pick_tm(T, 256)
    nt = T // tm

    def body(dx_ref, gate_ref, pp_ref, hn_ref, p_ref, x_ref, g_ref, wg_ref, after_ref,
             dxo_ref, dg_ref, dwg_ref, dwp_ref, acc_g, acc_p):
        i = pl.program_id(0)

        @pl.when(i == 0)
        def _():
            acc_g[...] = jnp.zeros_like(acc_g)
            acc_p[...] = jnp.zeros_like(acc_p)
            dg_ref[...] = jnp.zeros_like(dg_ref)

        dxv = dx_ref[...]
        gate = gate_ref[...]
        dz = (dxv * pp_ref[...] * gate * (1.0 - gate)).astype(BF16)
        dpp = (dxv * gate).astype(BF16)
        acc_g[...] += _dot_tn(hn_ref[...], dz)
        acc_p[...] += _dot_tn(p_ref[...].astype(BF16), dpp)
        dxn, dg = _rms_bwd_tile(x_ref[...], g_ref[...], _dot_nt(dz, wg_ref[...]))
        dxo_ref[...] = dxv + dxn
        dg_ref[...] += dg

        @pl.when(i == nt - 1)
        def _():
            dwg_ref[...] = acc_g[...].astype(BF16)
            dwp_ref[...] = acc_p[...].astype(BF16)

    row = pl.BlockSpec((tm, D), lambda i: (i, 0))
    vec = pl.BlockSpec((1, D), lambda i: (0, 0))
    return pl.pallas_call(
        body, name="ple_bwd", grid=(nt,),
        in_specs=[row, row, row, row, pl.BlockSpec((tm, P), lambda i: (i, 0)), row, vec,
                  pl.BlockSpec((D, D), lambda i: (0, 0)), ANY_SPEC],
        out_specs=[row, vec, pl.BlockSpec((D, D), lambda i: (0, 0)), pl.BlockSpec((P, D), lambda i: (0, 0))],
        out_shape=[jax.ShapeDtypeStruct((T, D), F32), jax.ShapeDtypeStruct((1, D), F32),
                   jax.ShapeDtypeStruct((D, D), BF16), jax.ShapeDtypeStruct((P, D), BF16)],
        scratch_shapes=[pltpu.VMEM((D, D), F32), pltpu.VMEM((P, D), F32)],
        compiler_params=_params(1))(dx, gate, pp, hn, p, x, g, wg, after)


def _loss_fwd_bwd(y, tgt):
    T, D = y.shape
    tm = _pick_tm(T, 512)

    def body(y_ref, t_ref, dy_ref, loss_ref):
        e = y_ref[...] - t_ref[...]
        dy_ref[...] = e / D

        @pl.when(pl.program_id(0) == 0)
        def _():
            loss_ref[...] = jnp.zeros_like(loss_ref)

        loss_ref[...] += 0.5 * jnp.sum(jnp.mean(e * e, axis=-1, keepdims=True), axis=0, keepdims=True)

    row = pl.BlockSpec((tm, D), lambda i: (i, 0))
    return pl.pallas_call(
        body, name="loss", grid=(T // tm,), in_specs=[row, row],
        out_specs=[row, pl.BlockSpec((8, LANES), lambda i: (0, 0))],
        out_shape=[jax.ShapeDtypeStruct((T, D), F32), jax.ShapeDtypeStruct((8, LANES), F32)],
        compiler_params=_params(1))(y, tgt)


def _qkv_layout(D):
    n_a = D // (2 * LANES)
    n_b = D // (2 * LANES)
    n_kv = max(1, (2 * n_b) // GROUP_B) * HEAD_DIM // LANES
    return n_a, n_b, n_kv


def _dup_half(xv, half):
    rolled = pltpu.roll(xv, HEAD_DIM, 1)
    lo = _lo_lanes(xv.shape)
    return jnp.where(lo, xv, rolled) if half == 0 else jnp.where(lo, rolled, xv)


def _qkv_post(raw, gains):
    T, W = raw.shape
    n_a, n_b, n_kv = _qkv_layout(W * 4 // 9)
    tm = _pick_tm(T, 256)
    o_qb = 3 * n_a

    def norm(xv, gv, scale):
        ms = _seg_sum(xv * xv) * (1.0 / HEAD_DIM)
        return xv * lax.rsqrt(ms + EPS) * gv * scale

    def body(raw_ref, g_ref, a_ref, b_ref):
        def blk(cb):
            return raw_ref[:, cb * LANES:(cb + 1) * LANES]

        def gn(cb):
            return g_ref[:, cb * LANES:(cb + 1) * LANES]

        for cb in range(n_a):
            a_ref[:, cb * LANES:(cb + 1) * LANES] = norm(blk(cb), gn(cb), Q_SCALE)
            cbk = n_a + cb
            a_ref[:, cbk * LANES:(cbk + 1) * LANES] = norm(blk(cbk), gn(cbk), 1.0)
            cbv = 2 * n_a + cb
            a_ref[:, cbv * LANES:(cbv + 1) * LANES] = blk(cbv)
        for cb in range(n_b):
            src = o_qb + cb
            b_ref[:, cb * LANES:(cb + 1) * LANES] = norm(blk(src), gn(src), Q_SCALE)
        for e in range(n_b):
            kvh = (2 * e) // GROUP_B
            ck = o_qb + n_b + kvh // 2
            cv = ck + n_kv
            kn = norm(blk(ck), gn(ck), 1.0)
            b_ref[:, (n_b + e) * LANES:(n_b + e + 1) * LANES] = _dup_half(kn, kvh % 2)
            b_ref[:, (2 * n_b + e) * LANES:(2 * n_b + e + 1) * LANES] = _dup_half(blk(cv), kvh % 2)

    wa, wb = 3 * n_a * LANES, 3 * n_b * LANES
    return pl.pallas_call(
        body, name="qkv_post", grid=(T // tm,),
        in_specs=[pl.BlockSpec((tm, W), lambda i: (i, 0)), pl.BlockSpec((1, W), lambda i: (0, 0))],
        out_specs=[pl.BlockSpec((tm, wa), lambda i: (i, 0)), pl.BlockSpec((tm, wb), lambda i: (i, 0))],
        out_shape=[jax.ShapeDtypeStruct((T, wa), F32), jax.ShapeDtypeStruct((T, wb), F32)],
        compiler_params=_params(1))(raw, gains)


def _qkv_post_bwd(raw, gains, d_a, d_b):
    T, W = raw.shape
    n_a, n_b, n_kv = _qkv_layout(W * 4 // 9)
    tm = _pick_tm(T, 256)
    o_qb = 3 * n_a

    def body(raw_ref, g_ref, daq, dak, dav, dbq, dbk, dbv, o_ref, dg_ref):
        @pl.when(pl.program_id(0) == 0)
        def _():
            dg_ref[...] = jnp.zeros_like(dg_ref)

        def cols(ref, cb):
            return ref[:, cb * LANES:(cb + 1) * LANES]

        def norm_bwd(cb, dy, scale):
            xv = cols(raw_ref, cb)
            gv = cols(g_ref, cb)
            r = lax.rsqrt(_seg_sum(xv * xv) * (1.0 / HEAD_DIM) + EPS)
            xh = xv * r
            dys = dy * scale
            dyg = dys * gv
            dxv = r * (dyg - xh * (_seg_sum(dyg * xh) * (1.0 / HEAD_DIM)))
            o_ref[:, cb * LANES:(cb + 1) * LANES] = dxv.astype(BF16)
            dg_ref[:, cb * LANES:(cb + 1) * LANES] += jnp.sum(dys * xh, axis=0, keepdims=True)

        def fold(ref, kv_blk):
            halves = []
            for half in range(2):
                kvh = 2 * kv_blk + half
                blocks = [e for e in range(n_b) if (2 * e) // GROUP_B == kvh]
                s = cols(ref, blocks[0])
                for e in blocks[1:]:
                    s = s + cols(ref, e)
                halves.append(s + pltpu.roll(s, HEAD_DIM, 1))
            return jnp.where(_lo_lanes(halves[0].shape), halves[0], halves[1])

        for cb in range(n_a):
            norm_bwd(cb, cols(daq, cb), Q_SCALE)
            norm_bwd(n_a + cb, cols(dak, cb), 1.0)
            cbv = 2 * n_a + cb
            o_ref[:, cbv * LANES:(cbv + 1) * LANES] = cols(dav, cb).astype(BF16)
        for cb in range(n_b):
            norm_bwd(o_qb + cb, cols(dbq, cb), Q_SCALE)
        for kb in range(n_kv):
            ck = o_qb + n_b + kb
            cv = ck + n_kv
            norm_bwd(ck, fold(dbk, kb), 1.0)
            o_ref[:, cv * LANES:(cv + 1) * LANES] = fold(dbv, kb).astype(BF16)

    hw_a, hw_b = n_a * LANES, n_b * LANES
    return pl.pallas_call(
        body, name="qkv_post_bwd", grid=(T // tm,),
        in_specs=[pl.BlockSpec((tm, W), lambda i: (i, 0)), pl.BlockSpec((1, W), lambda i: (0, 0))]
        + [pl.BlockSpec((tm, hw_a), lambda i: (i, 0))] * 3 + [pl.BlockSpec((tm, hw_b), lambda i: (i, 0))] * 3,
        out_specs=[pl.BlockSpec((tm, W), lambda i: (i, 0)), pl.BlockSpec((1, W), lambda i: (0, 0))],
        out_shape=[jax.ShapeDtypeStruct((T, W), BF16), jax.ShapeDtypeStruct((1, W), F32)],
        compiler_params=_params(1))(raw, gains, *d_a, *d_b)


def _t5_bucket_np(rel):
    half = N_BUCKETS // 2
    max_exact = half // 2
    ret = np.where(rel > 0, half, 0)
    n = np.abs(rel)
    nf = np.maximum(n, 1).astype(np.float32)
    large = max_exact + (np.log(nf / np.float32(max_exact)) / np.float32(math.log(MAX_DISTANCE / max_exact))
                         * np.float32(half - max_exact)).astype(np.int32)
    large = np.minimum(large, half - 1)
    return ret + np.where(n < max_exact, n, large)


def _window_pad(radius):
    assert radius <= QBLOCK
    return HEAD_DIM if radius <= HEAD_DIM else QBLOCK


def _bucket_maps(configs):
    pad = _window_pad(configs[0][0])
    q = np.arange(QBLOCK)[:, None]
    kk = np.arange(QBLOCK + 2 * pad)[None, :]
    rel = kk - pad - q
    maps = [np.where(np.abs(rel) <= radius, _t5_bucket_np(rel * dil), -1) for radius, dil in configs]
    return np.stack(maps).astype(np.int32)


def _bias_build(rel_bias, bmaps, col0):
    n_sets, _, W = bmaps.shape
    n_heads = rel_bias.shape[1] // 2

    def body(rb_ref, bm_ref, o_ref):
        h = pl.program_id(1)
        bm = bm_ref[...]

        def step(n, acc):
            return jnp.where(bm == n, rb_ref[n, col0 + h], acc)

        o_ref[...] = lax.fori_loop(0, N_BUCKETS, step, jnp.where(bm < 0, NEG, 0.0).astype(F32))

    return pl.pallas_call(
        body, name="bias_build", grid=(n_sets, n_heads),
        in_specs=[pl.BlockSpec(memory_space=pltpu.SMEM), pl.BlockSpec((None, QBLOCK, W), lambda s, h: (s, 0, 0))],
        out_specs=pl.BlockSpec((None, None, QBLOCK, W), lambda s, h: (s, h, 0, 0)),
        out_shape=jax.ShapeDtypeStruct((n_sets, n_heads, QBLOCK, W), F32),
        compiler_params=_params(2))(rel_bias, bmaps)


def _bias_grad(dtiles, bmaps, col0):
    n_sets, _, W = bmaps.shape
    n_heads = dtiles[0].shape[1]
    n_l = len(dtiles)

    def body(*refs):
        bm_ref, o_ref = refs[0], refs[1 + n_l]
        s, h = pl.program_id(0), pl.program_id(1)

        @pl.when((s == 0) & (h == 0))
        def _():
            o_ref[...] = jnp.zeros_like(o_ref)

        d = refs[1][...]
        for r in refs[2:1 + n_l]:
            d = d + r[...]
        acc8 = d[0:8, :]
        for a in range(1, QBLOCK // 8):
            acc8 = acc8 + pltpu.roll(d[8 * a:8 * a + 8, :], W - 8 * a, 1)
        per_offset = acc8[0:1, :]
        for b in range(1, 8):
            per_offset = per_offset + pltpu.roll(acc8[b:b + 1, :], W - b, 1)
        bucket = lax.broadcasted_iota(jnp.int32, (N_BUCKETS, W), 0)
        hit = bucket == bm_ref[0:1, :]
        per_bucket = jnp.sum(jnp.where(hit, per_offset, 0.0), axis=1, keepdims=True)
        lanes = lax.broadcasted_iota(jnp.int32, o_ref.shape, 1)
        o_ref[...] += jnp.where(lanes == col0 + h, per_bucket, 0.0)

    tile = pl.BlockSpec((None, None, QBLOCK, W), lambda s, h: (s, h, 0, 0))
    return pl.pallas_call(
        body, name="bias_grad", grid=(n_sets, n_heads),
        in_specs=[pl.BlockSpec((None, QBLOCK, W), lambda s, h: (s, 0, 0))] + [tile] * n_l,
        out_specs=pl.BlockSpec((N_BUCKETS, LANES), lambda s, h: (0, 0)),
        out_shape=jax.ShapeDtypeStruct((N_BUCKETS, LANES), F32), compiler_params=_params(2))(bmaps, *dtiles)


def _rows(l_start, n, d, r):
    if d == 1:
        return pl.ds(pl.multiple_of(l_start, 8), n)
    return pl.ds(l_start * d + r, n, stride=d)


def _stack_heads(xv, lo):
    z = jnp.zeros_like(xv)
    return jnp.concatenate([jnp.where(lo, xv, z), jnp.where(lo, z, xv)], axis=0)


def _unstack_heads(xv, lo):
    return jnp.where(lo, xv[:QBLOCK], xv[QBLOCK:])


def _per_head_rows(v0, v1):
    if jnp.ndim(v0) == 0:
        return jnp.where(lax.broadcasted_iota(jnp.int32, (2 * QBLOCK, 1), 0) < QBLOCK, v0, v1)
    return jnp.concatenate([v0, v1], axis=0)


def _block_geometry(b, nb_sub, pad):
    r, lb = b // nb_sub, b % nb_sub
    l0 = lb * QBLOCK
    lp = jnp.maximum(l0 - pad, 0)
    ln = jnp.minimum(l0 + QBLOCK, nb_sub * QBLOCK - pad)
    col = lax.broadcasted_iota(jnp.int32, (2 * QBLOCK, QBLOCK + 2 * pad), 1)
    valid = ((col >= pad) | (lb != 0)) & ((col < pad + QBLOCK) | (lb != nb_sub - 1))
    return r, l0, lp, ln, valid


def _window(ref, l0, lp, ln, pad, d, r):
    return jnp.concatenate([ref[_rows(lp, pad, d, r), :], ref[_rows(l0, QBLOCK, d, r), :],
                            ref[_rows(ln, pad, d, r), :]], axis=0)


def _attn_fwd(qkv, bias, sink, dils, pad):
    T = qkv.shape[0]
    hw = qkv.shape[1] // 3
    ng = hw // LANES
    n_br = len(dils)
    n_blocks = T // QBLOCK
    W = QBLOCK + 2 * pad
    chunk = 256

    def body(sink_ref, q_ref, k_ref, v_ref, bias_ref, o_ref, lse_ref, *scratch):
        g = pl.program_id(0)
        lo = _lo_lanes((QBLOCK, LANES))
        snk = _per_head_rows(sink_ref[2 * g], sink_ref[2 * g + 1])
        for c, d in enumerate(dils):
            nb_sub = n_blocks // d
            o_dst = scratch[0].at[c] if n_br > 1 else o_ref
            l_dst = scratch[1].at[c] if n_br > 1 else lse_ref

            def block(b, carry, c=c, d=d, nb_sub=nb_sub, o_dst=o_dst, l_dst=l_dst):
                r, l0, lp, ln, valid = _block_geometry(b, nb_sub, pad)
                q = _stack_heads(q_ref[_rows(l0, QBLOCK, d, r), :].astype(BF16), lo)
                k = _window(k_ref, l0, lp, ln, pad, d, r).astype(BF16)
                v = _window(v_ref, l0, lp, ln, pad, d, r).astype(BF16)
                s = jnp.where(valid, _dot_nt(q, k) + bias_ref[c], NEG)
                m = jnp.maximum(jnp.max(s, axis=1, keepdims=True), snk)
                p = jnp.exp(s - m)
                den = jnp.sum(p, axis=1, keepdims=True) + jnp.exp(snk - m)
                o_dst[_rows(l0, QBLOCK, d, r), :] = _unstack_heads(_dot(p.astype(BF16), v) / den, lo)
                l_dst[_rows(l0, QBLOCK, d, r), :] = _unstack_heads(
                    jnp.broadcast_to(m + jnp.log(den), (2 * QBLOCK, LANES)), lo)
                return carry

            lax.fori_loop(0, n_blocks, block, 0, unroll=FWD_UNROLL)

        if n_br > 1:
            def merge(i, carry):
                rs = pl.ds(pl.multiple_of(i * chunk, chunk), chunk)
                ls = [scratch[1][c, rs, :] for c in range(n_br)]
                m = ls[0]
                for t in ls[1:]:
                    m = jnp.maximum(m, t)
                ws = [jnp.exp(t - m) for t in ls]
                z = ws[0]
                acc = ws[0] * scratch[0][0, rs, :]
                for c in range(1, n_br):
                    z = z + ws[c]
                    acc = acc + ws[c] * scratch[0][c, rs, :]
                o_ref[rs, :] = acc / z
                lse_ref[rs, :] = m + jnp.log(z)
                return carry

            lax.fori_loop(0, T // chunk, merge, 0)

    def col(base):
        return pl.BlockSpec((T, LANES), lambda g: (0, base + g))

    out = jax.ShapeDtypeStruct((T, hw), F32)
    scratch = [pltpu.VMEM((n_br, T, LANES), F32)] * 2 if n_br > 1 else []
    return pl.pallas_call(
        body, name="attn_fwd", grid=(ng,),
        in_specs=[pl.BlockSpec(memory_space=pltpu.SMEM), col(0), col(ng), col(2 * ng),
                  pl.BlockSpec((n_br, 2 * QBLOCK, W), lambda g: (0, g, 0))],
        out_specs=[col(0), col(0)], out_shape=[out, out], scratch_shapes=scratch,
        compiler_params=_params(1))(sink, qkv, qkv, qkv, bias.reshape(n_br, -1, W))


def _attn_bwd(qkv, bias, sink, dils, pad, do, lse, dd, col_base):
    T = qkv.shape[0]
    hw = qkv.shape[1] // 3
    ng = hw // LANES
    n_br = len(dils)
    n_blocks = T // QBLOCK
    W = QBLOCK + 2 * pad

    def body(sink_ref, q_ref, k_ref, v_ref, bias_ref, do_ref, lse_ref, dd_ref,
             dq_ref, dk_ref, dv_ref, dt_ref, ds_ref):
        g = pl.program_id(0)
        dq_ref[...] = jnp.zeros_like(dq_ref)
        dk_ref[...] = jnp.zeros_like(dk_ref)
        dv_ref[...] = jnp.zeros_like(dv_ref)
        dt_ref[...] = jnp.zeros_like(dt_ref)
        ds_ref[...] = jnp.zeros_like(ds_ref)
        lo = _lo_lanes((QBLOCK, LANES))
        snk = jnp.where(lo, sink_ref[2 * g], sink_ref[2 * g + 1])
        for c, d in enumerate(dils):
            nb_sub = n_blocks // d

            def block(b, carry, c=c, d=d, nb_sub=nb_sub):
                r, l0, lp, ln, valid = _block_geometry(b, nb_sub, pad)
                rows_q = _rows(l0, QBLOCK, d, r)
                q = _stack_heads(q_ref[rows_q, :].astype(BF16), lo)
                k = _window(k_ref, l0, lp, ln, pad, d, r).astype(BF16)
                v = _window(v_ref, l0, lp, ln, pad, d, r).astype(BF16)
                dob = _stack_heads(do_ref[rows_q, :].astype(BF16), lo)
                lse_b = lse_ref[rows_q, :]
                dd_b = dd_ref[rows_q, :]
                s = jnp.where(valid, _dot_nt(q, k) + bias_ref[c], NEG)
                p = jnp.exp(s - _per_head_rows(lse_b[:, 0:1], lse_b[:, HEAD_DIM:HEAD_DIM + 1]))
                ds = p * (_dot_nt(dob, v) - _per_head_rows(dd_b[:, 0:1], dd_b[:, HEAD_DIM:HEAD_DIM + 1]))
                dsb = ds.astype(BF16)
                dkw = _dot_tn(dsb, q)
                dvw = _dot_tn(p.astype(BF16), dob)
                dt_ref[c] += ds
                dq_ref[rows_q, :] += _unstack_heads(_dot(dsb, k), lo)
                ds_ref[0:1, :] += jnp.sum(-jnp.exp(snk - lse_b) * dd_b, axis=0, keepdims=True)
                for part, (start, n) in zip((0, pad, pad + QBLOCK), ((lp, pad), (l0, QBLOCK), (ln, pad))):
                    dk_ref[_rows(start, n, d, r), :] += dkw[part:part + n]
                    dv_ref[_rows(start, n, d, r), :] += dvw[part:part + n]
                return carry

            lax.fori_loop(0, n_blocks, block, 0, unroll=BWD_UNROLL)

    def col(base):
        return pl.BlockSpec((T, LANES), lambda g: (0, base + g))

    tile = pl.BlockSpec((n_br, 2 * QBLOCK, W), lambda g: (0, g, 0))
    full = jax.ShapeDtypeStruct((T, hw), F32)
    dq, dk, dv, dt, dsink = pl.pallas_call(
        body, name="attn_bwd", grid=(ng,),
        in_specs=[pl.BlockSpec(memory_space=pltpu.SMEM), col(0), col(ng), col(2 * ng), tile,
                  col(col_base), col(0), col(col_base)],
        out_specs=[col(0), col(0), col(0), tile, pl.BlockSpec((None, 8, LANES), lambda g: (g, 0, 0))],
        out_shape=[full, full, full, jax.ShapeDtypeStruct((n_br, 2 * ng * QBLOCK, W), F32),
                   jax.ShapeDtypeStruct((ng, 8, LANES), F32)],
        compiler_params=_params(1))(sink, qkv, qkv, qkv, bias.reshape(n_br, -1, W), do, lse, dd)
    return dq, dk, dv, dt.reshape(n_br, 2 * ng, QBLOCK, W), dsink


def _mix_bwd_in(dx, wo, o_list):
    T, D = dx.shape
    widths = [o.shape[1] for o in o_list]
    hw = sum(widths)
    n = len(o_list)
    tm = _pick_tm(T, 256)

    def body(*refs):
        dx_ref, w_ref, o_refs, do_ref, dd_ref = refs[0], refs[1], refs[2:2 + n], refs[2 + n], refs[3 + n]
        dov = _dot_nt(dx_ref[...].astype(BF16), w_ref[...])
        do_ref[...] = dov
        off = 0
        for o_ref, k in zip(o_refs, widths):
            prod = dov[:, off:off + k] * o_ref[...]
            for cb in range(k // LANES):
                dd_ref[:, off + cb * LANES:off + (cb + 1) * LANES] = _seg_sum(prod[:, cb * LANES:(cb + 1) * LANES])
            off += k

    row = pl.BlockSpec((tm, hw), lambda i: (i, 0))
    out = jax.ShapeDtypeStruct((T, hw), F32)
    return pl.pallas_call(
        body, name="mix_bwd_in", grid=(T // tm,),
        in_specs=[pl.BlockSpec((tm, D), lambda i: (i, 0)), pl.BlockSpec((hw, D), lambda i: (0, 0))]
        + [pl.BlockSpec((tm, k), lambda i: (i, 0)) for k in widths],
        out_specs=[row, row], out_shape=[out, out], compiler_params=_params(1))(dx, wo, *o_list)


def _mesh_pos():
    return lax.axis_index("x"), lax.axis_index("y"), lax.axis_index("c")


def _my_chip():
    return 2 * lax.axis_index("x") + lax.axis_index("y")


def _other_chips(x, y):
    return [(1 - x, y), (x, 1 - y), (1 - x, 1 - y)]


def _half_rows(rows, cc):
    hr = rows // 2
    return pl.ds(pl.multiple_of(cc * hr, 16), hr)


def _cast_into_slot(w, l):
    _, R, C = w.shape
    tm = _pick_tm(R, 512)

    def body(w_ref, o_ref):
        o_ref[...] = w_ref[...].astype(BF16)

    return pl.pallas_call(
        body, name="cast_into_slot", grid=(R // tm,),
        in_specs=[pl.BlockSpec((None, tm, C), lambda i: (l, i, 0))],
        out_specs=pl.BlockSpec((None, tm, C), lambda i: (_my_chip(), i, 0)),
        out_shape=jax.ShapeDtypeStruct((N_CHIPS, R, C), BF16), compiler_params=_params(1))(w)


def _split_start(name, arrays, make_copies, n_sem, after):
    n = len(arrays)
    n_in = n + (0 if after is None else 1)

    def body(*refs):
        send_s, recv_s = refs[n_in], refs[n_in + 1]
        token = refs[n_in + 2 + n]
        for send, _ in make_copies(refs[:n], send_s, recv_s):
            send.start()
        token[...] = jnp.zeros_like(token)

    res = pl.pallas_call(
        body, name=name,
        out_shape=(pltpu.SemaphoreType.DMA((n_sem,)), pltpu.SemaphoreType.DMA((n_sem,)),
                   *[pltpu.HBM(a.shape, a.dtype) for a in arrays], jax.ShapeDtypeStruct((8, LANES), F32)),
        in_specs=[HBM_SPEC] * n + [ANY_SPEC] * (n_in - n),
        out_specs=(SEM_SPEC, SEM_SPEC, *([HBM_SPEC] * n), pl.BlockSpec(memory_space=pltpu.VMEM)),
        input_output_aliases={i: 2 + i for i in range(n)},
        compiler_params=pltpu.CompilerParams(has_side_effects=pltpu.SideEffectType.DATAFLOW_SIDE_EFFECTING),
    )(*[pltpu.with_memory_space_constraint(a, pltpu.HBM) for a in arrays], *([] if after is None else [after]))
    return res[0], res[1], list(res[2:2 + n]), res[2 + n]


def _split_wait(name, send_s, recv_s, arrays, make_copies, after):
    n = len(arrays)

    def body(*refs):
        for send, landed in make_copies(refs[:n], refs[n], refs[n + 1]):
            send.wait_send()
            landed.wait_recv()

    return list(pl.pallas_call(
        body, name=name, out_shape=[pltpu.HBM(a.shape, a.dtype) for a in arrays],
        in_specs=[HBM_SPEC] * n + [SEM_SPEC, SEM_SPEC, ANY_SPEC], out_specs=[HBM_SPEC] * n,
        input_output_aliases={i: i for i in range(n)},
        compiler_params=pltpu.CompilerParams(has_side_effects=pltpu.SideEffectType.DATAFLOW_SIDE_EFFECTING),
    )(*arrays, send_s, recv_s, after))


def _gather_copies(shapes):
    n = len(shapes)

    def make(refs, send_s, recv_s):
        x, y, c = _mesh_pos()
        my = 2 * x + y
        copies = []
        for w in range(n):
            for k, (px, py) in enumerate(_other_chips(x, y)):
                def part(slot, w=w):
                    return refs[w].at[slot, _half_rows(shapes[w][1], c), :]
                sems = dict(send_sem=send_s.at[k * n + w], recv_sem=recv_s.at[k * n + w],
                            device_id=(px, py, c), device_id_type=MESH)
                copies.append((pltpu.make_async_remote_copy(src_ref=part(my), dst_ref=part(my), **sems),
                               pltpu.make_async_remote_copy(src_ref=part(2 * px + py), dst_ref=part(2 * px + py), **sems)))
        return copies

    return make


def _pair_forward(bufs):
    n = len(bufs)

    def body(*refs):
        arr = refs[n:2 * n]
        send_s, recv_s = refs[2 * n:]
        x, y, c = _mesh_pos()
        sent, landed = [], []
        for w in range(n):
            for k, (px, py) in enumerate(_other_chips(x, y)):
                def part(cc, w=w, slot=2 * px + py):
                    return arr[w].at[slot, _half_rows(bufs[w].shape[1], cc), :]
                sems = dict(send_sem=send_s.at[k * n + w], recv_sem=recv_s.at[k * n + w],
                            device_id=(x, y, 1 - c), device_id_type=MESH)
                sent.append(pltpu.make_async_remote_copy(src_ref=part(c), dst_ref=part(c), **sems))
                landed.append(pltpu.make_async_remote_copy(src_ref=part(1 - c), dst_ref=part(1 - c), **sems))
        for cp in sent:
            cp.start()
        for cp in landed:
            cp.wait_recv()
        for cp in sent:
            cp.wait_send()

    return list(pl.pallas_call(
        body, name="ag_pair_forward", in_specs=[HBM_SPEC] * n, out_specs=[HBM_SPEC] * n,
        out_shape=[jax.ShapeDtypeStruct(b.shape, b.dtype) for b in bufs],
        input_output_aliases={w: w for w in range(n)},
        scratch_shapes=[pltpu.SemaphoreType.DMA((3 * n,)), pltpu.SemaphoreType.DMA((3 * n,))],
    )(*bufs))


def _rs_pair_exchange(grads):
    n = len(grads)

    def body(*refs):
        ins, recv = refs[:n], refs[n:2 * n]
        send_s, recv_s = refs[2 * n:]
        x, y, c = _mesh_pos()
        remote = [pltpu.make_async_remote_copy(
            src_ref=ins[t].at[:, _half_rows(grads[t].shape[1], 1 - c), :], dst_ref=recv[t],
            send_sem=send_s.at[t], recv_sem=recv_s.at[t], device_id=(x, y, 1 - c), device_id_type=MESH)
            for t in range(n)]
        for cp in remote:
            cp.start()
        for cp in remote:
            cp.wait_recv()
        for cp in remote:
            cp.wait_send()

    return pl.pallas_call(
        body, name="rs_pair_exchange", in_specs=[HBM_SPEC] * n, out_specs=[HBM_SPEC] * n,
        out_shape=[jax.ShapeDtypeStruct((g.shape[0], g.shape[1] // 2, g.shape[2]), g.dtype) for g in grads],
        scratch_shapes=[pltpu.SemaphoreType.DMA((n,)), pltpu.SemaphoreType.DMA((n,))],
    )(*grads)


def _rs_add_pair(grad, recv):
    n_slot, hr, C = recv.shape
    tm = _pick_tm(hr, 192)
    nb = hr // tm

    def body(a_ref, b_ref, o_ref):
        o_ref[...] = (a_ref[...].astype(F32) + b_ref[...].astype(F32)).astype(BF16)

    blk = pl.BlockSpec((n_slot, tm, C), lambda i: (0, i, 0))
    return pl.pallas_call(
        body, name="rs_add_pair", grid=(nb,),
        in_specs=[pl.BlockSpec((n_slot, tm, C), lambda i: (0, lax.axis_index("c") * nb + i, 0)), blk],
        out_specs=blk, out_shape=jax.ShapeDtypeStruct(recv.shape, BF16), compiler_params=_params(1))(grad, recv)


def _scatter_copies(n):
    def make(refs, send_s, recv_s):
        x, y, c = _mesh_pos()
        copies = []
        for t in range(n):
            for k, (px, py) in enumerate(_other_chips(x, y)):
                sems = dict(send_sem=send_s.at[3 * t + k], recv_sem=recv_s.at[3 * t + k],
                            device_id=(px, py, c), device_id_type=MESH)
                land = refs[n + t].at[k]
                copies.append((pltpu.make_async_remote_copy(src_ref=refs[t].at[2 * px + py], dst_ref=land, **sems),
                               pltpu.make_async_remote_copy(src_ref=land, dst_ref=land, **sems)))
        return copies

    return make


def _rs_add_chips(part, recv):
    _, hr, C = part.shape
    tm = _pick_tm(hr, 256)
    nb = hr // tm

    def body(a_ref, r0, r1, r2, o_ref):
        o_ref[...] = ((a_ref[...].astype(F32) + r0[...].astype(F32)) + r1[...].astype(F32)) + r2[...].astype(F32)

    def rel(k):
        return pl.BlockSpec((None, tm, C), lambda i: (k, i, 0))

    return pl.pallas_call(
        body, name="rs_add_chips", grid=(nb,),
        in_specs=[pl.BlockSpec((None, tm, C), lambda i: (_my_chip(), i, 0)), rel(0), rel(1), rel(2)],
        out_specs=pl.BlockSpec((tm, C), lambda i: (lax.axis_index("c") * nb + i, 0)),
        out_shape=jax.ShapeDtypeStruct((2 * hr, C), F32), compiler_params=_params(1))(part, recv, recv, recv)


def _rs_pair_share(halves):
    n = len(halves)

    def body(*refs):
        bufs = refs[n:2 * n]
        send_s, recv_s = refs[2 * n:]
        x, y, c = _mesh_pos()

        def half(t, cc):
            return bufs[t].at[_half_rows(halves[t].shape[0], cc), :]

        sent = [pltpu.make_async_remote_copy(
            src_ref=half(t, c), dst_ref=half(t, c), send_sem=send_s.at[t], recv_sem=recv_s.at[t],
            device_id=(x, y, 1 - c), device_id_type=MESH) for t in range(n)]
        landed = [pltpu.make_async_remote_copy(
            src_ref=half(t, 1 - c), dst_ref=half(t, 1 - c), send_sem=send_s.at[t], recv_sem=recv_s.at[t],
            device_id=(x, y, 1 - c), device_id_type=MESH) for t in range(n)]
        for cp in sent:
            cp.start()
        for cp in landed:
            cp.wait_recv()
        for cp in sent:
            cp.wait_send()

    return pl.pallas_call(
        body, name="rs_pair_share", in_specs=[HBM_SPEC] * n, out_specs=[HBM_SPEC] * n,
        out_shape=[jax.ShapeDtypeStruct(h.shape, h.dtype) for h in halves],
        input_output_aliases={t: t for t in range(n)},
        scratch_shapes=[pltpu.SemaphoreType.DMA((n,)), pltpu.SemaphoreType.DMA((n,))],
    )(*halves)


def _allreduce_small(v):
    rows = v.shape[0]

    def body(v_ref, o_ref, buf, send_s, recv_s):
        x, y, c = _mesh_pos()
        me = 4 * x + 2 * y + c
        buf[me] = v_ref[...]
        copies = []
        for r in range(1, N_DEV):
            px = 1 - x if r & 4 else x
            py = 1 - y if r & 2 else y
            pc = 1 - c if r & 1 else c
            send = pltpu.make_async_remote_copy(
                src_ref=v_ref, dst_ref=buf.at[me], send_sem=send_s.at[r - 1], recv_sem=recv_s.at[r - 1],
                device_id=(px, py, pc), device_id_type=MESH)
            peer_slot = buf.at[4 * px + 2 * py + pc]
            landed = pltpu.make_async_remote_copy(
                src_ref=peer_slot, dst_ref=peer_slot, send_sem=send_s.at[r - 1], recv_sem=recv_s.at[r - 1],
                device_id=(px, py, pc), device_id_type=MESH)
            copies.append((send, landed))
        for send, _ in copies:
            send.start()
        for _, landed in copies:
            landed.wait_recv()
        for send, _ in copies:
            send.wait_send()
        acc = buf[0]
        for j in range(1, N_DEV):
            acc = acc + buf[j]
        o_ref[...] = acc

    vm = pl.BlockSpec(memory_space=pltpu.VMEM)
    return pl.pallas_call(
        body, name="allreduce_small", in_specs=[vm], out_specs=vm,
        out_shape=jax.ShapeDtypeStruct((rows, LANES), F32),
        scratch_shapes=[pltpu.VMEM((N_DEV, rows, LANES), F32), pltpu.SemaphoreType.DMA((N_DEV - 1,)),
                        pltpu.SemaphoreType.DMA((N_DEV - 1,))],
    )(v)


def _adamw_fn(w, g, m, v):
    m2 = ADAM_B1 * m + (1.0 - ADAM_B1) * g
    v2 = ADAM_B2 * v + (1.0 - ADAM_B2) * (g * g)
    m_hat = m2 / (1.0 - ADAM_B1 ** ADAM_STEP)
    v_hat = v2 / (1.0 - ADAM_B2 ** ADAM_STEP)
    delta = -ADAM_LR * (m_hat / (jnp.sqrt(v_hat) + ADAM_EPS) + ADAM_WD * w)
    return g, delta, m2, v2


def _adamw_layer(w, g, m, v, l, prev):
    NL, R, C = w.shape
    tm = _pick_tm(R, 128)
    n_prev = 0 if prev is None else 4

    def body(w_ref, g_ref, m_ref, v_ref, *rest):
        outs = rest[n_prev:]
        for o_ref, val in zip(outs, _adamw_fn(w_ref[...], g_ref[...], m_ref[...], v_ref[...])):
            o_ref[...] = val

    lay = pl.BlockSpec((None, tm, C), lambda i: (l, i, 0))
    shape = jax.ShapeDtypeStruct((NL, R, C), F32)
    return pl.pallas_call(
        body, name="adamw", grid=(R // tm,),
        in_specs=[lay, pl.BlockSpec((tm, C), lambda i: (i, 0)), lay, lay] + [pl.BlockSpec(memory_space=pl.ANY)] * n_prev,
        out_specs=[lay] * 4, out_shape=[shape] * 4,
        input_output_aliases={4 + j: j for j in range(n_prev)},
        compiler_params=_params(1))(w, g, m, v, *(prev or []))


def _pack_small(parts):
    out = []
    for a in parts:
        flat = a.reshape(-1)
        n = -(-flat.shape[0] // (8 * LANES)) * 8 * LANES
        out.append(jnp.pad(flat, (0, n - flat.shape[0])).reshape(-1, LANES))
    return jnp.concatenate(out, axis=0)


def _unpack_small(packed, like):
    out, r = [], 0
    for a in like:
        size = int(np.prod(a.shape))
        rows = -(-size // (8 * LANES)) * 8
        out.append(packed[r:r + rows].reshape(-1)[:size].reshape(a.shape))
        r += rows
    return out


def _ffn_forward(x, g, win, wout):
    h, silu, dgate, act = _ffn_up(x, g, win)
    return _mm_res("ffn_down", x, [act], wout, 0.5), (x, h, silu, dgate, act)


def _ffn_backward(dx, saved, g, win, wout, after):
    x, h, silu, dgate, act = saved
    D = x.shape[1]
    wc = win.shape[2]
    dgu = _ffn_bwd_a(dx, wout, silu, dgate, after)
    dwout = _mm_tn("dw_ffn_out", act, wc, dx, D, 2, True, False, 0.5)
    dwin = _mm_tn("dw_ffn_in", h, D, dgu, wc, 4, False, True, 1.0)
    dx_in, dg = _bwd_into_norm("ffn_bwd_b", dgu, wc, win, 4, x, g, dx)
    return dx_in, dg, dwin, dwout.reshape(N_CHIPS, -1, D)


GROUP_FFN1 = ("ffn1_w_in", "ffn1_w_out")
GROUP_REST = ("w_qkv", "w_o", "ffn2_w_in", "ffn2_w_out", "w_ple_gate", "w_ple_proj")


def _gather_start(tag, slotted, after):
    return _split_start("ag_start_" + tag, slotted, _gather_copies([a.shape for a in slotted]), 3 * len(slotted), after)


def _gather_finish(tag, started, after):
    send_s, recv_s, arrays, _ = started
    return _pair_forward(_split_wait("ag_wait_" + tag, send_s, recv_s, arrays,
                                     _gather_copies([a.shape for a in arrays]), after))


def _scatter_start(tag, grads):
    n = len(grads)
    part = [_rs_add_pair(g_, r_) for g_, r_ in zip(grads, _rs_pair_exchange(grads))]
    land = [lax.empty((3,) + p_.shape[1:], p_.dtype) for p_ in part]
    return _split_start("rs_start_" + tag, part + land, _scatter_copies(n), 3 * n, None)


def _scatter_finish(tag, started, after):
    send_s, recv_s, arrays, _ = started
    n = len(arrays) // 2
    arrays = _split_wait("rs_wait_" + tag, send_s, recv_s, arrays, _scatter_copies(n), after)
    return _rs_pair_share([_rs_add_chips(p_, r_) for p_, r_ in zip(arrays[:n], arrays[n:])])


def kernel(x, p, rel_bias, norm_ffn1, ffn1_w_in, ffn1_w_out, norm_mix, w_qkv, q_norm_a, k_norm_a, q_norm_b, k_norm_b, sink_b, w_o, norm_ffn2, ffn2_w_in, ffn2_w_out, norm_ple, w_ple_gate, w_ple_proj, loss_target, m_rel_bias, m_norm_ffn1, m_ffn1_w_in, m_ffn1_w_out, m_norm_mix, m_w_qkv, m_q_norm_a, m_k_norm_a, m_q_norm_b, m_k_norm_b, m_sink_b, m_w_o, m_norm_ffn2, m_ffn2_w_in, m_ffn2_w_out, m_norm_ple, m_w_ple_gate, m_w_ple_proj, v_rel_bias, v_norm_ffn1, v_ffn1_w_in, v_ffn1_w_out, v_norm_mix, v_w_qkv, v_q_norm_a, v_k_norm_a, v_q_norm_b, v_k_norm_b, v_sink_b, v_w_o, v_norm_ffn2, v_ffn2_w_in, v_ffn2_w_out, v_norm_ple, v_w_ple_gate, v_w_ple_proj):
    given = dict(locals())
    T, D = x.shape[1], x.shape[2]
    NL = norm_ffn1.shape[0]
    x0 = x.reshape(T, D)
    tgt = loss_target.reshape(T, D)
    n_a, n_b, n_kv = _qkv_layout(D)

    assert NL == 2
    slot = [{name: _cast_into_slot(given[name], l) for name in BIG} for l in range(NL)]
    ag_a = _gather_start("a", [slot[0][n] for n in GROUP_FFN1], None)
    ag_b = _gather_start("b", [slot[0][n] for n in GROUP_REST], ag_a[3])
    ag_1 = _gather_start("1", [slot[1][n] for n in BIG], ag_b[3])

    def by_rows(a):
        return a.reshape(-1, a.shape[-1])

    def by_cols(a):
        return a.transpose(1, 0, 2).reshape(a.shape[1], -1)

    QW = N_CHIPS * w_qkv.shape[2]

    dils = tuple(d for _, d in DILATED_CONFIGS)
    cfg_a = [(w // (2 * d), d) for w, d in DILATED_CONFIGS]
    pad_a, pad_b = _window_pad(cfg_a[0][0]), _window_pad(SWA_RADIUS)
    bmaps_a, bmaps_b = jnp.asarray(_bucket_maps(cfg_a)), jnp.asarray(_bucket_maps([(SWA_RADIUS, 1)]))
    n_heads = rel_bias.shape[1] // 2
    bias_a = _bias_build(rel_bias, bmaps_a, 0)
    bias_b = _bias_build(rel_bias, bmaps_b, n_heads)
    no_sink = jnp.full((n_heads,), NEG, F32)

    def gains_row(l):
        ones = jnp.ones((n_a * LANES,), F32)
        return jnp.concatenate([
            jnp.tile(q_norm_a[l], 2 * n_a), jnp.tile(k_norm_a[l], 2 * n_a), ones,
            jnp.tile(q_norm_b[l], 2 * n_b), jnp.tile(k_norm_b[l], 2 * n_kv), jnp.ones((n_kv * LANES,), F32)]).reshape(1, QW)

    saved, weights = [], []
    xc = x0
    for l in range(NL):
        s, w = {}, {}
        if l == 0:
            w.update(zip(GROUP_FFN1, _gather_finish("a", ag_a, ag_1[3])))
        else:
            w.update(zip(BIG, _gather_finish("1", ag_1, xc)))
        w["ffn1_w_out"] = by_rows(w["ffn1_w_out"])
        xc, s["ffn1"] = _ffn_forward(xc, norm_ffn1[l:l + 1], w["ffn1_w_in"], w["ffn1_w_out"])
        s["x1"] = xc
        if l == 0:
            w.update(zip(GROUP_REST, _gather_finish("b", ag_b, xc)))
        w["w_qkv"] = by_cols(w["w_qkv"])
        w["w_qkv4"] = w["w_qkv"].reshape(1, D, QW)
        w["w_ple_proj"] = by_cols(w["w_ple_proj"])
        for name in ("w_o", "ffn2_w_out", "w_ple_gate"):
            w[name] = by_rows(w[name])
        s["h2"], raw = _norm_proj("qkv_proj", xc, norm_mix[l:l + 1], w["w_qkv"])
        s["raw"] = raw
        s["qkv_a"], s["qkv_b"] = _qkv_post(raw, gains_row(l))
        s["o_a"], s["lse_a"] = _attn_fwd(s["qkv_a"], bias_a, no_sink, dils, pad_a)
        s["o_b"], s["lse_b"] = _attn_fwd(s["qkv_b"], bias_b, sink_b[l], (1,), pad_b)
        xc = _mm_res("attn_out", xc, [s["o_a"], s["o_b"]], w["w_o"], 1.0)
        xc, s["ffn2"] = _ffn_forward(xc, norm_ffn2[l:l + 1], w["ffn2_w_in"], w["ffn2_w_out"])
        s["x3"] = xc
        s["p"] = p[l].reshape(T, -1)
        s["hn"], xc, s["gate"], s["pp"] = _ple_fwd(xc, norm_ple[l:l + 1], s["p"], w["w_ple_gate"], w["w_ple_proj"])
        saved.append(s)
        weights.append(w)

    dx, loss_blk = _loss_fwd_bwd(xc, tgt)
    loss = lax.psum(loss_blk[0, 0], ("x", "y", "c"))

    gs = {name: [None] * NL for name in SMALL if name != "rel_bias"}
    dt_a, dt_b = [], []
    g_full = {}
    rs_1 = rs_0a = None
    for l in reversed(range(NL)):
        s, w, gw = saved[l], weights[l], {}
        dx, gs["norm_ple"][l], dwg, dwp = _ple_bwd(dx, s["gate"], s["pp"], s["hn"], s["p"], s["x3"],
                                                   norm_ple[l:l + 1], w["w_ple_gate"],
                                                   loss_blk if rs_1 is None else rs_1[3])
        gw["w_ple_gate"] = dwg.reshape(N_CHIPS, -1, D)
        gw["w_ple_proj"] = dwp.reshape(dwp.shape[0], N_CHIPS, -1).transpose(1, 0, 2)
        dx, gs["norm_ffn2"][l], gw["ffn2_w_in"], gw["ffn2_w_out"] = _ffn_backward(
            dx, s["ffn2"], norm_ffn2[l:l + 1], w["ffn2_w_in"], w["ffn2_w_out"], dx)
        do, dd = _mix_bwd_in(dx, w["w_o"], [s["o_a"], s["o_b"]])
        hwa = s["o_a"].shape[1]
        gw["w_o"] = jnp.concatenate([
            _mm_tn("dw_o", o_, o_.shape[1], dx, D, 1, False, False, 1.0).reshape(-1, D // N_CHIPS, D)
            for o_ in (s["o_a"], s["o_b"])], axis=0)
        dqa, dka, dva, dt, _ = _attn_bwd(s["qkv_a"], bias_a, no_sink, dils, pad_a, do, s["lse_a"], dd, 0)
        dt_a.append(dt)
        dqb, dkb, dvb, dt, dsink = _attn_bwd(s["qkv_b"], bias_b, sink_b[l], (1,), pad_b, do, s["lse_b"], dd,
                                             hwa // LANES)
        dt_b.append(dt)
        gs["sink_b"][l] = dsink[:, 0, ::HEAD_DIM].reshape(-1)
        draw, dgains = _qkv_post_bwd(s["raw"], gains_row(l), (dqa, dka, dva), (dqb, dkb, dvb))
        dgv = dgains.reshape(-1, HEAD_DIM)
        gs["q_norm_a"][l] = dgv[:2 * n_a].sum(0)
        gs["k_norm_a"][l] = dgv[2 * n_a:4 * n_a].sum(0)
        gs["q_norm_b"][l] = dgv[6 * n_a:6 * n_a + 2 * n_b].sum(0)
        gs["k_norm_b"][l] = dgv[6 * n_a + 2 * n_b:6 * n_a + 2 * n_b + 2 * n_kv].sum(0)
        dwqkv = _mm_tn("dw_qkv", s["h2"], D, draw, QW, 1, False, False, 1.0)
        gw["w_qkv"] = dwqkv.reshape(D, N_CHIPS, -1).transpose(1, 0, 2)
        dx, gs["norm_mix"][l] = _bwd_into_norm("qkv_bwd_b", draw, QW, w["w_qkv4"], 1, s["x1"], norm_mix[l:l + 1], dx)
        if l > 0:
            dx, gs["norm_ffn1"][l], gw["ffn1_w_in"], gw["ffn1_w_out"] = _ffn_backward(
                dx, s["ffn1"], norm_ffn1[l:l + 1], w["ffn1_w_in"], w["ffn1_w_out"], dx)
            rs_1 = _scatter_start("1", [gw[n] for n in BIG])
        else:
            rs_0a = _scatter_start("0a", [gw[n] for n in GROUP_REST])
            g_full.update(zip([(n, 1) for n in BIG], _scatter_finish("1", rs_1, rs_0a[3])))
            dx, gs["norm_ffn1"][l], gw["ffn1_w_in"], gw["ffn1_w_out"] = _ffn_backward(
                dx, s["ffn1"], norm_ffn1[l:l + 1], w["ffn1_w_in"], w["ffn1_w_out"], rs_0a[3])
            rs_0b = _scatter_start("0b", [gw[n] for n in GROUP_FFN1])
            done_0a = _scatter_finish("0a", rs_0a, rs_0b[3])
            g_full.update(zip([(n, 0) for n in GROUP_REST], done_0a))
            g_full.update(zip([(n, 0) for n in GROUP_FFN1], _scatter_finish("0b", rs_0b, done_0a[0])))
    grad_x = dx.reshape(x.shape)
    d_rel_bias = (_bias_grad(dt_a, bmaps_a, 0) + _bias_grad(dt_b, bmaps_b, n_heads))[:, :rel_bias.shape[1]]

    out = {}
    for name in BIG:
        res = None
        for l in reversed(range(NL)):
            res = _adamw_layer(given[name], g_full[(name, l)], given["m_" + name], given["v_" + name], l, res)
        out[name] = res

    small_g = [d_rel_bias] + [jnp.stack([t.reshape(-1) for t in gs[name]]) for name in SMALL[1:]]
    g_sum = _allreduce_small(_pack_small(small_g))
    res = _ew("adamw_small", _adamw_fn,
              [_pack_small([given[n] for n in SMALL]), g_sum, _pack_small([given["m_" + n] for n in SMALL]),
               _pack_small([given["v_" + n] for n in SMALL])], [(LANES, F32)] * 4)
    like = [given[n] for n in SMALL]
    unpacked = [_unpack_small(r, like) for r in res]
    for i, name in enumerate(SMALL):
        out[name] = [u[i] for u in unpacked]

    return (loss, grad_x, *[out[n][0] for n in WEIGHTS], *[out[n][1] for n in WEIGHTS],
            *[out[n][2] for n in WEIGHTS], *[out[n][3] for n in WEIGHTS])
```

```python
import functools
import math

import numpy as np
import jax
import jax.numpy as jnp
from jax import lax
from jax.experimental import pallas as pl
from jax.experimental.pallas import tpu as pltpu

F32 = jnp.float32
BF16 = jnp.bfloat16
MESH = pl.DeviceIdType.MESH

HEAD_DIM = 64
LANES = 128
QBLOCK = 128
FWD_UNROLL, BWD_UNROLL = 4, 4
N_BUCKETS = 32
MAX_DISTANCE = 1024
DILATED_CONFIGS = ((128, 1), (512, 4), (2048, 16))
SWA_RADIUS = 128
GROUP_B = 4
EPS = 1e-6
NEG = -1e30
Q_SCALE = HEAD_DIM ** -0.5
ADAM_LR, ADAM_B1, ADAM_B2, ADAM_EPS, ADAM_WD, ADAM_STEP = 0.001, 0.9, 0.999, 1e-08, 0.01, 10
VMEM_LIMIT = 56 * 2 ** 20
N_CHIPS = 4
N_DEV = 8

BIG = ("ffn1_w_in", "ffn1_w_out", "w_qkv", "w_o", "ffn2_w_in", "ffn2_w_out", "w_ple_gate", "w_ple_proj")
SMALL = ("rel_bias", "norm_ffn1", "norm_mix", "q_norm_a", "k_norm_a", "q_norm_b", "k_norm_b", "sink_b",
         "norm_ffn2", "norm_ple")
WEIGHTS = ("rel_bias", "norm_ffn1", "ffn1_w_in", "ffn1_w_out", "norm_mix", "w_qkv", "q_norm_a", "k_norm_a",
           "q_norm_b", "k_norm_b", "sink_b", "w_o", "norm_ffn2", "ffn2_w_in", "ffn2_w_out", "norm_ple",
           "w_ple_gate", "w_ple_proj")


HBM_SPEC = pl.BlockSpec(memory_space=pltpu.HBM)
ANY_SPEC = pl.BlockSpec(memory_space=pl.ANY)
SEM_SPEC = pl.BlockSpec(memory_space=pltpu.SEMAPHORE)


def _params(n_grid):
    return pltpu.CompilerParams(dimension_semantics=("arbitrary",) * n_grid, vmem_limit_bytes=VMEM_LIMIT)


def _pick_tm(rows, cap):
    t = (min(cap, rows) // 16) * 16
    while t >= 16:
        if rows % t == 0:
            return t
        t -= 16
    return rows


def _dot(a, b):
    return jnp.dot(a, b, preferred_element_type=F32)


def _dot_nt(a, b):
    return lax.dot_general(a, b, (((1,), (1,)), ((), ())), preferred_element_type=F32)


def _dot_tn(a, b):
    return lax.dot_general(a, b, (((0,), (0,)), ((), ())), preferred_element_type=F32)


def _sigmoid(z):
    return 1.0 / (1.0 + jnp.exp(-z))


def _lo_lanes(shape):
    return lax.broadcasted_iota(jnp.int32, shape, len(shape) - 1) % LANES < HEAD_DIM


def _seg_sum(blk):
    lo = _lo_lanes(blk.shape)
    s_lo = jnp.sum(jnp.where(lo, blk, 0.0), axis=1, keepdims=True)
    s_hi = jnp.sum(jnp.where(lo, 0.0, blk), axis=1, keepdims=True)
    return jnp.where(lo, s_lo, s_hi)


def _rms_bwd_tile(x, g, dh):
    r = lax.rsqrt(jnp.mean(x * x, axis=-1, keepdims=True) + EPS)
    xh = x * r
    dyg = dh * g
    dx = r * (dyg - xh * jnp.mean(dyg * xh, axis=-1, keepdims=True))
    return dx, jnp.sum(dh * xh, axis=0, keepdims=True)


def _ew(name, fn, ins, out_defs, cap=512):
    rows = ins[0].shape[0]
    tm = _pick_tm(rows, cap)
    n_in = len(ins)

    def body(*refs):
        vals = fn(*[r[...] for r in refs[:n_in]])
        if not isinstance(vals, tuple):
            vals = (vals,)
        for r, v in zip(refs[n_in:], vals):
            r[...] = v.astype(r.dtype)

    return pl.pallas_call(
        body, name=name, grid=(rows // tm,),
        in_specs=[pl.BlockSpec((tm, a.shape[1]), lambda i: (i, 0)) for a in ins],
        out_specs=[pl.BlockSpec((tm, c), lambda i: (i, 0)) for c, _ in out_defs],
        out_shape=[jax.ShapeDtypeStruct((rows, c), dt) for c, dt in out_defs],
        compiler_params=_params(1))(*ins)


def _rms_tile(xv, gv):
    r = lax.rsqrt(jnp.mean(xv * xv, axis=-1, keepdims=True) + EPS)
    return (xv * r * gv).astype(BF16)


def _ffn_up(x, g, win):
    T, D = x.shape
    wc = win.shape[2]
    tm = _pick_tm(T, 512)

    def body(x_ref, g_ref, wg_ref, wu_ref, h_ref, silu_ref, dgate_ref, act_ref, wcat):
        @pl.when(pl.program_id(1) == 0)
        def _():
            wcat[:, :wc] = wg_ref[...]
            wcat[:, wc:] = wu_ref[...]

        hv = _rms_tile(x_ref[...], g_ref[...])
        h_ref[...] = hv
        gu = _dot(hv, wcat[...])
        gte, u = gu[:, :wc], gu[:, wc:]
        sg = _sigmoid(gte)
        silu = gte * sg
        silu_ref[...] = silu.astype(BF16)
        dgate_ref[...] = ((sg + silu * (1.0 - sg)) * u).astype(BF16)
        act_ref[...] = (silu * u).astype(BF16)

    out = jax.ShapeDtypeStruct((T, 2 * wc), BF16)
    ospec = pl.BlockSpec((tm, wc), lambda j, i: (i, j))
    nt = T // tm
    h_spec = pl.BlockSpec((tm, D), lambda j, i: (jnp.where(j == 0, i, nt), 0))
    return pl.pallas_call(
        body, name="ffn_up", grid=(2, nt),
        in_specs=[pl.BlockSpec((tm, D), lambda j, i: (i, 0)), pl.BlockSpec((1, D), lambda j, i: (0, 0)),
                  pl.BlockSpec((None, D, wc), lambda j, i: (j, 0, 0)),
                  pl.BlockSpec((None, D, wc), lambda j, i: (j + 2, 0, 0))],
        out_specs=[h_spec] + [ospec] * 3, out_shape=[jax.ShapeDtypeStruct((T + tm, D), BF16)] + [out] * 3,
        scratch_shapes=[pltpu.VMEM((D, 2 * wc), BF16)], compiler_params=_params(2))(x, g, win, win)


def _mm_res(name, res, a_list, w, scale):
    T, N = res.shape
    n = len(a_list)
    widths = [a.shape[1] for a in a_list]
    tm = _pick_tm(T, 512)

    def body(*refs):
        r_ref, a_refs, w_refs, o_ref = refs[0], refs[1:1 + n], refs[1 + n:1 + 2 * n], refs[1 + 2 * n]
        acc = _dot(a_refs[0][...].astype(BF16), w_refs[0][...])
        for a_ref, w_ref in zip(a_refs[1:], w_refs[1:]):
            acc = acc + _dot(a_ref[...].astype(BF16), w_ref[...])
        o_ref[...] = r_ref[...] + scale * acc

    w_specs, off = [], 0
    for k in widths:
        w_specs.append(pl.BlockSpec((k, N), lambda i, blk=off // k: (blk, 0)))
        off += k
    return pl.pallas_call(
        body, name=name, grid=(T // tm,),
        in_specs=[pl.BlockSpec((tm, N), lambda i: (i, 0))]
        + [pl.BlockSpec((tm, k), lambda i: (i, 0)) for k in widths] + w_specs,
        out_specs=pl.BlockSpec((tm, N), lambda i: (i, 0)),
        out_shape=jax.ShapeDtypeStruct((T, N), F32), compiler_params=_params(1))(res, *a_list, *([w] * n))


def _norm_proj(name, x, g, w):
    T, K = x.shape
    N = w.shape[1]
    tm = _pick_tm(T, 512)

    def body(x_ref, g_ref, w_ref, h_ref, o_ref):
        hv = _rms_tile(x_ref[...], g_ref[...])
        h_ref[...] = hv
        o_ref[...] = _dot(hv, w_ref[...])

    row = pl.BlockSpec((tm, K), lambda i: (i, 0))
    return pl.pallas_call(
        body, name=name, grid=(T // tm,),
        in_specs=[row, pl.BlockSpec((1, K), lambda i: (0, 0)), pl.BlockSpec((K, N), lambda i: (0, 0))],
        out_specs=[row, pl.BlockSpec((tm, N), lambda i: (i, 0))],
        out_shape=[jax.ShapeDtypeStruct((T, K), BF16), jax.ShapeDtypeStruct((T, N), F32)],
        compiler_params=_params(1))(x, g, w)


def _mm_tn(name, a, a_w, b, b_w, n_slots, a_by_slot, b_by_slot, scale, tm_cap=512):
    T = b.shape[0]
    tm = _pick_tm(T, tm_cap)
    nt = T // tm

    def body(a_ref, b_ref, o_ref, acc):
        i = pl.program_id(1)

        @pl.when(i == 0)
        def _():
            acc[...] = jnp.zeros_like(acc)

        acc[...] += _dot_tn(a_ref[...].astype(BF16), b_ref[...].astype(BF16))

        @pl.when(i == nt - 1)
        def _():
            o_ref[...] = (acc[...] * scale).astype(BF16)

    return pl.pallas_call(
        body, name=name, grid=(n_slots, nt),
        in_specs=[pl.BlockSpec((tm, a_w), (lambda s, i: (i, s)) if a_by_slot else (lambda s, i: (i, 0))),
                  pl.BlockSpec((tm, b_w), (lambda s, i: (i, s)) if b_by_slot else (lambda s, i: (i, 0)))],
        out_specs=pl.BlockSpec((None, a_w, b_w), lambda s, i: (s, 0, 0)),
        out_shape=jax.ShapeDtypeStruct((n_slots, a_w, b_w), BF16),
        scratch_shapes=[pltpu.VMEM((a_w, b_w), F32)], compiler_params=_params(2))(a, b)


def _mm_tn_pairs(name, a, b, b_w, n_slots):
    T = b.shape[0]
    a_w = a.shape[1]
    tm = _pick_tm(T, 512)
    nt = T // tm

    def body(a_ref, b_ref, o_ref, acc):
        i = pl.program_id(1)

        @pl.when(i == 0)
        def _():
            acc[...] = jnp.zeros_like(acc)

        acc[...] += _dot_tn(a_ref[...], b_ref[...])

        @pl.when(i == nt - 1)
        def _():
            o_ref[0] = acc[:, :b_w].astype(BF16)
            o_ref[1] = acc[:, b_w:].astype(BF16)

    return pl.pallas_call(
        body, name=name, grid=(n_slots // 2, nt),
        in_specs=[pl.BlockSpec((tm, a_w), lambda s, i: (i, 0)), pl.BlockSpec((tm, 2 * b_w), lambda s, i: (i, s))],
        out_specs=pl.BlockSpec((2, a_w, b_w), lambda s, i: (s, 0, 0)),
        out_shape=jax.ShapeDtypeStruct((n_slots, a_w, b_w), BF16),
        scratch_shapes=[pltpu.VMEM((a_w, 2 * b_w), F32)], compiler_params=_params(2))(a, b)


def _ffn_bwd_a(dx, wout, silu, dgate, after):
    T, D = dx.shape
    F = silu.shape[1]
    tm = _pick_tm(T, 256)

    def body(dx_ref, w_ref, s_ref, dg_ref, after_ref, o_ref):
        dact = 0.5 * _dot_nt(dx_ref[...].astype(BF16), w_ref[...])
        o_ref[:, :F] = (dact * dg_ref[...].astype(F32)).astype(BF16)
        o_ref[:, F:] = (dact * s_ref[...].astype(F32)).astype(BF16)

    act_spec = pl.BlockSpec((tm, F), lambda i: (i, 0))
    return pl.pallas_call(
        body, name="ffn_bwd_a", grid=(T // tm,),
        in_specs=[pl.BlockSpec((tm, D), lambda i: (i, 0)), pl.BlockSpec((F, D), lambda i: (0, 0)),
                  act_spec, act_spec, ANY_SPEC],
        out_specs=pl.BlockSpec((tm, 2 * F), lambda i: (i, 0)),
        out_shape=jax.ShapeDtypeStruct((T, 2 * F), BF16),
        compiler_params=_params(1))(dx, wout, silu, dgate, after)


def _bwd_into_norm(name, d, d_w, w, n_slots, group, x, g, dx_in):
    T, D = x.shape
    tm = _pick_tm(T, 256)
    n_mm = n_slots // group
    k_w = group * d_w

    def body(*refs):
        d_refs, w_hbm = refs[:n_mm], refs[n_mm]
        x_ref, g_ref, dxi_ref, dx_ref, dg_ref, wcat, sem = refs[n_mm + 1:]

        @pl.when(pl.program_id(0) == 0)
        def _():
            copies = [pltpu.make_async_copy(w_hbm.at[s] if w.ndim == 3 else w_hbm,
                                            wcat.at[s // group, :, pl.ds((s % group) * d_w, d_w)], sem.at[s])
                      for s in range(n_slots)]
            for cp in copies:
                cp.start()
            for cp in copies:
                cp.wait()
            dg_ref[...] = jnp.zeros_like(dg_ref)

        dh = _dot_nt(d_refs[0][...], wcat[0])
        for m in range(1, n_mm):
            dh = dh + _dot_nt(d_refs[m][...], wcat[m])
        dxn, dg = _rms_bwd_tile(x_ref[...], g_ref[...], dh)
        dx_ref[...] = dxi_ref[...] + dxn
        dg_ref[...] += dg

    row = pl.BlockSpec((tm, D), lambda i: (i, 0))
    vec = pl.BlockSpec((1, D), lambda i: (0, 0))
    return pl.pallas_call(
        body, name=name, grid=(T // tm,),
        in_specs=[pl.BlockSpec((tm, k_w), lambda i, m=m: (i, m)) for m in range(n_mm)] + [ANY_SPEC, row, vec, row],
        out_specs=[row, vec],
        out_shape=[jax.ShapeDtypeStruct((T, D), F32), jax.ShapeDtypeStruct((1, D), F32)],
        scratch_shapes=[pltpu.VMEM((n_mm, D, k_w), BF16), pltpu.SemaphoreType.DMA((n_slots,))],
        compiler_params=_params(1))(*([d] * n_mm + [w, x, g, dx_in]))


def _ple_fwd(x, g, p, wg, wp):
    T, D = x.shape
    P = p.shape[1]
    tm = _pick_tm(T, 256)

    def body(x_ref, g_ref, p_ref, wg_ref, wp_ref, hn_ref, xo_ref, gate_ref, pp_ref):
        xv = x_ref[...]
        hn = _rms_tile(xv, g_ref[...])
        hn_ref[...] = hn
        gate = _sigmoid(_dot(hn, wg_ref[...]))
        pp = _dot(p_ref[...].astype(BF16), wp_ref[...])
        gate_ref[...] = gate
        pp_ref[...] = pp
        xo_ref[...] = xv + gate * pp

    row = pl.BlockSpec((tm, D), lambda i: (i, 0))
    out = jax.ShapeDtypeStruct((T, D), F32)
    return pl.pallas_call(
        body, name="ple_fwd", grid=(T // tm,),
        in_specs=[row, pl.BlockSpec((1, D), lambda i: (0, 0)), pl.BlockSpec((tm, P), lambda i: (i, 0)),
                  pl.BlockSpec((D, D), lambda i: (0, 0)), pl.BlockSpec((P, D), lambda i: (0, 0))],
        out_specs=[row, row, row, row], out_shape=[jax.ShapeDtypeStruct((T, D), BF16), out, out, out],
        compiler_params=_params(1))(x, g, p, wg, wp)


def _ple_bwd(dx, gate, pp, hn, p, x, g, wg, after):
    T, D = x.shape
    P = p.shape[1]
    tm = _pick_tm(T, 256)
    nt = T // tm

    def body(dx_ref, gate_ref, pp_ref, hn_ref, p_ref, x_ref, g_ref, wg_ref, after_ref,
             dxo_ref, dg_ref, dwg_ref, dwp_ref, acc_g, acc_p):
        i = pl.program_id(0)

        @pl.when(i == 0)
        def _():
            acc_g[...] = jnp.zeros_like(acc_g)
            acc_p[...] = jnp.zeros_like(acc_p)
            dg_ref[...] = jnp.zeros_like(dg_ref)

        dxv = dx_ref[...]
        gate = gate_ref[...]
        dz = (dxv * pp_ref[...] * gate * (1.0 - gate)).astype(BF16)
        dpp = (dxv * gate).astype(BF16)
        acc_g[...] += _dot_tn(hn_ref[...], dz)
        acc_p[...] += _dot_tn(p_ref[...].astype(BF16), dpp)
        dxn, dg = _rms_bwd_tile(x_ref[...], g_ref[...], _dot_nt(dz, wg_ref[...]))
        dxo_ref[...] = dxv + dxn
        dg_ref[...] += dg

        @pl.when(i == nt - 1)
        def _():
            dwg_ref[...] = acc_g[...].astype(BF16)
            dwp_ref[...] = acc_p[...].astype(BF16)

    row = pl.BlockSpec((tm, D), lambda i: (i, 0))
    vec = pl.BlockSpec((1, D), lambda i: (0, 0))
    return pl.pallas_call(
        body, name="ple_bwd", grid=(nt,),
        in_specs=[row, row, row, row, pl.BlockSpec((tm, P), lambda i: (i, 0)), row, vec,
                  pl.BlockSpec((D, D), lambda i: (0, 0)), ANY_SPEC],
        out_specs=[row, vec, pl.BlockSpec((D, D), lambda i: (0, 0)), pl.BlockSpec((P, D), lambda i: (0, 0))],
        out_shape=[jax.ShapeDtypeStruct((T, D), F32), jax.ShapeDtypeStruct((1, D), F32),
                   jax.ShapeDtypeStruct((D, D), BF16), jax.ShapeDtypeStruct((P, D), BF16)],
        scratch_shapes=[pltpu.VMEM((D, D), F32), pltpu.VMEM((P, D), F32)],
        compiler_params=_params(1))(dx, gate, pp, hn, p, x, g, wg, after)


def _loss_fwd_bwd(y, tgt):
    T, D = y.shape
    tm = _pick_tm(T, 512)

    def body(y_ref, t_ref, dy_ref, loss_ref):
        e = y_ref[...] - t_ref[...]
        dy_ref[...] = e / D

        @pl.when(pl.program_id(0) == 0)
        def _():
            loss_ref[...] = jnp.zeros_like(loss_ref)

        loss_ref[...] += 0.5 * jnp.sum(jnp.mean(e * e, axis=-1, keepdims=True), axis=0, keepdims=True)

    row = pl.BlockSpec((tm, D), lambda i: (i, 0))
    return pl.pallas_call(
        body, name="loss", grid=(T // tm,), in_specs=[row, row],
        out_specs=[row, pl.BlockSpec((8, LANES), lambda i: (0, 0))],
        out_shape=[jax.ShapeDtypeStruct((T, D), F32), jax.ShapeDtypeStruct((8, LANES), F32)],
        compiler_params=_params(1))(y, tgt)


def _qkv_layout(D):
    n_a = D // (2 * LANES)
    n_b = D // (2 * LANES)
    n_kv = max(1, (2 * n_b) // GROUP_B) * HEAD_DIM // LANES
    return n_a, n_b, n_kv


def _dup_half(xv, half):
    rolled = pltpu.roll(xv, HEAD_DIM, 1)
    lo = _lo_lanes(xv.shape)
    return jnp.where(lo, xv, rolled) if half == 0 else jnp.where(lo, rolled, xv)


def _qkv_post(raw, gains):
    T, W = raw.shape
    n_a, n_b, n_kv = _qkv_layout(W * 4 // 9)
    tm = _pick_tm(T, 256)
    o_qb = 3 * n_a

    def norm(xv, gv, scale):
        ms = _seg_sum(xv * xv) * (1.0 / HEAD_DIM)
        return xv * lax.rsqrt(ms + EPS) * gv * scale

    def body(raw_ref, g_ref, a_ref, b_ref):
        def blk(cb):
            return raw_ref[:, cb * LANES:(cb + 1) * LANES]

        def gn(cb):
            return g_ref[:, cb * LANES:(cb + 1) * LANES]

        for cb in range(n_a):
            a_ref[:, cb * LANES:(cb + 1) * LANES] = norm(blk(cb), gn(cb), Q_SCALE)
            cbk = n_a + cb
            a_ref[:, cbk * LANES:(cbk + 1) * LANES] = norm(blk(cbk), gn(cbk), 1.0)
            cbv = 2 * n_a + cb
            a_ref[:, cbv * LANES:(cbv + 1) * LANES] = blk(cbv)
        for cb in range(n_b):
            src = o_qb + cb
            b_ref[:, cb * LANES:(cb + 1) * LANES] = norm(blk(src), gn(src), Q_SCALE)
        for e in range(n_b):
            kvh = (2 * e) // GROUP_B
            ck = o_qb + n_b + kvh // 2
            cv = ck + n_kv
            kn = norm(blk(ck), gn(ck), 1.0)
            b_ref[:, (n_b + e) * LANES:(n_b + e + 1) * LANES] = _dup_half(kn, kvh % 2)
            b_ref[:, (2 * n_b + e) * LANES:(2 * n_b + e + 1) * LANES] = _dup_half(blk(cv), kvh % 2)

    wa, wb = 3 * n_a * LANES, 3 * n_b * LANES
    return pl.pallas_call(
        body, name="qkv_post", grid=(T // tm,),
        in_specs=[pl.BlockSpec((tm, W), lambda i: (i, 0)), pl.BlockSpec((1, W), lambda i: (0, 0))],
        out_specs=[pl.BlockSpec((tm, wa), lambda i: (i, 0)), pl.BlockSpec((tm, wb), lambda i: (i, 0))],
        out_shape=[jax.ShapeDtypeStruct((T, wa), F32), jax.ShapeDtypeStruct((T, wb), F32)],
        compiler_params=_params(1))(raw, gains)


def _qkv_post_bwd(raw, gains, d_a, d_b):
    T, W = raw.shape
    n_a, n_b, n_kv = _qkv_layout(W * 4 // 9)
    tm = _pick_tm(T, 256)
    o_qb = 3 * n_a

    def body(raw_ref, g_ref, daq, dak, dav, dbq, dbk, dbv, o_ref, dg_ref):
        @pl.when(pl.program_id(0) == 0)
        def _():
            dg_ref[...] = jnp.zeros_like(dg_ref)

        def cols(ref, cb):
            return ref[:, cb * LANES:(cb + 1) * LANES]

        def norm_bwd(cb, dy, scale):
            xv = cols(raw_ref, cb)
            gv = cols(g_ref, cb)
            r = lax.rsqrt(_seg_sum(xv * xv) * (1.0 / HEAD_DIM) + EPS)
            xh = xv * r
            dys = dy * scale
            dyg = dys * gv
            dxv = r * (dyg - xh * (_seg_sum(dyg * xh) * (1.0 / HEAD_DIM)))
            o_ref[:, cb * LANES:(cb + 1) * LANES] = dxv.astype(BF16)
            dg_ref[:, cb * LANES:(cb + 1) * LANES] += jnp.sum(dys * xh, axis=0, keepdims=True)

        def fold(ref, kv_blk):
            halves = []
            for half in range(2):
                kvh = 2 * kv_blk + half
                blocks = [e for e in range(n_b) if (2 * e) // GROUP_B == kvh]
                s = cols(ref, blocks[0])
                for e in blocks[1:]:
                    s = s + cols(ref, e)
                halves.append(s + pltpu.roll(s, HEAD_DIM, 1))
            return jnp.where(_lo_lanes(halves[0].shape), halves[0], halves[1])

        for cb in range(n_a):
            norm_bwd(cb, cols(daq, cb), Q_SCALE)
            norm_bwd(n_a + cb, cols(dak, cb), 1.0)
            cbv = 2 * n_a + cb
            o_ref[:, cbv * LANES:(cbv + 1) * LANES] = cols(dav, cb).astype(BF16)
        for cb in range(n_b):
            norm_bwd(o_qb + cb, cols(dbq, cb), Q_SCALE)
        for kb in range(n_kv):
            ck = o_qb + n_b + kb
            cv = ck + n_kv
            norm_bwd(ck, fold(dbk, kb), 1.0)
            o_ref[:, cv * LANES:(cv + 1) * LANES] = fold(dbv, kb).astype(BF16)

    hw_a, hw_b = n_a * LANES, n_b * LANES
    return pl.pallas_call(
        body, name="qkv_post_bwd", grid=(T // tm,),
        in_specs=[pl.BlockSpec((tm, W), lambda i: (i, 0)), pl.BlockSpec((1, W), lambda i: (0, 0))]
        + [pl.BlockSpec((tm, hw_a), lambda i: (i, 0))] * 3 + [pl.BlockSpec((tm, hw_b), lambda i: (i, 0))] * 3,
        out_specs=[pl.BlockSpec((tm, W), lambda i: (i, 0)), pl.BlockSpec((1, W), lambda i: (0, 0))],
        out_shape=[jax.ShapeDtypeStruct((T, W), BF16), jax.ShapeDtypeStruct((1, W), F32)],
        compiler_params=_params(1))(raw, gains, *d_a, *d_b)


def _t5_bucket_np(rel):
    half = N_BUCKETS // 2
    max_exact = half // 2
    ret = np.where(rel > 0, half, 0)
    n = np.abs(rel)
    nf = np.maximum(n, 1).astype(np.float32)
    large = max_exact + (np.log(nf / np.float32(max_exact)) / np.float32(math.log(MAX_DISTANCE / max_exact))
                         * np.float32(half - max_exact)).astype(np.int32)
    large = np.minimum(large, half - 1)
    return ret + np.where(n < max_exact, n, large)


def _window_pad(radius):
    assert radius <= QBLOCK
    return HEAD_DIM if radius <= HEAD_DIM else QBLOCK


def _bucket_maps(configs):
    pad = _window_pad(configs[0][0])
    q = np.arange(QBLOCK)[:, None]
    kk = np.arange(QBLOCK + 2 * pad)[None, :]
    rel = kk - pad - q
    maps = [np.where(np.abs(rel) <= radius, _t5_bucket_np(rel * dil), -1) for radius, dil in configs]
    return np.stack(maps).astype(np.int32)


def _bias_build(rel_bias, bmaps, col0):
    n_sets, _, W = bmaps.shape
    n_heads = rel_bias.shape[1] // 2

    def body(rb_ref, bm_ref, o_ref):
        h = pl.program_id(1)
        bm = bm_ref[...]

        def step(n, acc):
            return jnp.where(bm == n, rb_ref[n, col0 + h], acc)

        o_ref[...] = lax.fori_loop(0, N_BUCKETS, step, jnp.where(bm < 0, NEG, 0.0).astype(F32))

    return pl.pallas_call(
        body, name="bias_build", grid=(n_sets, n_heads),
        in_specs=[pl.BlockSpec(memory_space=pltpu.SMEM), pl.BlockSpec((None, QBLOCK, W), lambda s, h: (s, 0, 0))],
        out_specs=pl.BlockSpec((None, None, QBLOCK, W), lambda s, h: (s, h, 0, 0)),
        out_shape=jax.ShapeDtypeStruct((n_sets, n_heads, QBLOCK, W), F32),
        compiler_params=_params(2))(rel_bias, bmaps)


def _bias_grad(dtiles, bmaps, col0):
    n_sets, _, W = bmaps.shape
    n_heads = dtiles[0].shape[1]
    n_l = len(dtiles)

    def body(*refs):
        bm_ref, o_ref = refs[0], refs[1 + n_l]
        s, h = pl.program_id(0), pl.program_id(1)

        @pl.when((s == 0) & (h == 0))
        def _():
            o_ref[...] = jnp.zeros_like(o_ref)

        d = refs[1][...]
        for r in refs[2:1 + n_l]:
            d = d + r[...]
        acc8 = d[0:8, :]
        for a in range(1, QBLOCK // 8):
            acc8 = acc8 + pltpu.roll(d[8 * a:8 * a + 8, :], W - 8 * a, 1)
        per_offset = acc8[0:1, :]
        for b in range(1, 8):
            per_offset = per_offset + pltpu.roll(acc8[b:b + 1, :], W - b, 1)
        bucket = lax.broadcasted_iota(jnp.int32, (N_BUCKETS, W), 0)
        hit = bucket == bm_ref[0:1, :]
        per_bucket = jnp.sum(jnp.where(hit, per_offset, 0.0), axis=1, keepdims=True)
        lanes = lax.broadcasted_iota(jnp.int32, o_ref.shape, 1)
        o_ref[...] += jnp.where(lanes == col0 + h, per_bucket, 0.0)

    tile = pl.BlockSpec((None, None, QBLOCK, W), lambda s, h: (s, h, 0, 0))
    return pl.pallas_call(
        body, name="bias_grad", grid=(n_sets, n_heads),
        in_specs=[pl.BlockSpec((None, QBLOCK, W), lambda s, h: (s, 0, 0))] + [tile] * n_l,
        out_specs=pl.BlockSpec((N_BUCKETS, LANES), lambda s, h: (0, 0)),
        out_shape=jax.ShapeDtypeStruct((N_BUCKETS, LANES), F32), compiler_params=_params(2))(bmaps, *dtiles)


def _rows(l_start, n, d, r):
    if d == 1:
        return pl.ds(pl.multiple_of(l_start, 8), n)
    return pl.ds(l_start * d + r, n, stride=d)


def _stack_heads(xv, lo):
    z = jnp.zeros_like(xv)
    return jnp.concatenate([jnp.where(lo, xv, z), jnp.where(lo, z, xv)], axis=0)


def _unstack_heads(xv, lo):
    return jnp.where(lo, xv[:QBLOCK], xv[QBLOCK:])


def _per_head_rows(v0, v1):
    if jnp.ndim(v0) == 0:
        return jnp.where(lax.broadcasted_iota(jnp.int32, (2 * QBLOCK, 1), 0) < QBLOCK, v0, v1)
    return jnp.concatenate([v0, v1], axis=0)


def _block_geometry(b, nb_sub, pad):
    r, lb = b // nb_sub, b % nb_sub
    l0 = lb * QBLOCK
    lp = jnp.maximum(l0 - pad, 0)
    ln = jnp.minimum(l0 + QBLOCK, nb_sub * QBLOCK - pad)
    return r, l0, lp, ln, (lb == 0).astype(jnp.int32) + 2 * (lb == nb_sub - 1).astype(jnp.int32)


def _edge_variants(bias, pad):
    n_br, _, _, W = bias.shape
    col = np.arange(W)
    left, right = col < pad, col >= pad + QBLOCK
    masked = jnp.asarray(np.stack([np.zeros(W, bool), left, right, left | right]))
    return jnp.where(masked[None, :, None, :], NEG, bias.reshape(n_br, 1, -1, W))


def _window(ref, l0, lp, ln, pad, d, r):
    return jnp.concatenate([ref[_rows(lp, pad, d, r), :], ref[_rows(l0, QBLOCK, d, r), :],
                            ref[_rows(ln, pad, d, r), :]], axis=0)


def _attn_fwd(qkv, bias, sink, dils, pad):
    T = qkv.shape[0]
    hw = qkv.shape[1] // 3
    ng = hw // LANES
    n_br = len(dils)
    n_blocks = T // QBLOCK
    W = QBLOCK + 2 * pad
    chunk = 256

    def body(sink_ref, q_ref, k_ref, v_ref, bias_ref, o_ref, lse_ref, *scratch):
        g = pl.program_id(0)
        lo = _lo_lanes((QBLOCK, LANES))
        snk = _per_head_rows(sink_ref[2 * g], sink_ref[2 * g + 1])
        for c, d in enumerate(dils):
            nb_sub = n_blocks // d
            o_dst = scratch[0].at[c] if n_br > 1 else o_ref
            l_dst = scratch[1].at[c] if n_br > 1 else lse_ref

            def block(b, carry, c=c, d=d, nb_sub=nb_sub, o_dst=o_dst, l_dst=l_dst):
                r, l0, lp, ln, edge = _block_geometry(b, nb_sub, pad)
                q = _stack_heads(q_ref[_rows(l0, QBLOCK, d, r), :].astype(BF16), lo)
                k = _window(k_ref, l0, lp, ln, pad, d, r).astype(BF16)
                v = _window(v_ref, l0, lp, ln, pad, d, r).astype(BF16)
                s = _dot_nt(q, k) + bias_ref[c, edge]
                m = jnp.maximum(jnp.max(s, axis=1, keepdims=True), snk)
                p = jnp.exp(s - m)
                den = jnp.sum(p, axis=1, keepdims=True) + jnp.exp(snk - m)
                o_dst[_rows(l0, QBLOCK, d, r), :] = _unstack_heads(_dot(p.astype(BF16), v) / den, lo)
                l_dst[_rows(l0, QBLOCK, d, r), :] = _unstack_heads(
                    jnp.broadcast_to(m + jnp.log(den), (2 * QBLOCK, LANES)), lo)
                return carry

            lax.fori_loop(0, n_blocks, block, 0, unroll=FWD_UNROLL)

        if n_br > 1:
            def merge(i, carry):
                rs = pl.ds(pl.multiple_of(i * chunk, chunk), chunk)
                ls = [scratch[1][c, rs, :] for c in range(n_br)]
                m = ls[0]
                for t in ls[1:]:
                    m = jnp.maximum(m, t)
                ws = [jnp.exp(t - m) for t in ls]
                z = ws[0]
                acc = ws[0] * scratch[0][0, rs, :]
                for c in range(1, n_br):
                    z = z + ws[c]
                    acc = acc + ws[c] * scratch[0][c, rs, :]
                o_ref[rs, :] = acc / z
                lse_ref[rs, :] = m + jnp.log(z)
                return carry

            lax.fori_loop(0, T // chunk, merge, 0)

    def col(base):
        return pl.BlockSpec((T, LANES), lambda g: (0, base + g))

    out = jax.ShapeDtypeStruct((T, hw), F32)
    scratch = [pltpu.VMEM((n_br, T, LANES), F32)] * 2 if n_br > 1 else []
    return pl.pallas_call(
        body, name="attn_fwd", grid=(ng,),
        in_specs=[pl.BlockSpec(memory_space=pltpu.SMEM), col(0), col(ng), col(2 * ng),
                  pl.BlockSpec((n_br, 4, 2 * QBLOCK, W), lambda g: (0, 0, g, 0))],
        out_specs=[col(0), col(0)], out_shape=[out, out], scratch_shapes=scratch,
        compiler_params=_params(1))(sink, qkv, qkv, qkv, bias)


def _attn_bwd(qkv, bias, sink, dils, pad, do, lse, dd, col_base):
    T = qkv.shape[0]
    hw = qkv.shape[1] // 3
    ng = hw // LANES
    n_br = len(dils)
    n_blocks = T // QBLOCK
    W = QBLOCK + 2 * pad

    def body(sink_ref, q_ref, k_ref, v_ref, bias_ref, do_ref, lse_ref, dd_ref,
             dq_ref, dk_ref, dv_ref, dt_ref, ds_ref):
        g = pl.program_id(0)
        dq_ref[...] = jnp.zeros_like(dq_ref)
        dk_ref[...] = jnp.zeros_like(dk_ref)
        dv_ref[...] = jnp.zeros_like(dv_ref)
        dt_ref[...] = jnp.zeros_like(dt_ref)
        ds_ref[...] = jnp.zeros_like(ds_ref)
        lo = _lo_lanes((QBLOCK, LANES))
        snk = jnp.where(lo, sink_ref[2 * g], sink_ref[2 * g + 1])
        for c, d in enumerate(dils):
            nb_sub = n_blocks // d

            def block(b, carry, c=c, d=d, nb_sub=nb_sub):
                r, l0, lp, ln, edge = _block_geometry(b, nb_sub, pad)
                rows_q = _rows(l0, QBLOCK, d, r)
                q = _stack_heads(q_ref[rows_q, :].astype(BF16), lo)
                k = _window(k_ref, l0, lp, ln, pad, d, r).astype(BF16)
                v = _window(v_ref, l0, lp, ln, pad, d, r).astype(BF16)
                dob = _stack_heads(do_ref[rows_q, :].astype(BF16), lo)
                lse_b = lse_ref[rows_q, :]
                dd_b = dd_ref[rows_q, :]
                s = _dot_nt(q, k) + bias_ref[c, edge]
                p = jnp.exp(s - _per_head_rows(lse_b[:, 0:1], lse_b[:, HEAD_DIM:HEAD_DIM + 1]))
                ds = p * (_dot_nt(dob, v) - _per_head_rows(dd_b[:, 0:1], dd_b[:, HEAD_DIM:HEAD_DIM + 1]))
                dsb = ds.astype(BF16)
                dkw = _dot_tn(dsb, q)
                dvw = _dot_tn(p.astype(BF16), dob)
                dt_ref[c] += ds
                dq_ref[rows_q, :] += _unstack_heads(_dot(dsb, k), lo)
                ds_ref[0:1, :] += jnp.sum(-jnp.exp(snk - lse_b) * dd_b, axis=0, keepdims=True)
                for part, (start, n) in zip((0, pad, pad + QBLOCK), ((lp, pad), (l0, QBLOCK), (ln, pad))):
                    dk_ref[_rows(start, n, d, r), :] += dkw[part:part + n]
                    dv_ref[_rows(start, n, d, r), :] += dvw[part:part + n]
                return carry

            lax.fori_loop(0, n_blocks, block, 0, unroll=BWD_UNROLL)

    def col(base):
        return pl.BlockSpec((T, LANES), lambda g: (0, base + g))

    tile = pl.BlockSpec((n_br, 2 * QBLOCK, W), lambda g: (0, g, 0))
    full = jax.ShapeDtypeStruct((T, hw), F32)
    dq, dk, dv, dt, dsink = pl.pallas_call(
        body, name="attn_bwd", grid=(ng,),
        in_specs=[pl.BlockSpec(memory_space=pltpu.SMEM), col(0), col(ng), col(2 * ng),
                  pl.BlockSpec((n_br, 4, 2 * QBLOCK, W), lambda g: (0, 0, g, 0)),
                  col(col_base), col(0), col(col_base)],
        out_specs=[col(0), col(0), col(0), tile, pl.BlockSpec((None, 8, LANES), lambda g: (g, 0, 0))],
        out_shape=[full, full, full, jax.ShapeDtypeStruct((n_br, 2 * ng * QBLOCK, W), F32),
                   jax.ShapeDtypeStruct((ng, 8, LANES), F32)],
        compiler_params=_params(1))(sink, qkv, qkv, qkv, bias, do, lse, dd)
    return dq, dk, dv, dt.reshape(n_br, 2 * ng, QBLOCK, W), dsink


def _mix_bwd_in(dx, wo, o_list):
    T, D = dx.shape
    widths = [o.shape[1] for o in o_list]
    hw = sum(widths)
    n = len(o_list)
    tm = _pick_tm(T, 256)

    def body(*refs):
        dx_ref, w_ref, o_refs, do_ref, dd_ref = refs[0], refs[1], refs[2:2 + n], refs[2 + n], refs[3 + n]
        dov = _dot_nt(dx_ref[...].astype(BF16), w_ref[...])
        do_ref[...] = dov
        off = 0
        for o_ref, k in zip(o_refs, widths):
            prod = dov[:, off:off + k] * o_ref[...]
            for cb in range(k // LANES):
                dd_ref[:, off + cb * LANES:off + (cb + 1) * LANES] = _seg_sum(prod[:, cb * LANES:(cb + 1) * LANES])
            off += k

    row = pl.BlockSpec((tm, hw), lambda i: (i, 0))
    out = jax.ShapeDtypeStruct((T, hw), F32)
    return pl.pallas_call(
        body, name="mix_bwd_in", grid=(T // tm,),
        in_specs=[pl.BlockSpec((tm, D), lambda i: (i, 0)), pl.BlockSpec((hw, D), lambda i: (0, 0))]
        + [pl.BlockSpec((tm, k), lambda i: (i, 0)) for k in widths],
        out_specs=[row, row], out_shape=[out, out], compiler_params=_params(1))(dx, wo, *o_list)


def _mesh_pos():
    return lax.axis_index("x"), lax.axis_index("y"), lax.axis_index("c")


def _my_chip():
    return 2 * lax.axis_index("x") + lax.axis_index("y")


def _other_chips(x, y):
    return [(1 - x, y), (x, 1 - y), (1 - x, 1 - y)]


def _half_rows(rows, cc):
    hr = rows // 2
    return pl.ds(pl.multiple_of(cc * hr, 16), hr)


def _cast_into_slot(w, l):
    _, R, C = w.shape
    tm = _pick_tm(R, 512)

    def body(w_ref, o_ref):
        o_ref[...] = w_ref[...].astype(BF16)

    return pl.pallas_call(
        body, name="cast_into_slot", grid=(R // tm,),
        in_specs=[pl.BlockSpec((None, tm, C), lambda i: (l, i, 0))],
        out_specs=pl.BlockSpec((None, tm, C), lambda i: (_my_chip(), i, 0)),
        out_shape=jax.ShapeDtypeStruct((N_CHIPS, R, C), BF16), compiler_params=_params(1))(w)


def _split_start(name, arrays, make_copies, n_sem, after):
    n = len(arrays)
    n_in = n + (0 if after is None else 1)

    def body(*refs):
        send_s, recv_s = refs[n_in], refs[n_in + 1]
        token = refs[n_in + 2 + n]
        for send, _ in make_copies(refs[:n], send_s, recv_s):
            send.start()
        token[...] = jnp.zeros_like(token)

    res = pl.pallas_call(
        body, name=name,
        out_shape=(pltpu.SemaphoreType.DMA((n_sem,)), pltpu.SemaphoreType.DMA((n_sem,)),
                   *[pltpu.HBM(a.shape, a.dtype) for a in arrays], jax.ShapeDtypeStruct((8, LANES), F32)),
        in_specs=[HBM_SPEC] * n + [ANY_SPEC] * (n_in - n),
        out_specs=(SEM_SPEC, SEM_SPEC, *([HBM_SPEC] * n), pl.BlockSpec(memory_space=pltpu.VMEM)),
        input_output_aliases={i: 2 + i for i in range(n)},
        compiler_params=pltpu.CompilerParams(has_side_effects=pltpu.SideEffectType.DATAFLOW_SIDE_EFFECTING),
    )(*[pltpu.with_memory_space_constraint(a, pltpu.HBM) for a in arrays], *([] if after is None else [after]))
    return res[0], res[1], list(res[2:2 + n]), res[2 + n]


def _split_wait(name, send_s, recv_s, arrays, make_copies, after):
    n = len(arrays)

    def body(*refs):
        for send, landed in make_copies(refs[:n], refs[n], refs[n + 1]):
            send.wait_send()
            landed.wait_recv()

    return list(pl.pallas_call(
        body, name=name, out_shape=[pltpu.HBM(a.shape, a.dtype) for a in arrays],
        in_specs=[HBM_SPEC] * n + [SEM_SPEC, SEM_SPEC, ANY_SPEC], out_specs=[HBM_SPEC] * n,
        input_output_aliases={i: i for i in range(n)},
        compiler_params=pltpu.CompilerParams(has_side_effects=pltpu.SideEffectType.DATAFLOW_SIDE_EFFECTING),
    )(*arrays, send_s, recv_s, after))


def _gather_copies(shapes):
    n = len(shapes)

    def make(refs, send_s, recv_s):
        x, y, c = _mesh_pos()
        my = 2 * x + y
        copies = []
        for w in range(n):
            for k, (px, py) in enumerate(_other_chips(x, y)):
                def part(slot, w=w):
                    return refs[w].at[slot, _half_rows(shapes[w][1], c), :]
                sems = dict(send_sem=send_s.at[k * n + w], recv_sem=recv_s.at[k * n + w],
                            device_id=(px, py, c), device_id_type=MESH)
                copies.append((pltpu.make_async_remote_copy(src_ref=part(my), dst_ref=part(my), **sems),
                               pltpu.make_async_remote_copy(src_ref=part(2 * px + py), dst_ref=part(2 * px + py), **sems)))
        return copies

    return make


def _pair_forward(bufs):
    n = len(bufs)

    def body(*refs):
        arr = refs[n:2 * n]
        send_s, recv_s = refs[2 * n:]
        x, y, c = _mesh_pos()
        sent, landed = [], []
        for w in range(n):
            for k, (px, py) in enumerate(_other_chips(x, y)):
                def part(cc, w=w, slot=2 * px + py):
                    return arr[w].at[slot, _half_rows(bufs[w].shape[1], cc), :]
                sems = dict(send_sem=send_s.at[k * n + w], recv_sem=recv_s.at[k * n + w],
                            device_id=(x, y, 1 - c), device_id_type=MESH)
                sent.append(pltpu.make_async_remote_copy(src_ref=part(c), dst_ref=part(c), **sems))
                landed.append(pltpu.make_async_remote_copy(src_ref=part(1 - c), dst_ref=part(1 - c), **sems))
        for cp in sent:
            cp.start()
        for cp in landed:
            cp.wait_recv()
        for cp in sent:
            cp.wait_send()

    return list(pl.pallas_call(
        body, name="ag_pair_forward", in_specs=[HBM_SPEC] * n, out_specs=[HBM_SPEC] * n,
        out_shape=[jax.ShapeDtypeStruct(b.shape, b.dtype) for b in bufs],
        input_output_aliases={w: w for w in range(n)},
        scratch_shapes=[pltpu.SemaphoreType.DMA((3 * n,)), pltpu.SemaphoreType.DMA((3 * n,))],
    )(*bufs))


def _rs_pair_exchange(grads):
    n = len(grads)

    def body(*refs):
        ins, recv = refs[:n], refs[n:2 * n]
        send_s, recv_s = refs[2 * n:]
        x, y, c = _mesh_pos()
        remote = [pltpu.make_async_remote_copy(
            src_ref=ins[t].at[:, _half_rows(grads[t].shape[1], 1 - c), :], dst_ref=recv[t],
            send_sem=send_s.at[t], recv_sem=recv_s.at[t], device_id=(x, y, 1 - c), device_id_type=MESH)
            for t in range(n)]
        for cp in remote:
            cp.start()
        for cp in remote:
            cp.wait_recv()
        for cp in remote:
            cp.wait_send()

    return pl.pallas_call(
        body, name="rs_pair_exchange", in_specs=[HBM_SPEC] * n, out_specs=[HBM_SPEC] * n,
        out_shape=[jax.ShapeDtypeStruct((g.shape[0], g.shape[1] // 2, g.shape[2]), g.dtype) for g in grads],
        scratch_shapes=[pltpu.SemaphoreType.DMA((n,)), pltpu.SemaphoreType.DMA((n,))],
    )(*grads)


def _rs_add_pair(grad, recv):
    n_slot, hr, C = recv.shape
    tm = _pick_tm(hr, 192)
    nb = hr // tm

    def body(a_ref, b_ref, o_ref):
        o_ref[...] = (a_ref[...].astype(F32) + b_ref[...].astype(F32)).astype(BF16)

    blk = pl.BlockSpec((n_slot, tm, C), lambda i: (0, i, 0))
    return pl.pallas_call(
        body, name="rs_add_pair", grid=(nb,),
        in_specs=[pl.BlockSpec((n_slot, tm, C), lambda i: (0, lax.axis_index("c") * nb + i, 0)), blk],
        out_specs=blk, out_shape=jax.ShapeDtypeStruct(recv.shape, BF16), compiler_params=_params(1))(grad, recv)


def _scatter_copies(n):
    def make(refs, send_s, recv_s):
        x, y, c = _mesh_pos()
        copies = []
        for t in range(n):
            for k, (px, py) in enumerate(_other_chips(x, y)):
                sems = dict(send_sem=send_s.at[3 * t + k], recv_sem=recv_s.at[3 * t + k],
                            device_id=(px, py, c), device_id_type=MESH)
                land = refs[n + t].at[k]
                copies.append((pltpu.make_async_remote_copy(src_ref=refs[t].at[2 * px + py], dst_ref=land, **sems),
                               pltpu.make_async_remote_copy(src_ref=land, dst_ref=land, **sems)))
        return copies

    return make


def _rs_add_chips(part, recv):
    _, hr, C = part.shape
    tm = _pick_tm(hr, 256)
    nb = hr // tm

    def body(a_ref, r0, r1, r2, o_ref):
        o_ref[...] = ((a_ref[...].astype(F32) + r0[...].astype(F32)) + r1[...].astype(F32)) + r2[...].astype(F32)

    def rel(k):
        return pl.BlockSpec((None, tm, C), lambda i: (k, i, 0))

    return pl.pallas_call(
        body, name="rs_add_chips", grid=(nb,),
        in_specs=[pl.BlockSpec((None, tm, C), lambda i: (_my_chip(), i, 0)), rel(0), rel(1), rel(2)],
        out_specs=pl.BlockSpec((tm, C), lambda i: (lax.axis_index("c") * nb + i, 0)),
        out_shape=jax.ShapeDtypeStruct((2 * hr, C), F32), compiler_params=_params(1))(part, recv, recv, recv)


def _rs_pair_share(halves):
    n = len(halves)

    def body(*refs):
        bufs = refs[n:2 * n]
        send_s, recv_s = refs[2 * n:]
        x, y, c = _mesh_pos()

        def half(t, cc):
            return bufs[t].at[_half_rows(halves[t].shape[0], cc), :]

        sent = [pltpu.make_async_remote_copy(
            src_ref=half(t, c), dst_ref=half(t, c), send_sem=send_s.at[t], recv_sem=recv_s.at[t],
            device_id=(x, y, 1 - c), device_id_type=MESH) for t in range(n)]
        landed = [pltpu.make_async_remote_copy(
            src_ref=half(t, 1 - c), dst_ref=half(t, 1 - c), send_sem=send_s.at[t], recv_sem=recv_s.at[t],
            device_id=(x, y, 1 - c), device_id_type=MESH) for t in range(n)]
        for cp in sent:
            cp.start()
        for cp in landed:
            cp.wait_recv()
        for cp in sent:
            cp.wait_send()

    return pl.pallas_call(
        body, name="rs_pair_share", in_specs=[HBM_SPEC] * n, out_specs=[HBM_SPEC] * n,
        out_shape=[jax.ShapeDtypeStruct(h.shape, h.dtype) for h in halves],
        input_output_aliases={t: t for t in range(n)},
        scratch_shapes=[pltpu.SemaphoreType.DMA((n,)), pltpu.SemaphoreType.DMA((n,))],
    )(*halves)


def _allreduce_small(v):
    rows = v.shape[0]

    def body(v_ref, o_ref, buf, send_s, recv_s):
        x, y, c = _mesh_pos()
        me = 4 * x + 2 * y + c
        buf[me] = v_ref[...]
        copies = []
        for r in range(1, N_DEV):
            px = 1 - x if r & 4 else x
            py = 1 - y if r & 2 else y
            pc = 1 - c if r & 1 else c
            send = pltpu.make_async_remote_copy(
                src_ref=v_ref, dst_ref=buf.at[me], send_sem=send_s.at[r - 1], recv_sem=recv_s.at[r - 1],
                device_id=(px, py, pc), device_id_type=MESH)
            peer_slot = buf.at[4 * px + 2 * py + pc]
            landed = pltpu.make_async_remote_copy(
                src_ref=peer_slot, dst_ref=peer_slot, send_sem=send_s.at[r - 1], recv_sem=recv_s.at[r - 1],
                device_id=(px, py, pc), device_id_type=MESH)
            copies.append((send, landed))
        for send, _ in copies:
            send.start()
        for _, landed in copies:
            landed.wait_recv()
        for send, _ in copies:
            send.wait_send()
        acc = buf[0]
        for j in range(1, N_DEV):
            acc = acc + buf[j]
        o_ref[...] = acc

    vm = pl.BlockSpec(memory_space=pltpu.VMEM)
    return pl.pallas_call(
        body, name="allreduce_small", in_specs=[vm], out_specs=vm,
        out_shape=jax.ShapeDtypeStruct((rows, LANES), F32),
        scratch_shapes=[pltpu.VMEM((N_DEV, rows, LANES), F32), pltpu.SemaphoreType.DMA((N_DEV - 1,)),
                        pltpu.SemaphoreType.DMA((N_DEV - 1,))],
    )(v)


def _adamw_fn(w, g, m, v):
    m2 = ADAM_B1 * m + (1.0 - ADAM_B1) * g
    v2 = ADAM_B2 * v + (1.0 - ADAM_B2) * (g * g)
    m_hat = m2 / (1.0 - ADAM_B1 ** ADAM_STEP)
    v_hat = v2 / (1.0 - ADAM_B2 ** ADAM_STEP)
    delta = -ADAM_LR * (m_hat / (jnp.sqrt(v_hat) + ADAM_EPS) + ADAM_WD * w)
    return g, delta, m2, v2


def _adamw_layer(w, g, m, v, l, prev):
    NL, R, C = w.shape
    tm = _pick_tm(R, 128)
    n_prev = 0 if prev is None else 4

    def body(w_ref, g_ref, m_ref, v_ref, *rest):
        outs = rest[n_prev:]
        for o_ref, val in zip(outs, _adamw_fn(w_ref[...], g_ref[...], m_ref[...], v_ref[...])):
            o_ref[...] = val

    lay = pl.BlockSpec((None, tm, C), lambda i: (l, i, 0))
    shape = jax.ShapeDtypeStruct((NL, R, C), F32)
    return pl.pallas_call(
        body, name="adamw", grid=(R // tm,),
        in_specs=[lay, pl.BlockSpec((tm, C), lambda i: (i, 0)), lay, lay] + [pl.BlockSpec(memory_space=pl.ANY)] * n_prev,
        out_specs=[lay] * 4, out_shape=[shape] * 4,
        input_output_aliases={4 + j: j for j in range(n_prev)},
        compiler_params=_params(1))(w, g, m, v, *(prev or []))


def _pack_small(parts):
    out = []
    for a in parts:
        flat = a.reshape(-1)
        n = -(-flat.shape[0] // (8 * LANES)) * 8 * LANES
        out.append(jnp.pad(flat, (0, n - flat.shape[0])).reshape(-1, LANES))
    return jnp.concatenate(out, axis=0)


def _unpack_small(packed, like):
    out, r = [], 0
    for a in like:
        size = int(np.prod(a.shape))
        rows = -(-size // (8 * LANES)) * 8
        out.append(packed[r:r + rows].reshape(-1)[:size].reshape(a.shape))
        r += rows
    return out


def _ffn_forward(x, g, win, wout):
    h, silu, dgate, act = _ffn_up(x, g, win)
    return _mm_res("ffn_down", x, [act], wout, 0.5), (x, h, silu, dgate, act)


def _ffn_backward(dx, saved, g, win, wout, after):
    x, h, silu, dgate, act = saved
    D = x.shape[1]
    wc = win.shape[2]
    dgu = _ffn_bwd_a(dx, wout, silu, dgate, after)
    dwout = _mm_tn("dw_ffn_out", act, wc, dx, D, 2, True, False, 0.5)
    dwin = _mm_tn_pairs("dw_ffn_in", h, dgu, wc, 4)
    dx_in, dg = _bwd_into_norm("ffn_bwd_b", dgu, wc, win, 4, 2, x, g, dx)
    return dx_in, dg, dwin, dwout.reshape(N_CHIPS, -1, D)


GROUP_FFN1 = ("ffn1_w_in", "ffn1_w_out")
GROUP_REST = ("w_qkv", "w_o", "ffn2_w_in", "ffn2_w_out", "w_ple_gate", "w_ple_proj")


def _gather_start(tag, slotted, after):
    return _split_start("ag_start_" + tag, slotted, _gather_copies([a.shape for a in slotted]), 3 * len(slotted), after)


def _gather_finish(tag, started, after):
    send_s, recv_s, arrays, _ = started
    return _pair_forward(_split_wait("ag_wait_" + tag, send_s, recv_s, arrays,
                                     _gather_copies([a.shape for a in arrays]), after))


def _scatter_start(tag, grads):
    n = len(grads)
    part = [_rs_add_pair(g_, r_) for g_, r_ in zip(grads, _rs_pair_exchange(grads))]
    land = [lax.empty((3,) + p_.shape[1:], p_.dtype) for p_ in part]
    return _split_start("rs_start_" + tag, part + land, _scatter_copies(n), 3 * n, None)


def _scatter_finish(tag, started, after):
    send_s, recv_s, arrays, _ = started
    n = len(arrays) // 2
    arrays = _split_wait("rs_wait_" + tag, send_s, recv_s, arrays, _scatter_copies(n), after)
    return _rs_pair_share([_rs_add_chips(p_, r_) for p_, r_ in zip(arrays[:n], arrays[n:])])


def kernel(x, p, rel_bias, norm_ffn1, ffn1_w_in, ffn1_w_out, norm_mix, w_qkv, q_norm_a, k_norm_a, q_norm_b, k_norm_b, sink_b, w_o, norm_ffn2, ffn2_w_in, ffn2_w_out, norm_ple, w_ple_gate, w_ple_proj, loss_target, m_rel_bias, m_norm_ffn1, m_ffn1_w_in, m_ffn1_w_out, m_norm_mix, m_w_qkv, m_q_norm_a, m_k_norm_a, m_q_norm_b, m_k_norm_b, m_sink_b, m_w_o, m_norm_ffn2, m_ffn2_w_in, m_ffn2_w_out, m_norm_ple, m_w_ple_gate, m_w_ple_proj, v_rel_bias, v_norm_ffn1, v_ffn1_w_in, v_ffn1_w_out, v_norm_mix, v_w_qkv, v_q_norm_a, v_k_norm_a, v_q_norm_b, v_k_norm_b, v_sink_b, v_w_o, v_norm_ffn2, v_ffn2_w_in, v_ffn2_w_out, v_norm_ple, v_w_ple_gate, v_w_ple_proj):
    given = dict(locals())
    T, D = x.shape[1], x.shape[2]
    NL = norm_ffn1.shape[0]
    x0 = x.reshape(T, D)
    tgt = loss_target.reshape(T, D)
    n_a, n_b, n_kv = _qkv_layout(D)

    assert NL == 2
    slot = [{name: _cast_into_slot(given[name], l) for name in BIG} for l in range(NL)]
    ag_a = _gather_start("a", [slot[0][n] for n in GROUP_FFN1], None)
    ag_b = _gather_start("b", [slot[0][n] for n in GROUP_REST], ag_a[3])
    ag_1 = _gather_start("1", [slot[1][n] for n in BIG], ag_b[3])

    def by_rows(a):
        return a.reshape(-1, a.shape[-1])

    def by_cols(a):
        return a.transpose(1, 0, 2).reshape(a.shape[1], -1)

    QW = N_CHIPS * w_qkv.shape[2]

    dils = tuple(d for _, d in DILATED_CONFIGS)
    cfg_a = [(w // (2 * d), d) for w, d in DILATED_CONFIGS]
    pad_a, pad_b = _window_pad(cfg_a[0][0]), _window_pad(SWA_RADIUS)
    bmaps_a, bmaps_b = jnp.asarray(_bucket_maps(cfg_a)), jnp.asarray(_bucket_maps([(SWA_RADIUS, 1)]))
    n_heads = rel_bias.shape[1] // 2
    bias_a = _edge_variants(_bias_build(rel_bias, bmaps_a, 0), pad_a)
    bias_b = _edge_variants(_bias_build(rel_bias, bmaps_b, n_heads), pad_b)
    no_sink = jnp.full((n_heads,), NEG, F32)

    def gains_row(l):
        ones = jnp.ones((n_a * LANES,), F32)
        return jnp.concatenate([
            jnp.tile(q_norm_a[l], 2 * n_a), jnp.tile(k_norm_a[l], 2 * n_a), ones,
            jnp.tile(q_norm_b[l], 2 * n_b), jnp.tile(k_norm_b[l], 2 * n_kv), jnp.ones((n_kv * LANES,), F32)]).reshape(1, QW)

    saved, weights = [], []
    xc = x0
    for l in range(NL):
        s, w = {}, {}
        if l == 0:
            w.update(zip(GROUP_FFN1, _gather_finish("a", ag_a, ag_1[3])))
        else:
            w.update(zip(BIG, _gather_finish("1", ag_1, xc)))
        w["ffn1_w_out"] = by_rows(w["ffn1_w_out"])
        xc, s["ffn1"] = _ffn_forward(xc, norm_ffn1[l:l + 1], w["ffn1_w_in"], w["ffn1_w_out"])
        s["x1"] = xc
        if l == 0:
            w.update(zip(GROUP_REST, _gather_finish("b", ag_b, xc)))
        w["w_qkv"] = by_cols(w["w_qkv"])
        w["w_ple_proj"] = by_cols(w["w_ple_proj"])
        for name in ("w_o", "ffn2_w_out", "w_ple_gate"):
            w[name] = by_rows(w[name])
        s["h2"], raw = _norm_proj("qkv_proj", xc, norm_mix[l:l + 1], w["w_qkv"])
        s["raw"] = raw
        s["qkv_a"], s["qkv_b"] = _qkv_post(raw, gains_row(l))
        s["o_a"], s["lse_a"] = _attn_fwd(s["qkv_a"], bias_a, no_sink, dils, pad_a)
        s["o_b"], s["lse_b"] = _attn_fwd(s["qkv_b"], bias_b, sink_b[l], (1,), pad_b)
        xc = _mm_res("attn_out", xc, [s["o_a"], s["o_b"]], w["w_o"], 1.0)
        xc, s["ffn2"] = _ffn_forward(xc, norm_ffn2[l:l + 1], w["ffn2_w_in"], w["ffn2_w_out"])
        s["x3"] = xc
        s["p"] = p[l].reshape(T, -1)
        s["hn"], xc, s["gate"], s["pp"] = _ple_fwd(xc, norm_ple[l:l + 1], s["p"], w["w_ple_gate"], w["w_ple_proj"])
        saved.append(s)
        weights.append(w)

    dx, loss_blk = _loss_fwd_bwd(xc, tgt)
    loss = lax.psum(loss_blk[0, 0], ("x", "y", "c"))

    gs = {name: [None] * NL for name in SMALL if name != "rel_bias"}
    dt_a, dt_b = [], []
    g_full = {}
    rs_1 = rs_0a = None
    for l in reversed(range(NL)):
        s, w, gw = saved[l], weights[l], {}
        dx, gs["norm_ple"][l], dwg, dwp = _ple_bwd(dx, s["gate"], s["pp"], s["hn"], s["p"], s["x3"],
                                                   norm_ple[l:l + 1], w["w_ple_gate"],
                                                   loss_blk if rs_1 is None else rs_1[3])
        gw["w_ple_gate"] = dwg.reshape(N_CHIPS, -1, D)
        gw["w_ple_proj"] = dwp.reshape(dwp.shape[0], N_CHIPS, -1).transpose(1, 0, 2)
        dx, gs["norm_ffn2"][l], gw["ffn2_w_in"], gw["ffn2_w_out"] = _ffn_backward(
            dx, s["ffn2"], norm_ffn2[l:l + 1], w["ffn2_w_in"], w["ffn2_w_out"], dx)
        do, dd = _mix_bwd_in(dx, w["w_o"], [s["o_a"], s["o_b"]])
        hwa = s["o_a"].shape[1]
        gw["w_o"] = jnp.concatenate([
            _mm_tn("dw_o", o_, o_.shape[1], dx, D, 1, False, False, 1.0).reshape(-1, D // N_CHIPS, D)
            for o_ in (s["o_a"], s["o_b"])], axis=0)
        dqa, dka, dva, dt, _ = _attn_bwd(s["qkv_a"], bias_a, no_sink, dils, pad_a, do, s["lse_a"], dd, 0)
        dt_a.append(dt)
        dqb, dkb, dvb, dt, dsink = _attn_bwd(s["qkv_b"], bias_b, sink_b[l], (1,), pad_b, do, s["lse_b"], dd,
                                             hwa // LANES)
        dt_b.append(dt)
        gs["sink_b"][l] = dsink[:, 0, ::HEAD_DIM].reshape(-1)
        draw, dgains = _qkv_post_bwd(s["raw"], gains_row(l), (dqa, dka, dva), (dqb, dkb, dvb))
        dgv = dgains.reshape(-1, HEAD_DIM)
        gs["q_norm_a"][l] = dgv[:2 * n_a].sum(0)
        gs["k_norm_a"][l] = dgv[2 * n_a:4 * n_a].sum(0)
        gs["q_norm_b"][l] = dgv[6 * n_a:6 * n_a + 2 * n_b].sum(0)
        gs["k_norm_b"][l] = dgv[6 * n_a + 2 * n_b:6 * n_a + 2 * n_b + 2 * n_kv].sum(0)
        dwqkv = _mm_tn("dw_qkv", s["h2"], D, draw, QW, 1, False, False, 1.0)
        gw["w_qkv"] = dwqkv.reshape(D, N_CHIPS, -1).transpose(1, 0, 2)
        dx, gs["norm_mix"][l] = _bwd_into_norm("qkv_bwd_b", draw, QW, w["w_qkv"], 1, 1, s["x1"], norm_mix[l:l + 1], dx)
        if l > 0:
            dx, gs["norm_ffn1"][l], gw["ffn1_w_in"], gw["ffn1_w_out"] = _ffn_backward(
                dx, s["ffn1"], norm_ffn1[l:l + 1], w["ffn1_w_in"], w["ffn1_w_out"], dx)
            rs_1 = _scatter_start("1", [gw[n] for n in BIG])
        else:
            rs_0a = _scatter_start("0a", [gw[n] for n in GROUP_REST])
            g_full.update(zip([(n, 1) for n in BIG], _scatter_finish("1", rs_1, rs_0a[3])))
            dx, gs["norm_ffn1"][l], gw["ffn1_w_in"], gw["ffn1_w_out"] = _ffn_backward(
                dx, s["ffn1"], norm_ffn1[l:l + 1], w["ffn1_w_in"], w["ffn1_w_out"], rs_0a[3])
            rs_0b = _scatter_start("0b", [gw[n] for n in GROUP_FFN1])
            done_0a = _scatter_finish("0a", rs_0a, rs_0b[3])
            g_full.update(zip([(n, 0) for n in GROUP_REST], done_0a))
            g_full.update(zip([(n, 0) for n in GROUP_FFN1], _scatter_finish("0b", rs_0b, done_0a[0])))
    grad_x = dx.reshape(x.shape)
    d_rel_bias = (_bias_grad(dt_a, bmaps_a, 0) + _bias_grad(dt_b, bmaps_b, n_heads))[:, :rel_bias.shape[1]]

    out = {}
    for name in BIG:
        res = None
        for l in reversed(range(NL)):
            res = _adamw_layer(given[name], g_full[(name, l)], given["m_" + name], given["v_" + name], l, res)
        out[name] = res

    small_g = [d_rel_bias] + [jnp.stack([t.reshape(-1) for t in gs[name]]) for name in SMALL[1:]]
    g_sum = _allreduce_small(_pack_small(small_g))
    res = _ew("adamw_small", _adamw_fn,
              [_pack_small([given[n] for n in SMALL]), g_sum, _pack_small([given["m_" + n] for n in SMALL]),
               _pack_small([given["v_" + n] for n in SMALL])], [(LANES, F32)] * 4)
    like = [given[n] for n in SMALL]
    unpacked = [_unpack_small(r, like) for r in res]
    for i, name in enumerate(SMALL):
        out[name] = [u[i] for u in unpacked]

    return (loss, grad_x, *[out[n][0] for n in WEIGHTS], *[out[n][1] for n in WEIGHTS],
            *[out[n][2] for n in WEIGHTS], *[out[n][3] for n in WEIGHTS])
```

```python
import functools
import math

import numpy as np
import jax
import jax.numpy as jnp
from jax import lax
from jax.experimental import pallas as pl
from jax.experimental.pallas import tpu as pltpu

F32 = jnp.float32
BF16 = jnp.bfloat16
MESH = pl.DeviceIdType.MESH

HEAD_DIM = 64
LANES = 128
QBLOCK = 128
FWD_UNROLL, BWD_UNROLL = 4, 4
N_BUCKETS = 32
MAX_DISTANCE = 1024
DILATED_CONFIGS = ((128, 1), (512, 4), (2048, 16))
SWA_RADIUS = 128
GROUP_B = 4
EPS = 1e-6
NEG = -1e30
Q_SCALE = HEAD_DIM ** -0.5
ADAM_LR, ADAM_B1, ADAM_B2, ADAM_EPS, ADAM_WD, ADAM_STEP = 0.001, 0.9, 0.999, 1e-08, 0.01, 10
VMEM_LIMIT = 56 * 2 ** 20
N_CHIPS = 4
N_DEV = 8

BIG = ("ffn1_w_in", "ffn1_w_out", "w_qkv", "w_o", "ffn2_w_in", "ffn2_w_out", "w_ple_gate", "w_ple_proj")
SMALL = ("rel_bias", "norm_ffn1", "norm_mix", "q_norm_a", "k_norm_a", "q_norm_b", "k_norm_b", "sink_b",
         "norm_ffn2", "norm_ple")
WEIGHTS = ("rel_bias", "norm_ffn1", "ffn1_w_in", "ffn1_w_out", "norm_mix", "w_qkv", "q_norm_a", "k_norm_a",
           "q_norm_b", "k_norm_b", "sink_b", "w_o", "norm_ffn2", "ffn2_w_in", "ffn2_w_out", "norm_ple",
           "w_ple_gate", "w_ple_proj")


HBM_SPEC = pl.BlockSpec(memory_space=pltpu.HBM)
ANY_SPEC = pl.BlockSpec(memory_space=pl.ANY)
SEM_SPEC = pl.BlockSpec(memory_space=pltpu.SEMAPHORE)


def _params(n_grid):
    return pltpu.CompilerParams(dimension_semantics=("arbitrary",) * n_grid, vmem_limit_bytes=VMEM_LIMIT)


def _pick_tm(rows, cap):
    t = (min(cap, rows) // 16) * 16
    while t >= 16:
        if rows % t == 0:
            return t
        t -= 16
    return rows


def _dot(a, b):
    return jnp.dot(a, b, preferred_element_type=F32)


def _dot_nt(a, b):
    return lax.dot_general(a, b, (((1,), (1,)), ((), ())), preferred_element_type=F32)


def _dot_tn(a, b):
    return lax.dot_general(a, b, (((0,), (0,)), ((), ())), preferred_element_type=F32)


def _sigmoid(z):
    return 1.0 / (1.0 + jnp.exp(-z))


def _lo_lanes(shape):
    return lax.broadcasted_iota(jnp.int32, shape, len(shape) - 1) % LANES < HEAD_DIM


def _seg_sum(blk):
    lo = _lo_lanes(blk.shape)
    s_lo = jnp.sum(jnp.where(lo, blk, 0.0), axis=1, keepdims=True)
    s_hi = jnp.sum(jnp.where(lo, 0.0, blk), axis=1, keepdims=True)
    return jnp.where(lo, s_lo, s_hi)


def _rms_bwd_tile(x, g, dh):
    r = lax.rsqrt(jnp.mean(x * x, axis=-1, keepdims=True) + EPS)
    xh = x * r
    dyg = dh * g
    dx = r * (dyg - xh * jnp.mean(dyg * xh, axis=-1, keepdims=True))
    return dx, jnp.sum(dh * xh, axis=0, keepdims=True)


def _ew(name, fn, ins, out_defs, cap=512):
    rows = ins[0].shape[0]
    tm = _pick_tm(rows, cap)
    n_in = len(ins)

    def body(*refs):
        vals = fn(*[r[...] for r in refs[:n_in]])
        if not isinstance(vals, tuple):
            vals = (vals,)
        for r, v in zip(refs[n_in:], vals):
            r[...] = v.astype(r.dtype)

    return pl.pallas_call(
        body, name=name, grid=(rows // tm,),
        in_specs=[pl.BlockSpec((tm, a.shape[1]), lambda i: (i, 0)) for a in ins],
        out_specs=[pl.BlockSpec((tm, c), lambda i: (i, 0)) for c, _ in out_defs],
        out_shape=[jax.ShapeDtypeStruct((rows, c), dt) for c, dt in out_defs],
        compiler_params=_params(1))(*ins)


def _rms_tile(xv, gv):
    r = lax.rsqrt(jnp.mean(xv * xv, axis=-1, keepdims=True) + EPS)
    return (xv * r * gv).astype(BF16)


def _ffn_up(x, g, win):
    T, D = x.shape
    wc = win.shape[2]
    tm = _pick_tm(T, 512)

    def body(x_ref, g_ref, wg_ref, wu_ref, h_ref, silu_ref, dgate_ref, act_ref, wcat):
        @pl.when(pl.program_id(1) == 0)
        def _():
            wcat[:, :wc] = wg_ref[...]
            wcat[:, wc:] = wu_ref[...]

        hv = _rms_tile(x_ref[...], g_ref[...])
        h_ref[...] = hv
        gu = _dot(hv, wcat[...])
        gte, u = gu[:, :wc], gu[:, wc:]
        sg = _sigmoid(gte)
        silu = gte * sg
        silu_ref[...] = silu.astype(BF16)
        dgate_ref[...] = ((sg + silu * (1.0 - sg)) * u).astype(BF16)
        act_ref[...] = (silu * u).astype(BF16)

    out = jax.ShapeDtypeStruct((T, 2 * wc), BF16)
    ospec = pl.BlockSpec((tm, wc), lambda j, i: (i, j))
    nt = T // tm
    h_spec = pl.BlockSpec((tm, D), lambda j, i: (jnp.where(j == 0, i, nt), 0))
    return pl.pallas_call(
        body, name="ffn_up", grid=(2, nt),
        in_specs=[pl.BlockSpec((tm, D), lambda j, i: (i, 0)), pl.BlockSpec((1, D), lambda j, i: (0, 0)),
                  pl.BlockSpec((None, D, wc), lambda j, i: (j, 0, 0)),
                  pl.BlockSpec((None, D, wc), lambda j, i: (j + 2, 0, 0))],
        out_specs=[h_spec] + [ospec] * 3, out_shape=[jax.ShapeDtypeStruct((T + tm, D), BF16)] + [out] * 3,
        scratch_shapes=[pltpu.VMEM((D, 2 * wc), BF16)], compiler_params=_params(2))(x, g, win, win)


def _mm_res(name, res, a_list, w, scale):
    T, N = res.shape
    n = len(a_list)
    widths = [a.shape[1] for a in a_list]
    tm = _pick_tm(T, 512)

    def body(*refs):
        r_ref, a_refs, w_refs, o_ref = refs[0], refs[1:1 + n], refs[1 + n:1 + 2 * n], refs[1 + 2 * n]
        acc = _dot(a_refs[0][...].astype(BF16), w_refs[0][...])
        for a_ref, w_ref in zip(a_refs[1:], w_refs[1:]):
            acc = acc + _dot(a_ref[...].astype(BF16), w_ref[...])
        o_ref[...] = r_ref[...] + scale * acc

    w_specs, off = [], 0
    for k in widths:
        w_specs.append(pl.BlockSpec((k, N), lambda i, blk=off // k: (blk, 0)))
        off += k
    return pl.pallas_call(
        body, name=name, grid=(T // tm,),
        in_specs=[pl.BlockSpec((tm, N), lambda i: (i, 0))]
        + [pl.BlockSpec((tm, k), lambda i: (i, 0)) for k in widths] + w_specs,
        out_specs=pl.BlockSpec((tm, N), lambda i: (i, 0)),
        out_shape=jax.ShapeDtypeStruct((T, N), F32), compiler_params=_params(1))(res, *a_list, *([w] * n))


def _norm_proj(name, x, g, w):
    T, K = x.shape
    N = w.shape[1]
    tm = _pick_tm(T, 512)

    def body(x_ref, g_ref, w_ref, h_ref, o_ref):
        hv = _rms_tile(x_ref[...], g_ref[...])
        h_ref[...] = hv
        o_ref[...] = _dot(hv, w_ref[...])

    row = pl.BlockSpec((tm, K), lambda i: (i, 0))
    return pl.pallas_call(
        body, name=name, grid=(T // tm,),
        in_specs=[row, pl.BlockSpec((1, K), lambda i: (0, 0)), pl.BlockSpec((K, N), lambda i: (0, 0))],
        out_specs=[row, pl.BlockSpec((tm, N), lambda i: (i, 0))],
        out_shape=[jax.ShapeDtypeStruct((T, K), BF16), jax.ShapeDtypeStruct((T, N), F32)],
        compiler_params=_params(1))(x, g, w)


def _mm_tn(name, a, a_w, b, b_w, n_slots, a_by_slot, b_by_slot, scale, tm_cap=512):
    T = b.shape[0]
    tm = _pick_tm(T, tm_cap)
    nt = T // tm

    def body(a_ref, b_ref, o_ref, acc):
        i = pl.program_id(1)

        @pl.when(i == 0)
        def _():
            acc[...] = jnp.zeros_like(acc)

        acc[...] += _dot_tn(a_ref[...].astype(BF16), b_ref[...].astype(BF16))

        @pl.when(i == nt - 1)
        def _():
            o_ref[...] = (acc[...] * scale).astype(BF16)

    return pl.pallas_call(
        body, name=name, grid=(n_slots, nt),
        in_specs=[pl.BlockSpec((tm, a_w), (lambda s, i: (i, s)) if a_by_slot else (lambda s, i: (i, 0))),
                  pl.BlockSpec((tm, b_w), (lambda s, i: (i, s)) if b_by_slot else (lambda s, i: (i, 0)))],
        out_specs=pl.BlockSpec((None, a_w, b_w), lambda s, i: (s, 0, 0)),
        out_shape=jax.ShapeDtypeStruct((n_slots, a_w, b_w), BF16),
        scratch_shapes=[pltpu.VMEM((a_w, b_w), F32)], compiler_params=_params(2))(a, b)


def _mm_tn_pairs(name, a, b, b_w, n_slots):
    T = b.shape[0]
    a_w = a.shape[1]
    tm = _pick_tm(T, 512)
    nt = T // tm

    def body(a_ref, b_ref, o_ref, acc):
        i = pl.program_id(1)

        @pl.when(i == 0)
        def _():
            acc[...] = jnp.zeros_like(acc)

        acc[...] += _dot_tn(a_ref[...], b_ref[...])

        @pl.when(i == nt - 1)
        def _():
            o_ref[0] = acc[:, :b_w].astype(BF16)
            o_ref[1] = acc[:, b_w:].astype(BF16)

    return pl.pallas_call(
        body, name=name, grid=(n_slots // 2, nt),
        in_specs=[pl.BlockSpec((tm, a_w), lambda s, i: (i, 0)), pl.BlockSpec((tm, 2 * b_w), lambda s, i: (i, s))],
        out_specs=pl.BlockSpec((2, a_w, b_w), lambda s, i: (s, 0, 0)),
        out_shape=jax.ShapeDtypeStruct((n_slots, a_w, b_w), BF16),
        scratch_shapes=[pltpu.VMEM((a_w, 2 * b_w), F32)], compiler_params=_params(2))(a, b)


def _ffn_bwd_a(dx, wout, silu, dgate, after):
    T, D = dx.shape
    F = silu.shape[1]
    tm = _pick_tm(T, 256)

    def body(dx_ref, w_ref, s_ref, dg_ref, after_ref, o_ref):
        dact = 0.5 * _dot_nt(dx_ref[...].astype(BF16), w_ref[...])
        o_ref[:, :F] = (dact * dg_ref[...].astype(F32)).astype(BF16)
        o_ref[:, F:] = (dact * s_ref[...].astype(F32)).astype(BF16)

    act_spec = pl.BlockSpec((tm, F), lambda i: (i, 0))
    return pl.pallas_call(
        body, name="ffn_bwd_a", grid=(T // tm,),
        in_specs=[pl.BlockSpec((tm, D), lambda i: (i, 0)), pl.BlockSpec((F, D), lambda i: (0, 0)),
                  act_spec, act_spec, ANY_SPEC],
        out_specs=pl.BlockSpec((tm, 2 * F), lambda i: (i, 0)),
        out_shape=jax.ShapeDtypeStruct((T, 2 * F), BF16),
        compiler_params=_params(1))(dx, wout, silu, dgate, after)


def _bwd_into_norm(name, d, d_w, w, n_slots, group, x, g, dx_in):
    T, D = x.shape
    tm = _pick_tm(T, 256)
    n_mm = n_slots // group
    k_w = group * d_w

    def body(*refs):
        d_refs, w_hbm = refs[:n_mm], refs[n_mm]
        x_ref, g_ref, dxi_ref, dx_ref, dg_ref, wcat, sem = refs[n_mm + 1:]

        @pl.when(pl.program_id(0) == 0)
        def _():
            copies = [pltpu.make_async_copy(w_hbm.at[s] if w.ndim == 3 else w_hbm,
                                            wcat.at[s // group, :, pl.ds((s % group) * d_w, d_w)], sem.at[s])
                      for s in range(n_slots)]
            for cp in copies:
                cp.start()
            for cp in copies:
                cp.wait()
            dg_ref[...] = jnp.zeros_like(dg_ref)

        dh = _dot_nt(d_refs[0][...], wcat[0])
        for m in range(1, n_mm):
            dh = dh + _dot_nt(d_refs[m][...], wcat[m])
        dxn, dg = _rms_bwd_tile(x_ref[...], g_ref[...], dh)
        dx_ref[...] = dxi_ref[...] + dxn
        dg_ref[...] += dg

    row = pl.BlockSpec((tm, D), lambda i: (i, 0))
    vec = pl.BlockSpec((1, D), lambda i: (0, 0))
    return pl.pallas_call(
        body, name=name, grid=(T // tm,),
        in_specs=[pl.BlockSpec((tm, k_w), lambda i, m=m: (i, m)) for m in range(n_mm)] + [ANY_SPEC, row, vec, row],
        out_specs=[row, vec],
        out_shape=[jax.ShapeDtypeStruct((T, D), F32), jax.ShapeDtypeStruct((1, D), F32)],
        scratch_shapes=[pltpu.VMEM((n_mm, D, k_w), BF16), pltpu.SemaphoreType.DMA((n_slots,))],
        compiler_params=_params(1))(*([d] * n_mm + [w, x, g, dx_in]))


def _ple_fwd(x, g, p, wg, wp):
    T, D = x.shape
    P = p.shape[1]
    tm = _pick_tm(T, 256)

    def body(x_ref, g_ref, p_ref, wg_ref, wp_ref, hn_ref, xo_ref, gate_ref, pp_ref):
        xv = x_ref[...]
        hn = _rms_tile(xv, g_ref[...])
        hn_ref[...] = hn
        gate = _sigmoid(_dot(hn, wg_ref[...]))
        pp = _dot(p_ref[...].astype(BF16), wp_ref[...])
        gate_ref[...] = gate
        pp_ref[...] = pp
        xo_ref[...] = xv + gate * pp

    row = pl.BlockSpec((tm, D), lambda i: (i, 0))
    out = jax.ShapeDtypeStruct((T, D), F32)
    return pl.pallas_call(
        body, name="ple_fwd", grid=(T // tm,),
        in_specs=[row, pl.BlockSpec((1, D), lambda i: (0, 0)), pl.BlockSpec((tm, P), lambda i: (i, 0)),
                  pl.BlockSpec((D, D), lambda i: (0, 0)), pl.BlockSpec((P, D), lambda i: (0, 0))],
        out_specs=[row, row, row, row], out_shape=[jax.ShapeDtypeStruct((T, D), BF16), out, out, out],
        compiler_params=_params(1))(x, g, p, wg, wp)


def _ple_bwd(dx, gate, pp, hn, p, x, g, wg, after):
    T, D = x.shape
    P = p.shape[1]
    tm = _pick_tm(T, 256)
    nt = T // tm

    def body(dx_ref, gate_ref, pp_ref, hn_ref, p_ref, x_ref, g_ref, wg_ref, after_ref,
             dxo_ref, dg_ref, dwg_ref, dwp_ref, acc_g, acc_p):
        i = pl.program_id(0)

        @pl.when(i == 0)
        def _():
            acc_g[...] = jnp.zeros_like(acc_g)
            acc_p[...] = jnp.zeros_like(acc_p)
            dg_ref[...] = jnp.zeros_like(dg_ref)

        dxv = dx_ref[...]
        gate = gate_ref[...]
        dz = (dxv * pp_ref[...] * gate * (1.0 - gate)).astype(BF16)
        dpp = (dxv * gate).astype(BF16)
        acc_g[...] += _dot_tn(hn_ref[...], dz)
        acc_p[...] += _dot_tn(p_ref[...].astype(BF16), dpp)
        dxn, dg = _rms_bwd_tile(x_ref[...], g_ref[...], _dot_nt(dz, wg_ref[...]))
        dxo_ref[...] = dxv + dxn
        dg_ref[...] += dg

        @pl.when(i == nt - 1)
        def _():
            dwg_ref[...] = acc_g[...].astype(BF16)
            dwp_ref[...] = acc_p[...].astype(BF16)

    row = pl.BlockSpec((tm, D), lambda i: (i, 0))
    vec = pl.BlockSpec((1, D), lambda i: (0, 0))
    return pl.pallas_call(
        body, name="ple_bwd", grid=(nt,),
        in_specs=[row, row, row, row, pl.BlockSpec((tm, P), lambda i: (i, 0)), row, vec,
                  pl.BlockSpec((D, D), lambda i: (0, 0)), ANY_SPEC],
        out_specs=[row, vec, pl.BlockSpec((D, D), lambda i: (0, 0)), pl.BlockSpec((P, D), lambda i: (0, 0))],
        out_shape=[jax.ShapeDtypeStruct((T, D), F32), jax.ShapeDtypeStruct((1, D), F32),
                   jax.ShapeDtypeStruct((D, D), BF16), jax.ShapeDtypeStruct((P, D), BF16)],
        scratch_shapes=[pltpu.VMEM((D, D), F32), pltpu.VMEM((P, D), F32)],
        compiler_params=_params(1))(dx, gate, pp, hn, p, x, g, wg, after)


def _loss_fwd_bwd(y, tgt):
    T, D = y.shape
    tm = _pick_tm(T, 512)

    def body(y_ref, t_ref, dy_ref, loss_ref):
        e = y_ref[...] - t_ref[...]
        dy_ref[...] = e / D

        @pl.when(pl.program_id(0) == 0)
        def _():
            loss_ref[...] = jnp.zeros_like(loss_ref)

        loss_ref[...] += 0.5 * jnp.sum(jnp.mean(e * e, axis=-1, keepdims=True), axis=0, keepdims=True)

    row = pl.BlockSpec((tm, D), lambda i: (i, 0))
    return pl.pallas_call(
        body, name="loss", grid=(T // tm,), in_specs=[row, row],
        out_specs=[row, pl.BlockSpec((8, LANES), lambda i: (0, 0))],
        out_shape=[jax.ShapeDtypeStruct((T, D), F32), jax.ShapeDtypeStruct((8, LANES), F32)],
        compiler_params=_params(1))(y, tgt)


def _qkv_layout(D):
    n_a = D // (2 * LANES)
    n_b = D // (2 * LANES)
    n_kv = max(1, (2 * n_b) // GROUP_B) * HEAD_DIM // LANES
    return n_a, n_b, n_kv


def _dup_half(xv, half):
    rolled = pltpu.roll(xv, HEAD_DIM, 1)
    lo = _lo_lanes(xv.shape)
    return jnp.where(lo, xv, rolled) if half == 0 else jnp.where(lo, rolled, xv)


def _qkv_post(raw, gains):
    T, W = raw.shape
    n_a, n_b, n_kv = _qkv_layout(W * 4 // 9)
    tm = _pick_tm(T, 256)
    o_qb = 3 * n_a

    def norm(xv, gv, scale):
        ms = _seg_sum(xv * xv) * (1.0 / HEAD_DIM)
        return xv * lax.rsqrt(ms + EPS) * gv * scale

    def body(raw_ref, g_ref, a_ref, b_ref):
        def blk(cb):
            return raw_ref[:, cb * LANES:(cb + 1) * LANES]

        def gn(cb):
            return g_ref[:, cb * LANES:(cb + 1) * LANES]

        for cb in range(n_a):
            a_ref[:, cb * LANES:(cb + 1) * LANES] = norm(blk(cb), gn(cb), Q_SCALE)
            cbk = n_a + cb
            a_ref[:, cbk * LANES:(cbk + 1) * LANES] = norm(blk(cbk), gn(cbk), 1.0)
            cbv = 2 * n_a + cb
            a_ref[:, cbv * LANES:(cbv + 1) * LANES] = blk(cbv)
        for cb in range(n_b):
            src = o_qb + cb
            b_ref[:, cb * LANES:(cb + 1) * LANES] = norm(blk(src), gn(src), Q_SCALE)
        for e in range(n_b):
            kvh = (2 * e) // GROUP_B
            ck = o_qb + n_b + kvh // 2
            cv = ck + n_kv
            kn = norm(blk(ck), gn(ck), 1.0)
            b_ref[:, (n_b + e) * LANES:(n_b + e + 1) * LANES] = _dup_half(kn, kvh % 2)
            b_ref[:, (2 * n_b + e) * LANES:(2 * n_b + e + 1) * LANES] = _dup_half(blk(cv), kvh % 2)

    wa, wb = 3 * n_a * LANES, 3 * n_b * LANES
    return pl.pallas_call(
        body, name="qkv_post", grid=(T // tm,),
        in_specs=[pl.BlockSpec((tm, W), lambda i: (i, 0)), pl.BlockSpec((1, W), lambda i: (0, 0))],
        out_specs=[pl.BlockSpec((tm, wa), lambda i: (i, 0)), pl.BlockSpec((tm, wb), lambda i: (i, 0))],
        out_shape=[jax.ShapeDtypeStruct((T, wa), F32), jax.ShapeDtypeStruct((T, wb), F32)],
        compiler_params=_params(1))(raw, gains)


def _qkv_post_bwd(raw, gains, d_a, d_b):
    T, W = raw.shape
    n_a, n_b, n_kv = _qkv_layout(W * 4 // 9)
    tm = _pick_tm(T, 256)
    o_qb = 3 * n_a

    def body(raw_ref, g_ref, daq, dak, dav, dbq, dbk, dbv, o_ref, dg_ref):
        @pl.when(pl.program_id(0) == 0)
        def _():
            dg_ref[...] = jnp.zeros_like(dg_ref)

        def cols(ref, cb):
            return ref[:, cb * LANES:(cb + 1) * LANES]

        def norm_bwd(cb, dy, scale):
            xv = cols(raw_ref, cb)
            gv = cols(g_ref, cb)
            r = lax.rsqrt(_seg_sum(xv * xv) * (1.0 / HEAD_DIM) + EPS)
            xh = xv * r
            dys = dy * scale
            dyg = dys * gv
            dxv = r * (dyg - xh * (_seg_sum(dyg * xh) * (1.0 / HEAD_DIM)))
            o_ref[:, cb * LANES:(cb + 1) * LANES] = dxv.astype(BF16)
            dg_ref[:, cb * LANES:(cb + 1) * LANES] += jnp.sum(dys * xh, axis=0, keepdims=True)

        def fold(ref, kv_blk):
            halves = []
            for half in range(2):
                kvh = 2 * kv_blk + half
                blocks = [e for e in range(n_b) if (2 * e) // GROUP_B == kvh]
                s = cols(ref, blocks[0])
                for e in blocks[1:]:
                    s = s + cols(ref, e)
                halves.append(s + pltpu.roll(s, HEAD_DIM, 1))
            return jnp.where(_lo_lanes(halves[0].shape), halves[0], halves[1])

        for cb in range(n_a):
            norm_bwd(cb, cols(daq, cb), Q_SCALE)
            norm_bwd(n_a + cb, cols(dak, cb), 1.0)
            cbv = 2 * n_a + cb
            o_ref[:, cbv * LANES:(cbv + 1) * LANES] = cols(dav, cb).astype(BF16)
        for cb in range(n_b):
            norm_bwd(o_qb + cb, cols(dbq, cb), Q_SCALE)
        for kb in range(n_kv):
            ck = o_qb + n_b + kb
            cv = ck + n_kv
            norm_bwd(ck, fold(dbk, kb), 1.0)
            o_ref[:, cv * LANES:(cv + 1) * LANES] = fold(dbv, kb).astype(BF16)

    hw_a, hw_b = n_a * LANES, n_b * LANES
    return pl.pallas_call(
        body, name="qkv_post_bwd", grid=(T // tm,),
        in_specs=[pl.BlockSpec((tm, W), lambda i: (i, 0)), pl.BlockSpec((1, W), lambda i: (0, 0))]
        + [pl.BlockSpec((tm, hw_a), lambda i: (i, 0))] * 3 + [pl.BlockSpec((tm, hw_b), lambda i: (i, 0))] * 3,
        out_specs=[pl.BlockSpec((tm, W), lambda i: (i, 0)), pl.BlockSpec((1, W), lambda i: (0, 0))],
        out_shape=[jax.ShapeDtypeStruct((T, W), BF16), jax.ShapeDtypeStruct((1, W), F32)],
        compiler_params=_params(1))(raw, gains, *d_a, *d_b)


def _t5_bucket_np(rel):
    half = N_BUCKETS // 2
    max_exact = half // 2
    ret = np.where(rel > 0, half, 0)
    n = np.abs(rel)
    nf = np.maximum(n, 1).astype(np.float32)
    large = max_exact + (np.log(nf / np.float32(max_exact)) / np.float32(math.log(MAX_DISTANCE / max_exact))
                         * np.float32(half - max_exact)).astype(np.int32)
    large = np.minimum(large, half - 1)
    return ret + np.where(n < max_exact, n, large)


def _window_pad(radius):
    assert radius <= QBLOCK
    return HEAD_DIM if radius <= HEAD_DIM else QBLOCK


def _bucket_maps(configs):
    pad = _window_pad(configs[0][0])
    q = np.arange(QBLOCK)[:, None]
    kk = np.arange(QBLOCK + 2 * pad)[None, :]
    rel = kk - pad - q
    maps = [np.where(np.abs(rel) <= radius, _t5_bucket_np(rel * dil), -1) for radius, dil in configs]
    return np.stack(maps).astype(np.int32)


def _bias_build(rel_bias, bmaps, col0):
    n_sets, _, W = bmaps.shape
    n_heads = rel_bias.shape[1] // 2

    def body(rb_ref, bm_ref, o_ref):
        h = pl.program_id(1)
        bm = bm_ref[...]

        def step(n, acc):
            return jnp.where(bm == n, rb_ref[n, col0 + h], acc)

        o_ref[...] = lax.fori_loop(0, N_BUCKETS, step, jnp.where(bm < 0, NEG, 0.0).astype(F32))

    return pl.pallas_call(
        body, name="bias_build", grid=(n_sets, n_heads),
        in_specs=[pl.BlockSpec(memory_space=pltpu.SMEM), pl.BlockSpec((None, QBLOCK, W), lambda s, h: (s, 0, 0))],
        out_specs=pl.BlockSpec((None, None, QBLOCK, W), lambda s, h: (s, h, 0, 0)),
        out_shape=jax.ShapeDtypeStruct((n_sets, n_heads, QBLOCK, W), F32),
        compiler_params=_params(2))(rel_bias, bmaps)


def _bias_grad(dtiles, bmaps, col0):
    n_sets, _, W = bmaps.shape
    n_heads = dtiles[0].shape[1]
    n_l = len(dtiles)

    def body(*refs):
        bm_ref, o_ref = refs[0], refs[1 + n_l]
        s, h = pl.program_id(0), pl.program_id(1)

        @pl.when((s == 0) & (h == 0))
        def _():
            o_ref[...] = jnp.zeros_like(o_ref)

        d = refs[1][...]
        for r in refs[2:1 + n_l]:
            d = d + r[...]
        acc8 = d[0:8, :]
        for a in range(1, QBLOCK // 8):
            acc8 = acc8 + pltpu.roll(d[8 * a:8 * a + 8, :], W - 8 * a, 1)
        per_offset = acc8[0:1, :]
        for b in range(1, 8):
            per_offset = per_offset + pltpu.roll(acc8[b:b + 1, :], W - b, 1)
        bucket = lax.broadcasted_iota(jnp.int32, (N_BUCKETS, W), 0)
        hit = bucket == bm_ref[0:1, :]
        per_bucket = jnp.sum(jnp.where(hit, per_offset, 0.0), axis=1, keepdims=True)
        lanes = lax.broadcasted_iota(jnp.int32, o_ref.shape, 1)
        o_ref[...] += jnp.where(lanes == col0 + h, per_bucket, 0.0)

    tile = pl.BlockSpec((None, None, QBLOCK, W), lambda s, h: (s, h, 0, 0))
    return pl.pallas_call(
        body, name="bias_grad", grid=(n_sets, n_heads),
        in_specs=[pl.BlockSpec((None, QBLOCK, W), lambda s, h: (s, 0, 0))] + [tile] * n_l,
        out_specs=pl.BlockSpec((N_BUCKETS, LANES), lambda s, h: (0, 0)),
        out_shape=jax.ShapeDtypeStruct((N_BUCKETS, LANES), F32), compiler_params=_params(2))(bmaps, *dtiles)


def _rows(l_start, n, d, r):
    if d == 1:
        return pl.ds(pl.multiple_of(l_start, 8), n)
    return pl.ds(l_start * d + r, n, stride=d)


def _stack_heads(xv, lo):
    z = jnp.zeros_like(xv)
    return jnp.concatenate([jnp.where(lo, xv, z), jnp.where(lo, z, xv)], axis=0)


def _unstack_heads(xv, lo):
    return jnp.where(lo, xv[:QBLOCK], xv[QBLOCK:])


def _per_head_rows(v0, v1):
    if jnp.ndim(v0) == 0:
        return jnp.where(lax.broadcasted_iota(jnp.int32, (2 * QBLOCK, 1), 0) < QBLOCK, v0, v1)
    return jnp.concatenate([v0, v1], axis=0)


def _block_geometry(b, nb_sub, pad):
    r, lb = b // nb_sub, b % nb_sub
    l0 = lb * QBLOCK
    lp = jnp.maximum(l0 - pad, 0)
    ln = jnp.minimum(l0 + QBLOCK, nb_sub * QBLOCK - pad)
    return r, l0, lp, ln, (lb == 0).astype(jnp.int32) + 2 * (lb == nb_sub - 1).astype(jnp.int32)


def _edge_variants(bias, pad):
    n_br, _, _, W = bias.shape
    col = np.arange(W)
    left, right = col < pad, col >= pad + QBLOCK
    masked = jnp.asarray(np.stack([np.zeros(W, bool), left, right, left | right]))
    return jnp.where(masked[None, :, None, :], NEG, bias.reshape(n_br, 1, -1, W))


def _window(ref, l0, lp, ln, pad, d, r):
    return jnp.concatenate([ref[_rows(lp, pad, d, r), :], ref[_rows(l0, QBLOCK, d, r), :],
                            ref[_rows(ln, pad, d, r), :]], axis=0)


def _attn_fwd(qkv, bias, sink, dils, pad):
    T = qkv.shape[0]
    hw = qkv.shape[1] // 3
    ng = hw // LANES
    n_br = len(dils)
    n_blocks = T // QBLOCK
    W = QBLOCK + 2 * pad
    chunk = 256

    def body(sink_ref, q_ref, k_ref, v_ref, bias_ref, o_ref, lse_ref, *scratch):
        g = pl.program_id(0)
        lo = _lo_lanes((QBLOCK, LANES))
        snk = _per_head_rows(sink_ref[2 * g], sink_ref[2 * g + 1])
        for c, d in enumerate(dils):
            nb_sub = n_blocks // d
            o_dst = scratch[0].at[c] if n_br > 1 else o_ref
            l_dst = scratch[1].at[c] if n_br > 1 else lse_ref

            def block(b, carry, c=c, d=d, nb_sub=nb_sub, o_dst=o_dst, l_dst=l_dst):
                r, l0, lp, ln, edge = _block_geometry(b, nb_sub, pad)
                q = _stack_heads(q_ref[_rows(l0, QBLOCK, d, r), :].astype(BF16), lo)
                k = _window(k_ref, l0, lp, ln, pad, d, r).astype(BF16)
                v = _window(v_ref, l0, lp, ln, pad, d, r).astype(BF16)
                s = _dot_nt(q, k) + bias_ref[c, edge]
                m = jnp.maximum(jnp.max(s, axis=1, keepdims=True), snk)
                p = jnp.exp(s - m)
                den = jnp.sum(p, axis=1, keepdims=True) + jnp.exp(snk - m)
                o_dst[_rows(l0, QBLOCK, d, r), :] = _unstack_heads(_dot(p.astype(BF16), v) / den, lo)
                l_dst[_rows(l0, QBLOCK, d, r), :] = _unstack_heads(
                    jnp.broadcast_to(m + jnp.log(den), (2 * QBLOCK, LANES)), lo)
                return carry

            lax.fori_loop(0, n_blocks, block, 0, unroll=FWD_UNROLL)

        if n_br > 1:
            def merge(i, carry):
                rs = pl.ds(pl.multiple_of(i * chunk, chunk), chunk)
                ls = [scratch[1][c, rs, :] for c in range(n_br)]
                m = ls[0]
                for t in ls[1:]:
                    m = jnp.maximum(m, t)
                ws = [jnp.exp(t - m) for t in ls]
                z = ws[0]
                acc = ws[0] * scratch[0][0, rs, :]
                for c in range(1, n_br):
                    z = z + ws[c]
                    acc = acc + ws[c] * scratch[0][c, rs, :]
                o_ref[rs, :] = acc / z
                lse_ref[rs, :] = m + jnp.log(z)
                return carry

            lax.fori_loop(0, T // chunk, merge, 0)

    def col(base):
        return pl.BlockSpec((T, LANES), lambda g: (0, base + g))

    out = jax.ShapeDtypeStruct((T, hw), F32)
    scratch = [pltpu.VMEM((n_br, T, LANES), F32)] * 2 if n_br > 1 else []
    return pl.pallas_call(
        body, name="attn_fwd", grid=(ng,),
        in_specs=[pl.BlockSpec(memory_space=pltpu.SMEM), col(0), col(ng), col(2 * ng),
                  pl.BlockSpec((n_br, 4, 2 * QBLOCK, W), lambda g: (0, 0, g, 0))],
        out_specs=[col(0), col(0)], out_shape=[out, out], scratch_shapes=scratch,
        compiler_params=_params(1))(sink, qkv, qkv, qkv, bias)


def _attn_bwd(qkv, bias, sink, dils, pad, do, lse, dd, col_base, after):
    T = qkv.shape[0]
    hw = qkv.shape[1] // 3
    ng = hw // LANES
    n_br = len(dils)
    n_blocks = T // QBLOCK
    W = QBLOCK + 2 * pad

    def body(sink_ref, q_ref, k_ref, v_ref, bias_ref, do_ref, lse_ref, dd_ref, after_ref,
             dq_ref, dk_ref, dv_ref, dt_ref, ds_ref):
        g = pl.program_id(0)
        dq_ref[...] = jnp.zeros_like(dq_ref)
        dk_ref[...] = jnp.zeros_like(dk_ref)
        dv_ref[...] = jnp.zeros_like(dv_ref)
        dt_ref[...] = jnp.zeros_like(dt_ref)
        ds_ref[...] = jnp.zeros_like(ds_ref)
        lo = _lo_lanes((QBLOCK, LANES))
        snk = jnp.where(lo, sink_ref[2 * g], sink_ref[2 * g + 1])
        for c, d in enumerate(dils):
            nb_sub = n_blocks // d

            def block(b, carry, c=c, d=d, nb_sub=nb_sub):
                r, l0, lp, ln, edge = _block_geometry(b, nb_sub, pad)
                rows_q = _rows(l0, QBLOCK, d, r)
                q = _stack_heads(q_ref[rows_q, :].astype(BF16), lo)
                k = _window(k_ref, l0, lp, ln, pad, d, r).astype(BF16)
                v = _window(v_ref, l0, lp, ln, pad, d, r).astype(BF16)
                dob = _stack_heads(do_ref[rows_q, :].astype(BF16), lo)
                lse_b = lse_ref[rows_q, :]
                dd_b = dd_ref[rows_q, :]
                s = _dot_nt(q, k) + bias_ref[c, edge]
                p = jnp.exp(s - _per_head_rows(lse_b[:, 0:1], lse_b[:, HEAD_DIM:HEAD_DIM + 1]))
                ds = p * (_dot_nt(dob, v) - _per_head_rows(dd_b[:, 0:1], dd_b[:, HEAD_DIM:HEAD_DIM + 1]))
                dsb = ds.astype(BF16)
                dkw = _dot_tn(dsb, q)
                dvw = _dot_tn(p.astype(BF16), dob)
                dt_ref[c] += ds
                dq_ref[rows_q, :] += _unstack_heads(_dot(dsb, k), lo)
                ds_ref[0:1, :] += jnp.sum(-jnp.exp(snk - lse_b) * dd_b, axis=0, keepdims=True)
                for part, (start, n) in zip((0, pad, pad + QBLOCK), ((lp, pad), (l0, QBLOCK), (ln, pad))):
                    dk_ref[_rows(start, n, d, r), :] += dkw[part:part + n]
                    dv_ref[_rows(start, n, d, r), :] += dvw[part:part + n]
                return carry

            lax.fori_loop(0, n_blocks, block, 0, unroll=BWD_UNROLL)

    def col(base):
        return pl.BlockSpec((T, LANES), lambda g: (0, base + g))

    tile = pl.BlockSpec((n_br, 2 * QBLOCK, W), lambda g: (0, g, 0))
    full = jax.ShapeDtypeStruct((T, hw), F32)
    dq, dk, dv, dt, dsink = pl.pallas_call(
        body, name="attn_bwd", grid=(ng,),
        in_specs=[pl.BlockSpec(memory_space=pltpu.SMEM), col(0), col(ng), col(2 * ng),
                  pl.BlockSpec((n_br, 4, 2 * QBLOCK, W), lambda g: (0, 0, g, 0)),
                  col(col_base), col(0), col(col_base), ANY_SPEC],
        out_specs=[col(0), col(0), col(0), tile, pl.BlockSpec((None, 8, LANES), lambda g: (g, 0, 0))],
        out_shape=[full, full, full, jax.ShapeDtypeStruct((n_br, 2 * ng * QBLOCK, W), F32),
                   jax.ShapeDtypeStruct((ng, 8, LANES), F32)],
        compiler_params=_params(1))(sink, qkv, qkv, qkv, bias, do, lse, dd, after)
    return dq, dk, dv, dt.reshape(n_br, 2 * ng, QBLOCK, W), dsink


def _mix_bwd_in(dx, wo, o_list):
    T, D = dx.shape
    widths = [o.shape[1] for o in o_list]
    hw = sum(widths)
    n = len(o_list)
    tm = _pick_tm(T, 256)

    def body(*refs):
        dx_ref, w_ref, o_refs, do_ref, dd_ref = refs[0], refs[1], refs[2:2 + n], refs[2 + n], refs[3 + n]
        dov = _dot_nt(dx_ref[...].astype(BF16), w_ref[...])
        do_ref[...] = dov
        off = 0
        for o_ref, k in zip(o_refs, widths):
            prod = dov[:, off:off + k] * o_ref[...]
            for cb in range(k // LANES):
                dd_ref[:, off + cb * LANES:off + (cb + 1) * LANES] = _seg_sum(prod[:, cb * LANES:(cb + 1) * LANES])
            off += k

    row = pl.BlockSpec((tm, hw), lambda i: (i, 0))
    out = jax.ShapeDtypeStruct((T, hw), F32)
    return pl.pallas_call(
        body, name="mix_bwd_in", grid=(T // tm,),
        in_specs=[pl.BlockSpec((tm, D), lambda i: (i, 0)), pl.BlockSpec((hw, D), lambda i: (0, 0))]
        + [pl.BlockSpec((tm, k), lambda i: (i, 0)) for k in widths],
        out_specs=[row, row], out_shape=[out, out], compiler_params=_params(1))(dx, wo, *o_list)


def _mesh_pos():
    return lax.axis_index("x"), lax.axis_index("y"), lax.axis_index("c")


def _my_chip():
    return 2 * lax.axis_index("x") + lax.axis_index("y")


def _other_chips(x, y):
    return [(1 - x, y), (x, 1 - y), (1 - x, 1 - y)]


def _half_rows(rows, cc):
    hr = rows // 2
    return pl.ds(pl.multiple_of(cc * hr, 16), hr)


def _cast_into_slot(w, l):
    _, R, C = w.shape
    tm = _pick_tm(R, 512)

    def body(w_ref, o_ref):
        o_ref[...] = w_ref[...].astype(BF16)

    return pl.pallas_call(
        body, name="cast_into_slot", grid=(R // tm,),
        in_specs=[pl.BlockSpec((None, tm, C), lambda i: (l, i, 0))],
        out_specs=pl.BlockSpec((None, tm, C), lambda i: (_my_chip(), i, 0)),
        out_shape=jax.ShapeDtypeStruct((N_CHIPS, R, C), BF16), compiler_params=_params(1))(w)


def _split_start(name, arrays, make_copies, n_sem, after):
    n = len(arrays)
    n_in = n + (0 if after is None else 1)

    def body(*refs):
        send_s, recv_s = refs[n_in], refs[n_in + 1]
        token = refs[n_in + 2 + n]
        for send, _ in make_copies(refs[:n], send_s, recv_s):
            send.start()
        token[...] = jnp.zeros_like(token)

    res = pl.pallas_call(
        body, name=name,
        out_shape=(pltpu.SemaphoreType.DMA((n_sem,)), pltpu.SemaphoreType.DMA((n_sem,)),
                   *[pltpu.HBM(a.shape, a.dtype) for a in arrays], jax.ShapeDtypeStruct((8, LANES), F32)),
        in_specs=[HBM_SPEC] * n + [ANY_SPEC] * (n_in - n),
        out_specs=(SEM_SPEC, SEM_SPEC, *([HBM_SPEC] * n), pl.BlockSpec(memory_space=pltpu.VMEM)),
        input_output_aliases={i: 2 + i for i in range(n)},
        compiler_params=pltpu.CompilerParams(has_side_effects=pltpu.SideEffectType.DATAFLOW_SIDE_EFFECTING),
    )(*[pltpu.with_memory_space_constraint(a, pltpu.HBM) for a in arrays], *([] if after is None else [after]))
    return res[0], res[1], list(res[2:2 + n]), res[2 + n]


def _split_wait(name, send_s, recv_s, arrays, make_copies, after):
    n = len(arrays)

    def body(*refs):
        for send, landed in make_copies(refs[:n], refs[n], refs[n + 1]):
            send.wait_send()
            landed.wait_recv()

    return list(pl.pallas_call(
        body, name=name, out_shape=[pltpu.HBM(a.shape, a.dtype) for a in arrays],
        in_specs=[HBM_SPEC] * n + [SEM_SPEC, SEM_SPEC, ANY_SPEC], out_specs=[HBM_SPEC] * n,
        input_output_aliases={i: i for i in range(n)},
        compiler_params=pltpu.CompilerParams(has_side_effects=pltpu.SideEffectType.DATAFLOW_SIDE_EFFECTING),
    )(*arrays, send_s, recv_s, after))


def _gather_copies(shapes):
    n = len(shapes)

    def make(refs, send_s, recv_s):
        x, y, c = _mesh_pos()
        my = 2 * x + y
        copies = []
        for w in range(n):
            for k, (px, py) in enumerate(_other_chips(x, y)):
                def part(slot, w=w):
                    return refs[w].at[slot, _half_rows(shapes[w][1], c), :]
                sems = dict(send_sem=send_s.at[k * n + w], recv_sem=recv_s.at[k * n + w],
                            device_id=(px, py, c), device_id_type=MESH)
                copies.append((pltpu.make_async_remote_copy(src_ref=part(my), dst_ref=part(my), **sems),
                               pltpu.make_async_remote_copy(src_ref=part(2 * px + py), dst_ref=part(2 * px + py), **sems)))
        return copies

    return make


def _forward_copies(shapes):
    n = len(shapes)

    def make(refs, send_s, recv_s):
        x, y, c = _mesh_pos()
        copies = []
        for w in range(n):
            for k, (px, py) in enumerate(_other_chips(x, y)):
                def part(cc, w=w, slot=2 * px + py):
                    return refs[w].at[slot, _half_rows(shapes[w][1], cc), :]
                sems = dict(send_sem=send_s.at[k * n + w], recv_sem=recv_s.at[k * n + w],
                            device_id=(x, y, 1 - c), device_id_type=MESH)
                copies.append((pltpu.make_async_remote_copy(src_ref=part(c), dst_ref=part(c), **sems),
                               pltpu.make_async_remote_copy(src_ref=part(1 - c), dst_ref=part(1 - c), **sems)))
        return copies

    return make


def _pair_forward(bufs):
    n = len(bufs)
    make = _forward_copies([b.shape for b in bufs])

    def body(*refs):
        copies = make(refs[n:2 * n], refs[2 * n], refs[2 * n + 1])
        for send, _ in copies:
            send.start()
        for _, landed in copies:
            landed.wait_recv()
        for send, _ in copies:
            send.wait_send()

    return list(pl.pallas_call(
        body, name="ag_pair_forward", in_specs=[HBM_SPEC] * n, out_specs=[HBM_SPEC] * n,
        out_shape=[jax.ShapeDtypeStruct(b.shape, b.dtype) for b in bufs],
        input_output_aliases={w: w for w in range(n)},
        scratch_shapes=[pltpu.SemaphoreType.DMA((3 * n,)), pltpu.SemaphoreType.DMA((3 * n,))],
    )(*bufs))


def _pair_exchange_copies(shapes):
    n = len(shapes)

    def make(refs, send_s, recv_s):
        x, y, c = _mesh_pos()
        copies = []
        for t in range(n):
            sems = dict(send_sem=send_s.at[t], recv_sem=recv_s.at[t], device_id=(x, y, 1 - c), device_id_type=MESH)
            land = refs[n + t]
            copies.append((pltpu.make_async_remote_copy(
                src_ref=refs[t].at[:, _half_rows(shapes[t][1], 1 - c), :], dst_ref=land, **sems),
                pltpu.make_async_remote_copy(src_ref=land, dst_ref=land, **sems)))
        return copies

    return make


def _pair_share_copies(shapes):
    n = len(shapes)

    def make(refs, send_s, recv_s):
        x, y, c = _mesh_pos()
        copies = []
        for t in range(n):
            def half(cc, t=t):
                return refs[t].at[_half_rows(shapes[t][0], cc), :]
            sems = dict(send_sem=send_s.at[t], recv_sem=recv_s.at[t], device_id=(x, y, 1 - c), device_id_type=MESH)
            copies.append((pltpu.make_async_remote_copy(src_ref=half(c), dst_ref=half(c), **sems),
                           pltpu.make_async_remote_copy(src_ref=half(1 - c), dst_ref=half(1 - c), **sems)))
        return copies

    return make


def _rs_add_pair(grad, recv):
    n_slot, hr, C = recv.shape
    tm = _pick_tm(hr, 192)
    nb = hr // tm

    def body(a_ref, b_ref, o_ref):
        o_ref[...] = (a_ref[...].astype(F32) + b_ref[...].astype(F32)).astype(BF16)

    blk = pl.BlockSpec((n_slot, tm, C), lambda i: (0, i, 0))
    return pl.pallas_call(
        body, name="rs_add_pair", grid=(nb,),
        in_specs=[pl.BlockSpec((n_slot, tm, C), lambda i: (0, lax.axis_index("c") * nb + i, 0)), blk],
        out_specs=blk, out_shape=jax.ShapeDtypeStruct(recv.shape, BF16), compiler_params=_params(1))(grad, recv)


def _scatter_copies(n):
    def make(refs, send_s, recv_s):
        x, y, c = _mesh_pos()
        copies = []
        for t in range(n):
            for k, (px, py) in enumerate(_other_chips(x, y)):
                sems = dict(send_sem=send_s.at[3 * t + k], recv_sem=recv_s.at[3 * t + k],
                            device_id=(px, py, c), device_id_type=MESH)
                land = refs[n + t].at[k]
                copies.append((pltpu.make_async_remote_copy(src_ref=refs[t].at[2 * px + py], dst_ref=land, **sems),
                               pltpu.make_async_remote_copy(src_ref=land, dst_ref=land, **sems)))
        return copies

    return make


def _rs_add_chips(part, recv):
    _, hr, C = part.shape
    tm = _pick_tm(hr, 256)
    nb = hr // tm

    def body(a_ref, r0, r1, r2, o_ref):
        o_ref[...] = ((a_ref[...].astype(F32) + r0[...].astype(F32)) + r1[...].astype(F32)) + r2[...].astype(F32)

    def rel(k):
        return pl.BlockSpec((None, tm, C), lambda i: (k, i, 0))

    return pl.pallas_call(
        body, name="rs_add_chips", grid=(nb,),
        in_specs=[pl.BlockSpec((None, tm, C), lambda i: (_my_chip(), i, 0)), rel(0), rel(1), rel(2)],
        out_specs=pl.BlockSpec((tm, C), lambda i: (lax.axis_index("c") * nb + i, 0)),
        out_shape=jax.ShapeDtypeStruct((2 * hr, C), F32), compiler_params=_params(1))(part, recv, recv, recv)


def _allreduce_small(v):
    rows = v.shape[0]

    def body(v_ref, o_ref, buf, send_s, recv_s):
        x, y, c = _mesh_pos()
        me = 4 * x + 2 * y + c
        buf[me] = v_ref[...]
        copies = []
        for r in range(1, N_DEV):
            px = 1 - x if r & 4 else x
            py = 1 - y if r & 2 else y
            pc = 1 - c if r & 1 else c
            send = pltpu.make_async_remote_copy(
                src_ref=v_ref, dst_ref=buf.at[me], send_sem=send_s.at[r - 1], recv_sem=recv_s.at[r - 1],
                device_id=(px, py, pc), device_id_type=MESH)
            peer_slot = buf.at[4 * px + 2 * py + pc]
            landed = pltpu.make_async_remote_copy(
                src_ref=peer_slot, dst_ref=peer_slot, send_sem=send_s.at[r - 1], recv_sem=recv_s.at[r - 1],
                device_id=(px, py, pc), device_id_type=MESH)
            copies.append((send, landed))
        for send, _ in copies:
            send.start()
        for _, landed in copies:
            landed.wait_recv()
        for send, _ in copies:
            send.wait_send()
        acc = buf[0]
        for j in range(1, N_DEV):
            acc = acc + buf[j]
        o_ref[...] = acc

    vm = pl.BlockSpec(memory_space=pltpu.VMEM)
    return pl.pallas_call(
        body, name="allreduce_small", in_specs=[vm], out_specs=vm,
        out_shape=jax.ShapeDtypeStruct((rows, LANES), F32),
        scratch_shapes=[pltpu.VMEM((N_DEV, rows, LANES), F32), pltpu.SemaphoreType.DMA((N_DEV - 1,)),
                        pltpu.SemaphoreType.DMA((N_DEV - 1,))],
    )(v)


def _adamw_fn(w, g, m, v):
    m2 = ADAM_B1 * m + (1.0 - ADAM_B1) * g
    v2 = ADAM_B2 * v + (1.0 - ADAM_B2) * (g * g)
    m_hat = m2 / (1.0 - ADAM_B1 ** ADAM_STEP)
    v_hat = v2 / (1.0 - ADAM_B2 ** ADAM_STEP)
    delta = -ADAM_LR * (m_hat / (jnp.sqrt(v_hat) + ADAM_EPS) + ADAM_WD * w)
    return g, delta, m2, v2


def _adamw_layer(w, g, m, v, l, prev, after):
    NL, R, C = w.shape
    tm = _pick_tm(R, 128)
    n_prev = 0 if prev is None else 4

    def body(w_ref, g_ref, m_ref, v_ref, after_ref, *rest):
        outs = rest[n_prev:]
        for o_ref, val in zip(outs, _adamw_fn(w_ref[...], g_ref[...], m_ref[...], v_ref[...])):
            o_ref[...] = val

    lay = pl.BlockSpec((None, tm, C), lambda i: (l, i, 0))
    shape = jax.ShapeDtypeStruct((NL, R, C), F32)
    return pl.pallas_call(
        body, name="adamw", grid=(R // tm,),
        in_specs=[lay, pl.BlockSpec((tm, C), lambda i: (i, 0)), lay, lay, ANY_SPEC] + [ANY_SPEC] * n_prev,
        out_specs=[lay] * 4, out_shape=[shape] * 4,
        input_output_aliases={5 + j: j for j in range(n_prev)},
        compiler_params=_params(1))(w, g, m, v, after, *(prev or []))


def _pack_small(parts):
    out = []
    for a in parts:
        flat = a.reshape(-1)
        n = -(-flat.shape[0] // (8 * LANES)) * 8 * LANES
        out.append(jnp.pad(flat, (0, n - flat.shape[0])).reshape(-1, LANES))
    return jnp.concatenate(out, axis=0)


def _unpack_small(packed, like):
    out, r = [], 0
    for a in like:
        size = int(np.prod(a.shape))
        rows = -(-size // (8 * LANES)) * 8
        out.append(packed[r:r + rows].reshape(-1)[:size].reshape(a.shape))
        r += rows
    return out


def _ffn_forward(x, g, win, wout):
    h, silu, dgate, act = _ffn_up(x, g, win)
    return _mm_res("ffn_down", x, [act], wout, 0.5), (x, h, silu, dgate, act)


def _ffn_backward(dx, saved, g, win, wout, after, mid=None):
    x, h, silu, dgate, act = saved
    D = x.shape[1]
    wc = win.shape[2]
    dgu = _ffn_bwd_a(dx, wout, silu, dgate, after)
    if mid is not None:
        mid(dgu)
    dwout = _mm_tn("dw_ffn_out", act, wc, dx, D, 2, True, False, 0.5)
    dwin = _mm_tn_pairs("dw_ffn_in", h, dgu, wc, 4)
    dx_in, dg = _bwd_into_norm("ffn_bwd_b", dgu, wc, win, 4, 2, x, g, dx)
    return dx_in, dg, dwin, dwout.reshape(N_CHIPS, -1, D)


GROUP_FFN1 = ("ffn1_w_in", "ffn1_w_out")
GROUP_REST = ("w_qkv", "w_o", "ffn2_w_in", "ffn2_w_out", "w_ple_gate", "w_ple_proj")


def _gather_start(tag, slotted, after):
    return _split_start("ag_start_" + tag, slotted, _gather_copies([a.shape for a in slotted]), 3 * len(slotted), after)


def _gather_finish(tag, started, after):
    send_s, recv_s, arrays, _ = started
    return _pair_forward(_split_wait("ag_wait_" + tag, send_s, recv_s, arrays,
                                     _gather_copies([a.shape for a in arrays]), after))


def _scatter_exchange(tag, grads):
    n = len(grads)
    land = [lax.empty((g_.shape[0], g_.shape[1] // 2, g_.shape[2]), g_.dtype) for g_ in grads]
    return _split_start("rs_px_start_" + tag, list(grads) + land, _pair_exchange_copies([g_.shape for g_ in grads]),
                        n, None)


def _scatter_chips(tag, started, after):
    send_s, recv_s, arrays, _ = started
    n = len(arrays) // 2
    arrays = _split_wait("rs_px_wait_" + tag, send_s, recv_s, arrays,
                         _pair_exchange_copies([a.shape for a in arrays[:n]]), after)
    part = [_rs_add_pair(g_, r_) for g_, r_ in zip(arrays[:n], arrays[n:])]
    land = [lax.empty((3,) + p_.shape[1:], p_.dtype) for p_ in part]
    return _split_start("rs_start_" + tag, part + land, _scatter_copies(n), 3 * n, None)


def _scatter_share(tag, started, after):
    send_s, recv_s, arrays, _ = started
    n = len(arrays) // 2
    arrays = _split_wait("rs_wait_" + tag, send_s, recv_s, arrays, _scatter_copies(n), after)
    halves = [_rs_add_chips(p_, r_) for p_, r_ in zip(arrays[:n], arrays[n:])]
    return _split_start("rs_ps_start_" + tag, halves, _pair_share_copies([h_.shape for h_ in halves]), n, None)


def _scatter_done(tag, started, after):
    send_s, recv_s, arrays, _ = started
    return _split_wait("rs_ps_wait_" + tag, send_s, recv_s, arrays, _pair_share_copies([a.shape for a in arrays]), after)


def kernel(x, p, rel_bias, norm_ffn1, ffn1_w_in, ffn1_w_out, norm_mix, w_qkv, q_norm_a, k_norm_a, q_norm_b, k_norm_b, sink_b, w_o, norm_ffn2, ffn2_w_in, ffn2_w_out, norm_ple, w_ple_gate, w_ple_proj, loss_target, m_rel_bias, m_norm_ffn1, m_ffn1_w_in, m_ffn1_w_out, m_norm_mix, m_w_qkv, m_q_norm_a, m_k_norm_a, m_q_norm_b, m_k_norm_b, m_sink_b, m_w_o, m_norm_ffn2, m_ffn2_w_in, m_ffn2_w_out, m_norm_ple, m_w_ple_gate, m_w_ple_proj, v_rel_bias, v_norm_ffn1, v_ffn1_w_in, v_ffn1_w_out, v_norm_mix, v_w_qkv, v_q_norm_a, v_k_norm_a, v_q_norm_b, v_k_norm_b, v_sink_b, v_w_o, v_norm_ffn2, v_ffn2_w_in, v_ffn2_w_out, v_norm_ple, v_w_ple_gate, v_w_ple_proj):
    given = dict(locals())
    T, D = x.shape[1], x.shape[2]
    NL = norm_ffn1.shape[0]
    x0 = x.reshape(T, D)
    tgt = loss_target.reshape(T, D)
    n_a, n_b, n_kv = _qkv_layout(D)

    assert NL == 2
    slot = [{name: _cast_into_slot(given[name], l) for name in BIG} for l in range(NL)]
    ag_a = _gather_start("a", [slot[0][n] for n in GROUP_FFN1], None)
    ag_b = _gather_start("b", [slot[0][n] for n in GROUP_REST], ag_a[3])
    ag_1 = _gather_start("1", [slot[1][n] for n in BIG], ag_b[3])

    def by_rows(a):
        return a.reshape(-1, a.shape[-1])

    def by_cols(a):
        return a.transpose(1, 0, 2).reshape(a.shape[1], -1)

    QW = N_CHIPS * w_qkv.shape[2]

    dils = tuple(d for _, d in DILATED_CONFIGS)
    cfg_a = [(w // (2 * d), d) for w, d in DILATED_CONFIGS]
    pad_a, pad_b = _window_pad(cfg_a[0][0]), _window_pad(SWA_RADIUS)
    bmaps_a, bmaps_b = jnp.asarray(_bucket_maps(cfg_a)), jnp.asarray(_bucket_maps([(SWA_RADIUS, 1)]))
    n_heads = rel_bias.shape[1] // 2
    bias_a = _edge_variants(_bias_build(rel_bias, bmaps_a, 0), pad_a)
    bias_b = _edge_variants(_bias_build(rel_bias, bmaps_b, n_heads), pad_b)
    no_sink = jnp.full((n_heads,), NEG, F32)

    def gains_row(l):
        ones = jnp.ones((n_a * LANES,), F32)
        return jnp.concatenate([
            jnp.tile(q_norm_a[l], 2 * n_a), jnp.tile(k_norm_a[l], 2 * n_a), ones,
            jnp.tile(q_norm_b[l], 2 * n_b), jnp.tile(k_norm_b[l], 2 * n_kv), jnp.ones((n_kv * LANES,), F32)]).reshape(1, QW)

    saved, weights = [], []
    xc = x0
    pf_1 = None
    for l in range(NL):
        s, w = {}, {}
        if l == 0:
            w.update(zip(GROUP_FFN1, _gather_finish("a", ag_a, ag_1[3])))
        else:
            shapes = [a.shape for a in pf_1[2]]
            w.update(zip(BIG, _split_wait("ag_pf_wait_1", pf_1[0], pf_1[1], pf_1[2], _forward_copies(shapes), xc)))
        w["ffn1_w_out"] = by_rows(w["ffn1_w_out"])
        xc, s["ffn1"] = _ffn_forward(xc, norm_ffn1[l:l + 1], w["ffn1_w_in"], w["ffn1_w_out"])
        s["x1"] = xc
        if l == 0:
            w.update(zip(GROUP_REST, _gather_finish("b", ag_b, xc)))
        w["w_qkv"] = by_cols(w["w_qkv"])
        w["w_ple_proj"] = by_cols(w["w_ple_proj"])
        for name in ("w_o", "ffn2_w_out", "w_ple_gate"):
            w[name] = by_rows(w[name])
        s["h2"], raw = _norm_proj("qkv_proj", xc, norm_mix[l:l + 1], w["w_qkv"])
        s["raw"] = raw
        s["qkv_a"], s["qkv_b"] = _qkv_post(raw, gains_row(l))
        s["o_a"], s["lse_a"] = _attn_fwd(s["qkv_a"], bias_a, no_sink, dils, pad_a)
        s["o_b"], s["lse_b"] = _attn_fwd(s["qkv_b"], bias_b, sink_b[l], (1,), pad_b)
        xc = _mm_res("attn_out", xc, [s["o_a"], s["o_b"]], w["w_o"], 1.0)
        xc, s["ffn2"] = _ffn_forward(xc, norm_ffn2[l:l + 1], w["ffn2_w_in"], w["ffn2_w_out"])
        s["x3"] = xc
        if l == 0:
            landed = _split_wait("ag_wait_1", ag_1[0], ag_1[1], ag_1[2], _gather_copies([a.shape for a in ag_1[2]]), xc)
            pf_1 = _split_start("ag_pf_start_1", landed, _forward_copies([a.shape for a in landed]), 3 * len(landed),
                                None)
        s["p"] = p[l].reshape(T, -1)
        s["hn"], xc, s["gate"], s["pp"] = _ple_fwd(xc, norm_ple[l:l + 1], s["p"], w["w_ple_gate"], w["w_ple_proj"])
        saved.append(s)
        weights.append(w)

    dx, loss_blk = _loss_fwd_bwd(xc, tgt)
    loss = lax.psum(loss_blk[0, 0], ("x", "y", "c"))

    gs = {name: [None] * NL for name in SMALL if name != "rel_bias"}
    dt_a, dt_b = [], []

    def layer_backward(l, dx, hooks):
        def at(point, ready, *more):
            return hooks[point](ready, *more) if point in hooks else ready

        s, w, gw = saved[l], weights[l], {}
        dx, gs["norm_ple"][l], dwg, dwp = _ple_bwd(dx, s["gate"], s["pp"], s["hn"], s["p"], s["x3"],
                                                   norm_ple[l:l + 1], w["w_ple_gate"], at("start", dx))
        gw["w_ple_gate"] = dwg.reshape(N_CHIPS, -1, D)
        gw["w_ple_proj"] = dwp.reshape(dwp.shape[0], N_CHIPS, -1).transpose(1, 0, 2)
        dx, gs["norm_ffn2"][l], gw["ffn2_w_in"], gw["ffn2_w_out"] = _ffn_backward(
            dx, s["ffn2"], norm_ffn2[l:l + 1], w["ffn2_w_in"], w["ffn2_w_out"], at("after_ple", dx))
        do, dd = _mix_bwd_in(dx, w["w_o"], [s["o_a"], s["o_b"]])
        hwa = s["o_a"].shape[1]
        gw["w_o"] = jnp.concatenate([
            _mm_tn("dw_o", o_, o_.shape[1], dx, D, 1, False, False, 1.0).reshape(-1, D // N_CHIPS, D)
            for o_ in (s["o_a"], s["o_b"])], axis=0)
        dqa, dka, dva, dt, _ = _attn_bwd(s["qkv_a"], bias_a, no_sink, dils, pad_a, do, s["lse_a"], dd, 0, do)
        dt_a.append(dt)
        dqb, dkb, dvb, dt, dsink = _attn_bwd(s["qkv_b"], bias_b, sink_b[l], (1,), pad_b, do, s["lse_b"], dd,
                                             hwa // LANES, at("after_attn_a", dqa))
        dt_b.append(dt)
        gs["sink_b"][l] = dsink[:, 0, ::HEAD_DIM].reshape(-1)
        draw, dgains = _qkv_post_bwd(s["raw"], gains_row(l), (dqa, dka, dva), (dqb, dkb, dvb))
        dgv = dgains.reshape(-1, HEAD_DIM)
        gs["q_norm_a"][l] = dgv[:2 * n_a].sum(0)
        gs["k_norm_a"][l] = dgv[2 * n_a:4 * n_a].sum(0)
        gs["q_norm_b"][l] = dgv[6 * n_a:6 * n_a + 2 * n_b].sum(0)
        gs["k_norm_b"][l] = dgv[6 * n_a + 2 * n_b:6 * n_a + 2 * n_b + 2 * n_kv].sum(0)
        dwqkv = _mm_tn("dw_qkv", s["h2"], D, draw, QW, 1, False, False, 1.0)
        gw["w_qkv"] = dwqkv.reshape(D, N_CHIPS, -1).transpose(1, 0, 2)
        dx, gs["norm_mix"][l] = _bwd_into_norm("qkv_bwd_b", draw, QW, w["w_qkv"], 1, 1, s["x1"], norm_mix[l:l + 1], dx)
        dx, gs["norm_ffn1"][l], gw["ffn1_w_in"], gw["ffn1_w_out"] = _ffn_backward(
            dx, s["ffn1"], norm_ffn1[l:l + 1], w["ffn1_w_in"], w["ffn1_w_out"],
            at("before_ffn1", dx, [gw[n] for n in GROUP_REST]), hooks.get("in_ffn1"))
        return dx, gw

    out = {}

    def adamw_group(names, l, grads, after):
        for name, g_ in zip(names, grads):
            out[name] = _adamw_layer(given[name], g_, given["m_" + name], given["v_" + name], l, out.get(name), after)
        return out[names[-1]][0]

    dx, gw1 = layer_backward(NL - 1, dx, {})
    px_1 = _scatter_exchange("1", [gw1[n] for n in BIG])
    rs = {}

    def chips_1(ready):
        rs["chips_1"] = _scatter_chips("1", px_1, ready)
        return rs["chips_1"][3]

    def share_1(ready):
        rs["share_1"] = _scatter_share("1", rs["chips_1"], ready)
        return rs["share_1"][3]

    def exchange_0a(ready, grads):
        rs["px_0a"] = _scatter_exchange("0a", grads)
        return rs["px_0a"][3]

    def chips_0a(ready):
        rs["chips_0a"] = _scatter_chips("0a", rs["px_0a"], ready)

    dx, gw0 = layer_backward(0, dx, {"start": lambda ready: px_1[3], "after_ple": chips_1, "after_attn_a": share_1,
                                     "before_ffn1": exchange_0a, "in_ffn1": chips_0a})
    grad_x = dx.reshape(x.shape)

    share_0a = _scatter_share("0a", rs["chips_0a"], dx)
    px_0b = _scatter_exchange("0b", [gw0[n] for n in GROUP_FFN1])
    g_1 = _scatter_done("1", rs["share_1"], px_0b[3])
    chips_0b = _scatter_chips("0b", px_0b, g_1[0])
    ready = adamw_group(BIG, 1, g_1, chips_0b[3])
    ready = adamw_group(GROUP_REST, 0, _scatter_done("0a", share_0a, ready), ready)
    share_0b = _scatter_share("0b", chips_0b, ready)

    d_rel_bias = (_bias_grad(dt_a, bmaps_a, 0) + _bias_grad(dt_b, bmaps_b, n_heads))[:, :rel_bias.shape[1]]
    small_g = [d_rel_bias] + [jnp.stack([t.reshape(-1) for t in gs[name]]) for name in SMALL[1:]]
    g_sum = _allreduce_small(_pack_small(small_g))
    res = _ew("adamw_small", _adamw_fn,
              [_pack_small([given[n] for n in SMALL]), g_sum, _pack_small([given["m_" + n] for n in SMALL]),
               _pack_small([given["v_" + n] for n in SMALL])], [(LANES, F32)] * 4)
    like = [given[n] for n in SMALL]
    unpacked = [_unpack_small(r, like) for r in res]
    for i, name in enumerate(SMALL):
        out[name] = [u[i] for u in unpacked]

    adamw_group(GROUP_FFN1, 0, _scatter_done("0b", share_0b, res[0]), res[0])

    return (loss, grad_x, *[out[n][0] for n in WEIGHTS], *[out[n][1] for n in WEIGHTS],
            *[out[n][2] for n in WEIGHTS], *[out[n][3] for n in WEIGHTS])
```

```python
import functools
import math

import numpy as np
import jax
import jax.numpy as jnp
from jax import lax
from jax.experimental import pallas as pl
from jax.experimental.pallas import tpu as pltpu

F32 = jnp.float32
BF16 = jnp.bfloat16
MESH = pl.DeviceIdType.MESH

HEAD_DIM = 64
LANES = 128
QBLOCK = 128
FWD_UNROLL, BWD_UNROLL = 4, 4
N_BUCKETS = 32
MAX_DISTANCE = 1024
DILATED_CONFIGS = ((128, 1), (512, 4), (2048, 16))
SWA_RADIUS = 128
GROUP_B = 4
EPS = 1e-6
NEG = -1e30
Q_SCALE = HEAD_DIM ** -0.5
ADAM_LR, ADAM_B1, ADAM_B2, ADAM_EPS, ADAM_WD, ADAM_STEP = 0.001, 0.9, 0.999, 1e-08, 0.01, 10
VMEM_LIMIT = 56 * 2 ** 20
N_CHIPS = 4
N_DEV = 8

BIG = ("ffn1_w_in", "ffn1_w_out", "w_qkv", "w_o", "ffn2_w_in", "ffn2_w_out", "w_ple_gate", "w_ple_proj")
SMALL = ("rel_bias", "norm_ffn1", "norm_mix", "q_norm_a", "k_norm_a", "q_norm_b", "k_norm_b", "sink_b",
         "norm_ffn2", "norm_ple")
WEIGHTS = ("rel_bias", "norm_ffn1", "ffn1_w_in", "ffn1_w_out", "norm_mix", "w_qkv", "q_norm_a", "k_norm_a",
           "q_norm_b", "k_norm_b", "sink_b", "w_o", "norm_ffn2", "ffn2_w_in", "ffn2_w_out", "norm_ple",
           "w_ple_gate", "w_ple_proj")


HBM_SPEC = pl.BlockSpec(memory_space=pltpu.HBM)
ANY_SPEC = pl.BlockSpec(memory_space=pl.ANY)
SEM_SPEC = pl.BlockSpec(memory_space=pltpu.SEMAPHORE)


def _params(n_grid):
    return pltpu.CompilerParams(dimension_semantics=("arbitrary",) * n_grid, vmem_limit_bytes=VMEM_LIMIT)


def _pick_tm(rows, cap):
    t = (min(cap, rows) // 16) * 16
    while t >= 16:
        if rows % t == 0:
            return t
        t -= 16
    return rows


def _dot(a, b):
    return jnp.dot(a, b, preferred_element_type=F32)


def _dot_nt(a, b):
    return lax.dot_general(a, b, (((1,), (1,)), ((), ())), preferred_element_type=F32)


def _dot_tn(a, b):
    return lax.dot_general(a, b, (((0,), (0,)), ((), ())), preferred_element_type=F32)


def _sigmoid(z):
    return 1.0 / (1.0 + jnp.exp(-z))


def _lo_lanes(shape):
    return lax.broadcasted_iota(jnp.int32, shape, len(shape) - 1) % LANES < HEAD_DIM


def _seg_sum(blk):
    lo = _lo_lanes(blk.shape)
    s_lo = jnp.sum(jnp.where(lo, blk, 0.0), axis=1, keepdims=True)
    s_hi = jnp.sum(jnp.where(lo, 0.0, blk), axis=1, keepdims=True)
    return jnp.where(lo, s_lo, s_hi)


def _rms_bwd_tile(x, g, dh):
    r = lax.rsqrt(jnp.mean(x * x, axis=-1, keepdims=True) + EPS)
    xh = x * r
    dyg = dh * g
    dx = r * (dyg - xh * jnp.mean(dyg * xh, axis=-1, keepdims=True))
    return dx, jnp.sum(dh * xh, axis=0, keepdims=True)


def _ew(name, fn, ins, out_defs, cap=512):
    rows = ins[0].shape[0]
    tm = _pick_tm(rows, cap)
    n_in = len(ins)

    def body(*refs):
        vals = fn(*[r[...] for r in refs[:n_in]])
        if not isinstance(vals, tuple):
            vals = (vals,)
        for r, v in zip(refs[n_in:], vals):
            r[...] = v.astype(r.dtype)

    return pl.pallas_call(
        body, name=name, grid=(rows // tm,),
        in_specs=[pl.BlockSpec((tm, a.shape[1]), lambda i: (i, 0)) for a in ins],
        out_specs=[pl.BlockSpec((tm, c), lambda i: (i, 0)) for c, _ in out_defs],
        out_shape=[jax.ShapeDtypeStruct((rows, c), dt) for c, dt in out_defs],
        compiler_params=_params(1))(*ins)


def _rms_tile(xv, gv):
    r = lax.rsqrt(jnp.mean(xv * xv, axis=-1, keepdims=True) + EPS)
    return (xv * r * gv).astype(BF16)


def _ffn_up(x, g, win):
    T, D = x.shape
    wc = win.shape[2]
    tm = _pick_tm(T, 512)

    def body(x_ref, g_ref, wg_ref, wu_ref, h_ref, silu_ref, dgate_ref, act_ref, wcat):
        @pl.when(pl.program_id(1) == 0)
        def _():
            wcat[:, :wc] = wg_ref[...]
            wcat[:, wc:] = wu_ref[...]

        hv = _rms_tile(x_ref[...], g_ref[...])
        h_ref[...] = hv
        gu = _dot(hv, wcat[...])
        gte, u = gu[:, :wc], gu[:, wc:]
        sg = _sigmoid(gte)
        silu = gte * sg
        silu_ref[...] = silu.astype(BF16)
        dgate_ref[...] = ((sg + silu * (1.0 - sg)) * u).astype(BF16)
        act_ref[...] = (silu * u).astype(BF16)

    out = jax.ShapeDtypeStruct((T, 2 * wc), BF16)
    ospec = pl.BlockSpec((tm, wc), lambda j, i: (i, j))
    nt = T // tm
    h_spec = pl.BlockSpec((tm, D), lambda j, i: (jnp.where(j == 0, i, nt), 0))
    return pl.pallas_call(
        body, name="ffn_up", grid=(2, nt),
        in_specs=[pl.BlockSpec((tm, D), lambda j, i: (i, 0)), pl.BlockSpec((1, D), lambda j, i: (0, 0)),
                  pl.BlockSpec((None, D, wc), lambda j, i: (j, 0, 0)),
                  pl.BlockSpec((None, D, wc), lambda j, i: (j + 2, 0, 0))],
        out_specs=[h_spec] + [ospec] * 3, out_shape=[jax.ShapeDtypeStruct((T + tm, D), BF16)] + [out] * 3,
        scratch_shapes=[pltpu.VMEM((D, 2 * wc), BF16)], compiler_params=_params(2))(x, g, win, win)


def _mm_res(name, res, a_list, w, scale):
    T, N = res.shape
    n = len(a_list)
    widths = [a.shape[1] for a in a_list]
    tm = _pick_tm(T, 512)

    def body(*refs):
        r_ref, a_refs, w_refs, o_ref = refs[0], refs[1:1 + n], refs[1 + n:1 + 2 * n], refs[1 + 2 * n]
        acc = _dot(a_refs[0][...].astype(BF16), w_refs[0][...])
        for a_ref, w_ref in zip(a_refs[1:], w_refs[1:]):
            acc = acc + _dot(a_ref[...].astype(BF16), w_ref[...])
        o_ref[...] = r_ref[...] + scale * acc

    w_specs, off = [], 0
    for k in widths:
        w_specs.append(pl.BlockSpec((k, N), lambda i, blk=off // k: (blk, 0)))
        off += k
    return pl.pallas_call(
        body, name=name, grid=(T // tm,),
        in_specs=[pl.BlockSpec((tm, N), lambda i: (i, 0))]
        + [pl.BlockSpec((tm, k), lambda i: (i, 0)) for k in widths] + w_specs,
        out_specs=pl.BlockSpec((tm, N), lambda i: (i, 0)),
        out_shape=jax.ShapeDtypeStruct((T, N), F32), compiler_params=_params(1))(res, *a_list, *([w] * n))


def _norm_proj(name, x, g, w):
    T, K = x.shape
    N = w.shape[1]
    tm = _pick_tm(T, 512)

    def body(x_ref, g_ref, w_ref, h_ref, o_ref):
        hv = _rms_tile(x_ref[...], g_ref[...])
        h_ref[...] = hv
        o_ref[...] = _dot(hv, w_ref[...])

    row = pl.BlockSpec((tm, K), lambda i: (i, 0))
    return pl.pallas_call(
        body, name=name, grid=(T // tm,),
        in_specs=[row, pl.BlockSpec((1, K), lambda i: (0, 0)), pl.BlockSpec((K, N), lambda i: (0, 0))],
        out_specs=[row, pl.BlockSpec((tm, N), lambda i: (i, 0))],
        out_shape=[jax.ShapeDtypeStruct((T, K), BF16), jax.ShapeDtypeStruct((T, N), F32)],
        compiler_params=_params(1))(x, g, w)


def _mm_tn(name, a, a_w, b, b_w, n_slots, a_by_slot, b_by_slot, scale, tm_cap=512):
    T = b.shape[0]
    tm = _pick_tm(T, tm_cap)
    nt = T // tm

    def body(a_ref, b_ref, o_ref, acc):
        i = pl.program_id(1)

        @pl.when(i == 0)
        def _():
            acc[...] = jnp.zeros_like(acc)

        acc[...] += _dot_tn(a_ref[...].astype(BF16), b_ref[...].astype(BF16))

        @pl.when(i == nt - 1)
        def _():
            o_ref[...] = (acc[...] * scale).astype(BF16)

    return pl.pallas_call(
        body, name=name, grid=(n_slots, nt),
        in_specs=[pl.BlockSpec((tm, a_w), (lambda s, i: (i, s)) if a_by_slot else (lambda s, i: (i, 0))),
                  pl.BlockSpec((tm, b_w), (lambda s, i: (i, s)) if b_by_slot else (lambda s, i: (i, 0)))],
        out_specs=pl.BlockSpec((None, a_w, b_w), lambda s, i: (s, 0, 0)),
        out_shape=jax.ShapeDtypeStruct((n_slots, a_w, b_w), BF16),
        scratch_shapes=[pltpu.VMEM((a_w, b_w), F32)], compiler_params=_params(2))(a, b)


def _mm_tn_pairs(name, a, b, b_w, n_slots):
    T = b.shape[0]
    a_w = a.shape[1]
    tm = _pick_tm(T, 512)
    nt = T // tm

    def body(a_ref, b_ref, o_ref, acc):
        i = pl.program_id(1)

        @pl.when(i == 0)
        def _():
            acc[...] = jnp.zeros_like(acc)

        acc[...] += _dot_tn(a_ref[...], b_ref[...])

        @pl.when(i == nt - 1)
        def _():
            o_ref[0] = acc[:, :b_w].astype(BF16)
            o_ref[1] = acc[:, b_w:].astype(BF16)

    return pl.pallas_call(
        body, name=name, grid=(n_slots // 2, nt),
        in_specs=[pl.BlockSpec((tm, a_w), lambda s, i: (i, 0)), pl.BlockSpec((tm, 2 * b_w), lambda s, i: (i, s))],
        out_specs=pl.BlockSpec((2, a_w, b_w), lambda s, i: (s, 0, 0)),
        out_shape=jax.ShapeDtypeStruct((n_slots, a_w, b_w), BF16),
        scratch_shapes=[pltpu.VMEM((a_w, 2 * b_w), F32)], compiler_params=_params(2))(a, b)


def _ffn_bwd_a(dx, wout, silu, dgate, after):
    T, D = dx.shape
    F = silu.shape[1]
    tm = _pick_tm(T, 256)

    def body(dx_ref, w_ref, s_ref, dg_ref, after_ref, o_ref):
        dact = 0.5 * _dot_nt(dx_ref[...].astype(BF16), w_ref[...])
        o_ref[:, :F] = (dact * dg_ref[...].astype(F32)).astype(BF16)
        o_ref[:, F:] = (dact * s_ref[...].astype(F32)).astype(BF16)

    act_spec = pl.BlockSpec((tm, F), lambda i: (i, 0))
    return pl.pallas_call(
        body, name="ffn_bwd_a", grid=(T // tm,),
        in_specs=[pl.BlockSpec((tm, D), lambda i: (i, 0)), pl.BlockSpec((F, D), lambda i: (0, 0)),
                  act_spec, act_spec, ANY_SPEC],
        out_specs=pl.BlockSpec((tm, 2 * F), lambda i: (i, 0)),
        out_shape=jax.ShapeDtypeStruct((T, 2 * F), BF16),
        compiler_params=_params(1))(dx, wout, silu, dgate, after)


def _bwd_into_norm(name, d, d_w, w, n_slots, group, x, g, dx_in, after):
    T, D = x.shape
    tm = _pick_tm(T, 256)
    n_mm = n_slots // group
    k_w = group * d_w

    def body(*refs):
        d_refs, w_hbm = refs[:n_mm], refs[n_mm]
        x_ref, g_ref, dxi_ref, _, dx_ref, dg_ref, wcat, sem = refs[n_mm + 1:]

        @pl.when(pl.program_id(0) == 0)
        def _():
            copies = [pltpu.make_async_copy(w_hbm.at[s] if w.ndim == 3 else w_hbm,
                                            wcat.at[s // group, :, pl.ds((s % group) * d_w, d_w)], sem.at[s])
                      for s in range(n_slots)]
            for cp in copies:
                cp.start()
            for cp in copies:
                cp.wait()
            dg_ref[...] = jnp.zeros_like(dg_ref)

        dh = _dot_nt(d_refs[0][...], wcat[0])
        for m in range(1, n_mm):
            dh = dh + _dot_nt(d_refs[m][...], wcat[m])
        dxn, dg = _rms_bwd_tile(x_ref[...], g_ref[...], dh)
        dx_ref[...] = dxi_ref[...] + dxn
        dg_ref[...] += dg

    row = pl.BlockSpec((tm, D), lambda i: (i, 0))
    vec = pl.BlockSpec((1, D), lambda i: (0, 0))
    return pl.pallas_call(
        body, name=name, grid=(T // tm,),
        in_specs=[pl.BlockSpec((tm, k_w), lambda i, m=m: (i, m)) for m in range(n_mm)]
        + [ANY_SPEC, row, vec, row, ANY_SPEC],
        out_specs=[row, vec],
        out_shape=[jax.ShapeDtypeStruct((T, D), F32), jax.ShapeDtypeStruct((1, D), F32)],
        scratch_shapes=[pltpu.VMEM((n_mm, D, k_w), BF16), pltpu.SemaphoreType.DMA((n_slots,))],
        compiler_params=_params(1))(*([d] * n_mm + [w, x, g, dx_in, after]))


def _ple_fwd(x, g, p, wg, wp, after):
    T, D = x.shape
    P = p.shape[1]
    tm = _pick_tm(T, 256)

    def body(x_ref, g_ref, p_ref, wg_ref, wp_ref, after_ref, hn_ref, xo_ref, gate_ref, pp_ref):
        xv = x_ref[...]
        hn = _rms_tile(xv, g_ref[...])
        hn_ref[...] = hn
        gate = _sigmoid(_dot(hn, wg_ref[...]))
        pp = _dot(p_ref[...].astype(BF16), wp_ref[...])
        gate_ref[...] = gate
        pp_ref[...] = pp
        xo_ref[...] = xv + gate * pp

    row = pl.BlockSpec((tm, D), lambda i: (i, 0))
    out = jax.ShapeDtypeStruct((T, D), F32)
    return pl.pallas_call(
        body, name="ple_fwd", grid=(T // tm,),
        in_specs=[row, pl.BlockSpec((1, D), lambda i: (0, 0)), pl.BlockSpec((tm, P), lambda i: (i, 0)),
                  pl.BlockSpec((D, D), lambda i: (0, 0)), pl.BlockSpec((P, D), lambda i: (0, 0)), ANY_SPEC],
        out_specs=[row, row, row, row], out_shape=[jax.ShapeDtypeStruct((T, D), BF16), out, out, out],
        compiler_params=_params(1))(x, g, p, wg, wp, after)


def _ple_bwd(dx, gate, pp, hn, p, x, g, wg, after):
    T, D = x.shape
    P = p.shape[1]
    tm = _pick_tm(T, 256)
    nt = T // tm

    def body(dx_ref, gate_ref, pp_ref, hn_ref, p_ref, x_ref, g_ref, wg_ref, after_ref,
             dxo_ref, dg_ref, dwg_ref, dwp_ref, acc_g, acc_p):
        i = pl.program_id(0)

        @pl.when(i == 0)
        def _():
            acc_g[...] = jnp.zeros_like(acc_g)
            acc_p[...] = jnp.zeros_like(acc_p)
            dg_ref[...] = jnp.zeros_like(dg_ref)

        dxv = dx_ref[...]
        gate = gate_ref[...]
        dz = (dxv * pp_ref[...] * gate * (1.0 - gate)).astype(BF16)
        dpp = (dxv * gate).astype(BF16)
        acc_g[...] += _dot_tn(hn_ref[...], dz)
        acc_p[...] += _dot_tn(p_ref[...].astype(BF16), dpp)
        dxn, dg = _rms_bwd_tile(x_ref[...], g_ref[...], _dot_nt(dz, wg_ref[...]))
        dxo_ref[...] = dxv + dxn
        dg_ref[...] += dg

        @pl.when(i == nt - 1)
        def _():
            dwg_ref[...] = acc_g[...].astype(BF16)
            dwp_ref[...] = acc_p[...].astype(BF16)

    row = pl.BlockSpec((tm, D), lambda i: (i, 0))
    vec = pl.BlockSpec((1, D), lambda i: (0, 0))
    return pl.pallas_call(
        body, name="ple_bwd", grid=(nt,),
        in_specs=[row, row, row, row, pl.BlockSpec((tm, P), lambda i: (i, 0)), row, vec,
                  pl.BlockSpec((D, D), lambda i: (0, 0)), ANY_SPEC],
        out_specs=[row, vec, pl.BlockSpec((D, D), lambda i: (0, 0)), pl.BlockSpec((P, D), lambda i: (0, 0))],
        out_shape=[jax.ShapeDtypeStruct((T, D), F32), jax.ShapeDtypeStruct((1, D), F32),
                   jax.ShapeDtypeStruct((D, D), BF16), jax.ShapeDtypeStruct((P, D), BF16)],
        scratch_shapes=[pltpu.VMEM((D, D), F32), pltpu.VMEM((P, D), F32)],
        compiler_params=_params(1))(dx, gate, pp, hn, p, x, g, wg, after)


def _loss_fwd_bwd(y, tgt):
    T, D = y.shape
    tm = _pick_tm(T, 512)

    def body(y_ref, t_ref, dy_ref, loss_ref):
        e = y_ref[...] - t_ref[...]
        dy_ref[...] = e / D

        @pl.when(pl.program_id(0) == 0)
        def _():
            loss_ref[...] = jnp.zeros_like(loss_ref)

        loss_ref[...] += 0.5 * jnp.sum(jnp.mean(e * e, axis=-1, keepdims=True), axis=0, keepdims=True)

    row = pl.BlockSpec((tm, D), lambda i: (i, 0))
    return pl.pallas_call(
        body, name="loss", grid=(T // tm,), in_specs=[row, row],
        out_specs=[row, pl.BlockSpec((8, LANES), lambda i: (0, 0))],
        out_shape=[jax.ShapeDtypeStruct((T, D), F32), jax.ShapeDtypeStruct((8, LANES), F32)],
        compiler_params=_params(1))(y, tgt)


def _qkv_layout(D):
    n_a = D // (2 * LANES)
    n_b = D // (2 * LANES)
    n_kv = max(1, (2 * n_b) // GROUP_B) * HEAD_DIM // LANES
    return n_a, n_b, n_kv


def _dup_half(xv, half):
    rolled = pltpu.roll(xv, HEAD_DIM, 1)
    lo = _lo_lanes(xv.shape)
    return jnp.where(lo, xv, rolled) if half == 0 else jnp.where(lo, rolled, xv)


def _qkv_post(raw, gains):
    T, W = raw.shape
    n_a, n_b, n_kv = _qkv_layout(W * 4 // 9)
    tm = _pick_tm(T, 256)
    o_qb = 3 * n_a

    def norm(xv, gv, scale):
        ms = _seg_sum(xv * xv) * (1.0 / HEAD_DIM)
        return xv * lax.rsqrt(ms + EPS) * gv * scale

    def body(raw_ref, g_ref, a_ref, b_ref):
        def blk(cb):
            return raw_ref[:, cb * LANES:(cb + 1) * LANES]

        def gn(cb):
            return g_ref[:, cb * LANES:(cb + 1) * LANES]

        for cb in range(n_a):
            a_ref[:, cb * LANES:(cb + 1) * LANES] = norm(blk(cb), gn(cb), Q_SCALE)
            cbk = n_a + cb
            a_ref[:, cbk * LANES:(cbk + 1) * LANES] = norm(blk(cbk), gn(cbk), 1.0)
            cbv = 2 * n_a + cb
            a_ref[:, cbv * LANES:(cbv + 1) * LANES] = blk(cbv)
        for cb in range(n_b):
            src = o_qb + cb
            b_ref[:, cb * LANES:(cb + 1) * LANES] = norm(blk(src), gn(src), Q_SCALE)
        for e in range(n_b):
            kvh = (2 * e) // GROUP_B
            ck = o_qb + n_b + kvh // 2
            cv = ck + n_kv
            kn = norm(blk(ck), gn(ck), 1.0)
            b_ref[:, (n_b + e) * LANES:(n_b + e + 1) * LANES] = _dup_half(kn, kvh % 2)
            b_ref[:, (2 * n_b + e) * LANES:(2 * n_b + e + 1) * LANES] = _dup_half(blk(cv), kvh % 2)

    wa, wb = 3 * n_a * LANES, 3 * n_b * LANES
    return pl.pallas_call(
        body, name="qkv_post", grid=(T // tm,),
        in_specs=[pl.BlockSpec((tm, W), lambda i: (i, 0)), pl.BlockSpec((1, W), lambda i: (0, 0))],
        out_specs=[pl.BlockSpec((tm, wa), lambda i: (i, 0)), pl.BlockSpec((tm, wb), lambda i: (i, 0))],
        out_shape=[jax.ShapeDtypeStruct((T, wa), F32), jax.ShapeDtypeStruct((T, wb), F32)],
        compiler_params=_params(1))(raw, gains)


def _qkv_post_bwd(raw, gains, d_a, d_b):
    T, W = raw.shape
    n_a, n_b, n_kv = _qkv_layout(W * 4 // 9)
    tm = _pick_tm(T, 256)
    o_qb = 3 * n_a

    def body(raw_ref, g_ref, daq, dak, dav, dbq, dbk, dbv, o_ref, dg_ref):
        @pl.when(pl.program_id(0) == 0)
        def _():
            dg_ref[...] = jnp.zeros_like(dg_ref)

        def cols(ref, cb):
            return ref[:, cb * LANES:(cb + 1) * LANES]

        def norm_bwd(cb, dy, scale):
            xv = cols(raw_ref, cb)
            gv = cols(g_ref, cb)
            r = lax.rsqrt(_seg_sum(xv * xv) * (1.0 / HEAD_DIM) + EPS)
            xh = xv * r
            dys = dy * scale
            dyg = dys * gv
            dxv = r * (dyg - xh * (_seg_sum(dyg * xh) * (1.0 / HEAD_DIM)))
            o_ref[:, cb * LANES:(cb + 1) * LANES] = dxv.astype(BF16)
            dg_ref[:, cb * LANES:(cb + 1) * LANES] += jnp.sum(dys * xh, axis=0, keepdims=True)

        def fold(ref, kv_blk):
            halves = []
            for half in range(2):
                kvh = 2 * kv_blk + half
                blocks = [e for e in range(n_b) if (2 * e) // GROUP_B == kvh]
                s = cols(ref, blocks[0])
                for e in blocks[1:]:
                    s = s + cols(ref, e)
                halves.append(s + pltpu.roll(s, HEAD_DIM, 1))
            return jnp.where(_lo_lanes(halves[0].shape), halves[0], halves[1])

        for cb in range(n_a):
            norm_bwd(cb, cols(daq, cb), Q_SCALE)
            norm_bwd(n_a + cb, cols(dak, cb), 1.0)
            cbv = 2 * n_a + cb
            o_ref[:, cbv * LANES:(cbv + 1) * LANES] = cols(dav, cb).astype(BF16)
        for cb in range(n_b):
            norm_bwd(o_qb + cb, cols(dbq, cb), Q_SCALE)
        for kb in range(n_kv):
            ck = o_qb + n_b + kb
            cv = ck + n_kv
            norm_bwd(ck, fold(dbk, kb), 1.0)
            o_ref[:, cv * LANES:(cv + 1) * LANES] = fold(dbv, kb).astype(BF16)

    hw_a, hw_b = n_a * LANES, n_b * LANES
    return pl.pallas_call(
        body, name="qkv_post_bwd", grid=(T // tm,),
        in_specs=[pl.BlockSpec((tm, W), lambda i: (i, 0)), pl.BlockSpec((1, W), lambda i: (0, 0))]
        + [pl.BlockSpec((tm, hw_a), lambda i: (i, 0))] * 3 + [pl.BlockSpec((tm, hw_b), lambda i: (i, 0))] * 3,
        out_specs=[pl.BlockSpec((tm, W), lambda i: (i, 0)), pl.BlockSpec((1, W), lambda i: (0, 0))],
        out_shape=[jax.ShapeDtypeStruct((T, W), BF16), jax.ShapeDtypeStruct((1, W), F32)],
        compiler_params=_params(1))(raw, gains, *d_a, *d_b)


def _t5_bucket_np(rel):
    half = N_BUCKETS // 2
    max_exact = half // 2
    ret = np.where(rel > 0, half, 0)
    n = np.abs(rel)
    nf = np.maximum(n, 1).astype(np.float32)
    large = max_exact + (np.log(nf / np.float32(max_exact)) / np.float32(math.log(MAX_DISTANCE / max_exact))
                         * np.float32(half - max_exact)).astype(np.int32)
    large = np.minimum(large, half - 1)
    return ret + np.where(n < max_exact, n, large)


def _window_pad(radius):
    assert radius <= QBLOCK
    return HEAD_DIM if radius <= HEAD_DIM else QBLOCK


def _bucket_maps(configs):
    pad = _window_pad(configs[0][0])
    q = np.arange(QBLOCK)[:, None]
    kk = np.arange(QBLOCK + 2 * pad)[None, :]
    rel = kk - pad - q
    maps = [np.where(np.abs(rel) <= radius, _t5_bucket_np(rel * dil), -1) for radius, dil in configs]
    return np.stack(maps).astype(np.int32)


def _bias_build(rel_bias, bmaps, col0):
    n_sets, _, W = bmaps.shape
    n_heads = rel_bias.shape[1] // 2

    def body(rb_ref, bm_ref, o_ref):
        h = pl.program_id(1)
        bm = bm_ref[...]

        def step(n, acc):
            return jnp.where(bm == n, rb_ref[n, col0 + h], acc)

        o_ref[...] = lax.fori_loop(0, N_BUCKETS, step, jnp.where(bm < 0, NEG, 0.0).astype(F32))

    return pl.pallas_call(
        body, name="bias_build", grid=(n_sets, n_heads),
        in_specs=[pl.BlockSpec(memory_space=pltpu.SMEM), pl.BlockSpec((None, QBLOCK, W), lambda s, h: (s, 0, 0))],
        out_specs=pl.BlockSpec((None, None, QBLOCK, W), lambda s, h: (s, h, 0, 0)),
        out_shape=jax.ShapeDtypeStruct((n_sets, n_heads, QBLOCK, W), F32),
        compiler_params=_params(2))(rel_bias, bmaps)


def _bias_grad(dtiles, bmaps, col0):
    n_sets, _, W = bmaps.shape
    n_heads = dtiles[0].shape[1]
    n_l = len(dtiles)

    def body(*refs):
        bm_ref, o_ref = refs[0], refs[1 + n_l]
        s, h = pl.program_id(0), pl.program_id(1)

        @pl.when((s == 0) & (h == 0))
        def _():
            o_ref[...] = jnp.zeros_like(o_ref)

        d = refs[1][...]
        for r in refs[2:1 + n_l]:
            d = d + r[...]
        acc8 = d[0:8, :]
        for a in range(1, QBLOCK // 8):
            acc8 = acc8 + pltpu.roll(d[8 * a:8 * a + 8, :], W - 8 * a, 1)
        per_offset = acc8[0:1, :]
        for b in range(1, 8):
            per_offset = per_offset + pltpu.roll(acc8[b:b + 1, :], W - b, 1)
        bucket = lax.broadcasted_iota(jnp.int32, (N_BUCKETS, W), 0)
        hit = bucket == bm_ref[0:1, :]
        per_bucket = jnp.sum(jnp.where(hit, per_offset, 0.0), axis=1, keepdims=True)
        lanes = lax.broadcasted_iota(jnp.int32, o_ref.shape, 1)
        o_ref[...] += jnp.where(lanes == col0 + h, per_bucket, 0.0)

    tile = pl.BlockSpec((None, None, QBLOCK, W), lambda s, h: (s, h, 0, 0))
    return pl.pallas_call(
        body, name="bias_grad", grid=(n_sets, n_heads),
        in_specs=[pl.BlockSpec((None, QBLOCK, W), lambda s, h: (s, 0, 0))] + [tile] * n_l,
        out_specs=pl.BlockSpec((N_BUCKETS, LANES), lambda s, h: (0, 0)),
        out_shape=jax.ShapeDtypeStruct((N_BUCKETS, LANES), F32), compiler_params=_params(2))(bmaps, *dtiles)


def _rows(l_start, n, d, r):
    if d == 1:
        return pl.ds(pl.multiple_of(l_start, 8), n)
    return pl.ds(l_start * d + r, n, stride=d)


def _stack_heads(xv, lo):
    z = jnp.zeros_like(xv)
    return jnp.concatenate([jnp.where(lo, xv, z), jnp.where(lo, z, xv)], axis=0)


def _unstack_heads(xv, lo):
    return jnp.where(lo, xv[:QBLOCK], xv[QBLOCK:])


def _per_head_rows(v0, v1):
    if jnp.ndim(v0) == 0:
        return jnp.where(lax.broadcasted_iota(jnp.int32, (2 * QBLOCK, 1), 0) < QBLOCK, v0, v1)
    return jnp.concatenate([v0, v1], axis=0)


def _block_geometry(b, nb_sub, pad):
    r, lb = b // nb_sub, b % nb_sub
    l0 = lb * QBLOCK
    lp = jnp.maximum(l0 - pad, 0)
    ln = jnp.minimum(l0 + QBLOCK, nb_sub * QBLOCK - pad)
    return r, l0, lp, ln, (lb == 0).astype(jnp.int32) + 2 * (lb == nb_sub - 1).astype(jnp.int32)


def _edge_variants(bias, pad):
    n_br, _, _, W = bias.shape
    col = np.arange(W)
    left, right = col < pad, col >= pad + QBLOCK
    masked = jnp.asarray(np.stack([np.zeros(W, bool), left, right, left | right]))
    return jnp.where(masked[None, :, None, :], NEG, bias.reshape(n_br, 1, -1, W))


def _window(ref, l0, lp, ln, pad, d, r):
    return jnp.concatenate([ref[_rows(lp, pad, d, r), :], ref[_rows(l0, QBLOCK, d, r), :],
                            ref[_rows(ln, pad, d, r), :]], axis=0)


def _attn_fwd(qkv, bias, sink, dils, pad):
    T = qkv.shape[0]
    hw = qkv.shape[1] // 3
    ng = hw // LANES
    n_br = len(dils)
    n_blocks = T // QBLOCK
    W = QBLOCK + 2 * pad
    chunk = 256

    def body(sink_ref, q_ref, k_ref, v_ref, bias_ref, o_ref, lse_ref, *scratch):
        g = pl.program_id(0)
        lo = _lo_lanes((QBLOCK, LANES))
        snk = _per_head_rows(sink_ref[2 * g], sink_ref[2 * g + 1])
        for c, d in enumerate(dils):
            nb_sub = n_blocks // d
            o_dst = scratch[0].at[c] if n_br > 1 else o_ref
            l_dst = scratch[1].at[c] if n_br > 1 else lse_ref

            def block(b, carry, c=c, d=d, nb_sub=nb_sub, o_dst=o_dst, l_dst=l_dst):
                r, l0, lp, ln, edge = _block_geometry(b, nb_sub, pad)
                q = _stack_heads(q_ref[_rows(l0, QBLOCK, d, r), :].astype(BF16), lo)
                k = _window(k_ref, l0, lp, ln, pad, d, r).astype(BF16)
                v = _window(v_ref, l0, lp, ln, pad, d, r).astype(BF16)
                s = _dot_nt(q, k) + bias_ref[c, edge]
                m = jnp.maximum(jnp.max(s, axis=1, keepdims=True), snk)
                p = jnp.exp(s - m)
                den = jnp.sum(p, axis=1, keepdims=True) + jnp.exp(snk - m)
                o_dst[_rows(l0, QBLOCK, d, r), :] = _unstack_heads(_dot(p.astype(BF16), v) / den, lo)
                l_dst[_rows(l0, QBLOCK, d, r), :] = _unstack_heads(
                    jnp.broadcast_to(m + jnp.log(den), (2 * QBLOCK, LANES)), lo)
                return carry

            lax.fori_loop(0, n_blocks, block, 0, unroll=FWD_UNROLL)

        if n_br > 1:
            def merge(i, carry):
                rs = pl.ds(pl.multiple_of(i * chunk, chunk), chunk)
                ls = [scratch[1][c, rs, :] for c in range(n_br)]
                m = ls[0]
                for t in ls[1:]:
                    m = jnp.maximum(m, t)
                ws = [jnp.exp(t - m) for t in ls]
                z = ws[0]
                acc = ws[0] * scratch[0][0, rs, :]
                for c in range(1, n_br):
                    z = z + ws[c]
                    acc = acc + ws[c] * scratch[0][c, rs, :]
                o_ref[rs, :] = acc / z
                lse_ref[rs, :] = m + jnp.log(z)
                return carry

            lax.fori_loop(0, T // chunk, merge, 0)

    def col(base):
        return pl.BlockSpec((T, LANES), lambda g: (0, base + g))

    out = jax.ShapeDtypeStruct((T, hw), F32)
    scratch = [pltpu.VMEM((n_br, T, LANES), F32)] * 2 if n_br > 1 else []
    return pl.pallas_call(
        body, name="attn_fwd", grid=(ng,),
        in_specs=[pl.BlockSpec(memory_space=pltpu.SMEM), col(0), col(ng), col(2 * ng),
                  pl.BlockSpec((n_br, 4, 2 * QBLOCK, W), lambda g: (0, 0, g, 0))],
        out_specs=[col(0), col(0)], out_shape=[out, out], scratch_shapes=scratch,
        compiler_params=_params(1))(sink, qkv, qkv, qkv, bias)


def _attn_bwd(qkv, bias, sink, dils, pad, do, lse, dd, col_base, after):
    T = qkv.shape[0]
    hw = qkv.shape[1] // 3
    ng = hw // LANES
    n_br = len(dils)
    n_blocks = T // QBLOCK
    W = QBLOCK + 2 * pad

    def body(sink_ref, q_ref, k_ref, v_ref, bias_ref, do_ref, lse_ref, dd_ref, after_ref,
             dq_ref, dk_ref, dv_ref, dt_ref, ds_ref):
        g = pl.program_id(0)
        dq_ref[...] = jnp.zeros_like(dq_ref)
        dk_ref[...] = jnp.zeros_like(dk_ref)
        dv_ref[...] = jnp.zeros_like(dv_ref)
        dt_ref[...] = jnp.zeros_like(dt_ref)
        ds_ref[...] = jnp.zeros_like(ds_ref)
        lo = _lo_lanes((QBLOCK, LANES))
        snk = jnp.where(lo, sink_ref[2 * g], sink_ref[2 * g + 1])
        for c, d in enumerate(dils):
            nb_sub = n_blocks // d

            def block(b, carry, c=c, d=d, nb_sub=nb_sub):
                r, l0, lp, ln, edge = _block_geometry(b, nb_sub, pad)
                rows_q = _rows(l0, QBLOCK, d, r)
                q = _stack_heads(q_ref[rows_q, :].astype(BF16), lo)
                k = _window(k_ref, l0, lp, ln, pad, d, r).astype(BF16)
                v = _window(v_ref, l0, lp, ln, pad, d, r).astype(BF16)
                dob = _stack_heads(do_ref[rows_q, :].astype(BF16), lo)
                lse_b = lse_ref[rows_q, :]
                dd_b = dd_ref[rows_q, :]
                s = _dot_nt(q, k) + bias_ref[c, edge]
                p = jnp.exp(s - _per_head_rows(lse_b[:, 0:1], lse_b[:, HEAD_DIM:HEAD_DIM + 1]))
                ds = p * (_dot_nt(dob, v) - _per_head_rows(dd_b[:, 0:1], dd_b[:, HEAD_DIM:HEAD_DIM + 1]))
                dsb = ds.astype(BF16)
                dkw = _dot_tn(dsb, q)
                dvw = _dot_tn(p.astype(BF16), dob)
                dt_ref[c] += ds
                dq_ref[rows_q, :] += _unstack_heads(_dot(dsb, k), lo)
                ds_ref[0:1, :] += jnp.sum(-jnp.exp(snk - lse_b) * dd_b, axis=0, keepdims=True)
                for part, (start, n) in zip((0, pad, pad + QBLOCK), ((lp, pad), (l0, QBLOCK), (ln, pad))):
                    dk_ref[_rows(start, n, d, r), :] += dkw[part:part + n]
                    dv_ref[_rows(start, n, d, r), :] += dvw[part:part + n]
                return carry

            lax.fori_loop(0, n_blocks, block, 0, unroll=BWD_UNROLL)

    def col(base):
        return pl.BlockSpec((T, LANES), lambda g: (0, base + g))

    tile = pl.BlockSpec((n_br, 2 * QBLOCK, W), lambda g: (0, g, 0))
    full = jax.ShapeDtypeStruct((T, hw), F32)
    dq, dk, dv, dt, dsink = pl.pallas_call(
        body, name="attn_bwd", grid=(ng,),
        in_specs=[pl.BlockSpec(memory_space=pltpu.SMEM), col(0), col(ng), col(2 * ng),
                  pl.BlockSpec((n_br, 4, 2 * QBLOCK, W), lambda g: (0, 0, g, 0)),
                  col(col_base), col(0), col(col_base), ANY_SPEC],
        out_specs=[col(0), col(0), col(0), tile, pl.BlockSpec((None, 8, LANES), lambda g: (g, 0, 0))],
        out_shape=[full, full, full, jax.ShapeDtypeStruct((n_br, 2 * ng * QBLOCK, W), F32),
                   jax.ShapeDtypeStruct((ng, 8, LANES), F32)],
        compiler_params=_params(1))(sink, qkv, qkv, qkv, bias, do, lse, dd, after)
    return dq, dk, dv, dt.reshape(n_br, 2 * ng, QBLOCK, W), dsink


def _mix_bwd_in(dx, wo, o_list):
    T, D = dx.shape
    widths = [o.shape[1] for o in o_list]
    hw = sum(widths)
    n = len(o_list)
    tm = _pick_tm(T, 256)

    def body(*refs):
        dx_ref, w_ref, o_refs, do_ref, dd_ref = refs[0], refs[1], refs[2:2 + n], refs[2 + n], refs[3 + n]
        dov = _dot_nt(dx_ref[...].astype(BF16), w_ref[...])
        do_ref[...] = dov
        off = 0
        for o_ref, k in zip(o_refs, widths):
            prod = dov[:, off:off + k] * o_ref[...]
            for cb in range(k // LANES):
                dd_ref[:, off + cb * LANES:off + (cb + 1) * LANES] = _seg_sum(prod[:, cb * LANES:(cb + 1) * LANES])
            off += k

    row = pl.BlockSpec((tm, hw), lambda i: (i, 0))
    out = jax.ShapeDtypeStruct((T, hw), F32)
    return pl.pallas_call(
        body, name="mix_bwd_in", grid=(T // tm,),
        in_specs=[pl.BlockSpec((tm, D), lambda i: (i, 0)), pl.BlockSpec((hw, D), lambda i: (0, 0))]
        + [pl.BlockSpec((tm, k), lambda i: (i, 0)) for k in widths],
        out_specs=[row, row], out_shape=[out, out], compiler_params=_params(1))(dx, wo, *o_list)


def _mesh_pos():
    return lax.axis_index("x"), lax.axis_index("y"), lax.axis_index("c")


def _my_chip():
    return 2 * lax.axis_index("x") + lax.axis_index("y")


def _other_chips(x, y):
    return [(1 - x, y), (x, 1 - y), (1 - x, 1 - y)]


def _half_rows(rows, cc):
    hr = rows // 2
    return pl.ds(pl.multiple_of(cc * hr, 16), hr)


def _cast_into_slot(w, l):
    _, R, C = w.shape
    tm = _pick_tm(R, 512)

    def body(w_ref, o_ref):
        o_ref[...] = w_ref[...].astype(BF16)

    return pl.pallas_call(
        body, name="cast_into_slot", grid=(R // tm,),
        in_specs=[pl.BlockSpec((None, tm, C), lambda i: (l, i, 0))],
        out_specs=pl.BlockSpec((None, tm, C), lambda i: (_my_chip(), i, 0)),
        out_shape=jax.ShapeDtypeStruct((N_CHIPS, R, C), BF16), compiler_params=_params(1))(w)


def _split_start(name, arrays, make_copies, n_sem, after):
    n = len(arrays)
    n_in = n + (0 if after is None else 1)

    def body(*refs):
        send_s, recv_s = refs[n_in], refs[n_in + 1]
        token = refs[n_in + 2 + n]
        for send, _ in make_copies(refs[:n], send_s, recv_s):
            send.start()
        token[...] = jnp.zeros_like(token)

    res = pl.pallas_call(
        body, name=name,
        out_shape=(pltpu.SemaphoreType.DMA((n_sem,)), pltpu.SemaphoreType.DMA((n_sem,)),
                   *[pltpu.HBM(a.shape, a.dtype) for a in arrays], jax.ShapeDtypeStruct((8, LANES), F32)),
        in_specs=[HBM_SPEC] * n + [ANY_SPEC] * (n_in - n),
        out_specs=(SEM_SPEC, SEM_SPEC, *([HBM_SPEC] * n), pl.BlockSpec(memory_space=pltpu.VMEM)),
        input_output_aliases={i: 2 + i for i in range(n)},
        compiler_params=pltpu.CompilerParams(has_side_effects=pltpu.SideEffectType.DATAFLOW_SIDE_EFFECTING),
    )(*[pltpu.with_memory_space_constraint(a, pltpu.HBM) for a in arrays], *([] if after is None else [after]))
    return res[0], res[1], list(res[2:2 + n]), res[2 + n]


def _split_wait(name, send_s, recv_s, arrays, make_copies, after):
    n = len(arrays)

    def body(*refs):
        for send, landed in make_copies(refs[:n], refs[n], refs[n + 1]):
            send.wait_send()
            landed.wait_recv()

    return list(pl.pallas_call(
        body, name=name, out_shape=[pltpu.HBM(a.shape, a.dtype) for a in arrays],
        in_specs=[HBM_SPEC] * n + [SEM_SPEC, SEM_SPEC, ANY_SPEC], out_specs=[HBM_SPEC] * n,
        input_output_aliases={i: i for i in range(n)},
        compiler_params=pltpu.CompilerParams(has_side_effects=pltpu.SideEffectType.DATAFLOW_SIDE_EFFECTING),
    )(*arrays, send_s, recv_s, after))


def _gather_copies(shapes):
    n = len(shapes)

    def make(refs, send_s, recv_s):
        x, y, c = _mesh_pos()
        my = 2 * x + y
        copies = []
        for w in range(n):
            for k, (px, py) in enumerate(_other_chips(x, y)):
                def part(slot, w=w):
                    return refs[w].at[slot, _half_rows(shapes[w][1], c), :]
                sems = dict(send_sem=send_s.at[k * n + w], recv_sem=recv_s.at[k * n + w],
                            device_id=(px, py, c), device_id_type=MESH)
                copies.append((pltpu.make_async_remote_copy(src_ref=part(my), dst_ref=part(my), **sems),
                               pltpu.make_async_remote_copy(src_ref=part(2 * px + py), dst_ref=part(2 * px + py), **sems)))
        return copies

    return make


def _forward_copies(shapes):
    n = len(shapes)

    def make(refs, send_s, recv_s):
        x, y, c = _mesh_pos()
        copies = []
        for w in range(n):
            for k, (px, py) in enumerate(_other_chips(x, y)):
                def part(cc, w=w, slot=2 * px + py):
                    return refs[w].at[slot, _half_rows(shapes[w][1], cc), :]
                sems = dict(send_sem=send_s.at[k * n + w], recv_sem=recv_s.at[k * n + w],
                            device_id=(x, y, 1 - c), device_id_type=MESH)
                copies.append((pltpu.make_async_remote_copy(src_ref=part(c), dst_ref=part(c), **sems),
                               pltpu.make_async_remote_copy(src_ref=part(1 - c), dst_ref=part(1 - c), **sems)))
        return copies

    return make


def _pair_forward(bufs):
    n = len(bufs)
    make = _forward_copies([b.shape for b in bufs])

    def body(*refs):
        copies = make(refs[n:2 * n], refs[2 * n], refs[2 * n + 1])
        for send, _ in copies:
            send.start()
        for _, landed in copies:
            landed.wait_recv()
        for send, _ in copies:
            send.wait_send()

    return list(pl.pallas_call(
        body, name="ag_pair_forward", in_specs=[HBM_SPEC] * n, out_specs=[HBM_SPEC] * n,
        out_shape=[jax.ShapeDtypeStruct(b.shape, b.dtype) for b in bufs],
        input_output_aliases={w: w for w in range(n)},
        scratch_shapes=[pltpu.SemaphoreType.DMA((3 * n,)), pltpu.SemaphoreType.DMA((3 * n,))],
    )(*bufs))


def _pair_exchange_copies(shapes):
    n = len(shapes)

    def make(refs, send_s, recv_s):
        x, y, c = _mesh_pos()
        copies = []
        for t in range(n):
            sems = dict(send_sem=send_s.at[t], recv_sem=recv_s.at[t], device_id=(x, y, 1 - c), device_id_type=MESH)
            land = refs[n + t]
            copies.append((pltpu.make_async_remote_copy(
                src_ref=refs[t].at[:, _half_rows(shapes[t][1], 1 - c), :], dst_ref=land, **sems),
                pltpu.make_async_remote_copy(src_ref=land, dst_ref=land, **sems)))
        return copies

    return make


def _pair_share_copies(shapes):
    n = len(shapes)

    def make(refs, send_s, recv_s):
        x, y, c = _mesh_pos()
        copies = []
        for t in range(n):
            def half(cc, t=t):
                return refs[t].at[_half_rows(shapes[t][0], cc), :]
            sems = dict(send_sem=send_s.at[t], recv_sem=recv_s.at[t], device_id=(x, y, 1 - c), device_id_type=MESH)
            copies.append((pltpu.make_async_remote_copy(src_ref=half(c), dst_ref=half(c), **sems),
                           pltpu.make_async_remote_copy(src_ref=half(1 - c), dst_ref=half(1 - c), **sems)))
        return copies

    return make


def _rs_add_pair(grad, recv):
    n_slot, hr, C = recv.shape
    tm = _pick_tm(hr, 192)
    nb = hr // tm

    def body(a_ref, b_ref, o_ref):
        o_ref[...] = (a_ref[...].astype(F32) + b_ref[...].astype(F32)).astype(BF16)

    blk = pl.BlockSpec((n_slot, tm, C), lambda i: (0, i, 0))
    return pl.pallas_call(
        body, name="rs_add_pair", grid=(nb,),
        in_specs=[pl.BlockSpec((n_slot, tm, C), lambda i: (0, lax.axis_index("c") * nb + i, 0)), blk],
        out_specs=blk, out_shape=jax.ShapeDtypeStruct(recv.shape, BF16), compiler_params=_params(1))(grad, recv)


def _scatter_copies(n):
    def make(refs, send_s, recv_s):
        x, y, c = _mesh_pos()
        copies = []
        for t in range(n):
            for k, (px, py) in enumerate(_other_chips(x, y)):
                sems = dict(send_sem=send_s.at[3 * t + k], recv_sem=recv_s.at[3 * t + k],
                            device_id=(px, py, c), device_id_type=MESH)
                land = refs[n + t].at[k]
                copies.append((pltpu.make_async_remote_copy(src_ref=refs[t].at[2 * px + py], dst_ref=land, **sems),
                               pltpu.make_async_remote_copy(src_ref=land, dst_ref=land, **sems)))
        return copies

    return make


def _rs_add_chips(part, recv):
    _, hr, C = part.shape
    tm = _pick_tm(hr, 256)
    nb = hr // tm

    def body(a_ref, r0, r1, r2, o_ref):
        o_ref[...] = ((a_ref[...].astype(F32) + r0[...].astype(F32)) + r1[...].astype(F32)) + r2[...].astype(F32)

    def rel(k):
        return pl.BlockSpec((None, tm, C), lambda i: (k, i, 0))

    return pl.pallas_call(
        body, name="rs_add_chips", grid=(nb,),
        in_specs=[pl.BlockSpec((None, tm, C), lambda i: (_my_chip(), i, 0)), rel(0), rel(1), rel(2)],
        out_specs=pl.BlockSpec((tm, C), lambda i: (lax.axis_index("c") * nb + i, 0)),
        out_shape=jax.ShapeDtypeStruct((2 * hr, C), F32), compiler_params=_params(1))(part, recv, recv, recv)


def _allreduce_small(v):
    rows = v.shape[0]

    def body(v_ref, o_ref, buf, send_s, recv_s):
        x, y, c = _mesh_pos()
        me = 4 * x + 2 * y + c
        buf[me] = v_ref[...]
        copies = []
        for r in range(1, N_DEV):
            px = 1 - x if r & 4 else x
            py = 1 - y if r & 2 else y
            pc = 1 - c if r & 1 else c
            send = pltpu.make_async_remote_copy(
                src_ref=v_ref, dst_ref=buf.at[me], send_sem=send_s.at[r - 1], recv_sem=recv_s.at[r - 1],
                device_id=(px, py, pc), device_id_type=MESH)
            peer_slot = buf.at[4 * px + 2 * py + pc]
            landed = pltpu.make_async_remote_copy(
                src_ref=peer_slot, dst_ref=peer_slot, send_sem=send_s.at[r - 1], recv_sem=recv_s.at[r - 1],
                device_id=(px, py, pc), device_id_type=MESH)
            copies.append((send, landed))
        for send, _ in copies:
            send.start()
        for _, landed in copies:
            landed.wait_recv()
        for send, _ in copies:
            send.wait_send()
        acc = buf[0]
        for j in range(1, N_DEV):
            acc = acc + buf[j]
        o_ref[...] = acc

    vm = pl.BlockSpec(memory_space=pltpu.VMEM)
    return pl.pallas_call(
        body, name="allreduce_small", in_specs=[vm], out_specs=vm,
        out_shape=jax.ShapeDtypeStruct((rows, LANES), F32),
        scratch_shapes=[pltpu.VMEM((N_DEV, rows, LANES), F32), pltpu.SemaphoreType.DMA((N_DEV - 1,)),
                        pltpu.SemaphoreType.DMA((N_DEV - 1,))],
    )(v)


def _adamw_fn(w, g, m, v):
    m2 = ADAM_B1 * m + (1.0 - ADAM_B1) * g
    v2 = ADAM_B2 * v + (1.0 - ADAM_B2) * (g * g)
    m_hat = m2 / (1.0 - ADAM_B1 ** ADAM_STEP)
    v_hat = v2 / (1.0 - ADAM_B2 ** ADAM_STEP)
    delta = -ADAM_LR * (m_hat / (jnp.sqrt(v_hat) + ADAM_EPS) + ADAM_WD * w)
    return g, delta, m2, v2


def _adamw_layer(w, g, m, v, l, prev, after):
    NL, R, C = w.shape
    tm = _pick_tm(R, 128)
    n_prev = 0 if prev is None else 4

    def body(w_ref, g_ref, m_ref, v_ref, after_ref, *rest):
        outs = rest[n_prev:]
        for o_ref, val in zip(outs, _adamw_fn(w_ref[...], g_ref[...], m_ref[...], v_ref[...])):
            o_ref[...] = val

    lay = pl.BlockSpec((None, tm, C), lambda i: (l, i, 0))
    shape = jax.ShapeDtypeStruct((NL, R, C), F32)
    return pl.pallas_call(
        body, name="adamw", grid=(R // tm,),
        in_specs=[lay, pl.BlockSpec((tm, C), lambda i: (i, 0)), lay, lay, ANY_SPEC] + [ANY_SPEC] * n_prev,
        out_specs=[lay] * 4, out_shape=[shape] * 4,
        input_output_aliases={5 + j: j for j in range(n_prev)},
        compiler_params=_params(1))(w, g, m, v, after, *(prev or []))


def _pack_small(parts):
    out = []
    for a in parts:
        flat = a.reshape(-1)
        n = -(-flat.shape[0] // (8 * LANES)) * 8 * LANES
        out.append(jnp.pad(flat, (0, n - flat.shape[0])).reshape(-1, LANES))
    return jnp.concatenate(out, axis=0)


def _unpack_small(packed, like):
    out, r = [], 0
    for a in like:
        size = int(np.prod(a.shape))
        rows = -(-size // (8 * LANES)) * 8
        out.append(packed[r:r + rows].reshape(-1)[:size].reshape(a.shape))
        r += rows
    return out


def _ffn_forward(x, g, win, wout):
    h, silu, dgate, act = _ffn_up(x, g, win)
    if callable(wout):
        wout = wout(act)
    return _mm_res("ffn_down", x, [act], wout, 0.5), (x, h, silu, dgate, act)


def _ffn_backward(dx, saved, g, win, wout, after, mid=None):
    x, h, silu, dgate, act = saved
    D = x.shape[1]
    wc = win.shape[2]
    dgu = _ffn_bwd_a(dx, wout, silu, dgate, after)
    after_b = dgu if mid is None else mid(dgu)
    dwout = _mm_tn("dw_ffn_out", act, wc, dx, D, 2, True, False, 0.5)
    dwin = _mm_tn_pairs("dw_ffn_in", h, dgu, wc, 4)
    dx_in, dg = _bwd_into_norm("ffn_bwd_b", dgu, wc, win, 4, 2, x, g, dx, after_b)
    return dx_in, dg, dwin, dwout.reshape(N_CHIPS, -1, D)


GROUP_FFN1 = ("ffn1_w_in", "ffn1_w_out")
GROUP_REST = ("w_qkv", "w_o", "ffn2_w_in", "ffn2_w_out", "w_ple_gate", "w_ple_proj")
GATHER_L0 = (("a", ("ffn1_w_in",)), ("b", ("ffn1_w_out",)), ("c", ("w_qkv", "w_o")),
             ("d", ("ffn2_w_in", "ffn2_w_out", "w_ple_gate", "w_ple_proj")))


def _gather_start(tag, slotted, after):
    return _split_start("ag_start_" + tag, slotted, _gather_copies([a.shape for a in slotted]), 3 * len(slotted), after)


def _gather_finish(tag, started, after):
    send_s, recv_s, arrays, _ = started
    return _pair_forward(_split_wait("ag_wait_" + tag, send_s, recv_s, arrays,
                                     _gather_copies([a.shape for a in arrays]), after))


def _scatter_exchange(tag, grads):
    n = len(grads)
    land = [lax.empty((g_.shape[0], g_.shape[1] // 2, g_.shape[2]), g_.dtype) for g_ in grads]
    return _split_start("rs_px_start_" + tag, list(grads) + land, _pair_exchange_copies([g_.shape for g_ in grads]),
                        n, None)


def _scatter_chips(tag, started, after):
    send_s, recv_s, arrays, _ = started
    n = len(arrays) // 2
    arrays = _split_wait("rs_px_wait_" + tag, send_s, recv_s, arrays,
                         _pair_exchange_copies([a.shape for a in arrays[:n]]), after)
    part = [_rs_add_pair(g_, r_) for g_, r_ in zip(arrays[:n], arrays[n:])]
    land = [lax.empty((3,) + p_.shape[1:], p_.dtype) for p_ in part]
    return _split_start("rs_start_" + tag, part + land, _scatter_copies(n), 3 * n, None)


def _scatter_share(tag, started, after):
    send_s, recv_s, arrays, _ = started
    n = len(arrays) // 2
    arrays = _split_wait("rs_wait_" + tag, send_s, recv_s, arrays, _scatter_copies(n), after)
    halves = [_rs_add_chips(p_, r_) for p_, r_ in zip(arrays[:n], arrays[n:])]
    return _split_start("rs_ps_start_" + tag, halves, _pair_share_copies([h_.shape for h_ in halves]), n, None)


def _scatter_done(tag, started, after):
    send_s, recv_s, arrays, _ = started
    return _split_wait("rs_ps_wait_" + tag, send_s, recv_s, arrays, _pair_share_copies([a.shape for a in arrays]), after)


def kernel(x, p, rel_bias, norm_ffn1, ffn1_w_in, ffn1_w_out, norm_mix, w_qkv, q_norm_a, k_norm_a, q_norm_b, k_norm_b, sink_b, w_o, norm_ffn2, ffn2_w_in, ffn2_w_out, norm_ple, w_ple_gate, w_ple_proj, loss_target, m_rel_bias, m_norm_ffn1, m_ffn1_w_in, m_ffn1_w_out, m_norm_mix, m_w_qkv, m_q_norm_a, m_k_norm_a, m_q_norm_b, m_k_norm_b, m_sink_b, m_w_o, m_norm_ffn2, m_ffn2_w_in, m_ffn2_w_out, m_norm_ple, m_w_ple_gate, m_w_ple_proj, v_rel_bias, v_norm_ffn1, v_ffn1_w_in, v_ffn1_w_out, v_norm_mix, v_w_qkv, v_q_norm_a, v_k_norm_a, v_q_norm_b, v_k_norm_b, v_sink_b, v_w_o, v_norm_ffn2, v_ffn2_w_in, v_ffn2_w_out, v_norm_ple, v_w_ple_gate, v_w_ple_proj):
    given = dict(locals())
    T, D = x.shape[1], x.shape[2]
    NL = norm_ffn1.shape[0]
    x0 = x.reshape(T, D)
    tgt = loss_target.reshape(T, D)
    n_a, n_b, n_kv = _qkv_layout(D)

    assert NL == 2
    slot = [{name: _cast_into_slot(given[name], l) for name in BIG} for l in range(NL)]
    ag, token = {}, None
    for tag, names in GATHER_L0:
        ag[tag] = _gather_start(tag, [slot[0][n] for n in names], token)
        token = ag[tag][3]
    ag_1 = _gather_start("1", [slot[1][n] for n in BIG], token)

    def arrived(tag, after):
        return dict(zip(dict(GATHER_L0)[tag], _gather_finish(tag, ag[tag], after)))

    def by_rows(a):
        return a.reshape(-1, a.shape[-1])

    def by_cols(a):
        return a.transpose(1, 0, 2).reshape(a.shape[1], -1)

    QW = N_CHIPS * w_qkv.shape[2]

    dils = tuple(d for _, d in DILATED_CONFIGS)
    cfg_a = [(w // (2 * d), d) for w, d in DILATED_CONFIGS]
    pad_a, pad_b = _window_pad(cfg_a[0][0]), _window_pad(SWA_RADIUS)
    bmaps_a, bmaps_b = jnp.asarray(_bucket_maps(cfg_a)), jnp.asarray(_bucket_maps([(SWA_RADIUS, 1)]))
    n_heads = rel_bias.shape[1] // 2
    bias_a = _edge_variants(_bias_build(rel_bias, bmaps_a, 0), pad_a)
    bias_b = _edge_variants(_bias_build(rel_bias, bmaps_b, n_heads), pad_b)
    no_sink = jnp.full((n_heads,), NEG, F32)

    def gains_row(l):
        ones = jnp.ones((n_a * LANES,), F32)
        return jnp.concatenate([
            jnp.tile(q_norm_a[l], 2 * n_a), jnp.tile(k_norm_a[l], 2 * n_a), ones,
            jnp.tile(q_norm_b[l], 2 * n_b), jnp.tile(k_norm_b[l], 2 * n_kv), jnp.ones((n_kv * LANES,), F32)]).reshape(1, QW)

    saved, weights = [], []
    xc = x0
    pf_1 = None
    for l in range(NL):
        s, w = {}, {}
        if l == 0:
            w.update(arrived("a", ag_1[3]))

            def ffn1_w_out(act, w=w):
                w.update(arrived("b", act))
                w["ffn1_w_out"] = by_rows(w["ffn1_w_out"])
                return w["ffn1_w_out"]
        else:
            shapes = [a.shape for a in pf_1[2]]
            w.update(zip(BIG, _split_wait("ag_pf_wait_1", pf_1[0], pf_1[1], pf_1[2], _forward_copies(shapes), xc)))
            ffn1_w_out = w["ffn1_w_out"] = by_rows(w["ffn1_w_out"])
        xc, s["ffn1"] = _ffn_forward(xc, norm_ffn1[l:l + 1], w["ffn1_w_in"], ffn1_w_out)
        s["x1"] = xc
        if l == 0:
            w.update(arrived("c", xc))
        w["w_qkv"] = by_cols(w["w_qkv"])
        w["w_o"] = by_rows(w["w_o"])
        s["h2"], raw = _norm_proj("qkv_proj", xc, norm_mix[l:l + 1], w["w_qkv"])
        s["raw"] = raw
        s["qkv_a"], s["qkv_b"] = _qkv_post(raw, gains_row(l))
        s["o_a"], s["lse_a"] = _attn_fwd(s["qkv_a"], bias_a, no_sink, dils, pad_a)
        s["o_b"], s["lse_b"] = _attn_fwd(s["qkv_b"], bias_b, sink_b[l], (1,), pad_b)
        xc = _mm_res("attn_out", xc, [s["o_a"], s["o_b"]], w["w_o"], 1.0)
        if l == 0:
            w.update(arrived("d", xc))
        w["w_ple_proj"] = by_cols(w["w_ple_proj"])
        for name in ("ffn2_w_out", "w_ple_gate"):
            w[name] = by_rows(w[name])
        xc, s["ffn2"] = _ffn_forward(xc, norm_ffn2[l:l + 1], w["ffn2_w_in"], w["ffn2_w_out"])
        s["x3"] = xc
        if l == 0:
            landed = _split_wait("ag_wait_1", ag_1[0], ag_1[1], ag_1[2], _gather_copies([a.shape for a in ag_1[2]]), xc)
            pf_1 = _split_start("ag_pf_start_1", landed, _forward_copies([a.shape for a in landed]), 3 * len(landed),
                                None)
        s["p"] = p[l].reshape(T, -1)
        s["hn"], xc, s["gate"], s["pp"] = _ple_fwd(xc, norm_ple[l:l + 1], s["p"], w["w_ple_gate"], w["w_ple_proj"],
                                                   pf_1[3] if l == 0 else xc)
        saved.append(s)
        weights.append(w)

    dx, loss_blk = _loss_fwd_bwd(xc, tgt)
    loss = lax.psum(loss_blk[0, 0], ("x", "y", "c"))

    gs = {name: [None] * NL for name in SMALL if name != "rel_bias"}
    dt_a, dt_b = [], []

    def layer_backward(l, dx, hooks):
        def at(point, ready, *more):
            return hooks[point](ready, *more) if point in hooks else ready

        s, w, gw = saved[l], weights[l], {}
        dx, gs["norm_ple"][l], dwg, dwp = _ple_bwd(dx, s["gate"], s["pp"], s["hn"], s["p"], s["x3"],
                                                   norm_ple[l:l + 1], w["w_ple_gate"], at("start", dx))
        gw["w_ple_gate"] = dwg.reshape(N_CHIPS, -1, D)
        gw["w_ple_proj"] = dwp.reshape(dwp.shape[0], N_CHIPS, -1).transpose(1, 0, 2)
        dx, gs["norm_ffn2"][l], gw["ffn2_w_in"], gw["ffn2_w_out"] = _ffn_backward(
            dx, s["ffn2"], norm_ffn2[l:l + 1], w["ffn2_w_in"], w["ffn2_w_out"], at("after_ple", dx))
        do, dd = _mix_bwd_in(dx, w["w_o"], [s["o_a"], s["o_b"]])
        hwa = s["o_a"].shape[1]
        gw["w_o"] = jnp.concatenate([
            _mm_tn("dw_o", o_, o_.shape[1], dx, D, 1, False, False, 1.0).reshape(-1, D // N_CHIPS, D)
            for o_ in (s["o_a"], s["o_b"])], axis=0)
        dqa, dka, dva, dt, _ = _attn_bwd(s["qkv_a"], bias_a, no_sink, dils, pad_a, do, s["lse_a"], dd, 0, do)
        dt_a.append(dt)
        dqb, dkb, dvb, dt, dsink = _attn_bwd(s["qkv_b"], bias_b, sink_b[l], (1,), pad_b, do, s["lse_b"], dd,
                                             hwa // LANES, at("after_attn_a", dqa))
        dt_b.append(dt)
        gs["sink_b"][l] = dsink[:, 0, ::HEAD_DIM].reshape(-1)
        draw, dgains = _qkv_post_bwd(s["raw"], gains_row(l), (dqa, dka, dva), (dqb, dkb, dvb))
        dgv = dgains.reshape(-1, HEAD_DIM)
        gs["q_norm_a"][l] = dgv[:2 * n_a].sum(0)
        gs["k_norm_a"][l] = dgv[2 * n_a:4 * n_a].sum(0)
        gs["q_norm_b"][l] = dgv[6 * n_a:6 * n_a + 2 * n_b].sum(0)
        gs["k_norm_b"][l] = dgv[6 * n_a + 2 * n_b:6 * n_a + 2 * n_b + 2 * n_kv].sum(0)
        dwqkv = _mm_tn("dw_qkv", s["h2"], D, draw, QW, 1, False, False, 1.0)
        gw["w_qkv"] = dwqkv.reshape(D, N_CHIPS, -1).transpose(1, 0, 2)
        dx, gs["norm_mix"][l] = _bwd_into_norm("qkv_bwd_b", draw, QW, w["w_qkv"], 1, 1, s["x1"], norm_mix[l:l + 1], dx,
                                               draw)
        dx, gs["norm_ffn1"][l], gw["ffn1_w_in"], gw["ffn1_w_out"] = _ffn_backward(
            dx, s["ffn1"], norm_ffn1[l:l + 1], w["ffn1_w_in"], w["ffn1_w_out"],
            at("before_ffn1", dx, [gw[n] for n in GROUP_REST]), hooks.get("in_ffn1"))
        return dx, gw

    out = {}

    def adamw_group(names, l, grads, after):
        for name, g_ in zip(names, grads):
            out[name] = _adamw_layer(given[name], g_, given["m_" + name], given["v_" + name], l, out.get(name), after)
        return out[names[-1]][0]

    dx, gw1 = layer_backward(NL - 1, dx, {})
    px_1 = _scatter_exchange("1", [gw1[n] for n in BIG])
    rs = {}

    def chips_1(ready):
        rs["chips_1"] = _scatter_chips("1", px_1, ready)
        return rs["chips_1"][3]

    def share_1(ready):
        rs["share_1"] = _scatter_share("1", rs["chips_1"], ready)
        return rs["share_1"][3]

    def exchange_0a(ready, grads):
        rs["px_0a"] = _scatter_exchange("0a", grads)
        return rs["px_0a"][3]

    def chips_0a(ready):
        rs["chips_0a"] = _scatter_chips("0a", rs["px_0a"], ready)
        return rs["chips_0a"][3]

    dx, gw0 = layer_backward(0, dx, {"start": lambda ready: px_1[3], "after_ple": chips_1, "after_attn_a": share_1,
                                     "before_ffn1": exchange_0a, "in_ffn1": chips_0a})
    grad_x = dx.reshape(x.shape)

    px_0b = _scatter_exchange("0b", [gw0[n] for n in GROUP_FFN1])
    share_0a = _scatter_share("0a", rs["chips_0a"], px_0b[3])
    g_1 = _scatter_done("1", rs["share_1"], share_0a[3])
    chips_0b = _scatter_chips("0b", px_0b, g_1[0])
    ready = adamw_group(BIG, 1, g_1, chips_0b[3])
    ready = adamw_group(GROUP_REST, 0, _scatter_done("0a", share_0a, ready), ready)
    share_0b = _scatter_share("0b", chips_0b, ready)

    d_rel_bias = (_bias_grad(dt_a, bmaps_a, 0) + _bias_grad(dt_b, bmaps_b, n_heads))[:, :rel_bias.shape[1]]
    small_g = [d_rel_bias] + [jnp.stack([t.reshape(-1) for t in gs[name]]) for name in SMALL[1:]]
    g_sum = _allreduce_small(_pack_small(small_g))
    res = _ew("adamw_small", _adamw_fn,
              [_pack_small([given[n] for n in SMALL]), g_sum, _pack_small([given["m_" + n] for n in SMALL]),
               _pack_small([given["v_" + n] for n in SMALL])], [(LANES, F32)] * 4)
    like = [given[n] for n in SMALL]
    unpacked = [_unpack_small(r, like) for r in res]
    for i, name in enumerate(SMALL):
        out[name] = [u[i] for u in unpacked]

    adamw_group(GROUP_FFN1, 0, _scatter_done("0b", share_0b, res[0]), res[0])

    return (loss, grad_x, *[out[n][0] for n in WEIGHTS], *[out[n][1] for n in WEIGHTS],
            *[out[n][2] for n in WEIGHTS], *[out[n][3] for n in WEIGHTS])
```

```python
import functools
import math

import numpy as np
import jax
import jax.numpy as jnp
from jax import lax
from jax.experimental import pallas as pl
from jax.experimental.pallas import tpu as pltpu

F32 = jnp.float32
BF16 = jnp.bfloat16
MESH = pl.DeviceIdType.MESH

HEAD_DIM = 64
LANES = 128
QBLOCK = 128
FWD_UNROLL, BWD_UNROLL = 4, 4
N_BUCKETS = 32
MAX_DISTANCE = 1024
DILATED_CONFIGS = ((128, 1), (512, 4), (2048, 16))
SWA_RADIUS = 128
GROUP_B = 4
EPS = 1e-6
NEG = -1e30
Q_SCALE = HEAD_DIM ** -0.5
ADAM_LR, ADAM_B1, ADAM_B2, ADAM_EPS, ADAM_WD, ADAM_STEP = 0.001, 0.9, 0.999, 1e-08, 0.01, 10
VMEM_LIMIT = 56 * 2 ** 20
N_CHIPS = 4
N_DEV = 8

BIG = ("ffn1_w_in", "ffn1_w_out", "w_qkv", "w_o", "ffn2_w_in", "ffn2_w_out", "w_ple_gate", "w_ple_proj")
SMALL = ("rel_bias", "norm_ffn1", "norm_mix", "q_norm_a", "k_norm_a", "q_norm_b", "k_norm_b", "sink_b",
         "norm_ffn2", "norm_ple")
WEIGHTS = ("rel_bias", "norm_ffn1", "ffn1_w_in", "ffn1_w_out", "norm_mix", "w_qkv", "q_norm_a", "k_norm_a",
           "q_norm_b", "k_norm_b", "sink_b", "w_o", "norm_ffn2", "ffn2_w_in", "ffn2_w_out", "norm_ple",
           "w_ple_gate", "w_ple_proj")


HBM_SPEC = pl.BlockSpec(memory_space=pltpu.HBM)
ANY_SPEC = pl.BlockSpec(memory_space=pl.ANY)
SEM_SPEC = pl.BlockSpec(memory_space=pltpu.SEMAPHORE)


def _params(n_grid):
    return pltpu.CompilerParams(dimension_semantics=("arbitrary",) * n_grid, vmem_limit_bytes=VMEM_LIMIT)


def _pick_tm(rows, cap):
    t = (min(cap, rows) // 16) * 16
    while t >= 16:
        if rows % t == 0:
            return t
        t -= 16
    return rows


def _dot(a, b):
    return jnp.dot(a, b, preferred_element_type=F32)


def _dot_nt(a, b):
    return lax.dot_general(a, b, (((1,), (1,)), ((), ())), preferred_element_type=F32)


def _dot_tn(a, b):
    return lax.dot_general(a, b, (((0,), (0,)), ((), ())), preferred_element_type=F32)


def _sigmoid(z):
    return 1.0 / (1.0 + jnp.exp(-z))


def _lo_lanes(shape):
    return lax.broadcasted_iota(jnp.int32, shape, len(shape) - 1) % LANES < HEAD_DIM


def _seg_sum(blk):
    lo = _lo_lanes(blk.shape)
    s_lo = jnp.sum(jnp.where(lo, blk, 0.0), axis=1, keepdims=True)
    s_hi = jnp.sum(jnp.where(lo, 0.0, blk), axis=1, keepdims=True)
    return jnp.where(lo, s_lo, s_hi)


def _rms_bwd_tile(x, g, dh):
    r = lax.rsqrt(jnp.mean(x * x, axis=-1, keepdims=True) + EPS)
    xh = x * r
    dyg = dh * g
    dx = r * (dyg - xh * jnp.mean(dyg * xh, axis=-1, keepdims=True))
    return dx, jnp.sum(dh * xh, axis=0, keepdims=True)


def _ew(name, fn, ins, out_defs, cap=512):
    rows = ins[0].shape[0]
    tm = _pick_tm(rows, cap)
    n_in = len(ins)

    def body(*refs):
        vals = fn(*[r[...] for r in refs[:n_in]])
        if not isinstance(vals, tuple):
            vals = (vals,)
        for r, v in zip(refs[n_in:], vals):
            r[...] = v.astype(r.dtype)

    return pl.pallas_call(
        body, name=name, grid=(rows // tm,),
        in_specs=[pl.BlockSpec((tm, a.shape[1]), lambda i: (i, 0)) for a in ins],
        out_specs=[pl.BlockSpec((tm, c), lambda i: (i, 0)) for c, _ in out_defs],
        out_shape=[jax.ShapeDtypeStruct((rows, c), dt) for c, dt in out_defs],
        compiler_params=_params(1))(*ins)


def _rms_tile(xv, gv):
    r = lax.rsqrt(jnp.mean(xv * xv, axis=-1, keepdims=True) + EPS)
    return (xv * r * gv).astype(BF16)


def _ffn_up(x, g, win):
    T, D = x.shape
    wc = win.shape[2]
    tm = _pick_tm(T, 512)

    def body(x_ref, g_ref, wg_ref, wu_ref, h_ref, silu_ref, dgate_ref, act_ref, wcat):
        @pl.when(pl.program_id(1) == 0)
        def _():
            wcat[:, :wc] = wg_ref[...]
            wcat[:, wc:] = wu_ref[...]

        hv = _rms_tile(x_ref[...], g_ref[...])
        h_ref[...] = hv
        gu = _dot(hv, wcat[...])
        gte, u = gu[:, :wc], gu[:, wc:]
        sg = _sigmoid(gte)
        silu = gte * sg
        silu_ref[...] = silu.astype(BF16)
        dgate_ref[...] = ((sg + silu * (1.0 - sg)) * u).astype(BF16)
        act_ref[...] = (silu * u).astype(BF16)

    out = jax.ShapeDtypeStruct((T, 2 * wc), BF16)
    ospec = pl.BlockSpec((tm, wc), lambda j, i: (i, j))
    nt = T // tm
    h_spec = pl.BlockSpec((tm, D), lambda j, i: (jnp.where(j == 0, i, nt), 0))
    return pl.pallas_call(
        body, name="ffn_up", grid=(2, nt),
        in_specs=[pl.BlockSpec((tm, D), lambda j, i: (i, 0)), pl.BlockSpec((1, D), lambda j, i: (0, 0)),
                  pl.BlockSpec((None, D, wc), lambda j, i: (j, 0, 0)),
                  pl.BlockSpec((None, D, wc), lambda j, i: (j + 2, 0, 0))],
        out_specs=[h_spec] + [ospec] * 3, out_shape=[jax.ShapeDtypeStruct((T + tm, D), BF16)] + [out] * 3,
        scratch_shapes=[pltpu.VMEM((D, 2 * wc), BF16)], compiler_params=_params(2))(x, g, win, win)


def _mm_res(name, res, a_list, w, scale):
    T, N = res.shape
    n = len(a_list)
    widths = [a.shape[1] for a in a_list]
    tm = _pick_tm(T, 512)

    def body(*refs):
        r_ref, a_refs, w_refs, o_ref = refs[0], refs[1:1 + n], refs[1 + n:1 + 2 * n], refs[1 + 2 * n]
        acc = _dot(a_refs[0][...].astype(BF16), w_refs[0][...])
        for a_ref, w_ref in zip(a_refs[1:], w_refs[1:]):
            acc = acc + _dot(a_ref[...].astype(BF16), w_ref[...])
        o_ref[...] = r_ref[...] + scale * acc

    w_specs, off = [], 0
    for k in widths:
        w_specs.append(pl.BlockSpec((k, N), lambda i, blk=off // k: (blk, 0)))
        off += k
    return pl.pallas_call(
        body, name=name, grid=(T // tm,),
        in_specs=[pl.BlockSpec((tm, N), lambda i: (i, 0))]
        + [pl.BlockSpec((tm, k), lambda i: (i, 0)) for k in widths] + w_specs,
        out_specs=pl.BlockSpec((tm, N), lambda i: (i, 0)),
        out_shape=jax.ShapeDtypeStruct((T, N), F32), compiler_params=_params(1))(res, *a_list, *([w] * n))


def _norm_proj(name, x, g, w):
    T, K = x.shape
    N = w.shape[1]
    tm = _pick_tm(T, 512)

    def body(x_ref, g_ref, w_ref, h_ref, o_ref):
        hv = _rms_tile(x_ref[...], g_ref[...])
        h_ref[...] = hv
        o_ref[...] = _dot(hv, w_ref[...])

    row = pl.BlockSpec((tm, K), lambda i: (i, 0))
    return pl.pallas_call(
        body, name=name, grid=(T // tm,),
        in_specs=[row, pl.BlockSpec((1, K), lambda i: (0, 0)), pl.BlockSpec((K, N), lambda i: (0, 0))],
        out_specs=[row, pl.BlockSpec((tm, N), lambda i: (i, 0))],
        out_shape=[jax.ShapeDtypeStruct((T, K), BF16), jax.ShapeDtypeStruct((T, N), F32)],
        compiler_params=_params(1))(x, g, w)


def _mm_tn(name, a, a_w, b, b_w, n_slots, a_by_slot, b_by_slot, scale, tm_cap=512):
    T = b.shape[0]
    tm = _pick_tm(T, tm_cap)
    nt = T // tm

    def body(a_ref, b_ref, o_ref, acc):
        i = pl.program_id(1)

        @pl.when(i == 0)
        def _():
            acc[...] = jnp.zeros_like(acc)

        acc[...] += _dot_tn(a_ref[...].astype(BF16), b_ref[...].astype(BF16))

        @pl.when(i == nt - 1)
        def _():
            o_ref[...] = (acc[...] * scale).astype(BF16)

    return pl.pallas_call(
        body, name=name, grid=(n_slots, nt),
        in_specs=[pl.BlockSpec((tm, a_w), (lambda s, i: (i, s)) if a_by_slot else (lambda s, i: (i, 0))),
                  pl.BlockSpec((tm, b_w), (lambda s, i: (i, s)) if b_by_slot else (lambda s, i: (i, 0)))],
        out_specs=pl.BlockSpec((None, a_w, b_w), lambda s, i: (s, 0, 0)),
        out_shape=jax.ShapeDtypeStruct((n_slots, a_w, b_w), BF16),
        scratch_shapes=[pltpu.VMEM((a_w, b_w), F32)], compiler_params=_params(2))(a, b)


def _mm_tn_pairs(name, a, b, b_w, n_slots):
    T = b.shape[0]
    a_w = a.shape[1]
    tm = _pick_tm(T, 512)
    nt = T // tm

    def body(a_ref, b_ref, o_ref, acc):
        i = pl.program_id(1)

        @pl.when(i == 0)
        def _():
            acc[...] = jnp.zeros_like(acc)

        acc[...] += _dot_tn(a_ref[...], b_ref[...])

        @pl.when(i == nt - 1)
        def _():
            o_ref[0] = acc[:, :b_w].astype(BF16)
            o_ref[1] = acc[:, b_w:].astype(BF16)

    return pl.pallas_call(
        body, name=name, grid=(n_slots // 2, nt),
        in_specs=[pl.BlockSpec((tm, a_w), lambda s, i: (i, 0)), pl.BlockSpec((tm, 2 * b_w), lambda s, i: (i, s))],
        out_specs=pl.BlockSpec((2, a_w, b_w), lambda s, i: (s, 0, 0)),
        out_shape=jax.ShapeDtypeStruct((n_slots, a_w, b_w), BF16),
        scratch_shapes=[pltpu.VMEM((a_w, 2 * b_w), F32)], compiler_params=_params(2))(a, b)


def _ffn_bwd_a(dx, wout, silu, dgate, after):
    T, D = dx.shape
    F = silu.shape[1]
    tm = _pick_tm(T, 256)

    def body(dx_ref, w_ref, s_ref, dg_ref, after_ref, o_ref):
        dact = 0.5 * _dot_nt(dx_ref[...].astype(BF16), w_ref[...])
        o_ref[:, :F] = (dact * dg_ref[...].astype(F32)).astype(BF16)
        o_ref[:, F:] = (dact * s_ref[...].astype(F32)).astype(BF16)

    act_spec = pl.BlockSpec((tm, F), lambda i: (i, 0))
    return pl.pallas_call(
        body, name="ffn_bwd_a", grid=(T // tm,),
        in_specs=[pl.BlockSpec((tm, D), lambda i: (i, 0)), pl.BlockSpec((F, D), lambda i: (0, 0)),
                  act_spec, act_spec, ANY_SPEC],
        out_specs=pl.BlockSpec((tm, 2 * F), lambda i: (i, 0)),
        out_shape=jax.ShapeDtypeStruct((T, 2 * F), BF16),
        compiler_params=_params(1))(dx, wout, silu, dgate, after)


def _bwd_into_norm(name, d, d_w, w, n_slots, group, x, g, dx_in, after):
    T, D = x.shape
    tm = _pick_tm(T, 256)
    n_mm = n_slots // group
    k_w = group * d_w

    def body(*refs):
        d_refs, w_hbm = refs[:n_mm], refs[n_mm]
        x_ref, g_ref, dxi_ref, _, dx_ref, dg_ref, wcat, sem = refs[n_mm + 1:]

        @pl.when(pl.program_id(0) == 0)
        def _():
            copies = [pltpu.make_async_copy(w_hbm.at[s] if w.ndim == 3 else w_hbm,
                                            wcat.at[s // group, :, pl.ds((s % group) * d_w, d_w)], sem.at[s])
                      for s in range(n_slots)]
            for cp in copies:
                cp.start()
            for cp in copies:
                cp.wait()
            dg_ref[...] = jnp.zeros_like(dg_ref)

        dh = _dot_nt(d_refs[0][...], wcat[0])
        for m in range(1, n_mm):
            dh = dh + _dot_nt(d_refs[m][...], wcat[m])
        dxn, dg = _rms_bwd_tile(x_ref[...], g_ref[...], dh)
        dx_ref[...] = dxi_ref[...] + dxn
        dg_ref[...] += dg

    row = pl.BlockSpec((tm, D), lambda i: (i, 0))
    vec = pl.BlockSpec((1, D), lambda i: (0, 0))
    return pl.pallas_call(
        body, name=name, grid=(T // tm,),
        in_specs=[pl.BlockSpec((tm, k_w), lambda i, m=m: (i, m)) for m in range(n_mm)]
        + [ANY_SPEC, row, vec, row, ANY_SPEC],
        out_specs=[row, vec],
        out_shape=[jax.ShapeDtypeStruct((T, D), F32), jax.ShapeDtypeStruct((1, D), F32)],
        scratch_shapes=[pltpu.VMEM((n_mm, D, k_w), BF16), pltpu.SemaphoreType.DMA((n_slots,))],
        compiler_params=_params(1))(*([d] * n_mm + [w, x, g, dx_in, after]))


def _ple_fwd(x, g, p, wg, wp, after):
    T, D = x.shape
    P = p.shape[1]
    tm = _pick_tm(T, 256)

    def body(x_ref, g_ref, p_ref, wg_ref, wp_ref, after_ref, hn_ref, xo_ref, gate_ref, pp_ref):
        xv = x_ref[...]
        hn = _rms_tile(xv, g_ref[...])
        hn_ref[...] = hn
        gate = _sigmoid(_dot(hn, wg_ref[...]))
        pp = _dot(p_ref[...].astype(BF16), wp_ref[...])
        gate_ref[...] = gate
        pp_ref[...] = pp
        xo_ref[...] = xv + gate * pp

    row = pl.BlockSpec((tm, D), lambda i: (i, 0))
    out = jax.ShapeDtypeStruct((T, D), F32)
    return pl.pallas_call(
        body, name="ple_fwd", grid=(T // tm,),
        in_specs=[row, pl.BlockSpec((1, D), lambda i: (0, 0)), pl.BlockSpec((tm, P), lambda i: (i, 0)),
                  pl.BlockSpec((D, D), lambda i: (0, 0)), pl.BlockSpec((P, D), lambda i: (0, 0)), ANY_SPEC],
        out_specs=[row, row, row, row], out_shape=[jax.ShapeDtypeStruct((T, D), BF16), out, out, out],
        compiler_params=_params(1))(x, g, p, wg, wp, after)


def _ple_bwd(dx, gate, pp, hn, p, x, g, wg, after):
    T, D = x.shape
    P = p.shape[1]
    tm = _pick_tm(T, 256)
    nt = T // tm

    def body(dx_ref, gate_ref, pp_ref, hn_ref, p_ref, x_ref, g_ref, wg_ref, after_ref,
             dxo_ref, dg_ref, dwg_ref, dwp_ref, acc_g, acc_p):
        i = pl.program_id(0)

        @pl.when(i == 0)
        def _():
            acc_g[...] = jnp.zeros_like(acc_g)
            acc_p[...] = jnp.zeros_like(acc_p)
            dg_ref[...] = jnp.zeros_like(dg_ref)

        dxv = dx_ref[...]
        gate = gate_ref[...]
        dz = (dxv * pp_ref[...] * gate * (1.0 - gate)).astype(BF16)
        dpp = (dxv * gate).astype(BF16)
        acc_g[...] += _dot_tn(hn_ref[...], dz)
        acc_p[...] += _dot_tn(p_ref[...].astype(BF16), dpp)
        dxn, dg = _rms_bwd_tile(x_ref[...], g_ref[...], _dot_nt(dz, wg_ref[...]))
        dxo_ref[...] = dxv + dxn
        dg_ref[...] += dg

        @pl.when(i == nt - 1)
        def _():
            dwg_ref[...] = acc_g[...].astype(BF16)
            dwp_ref[...] = acc_p[...].astype(BF16)

    row = pl.BlockSpec((tm, D), lambda i: (i, 0))
    vec = pl.BlockSpec((1, D), lambda i: (0, 0))
    return pl.pallas_call(
        body, name="ple_bwd", grid=(nt,),
        in_specs=[row, row, row, row, pl.BlockSpec((tm, P), lambda i: (i, 0)), row, vec,
                  pl.BlockSpec((D, D), lambda i: (0, 0)), ANY_SPEC],
        out_specs=[row, vec, pl.BlockSpec((D, D), lambda i: (0, 0)), pl.BlockSpec((P, D), lambda i: (0, 0))],
        out_shape=[jax.ShapeDtypeStruct((T, D), F32), jax.ShapeDtypeStruct((1, D), F32),
                   jax.ShapeDtypeStruct((D, D), BF16), jax.ShapeDtypeStruct((P, D), BF16)],
        scratch_shapes=[pltpu.VMEM((D, D), F32), pltpu.VMEM((P, D), F32)],
        compiler_params=_params(1))(dx, gate, pp, hn, p, x, g, wg, after)


def _loss_fwd_bwd(y, tgt):
    T, D = y.shape
    tm = _pick_tm(T, 512)

    def body(y_ref, t_ref, dy_ref, loss_ref):
        e = y_ref[...] - t_ref[...]
        dy_ref[...] = e / D

        @pl.when(pl.program_id(0) == 0)
        def _():
            loss_ref[...] = jnp.zeros_like(loss_ref)

        loss_ref[...] += 0.5 * jnp.sum(jnp.mean(e * e, axis=-1, keepdims=True), axis=0, keepdims=True)

    row = pl.BlockSpec((tm, D), lambda i: (i, 0))
    return pl.pallas_call(
        body, name="loss", grid=(T // tm,), in_specs=[row, row],
        out_specs=[row, pl.BlockSpec((8, LANES), lambda i: (0, 0))],
        out_shape=[jax.ShapeDtypeStruct((T, D), F32), jax.ShapeDtypeStruct((8, LANES), F32)],
        compiler_params=_params(1))(y, tgt)


def _qkv_layout(D):
    n_a = D // (2 * LANES)
    n_b = D // (2 * LANES)
    n_kv = max(1, (2 * n_b) // GROUP_B) * HEAD_DIM // LANES
    return n_a, n_b, n_kv


def _dup_half(xv, half):
    rolled = pltpu.roll(xv, HEAD_DIM, 1)
    lo = _lo_lanes(xv.shape)
    return jnp.where(lo, xv, rolled) if half == 0 else jnp.where(lo, rolled, xv)


def _qkv_post(raw, gains):
    T, W = raw.shape
    n_a, n_b, n_kv = _qkv_layout(W * 4 // 9)
    tm = _pick_tm(T, 256)
    o_qb = 3 * n_a

    def norm(xv, gv, scale):
        ms = _seg_sum(xv * xv) * (1.0 / HEAD_DIM)
        return xv * lax.rsqrt(ms + EPS) * gv * scale

    def body(raw_ref, g_ref, a_ref, b_ref):
        def blk(cb):
            return raw_ref[:, cb * LANES:(cb + 1) * LANES]

        def gn(cb):
            return g_ref[:, cb * LANES:(cb + 1) * LANES]

        for cb in range(n_a):
            a_ref[:, cb * LANES:(cb + 1) * LANES] = norm(blk(cb), gn(cb), Q_SCALE)
            cbk = n_a + cb
            a_ref[:, cbk * LANES:(cbk + 1) * LANES] = norm(blk(cbk), gn(cbk), 1.0)
            cbv = 2 * n_a + cb
            a_ref[:, cbv * LANES:(cbv + 1) * LANES] = blk(cbv)
        for cb in range(n_b):
            src = o_qb + cb
            b_ref[:, cb * LANES:(cb + 1) * LANES] = norm(blk(src), gn(src), Q_SCALE)
        for e in range(n_b):
            kvh = (2 * e) // GROUP_B
            ck = o_qb + n_b + kvh // 2
            cv = ck + n_kv
            kn = norm(blk(ck), gn(ck), 1.0)
            b_ref[:, (n_b + e) * LANES:(n_b + e + 1) * LANES] = _dup_half(kn, kvh % 2)
            b_ref[:, (2 * n_b + e) * LANES:(2 * n_b + e + 1) * LANES] = _dup_half(blk(cv), kvh % 2)

    wa, wb = 3 * n_a * LANES, 3 * n_b * LANES
    return pl.pallas_call(
        body, name="qkv_post", grid=(T // tm,),
        in_specs=[pl.BlockSpec((tm, W), lambda i: (i, 0)), pl.BlockSpec((1, W), lambda i: (0, 0))],
        out_specs=[pl.BlockSpec((tm, wa), lambda i: (i, 0)), pl.BlockSpec((tm, wb), lambda i: (i, 0))],
        out_shape=[jax.ShapeDtypeStruct((T, wa), F32), jax.ShapeDtypeStruct((T, wb), F32)],
        compiler_params=_params(1))(raw, gains)


def _qkv_post_bwd(raw, gains, d_a, d_b):
    T, W = raw.shape
    n_a, n_b, n_kv = _qkv_layout(W * 4 // 9)
    tm = _pick_tm(T, 256)
    o_qb = 3 * n_a

    def body(raw_ref, g_ref, daq, dak, dav, dbq, dbk, dbv, o_ref, dg_ref):
        @pl.when(pl.program_id(0) == 0)
        def _():
            dg_ref[...] = jnp.zeros_like(dg_ref)

        def cols(ref, cb):
            return ref[:, cb * LANES:(cb + 1) * LANES]

        def norm_bwd(cb, dy, scale):
            xv = cols(raw_ref, cb)
            gv = cols(g_ref, cb)
            r = lax.rsqrt(_seg_sum(xv * xv) * (1.0 / HEAD_DIM) + EPS)
            xh = xv * r
            dys = dy * scale
            dyg = dys * gv
            dxv = r * (dyg - xh * (_seg_sum(dyg * xh) * (1.0 / HEAD_DIM)))
            o_ref[:, cb * LANES:(cb + 1) * LANES] = dxv.astype(BF16)
            dg_ref[:, cb * LANES:(cb + 1) * LANES] += jnp.sum(dys * xh, axis=0, keepdims=True)

        def fold(ref, kv_blk):
            halves = []
            for half in range(2):
                kvh = 2 * kv_blk + half
                blocks = [e for e in range(n_b) if (2 * e) // GROUP_B == kvh]
                s = cols(ref, blocks[0])
                for e in blocks[1:]:
                    s = s + cols(ref, e)
                halves.append(s + pltpu.roll(s, HEAD_DIM, 1))
            return jnp.where(_lo_lanes(halves[0].shape), halves[0], halves[1])

        for cb in range(n_a):
            norm_bwd(cb, cols(daq, cb), Q_SCALE)
            norm_bwd(n_a + cb, cols(dak, cb), 1.0)
            cbv = 2 * n_a + cb
            o_ref[:, cbv * LANES:(cbv + 1) * LANES] = cols(dav, cb).astype(BF16)
        for cb in range(n_b):
            norm_bwd(o_qb + cb, cols(dbq, cb), Q_SCALE)
        for kb in range(n_kv):
            ck = o_qb + n_b + kb
            cv = ck + n_kv
            norm_bwd(ck, fold(dbk, kb), 1.0)
            o_ref[:, cv * LANES:(cv + 1) * LANES] = fold(dbv, kb).astype(BF16)

    hw_a, hw_b = n_a * LANES, n_b * LANES
    return pl.pallas_call(
        body, name="qkv_post_bwd", grid=(T // tm,),
        in_specs=[pl.BlockSpec((tm, W), lambda i: (i, 0)), pl.BlockSpec((1, W), lambda i: (0, 0))]
        + [pl.BlockSpec((tm, hw_a), lambda i: (i, 0))] * 3 + [pl.BlockSpec((tm, hw_b), lambda i: (i, 0))] * 3,
        out_specs=[pl.BlockSpec((tm, W), lambda i: (i, 0)), pl.BlockSpec((1, W), lambda i: (0, 0))],
        out_shape=[jax.ShapeDtypeStruct((T, W), BF16), jax.ShapeDtypeStruct((1, W), F32)],
        compiler_params=_params(1))(raw, gains, *d_a, *d_b)


def _t5_bucket_np(rel):
    half = N_BUCKETS // 2
    max_exact = half // 2
    ret = np.where(rel > 0, half, 0)
    n = np.abs(rel)
    nf = np.maximum(n, 1).astype(np.float32)
    large = max_exact + (np.log(nf / np.float32(max_exact)) / np.float32(math.log(MAX_DISTANCE / max_exact))
                         * np.float32(half - max_exact)).astype(np.int32)
    large = np.minimum(large, half - 1)
    return ret + np.where(n < max_exact, n, large)


def _window_pad(radius):
    assert radius <= QBLOCK
    return HEAD_DIM if radius <= HEAD_DIM else QBLOCK


def _bucket_maps(configs):
    pad = _window_pad(configs[0][0])
    q = np.arange(QBLOCK)[:, None]
    kk = np.arange(QBLOCK + 2 * pad)[None, :]
    rel = kk - pad - q
    maps = [np.where(np.abs(rel) <= radius, _t5_bucket_np(rel * dil), -1) for radius, dil in configs]
    return np.stack(maps).astype(np.int32)


def _bias_build(rel_bias, bmaps, col0):
    n_sets, _, W = bmaps.shape
    n_heads = rel_bias.shape[1] // 2

    def body(rb_ref, bm_ref, o_ref):
        h = pl.program_id(1)
        bm = bm_ref[...]

        def step(n, acc):
            return jnp.where(bm == n, rb_ref[n, col0 + h], acc)

        o_ref[...] = lax.fori_loop(0, N_BUCKETS, step, jnp.where(bm < 0, NEG, 0.0).astype(F32))

    return pl.pallas_call(
        body, name="bias_build", grid=(n_sets, n_heads),
        in_specs=[pl.BlockSpec(memory_space=pltpu.SMEM), pl.BlockSpec((None, QBLOCK, W), lambda s, h: (s, 0, 0))],
        out_specs=pl.BlockSpec((None, None, QBLOCK, W), lambda s, h: (s, h, 0, 0)),
        out_shape=jax.ShapeDtypeStruct((n_sets, n_heads, QBLOCK, W), F32),
        compiler_params=_params(2))(rel_bias, bmaps)


def _bias_grad(dtiles, bmaps, col0):
    n_sets, _, W = bmaps.shape
    n_heads = dtiles[0].shape[1]
    n_l = len(dtiles)

    def body(*refs):
        bm_ref, o_ref = refs[0], refs[1 + n_l]
        s, h = pl.program_id(0), pl.program_id(1)

        @pl.when((s == 0) & (h == 0))
        def _():
            o_ref[...] = jnp.zeros_like(o_ref)

        d = refs[1][...]
        for r in refs[2:1 + n_l]:
            d = d + r[...]
        acc8 = d[0:8, :]
        for a in range(1, QBLOCK // 8):
            acc8 = acc8 + pltpu.roll(d[8 * a:8 * a + 8, :], W - 8 * a, 1)
        per_offset = acc8[0:1, :]
        for b in range(1, 8):
            per_offset = per_offset + pltpu.roll(acc8[b:b + 1, :], W - b, 1)
        bucket = lax.broadcasted_iota(jnp.int32, (N_BUCKETS, W), 0)
        hit = bucket == bm_ref[0:1, :]
        per_bucket = jnp.sum(jnp.where(hit, per_offset, 0.0), axis=1, keepdims=True)
        lanes = lax.broadcasted_iota(jnp.int32, o_ref.shape, 1)
        o_ref[...] += jnp.where(lanes == col0 + h, per_bucket, 0.0)

    tile = pl.BlockSpec((None, None, QBLOCK, W), lambda s, h: (s, h, 0, 0))
    return pl.pallas_call(
        body, name="bias_grad", grid=(n_sets, n_heads),
        in_specs=[pl.BlockSpec((None, QBLOCK, W), lambda s, h: (s, 0, 0))] + [tile] * n_l,
        out_specs=pl.BlockSpec((N_BUCKETS, LANES), lambda s, h: (0, 0)),
        out_shape=jax.ShapeDtypeStruct((N_BUCKETS, LANES), F32), compiler_params=_params(2))(bmaps, *dtiles)


def _rows(l_start, n, d, r):
    if d == 1:
        return pl.ds(pl.multiple_of(l_start, 8), n)
    return pl.ds(l_start * d + r, n, stride=d)


def _stack_heads(xv, lo):
    z = jnp.zeros_like(xv)
    return jnp.concatenate([jnp.where(lo, xv, z), jnp.where(lo, z, xv)], axis=0)


def _unstack_heads(xv, lo):
    return jnp.where(lo, xv[:QBLOCK], xv[QBLOCK:])


def _per_head_rows(v0, v1):
    if jnp.ndim(v0) == 0:
        return jnp.where(lax.broadcasted_iota(jnp.int32, (2 * QBLOCK, 1), 0) < QBLOCK, v0, v1)
    return jnp.concatenate([v0, v1], axis=0)


def _block_geometry(b, nb_sub, pad):
    r, lb = b // nb_sub, b % nb_sub
    l0 = lb * QBLOCK
    lp = jnp.maximum(l0 - pad, 0)
    ln = jnp.minimum(l0 + QBLOCK, nb_sub * QBLOCK - pad)
    return r, l0, lp, ln, (lb == 0).astype(jnp.int32) + 2 * (lb == nb_sub - 1).astype(jnp.int32)


def _edge_variants(bias, pad):
    n_br, _, _, W = bias.shape
    col = np.arange(W)
    left, right = col < pad, col >= pad + QBLOCK
    masked = jnp.asarray(np.stack([np.zeros(W, bool), left, right, left | right]))
    return jnp.where(masked[None, :, None, :], NEG, bias.reshape(n_br, 1, -1, W))


def _window(ref, l0, lp, ln, pad, d, r):
    return jnp.concatenate([ref[_rows(lp, pad, d, r), :], ref[_rows(l0, QBLOCK, d, r), :],
                            ref[_rows(ln, pad, d, r), :]], axis=0)


def _attn_fwd(qkv, bias, sink, dils, pad):
    T = qkv.shape[0]
    hw = qkv.shape[1] // 3
    ng = hw // LANES
    n_br = len(dils)
    n_blocks = T // QBLOCK
    W = QBLOCK + 2 * pad
    chunk = 256

    def body(sink_ref, q_ref, k_ref, v_ref, bias_ref, o_ref, lse_ref, *scratch):
        g = pl.program_id(0)
        lo = _lo_lanes((QBLOCK, LANES))
        snk = _per_head_rows(sink_ref[2 * g], sink_ref[2 * g + 1])
        for c, d in enumerate(dils):
            nb_sub = n_blocks // d
            o_dst = scratch[0].at[c] if n_br > 1 else o_ref
            l_dst = scratch[1].at[c] if n_br > 1 else lse_ref

            def block(b, carry, c=c, d=d, nb_sub=nb_sub, o_dst=o_dst, l_dst=l_dst):
                r, l0, lp, ln, edge = _block_geometry(b, nb_sub, pad)
                q = _stack_heads(q_ref[_rows(l0, QBLOCK, d, r), :].astype(BF16), lo)
                k = _window(k_ref, l0, lp, ln, pad, d, r).astype(BF16)
                v = _window(v_ref, l0, lp, ln, pad, d, r).astype(BF16)
                s = _dot_nt(q, k) + bias_ref[c, edge]
                m = jnp.maximum(jnp.max(s, axis=1, keepdims=True), snk)
                p = jnp.exp(s - m)
                den = jnp.sum(p, axis=1, keepdims=True) + jnp.exp(snk - m)
                o_dst[_rows(l0, QBLOCK, d, r), :] = _unstack_heads(_dot(p.astype(BF16), v) / den, lo)
                l_dst[_rows(l0, QBLOCK, d, r), :] = _unstack_heads(
                    jnp.broadcast_to(m + jnp.log(den), (2 * QBLOCK, LANES)), lo)
                return carry

            lax.fori_loop(0, n_blocks, block, 0, unroll=FWD_UNROLL)

        if n_br > 1:
            def merge(i, carry):
                rs = pl.ds(pl.multiple_of(i * chunk, chunk), chunk)
                ls = [scratch[1][c, rs, :] for c in range(n_br)]
                m = ls[0]
                for t in ls[1:]:
                    m = jnp.maximum(m, t)
                ws = [jnp.exp(t - m) for t in ls]
                z = ws[0]
                acc = ws[0] * scratch[0][0, rs, :]
                for c in range(1, n_br):
                    z = z + ws[c]
                    acc = acc + ws[c] * scratch[0][c, rs, :]
                o_ref[rs, :] = acc / z
                lse_ref[rs, :] = m + jnp.log(z)
                return carry

            lax.fori_loop(0, T // chunk, merge, 0)

    def col(base):
        return pl.BlockSpec((T, LANES), lambda g: (0, base + g))

    out = jax.ShapeDtypeStruct((T, hw), F32)
    scratch = [pltpu.VMEM((n_br, T, LANES), F32)] * 2 if n_br > 1 else []
    return pl.pallas_call(
        body, name="attn_fwd", grid=(ng,),
        in_specs=[pl.BlockSpec(memory_space=pltpu.SMEM), col(0), col(ng), col(2 * ng),
                  pl.BlockSpec((n_br, 4, 2 * QBLOCK, W), lambda g: (0, 0, g, 0))],
        out_specs=[col(0), col(0)], out_shape=[out, out], scratch_shapes=scratch,
        compiler_params=_params(1))(sink, qkv, qkv, qkv, bias)


def _attn_bwd(qkv, bias, sink, dils, pad, do, lse, dd, col_base, after):
    T = qkv.shape[0]
    hw = qkv.shape[1] // 3
    ng = hw // LANES
    n_br = len(dils)
    n_blocks = T // QBLOCK
    W = QBLOCK + 2 * pad

    def body(sink_ref, q_ref, k_ref, v_ref, bias_ref, do_ref, lse_ref, dd_ref, after_ref,
             dq_ref, dk_ref, dv_ref, dt_ref, ds_ref):
        g = pl.program_id(0)
        dq_ref[...] = jnp.zeros_like(dq_ref)
        dk_ref[...] = jnp.zeros_like(dk_ref)
        dv_ref[...] = jnp.zeros_like(dv_ref)
        dt_ref[...] = jnp.zeros_like(dt_ref)
        ds_ref[...] = jnp.zeros_like(ds_ref)
        lo = _lo_lanes((QBLOCK, LANES))
        snk = jnp.where(lo, sink_ref[2 * g], sink_ref[2 * g + 1])
        for c, d in enumerate(dils):
            nb_sub = n_blocks // d

            def block(b, carry, c=c, d=d, nb_sub=nb_sub):
                r, l0, lp, ln, edge = _block_geometry(b, nb_sub, pad)
                rows_q = _rows(l0, QBLOCK, d, r)
                q = _stack_heads(q_ref[rows_q, :].astype(BF16), lo)
                k = _window(k_ref, l0, lp, ln, pad, d, r).astype(BF16)
                v = _window(v_ref, l0, lp, ln, pad, d, r).astype(BF16)
                dob = _stack_heads(do_ref[rows_q, :].astype(BF16), lo)
                lse_b = lse_ref[rows_q, :]
                dd_b = dd_ref[rows_q, :]
                s = _dot_nt(q, k) + bias_ref[c, edge]
                p = jnp.exp(s - _per_head_rows(lse_b[:, 0:1], lse_b[:, HEAD_DIM:HEAD_DIM + 1]))
                ds = p * (_dot_nt(dob, v) - _per_head_rows(dd_b[:, 0:1], dd_b[:, HEAD_DIM:HEAD_DIM + 1]))
                dsb = ds.astype(BF16)
                dkw = _dot_tn(dsb, q)
                dvw = _dot_tn(p.astype(BF16), dob)
                dt_ref[c] += ds
                dq_ref[rows_q, :] += _unstack_heads(_dot(dsb, k), lo)
                ds_ref[0:1, :] += jnp.sum(-jnp.exp(snk - lse_b) * dd_b, axis=0, keepdims=True)
                for part, (start, n) in zip((0, pad, pad + QBLOCK), ((lp, pad), (l0, QBLOCK), (ln, pad))):
                    dk_ref[_rows(start, n, d, r), :] += dkw[part:part + n]
                    dv_ref[_rows(start, n, d, r), :] += dvw[part:part + n]
                return carry

            lax.fori_loop(0, n_blocks, block, 0, unroll=BWD_UNROLL)

    def col(base):
        return pl.BlockSpec((T, LANES), lambda g: (0, base + g))

    tile = pl.BlockSpec((n_br, 2 * QBLOCK, W), lambda g: (0, g, 0))
    full = jax.ShapeDtypeStruct((T, hw), F32)
    dq, dk, dv, dt, dsink = pl.pallas_call(
        body, name="attn_bwd", grid=(ng,),
        in_specs=[pl.BlockSpec(memory_space=pltpu.SMEM), col(0), col(ng), col(2 * ng),
                  pl.BlockSpec((n_br, 4, 2 * QBLOCK, W), lambda g: (0, 0, g, 0)),
                  col(col_base), col(0), col(col_base), ANY_SPEC],
        out_specs=[col(0), col(0), col(0), tile, pl.BlockSpec((None, 8, LANES), lambda g: (g, 0, 0))],
        out_shape=[full, full, full, jax.ShapeDtypeStruct((n_br, 2 * ng * QBLOCK, W), F32),
                   jax.ShapeDtypeStruct((ng, 8, LANES), F32)],
        compiler_params=_params(1))(sink, qkv, qkv, qkv, bias, do, lse, dd, after)
    return dq, dk, dv, dt.reshape(n_br, 2 * ng, QBLOCK, W), dsink


def _mix_bwd_in(dx, wo, o_list):
    T, D = dx.shape
    widths = [o.shape[1] for o in o_list]
    hw = sum(widths)
    n = len(o_list)
    tm = _pick_tm(T, 256)

    def body(*refs):
        dx_ref, w_ref, o_refs, do_ref, dd_ref = refs[0], refs[1], refs[2:2 + n], refs[2 + n], refs[3 + n]
        dov = _dot_nt(dx_ref[...].astype(BF16), w_ref[...])
        do_ref[...] = dov
        off = 0
        for o_ref, k in zip(o_refs, widths):
            prod = dov[:, off:off + k] * o_ref[...]
            for cb in range(k // LANES):
                dd_ref[:, off + cb * LANES:off + (cb + 1) * LANES] = _seg_sum(prod[:, cb * LANES:(cb + 1) * LANES])
            off += k

    row = pl.BlockSpec((tm, hw), lambda i: (i, 0))
    out = jax.ShapeDtypeStruct((T, hw), F32)
    return pl.pallas_call(
        body, name="mix_bwd_in", grid=(T // tm,),
        in_specs=[pl.BlockSpec((tm, D), lambda i: (i, 0)), pl.BlockSpec((hw, D), lambda i: (0, 0))]
        + [pl.BlockSpec((tm, k), lambda i: (i, 0)) for k in widths],
        out_specs=[row, row], out_shape=[out, out], compiler_params=_params(1))(dx, wo, *o_list)


def _mesh_pos():
    return lax.axis_index("x"), lax.axis_index("y"), lax.axis_index("c")


def _my_chip():
    return 2 * lax.axis_index("x") + lax.axis_index("y")


def _other_chips(x, y):
    return [(1 - x, y), (x, 1 - y), (1 - x, 1 - y)]


def _half_rows(rows, cc):
    hr = rows // 2
    return pl.ds(pl.multiple_of(cc * hr, 16), hr)


def _cast_into_slot(w, l):
    _, R, C = w.shape
    tm = _pick_tm(R, 512)

    def body(w_ref, o_ref):
        o_ref[...] = w_ref[...].astype(BF16)

    return pl.pallas_call(
        body, name="cast_into_slot", grid=(R // tm,),
        in_specs=[pl.BlockSpec((None, tm, C), lambda i: (l, i, 0))],
        out_specs=pl.BlockSpec((None, tm, C), lambda i: (_my_chip(), i, 0)),
        out_shape=jax.ShapeDtypeStruct((N_CHIPS, R, C), BF16), compiler_params=_params(1))(w)


def _split_start(name, arrays, make_copies, n_sem, after):
    n = len(arrays)
    n_in = n + (0 if after is None else 1)

    def body(*refs):
        send_s, recv_s = refs[n_in], refs[n_in + 1]
        token = refs[n_in + 2 + n]
        for send, _ in make_copies(refs[:n], send_s, recv_s):
            send.start()
        token[...] = jnp.zeros_like(token)

    res = pl.pallas_call(
        body, name=name,
        out_shape=(pltpu.SemaphoreType.DMA((n_sem,)), pltpu.SemaphoreType.DMA((n_sem,)),
                   *[pltpu.HBM(a.shape, a.dtype) for a in arrays], jax.ShapeDtypeStruct((8, LANES), F32)),
        in_specs=[HBM_SPEC] * n + [ANY_SPEC] * (n_in - n),
        out_specs=(SEM_SPEC, SEM_SPEC, *([HBM_SPEC] * n), pl.BlockSpec(memory_space=pltpu.VMEM)),
        input_output_aliases={i: 2 + i for i in range(n)},
        compiler_params=pltpu.CompilerParams(has_side_effects=pltpu.SideEffectType.DATAFLOW_SIDE_EFFECTING),
    )(*[pltpu.with_memory_space_constraint(a, pltpu.HBM) for a in arrays], *([] if after is None else [after]))
    return res[0], res[1], list(res[2:2 + n]), res[2 + n]


def _split_wait(name, send_s, recv_s, arrays, make_copies, after):
    n = len(arrays)

    def body(*refs):
        for send, landed in make_copies(refs[:n], refs[n], refs[n + 1]):
            send.wait_send()
            landed.wait_recv()

    return list(pl.pallas_call(
        body, name=name, out_shape=[pltpu.HBM(a.shape, a.dtype) for a in arrays],
        in_specs=[HBM_SPEC] * n + [SEM_SPEC, SEM_SPEC, ANY_SPEC], out_specs=[HBM_SPEC] * n,
        input_output_aliases={i: i for i in range(n)},
        compiler_params=pltpu.CompilerParams(has_side_effects=pltpu.SideEffectType.DATAFLOW_SIDE_EFFECTING),
    )(*arrays, send_s, recv_s, after))


def _gather_copies(shapes):
    n = len(shapes)

    def make(refs, send_s, recv_s):
        x, y, c = _mesh_pos()
        my = 2 * x + y
        copies = []
        for w in range(n):
            for k, (px, py) in enumerate(_other_chips(x, y)):
                def part(slot, w=w):
                    return refs[w].at[slot, _half_rows(shapes[w][1], c), :]
                sems = dict(send_sem=send_s.at[k * n + w], recv_sem=recv_s.at[k * n + w],
                            device_id=(px, py, c), device_id_type=MESH)
                copies.append((pltpu.make_async_remote_copy(src_ref=part(my), dst_ref=part(my), **sems),
                               pltpu.make_async_remote_copy(src_ref=part(2 * px + py), dst_ref=part(2 * px + py), **sems)))
        return copies

    return make


def _forward_copies(shapes):
    n = len(shapes)

    def make(refs, send_s, recv_s):
        x, y, c = _mesh_pos()
        copies = []
        for w in range(n):
            for k, (px, py) in enumerate(_other_chips(x, y)):
                def part(cc, w=w, slot=2 * px + py):
                    return refs[w].at[slot, _half_rows(shapes[w][1], cc), :]
                sems = dict(send_sem=send_s.at[k * n + w], recv_sem=recv_s.at[k * n + w],
                            device_id=(x, y, 1 - c), device_id_type=MESH)
                copies.append((pltpu.make_async_remote_copy(src_ref=part(c), dst_ref=part(c), **sems),
                               pltpu.make_async_remote_copy(src_ref=part(1 - c), dst_ref=part(1 - c), **sems)))
        return copies

    return make


def _pair_forward(bufs):
    n = len(bufs)
    make = _forward_copies([b.shape for b in bufs])

    def body(*refs):
        copies = make(refs[n:2 * n], refs[2 * n], refs[2 * n + 1])
        for send, _ in copies:
            send.start()
        for _, landed in copies:
            landed.wait_recv()
        for send, _ in copies:
            send.wait_send()

    return list(pl.pallas_call(
        body, name="ag_pair_forward", in_specs=[HBM_SPEC] * n, out_specs=[HBM_SPEC] * n,
        out_shape=[jax.ShapeDtypeStruct(b.shape, b.dtype) for b in bufs],
        input_output_aliases={w: w for w in range(n)},
        scratch_shapes=[pltpu.SemaphoreType.DMA((3 * n,)), pltpu.SemaphoreType.DMA((3 * n,))],
    )(*bufs))


def _pair_exchange_copies(shapes):
    n = len(shapes)

    def make(refs, send_s, recv_s):
        x, y, c = _mesh_pos()
        copies = []
        for t in range(n):
            sems = dict(send_sem=send_s.at[t], recv_sem=recv_s.at[t], device_id=(x, y, 1 - c), device_id_type=MESH)
            land = refs[n + t]
            copies.append((pltpu.make_async_remote_copy(
                src_ref=refs[t].at[:, _half_rows(shapes[t][1], 1 - c), :], dst_ref=land, **sems),
                pltpu.make_async_remote_copy(src_ref=land, dst_ref=land, **sems)))
        return copies

    return make


def _pair_share_copies(shapes):
    n = len(shapes)

    def make(refs, send_s, recv_s):
        x, y, c = _mesh_pos()
        copies = []
        for t in range(n):
            def half(cc, t=t):
                return refs[t].at[_half_rows(shapes[t][0], cc), :]
            sems = dict(send_sem=send_s.at[t], recv_sem=recv_s.at[t], device_id=(x, y, 1 - c), device_id_type=MESH)
            copies.append((pltpu.make_async_remote_copy(src_ref=half(c), dst_ref=half(c), **sems),
                           pltpu.make_async_remote_copy(src_ref=half(1 - c), dst_ref=half(1 - c), **sems)))
        return copies

    return make


def _rs_add_pair(grad, recv):
    n_slot, hr, C = recv.shape
    tm = _pick_tm(hr, 192)
    nb = hr // tm

    def body(a_ref, b_ref, o_ref):
        o_ref[...] = (a_ref[...].astype(F32) + b_ref[...].astype(F32)).astype(BF16)

    blk = pl.BlockSpec((n_slot, tm, C), lambda i: (0, i, 0))
    return pl.pallas_call(
        body, name="rs_add_pair", grid=(nb,),
        in_specs=[pl.BlockSpec((n_slot, tm, C), lambda i: (0, lax.axis_index("c") * nb + i, 0)), blk],
        out_specs=blk, out_shape=jax.ShapeDtypeStruct(recv.shape, BF16), compiler_params=_params(1))(grad, recv)


def _scatter_copies(n):
    def make(refs, send_s, recv_s):
        x, y, c = _mesh_pos()
        copies = []
        for t in range(n):
            for k, (px, py) in enumerate(_other_chips(x, y)):
                sems = dict(send_sem=send_s.at[3 * t + k], recv_sem=recv_s.at[3 * t + k],
                            device_id=(px, py, c), device_id_type=MESH)
                land = refs[n + t].at[k]
                copies.append((pltpu.make_async_remote_copy(src_ref=refs[t].at[2 * px + py], dst_ref=land, **sems),
                               pltpu.make_async_remote_copy(src_ref=land, dst_ref=land, **sems)))
        return copies

    return make


def _rs_add_chips(part, recv):
    _, hr, C = part.shape
    tm = _pick_tm(hr, 256)
    nb = hr // tm

    def body(a_ref, r0, r1, r2, o_ref):
        o_ref[...] = ((a_ref[...].astype(F32) + r0[...].astype(F32)) + r1[...].astype(F32)) + r2[...].astype(F32)

    def rel(k):
        return pl.BlockSpec((None, tm, C), lambda i: (k, i, 0))

    return pl.pallas_call(
        body, name="rs_add_chips", grid=(nb,),
        in_specs=[pl.BlockSpec((None, tm, C), lambda i: (_my_chip(), i, 0)), rel(0), rel(1), rel(2)],
        out_specs=pl.BlockSpec((tm, C), lambda i: (lax.axis_index("c") * nb + i, 0)),
        out_shape=jax.ShapeDtypeStruct((2 * hr, C), F32), compiler_params=_params(1))(part, recv, recv, recv)


def _allreduce_small(v):
    rows = v.shape[0]

    def body(v_ref, o_ref, buf, send_s, recv_s):
        x, y, c = _mesh_pos()
        me = 4 * x + 2 * y + c
        buf[me] = v_ref[...]
        copies = []
        for r in range(1, N_DEV):
            px = 1 - x if r & 4 else x
            py = 1 - y if r & 2 else y
            pc = 1 - c if r & 1 else c
            send = pltpu.make_async_remote_copy(
                src_ref=v_ref, dst_ref=buf.at[me], send_sem=send_s.at[r - 1], recv_sem=recv_s.at[r - 1],
                device_id=(px, py, pc), device_id_type=MESH)
            peer_slot = buf.at[4 * px + 2 * py + pc]
            landed = pltpu.make_async_remote_copy(
                src_ref=peer_slot, dst_ref=peer_slot, send_sem=send_s.at[r - 1], recv_sem=recv_s.at[r - 1],
                device_id=(px, py, pc), device_id_type=MESH)
            copies.append((send, landed))
        for send, _ in copies:
            send.start()
        for _, landed in copies:
            landed.wait_recv()
        for send, _ in copies:
            send.wait_send()
        acc = buf[0]
        for j in range(1, N_DEV):
            acc = acc + buf[j]
        o_ref[...] = acc

    vm = pl.BlockSpec(memory_space=pltpu.VMEM)
    return pl.pallas_call(
        body, name="allreduce_small", in_specs=[vm], out_specs=vm,
        out_shape=jax.ShapeDtypeStruct((rows, LANES), F32),
        scratch_shapes=[pltpu.VMEM((N_DEV, rows, LANES), F32), pltpu.SemaphoreType.DMA((N_DEV - 1,)),
                        pltpu.SemaphoreType.DMA((N_DEV - 1,))],
    )(v)


def _adamw_fn(w, g, m, v):
    m2 = ADAM_B1 * m + (1.0 - ADAM_B1) * g
    v2 = ADAM_B2 * v + (1.0 - ADAM_B2) * (g * g)
    m_hat = m2 / (1.0 - ADAM_B1 ** ADAM_STEP)
    v_hat = v2 / (1.0 - ADAM_B2 ** ADAM_STEP)
    delta = -ADAM_LR * (m_hat / (jnp.sqrt(v_hat) + ADAM_EPS) + ADAM_WD * w)
    return g, delta, m2, v2


def _adamw_layer(w, g, m, v, l, prev, after):
    NL, R, C = w.shape
    tm = _pick_tm(R, 128)
    n_prev = 0 if prev is None else 4

    def body(w_ref, g_ref, m_ref, v_ref, after_ref, *rest):
        outs = rest[n_prev:]
        for o_ref, val in zip(outs, _adamw_fn(w_ref[...], g_ref[...], m_ref[...], v_ref[...])):
            o_ref[...] = val

    lay = pl.BlockSpec((None, tm, C), lambda i: (l, i, 0))
    shape = jax.ShapeDtypeStruct((NL, R, C), F32)
    return pl.pallas_call(
        body, name="adamw", grid=(R // tm,),
        in_specs=[lay, pl.BlockSpec((tm, C), lambda i: (i, 0)), lay, lay, ANY_SPEC] + [ANY_SPEC] * n_prev,
        out_specs=[lay] * 4, out_shape=[shape] * 4,
        input_output_aliases={5 + j: j for j in range(n_prev)},
        compiler_params=_params(1))(w, g, m, v, after, *(prev or []))


def _pack_small(parts):
    out = []
    for a in parts:
        flat = a.reshape(-1)
        n = -(-flat.shape[0] // (8 * LANES)) * 8 * LANES
        out.append(jnp.pad(flat, (0, n - flat.shape[0])).reshape(-1, LANES))
    return jnp.concatenate(out, axis=0)


def _unpack_small(packed, like):
    out, r = [], 0
    for a in like:
        size = int(np.prod(a.shape))
        rows = -(-size // (8 * LANES)) * 8
        out.append(packed[r:r + rows].reshape(-1)[:size].reshape(a.shape))
        r += rows
    return out


def _ffn_forward(x, g, win, wout):
    h, silu, dgate, act = _ffn_up(x, g, win)
    if callable(wout):
        wout = wout(act)
    return _mm_res("ffn_down", x, [act], wout, 0.5), (x, h, silu, dgate, act)


def _ffn_backward(dx, saved, g, win, wout, after, mid=None):
    x, h, silu, dgate, act = saved
    D = x.shape[1]
    wc = win.shape[2]
    dgu = _ffn_bwd_a(dx, wout, silu, dgate, after)
    after_b = dgu if mid is None else mid(dgu)
    dwout = _mm_tn("dw_ffn_out", act, wc, dx, D, 2, True, False, 0.5)
    dwin = _mm_tn_pairs("dw_ffn_in", h, dgu, wc, 4)
    dx_in, dg = _bwd_into_norm("ffn_bwd_b", dgu, wc, win, 4, 2, x, g, dx, after_b)
    return dx_in, dg, dwin, dwout.reshape(N_CHIPS, -1, D)


GROUP_FFN1 = ("ffn1_w_in", "ffn1_w_out")
GROUP_REST = ("w_qkv", "w_o", "ffn2_w_in", "ffn2_w_out", "w_ple_gate", "w_ple_proj")
GATHER_L0 = (("a", ("ffn1_w_in",)), ("b", ("ffn1_w_out",)), ("c", ("w_qkv", "w_o")),
             ("d", ("ffn2_w_in", "ffn2_w_out", "w_ple_gate", "w_ple_proj")))


def _gather_start(tag, slotted, after):
    return _split_start("ag_start_" + tag, slotted, _gather_copies([a.shape for a in slotted]), 3 * len(slotted), after)


def _gather_finish(tag, started, after):
    send_s, recv_s, arrays, _ = started
    return _pair_forward(_split_wait("ag_wait_" + tag, send_s, recv_s, arrays,
                                     _gather_copies([a.shape for a in arrays]), after))


def _scatter_exchange(tag, grads):
    n = len(grads)
    land = [lax.empty((g_.shape[0], g_.shape[1] // 2, g_.shape[2]), g_.dtype) for g_ in grads]
    return _split_start("rs_px_start_" + tag, list(grads) + land, _pair_exchange_copies([g_.shape for g_ in grads]),
                        n, None)


def _scatter_chips(tag, started, after):
    send_s, recv_s, arrays, _ = started
    n = len(arrays) // 2
    arrays = _split_wait("rs_px_wait_" + tag, send_s, recv_s, arrays,
                         _pair_exchange_copies([a.shape for a in arrays[:n]]), after)
    part = [_rs_add_pair(g_, r_) for g_, r_ in zip(arrays[:n], arrays[n:])]
    land = [lax.empty((3,) + p_.shape[1:], p_.dtype) for p_ in part]
    return _split_start("rs_start_" + tag, part + land, _scatter_copies(n), 3 * n, None)


def _scatter_share(tag, started, after):
    send_s, recv_s, arrays, _ = started
    n = len(arrays) // 2
    arrays = _split_wait("rs_wait_" + tag, send_s, recv_s, arrays, _scatter_copies(n), after)
    halves = [_rs_add_chips(p_, r_) for p_, r_ in zip(arrays[:n], arrays[n:])]
    return _split_start("rs_ps_start_" + tag, halves, _pair_share_copies([h_.shape for h_ in halves]), n, None)


def _scatter_done(tag, started, after):
    send_s, recv_s, arrays, _ = started
    return _split_wait("rs_ps_wait_" + tag, send_s, recv_s, arrays, _pair_share_copies([a.shape for a in arrays]), after)


def kernel(x, p, rel_bias, norm_ffn1, ffn1_w_in, ffn1_w_out, norm_mix, w_qkv, q_norm_a, k_norm_a, q_norm_b, k_norm_b, sink_b, w_o, norm_ffn2, ffn2_w_in, ffn2_w_out, norm_ple, w_ple_gate, w_ple_proj, loss_target, m_rel_bias, m_norm_ffn1, m_ffn1_w_in, m_ffn1_w_out, m_norm_mix, m_w_qkv, m_q_norm_a, m_k_norm_a, m_q_norm_b, m_k_norm_b, m_sink_b, m_w_o, m_norm_ffn2, m_ffn2_w_in, m_ffn2_w_out, m_norm_ple, m_w_ple_gate, m_w_ple_proj, v_rel_bias, v_norm_ffn1, v_ffn1_w_in, v_ffn1_w_out, v_norm_mix, v_w_qkv, v_q_norm_a, v_k_norm_a, v_q_norm_b, v_k_norm_b, v_sink_b, v_w_o, v_norm_ffn2, v_ffn2_w_in, v_ffn2_w_out, v_norm_ple, v_w_ple_gate, v_w_ple_proj):
    given = dict(locals())
    T, D = x.shape[1], x.shape[2]
    NL = norm_ffn1.shape[0]
    x0 = x.reshape(T, D)
    tgt = loss_target.reshape(T, D)
    n_a, n_b, n_kv = _qkv_layout(D)

    assert NL == 2
    slot = [{name: _cast_into_slot(given[name], l) for name in BIG} for l in range(NL)]
    ag, token = {}, None
    for tag, names in GATHER_L0:
        ag[tag] = _gather_start(tag, [slot[0][n] for n in names], token)
        token = ag[tag][3]
    ag_1 = _gather_start("1", [slot[1][n] for n in BIG], token)

    def arrived(tag, after):
        return dict(zip(dict(GATHER_L0)[tag], _gather_finish(tag, ag[tag], after)))

    def by_rows(a):
        return a.reshape(-1, a.shape[-1])

    def by_cols(a):
        return a.transpose(1, 0, 2).reshape(a.shape[1], -1)

    QW = N_CHIPS * w_qkv.shape[2]

    dils = tuple(d for _, d in DILATED_CONFIGS)
    cfg_a = [(w // (2 * d), d) for w, d in DILATED_CONFIGS]
    pad_a, pad_b = _window_pad(cfg_a[0][0]), _window_pad(SWA_RADIUS)
    bmaps_a, bmaps_b = jnp.asarray(_bucket_maps(cfg_a)), jnp.asarray(_bucket_maps([(SWA_RADIUS, 1)]))
    n_heads = rel_bias.shape[1] // 2
    bias_a = _edge_variants(_bias_build(rel_bias, bmaps_a, 0), pad_a)
    bias_b = _edge_variants(_bias_build(rel_bias, bmaps_b, n_heads), pad_b)
    no_sink = jnp.full((n_heads,), NEG, F32)

    def gains_row(l):
        ones = jnp.ones((n_a * LANES,), F32)
        return jnp.concatenate([
            jnp.tile(q_norm_a[l], 2 * n_a), jnp.tile(k_norm_a[l], 2 * n_a), ones,
            jnp.tile(q_norm_b[l], 2 * n_b), jnp.tile(k_norm_b[l], 2 * n_kv), jnp.ones((n_kv * LANES,), F32)]).reshape(1, QW)

    saved, weights = [], []
    xc = x0
    pf_1 = None
    for l in range(NL):
        s, w = {}, {}
        if l == 0:
            w.update(arrived("a", ag_1[3]))

            def ffn1_w_out(act, w=w):
                w.update(arrived("b", act))
                w["ffn1_w_out"] = by_rows(w["ffn1_w_out"])
                return w["ffn1_w_out"]
        else:
            shapes = [a.shape for a in pf_1[2]]
            w.update(zip(BIG, _split_wait("ag_pf_wait_1", pf_1[0], pf_1[1], pf_1[2], _forward_copies(shapes), xc)))
            ffn1_w_out = w["ffn1_w_out"] = by_rows(w["ffn1_w_out"])
        xc, s["ffn1"] = _ffn_forward(xc, norm_ffn1[l:l + 1], w["ffn1_w_in"], ffn1_w_out)
        s["x1"] = xc
        if l == 0:
            w.update(arrived("c", xc))
        w["w_qkv"] = by_cols(w["w_qkv"])
        w["w_o"] = by_rows(w["w_o"])
        s["h2"], raw = _norm_proj("qkv_proj", xc, norm_mix[l:l + 1], w["w_qkv"])
        s["raw"] = raw
        s["qkv_a"], s["qkv_b"] = _qkv_post(raw, gains_row(l))
        s["o_a"], s["lse_a"] = _attn_fwd(s["qkv_a"], bias_a, no_sink, dils, pad_a)
        s["o_b"], s["lse_b"] = _attn_fwd(s["qkv_b"], bias_b, sink_b[l], (1,), pad_b)
        xc = _mm_res("attn_out", xc, [s["o_a"], s["o_b"]], w["w_o"], 1.0)
        if l == 0:
            w.update(arrived("d", xc))
        w["w_ple_proj"] = by_cols(w["w_ple_proj"])
        for name in ("ffn2_w_out", "w_ple_gate"):
            w[name] = by_rows(w[name])
        xc, s["ffn2"] = _ffn_forward(xc, norm_ffn2[l:l + 1], w["ffn2_w_in"], w["ffn2_w_out"])
        s["x3"] = xc
        if l == 0:
            landed = _split_wait("ag_wait_1", ag_1[0], ag_1[1], ag_1[2], _gather_copies([a.shape for a in ag_1[2]]), xc)
            pf_1 = _split_start("ag_pf_start_1", landed, _forward_copies([a.shape for a in landed]), 3 * len(landed),
                                None)
        s["p"] = p[l].reshape(T, -1)
        s["hn"], xc, s["gate"], s["pp"] = _ple_fwd(xc, norm_ple[l:l + 1], s["p"], w["w_ple_gate"], w["w_ple_proj"],
                                                   pf_1[3] if l == 0 else xc)
        saved.append(s)
        weights.append(w)

    dx, loss_blk = _loss_fwd_bwd(xc, tgt)
    loss = lax.psum(loss_blk[0, 0], ("x", "y", "c"))

    gs = {name: [None] * NL for name in SMALL if name != "rel_bias"}
    dt_a, dt_b = [], []

    def layer_backward(l, dx, hooks):
        def at(point, ready, *more):
            return hooks[point](ready, *more) if point in hooks else ready

        s, w, gw = saved[l], weights[l], {}
        dx, gs["norm_ple"][l], dwg, dwp = _ple_bwd(dx, s["gate"], s["pp"], s["hn"], s["p"], s["x3"],
                                                   norm_ple[l:l + 1], w["w_ple_gate"], at("start", dx))
        gw["w_ple_gate"] = dwg.reshape(N_CHIPS, -1, D)
        gw["w_ple_proj"] = dwp.reshape(dwp.shape[0], N_CHIPS, -1).transpose(1, 0, 2)
        dx, gs["norm_ffn2"][l], gw["ffn2_w_in"], gw["ffn2_w_out"] = _ffn_backward(
            dx, s["ffn2"], norm_ffn2[l:l + 1], w["ffn2_w_in"], w["ffn2_w_out"], at("after_ple", dx))
        do, dd = _mix_bwd_in(dx, w["w_o"], [s["o_a"], s["o_b"]])
        hwa = s["o_a"].shape[1]
        gw["w_o"] = jnp.concatenate([
            _mm_tn("dw_o", o_, o_.shape[1], dx, D, 1, False, False, 1.0).reshape(-1, D // N_CHIPS, D)
            for o_ in (s["o_a"], s["o_b"])], axis=0)
        dqa, dka, dva, dt, _ = _attn_bwd(s["qkv_a"], bias_a, no_sink, dils, pad_a, do, s["lse_a"], dd, 0, do)
        dt_a.append(dt)
        dqb, dkb, dvb, dt, dsink = _attn_bwd(s["qkv_b"], bias_b, sink_b[l], (1,), pad_b, do, s["lse_b"], dd,
                                             hwa // LANES, at("after_attn_a", dqa))
        dt_b.append(dt)
        gs["sink_b"][l] = dsink[:, 0, ::HEAD_DIM].reshape(-1)
        draw, dgains = _qkv_post_bwd(s["raw"], gains_row(l), (dqa, dka, dva), (dqb, dkb, dvb))
        dgv = dgains.reshape(-1, HEAD_DIM)
        gs["q_norm_a"][l] = dgv[:2 * n_a].sum(0)
        gs["k_norm_a"][l] = dgv[2 * n_a:4 * n_a].sum(0)
        gs["q_norm_b"][l] = dgv[6 * n_a:6 * n_a + 2 * n_b].sum(0)
        gs["k_norm_b"][l] = dgv[6 * n_a + 2 * n_b:6 * n_a + 2 * n_b + 2 * n_kv].sum(0)
        dwqkv = _mm_tn("dw_qkv", s["h2"], D, draw, QW, 1, False, False, 1.0)
        gw["w_qkv"] = dwqkv.reshape(D, N_CHIPS, -1).transpose(1, 0, 2)
        dx, gs["norm_mix"][l] = _bwd_into_norm("qkv_bwd_b", draw, QW, w["w_qkv"], 1, 1, s["x1"], norm_mix[l:l + 1], dx,
                                               draw)
        dx, gs["norm_ffn1"][l], gw["ffn1_w_in"], gw["ffn1_w_out"] = _ffn_backward(
            dx, s["ffn1"], norm_ffn1[l:l + 1], w["ffn1_w_in"], w["ffn1_w_out"],
            at("before_ffn1", dx, [gw[n] for n in GROUP_REST]), hooks.get("in_ffn1"))
        return dx, gw

    out = {}

    def adamw_group(names, l, grads, after):
        for name, g_ in zip(names, grads):
            out[name] = _adamw_layer(given[name], g_, given["m_" + name], given["v_" + name], l, out.get(name), after)
            after = out[name][0]
        return after

    dx, gw1 = layer_backward(NL - 1, dx, {})
    px_1 = _scatter_exchange("1", [gw1[n] for n in BIG])
    rs = {}

    def chips_1(ready):
        rs["chips_1"] = _scatter_chips("1", px_1, ready)
        return rs["chips_1"][3]

    def share_1(ready):
        rs["share_1"] = _scatter_share("1", rs["chips_1"], ready)
        return rs["share_1"][3]

    def exchange_0a(ready, grads):
        rs["px_0a"] = _scatter_exchange("0a", grads)
        return rs["px_0a"][3]

    def chips_0a(ready):
        rs["chips_0a"] = _scatter_chips("0a", rs["px_0a"], ready)
        return rs["chips_0a"][3]

    dx, gw0 = layer_backward(0, dx, {"start": lambda ready: px_1[3], "after_ple": chips_1, "after_attn_a": share_1,
                                     "before_ffn1": exchange_0a, "in_ffn1": chips_0a})
    grad_x = dx.reshape(x.shape)

    px_0b = _scatter_exchange("0b", [gw0[n] for n in GROUP_FFN1])
    share_0a = _scatter_share("0a", rs["chips_0a"], px_0b[3])
    g_1 = _scatter_done("1", rs["share_1"], share_0a[3])
    chips_0b = _scatter_chips("0b", px_0b, g_1[0])
    ready = adamw_group(BIG, 1, g_1, chips_0b[3])
    ready = adamw_group(GROUP_REST, 0, _scatter_done("0a", share_0a, ready), ready)
    share_0b = _scatter_share("0b", chips_0b, ready)

    d_rel_bias = (_bias_grad(dt_a, bmaps_a, 0) + _bias_grad(dt_b, bmaps_b, n_heads))[:, :rel_bias.shape[1]]
    small_g = [d_rel_bias] + [jnp.stack([t.reshape(-1) for t in gs[name]]) for name in SMALL[1:]]
    g_sum = _allreduce_small(_pack_small(small_g))
    res = _ew("adamw_small", _adamw_fn,
              [_pack_small([given[n] for n in SMALL]), g_sum, _pack_small([given["m_" + n] for n in SMALL]),
               _pack_small([given["v_" + n] for n in SMALL])], [(LANES, F32)] * 4)
    like = [given[n] for n in SMALL]
    unpacked = [_unpack_small(r, like) for r in res]
    for i, name in enumerate(SMALL):
        out[name] = [u[i] for u in unpacked]

    adamw_group(GROUP_FFN1, 0, _scatter_done("0b", share_0b, res[0]), res[0])

    return (loss, grad_x, *[out[n][0] for n in WEIGHTS], *[out[n][1] for n in WEIGHTS],
            *[out[n][2] for n in WEIGHTS], *[out[n][3] for n in WEIGHTS])
```

```python
import functools
import math

import numpy as np
import jax
import jax.numpy as jnp
from jax import lax
from jax.experimental import pallas as pl
from jax.experimental.pallas import tpu as pltpu

F32 = jnp.float32
BF16 = jnp.bfloat16
MESH = pl.DeviceIdType.MESH

HEAD_DIM = 64
LANES = 128
QBLOCK = 128
FWD_UNROLL, BWD_UNROLL = 4, 4
N_BUCKETS = 32
MAX_DISTANCE = 1024
DILATED_CONFIGS = ((128, 1), (512, 4), (2048, 16))
SWA_RADIUS = 128
GROUP_B = 4
EPS = 1e-6
NEG = -1e30
Q_SCALE = HEAD_DIM ** -0.5
ADAM_LR, ADAM_B1, ADAM_B2, ADAM_EPS, ADAM_WD, ADAM_STEP = 0.001, 0.9, 0.999, 1e-08, 0.01, 10
VMEM_LIMIT = 56 * 2 ** 20
N_CHIPS = 4
N_DEV = 8

BIG = ("ffn1_w_in", "ffn1_w_out", "w_qkv", "w_o", "ffn2_w_in", "ffn2_w_out", "w_ple_gate", "w_ple_proj")
SMALL = ("rel_bias", "norm_ffn1", "norm_mix", "q_norm_a", "k_norm_a", "q_norm_b", "k_norm_b", "sink_b",
         "norm_ffn2", "norm_ple")
WEIGHTS = ("rel_bias", "norm_ffn1", "ffn1_w_in", "ffn1_w_out", "norm_mix", "w_qkv", "q_norm_a", "k_norm_a",
           "q_norm_b", "k_norm_b", "sink_b", "w_o", "norm_ffn2", "ffn2_w_in", "ffn2_w_out", "norm_ple",
           "w_ple_gate", "w_ple_proj")


HBM_SPEC = pl.BlockSpec(memory_space=pltpu.HBM)
ANY_SPEC = pl.BlockSpec(memory_space=pl.ANY)
SEM_SPEC = pl.BlockSpec(memory_space=pltpu.SEMAPHORE)


def _params(n_grid):
    return pltpu.CompilerParams(dimension_semantics=("arbitrary",) * n_grid, vmem_limit_bytes=VMEM_LIMIT)


def _pick_tm(rows, cap):
    t = (min(cap, rows) // 16) * 16
    while t >= 16:
        if rows % t == 0:
            return t
        t -= 16
    return rows


def _dot(a, b):
    return jnp.dot(a, b, preferred_element_type=F32)


def _dot_nt(a, b):
    return lax.dot_general(a, b, (((1,), (1,)), ((), ())), preferred_element_type=F32)


def _dot_tn(a, b):
    return lax.dot_general(a, b, (((0,), (0,)), ((), ())), preferred_element_type=F32)


def _sigmoid(z):
    return 1.0 / (1.0 + jnp.exp(-z))


def _lo_lanes(shape):
    return lax.broadcasted_iota(jnp.int32, shape, len(shape) - 1) % LANES < HEAD_DIM


def _seg_sum(blk):
    lo = _lo_lanes(blk.shape)
    s_lo = jnp.sum(jnp.where(lo, blk, 0.0), axis=1, keepdims=True)
    s_hi = jnp.sum(jnp.where(lo, 0.0, blk), axis=1, keepdims=True)
    return jnp.where(lo, s_lo, s_hi)


def _rms_bwd_tile(x, g, dh):
    r = lax.rsqrt(jnp.mean(x * x, axis=-1, keepdims=True) + EPS)
    xh = x * r
    dyg = dh * g
    dx = r * (dyg - xh * jnp.mean(dyg * xh, axis=-1, keepdims=True))
    return dx, jnp.sum(dh * xh, axis=0, keepdims=True)


def _ew(name, fn, ins, out_defs, cap=512):
    rows = ins[0].shape[0]
    tm = _pick_tm(rows, cap)
    n_in = len(ins)

    def body(*refs):
        vals = fn(*[r[...] for r in refs[:n_in]])
        if not isinstance(vals, tuple):
            vals = (vals,)
        for r, v in zip(refs[n_in:], vals):
            r[...] = v.astype(r.dtype)

    return pl.pallas_call(
        body, name=name, grid=(rows // tm,),
        in_specs=[pl.BlockSpec((tm, a.shape[1]), lambda i: (i, 0)) for a in ins],
        out_specs=[pl.BlockSpec((tm, c), lambda i: (i, 0)) for c, _ in out_defs],
        out_shape=[jax.ShapeDtypeStruct((rows, c), dt) for c, dt in out_defs],
        compiler_params=_params(1))(*ins)


def _rms_tile(xv, gv):
    r = lax.rsqrt(jnp.mean(xv * xv, axis=-1, keepdims=True) + EPS)
    return (xv * r * gv).astype(BF16)


def _ffn_up(x, g, win):
    T, D = x.shape
    wc = win.shape[2]
    tm = _pick_tm(T, 512)

    def body(x_ref, g_ref, wg_ref, wu_ref, h_ref, silu_ref, dgate_ref, act_ref, wcat):
        @pl.when(pl.program_id(1) == 0)
        def _():
            wcat[:, :wc] = wg_ref[...]
            wcat[:, wc:] = wu_ref[...]

        hv = _rms_tile(x_ref[...], g_ref[...])
        h_ref[...] = hv
        gu = _dot(hv, wcat[...])
        gte, u = gu[:, :wc], gu[:, wc:]
        sg = _sigmoid(gte)
        silu = gte * sg
        silu_ref[...] = silu.astype(BF16)
        dgate_ref[...] = ((sg + silu * (1.0 - sg)) * u).astype(BF16)
        act_ref[...] = (silu * u).astype(BF16)

    out = jax.ShapeDtypeStruct((T, 2 * wc), BF16)
    ospec = pl.BlockSpec((tm, wc), lambda j, i: (i, j))
    nt = T // tm
    h_spec = pl.BlockSpec((tm, D), lambda j, i: (jnp.where(j == 0, i, nt), 0))
    return pl.pallas_call(
        body, name="ffn_up", grid=(2, nt),
        in_specs=[pl.BlockSpec((tm, D), lambda j, i: (i, 0)), pl.BlockSpec((1, D), lambda j, i: (0, 0)),
                  pl.BlockSpec((None, D, wc), lambda j, i: (j, 0, 0)),
                  pl.BlockSpec((None, D, wc), lambda j, i: (j + 2, 0, 0))],
        out_specs=[h_spec] + [ospec] * 3, out_shape=[jax.ShapeDtypeStruct((T + tm, D), BF16)] + [out] * 3,
        scratch_shapes=[pltpu.VMEM((D, 2 * wc), BF16)], compiler_params=_params(2))(x, g, win, win)


def _mm_res(name, res, a_list, w, scale):
    T, N = res.shape
    n = len(a_list)
    widths = [a.shape[1] for a in a_list]
    tm = _pick_tm(T, 512)

    def body(*refs):
        r_ref, a_refs, w_refs, o_ref = refs[0], refs[1:1 + n], refs[1 + n:1 + 2 * n], refs[1 + 2 * n]
        acc = _dot(a_refs[0][...].astype(BF16), w_refs[0][...])
        for a_ref, w_ref in zip(a_refs[1:], w_refs[1:]):
            acc = acc + _dot(a_ref[...].astype(BF16), w_ref[...])
        o_ref[...] = r_ref[...] + scale * acc

    w_specs, off = [], 0
    for k in widths:
        w_specs.append(pl.BlockSpec((k, N), lambda i, blk=off // k: (blk, 0)))
        off += k
    return pl.pallas_call(
        body, name=name, grid=(T // tm,),
        in_specs=[pl.BlockSpec((tm, N), lambda i: (i, 0))]
        + [pl.BlockSpec((tm, k), lambda i: (i, 0)) for k in widths] + w_specs,
        out_specs=pl.BlockSpec((tm, N), lambda i: (i, 0)),
        out_shape=jax.ShapeDtypeStruct((T, N), F32), compiler_params=_params(1))(res, *a_list, *([w] * n))


def _norm_proj(name, x, g, w):
    T, K = x.shape
    N = w.shape[1]
    tm = _pick_tm(T, 512)

    def body(x_ref, g_ref, w_ref, h_ref, o_ref):
        hv = _rms_tile(x_ref[...], g_ref[...])
        h_ref[...] = hv
        o_ref[...] = _dot(hv, w_ref[...])

    row = pl.BlockSpec((tm, K), lambda i: (i, 0))
    return pl.pallas_call(
        body, name=name, grid=(T // tm,),
        in_specs=[row, pl.BlockSpec((1, K), lambda i: (0, 0)), pl.BlockSpec((K, N), lambda i: (0, 0))],
        out_specs=[row, pl.BlockSpec((tm, N), lambda i: (i, 0))],
        out_shape=[jax.ShapeDtypeStruct((T, K), BF16), jax.ShapeDtypeStruct((T, N), F32)],
        compiler_params=_params(1))(x, g, w)


def _mm_tn(name, a, a_w, b, b_w, n_slots, a_by_slot, b_by_slot, scale, tm_cap=512):
    T = b.shape[0]
    tm = _pick_tm(T, tm_cap)
    nt = T // tm

    def body(a_ref, b_ref, o_ref, acc):
        i = pl.program_id(1)

        @pl.when(i == 0)
        def _():
            acc[...] = jnp.zeros_like(acc)

        acc[...] += _dot_tn(a_ref[...].astype(BF16), b_ref[...].astype(BF16))

        @pl.when(i == nt - 1)
        def _():
            o_ref[...] = (acc[...] * scale).astype(BF16)

    return pl.pallas_call(
        body, name=name, grid=(n_slots, nt),
        in_specs=[pl.BlockSpec((tm, a_w), (lambda s, i: (i, s)) if a_by_slot else (lambda s, i: (i, 0))),
                  pl.BlockSpec((tm, b_w), (lambda s, i: (i, s)) if b_by_slot else (lambda s, i: (i, 0)))],
        out_specs=pl.BlockSpec((None, a_w, b_w), lambda s, i: (s, 0, 0)),
        out_shape=jax.ShapeDtypeStruct((n_slots, a_w, b_w), BF16),
        scratch_shapes=[pltpu.VMEM((a_w, b_w), F32)], compiler_params=_params(2))(a, b)


def _mm_tn_pairs(name, a, b, b_w, n_slots):
    T = b.shape[0]
    a_w = a.shape[1]
    tm = _pick_tm(T, 512)
    nt = T // tm

    def body(a_ref, b_ref, o_ref, acc):
        i = pl.program_id(1)

        @pl.when(i == 0)
        def _():
            acc[...] = jnp.zeros_like(acc)

        acc[...] += _dot_tn(a_ref[...], b_ref[...])

        @pl.when(i == nt - 1)
        def _():
            o_ref[0] = acc[:, :b_w].astype(BF16)
            o_ref[1] = acc[:, b_w:].astype(BF16)

    return pl.pallas_call(
        body, name=name, grid=(n_slots // 2, nt),
        in_specs=[pl.BlockSpec((tm, a_w), lambda s, i: (i, 0)), pl.BlockSpec((tm, 2 * b_w), lambda s, i: (i, s))],
        out_specs=pl.BlockSpec((2, a_w, b_w), lambda s, i: (s, 0, 0)),
        out_shape=jax.ShapeDtypeStruct((n_slots, a_w, b_w), BF16),
        scratch_shapes=[pltpu.VMEM((a_w, 2 * b_w), F32)], compiler_params=_params(2))(a, b)


def _ffn_bwd(dx, wout, silu, dgate, win, x, g, after):
    T, D = dx.shape
    F = silu.shape[1]
    n_slots, _, wc = win.shape
    tm = _pick_tm(T, 256)

    def body(dx_ref, wout_hbm, s_ref, dgt_ref, win_hbm, x_ref, g_ref, after_ref,
             dgu_ref, dxo_ref, dg_ref, wout_v, wcat, sem):
        @pl.when(pl.program_id(0) == 0)
        def _():
            copies = [pltpu.make_async_copy(wout_hbm, wout_v, sem.at[n_slots])]
            copies += [pltpu.make_async_copy(win_hbm.at[s], wcat.at[s // 2, :, pl.ds((s % 2) * wc, wc)], sem.at[s])
                       for s in range(n_slots)]
            for cp in copies:
                cp.start()
            for cp in copies:
                cp.wait()
            dg_ref[...] = jnp.zeros_like(dg_ref)

        dxv = dx_ref[...]
        dact = 0.5 * _dot_nt(dxv.astype(BF16), wout_v[...])
        d_gate = (dact * dgt_ref[...].astype(F32)).astype(BF16)
        d_up = (dact * s_ref[...].astype(F32)).astype(BF16)
        dgu_ref[:, :F] = d_gate
        dgu_ref[:, F:] = d_up
        dh = _dot_nt(d_gate, wcat[0]) + _dot_nt(d_up, wcat[1])
        dxn, dg = _rms_bwd_tile(x_ref[...], g_ref[...], dh)
        dxo_ref[...] = dxv + dxn
        dg_ref[...] += dg

    row = pl.BlockSpec((tm, D), lambda i: (i, 0))
    vec = pl.BlockSpec((1, D), lambda i: (0, 0))
    act_spec = pl.BlockSpec((tm, F), lambda i: (i, 0))
    return pl.pallas_call(
        body, name="ffn_bwd", grid=(T // tm,),
        in_specs=[row, ANY_SPEC, act_spec, act_spec, ANY_SPEC, row, vec, ANY_SPEC],
        out_specs=[pl.BlockSpec((tm, 2 * F), lambda i: (i, 0)), row, vec],
        out_shape=[jax.ShapeDtypeStruct((T, 2 * F), BF16), jax.ShapeDtypeStruct((T, D), F32),
                   jax.ShapeDtypeStruct((1, D), F32)],
        scratch_shapes=[pltpu.VMEM((F, D), BF16), pltpu.VMEM((n_slots // 2, D, 2 * wc), BF16),
                        pltpu.SemaphoreType.DMA((n_slots + 1,))],
        compiler_params=_params(1))(dx, wout, silu, dgate, win, x, g, after)


def _bwd_into_norm(name, d, d_w, w, n_slots, group, x, g, dx_in, after):
    T, D = x.shape
    tm = _pick_tm(T, 256)
    n_mm = n_slots // group
    k_w = group * d_w

    def body(*refs):
        d_refs, w_hbm = refs[:n_mm], refs[n_mm]
        x_ref, g_ref, dxi_ref, _, dx_ref, dg_ref, wcat, sem = refs[n_mm + 1:]

        @pl.when(pl.program_id(0) == 0)
        def _():
            copies = [pltpu.make_async_copy(w_hbm.at[s] if w.ndim == 3 else w_hbm,
                                            wcat.at[s // group, :, pl.ds((s % group) * d_w, d_w)], sem.at[s])
                      for s in range(n_slots)]
            for cp in copies:
                cp.start()
            for cp in copies:
                cp.wait()
            dg_ref[...] = jnp.zeros_like(dg_ref)

        dh = _dot_nt(d_refs[0][...], wcat[0])
        for m in range(1, n_mm):
            dh = dh + _dot_nt(d_refs[m][...], wcat[m])
        dxn, dg = _rms_bwd_tile(x_ref[...], g_ref[...], dh)
        dx_ref[...] = dxi_ref[...] + dxn
        dg_ref[...] += dg

    row = pl.BlockSpec((tm, D), lambda i: (i, 0))
    vec = pl.BlockSpec((1, D), lambda i: (0, 0))
    return pl.pallas_call(
        body, name=name, grid=(T // tm,),
        in_specs=[pl.BlockSpec((tm, k_w), lambda i, m=m: (i, m)) for m in range(n_mm)]
        + [ANY_SPEC, row, vec, row, ANY_SPEC],
        out_specs=[row, vec],
        out_shape=[jax.ShapeDtypeStruct((T, D), F32), jax.ShapeDtypeStruct((1, D), F32)],
        scratch_shapes=[pltpu.VMEM((n_mm, D, k_w), BF16), pltpu.SemaphoreType.DMA((n_slots,))],
        compiler_params=_params(1))(*([d] * n_mm + [w, x, g, dx_in, after]))


def _ple_fwd(x, g, p, wg, wp, after):
    T, D = x.shape
    P = p.shape[1]
    tm = _pick_tm(T, 256)

    def body(x_ref, g_ref, p_ref, wg_ref, wp_ref, after_ref, hn_ref, xo_ref, gate_ref, pp_ref):
        xv = x_ref[...]
        hn = _rms_tile(xv, g_ref[...])
        hn_ref[...] = hn
        gate = _sigmoid(_dot(hn, wg_ref[...]))
        pp = _dot(p_ref[...].astype(BF16), wp_ref[...])
        gate_ref[...] = gate.astype(BF16)
        pp_ref[...] = pp.astype(BF16)
        xo_ref[...] = xv + gate * pp

    row = pl.BlockSpec((tm, D), lambda i: (i, 0))
    out = jax.ShapeDtypeStruct((T, D), F32)
    half = jax.ShapeDtypeStruct((T, D), BF16)
    return pl.pallas_call(
        body, name="ple_fwd", grid=(T // tm,),
        in_specs=[row, pl.BlockSpec((1, D), lambda i: (0, 0)), pl.BlockSpec((tm, P), lambda i: (i, 0)),
                  pl.BlockSpec((D, D), lambda i: (0, 0)), pl.BlockSpec((P, D), lambda i: (0, 0)), ANY_SPEC],
        out_specs=[row, row, row, row], out_shape=[half, out, half, half],
        compiler_params=_params(1))(x, g, p, wg, wp, after)


def _ple_bwd(dx, gate, pp, hn, p, x, g, wg, after):
    T, D = x.shape
    P = p.shape[1]
    tm = _pick_tm(T, 256)
    nt = T // tm

    def body(dx_ref, gate_ref, pp_ref, hn_ref, p_ref, x_ref, g_ref, wg_ref, after_ref,
             dxo_ref, dg_ref, dwg_ref, dwp_ref, acc_g, acc_p):
        i = pl.program_id(0)

        @pl.when(i == 0)
        def _():
            acc_g[...] = jnp.zeros_like(acc_g)
            acc_p[...] = jnp.zeros_like(acc_p)
            dg_ref[...] = jnp.zeros_like(dg_ref)

        dxv = dx_ref[...]
        gate = gate_ref[...].astype(F32)
        dz = (dxv * pp_ref[...].astype(F32) * gate * (1.0 - gate)).astype(BF16)
        dpp = (dxv * gate).astype(BF16)
        acc_g[...] += _dot_tn(hn_ref[...], dz)
        acc_p[...] += _dot_tn(p_ref[...].astype(BF16), dpp)
        dxn, dg = _rms_bwd_tile(x_ref[...], g_ref[...], _dot_nt(dz, wg_ref[...]))
        dxo_ref[...] = dxv + dxn
        dg_ref[...] += dg

        @pl.when(i == nt - 1)
        def _():
            dwg_ref[...] = acc_g[...].astype(BF16)
            dwp_ref[...] = acc_p[...].astype(BF16)

    row = pl.BlockSpec((tm, D), lambda i: (i, 0))
    vec = pl.BlockSpec((1, D), lambda i: (0, 0))
    return pl.pallas_call(
        body, name="ple_bwd", grid=(nt,),
        in_specs=[row, row, row, row, pl.BlockSpec((tm, P), lambda i: (i, 0)), row, vec,
                  pl.BlockSpec((D, D), lambda i: (0, 0)), ANY_SPEC],
        out_specs=[row, vec, pl.BlockSpec((D, D), lambda i: (0, 0)), pl.BlockSpec((P, D), lambda i: (0, 0))],
        out_shape=[jax.ShapeDtypeStruct((T, D), F32), jax.ShapeDtypeStruct((1, D), F32),
                   jax.ShapeDtypeStruct((D, D), BF16), jax.ShapeDtypeStruct((P, D), BF16)],
        scratch_shapes=[pltpu.VMEM((D, D), F32), pltpu.VMEM((P, D), F32)],
        compiler_params=_params(1))(dx, gate, pp, hn, p, x, g, wg, after)


def _loss_fwd_bwd(y, tgt):
    T, D = y.shape
    tm = _pick_tm(T, 512)

    def body(y_ref, t_ref, dy_ref, loss_ref):
        e = y_ref[...] - t_ref[...]
        dy_ref[...] = e / D

        @pl.when(pl.program_id(0) == 0)
        def _():
            loss_ref[...] = jnp.zeros_like(loss_ref)

        loss_ref[...] += 0.5 * jnp.sum(jnp.mean(e * e, axis=-1, keepdims=True), axis=0, keepdims=True)

    row = pl.BlockSpec((tm, D), lambda i: (i, 0))
    return pl.pallas_call(
        body, name="loss", grid=(T // tm,), in_specs=[row, row],
        out_specs=[row, pl.BlockSpec((8, LANES), lambda i: (0, 0))],
        out_shape=[jax.ShapeDtypeStruct((T, D), F32), jax.ShapeDtypeStruct((8, LANES), F32)],
        compiler_params=_params(1))(y, tgt)


def _qkv_layout(D):
    n_a = D // (2 * LANES)
    n_b = D // (2 * LANES)
    n_kv = max(1, (2 * n_b) // GROUP_B) * HEAD_DIM // LANES
    return n_a, n_b, n_kv


def _dup_half(xv, half):
    rolled = pltpu.roll(xv, HEAD_DIM, 1)
    lo = _lo_lanes(xv.shape)
    return jnp.where(lo, xv, rolled) if half == 0 else jnp.where(lo, rolled, xv)


def _qkv_post(raw, gains):
    T, W = raw.shape
    n_a, n_b, n_kv = _qkv_layout(W * 4 // 9)
    tm = _pick_tm(T, 256)
    o_qb = 3 * n_a

    def norm(xv, gv, scale):
        ms = _seg_sum(xv * xv) * (1.0 / HEAD_DIM)
        return xv * lax.rsqrt(ms + EPS) * gv * scale

    def body(raw_ref, g_ref, a_ref, b_ref):
        def blk(cb):
            return raw_ref[:, cb * LANES:(cb + 1) * LANES]

        def gn(cb):
            return g_ref[:, cb * LANES:(cb + 1) * LANES]

        for cb in range(n_a):
            a_ref[:, cb * LANES:(cb + 1) * LANES] = norm(blk(cb), gn(cb), Q_SCALE)
            cbk = n_a + cb
            a_ref[:, cbk * LANES:(cbk + 1) * LANES] = norm(blk(cbk), gn(cbk), 1.0)
            cbv = 2 * n_a + cb
            a_ref[:, cbv * LANES:(cbv + 1) * LANES] = blk(cbv)
        for cb in range(n_b):
            src = o_qb + cb
            b_ref[:, cb * LANES:(cb + 1) * LANES] = norm(blk(src), gn(src), Q_SCALE)
        for e in range(n_b):
            kvh = (2 * e) // GROUP_B
            ck = o_qb + n_b + kvh // 2
            cv = ck + n_kv
            kn = norm(blk(ck), gn(ck), 1.0)
            b_ref[:, (n_b + e) * LANES:(n_b + e + 1) * LANES] = _dup_half(kn, kvh % 2)
            b_ref[:, (2 * n_b + e) * LANES:(2 * n_b + e + 1) * LANES] = _dup_half(blk(cv), kvh % 2)

    wa, wb = 3 * n_a * LANES, 3 * n_b * LANES
    return pl.pallas_call(
        body, name="qkv_post", grid=(T // tm,),
        in_specs=[pl.BlockSpec((tm, W), lambda i: (i, 0)), pl.BlockSpec((1, W), lambda i: (0, 0))],
        out_specs=[pl.BlockSpec((tm, wa), lambda i: (i, 0)), pl.BlockSpec((tm, wb), lambda i: (i, 0))],
        out_shape=[jax.ShapeDtypeStruct((T, wa), F32), jax.ShapeDtypeStruct((T, wb), F32)],
        compiler_params=_params(1))(raw, gains)


def _qkv_post_bwd(raw, gains, d_a, d_b):
    T, W = raw.shape
    n_a, n_b, n_kv = _qkv_layout(W * 4 // 9)
    tm = _pick_tm(T, 256)
    o_qb = 3 * n_a

    def body(raw_ref, g_ref, daq, dak, dav, dbq, dbk, dbv, o_ref, dg_ref):
        @pl.when(pl.program_id(0) == 0)
        def _():
            dg_ref[...] = jnp.zeros_like(dg_ref)

        def cols(ref, cb):
            return ref[:, cb * LANES:(cb + 1) * LANES]

        def norm_bwd(cb, dy, scale):
            xv = cols(raw_ref, cb)
            gv = cols(g_ref, cb)
            r = lax.rsqrt(_seg_sum(xv * xv) * (1.0 / HEAD_DIM) + EPS)
            xh = xv * r
            dys = dy * scale
            dyg = dys * gv
            dxv = r * (dyg - xh * (_seg_sum(dyg * xh) * (1.0 / HEAD_DIM)))
            o_ref[:, cb * LANES:(cb + 1) * LANES] = dxv.astype(BF16)
            dg_ref[:, cb * LANES:(cb + 1) * LANES] += jnp.sum(dys * xh, axis=0, keepdims=True)

        def fold(ref, kv_blk):
            halves = []
            for half in range(2):
                kvh = 2 * kv_blk + half
                blocks = [e for e in range(n_b) if (2 * e) // GROUP_B == kvh]
                s = cols(ref, blocks[0])
                for e in blocks[1:]:
                    s = s + cols(ref, e)
                halves.append(s + pltpu.roll(s, HEAD_DIM, 1))
            return jnp.where(_lo_lanes(halves[0].shape), halves[0], halves[1])

        for cb in range(n_a):
            norm_bwd(cb, cols(daq, cb), Q_SCALE)
            norm_bwd(n_a + cb, cols(dak, cb), 1.0)
            cbv = 2 * n_a + cb
            o_ref[:, cbv * LANES:(cbv + 1) * LANES] = cols(dav, cb).astype(BF16)
        for cb in range(n_b):
            norm_bwd(o_qb + cb, cols(dbq, cb), Q_SCALE)
        for kb in range(n_kv):
            ck = o_qb + n_b + kb
            cv = ck + n_kv
            norm_bwd(ck, fold(dbk, kb), 1.0)
            o_ref[:, cv * LANES:(cv + 1) * LANES] = fold(dbv, kb).astype(BF16)

    hw_a, hw_b = n_a * LANES, n_b * LANES
    return pl.pallas_call(
        body, name="qkv_post_bwd", grid=(T // tm,),
        in_specs=[pl.BlockSpec((tm, W), lambda i: (i, 0)), pl.BlockSpec((1, W), lambda i: (0, 0))]
        + [pl.BlockSpec((tm, hw_a), lambda i: (i, 0))] * 3 + [pl.BlockSpec((tm, hw_b), lambda i: (i, 0))] * 3,
        out_specs=[pl.BlockSpec((tm, W), lambda i: (i, 0)), pl.BlockSpec((1, W), lambda i: (0, 0))],
        out_shape=[jax.ShapeDtypeStruct((T, W), BF16), jax.ShapeDtypeStruct((1, W), F32)],
        compiler_params=_params(1))(raw, gains, *d_a, *d_b)


def _t5_bucket_np(rel):
    half = N_BUCKETS // 2
    max_exact = half // 2
    ret = np.where(rel > 0, half, 0)
    n = np.abs(rel)
    nf = np.maximum(n, 1).astype(np.float32)
    large = max_exact + (np.log(nf / np.float32(max_exact)) / np.float32(math.log(MAX_DISTANCE / max_exact))
                         * np.float32(half - max_exact)).astype(np.int32)
    large = np.minimum(large, half - 1)
    return ret + np.where(n < max_exact, n, large)


def _window_pad(radius):
    assert radius <= QBLOCK
    return HEAD_DIM if radius <= HEAD_DIM else QBLOCK


def _bucket_maps(configs):
    pad = _window_pad(configs[0][0])
    q = np.arange(QBLOCK)[:, None]
    kk = np.arange(QBLOCK + 2 * pad)[None, :]
    rel = kk - pad - q
    maps = [np.where(np.abs(rel) <= radius, _t5_bucket_np(rel * dil), -1) for radius, dil in configs]
    return np.stack(maps).astype(np.int32)


def _bias_build(rel_bias, bmaps, col0):
    n_sets, _, W = bmaps.shape
    n_heads = rel_bias.shape[1] // 2

    def body(rb_ref, bm_ref, o_ref):
        h = pl.program_id(1)
        bm = bm_ref[...]

        def step(n, acc):
            return jnp.where(bm == n, rb_ref[n, col0 + h], acc)

        o_ref[...] = lax.fori_loop(0, N_BUCKETS, step, jnp.where(bm < 0, NEG, 0.0).astype(F32))

    return pl.pallas_call(
        body, name="bias_build", grid=(n_sets, n_heads),
        in_specs=[pl.BlockSpec(memory_space=pltpu.SMEM), pl.BlockSpec((None, QBLOCK, W), lambda s, h: (s, 0, 0))],
        out_specs=pl.BlockSpec((None, None, QBLOCK, W), lambda s, h: (s, h, 0, 0)),
        out_shape=jax.ShapeDtypeStruct((n_sets, n_heads, QBLOCK, W), F32),
        compiler_params=_params(2))(rel_bias, bmaps)


def _bias_grad(dtiles, bmaps, col0):
    n_sets, _, W = bmaps.shape
    n_heads = dtiles[0].shape[1]
    n_l = len(dtiles)

    def body(*refs):
        bm_ref, o_ref = refs[0], refs[1 + n_l]
        s, h = pl.program_id(0), pl.program_id(1)

        @pl.when((s == 0) & (h == 0))
        def _():
            o_ref[...] = jnp.zeros_like(o_ref)

        d = refs[1][...]
        for r in refs[2:1 + n_l]:
            d = d + r[...]
        acc8 = d[0:8, :]
        for a in range(1, QBLOCK // 8):
            acc8 = acc8 + pltpu.roll(d[8 * a:8 * a + 8, :], W - 8 * a, 1)
        per_offset = acc8[0:1, :]
        for b in range(1, 8):
            per_offset = per_offset + pltpu.roll(acc8[b:b + 1, :], W - b, 1)
        bucket = lax.broadcasted_iota(jnp.int32, (N_BUCKETS, W), 0)
        hit = bucket == bm_ref[0:1, :]
        per_bucket = jnp.sum(jnp.where(hit, per_offset, 0.0), axis=1, keepdims=True)
        lanes = lax.broadcasted_iota(jnp.int32, o_ref.shape, 1)
        o_ref[...] += jnp.where(lanes == col0 + h, per_bucket, 0.0)

    tile = pl.BlockSpec((None, None, QBLOCK, W), lambda s, h: (s, h, 0, 0))
    return pl.pallas_call(
        body, name="bias_grad", grid=(n_sets, n_heads),
        in_specs=[pl.BlockSpec((None, QBLOCK, W), lambda s, h: (s, 0, 0))] + [tile] * n_l,
        out_specs=pl.BlockSpec((N_BUCKETS, LANES), lambda s, h: (0, 0)),
        out_shape=jax.ShapeDtypeStruct((N_BUCKETS, LANES), F32), compiler_params=_params(2))(bmaps, *dtiles)


def _rows(l_start, n, d, r):
    if d == 1:
        return pl.ds(pl.multiple_of(l_start, 8), n)
    return pl.ds(l_start * d + r, n, stride=d)


def _stack_heads(xv, lo):
    z = jnp.zeros_like(xv)
    return jnp.concatenate([jnp.where(lo, xv, z), jnp.where(lo, z, xv)], axis=0)


def _unstack_heads(xv, lo):
    return jnp.where(lo, xv[:QBLOCK], xv[QBLOCK:])


def _per_head_rows(v0, v1):
    if jnp.ndim(v0) == 0:
        return jnp.where(lax.broadcasted_iota(jnp.int32, (2 * QBLOCK, 1), 0) < QBLOCK, v0, v1)
    return jnp.concatenate([v0, v1], axis=0)


def _block_geometry(b, nb_sub, pad):
    r, lb = b // nb_sub, b % nb_sub
    l0 = lb * QBLOCK
    lp = jnp.maximum(l0 - pad, 0)
    ln = jnp.minimum(l0 + QBLOCK, nb_sub * QBLOCK - pad)
    return r, l0, lp, ln, (lb == 0).astype(jnp.int32) + 2 * (lb == nb_sub - 1).astype(jnp.int32)


def _edge_variants(bias, pad):
    n_br, _, _, W = bias.shape
    col = np.arange(W)
    left, right = col < pad, col >= pad + QBLOCK
    masked = jnp.asarray(np.stack([np.zeros(W, bool), left, right, left | right]))
    return jnp.where(masked[None, :, None, :], NEG, bias.reshape(n_br, 1, -1, W))


def _window(ref, l0, lp, ln, pad, d, r):
    return jnp.concatenate([ref[_rows(lp, pad, d, r), :], ref[_rows(l0, QBLOCK, d, r), :],
                            ref[_rows(ln, pad, d, r), :]], axis=0)


def _attn_fwd(qkv, bias, sink, dils, pad):
    T = qkv.shape[0]
    hw = qkv.shape[1] // 3
    ng = hw // LANES
    n_br = len(dils)
    n_blocks = T // QBLOCK
    W = QBLOCK + 2 * pad
    chunk = 256

    def body(sink_ref, q_ref, k_ref, v_ref, bias_ref, o_ref, lse_ref, *scratch):
        g = pl.program_id(0)
        lo = _lo_lanes((QBLOCK, LANES))
        snk = _per_head_rows(sink_ref[2 * g], sink_ref[2 * g + 1])
        for c, d in enumerate(dils):
            nb_sub = n_blocks // d
            o_dst = scratch[0].at[c] if n_br > 1 else o_ref
            l_dst = scratch[1].at[c] if n_br > 1 else lse_ref

            def block(b, carry, c=c, d=d, nb_sub=nb_sub, o_dst=o_dst, l_dst=l_dst):
                r, l0, lp, ln, edge = _block_geometry(b, nb_sub, pad)
                q = _stack_heads(q_ref[_rows(l0, QBLOCK, d, r), :].astype(BF16), lo)
                k = _window(k_ref, l0, lp, ln, pad, d, r).astype(BF16)
                v = _window(v_ref, l0, lp, ln, pad, d, r).astype(BF16)
                s = _dot_nt(q, k) + bias_ref[c, edge]
                m = jnp.maximum(jnp.max(s, axis=1, keepdims=True), snk)
                p = jnp.exp(s - m)
                den = jnp.sum(p, axis=1, keepdims=True) + jnp.exp(snk - m)
                o_dst[_rows(l0, QBLOCK, d, r), :] = _unstack_heads(_dot(p.astype(BF16), v) / den, lo)
                l_dst[_rows(l0, QBLOCK, d, r), :] = _unstack_heads(
                    jnp.broadcast_to(m + jnp.log(den), (2 * QBLOCK, LANES)), lo)
                return carry

            lax.fori_loop(0, n_blocks, block, 0, unroll=FWD_UNROLL)

        if n_br > 1:
            def merge(i, carry):
                rs = pl.ds(pl.multiple_of(i * chunk, chunk), chunk)
                ls = [scratch[1][c, rs, :] for c in range(n_br)]
                m = ls[0]
                for t in ls[1:]:
                    m = jnp.maximum(m, t)
                ws = [jnp.exp(t - m) for t in ls]
                z = ws[0]
                acc = ws[0] * scratch[0][0, rs, :]
                for c in range(1, n_br):
                    z = z + ws[c]
                    acc = acc + ws[c] * scratch[0][c, rs, :]
                o_ref[rs, :] = acc / z
                lse_ref[rs, :] = m + jnp.log(z)
                return carry

            lax.fori_loop(0, T // chunk, merge, 0)

    def col(base):
        return pl.BlockSpec((T, LANES), lambda g: (0, base + g))

    out = jax.ShapeDtypeStruct((T, hw), F32)
    scratch = [pltpu.VMEM((n_br, T, LANES), F32)] * 2 if n_br > 1 else []
    return pl.pallas_call(
        body, name="attn_fwd", grid=(ng,),
        in_specs=[pl.BlockSpec(memory_space=pltpu.SMEM), col(0), col(ng), col(2 * ng),
                  pl.BlockSpec((n_br, 4, 2 * QBLOCK, W), lambda g: (0, 0, g, 0))],
        out_specs=[col(0), col(0)], out_shape=[out, out], scratch_shapes=scratch,
        compiler_params=_params(1))(sink, qkv, qkv, qkv, bias)


def _attn_bwd(qkv, bias, sink, dils, pad, do, lse, dd, col_base, after):
    T = qkv.shape[0]
    hw = qkv.shape[1] // 3
    ng = hw // LANES
    n_br = len(dils)
    n_blocks = T // QBLOCK
    W = QBLOCK + 2 * pad

    def body(sink_ref, q_ref, k_ref, v_ref, bias_ref, do_ref, lse_ref, dd_ref, after_ref,
             dq_ref, dk_ref, dv_ref, dt_ref, ds_ref):
        g = pl.program_id(0)
        dq_ref[...] = jnp.zeros_like(dq_ref)
        dk_ref[...] = jnp.zeros_like(dk_ref)
        dv_ref[...] = jnp.zeros_like(dv_ref)
        dt_ref[...] = jnp.zeros_like(dt_ref)
        ds_ref[...] = jnp.zeros_like(ds_ref)
        lo = _lo_lanes((QBLOCK, LANES))
        snk = jnp.where(lo, sink_ref[2 * g], sink_ref[2 * g + 1])
        for c, d in enumerate(dils):
            nb_sub = n_blocks // d

            def block(b, carry, c=c, d=d, nb_sub=nb_sub):
                r, l0, lp, ln, edge = _block_geometry(b, nb_sub, pad)
                rows_q = _rows(l0, QBLOCK, d, r)
                q = _stack_heads(q_ref[rows_q, :].astype(BF16), lo)
                k = _window(k_ref, l0, lp, ln, pad, d, r).astype(BF16)
                v = _window(v_ref, l0, lp, ln, pad, d, r).astype(BF16)
                dob = _stack_heads(do_ref[rows_q, :].astype(BF16), lo)
                lse_b = lse_ref[rows_q, :]
                dd_b = dd_ref[rows_q, :]
                s = _dot_nt(q, k) + bias_ref[c, edge]
                p = jnp.exp(s - _per_head_rows(lse_b[:, 0:1], lse_b[:, HEAD_DIM:HEAD_DIM + 1]))
                ds = p * (_dot_nt(dob, v) - _per_head_rows(dd_b[:, 0:1], dd_b[:, HEAD_DIM:HEAD_DIM + 1]))
                dsb = ds.astype(BF16)
                dkw = _dot_tn(dsb, q)
                dvw = _dot_tn(p.astype(BF16), dob)
                dt_ref[c] += ds
                dq_ref[rows_q, :] += _unstack_heads(_dot(dsb, k), lo)
                ds_ref[0:1, :] += jnp.sum(-jnp.exp(snk - lse_b) * dd_b, axis=0, keepdims=True)
                for part, (start, n) in zip((0, pad, pad + QBLOCK), ((lp, pad), (l0, QBLOCK), (ln, pad))):
                    dk_ref[_rows(start, n, d, r), :] += dkw[part:part + n]
                    dv_ref[_rows(start, n, d, r), :] += dvw[part:part + n]
                return carry

            lax.fori_loop(0, n_blocks, block, 0, unroll=BWD_UNROLL)

    def col(base):
        return pl.BlockSpec((T, LANES), lambda g: (0, base + g))

    tile = pl.BlockSpec((n_br, 2 * QBLOCK, W), lambda g: (0, g, 0))
    full = jax.ShapeDtypeStruct((T, hw), F32)
    dq, dk, dv, dt, dsink = pl.pallas_call(
        body, name="attn_bwd", grid=(ng,),
        in_specs=[pl.BlockSpec(memory_space=pltpu.SMEM), col(0), col(ng), col(2 * ng),
                  pl.BlockSpec((n_br, 4, 2 * QBLOCK, W), lambda g: (0, 0, g, 0)),
                  col(col_base), col(0), col(col_base), ANY_SPEC],
        out_specs=[col(0), col(0), col(0), tile, pl.BlockSpec((None, 8, LANES), lambda g: (g, 0, 0))],
        out_shape=[full, full, full, jax.ShapeDtypeStruct((n_br, 2 * ng * QBLOCK, W), F32),
                   jax.ShapeDtypeStruct((ng, 8, LANES), F32)],
        compiler_params=_params(1))(sink, qkv, qkv, qkv, bias, do, lse, dd, after)
    return dq, dk, dv, dt.reshape(n_br, 2 * ng, QBLOCK, W), dsink


def _mix_bwd_in(dx, wo, o_list):
    T, D = dx.shape
    widths = [o.shape[1] for o in o_list]
    hw = sum(widths)
    n = len(o_list)
    tm = _pick_tm(T, 256)

    def body(*refs):
        dx_ref, w_ref, o_refs, do_ref, dd_ref = refs[0], refs[1], refs[2:2 + n], refs[2 + n], refs[3 + n]
        dov = _dot_nt(dx_ref[...].astype(BF16), w_ref[...])
        do_ref[...] = dov
        off = 0
        for o_ref, k in zip(o_refs, widths):
            prod = dov[:, off:off + k] * o_ref[...]
            for cb in range(k // LANES):
                dd_ref[:, off + cb * LANES:off + (cb + 1) * LANES] = _seg_sum(prod[:, cb * LANES:(cb + 1) * LANES])
            off += k

    row = pl.BlockSpec((tm, hw), lambda i: (i, 0))
    out = jax.ShapeDtypeStruct((T, hw), F32)
    return pl.pallas_call(
        body, name="mix_bwd_in", grid=(T // tm,),
        in_specs=[pl.BlockSpec((tm, D), lambda i: (i, 0)), pl.BlockSpec((hw, D), lambda i: (0, 0))]
        + [pl.BlockSpec((tm, k), lambda i: (i, 0)) for k in widths],
        out_specs=[row, row], out_shape=[out, out], compiler_params=_params(1))(dx, wo, *o_list)


def _mesh_pos():
    return lax.axis_index("x"), lax.axis_index("y"), lax.axis_index("c")


def _my_chip():
    return 2 * lax.axis_index("x") + lax.axis_index("y")


def _other_chips(x, y):
    return [(1 - x, y), (x, 1 - y), (1 - x, 1 - y)]


def _half_rows(rows, cc):
    hr = rows // 2
    return pl.ds(pl.multiple_of(cc * hr, 16), hr)


def _cast_into_slot(w, l):
    _, R, C = w.shape
    tm = _pick_tm(R, 512)

    def body(w_ref, o_ref):
        o_ref[...] = w_ref[...].astype(BF16)

    return pl.pallas_call(
        body, name="cast_into_slot", grid=(R // tm,),
        in_specs=[pl.BlockSpec((None, tm, C), lambda i: (l, i, 0))],
        out_specs=pl.BlockSpec((None, tm, C), lambda i: (_my_chip(), i, 0)),
        out_shape=jax.ShapeDtypeStruct((N_CHIPS, R, C), BF16), compiler_params=_params(1))(w)


def _split_start(name, arrays, make_copies, n_sem, after):
    n = len(arrays)
    n_in = n + (0 if after is None else 1)

    def body(*refs):
        send_s, recv_s = refs[n_in], refs[n_in + 1]
        token = refs[n_in + 2 + n]
        for send, _ in make_copies(refs[:n], send_s, recv_s):
            send.start()
        token[...] = jnp.zeros_like(token)

    res = pl.pallas_call(
        body, name=name,
        out_shape=(pltpu.SemaphoreType.DMA((n_sem,)), pltpu.SemaphoreType.DMA((n_sem,)),
                   *[pltpu.HBM(a.shape, a.dtype) for a in arrays], jax.ShapeDtypeStruct((8, LANES), F32)),
        in_specs=[HBM_SPEC] * n + [ANY_SPEC] * (n_in - n),
        out_specs=(SEM_SPEC, SEM_SPEC, *([HBM_SPEC] * n), pl.BlockSpec(memory_space=pltpu.VMEM)),
        input_output_aliases={i: 2 + i for i in range(n)},
        compiler_params=pltpu.CompilerParams(has_side_effects=pltpu.SideEffectType.DATAFLOW_SIDE_EFFECTING),
    )(*[pltpu.with_memory_space_constraint(a, pltpu.HBM) for a in arrays], *([] if after is None else [after]))
    return res[0], res[1], list(res[2:2 + n]), res[2 + n]


def _split_wait(name, send_s, recv_s, arrays, make_copies, after):
    n = len(arrays)

    def body(*refs):
        for send, landed in make_copies(refs[:n], refs[n], refs[n + 1]):
            send.wait_send()
            landed.wait_recv()

    return list(pl.pallas_call(
        body, name=name, out_shape=[pltpu.HBM(a.shape, a.dtype) for a in arrays],
        in_specs=[HBM_SPEC] * n + [SEM_SPEC, SEM_SPEC, ANY_SPEC], out_specs=[HBM_SPEC] * n,
        input_output_aliases={i: i for i in range(n)},
        compiler_params=pltpu.CompilerParams(has_side_effects=pltpu.SideEffectType.DATAFLOW_SIDE_EFFECTING),
    )(*arrays, send_s, recv_s, after))


def _gather_copies(shapes):
    n = len(shapes)

    def make(refs, send_s, recv_s):
        x, y, c = _mesh_pos()
        my = 2 * x + y
        copies = []
        for w in range(n):
            for k, (px, py) in enumerate(_other_chips(x, y)):
                def part(slot, w=w):
                    return refs[w].at[slot, _half_rows(shapes[w][1], c), :]
                sems = dict(send_sem=send_s.at[k * n + w], recv_sem=recv_s.at[k * n + w],
                            device_id=(px, py, c), device_id_type=MESH)
                copies.append((pltpu.make_async_remote_copy(src_ref=part(my), dst_ref=part(my), **sems),
                               pltpu.make_async_remote_copy(src_ref=part(2 * px + py), dst_ref=part(2 * px + py), **sems)))
        return copies

    return make


def _forward_copies(shapes):
    n = len(shapes)

    def make(refs, send_s, recv_s):
        x, y, c = _mesh_pos()
        copies = []
        for w in range(n):
            for k, (px, py) in enumerate(_other_chips(x, y)):
                def part(cc, w=w, slot=2 * px + py):
                    return refs[w].at[slot, _half_rows(shapes[w][1], cc), :]
                sems = dict(send_sem=send_s.at[k * n + w], recv_sem=recv_s.at[k * n + w],
                            device_id=(x, y, 1 - c), device_id_type=MESH)
                copies.append((pltpu.make_async_remote_copy(src_ref=part(c), dst_ref=part(c), **sems),
                               pltpu.make_async_remote_copy(src_ref=part(1 - c), dst_ref=part(1 - c), **sems)))
        return copies

    return make


def _pair_forward(bufs):
    n = len(bufs)
    make = _forward_copies([b.shape for b in bufs])

    def body(*refs):
        copies = make(refs[n:2 * n], refs[2 * n], refs[2 * n + 1])
        for send, _ in copies:
            send.start()
        for _, landed in copies:
            landed.wait_recv()
        for send, _ in copies:
            send.wait_send()

    return list(pl.pallas_call(
        body, name="ag_pair_forward", in_specs=[HBM_SPEC] * n, out_specs=[HBM_SPEC] * n,
        out_shape=[jax.ShapeDtypeStruct(b.shape, b.dtype) for b in bufs],
        input_output_aliases={w: w for w in range(n)},
        scratch_shapes=[pltpu.SemaphoreType.DMA((3 * n,)), pltpu.SemaphoreType.DMA((3 * n,))],
    )(*bufs))


def _pair_exchange_copies(shapes):
    n = len(shapes)

    def make(refs, send_s, recv_s):
        x, y, c = _mesh_pos()
        copies = []
        for t in range(n):
            sems = dict(send_sem=send_s.at[t], recv_sem=recv_s.at[t], device_id=(x, y, 1 - c), device_id_type=MESH)
            land = refs[n + t]
            copies.append((pltpu.make_async_remote_copy(
                src_ref=refs[t].at[:, _half_rows(shapes[t][1], 1 - c), :], dst_ref=land, **sems),
                pltpu.make_async_remote_copy(src_ref=land, dst_ref=land, **sems)))
        return copies

    return make


def _pair_share_copies(shapes):
    n = len(shapes)

    def make(refs, send_s, recv_s):
        x, y, c = _mesh_pos()
        copies = []
        for t in range(n):
            def half(cc, t=t):
                return refs[t].at[_half_rows(shapes[t][0], cc), :]
            sems = dict(send_sem=send_s.at[t], recv_sem=recv_s.at[t], device_id=(x, y, 1 - c), device_id_type=MESH)
            copies.append((pltpu.make_async_remote_copy(src_ref=half(c), dst_ref=half(c), **sems),
                           pltpu.make_async_remote_copy(src_ref=half(1 - c), dst_ref=half(1 - c), **sems)))
        return copies

    return make


def _rs_add_pair(grad, recv):
    n_slot, hr, C = recv.shape
    tm = _pick_tm(hr, 192)
    nb = hr // tm

    def body(a_ref, b_ref, o_ref):
        o_ref[...] = (a_ref[...].astype(F32) + b_ref[...].astype(F32)).astype(BF16)

    blk = pl.BlockSpec((n_slot, tm, C), lambda i: (0, i, 0))
    return pl.pallas_call(
        body, name="rs_add_pair", grid=(nb,),
        in_specs=[pl.BlockSpec((n_slot, tm, C), lambda i: (0, lax.axis_index("c") * nb + i, 0)), blk],
        out_specs=blk, out_shape=jax.ShapeDtypeStruct(recv.shape, BF16), compiler_params=_params(1))(grad, recv)


def _scatter_copies(n):
    def make(refs, send_s, recv_s):
        x, y, c = _mesh_pos()
        copies = []
        for t in range(n):
            for k, (px, py) in enumerate(_other_chips(x, y)):
                sems = dict(send_sem=send_s.at[3 * t + k], recv_sem=recv_s.at[3 * t + k],
                            device_id=(px, py, c), device_id_type=MESH)
                land = refs[n + t].at[k]
                copies.append((pltpu.make_async_remote_copy(src_ref=refs[t].at[2 * px + py], dst_ref=land, **sems),
                               pltpu.make_async_remote_copy(src_ref=land, dst_ref=land, **sems)))
        return copies

    return make


def _rs_add_chips(part, recv):
    _, hr, C = part.shape
    tm = _pick_tm(hr, 256)
    nb = hr // tm

    def body(a_ref, r0, r1, r2, o_ref):
        o_ref[...] = ((a_ref[...].astype(F32) + r0[...].astype(F32)) + r1[...].astype(F32)) + r2[...].astype(F32)

    def rel(k):
        return pl.BlockSpec((None, tm, C), lambda i: (k, i, 0))

    return pl.pallas_call(
        body, name="rs_add_chips", grid=(nb,),
        in_specs=[pl.BlockSpec((None, tm, C), lambda i: (_my_chip(), i, 0)), rel(0), rel(1), rel(2)],
        out_specs=pl.BlockSpec((tm, C), lambda i: (lax.axis_index("c") * nb + i, 0)),
        out_shape=jax.ShapeDtypeStruct((2 * hr, C), F32), compiler_params=_params(1))(part, recv, recv, recv)


def _allreduce_small(v, after):
    rows = v.shape[0]

    def body(v_ref, after_ref, o_ref, buf, send_s, recv_s):
        x, y, c = _mesh_pos()
        me = 4 * x + 2 * y + c
        buf[me] = v_ref[...]
        copies = []
        for r in range(1, N_DEV):
            px = 1 - x if r & 4 else x
            py = 1 - y if r & 2 else y
            pc = 1 - c if r & 1 else c
            send = pltpu.make_async_remote_copy(
                src_ref=v_ref, dst_ref=buf.at[me], send_sem=send_s.at[r - 1], recv_sem=recv_s.at[r - 1],
                device_id=(px, py, pc), device_id_type=MESH)
            peer_slot = buf.at[4 * px + 2 * py + pc]
            landed = pltpu.make_async_remote_copy(
                src_ref=peer_slot, dst_ref=peer_slot, send_sem=send_s.at[r - 1], recv_sem=recv_s.at[r - 1],
                device_id=(px, py, pc), device_id_type=MESH)
            copies.append((send, landed))
        for send, _ in copies:
            send.start()
        for _, landed in copies:
            landed.wait_recv()
        for send, _ in copies:
            send.wait_send()
        acc = buf[0]
        for j in range(1, N_DEV):
            acc = acc + buf[j]
        o_ref[...] = acc

    vm = pl.BlockSpec(memory_space=pltpu.VMEM)
    return pl.pallas_call(
        body, name="allreduce_small", in_specs=[vm, ANY_SPEC], out_specs=vm,
        out_shape=jax.ShapeDtypeStruct((rows, LANES), F32),
        scratch_shapes=[pltpu.VMEM((N_DEV, rows, LANES), F32), pltpu.SemaphoreType.DMA((N_DEV - 1,)),
                        pltpu.SemaphoreType.DMA((N_DEV - 1,))],
    )(v, after)


def _adamw_fn(w, g, m, v):
    m2 = ADAM_B1 * m + (1.0 - ADAM_B1) * g
    v2 = ADAM_B2 * v + (1.0 - ADAM_B2) * (g * g)
    m_hat = m2 / (1.0 - ADAM_B1 ** ADAM_STEP)
    v_hat = v2 / (1.0 - ADAM_B2 ** ADAM_STEP)
    delta = -ADAM_LR * (m_hat / (jnp.sqrt(v_hat) + ADAM_EPS) + ADAM_WD * w)
    return g, delta, m2, v2


def _adamw_layer(w, g, m, v, l, prev, after):
    NL, R, C = w.shape
    tm = _pick_tm(R, 128)
    n_prev = 0 if prev is None else 4

    def body(w_ref, g_ref, m_ref, v_ref, after_ref, *rest):
        outs = rest[n_prev:]
        for o_ref, val in zip(outs, _adamw_fn(w_ref[...], g_ref[...], m_ref[...], v_ref[...])):
            o_ref[...] = val

    lay = pl.BlockSpec((None, tm, C), lambda i: (l, i, 0))
    shape = jax.ShapeDtypeStruct((NL, R, C), F32)
    return pl.pallas_call(
        body, name="adamw", grid=(R // tm,),
        in_specs=[lay, pl.BlockSpec((tm, C), lambda i: (i, 0)), lay, lay, ANY_SPEC] + [ANY_SPEC] * n_prev,
        out_specs=[lay] * 4, out_shape=[shape] * 4,
        input_output_aliases={5 + j: j for j in range(n_prev)},
        compiler_params=_params(1))(w, g, m, v, after, *(prev or []))


def _pack_small(parts):
    out = []
    for a in parts:
        flat = a.reshape(-1)
        n = -(-flat.shape[0] // (8 * LANES)) * 8 * LANES
        out.append(jnp.pad(flat, (0, n - flat.shape[0])).reshape(-1, LANES))
    return jnp.concatenate(out, axis=0)


def _unpack_small(packed, like):
    out, r = [], 0
    for a in like:
        size = int(np.prod(a.shape))
        rows = -(-size // (8 * LANES)) * 8
        out.append(packed[r:r + rows].reshape(-1)[:size].reshape(a.shape))
        r += rows
    return out


def _ffn_forward(x, g, win, wout):
    h, silu, dgate, act = _ffn_up(x, g, win)
    if callable(wout):
        wout = wout(act)
    return _mm_res("ffn_down", x, [act], wout, 0.5), (x, h, silu, dgate, act)


def _ffn_backward(dx, saved, g, win, wout, after):
    x, h, silu, dgate, act = saved
    D = x.shape[1]
    wc = win.shape[2]
    dgu, dx_in, dg = _ffn_bwd(dx, wout, silu, dgate, win, x, g, after)
    dwout = _mm_tn("dw_ffn_out", act, wc, dx, D, 2, True, False, 0.5)
    dwin = _mm_tn_pairs("dw_ffn_in", h, dgu, wc, 4)
    return dx_in, dg, dwin, dwout.reshape(N_CHIPS, -1, D)


GROUP_FFN1 = ("ffn1_w_in", "ffn1_w_out")
GROUP_REST = ("w_qkv", "w_o", "ffn2_w_in", "ffn2_w_out", "w_ple_gate", "w_ple_proj")
GATHER_L0 = (("a", ("ffn1_w_in",)), ("b", ("ffn1_w_out",)), ("c", ("w_qkv", "w_o")),
             ("d", ("ffn2_w_in", "ffn2_w_out", "w_ple_gate", "w_ple_proj")))


def _gather_start(tag, slotted, after):
    return _split_start("ag_start_" + tag, slotted, _gather_copies([a.shape for a in slotted]), 3 * len(slotted), after)


def _gather_finish(tag, started, after):
    send_s, recv_s, arrays, _ = started
    return _pair_forward(_split_wait("ag_wait_" + tag, send_s, recv_s, arrays,
                                     _gather_copies([a.shape for a in arrays]), after))


def _scatter_exchange(tag, grads):
    n = len(grads)
    land = [lax.empty((g_.shape[0], g_.shape[1] // 2, g_.shape[2]), g_.dtype) for g_ in grads]
    return _split_start("rs_px_start_" + tag, list(grads) + land, _pair_exchange_copies([g_.shape for g_ in grads]),
                        n, None)


def _scatter_chips(tag, started, after):
    send_s, recv_s, arrays, _ = started
    n = len(arrays) // 2
    arrays = _split_wait("rs_px_wait_" + tag, send_s, recv_s, arrays,
                         _pair_exchange_copies([a.shape for a in arrays[:n]]), after)
    part = [_rs_add_pair(g_, r_) for g_, r_ in zip(arrays[:n], arrays[n:])]
    land = [lax.empty((3,) + p_.shape[1:], p_.dtype) for p_ in part]
    return _split_start("rs_start_" + tag, part + land, _scatter_copies(n), 3 * n, None)


def _scatter_share(tag, started, after):
    send_s, recv_s, arrays, _ = started
    n = len(arrays) // 2
    arrays = _split_wait("rs_wait_" + tag, send_s, recv_s, arrays, _scatter_copies(n), after)
    halves = [_rs_add_chips(p_, r_) for p_, r_ in zip(arrays[:n], arrays[n:])]
    return _split_start("rs_ps_start_" + tag, halves, _pair_share_copies([h_.shape for h_ in halves]), n, None)


def _scatter_done(tag, started, after):
    send_s, recv_s, arrays, _ = started
    return _split_wait("rs_ps_wait_" + tag, send_s, recv_s, arrays, _pair_share_copies([a.shape for a in arrays]), after)


def kernel(x, p, rel_bias, norm_ffn1, ffn1_w_in, ffn1_w_out, norm_mix, w_qkv, q_norm_a, k_norm_a, q_norm_b, k_norm_b, sink_b, w_o, norm_ffn2, ffn2_w_in, ffn2_w_out, norm_ple, w_ple_gate, w_ple_proj, loss_target, m_rel_bias, m_norm_ffn1, m_ffn1_w_in, m_ffn1_w_out, m_norm_mix, m_w_qkv, m_q_norm_a, m_k_norm_a, m_q_norm_b, m_k_norm_b, m_sink_b, m_w_o, m_norm_ffn2, m_ffn2_w_in, m_ffn2_w_out, m_norm_ple, m_w_ple_gate, m_w_ple_proj, v_rel_bias, v_norm_ffn1, v_ffn1_w_in, v_ffn1_w_out, v_norm_mix, v_w_qkv, v_q_norm_a, v_k_norm_a, v_q_norm_b, v_k_norm_b, v_sink_b, v_w_o, v_norm_ffn2, v_ffn2_w_in, v_ffn2_w_out, v_norm_ple, v_w_ple_gate, v_w_ple_proj):
    given = dict(locals())
    T, D = x.shape[1], x.shape[2]
    NL = norm_ffn1.shape[0]
    x0 = x.reshape(T, D)
    tgt = loss_target.reshape(T, D)
    n_a, n_b, n_kv = _qkv_layout(D)

    assert NL == 2
    slot = [{name: _cast_into_slot(given[name], l) for name in BIG} for l in range(NL)]
    ag, token = {}, None
    for tag, names in GATHER_L0:
        ag[tag] = _gather_start(tag, [slot[0][n] for n in names], token)
        token = ag[tag][3]
    ag_1 = _gather_start("1", [slot[1][n] for n in BIG], token)

    def arrived(tag, after):
        return dict(zip(dict(GATHER_L0)[tag], _gather_finish(tag, ag[tag], after)))

    def by_rows(a):
        return a.reshape(-1, a.shape[-1])

    def by_cols(a):
        return a.transpose(1, 0, 2).reshape(a.shape[1], -1)

    QW = N_CHIPS * w_qkv.shape[2]

    dils = tuple(d for _, d in DILATED_CONFIGS)
    cfg_a = [(w // (2 * d), d) for w, d in DILATED_CONFIGS]
    pad_a, pad_b = _window_pad(cfg_a[0][0]), _window_pad(SWA_RADIUS)
    bmaps_a, bmaps_b = jnp.asarray(_bucket_maps(cfg_a)), jnp.asarray(_bucket_maps([(SWA_RADIUS, 1)]))
    n_heads = rel_bias.shape[1] // 2
    bias_a = _edge_variants(_bias_build(rel_bias, bmaps_a, 0), pad_a)
    bias_b = _edge_variants(_bias_build(rel_bias, bmaps_b, n_heads), pad_b)
    no_sink = jnp.full((n_heads,), NEG, F32)

    def gains_row(l):
        ones = jnp.ones((n_a * LANES,), F32)
        return jnp.concatenate([
            jnp.tile(q_norm_a[l], 2 * n_a), jnp.tile(k_norm_a[l], 2 * n_a), ones,
            jnp.tile(q_norm_b[l], 2 * n_b), jnp.tile(k_norm_b[l], 2 * n_kv), jnp.ones((n_kv * LANES,), F32)]).reshape(1, QW)

    saved, weights = [], []
    xc = x0
    pf_1 = None
    for l in range(NL):
        s, w = {}, {}
        if l == 0:
            w.update(arrived("a", ag_1[3]))

            def ffn1_w_out(act, w=w):
                w.update(arrived("b", act))
                w["ffn1_w_out"] = by_rows(w["ffn1_w_out"])
                return w["ffn1_w_out"]
        else:
            shapes = [a.shape for a in pf_1[2]]
            w.update(zip(BIG, _split_wait("ag_pf_wait_1", pf_1[0], pf_1[1], pf_1[2], _forward_copies(shapes), xc)))
            ffn1_w_out = w["ffn1_w_out"] = by_rows(w["ffn1_w_out"])
        xc, s["ffn1"] = _ffn_forward(xc, norm_ffn1[l:l + 1], w["ffn1_w_in"], ffn1_w_out)
        s["x1"] = xc
        if l == 0:
            w.update(arrived("c", xc))
        w["w_qkv"] = by_cols(w["w_qkv"])
        w["w_o"] = by_rows(w["w_o"])
        s["h2"], raw = _norm_proj("qkv_proj", xc, norm_mix[l:l + 1], w["w_qkv"])
        s["raw"] = raw
        s["qkv_a"], s["qkv_b"] = _qkv_post(raw, gains_row(l))
        s["o_a"], s["lse_a"] = _attn_fwd(s["qkv_a"], bias_a, no_sink, dils, pad_a)
        s["o_b"], s["lse_b"] = _attn_fwd(s["qkv_b"], bias_b, sink_b[l], (1,), pad_b)
        xc = _mm_res("attn_out", xc, [s["o_a"], s["o_b"]], w["w_o"], 1.0)
        if l == 0:
            w.update(arrived("d", xc))
        w["w_ple_proj"] = by_cols(w["w_ple_proj"])
        for name in ("ffn2_w_out", "w_ple_gate"):
            w[name] = by_rows(w[name])
        xc, s["ffn2"] = _ffn_forward(xc, norm_ffn2[l:l + 1], w["ffn2_w_in"], w["ffn2_w_out"])
        s["x3"] = xc
        if l == 0:
            landed = _split_wait("ag_wait_1", ag_1[0], ag_1[1], ag_1[2], _gather_copies([a.shape for a in ag_1[2]]), xc)
            pf_1 = _split_start("ag_pf_start_1", landed, _forward_copies([a.shape for a in landed]), 3 * len(landed),
                                None)
        s["p"] = p[l].reshape(T, -1)
        s["hn"], xc, s["gate"], s["pp"] = _ple_fwd(xc, norm_ple[l:l + 1], s["p"], w["w_ple_gate"], w["w_ple_proj"],
                                                   pf_1[3] if l == 0 else xc)
        saved.append(s)
        weights.append(w)

    dx, loss_blk = _loss_fwd_bwd(xc, tgt)
    loss = lax.psum(loss_blk[0, 0], ("x", "y", "c"))

    gs = {name: [None] * NL for name in SMALL if name != "rel_bias"}
    dt_a, dt_b = [], []

    def layer_backward(l, dx, hooks):
        def at(point, ready, *more):
            return hooks[point](ready, *more) if point in hooks else ready

        s, w, gw = saved[l], weights[l], {}
        dx, gs["norm_ple"][l], dwg, dwp = _ple_bwd(dx, s["gate"], s["pp"], s["hn"], s["p"], s["x3"],
                                                   norm_ple[l:l + 1], w["w_ple_gate"], at("start", dx))
        gw["w_ple_gate"] = dwg.reshape(N_CHIPS, -1, D)
        gw["w_ple_proj"] = dwp.reshape(dwp.shape[0], N_CHIPS, -1).transpose(1, 0, 2)
        dx, gs["norm_ffn2"][l], gw["ffn2_w_in"], gw["ffn2_w_out"] = _ffn_backward(
            dx, s["ffn2"], norm_ffn2[l:l + 1], w["ffn2_w_in"], w["ffn2_w_out"], at("after_ple", dx))
        do, dd = _mix_bwd_in(dx, w["w_o"], [s["o_a"], s["o_b"]])
        hwa = s["o_a"].shape[1]
        gw["w_o"] = jnp.concatenate([
            _mm_tn("dw_o", o_, o_.shape[1], dx, D, 1, False, False, 1.0).reshape(-1, D // N_CHIPS, D)
            for o_ in (s["o_a"], s["o_b"])], axis=0)
        dqa, dka, dva, dt, _ = _attn_bwd(s["qkv_a"], bias_a, no_sink, dils, pad_a, do, s["lse_a"], dd, 0, do)
        dt_a.append(dt)
        dqb, dkb, dvb, dt, dsink = _attn_bwd(s["qkv_b"], bias_b, sink_b[l], (1,), pad_b, do, s["lse_b"], dd,
                                             hwa // LANES, at("after_attn_a", dqa))
        dt_b.append(dt)
        gs["sink_b"][l] = dsink[:, 0, ::HEAD_DIM].reshape(-1)
        draw, dgains = _qkv_post_bwd(s["raw"], gains_row(l), (dqa, dka, dva), (dqb, dkb, dvb))
        dgv = dgains.reshape(-1, HEAD_DIM)
        gs["q_norm_a"][l] = dgv[:2 * n_a].sum(0)
        gs["k_norm_a"][l] = dgv[2 * n_a:4 * n_a].sum(0)
        gs["q_norm_b"][l] = dgv[6 * n_a:6 * n_a + 2 * n_b].sum(0)
        gs["k_norm_b"][l] = dgv[6 * n_a + 2 * n_b:6 * n_a + 2 * n_b + 2 * n_kv].sum(0)
        dwqkv = _mm_tn("dw_qkv", s["h2"], D, draw, QW, 1, False, False, 1.0)
        gw["w_qkv"] = dwqkv.reshape(D, N_CHIPS, -1).transpose(1, 0, 2)
        dx, gs["norm_mix"][l] = _bwd_into_norm("qkv_bwd_b", draw, QW, w["w_qkv"], 1, 1, s["x1"], norm_mix[l:l + 1], dx,
                                               at("before_qkv_bwd", draw, [gw[n] for n in GROUP_REST]))
        dx, gs["norm_ffn1"][l], gw["ffn1_w_in"], gw["ffn1_w_out"] = _ffn_backward(
            dx, s["ffn1"], norm_ffn1[l:l + 1], w["ffn1_w_in"], w["ffn1_w_out"], at("before_ffn1", dx))
        return dx, gw

    out = {}

    def adamw_group(names, l, grads, after):
        for name, g_ in zip(names, grads):
            out[name] = _adamw_layer(given[name], g_, given["m_" + name], given["v_" + name], l, out.get(name), after)
            after = out[name][0]
        return after

    dx, gw1 = layer_backward(NL - 1, dx, {})
    px_1 = _scatter_exchange("1", [gw1[n] for n in BIG])
    rs = {}

    def chips_1(ready):
        rs["chips_1"] = _scatter_chips("1", px_1, ready)
        return rs["chips_1"][3]

    def share_1(ready):
        rs["share_1"] = _scatter_share("1", rs["chips_1"], ready)
        return rs["share_1"][3]

    def exchange_0a(ready, grads):
        rs["px_0a"] = _scatter_exchange("0a", grads)
        return rs["px_0a"][3]

    def chips_0a(ready):
        rs["chips_0a"] = _scatter_chips("0a", rs["px_0a"], ready)
        return rs["chips_0a"][3]

    dx, gw0 = layer_backward(0, dx, {"start": lambda ready: px_1[3], "after_ple": chips_1, "after_attn_a": share_1,
                                     "before_qkv_bwd": exchange_0a, "before_ffn1": chips_0a})
    grad_x = dx.reshape(x.shape)

    px_0b = _scatter_exchange("0b", [gw0[n] for n in GROUP_FFN1])
    d_rel_bias = (_bias_grad(dt_a, bmaps_a, 0) + _bias_grad(dt_b, bmaps_b, n_heads))[:, :rel_bias.shape[1]]
    small_g = [d_rel_bias] + [jnp.stack([t.reshape(-1) for t in gs[name]]) for name in SMALL[1:]]
    g_sum = _allreduce_small(_pack_small(small_g), px_0b[3])
    res = _ew("adamw_small", _adamw_fn,
              [_pack_small([given[n] for n in SMALL]), g_sum, _pack_small([given["m_" + n] for n in SMALL]),
               _pack_small([given["v_" + n] for n in SMALL])], [(LANES, F32)] * 4)
    like = [given[n] for n in SMALL]
    unpacked = [_unpack_small(r, like) for r in res]
    for i, name in enumerate(SMALL):
        out[name] = [u[i] for u in unpacked]

    share_0a = _scatter_share("0a", rs["chips_0a"], res[0])
    g_1 = _scatter_done("1", rs["share_1"], share_0a[3])
    chips_0b = _scatter_chips("0b", px_0b, g_1[0])
    ready = adamw_group(BIG, 1, g_1, chips_0b[3])
    ready = adamw_group(GROUP_REST, 0, _scatter_done("0a", share_0a, ready), ready)
    share_0b = _scatter_share("0b", chips_0b, ready)
    adamw_group(GROUP_FFN1, 0, _scatter_done("0b", share_0b, share_0b[3]), share_0b[3])

    return (loss, grad_x, *[out[n][0] for n in WEIGHTS], *[out[n][1] for n in WEIGHTS],
            *[out[n][2] for n in WEIGHTS], *[out[n][3] for n in WEIGHTS])
```

```python
import functools
import math

import numpy as np
import jax
import jax.numpy as jnp
from jax import lax
from jax.experimental import pallas as pl
from jax.experimental.pallas import tpu as pltpu

F32 = jnp.float32
BF16 = jnp.bfloat16
MESH = pl.DeviceIdType.MESH

HEAD_DIM = 64
LANES = 128
QBLOCK = 128
FWD_UNROLL, BWD_UNROLL = 8, 8
N_BUCKETS = 32
MAX_DISTANCE = 1024
DILATED_CONFIGS = ((128, 1), (512, 4), (2048, 16))
SWA_RADIUS = 128
GROUP_B = 4
EPS = 1e-6
NEG = -1e30
Q_SCALE = HEAD_DIM ** -0.5
ADAM_LR, ADAM_B1, ADAM_B2, ADAM_EPS, ADAM_WD, ADAM_STEP = 0.001, 0.9, 0.999, 1e-08, 0.01, 10
VMEM_LIMIT = 56 * 2 ** 20
N_CHIPS = 4
N_DEV = 8

BIG = ("ffn1_w_in", "ffn1_w_out", "w_qkv", "w_o", "ffn2_w_in", "ffn2_w_out", "w_ple_gate", "w_ple_proj")
SMALL = ("rel_bias", "norm_ffn1", "norm_mix", "q_norm_a", "k_norm_a", "q_norm_b", "k_norm_b", "sink_b",
         "norm_ffn2", "norm_ple")
WEIGHTS = ("rel_bias", "norm_ffn1", "ffn1_w_in", "ffn1_w_out", "norm_mix", "w_qkv", "q_norm_a", "k_norm_a",
           "q_norm_b", "k_norm_b", "sink_b", "w_o", "norm_ffn2", "ffn2_w_in", "ffn2_w_out", "norm_ple",
           "w_ple_gate", "w_ple_proj")


HBM_SPEC = pl.BlockSpec(memory_space=pltpu.HBM)
ANY_SPEC = pl.BlockSpec(memory_space=pl.ANY)
SEM_SPEC = pl.BlockSpec(memory_space=pltpu.SEMAPHORE)


def _params(n_grid):
    return pltpu.CompilerParams(dimension_semantics=("arbitrary",) * n_grid, vmem_limit_bytes=VMEM_LIMIT)


def _pick_tm(rows, cap):
    t = (min(cap, rows) // 16) * 16
    while t >= 16:
        if rows % t == 0:
            return t
        t -= 16
    return rows


def _dot(a, b):
    return jnp.dot(a, b, preferred_element_type=F32)


def _dot_nt(a, b):
    return lax.dot_general(a, b, (((1,), (1,)), ((), ())), preferred_element_type=F32)


def _dot_tn(a, b):
    return lax.dot_general(a, b, (((0,), (0,)), ((), ())), preferred_element_type=F32)


def _sigmoid(z):
    return 1.0 / (1.0 + jnp.exp(-z))


def _lo_lanes(shape):
    return lax.broadcasted_iota(jnp.int32, shape, len(shape) - 1) % LANES < HEAD_DIM


def _seg_sum(blk):
    lo = _lo_lanes(blk.shape)
    s_lo = jnp.sum(jnp.where(lo, blk, 0.0), axis=1, keepdims=True)
    s_hi = jnp.sum(jnp.where(lo, 0.0, blk), axis=1, keepdims=True)
    return jnp.where(lo, s_lo, s_hi)


def _rms_bwd_tile(x, g, dh):
    r = lax.rsqrt(jnp.mean(x * x, axis=-1, keepdims=True) + EPS)
    xh = x * r
    dyg = dh * g
    dx = r * (dyg - xh * jnp.mean(dyg * xh, axis=-1, keepdims=True))
    return dx, jnp.sum(dh * xh, axis=0, keepdims=True)


def _ew(name, fn, ins, out_defs, cap=512):
    rows = ins[0].shape[0]
    tm = _pick_tm(rows, cap)
    n_in = len(ins)

    def body(*refs):
        vals = fn(*[r[...] for r in refs[:n_in]])
        if not isinstance(vals, tuple):
            vals = (vals,)
        for r, v in zip(refs[n_in:], vals):
            r[...] = v.astype(r.dtype)

    return pl.pallas_call(
        body, name=name, grid=(rows // tm,),
        in_specs=[pl.BlockSpec((tm, a.shape[1]), lambda i: (i, 0)) for a in ins],
        out_specs=[pl.BlockSpec((tm, c), lambda i: (i, 0)) for c, _ in out_defs],
        out_shape=[jax.ShapeDtypeStruct((rows, c), dt) for c, dt in out_defs],
        compiler_params=_params(1))(*ins)


def _rms_tile(xv, gv):
    r = lax.rsqrt(jnp.mean(xv * xv, axis=-1, keepdims=True) + EPS)
    return (xv * r * gv).astype(BF16)


def _ffn_up(x, g, win):
    T, D = x.shape
    wc = win.shape[2]
    tm = _pick_tm(T, 512)

    def body(x_ref, g_ref, wg_ref, wu_ref, h_ref, silu_ref, dgate_ref, act_ref, wcat):
        @pl.when(pl.program_id(1) == 0)
        def _():
            wcat[:, :wc] = wg_ref[...]
            wcat[:, wc:] = wu_ref[...]

        hv = _rms_tile(x_ref[...], g_ref[...])
        h_ref[...] = hv
        gu = _dot(hv, wcat[...])
        gte, u = gu[:, :wc], gu[:, wc:]
        sg = _sigmoid(gte)
        silu = gte * sg
        silu_ref[...] = silu.astype(BF16)
        dgate_ref[...] = ((sg + silu * (1.0 - sg)) * u).astype(BF16)
        act_ref[...] = (silu * u).astype(BF16)

    out = jax.ShapeDtypeStruct((T, 2 * wc), BF16)
    ospec = pl.BlockSpec((tm, wc), lambda j, i: (i, j))
    nt = T // tm
    h_spec = pl.BlockSpec((tm, D), lambda j, i: (jnp.where(j == 0, i, nt), 0))
    return pl.pallas_call(
        body, name="ffn_up", grid=(2, nt),
        in_specs=[pl.BlockSpec((tm, D), lambda j, i: (i, 0)), pl.BlockSpec((1, D), lambda j, i: (0, 0)),
                  pl.BlockSpec((None, D, wc), lambda j, i: (j, 0, 0)),
                  pl.BlockSpec((None, D, wc), lambda j, i: (j + 2, 0, 0))],
        out_specs=[h_spec] + [ospec] * 3, out_shape=[jax.ShapeDtypeStruct((T + tm, D), BF16)] + [out] * 3,
        scratch_shapes=[pltpu.VMEM((D, 2 * wc), BF16)], compiler_params=_params(2))(x, g, win, win)


def _mm_res(name, res, a_list, w, scale):
    T, N = res.shape
    n = len(a_list)
    widths = [a.shape[1] for a in a_list]
    tm = _pick_tm(T, 512)

    def body(*refs):
        r_ref, a_refs, w_refs, o_ref = refs[0], refs[1:1 + n], refs[1 + n:1 + 2 * n], refs[1 + 2 * n]
        acc = _dot(a_refs[0][...].astype(BF16), w_refs[0][...])
        for a_ref, w_ref in zip(a_refs[1:], w_refs[1:]):
            acc = acc + _dot(a_ref[...].astype(BF16), w_ref[...])
        o_ref[...] = r_ref[...] + scale * acc

    w_specs, off = [], 0
    for k in widths:
        w_specs.append(pl.BlockSpec((k, N), lambda i, blk=off // k: (blk, 0)))
        off += k
    return pl.pallas_call(
        body, name=name, grid=(T // tm,),
        in_specs=[pl.BlockSpec((tm, N), lambda i: (i, 0))]
        + [pl.BlockSpec((tm, k), lambda i: (i, 0)) for k in widths] + w_specs,
        out_specs=pl.BlockSpec((tm, N), lambda i: (i, 0)),
        out_shape=jax.ShapeDtypeStruct((T, N), F32), compiler_params=_params(1))(res, *a_list, *([w] * n))


def _mm_tn(name, a, a_w, b, b_w, n_slots, a_by_slot, b_by_slot, scale, tm_cap=512):
    T = b.shape[0]
    tm = _pick_tm(T, tm_cap)
    nt = T // tm

    def body(a_ref, b_ref, o_ref, acc):
        i = pl.program_id(1)

        @pl.when(i == 0)
        def _():
            acc[...] = jnp.zeros_like(acc)

        acc[...] += _dot_tn(a_ref[...].astype(BF16), b_ref[...].astype(BF16))

        @pl.when(i == nt - 1)
        def _():
            o_ref[...] = (acc[...] * scale).astype(BF16)

    return pl.pallas_call(
        body, name=name, grid=(n_slots, nt),
        in_specs=[pl.BlockSpec((tm, a_w), (lambda s, i: (i, s)) if a_by_slot else (lambda s, i: (i, 0))),
                  pl.BlockSpec((tm, b_w), (lambda s, i: (i, s)) if b_by_slot else (lambda s, i: (i, 0)))],
        out_specs=pl.BlockSpec((None, a_w, b_w), lambda s, i: (s, 0, 0)),
        out_shape=jax.ShapeDtypeStruct((n_slots, a_w, b_w), BF16),
        scratch_shapes=[pltpu.VMEM((a_w, b_w), F32)], compiler_params=_params(2))(a, b)


def _mm_tn_pairs(name, a, b, b_w, n_slots):
    T = b.shape[0]
    a_w = a.shape[1]
    tm = _pick_tm(T, 512)
    nt = T // tm

    def body(a_ref, b_ref, o_ref, acc):
        i = pl.program_id(1)

        @pl.when(i == 0)
        def _():
            acc[...] = jnp.zeros_like(acc)

        acc[...] += _dot_tn(a_ref[...], b_ref[...])

        @pl.when(i == nt - 1)
        def _():
            o_ref[0] = acc[:, :b_w].astype(BF16)
            o_ref[1] = acc[:, b_w:].astype(BF16)

    return pl.pallas_call(
        body, name=name, grid=(n_slots // 2, nt),
        in_specs=[pl.BlockSpec((tm, a_w), lambda s, i: (i, 0)), pl.BlockSpec((tm, 2 * b_w), lambda s, i: (i, s))],
        out_specs=pl.BlockSpec((2, a_w, b_w), lambda s, i: (s, 0, 0)),
        out_shape=jax.ShapeDtypeStruct((n_slots, a_w, b_w), BF16),
        scratch_shapes=[pltpu.VMEM((a_w, 2 * b_w), F32)], compiler_params=_params(2))(a, b)


def _ffn_bwd(dx, wout, silu, dgate, win, x, g, after):
    T, D = dx.shape
    F = silu.shape[1]
    n_slots, _, wc = win.shape
    tm = _pick_tm(T, 256)

    def body(dx_ref, wout_hbm, s_ref, dgt_ref, win_hbm, x_ref, g_ref, after_ref,
             dgu_ref, dxo_ref, dg_ref, wout_v, wcat, sem):
        @pl.when(pl.program_id(0) == 0)
        def _():
            copies = [pltpu.make_async_copy(wout_hbm, wout_v, sem.at[n_slots])]
            copies += [pltpu.make_async_copy(win_hbm.at[s], wcat.at[s // 2, :, pl.ds((s % 2) * wc, wc)], sem.at[s])
                       for s in range(n_slots)]
            for cp in copies:
                cp.start()
            for cp in copies:
                cp.wait()
            dg_ref[...] = jnp.zeros_like(dg_ref)

        dxv = dx_ref[...]
        dact = 0.5 * _dot_nt(dxv.astype(BF16), wout_v[...])
        d_gate = (dact * dgt_ref[...].astype(F32)).astype(BF16)
        d_up = (dact * s_ref[...].astype(F32)).astype(BF16)
        dgu_ref[:, :F] = d_gate
        dgu_ref[:, F:] = d_up
        dh = _dot_nt(d_gate, wcat[0]) + _dot_nt(d_up, wcat[1])
        dxn, dg = _rms_bwd_tile(x_ref[...], g_ref[...], dh)
        dxo_ref[...] = dxv + dxn
        dg_ref[...] += dg

    row = pl.BlockSpec((tm, D), lambda i: (i, 0))
    vec = pl.BlockSpec((1, D), lambda i: (0, 0))
    act_spec = pl.BlockSpec((tm, F), lambda i: (i, 0))
    return pl.pallas_call(
        body, name="ffn_bwd", grid=(T // tm,),
        in_specs=[row, ANY_SPEC, act_spec, act_spec, ANY_SPEC, row, vec, ANY_SPEC],
        out_specs=[pl.BlockSpec((tm, 2 * F), lambda i: (i, 0)), row, vec],
        out_shape=[jax.ShapeDtypeStruct((T, 2 * F), BF16), jax.ShapeDtypeStruct((T, D), F32),
                   jax.ShapeDtypeStruct((1, D), F32)],
        scratch_shapes=[pltpu.VMEM((F, D), BF16), pltpu.VMEM((n_slots // 2, D, 2 * wc), BF16),
                        pltpu.SemaphoreType.DMA((n_slots + 1,))],
        compiler_params=_params(1))(dx, wout, silu, dgate, win, x, g, after)


def _bwd_into_norm(name, d, d_w, w, n_slots, group, x, g, dx_in, after):
    T, D = x.shape
    tm = _pick_tm(T, 256)
    n_mm = n_slots // group
    k_w = group * d_w

    def body(*refs):
        d_refs, w_hbm = refs[:n_mm], refs[n_mm]
        x_ref, g_ref, dxi_ref, _, dx_ref, dg_ref, wcat, sem = refs[n_mm + 1:]

        @pl.when(pl.program_id(0) == 0)
        def _():
            copies = [pltpu.make_async_copy(w_hbm.at[s] if w.ndim == 3 else w_hbm,
                                            wcat.at[s // group, :, pl.ds((s % group) * d_w, d_w)], sem.at[s])
                      for s in range(n_slots)]
            for cp in copies:
                cp.start()
            for cp in copies:
                cp.wait()
            dg_ref[...] = jnp.zeros_like(dg_ref)

        dh = _dot_nt(d_refs[0][...], wcat[0])
        for m in range(1, n_mm):
            dh = dh + _dot_nt(d_refs[m][...], wcat[m])
        dxn, dg = _rms_bwd_tile(x_ref[...], g_ref[...], dh)
        dx_ref[...] = dxi_ref[...] + dxn
        dg_ref[...] += dg

    row = pl.BlockSpec((tm, D), lambda i: (i, 0))
    vec = pl.BlockSpec((1, D), lambda i: (0, 0))
    return pl.pallas_call(
        body, name=name, grid=(T // tm,),
        in_specs=[pl.BlockSpec((tm, k_w), lambda i, m=m: (i, m)) for m in range(n_mm)]
        + [ANY_SPEC, row, vec, row, ANY_SPEC],
        out_specs=[row, vec],
        out_shape=[jax.ShapeDtypeStruct((T, D), F32), jax.ShapeDtypeStruct((1, D), F32)],
        scratch_shapes=[pltpu.VMEM((n_mm, D, k_w), BF16), pltpu.SemaphoreType.DMA((n_slots,))],
        compiler_params=_params(1))(*([d] * n_mm + [w, x, g, dx_in, after]))


def _ple_fwd(x, g, p, wg, wp, after):
    T, D = x.shape
    P = p.shape[1]
    tm = _pick_tm(T, 256)

    def body(x_ref, g_ref, p_ref, wg_ref, wp_ref, after_ref, hn_ref, xo_ref, gate_ref, pp_ref):
        xv = x_ref[...]
        hn = _rms_tile(xv, g_ref[...])
        hn_ref[...] = hn
        gate = _sigmoid(_dot(hn, wg_ref[...]))
        pp = _dot(p_ref[...].astype(BF16), wp_ref[...])
        gate_ref[...] = gate.astype(BF16)
        pp_ref[...] = pp.astype(BF16)
        xo_ref[...] = xv + gate * pp

    row = pl.BlockSpec((tm, D), lambda i: (i, 0))
    out = jax.ShapeDtypeStruct((T, D), F32)
    half = jax.ShapeDtypeStruct((T, D), BF16)
    return pl.pallas_call(
        body, name="ple_fwd", grid=(T // tm,),
        in_specs=[row, pl.BlockSpec((1, D), lambda i: (0, 0)), pl.BlockSpec((tm, P), lambda i: (i, 0)),
                  pl.BlockSpec((D, D), lambda i: (0, 0)), pl.BlockSpec((P, D), lambda i: (0, 0)), ANY_SPEC],
        out_specs=[row, row, row, row], out_shape=[half, out, half, half],
        compiler_params=_params(1))(x, g, p, wg, wp, after)


def _ple_bwd(dx, gate, pp, hn, p, x, g, wg, after):
    T, D = x.shape
    P = p.shape[1]
    tm = _pick_tm(T, 256)
    nt = T // tm

    def body(dx_ref, gate_ref, pp_ref, hn_ref, p_ref, x_ref, g_ref, wg_ref, after_ref,
             dxo_ref, dg_ref, dwg_ref, dwp_ref, acc_g, acc_p):
        i = pl.program_id(0)

        @pl.when(i == 0)
        def _():
            acc_g[...] = jnp.zeros_like(acc_g)
            acc_p[...] = jnp.zeros_like(acc_p)
            dg_ref[...] = jnp.zeros_like(dg_ref)

        dxv = dx_ref[...]
        gate = gate_ref[...].astype(F32)
        dz = (dxv * pp_ref[...].astype(F32) * gate * (1.0 - gate)).astype(BF16)
        dpp = (dxv * gate).astype(BF16)
        acc_g[...] += _dot_tn(hn_ref[...], dz)
        acc_p[...] += _dot_tn(p_ref[...].astype(BF16), dpp)
        dxn, dg = _rms_bwd_tile(x_ref[...], g_ref[...], _dot_nt(dz, wg_ref[...]))
        dxo_ref[...] = dxv + dxn
        dg_ref[...] += dg

        @pl.when(i == nt - 1)
        def _():
            dwg_ref[...] = acc_g[...].astype(BF16)
            dwp_ref[...] = acc_p[...].astype(BF16)

    row = pl.BlockSpec((tm, D), lambda i: (i, 0))
    vec = pl.BlockSpec((1, D), lambda i: (0, 0))
    return pl.pallas_call(
        body, name="ple_bwd", grid=(nt,),
        in_specs=[row, row, row, row, pl.BlockSpec((tm, P), lambda i: (i, 0)), row, vec,
                  pl.BlockSpec((D, D), lambda i: (0, 0)), ANY_SPEC],
        out_specs=[row, vec, pl.BlockSpec((D, D), lambda i: (0, 0)), pl.BlockSpec((P, D), lambda i: (0, 0))],
        out_shape=[jax.ShapeDtypeStruct((T, D), F32), jax.ShapeDtypeStruct((1, D), F32),
                   jax.ShapeDtypeStruct((D, D), BF16), jax.ShapeDtypeStruct((P, D), BF16)],
        scratch_shapes=[pltpu.VMEM((D, D), F32), pltpu.VMEM((P, D), F32)],
        compiler_params=_params(1))(dx, gate, pp, hn, p, x, g, wg, after)


def _loss_fwd_bwd(y, tgt):
    T, D = y.shape
    tm = _pick_tm(T, 512)

    def body(y_ref, t_ref, dy_ref, loss_ref):
        e = y_ref[...] - t_ref[...]
        dy_ref[...] = e / D

        @pl.when(pl.program_id(0) == 0)
        def _():
            loss_ref[...] = jnp.zeros_like(loss_ref)

        loss_ref[...] += 0.5 * jnp.sum(jnp.mean(e * e, axis=-1, keepdims=True), axis=0, keepdims=True)

    row = pl.BlockSpec((tm, D), lambda i: (i, 0))
    return pl.pallas_call(
        body, name="loss", grid=(T // tm,), in_specs=[row, row],
        out_specs=[row, pl.BlockSpec((8, LANES), lambda i: (0, 0))],
        out_shape=[jax.ShapeDtypeStruct((T, D), F32), jax.ShapeDtypeStruct((8, LANES), F32)],
        compiler_params=_params(1))(y, tgt)


def _qkv_layout(D):
    n_a = D // (2 * LANES)
    n_b = D // (2 * LANES)
    n_kv = max(1, (2 * n_b) // GROUP_B) * HEAD_DIM // LANES
    return n_a, n_b, n_kv


def _dup_half(xv, half):
    rolled = pltpu.roll(xv, HEAD_DIM, 1)
    lo = _lo_lanes(xv.shape)
    return jnp.where(lo, xv, rolled) if half == 0 else jnp.where(lo, rolled, xv)


def _qkv_proj(x, g_mix, w, gains):
    T, D = x.shape
    W = w.shape[1]
    n_a, n_b, n_kv = _qkv_layout(D)
    tm = _pick_tm(T, 256)
    o_qb = 3 * n_a

    def norm(xv, gv, scale):
        ms = _seg_sum(xv * xv) * (1.0 / HEAD_DIM)
        return xv * lax.rsqrt(ms + EPS) * gv * scale

    def body(x_ref, gm_ref, w_ref, g_ref, h_ref, raw_ref, a_ref, b_ref):
        hv = _rms_tile(x_ref[...], gm_ref[...])
        h_ref[...] = hv
        raw = _dot(hv, w_ref[...])
        raw_ref[...] = raw

        def blk(cb):
            return raw[:, cb * LANES:(cb + 1) * LANES]

        def gn(cb):
            return g_ref[:, cb * LANES:(cb + 1) * LANES]

        for cb in range(n_a):
            a_ref[:, cb * LANES:(cb + 1) * LANES] = norm(blk(cb), gn(cb), Q_SCALE)
            cbk = n_a + cb
            a_ref[:, cbk * LANES:(cbk + 1) * LANES] = norm(blk(cbk), gn(cbk), 1.0)
            cbv = 2 * n_a + cb
            a_ref[:, cbv * LANES:(cbv + 1) * LANES] = blk(cbv)
        for cb in range(n_b):
            src = o_qb + cb
            b_ref[:, cb * LANES:(cb + 1) * LANES] = norm(blk(src), gn(src), Q_SCALE)
        for e in range(n_b):
            kvh = (2 * e) // GROUP_B
            ck = o_qb + n_b + kvh // 2
            cv = ck + n_kv
            kn = norm(blk(ck), gn(ck), 1.0)
            b_ref[:, (n_b + e) * LANES:(n_b + e + 1) * LANES] = _dup_half(kn, kvh % 2)
            b_ref[:, (2 * n_b + e) * LANES:(2 * n_b + e + 1) * LANES] = _dup_half(blk(cv), kvh % 2)

    wa, wb = 3 * n_a * LANES, 3 * n_b * LANES

    def rows(width):
        return pl.BlockSpec((tm, width), lambda i: (i, 0))

    return pl.pallas_call(
        body, name="qkv_proj", grid=(T // tm,),
        in_specs=[rows(D), pl.BlockSpec((1, D), lambda i: (0, 0)), pl.BlockSpec((D, W), lambda i: (0, 0)),
                  pl.BlockSpec((1, W), lambda i: (0, 0))],
        out_specs=[rows(D), rows(W), rows(wa), rows(wb)],
        out_shape=[jax.ShapeDtypeStruct((T, D), BF16), jax.ShapeDtypeStruct((T, W), F32),
                   jax.ShapeDtypeStruct((T, wa), F32), jax.ShapeDtypeStruct((T, wb), F32)],
        compiler_params=_params(1))(x, g_mix, w, gains)


def _qkv_post_bwd(raw, gains, d_a, d_b):
    T, W = raw.shape
    n_a, n_b, n_kv = _qkv_layout(W * 4 // 9)
    tm = _pick_tm(T, 256)
    o_qb = 3 * n_a

    def body(raw_ref, g_ref, daq, dak, dav, dbq, dbk, dbv, o_ref, dg_ref):
        @pl.when(pl.program_id(0) == 0)
        def _():
            dg_ref[...] = jnp.zeros_like(dg_ref)

        def cols(ref, cb):
            return ref[:, cb * LANES:(cb + 1) * LANES]

        def norm_bwd(cb, dy, scale):
            xv = cols(raw_ref, cb)
            gv = cols(g_ref, cb)
            r = lax.rsqrt(_seg_sum(xv * xv) * (1.0 / HEAD_DIM) + EPS)
            xh = xv * r
            dys = dy * scale
            dyg = dys * gv
            dxv = r * (dyg - xh * (_seg_sum(dyg * xh) * (1.0 / HEAD_DIM)))
            o_ref[:, cb * LANES:(cb + 1) * LANES] = dxv.astype(BF16)
            dg_ref[:, cb * LANES:(cb + 1) * LANES] += jnp.sum(dys * xh, axis=0, keepdims=True)

        def fold(ref, kv_blk):
            halves = []
            for half in range(2):
                kvh = 2 * kv_blk + half
                blocks = [e for e in range(n_b) if (2 * e) // GROUP_B == kvh]
                s = cols(ref, blocks[0])
                for e in blocks[1:]:
                    s = s + cols(ref, e)
                halves.append(s + pltpu.roll(s, HEAD_DIM, 1))
            return jnp.where(_lo_lanes(halves[0].shape), halves[0], halves[1])

        for cb in range(n_a):
            norm_bwd(cb, cols(daq, cb), Q_SCALE)
            norm_bwd(n_a + cb, cols(dak, cb), 1.0)
            cbv = 2 * n_a + cb
            o_ref[:, cbv * LANES:(cbv + 1) * LANES] = cols(dav, cb).astype(BF16)
        for cb in range(n_b):
            norm_bwd(o_qb + cb, cols(dbq, cb), Q_SCALE)
        for kb in range(n_kv):
            ck = o_qb + n_b + kb
            cv = ck + n_kv
            norm_bwd(ck, fold(dbk, kb), 1.0)
            o_ref[:, cv * LANES:(cv + 1) * LANES] = fold(dbv, kb).astype(BF16)

    hw_a, hw_b = n_a * LANES, n_b * LANES
    return pl.pallas_call(
        body, name="qkv_post_bwd", grid=(T // tm,),
        in_specs=[pl.BlockSpec((tm, W), lambda i: (i, 0)), pl.BlockSpec((1, W), lambda i: (0, 0))]
        + [pl.BlockSpec((tm, hw_a), lambda i: (i, 0))] * 3 + [pl.BlockSpec((tm, hw_b), lambda i: (i, 0))] * 3,
        out_specs=[pl.BlockSpec((tm, W), lambda i: (i, 0)), pl.BlockSpec((1, W), lambda i: (0, 0))],
        out_shape=[jax.ShapeDtypeStruct((T, W), BF16), jax.ShapeDtypeStruct((1, W), F32)],
        compiler_params=_params(1))(raw, gains, *d_a, *d_b)


def _t5_bucket_np(rel):
    half = N_BUCKETS // 2
    max_exact = half // 2
    ret = np.where(rel > 0, half, 0)
    n = np.abs(rel)
    nf = np.maximum(n, 1).astype(np.float32)
    large = max_exact + (np.log(nf / np.float32(max_exact)) / np.float32(math.log(MAX_DISTANCE / max_exact))
                         * np.float32(half - max_exact)).astype(np.int32)
    large = np.minimum(large, half - 1)
    return ret + np.where(n < max_exact, n, large)


def _window_pad(radius):
    assert radius <= QBLOCK
    return HEAD_DIM if radius <= HEAD_DIM else QBLOCK


def _bucket_maps(configs):
    pad = _window_pad(configs[0][0])
    q = np.arange(QBLOCK)[:, None]
    kk = np.arange(QBLOCK + 2 * pad)[None, :]
    rel = kk - pad - q
    maps = [np.where(np.abs(rel) <= radius, _t5_bucket_np(rel * dil), -1) for radius, dil in configs]
    return np.stack(maps).astype(np.int32)


def _bias_build(rel_bias, bmaps, col0):
    n_sets, _, W = bmaps.shape
    n_heads = rel_bias.shape[1] // 2

    def body(rb_ref, bm_ref, o_ref):
        h = pl.program_id(1)
        bm = bm_ref[...]

        def step(n, acc):
            return jnp.where(bm == n, rb_ref[n, col0 + h], acc)

        o_ref[...] = lax.fori_loop(0, N_BUCKETS, step, jnp.where(bm < 0, NEG, 0.0).astype(F32))

    return pl.pallas_call(
        body, name="bias_build", grid=(n_sets, n_heads),
        in_specs=[pl.BlockSpec(memory_space=pltpu.SMEM), pl.BlockSpec((None, QBLOCK, W), lambda s, h: (s, 0, 0))],
        out_specs=pl.BlockSpec((None, None, QBLOCK, W), lambda s, h: (s, h, 0, 0)),
        out_shape=jax.ShapeDtypeStruct((n_sets, n_heads, QBLOCK, W), F32),
        compiler_params=_params(2))(rel_bias, bmaps)


def _bias_grad(dtiles, bmaps, col0):
    n_sets, _, W = bmaps.shape
    n_heads = dtiles[0].shape[1]
    n_l = len(dtiles)

    def body(*refs):
        bm_ref, o_ref = refs[0], refs[1 + n_l]
        s, h = pl.program_id(0), pl.program_id(1)

        @pl.when((s == 0) & (h == 0))
        def _():
            o_ref[...] = jnp.zeros_like(o_ref)

        d = refs[1][...]
        for r in refs[2:1 + n_l]:
            d = d + r[...]
        acc8 = d[0:8, :]
        for a in range(1, QBLOCK // 8):
            acc8 = acc8 + pltpu.roll(d[8 * a:8 * a + 8, :], W - 8 * a, 1)
        per_offset = acc8[0:1, :]
        for b in range(1, 8):
            per_offset = per_offset + pltpu.roll(acc8[b:b + 1, :], W - b, 1)
        bucket = lax.broadcasted_iota(jnp.int32, (N_BUCKETS, W), 0)
        hit = bucket == bm_ref[0:1, :]
        per_bucket = jnp.sum(jnp.where(hit, per_offset, 0.0), axis=1, keepdims=True)
        lanes = lax.broadcasted_iota(jnp.int32, o_ref.shape, 1)
        o_ref[...] += jnp.where(lanes == col0 + h, per_bucket, 0.0)

    tile = pl.BlockSpec((None, None, QBLOCK, W), lambda s, h: (s, h, 0, 0))
    return pl.pallas_call(
        body, name="bias_grad", grid=(n_sets, n_heads),
        in_specs=[pl.BlockSpec((None, QBLOCK, W), lambda s, h: (s, 0, 0))] + [tile] * n_l,
        out_specs=pl.BlockSpec((N_BUCKETS, LANES), lambda s, h: (0, 0)),
        out_shape=jax.ShapeDtypeStruct((N_BUCKETS, LANES), F32), compiler_params=_params(2))(bmaps, *dtiles)


def _rows(l_start, n, d, r):
    if d == 1:
        return pl.ds(pl.multiple_of(l_start, 8), n)
    return pl.ds(l_start * d + r, n, stride=d)


def _stack_heads(xv, lo):
    z = jnp.zeros_like(xv)
    return jnp.concatenate([jnp.where(lo, xv, z), jnp.where(lo, z, xv)], axis=0)


def _unstack_heads(xv, lo):
    return jnp.where(lo, xv[:QBLOCK], xv[QBLOCK:])


def _per_head_rows(v0, v1):
    if jnp.ndim(v0) == 0:
        return jnp.where(lax.broadcasted_iota(jnp.int32, (2 * QBLOCK, 1), 0) < QBLOCK, v0, v1)
    return jnp.concatenate([v0, v1], axis=0)


def _block_geometry(b, nb_sub, pad):
    r, lb = b // nb_sub, b % nb_sub
    l0 = lb * QBLOCK
    lp = jnp.maximum(l0 - pad, 0)
    ln = jnp.minimum(l0 + QBLOCK, nb_sub * QBLOCK - pad)
    return r, l0, lp, ln, (lb == 0).astype(jnp.int32) + 2 * (lb == nb_sub - 1).astype(jnp.int32)


def _edge_variants(bias, pad):
    n_br, _, _, W = bias.shape
    col = np.arange(W)
    left, right = col < pad, col >= pad + QBLOCK
    masked = jnp.asarray(np.stack([np.zeros(W, bool), left, right, left | right]))
    return jnp.where(masked[None, :, None, :], NEG, bias.reshape(n_br, 1, -1, W))


def _window(ref, l0, lp, ln, pad, d, r):
    return jnp.concatenate([ref[_rows(lp, pad, d, r), :], ref[_rows(l0, QBLOCK, d, r), :],
                            ref[_rows(ln, pad, d, r), :]], axis=0)


def _attn_fwd(qkv, bias, sink, dils, pad):
    T = qkv.shape[0]
    hw = qkv.shape[1] // 3
    ng = hw // LANES
    n_br = len(dils)
    n_blocks = T // QBLOCK
    W = QBLOCK + 2 * pad
    chunk = 256

    def body(sink_ref, q_ref, k_ref, v_ref, bias_ref, o_ref, lse_ref, *scratch):
        g = pl.program_id(0)
        lo = _lo_lanes((QBLOCK, LANES))
        snk = _per_head_rows(sink_ref[2 * g], sink_ref[2 * g + 1])
        for c, d in enumerate(dils):
            nb_sub = n_blocks // d
            o_dst = scratch[0].at[c] if n_br > 1 else o_ref
            l_dst = scratch[1].at[c] if n_br > 1 else lse_ref

            def block(b, carry, c=c, d=d, nb_sub=nb_sub, o_dst=o_dst, l_dst=l_dst):
                r, l0, lp, ln, edge = _block_geometry(b, nb_sub, pad)
                q = _stack_heads(q_ref[_rows(l0, QBLOCK, d, r), :].astype(BF16), lo)
                k = _window(k_ref, l0, lp, ln, pad, d, r).astype(BF16)
                v = _window(v_ref, l0, lp, ln, pad, d, r).astype(BF16)
                s = _dot_nt(q, k) + bias_ref[c, edge]
                m = jnp.maximum(jnp.max(s, axis=1, keepdims=True), snk)
                p = jnp.exp(s - m)
                den = jnp.sum(p, axis=1, keepdims=True) + jnp.exp(snk - m)
                o_dst[_rows(l0, QBLOCK, d, r), :] = _unstack_heads(_dot(p.astype(BF16), v) / den, lo)
                l_dst[_rows(l0, QBLOCK, d, r), :] = _unstack_heads(
                    jnp.broadcast_to(m + jnp.log(den), (2 * QBLOCK, LANES)), lo)
                return carry

            lax.fori_loop(0, n_blocks, block, 0, unroll=FWD_UNROLL)

        if n_br > 1:
            def merge(i, carry):
                rs = pl.ds(pl.multiple_of(i * chunk, chunk), chunk)
                ls = [scratch[1][c, rs, :] for c in range(n_br)]
                m = ls[0]
                for t in ls[1:]:
                    m = jnp.maximum(m, t)
                ws = [jnp.exp(t - m) for t in ls]
                z = ws[0]
                acc = ws[0] * scratch[0][0, rs, :]
                for c in range(1, n_br):
                    z = z + ws[c]
                    acc = acc + ws[c] * scratch[0][c, rs, :]
                o_ref[rs, :] = acc / z
                lse_ref[rs, :] = m + jnp.log(z)
                return carry

            lax.fori_loop(0, T // chunk, merge, 0)

    def col(base):
        return pl.BlockSpec((T, LANES), lambda g: (0, base + g))

    out = jax.ShapeDtypeStruct((T, hw), F32)
    scratch = [pltpu.VMEM((n_br, T, LANES), F32)] * 2 if n_br > 1 else []
    return pl.pallas_call(
        body, name="attn_fwd", grid=(ng,),
        in_specs=[pl.BlockSpec(memory_space=pltpu.SMEM), col(0), col(ng), col(2 * ng),
                  pl.BlockSpec((n_br, 4, 2 * QBLOCK, W), lambda g: (0, 0, g, 0))],
        out_specs=[col(0), col(0)], out_shape=[out, out], scratch_shapes=scratch,
        compiler_params=_params(1))(sink, qkv, qkv, qkv, bias)


def _attn_bwd(qkv, bias, sink, dils, pad, do, lse, dd, col_base, after):
    T = qkv.shape[0]
    hw = qkv.shape[1] // 3
    ng = hw // LANES
    n_br = len(dils)
    n_blocks = T // QBLOCK
    W = QBLOCK + 2 * pad

    def body(sink_ref, q_ref, k_ref, v_ref, bias_ref, do_ref, lse_ref, dd_ref, after_ref,
             dq_ref, dk_ref, dv_ref, dt_ref, ds_ref):
        g = pl.program_id(0)
        dq_ref[...] = jnp.zeros_like(dq_ref)
        dk_ref[...] = jnp.zeros_like(dk_ref)
        dv_ref[...] = jnp.zeros_like(dv_ref)
        dt_ref[...] = jnp.zeros_like(dt_ref)
        ds_ref[...] = jnp.zeros_like(ds_ref)
        lo = _lo_lanes((QBLOCK, LANES))
        snk = jnp.where(lo, sink_ref[2 * g], sink_ref[2 * g + 1])
        for c, d in enumerate(dils):
            nb_sub = n_blocks // d

            def block(b, carry, c=c, d=d, nb_sub=nb_sub):
                r, l0, lp, ln, edge = _block_geometry(b, nb_sub, pad)
                rows_q = _rows(l0, QBLOCK, d, r)
                q = _stack_heads(q_ref[rows_q, :].astype(BF16), lo)
                k = _window(k_ref, l0, lp, ln, pad, d, r).astype(BF16)
                v = _window(v_ref, l0, lp, ln, pad, d, r).astype(BF16)
                dob = _stack_heads(do_ref[rows_q, :].astype(BF16), lo)
                lse_b = lse_ref[rows_q, :]
                dd_b = dd_ref[rows_q, :]
                s = _dot_nt(q, k) + bias_ref[c, edge]
                p = jnp.exp(s - _per_head_rows(lse_b[:, 0:1], lse_b[:, HEAD_DIM:HEAD_DIM + 1]))
                ds = p * (_dot_nt(dob, v) - _per_head_rows(dd_b[:, 0:1], dd_b[:, HEAD_DIM:HEAD_DIM + 1]))
                dsb = ds.astype(BF16)
                dkw = _dot_tn(dsb, q)
                dvw = _dot_tn(p.astype(BF16), dob)
                dt_ref[c] += ds
                dq_ref[rows_q, :] += _unstack_heads(_dot(dsb, k), lo)
                ds_ref[0:1, :] += jnp.sum(-jnp.exp(snk - lse_b) * dd_b, axis=0, keepdims=True)
                for part, (start, n) in zip((0, pad, pad + QBLOCK), ((lp, pad), (l0, QBLOCK), (ln, pad))):
                    dk_ref[_rows(start, n, d, r), :] += dkw[part:part + n]
                    dv_ref[_rows(start, n, d, r), :] += dvw[part:part + n]
                return carry

            lax.fori_loop(0, n_blocks, block, 0, unroll=BWD_UNROLL)

    def col(base):
        return pl.BlockSpec((T, LANES), lambda g: (0, base + g))

    tile = pl.BlockSpec((n_br, 2 * QBLOCK, W), lambda g: (0, g, 0))
    full = jax.ShapeDtypeStruct((T, hw), F32)
    dq, dk, dv, dt, dsink = pl.pallas_call(
        body, name="attn_bwd", grid=(ng,),
        in_specs=[pl.BlockSpec(memory_space=pltpu.SMEM), col(0), col(ng), col(2 * ng),
                  pl.BlockSpec((n_br, 4, 2 * QBLOCK, W), lambda g: (0, 0, g, 0)),
                  col(col_base), col(0), col(col_base), ANY_SPEC],
        out_specs=[col(0), col(0), col(0), tile, pl.BlockSpec((None, 8, LANES), lambda g: (g, 0, 0))],
        out_shape=[full, full, full, jax.ShapeDtypeStruct((n_br, 2 * ng * QBLOCK, W), F32),
                   jax.ShapeDtypeStruct((ng, 8, LANES), F32)],
        compiler_params=_params(1))(sink, qkv, qkv, qkv, bias, do, lse, dd, after)
    return dq, dk, dv, dt.reshape(n_br, 2 * ng, QBLOCK, W), dsink


def _mix_bwd_in(dx, wo, o_list):
    T, D = dx.shape
    widths = [o.shape[1] for o in o_list]
    hw = sum(widths)
    n = len(o_list)
    tm = _pick_tm(T, 256)

    def body(*refs):
        dx_ref, w_ref, o_refs, do_ref, dd_ref = refs[0], refs[1], refs[2:2 + n], refs[2 + n], refs[3 + n]
        dov = _dot_nt(dx_ref[...].astype(BF16), w_ref[...])
        do_ref[...] = dov
        off = 0
        for o_ref, k in zip(o_refs, widths):
            prod = dov[:, off:off + k] * o_ref[...]
            for cb in range(k // LANES):
                dd_ref[:, off + cb * LANES:off + (cb + 1) * LANES] = _seg_sum(prod[:, cb * LANES:(cb + 1) * LANES])
            off += k

    row = pl.BlockSpec((tm, hw), lambda i: (i, 0))
    out = jax.ShapeDtypeStruct((T, hw), F32)
    return pl.pallas_call(
        body, name="mix_bwd_in", grid=(T // tm,),
        in_specs=[pl.BlockSpec((tm, D), lambda i: (i, 0)), pl.BlockSpec((hw, D), lambda i: (0, 0))]
        + [pl.BlockSpec((tm, k), lambda i: (i, 0)) for k in widths],
        out_specs=[row, row], out_shape=[out, out], compiler_params=_params(1))(dx, wo, *o_list)


def _mesh_pos():
    return lax.axis_index("x"), lax.axis_index("y"), lax.axis_index("c")


def _my_chip():
    return 2 * lax.axis_index("x") + lax.axis_index("y")


def _other_chips(x, y):
    return [(1 - x, y), (x, 1 - y), (1 - x, 1 - y)]


def _half_rows(rows, cc):
    hr = rows // 2
    return pl.ds(pl.multiple_of(cc * hr, 16), hr)


def _cast_into_slot(w, l):
    _, R, C = w.shape
    tm = _pick_tm(R, 512)

    def body(w_ref, o_ref):
        o_ref[...] = w_ref[...].astype(BF16)

    return pl.pallas_call(
        body, name="cast_into_slot", grid=(R // tm,),
        in_specs=[pl.BlockSpec((None, tm, C), lambda i: (l, i, 0))],
        out_specs=pl.BlockSpec((None, tm, C), lambda i: (_my_chip(), i, 0)),
        out_shape=jax.ShapeDtypeStruct((N_CHIPS, R, C), BF16), compiler_params=_params(1))(w)


def _split_start(name, arrays, make_copies, n_sem, after):
    n = len(arrays)
    n_in = n + (0 if after is None else 1)

    def body(*refs):
        send_s, recv_s = refs[n_in], refs[n_in + 1]
        token = refs[n_in + 2 + n]
        for send, _ in make_copies(refs[:n], send_s, recv_s):
            send.start()
        token[...] = jnp.zeros_like(token)

    res = pl.pallas_call(
        body, name=name,
        out_shape=(pltpu.SemaphoreType.DMA((n_sem,)), pltpu.SemaphoreType.DMA((n_sem,)),
                   *[pltpu.HBM(a.shape, a.dtype) for a in arrays], jax.ShapeDtypeStruct((8, LANES), F32)),
        in_specs=[HBM_SPEC] * n + [ANY_SPEC] * (n_in - n),
        out_specs=(SEM_SPEC, SEM_SPEC, *([HBM_SPEC] * n), pl.BlockSpec(memory_space=pltpu.VMEM)),
        input_output_aliases={i: 2 + i for i in range(n)},
        compiler_params=pltpu.CompilerParams(has_side_effects=pltpu.SideEffectType.DATAFLOW_SIDE_EFFECTING),
    )(*[pltpu.with_memory_space_constraint(a, pltpu.HBM) for a in arrays], *([] if after is None else [after]))
    return res[0], res[1], list(res[2:2 + n]), res[2 + n]


def _split_wait(name, send_s, recv_s, arrays, make_copies, after):
    n = len(arrays)

    def body(*refs):
        for send, landed in make_copies(refs[:n], refs[n], refs[n + 1]):
            send.wait_send()
            landed.wait_recv()

    return list(pl.pallas_call(
        body, name=name, out_shape=[pltpu.HBM(a.shape, a.dtype) for a in arrays],
        in_specs=[HBM_SPEC] * n + [SEM_SPEC, SEM_SPEC, ANY_SPEC], out_specs=[HBM_SPEC] * n,
        input_output_aliases={i: i for i in range(n)},
        compiler_params=pltpu.CompilerParams(has_side_effects=pltpu.SideEffectType.DATAFLOW_SIDE_EFFECTING),
    )(*arrays, send_s, recv_s, after))


def _gather_copies(shapes):
    n = len(shapes)

    def make(refs, send_s, recv_s):
        x, y, c = _mesh_pos()
        my = 2 * x + y
        copies = []
        for w in range(n):
            for k, (px, py) in enumerate(_other_chips(x, y)):
                def part(slot, w=w):
                    return refs[w].at[slot, _half_rows(shapes[w][1], c), :]
                sems = dict(send_sem=send_s.at[k * n + w], recv_sem=recv_s.at[k * n + w],
                            device_id=(px, py, c), device_id_type=MESH)
                copies.append((pltpu.make_async_remote_copy(src_ref=part(my), dst_ref=part(my), **sems),
                               pltpu.make_async_remote_copy(src_ref=part(2 * px + py), dst_ref=part(2 * px + py), **sems)))
        return copies

    return make


def _forward_copies(shapes):
    n = len(shapes)

    def make(refs, send_s, recv_s):
        x, y, c = _mesh_pos()
        copies = []
        for w in range(n):
            for k, (px, py) in enumerate(_other_chips(x, y)):
                def part(cc, w=w, slot=2 * px + py):
                    return refs[w].at[slot, _half_rows(shapes[w][1], cc), :]
                sems = dict(send_sem=send_s.at[k * n + w], recv_sem=recv_s.at[k * n + w],
                            device_id=(x, y, 1 - c), device_id_type=MESH)
                copies.append((pltpu.make_async_remote_copy(src_ref=part(c), dst_ref=part(c), **sems),
                               pltpu.make_async_remote_copy(src_ref=part(1 - c), dst_ref=part(1 - c), **sems)))
        return copies

    return make


def _pair_forward(bufs):
    n = len(bufs)
    make = _forward_copies([b.shape for b in bufs])

    def body(*refs):
        copies = make(refs[n:2 * n], refs[2 * n], refs[2 * n + 1])
        for send, _ in copies:
            send.start()
        for _, landed in copies:
            landed.wait_recv()
        for send, _ in copies:
            send.wait_send()

    return list(pl.pallas_call(
        body, name="ag_pair_forward", in_specs=[HBM_SPEC] * n, out_specs=[HBM_SPEC] * n,
        out_shape=[jax.ShapeDtypeStruct(b.shape, b.dtype) for b in bufs],
        input_output_aliases={w: w for w in range(n)},
        scratch_shapes=[pltpu.SemaphoreType.DMA((3 * n,)), pltpu.SemaphoreType.DMA((3 * n,))],
    )(*bufs))


def _pair_exchange_copies(shapes):
    n = len(shapes)

    def make(refs, send_s, recv_s):
        x, y, c = _mesh_pos()
        copies = []
        for t in range(n):
            sems = dict(send_sem=send_s.at[t], recv_sem=recv_s.at[t], device_id=(x, y, 1 - c), device_id_type=MESH)
            land = refs[n + t]
            copies.append((pltpu.make_async_remote_copy(
                src_ref=refs[t].at[:, _half_rows(shapes[t][1], 1 - c), :], dst_ref=land, **sems),
                pltpu.make_async_remote_copy(src_ref=land, dst_ref=land, **sems)))
        return copies

    return make


def _pair_share_copies(shapes):
    n = len(shapes)

    def make(refs, send_s, recv_s):
        x, y, c = _mesh_pos()
        copies = []
        for t in range(n):
            def half(cc, t=t):
                return refs[t].at[_half_rows(shapes[t][0], cc), :]
            sems = dict(send_sem=send_s.at[t], recv_sem=recv_s.at[t], device_id=(x, y, 1 - c), device_id_type=MESH)
            copies.append((pltpu.make_async_remote_copy(src_ref=half(c), dst_ref=half(c), **sems),
                           pltpu.make_async_remote_copy(src_ref=half(1 - c), dst_ref=half(1 - c), **sems)))
        return copies

    return make


def _rs_add_pair(grad, recv):
    n_slot, hr, C = recv.shape
    tm = _pick_tm(hr, 192)
    nb = hr // tm

    def body(a_ref, b_ref, o_ref):
        o_ref[...] = (a_ref[...].astype(F32) + b_ref[...].astype(F32)).astype(BF16)

    blk = pl.BlockSpec((n_slot, tm, C), lambda i: (0, i, 0))
    return pl.pallas_call(
        body, name="rs_add_pair", grid=(nb,),
        in_specs=[pl.BlockSpec((n_slot, tm, C), lambda i: (0, lax.axis_index("c") * nb + i, 0)), blk],
        out_specs=blk, out_shape=jax.ShapeDtypeStruct(recv.shape, BF16), compiler_params=_params(1))(grad, recv)


def _scatter_copies(n):
    def make(refs, send_s, recv_s):
        x, y, c = _mesh_pos()
        copies = []
        for t in range(n):
            for k, (px, py) in enumerate(_other_chips(x, y)):
                sems = dict(send_sem=send_s.at[3 * t + k], recv_sem=recv_s.at[3 * t + k],
                            device_id=(px, py, c), device_id_type=MESH)
                land = refs[n + t].at[k]
                copies.append((pltpu.make_async_remote_copy(src_ref=refs[t].at[2 * px + py], dst_ref=land, **sems),
                               pltpu.make_async_remote_copy(src_ref=land, dst_ref=land, **sems)))
        return copies

    return make


def _rs_add_chips(part, recv):
    _, hr, C = part.shape
    tm = _pick_tm(hr, 256)
    nb = hr // tm

    def body(a_ref, r0, r1, r2, o_ref):
        o_ref[...] = ((a_ref[...].astype(F32) + r0[...].astype(F32)) + r1[...].astype(F32)) + r2[...].astype(F32)

    def rel(k):
        return pl.BlockSpec((None, tm, C), lambda i: (k, i, 0))

    return pl.pallas_call(
        body, name="rs_add_chips", grid=(nb,),
        in_specs=[pl.BlockSpec((None, tm, C), lambda i: (_my_chip(), i, 0)), rel(0), rel(1), rel(2)],
        out_specs=pl.BlockSpec((tm, C), lambda i: (lax.axis_index("c") * nb + i, 0)),
        out_shape=jax.ShapeDtypeStruct((2 * hr, C), F32), compiler_params=_params(1))(part, recv, recv, recv)


def _allreduce_small(v, after):
    rows = v.shape[0]

    def body(v_ref, after_ref, o_ref, buf, send_s, recv_s):
        x, y, c = _mesh_pos()
        me = 4 * x + 2 * y + c
        buf[me] = v_ref[...]
        copies = []
        for r in range(1, N_DEV):
            px = 1 - x if r & 4 else x
            py = 1 - y if r & 2 else y
            pc = 1 - c if r & 1 else c
            send = pltpu.make_async_remote_copy(
                src_ref=v_ref, dst_ref=buf.at[me], send_sem=send_s.at[r - 1], recv_sem=recv_s.at[r - 1],
                device_id=(px, py, pc), device_id_type=MESH)
            peer_slot = buf.at[4 * px + 2 * py + pc]
            landed = pltpu.make_async_remote_copy(
                src_ref=peer_slot, dst_ref=peer_slot, send_sem=send_s.at[r - 1], recv_sem=recv_s.at[r - 1],
                device_id=(px, py, pc), device_id_type=MESH)
            copies.append((send, landed))
        for send, _ in copies:
            send.start()
        for _, landed in copies:
            landed.wait_recv()
        for send, _ in copies:
            send.wait_send()
        acc = buf[0]
        for j in range(1, N_DEV):
            acc = acc + buf[j]
        o_ref[...] = acc

    vm = pl.BlockSpec(memory_space=pltpu.VMEM)
    return pl.pallas_call(
        body, name="allreduce_small", in_specs=[vm, ANY_SPEC], out_specs=vm,
        out_shape=jax.ShapeDtypeStruct((rows, LANES), F32),
        scratch_shapes=[pltpu.VMEM((N_DEV, rows, LANES), F32), pltpu.SemaphoreType.DMA((N_DEV - 1,)),
                        pltpu.SemaphoreType.DMA((N_DEV - 1,))],
    )(v, after)


def _adamw_fn(w, g, m, v):
    m2 = ADAM_B1 * m + (1.0 - ADAM_B1) * g
    v2 = ADAM_B2 * v + (1.0 - ADAM_B2) * (g * g)
    m_hat = m2 / (1.0 - ADAM_B1 ** ADAM_STEP)
    v_hat = v2 / (1.0 - ADAM_B2 ** ADAM_STEP)
    delta = -ADAM_LR * (m_hat / (jnp.sqrt(v_hat) + ADAM_EPS) + ADAM_WD * w)
    return g, delta, m2, v2


def _adamw_layer(w, g, m, v, l, prev, after):
    NL, R, C = w.shape
    tm = _pick_tm(R, 256)
    n_prev = 0 if prev is None else 4

    def body(w_ref, g_ref, m_ref, v_ref, after_ref, *rest):
        outs = rest[n_prev:]
        for o_ref, val in zip(outs, _adamw_fn(w_ref[...], g_ref[...], m_ref[...], v_ref[...])):
            o_ref[...] = val

    lay = pl.BlockSpec((None, tm, C), lambda i: (l, i, 0))
    shape = jax.ShapeDtypeStruct((NL, R, C), F32)
    return pl.pallas_call(
        body, name="adamw", grid=(R // tm,),
        in_specs=[lay, pl.BlockSpec((tm, C), lambda i: (i, 0)), lay, lay, ANY_SPEC] + [ANY_SPEC] * n_prev,
        out_specs=[lay] * 4, out_shape=[shape] * 4,
        input_output_aliases={5 + j: j for j in range(n_prev)},
        compiler_params=_params(1))(w, g, m, v, after, *(prev or []))


def _pack_small(parts):
    out = []
    for a in parts:
        flat = a.reshape(-1)
        n = -(-flat.shape[0] // (8 * LANES)) * 8 * LANES
        out.append(jnp.pad(flat, (0, n - flat.shape[0])).reshape(-1, LANES))
    return jnp.concatenate(out, axis=0)


def _unpack_small(packed, like):
    out, r = [], 0
    for a in like:
        size = int(np.prod(a.shape))
        rows = -(-size // (8 * LANES)) * 8
        out.append(packed[r:r + rows].reshape(-1)[:size].reshape(a.shape))
        r += rows
    return out


def _ffn_forward(x, g, win, wout):
    h, silu, dgate, act = _ffn_up(x, g, win)
    if callable(wout):
        wout = wout(act)
    return _mm_res("ffn_down", x, [act], wout, 0.5), (x, h, silu, dgate, act)


def _ffn_backward(dx, saved, g, win, wout, after):
    x, h, silu, dgate, act = saved
    D = x.shape[1]
    wc = win.shape[2]
    dgu, dx_in, dg = _ffn_bwd(dx, wout, silu, dgate, win, x, g, after)
    dwout = _mm_tn("dw_ffn_out", act, wc, dx, D, 2, True, False, 0.5)
    dwin = _mm_tn_pairs("dw_ffn_in", h, dgu, wc, 4)
    return dx_in, dg, dwin, dwout.reshape(N_CHIPS, -1, D)


GROUP_FFN1 = ("ffn1_w_in", "ffn1_w_out")
GROUP_REST = ("w_qkv", "w_o", "ffn2_w_in", "ffn2_w_out", "w_ple_gate", "w_ple_proj")
GATHER_L0 = (("a", ("ffn1_w_in",)), ("b", ("ffn1_w_out",)), ("c", ("w_qkv", "w_o")),
             ("d", ("ffn2_w_in", "ffn2_w_out", "w_ple_gate", "w_ple_proj")))


def _gather_start(tag, slotted, after):
    return _split_start("ag_start_" + tag, slotted, _gather_copies([a.shape for a in slotted]), 3 * len(slotted), after)


def _gather_finish(tag, started, after):
    send_s, recv_s, arrays, _ = started
    return _pair_forward(_split_wait("ag_wait_" + tag, send_s, recv_s, arrays,
                                     _gather_copies([a.shape for a in arrays]), after))


def _scatter_exchange(tag, grads):
    n = len(grads)
    land = [lax.empty((g_.shape[0], g_.shape[1] // 2, g_.shape[2]), g_.dtype) for g_ in grads]
    return _split_start("rs_px_start_" + tag, list(grads) + land, _pair_exchange_copies([g_.shape for g_ in grads]),
                        n, None)


def _scatter_chips(tag, started, after):
    send_s, recv_s, arrays, _ = started
    n = len(arrays) // 2
    arrays = _split_wait("rs_px_wait_" + tag, send_s, recv_s, arrays,
                         _pair_exchange_copies([a.shape for a in arrays[:n]]), after)
    part = [_rs_add_pair(g_, r_) for g_, r_ in zip(arrays[:n], arrays[n:])]
    land = [lax.empty((3,) + p_.shape[1:], p_.dtype) for p_ in part]
    return _split_start("rs_start_" + tag, part + land, _scatter_copies(n), 3 * n, None)


def _scatter_share(tag, started, after):
    send_s, recv_s, arrays, _ = started
    n = len(arrays) // 2
    arrays = _split_wait("rs_wait_" + tag, send_s, recv_s, arrays, _scatter_copies(n), after)
    halves = [_rs_add_chips(p_, r_) for p_, r_ in zip(arrays[:n], arrays[n:])]
    return _split_start("rs_ps_start_" + tag, halves, _pair_share_copies([h_.shape for h_ in halves]), n, None)


def _scatter_done(tag, started, after):
    send_s, recv_s, arrays, _ = started
    return _split_wait("rs_ps_wait_" + tag, send_s, recv_s, arrays, _pair_share_copies([a.shape for a in arrays]), after)


def kernel(x, p, rel_bias, norm_ffn1, ffn1_w_in, ffn1_w_out, norm_mix, w_qkv, q_norm_a, k_norm_a, q_norm_b, k_norm_b, sink_b, w_o, norm_ffn2, ffn2_w_in, ffn2_w_out, norm_ple, w_ple_gate, w_ple_proj, loss_target, m_rel_bias, m_norm_ffn1, m_ffn1_w_in, m_ffn1_w_out, m_norm_mix, m_w_qkv, m_q_norm_a, m_k_norm_a, m_q_norm_b, m_k_norm_b, m_sink_b, m_w_o, m_norm_ffn2, m_ffn2_w_in, m_ffn2_w_out, m_norm_ple, m_w_ple_gate, m_w_ple_proj, v_rel_bias, v_norm_ffn1, v_ffn1_w_in, v_ffn1_w_out, v_norm_mix, v_w_qkv, v_q_norm_a, v_k_norm_a, v_q_norm_b, v_k_norm_b, v_sink_b, v_w_o, v_norm_ffn2, v_ffn2_w_in, v_ffn2_w_out, v_norm_ple, v_w_ple_gate, v_w_ple_proj):
    given = dict(locals())
    T, D = x.shape[1], x.shape[2]
    NL = norm_ffn1.shape[0]
    x0 = x.reshape(T, D)
    tgt = loss_target.reshape(T, D)
    n_a, n_b, n_kv = _qkv_layout(D)

    assert NL == 2
    slot = [{name: _cast_into_slot(given[name], l) for name in BIG} for l in range(NL)]
    ag, token = {}, None
    for tag, names in GATHER_L0:
        ag[tag] = _gather_start(tag, [slot[0][n] for n in names], token)
        token = ag[tag][3]
    ag_1 = _gather_start("1", [slot[1][n] for n in BIG], token)

    def arrived(tag, after):
        return dict(zip(dict(GATHER_L0)[tag], _gather_finish(tag, ag[tag], after)))

    def by_rows(a):
        return a.reshape(-1, a.shape[-1])

    def by_cols(a):
        return a.transpose(1, 0, 2).reshape(a.shape[1], -1)

    QW = N_CHIPS * w_qkv.shape[2]

    dils = tuple(d for _, d in DILATED_CONFIGS)
    cfg_a = [(w // (2 * d), d) for w, d in DILATED_CONFIGS]
    pad_a, pad_b = _window_pad(cfg_a[0][0]), _window_pad(SWA_RADIUS)
    bmaps_a, bmaps_b = jnp.asarray(_bucket_maps(cfg_a)), jnp.asarray(_bucket_maps([(SWA_RADIUS, 1)]))
    n_heads = rel_bias.shape[1] // 2
    bias_a = _edge_variants(_bias_build(rel_bias, bmaps_a, 0), pad_a)
    bias_b = _edge_variants(_bias_build(rel_bias, bmaps_b, n_heads), pad_b)
    no_sink = jnp.full((n_heads,), NEG, F32)

    def gains_row(l):
        ones = jnp.ones((n_a * LANES,), F32)
        return jnp.concatenate([
            jnp.tile(q_norm_a[l], 2 * n_a), jnp.tile(k_norm_a[l], 2 * n_a), ones,
            jnp.tile(q_norm_b[l], 2 * n_b), jnp.tile(k_norm_b[l], 2 * n_kv), jnp.ones((n_kv * LANES,), F32)]).reshape(1, QW)

    saved, weights = [], []
    xc = x0
    pf_1 = None
    for l in range(NL):
        s, w = {}, {}
        if l == 0:
            w.update(arrived("a", ag_1[3]))

            def ffn1_w_out(act, w=w):
                w.update(arrived("b", act))
                w["ffn1_w_out"] = by_rows(w["ffn1_w_out"])
                return w["ffn1_w_out"]
        else:
            shapes = [a.shape for a in pf_1[2]]
            w.update(zip(BIG, _split_wait("ag_pf_wait_1", pf_1[0], pf_1[1], pf_1[2], _forward_copies(shapes), xc)))
            ffn1_w_out = w["ffn1_w_out"] = by_rows(w["ffn1_w_out"])
        xc, s["ffn1"] = _ffn_forward(xc, norm_ffn1[l:l + 1], w["ffn1_w_in"], ffn1_w_out)
        s["x1"] = xc
        if l == 0:
            w.update(arrived("c", xc))
        w["w_qkv"] = by_cols(w["w_qkv"])
        w["w_o"] = by_rows(w["w_o"])
        s["h2"], s["raw"], s["qkv_a"], s["qkv_b"] = _qkv_proj(xc, norm_mix[l:l + 1], w["w_qkv"], gains_row(l))
        s["o_a"], s["lse_a"] = _attn_fwd(s["qkv_a"], bias_a, no_sink, dils, pad_a)
        s["o_b"], s["lse_b"] = _attn_fwd(s["qkv_b"], bias_b, sink_b[l], (1,), pad_b)
        xc = _mm_res("attn_out", xc, [s["o_a"], s["o_b"]], w["w_o"], 1.0)
        if l == 0:
            w.update(arrived("d", xc))
        w["w_ple_proj"] = by_cols(w["w_ple_proj"])
        for name in ("ffn2_w_out", "w_ple_gate"):
            w[name] = by_rows(w[name])
        xc, s["ffn2"] = _ffn_forward(xc, norm_ffn2[l:l + 1], w["ffn2_w_in"], w["ffn2_w_out"])
        s["x3"] = xc
        if l == 0:
            landed = _split_wait("ag_wait_1", ag_1[0], ag_1[1], ag_1[2], _gather_copies([a.shape for a in ag_1[2]]), xc)
            pf_1 = _split_start("ag_pf_start_1", landed, _forward_copies([a.shape for a in landed]), 3 * len(landed),
                                None)
        s["p"] = p[l].reshape(T, -1)
        s["hn"], xc, s["gate"], s["pp"] = _ple_fwd(xc, norm_ple[l:l + 1], s["p"], w["w_ple_gate"], w["w_ple_proj"],
                                                   pf_1[3] if l == 0 else xc)
        saved.append(s)
        weights.append(w)

    dx, loss_blk = _loss_fwd_bwd(xc, tgt)
    loss = lax.psum(loss_blk[0, 0], ("x", "y", "c"))

    gs = {name: [None] * NL for name in SMALL if name != "rel_bias"}
    dt_a, dt_b = [], []

    def layer_backward(l, dx, hooks):
        def at(point, ready, *more):
            return hooks[point](ready, *more) if point in hooks else ready

        s, w, gw = saved[l], weights[l], {}
        dx, gs["norm_ple"][l], dwg, dwp = _ple_bwd(dx, s["gate"], s["pp"], s["hn"], s["p"], s["x3"],
                                                   norm_ple[l:l + 1], w["w_ple_gate"], at("start", dx))
        gw["w_ple_gate"] = dwg.reshape(N_CHIPS, -1, D)
        gw["w_ple_proj"] = dwp.reshape(dwp.shape[0], N_CHIPS, -1).transpose(1, 0, 2)
        dx, gs["norm_ffn2"][l], gw["ffn2_w_in"], gw["ffn2_w_out"] = _ffn_backward(
            dx, s["ffn2"], norm_ffn2[l:l + 1], w["ffn2_w_in"], w["ffn2_w_out"], at("after_ple", dx))
        do, dd = _mix_bwd_in(dx, w["w_o"], [s["o_a"], s["o_b"]])
        hwa = s["o_a"].shape[1]
        gw["w_o"] = jnp.concatenate([
            _mm_tn("dw_o", o_, o_.shape[1], dx, D, 1, False, False, 1.0).reshape(-1, D // N_CHIPS, D)
            for o_ in (s["o_a"], s["o_b"])], axis=0)
        dqa, dka, dva, dt, _ = _attn_bwd(s["qkv_a"], bias_a, no_sink, dils, pad_a, do, s["lse_a"], dd, 0, do)
        dt_a.append(dt)
        dqb, dkb, dvb, dt, dsink = _attn_bwd(s["qkv_b"], bias_b, sink_b[l], (1,), pad_b, do, s["lse_b"], dd,
                                             hwa // LANES, at("after_attn_a", dqa))
        dt_b.append(dt)
        gs["sink_b"][l] = dsink[:, 0, ::HEAD_DIM].reshape(-1)
        draw, dgains = _qkv_post_bwd(s["raw"], gains_row(l), (dqa, dka, dva), (dqb, dkb, dvb))
        dgv = dgains.reshape(-1, HEAD_DIM)
        gs["q_norm_a"][l] = dgv[:2 * n_a].sum(0)
        gs["k_norm_a"][l] = dgv[2 * n_a:4 * n_a].sum(0)
        gs["q_norm_b"][l] = dgv[6 * n_a:6 * n_a + 2 * n_b].sum(0)
        gs["k_norm_b"][l] = dgv[6 * n_a + 2 * n_b:6 * n_a + 2 * n_b + 2 * n_kv].sum(0)
        dwqkv = _mm_tn("dw_qkv", s["h2"], D, draw, QW, 1, False, False, 1.0)
        gw["w_qkv"] = dwqkv.reshape(D, N_CHIPS, -1).transpose(1, 0, 2)
        dx, gs["norm_mix"][l] = _bwd_into_norm("qkv_bwd_b", draw, QW, w["w_qkv"], 1, 1, s["x1"], norm_mix[l:l + 1], dx,
                                               at("before_qkv_bwd", draw, [gw[n] for n in GROUP_REST]))
        dx, gs["norm_ffn1"][l], gw["ffn1_w_in"], gw["ffn1_w_out"] = _ffn_backward(
            dx, s["ffn1"], norm_ffn1[l:l + 1], w["ffn1_w_in"], w["ffn1_w_out"], at("before_ffn1", dx))
        return dx, gw

    out = {}

    def adamw_group(names, l, grads, after):
        for name, g_ in zip(names, grads):
            out[name] = _adamw_layer(given[name], g_, given["m_" + name], given["v_" + name], l, out.get(name), after)
            after = out[name][0]
        return after

    dx, gw1 = layer_backward(NL - 1, dx, {})
    px_1 = _scatter_exchange("1", [gw1[n] for n in BIG])
    rs = {}

    def chips_1(ready):
        rs["chips_1"] = _scatter_chips("1", px_1, ready)
        return rs["chips_1"][3]

    def share_1(ready):
        rs["share_1"] = _scatter_share("1", rs["chips_1"], ready)
        return rs["share_1"][3]

    def exchange_0a(ready, grads):
        rs["px_0a"] = _scatter_exchange("0a", grads)
        return rs["px_0a"][3]

    def chips_0a(ready):
        rs["chips_0a"] = _scatter_chips("0a", rs["px_0a"], ready)
        return rs["chips_0a"][3]

    dx, gw0 = layer_backward(0, dx, {"start": lambda ready: px_1[3], "after_ple": chips_1, "after_attn_a": share_1,
                                     "before_qkv_bwd": exchange_0a, "before_ffn1": chips_0a})
    grad_x = dx.reshape(x.shape)

    px_0b = _scatter_exchange("0b", [gw0[n] for n in GROUP_FFN1])
    d_rel_bias = (_bias_grad(dt_a, bmaps_a, 0) + _bias_grad(dt_b, bmaps_b, n_heads))[:, :rel_bias.shape[1]]
    small_g = [d_rel_bias] + [jnp.stack([t.reshape(-1) for t in gs[name]]) for name in SMALL[1:]]
    g_sum = _allreduce_small(_pack_small(small_g), px_0b[3])
    res = _ew("adamw_small", _adamw_fn,
              [_pack_small([given[n] for n in SMALL]), g_sum, _pack_small([given["m_" + n] for n in SMALL]),
               _pack_small([given["v_" + n] for n in SMALL])], [(LANES, F32)] * 4)
    like = [given[n] for n in SMALL]
    unpacked = [_unpack_small(r, like) for r in res]
    for i, name in enumerate(SMALL):
        out[name] = [u[i] for u in unpacked]

    share_0a = _scatter_share("0a", rs["chips_0a"], res[0])
    g_1 = _scatter_done("1", rs["share_1"], share_0a[3])
    chips_0b = _scatter_chips("0b", px_0b, g_1[0])
    ready = adamw_group(BIG, 1, g_1, chips_0b[3])
    ready = adamw_group(GROUP_REST, 0, _scatter_done("0a", share_0a, ready), ready)
    share_0b = _scatter_share("0b", chips_0b, ready)
    adamw_group(GROUP_FFN1, 0, _scatter_done("0b", share_0b, share_0b[3]), share_0b[3])

    return (loss, grad_x, *[out[n][0] for n in WEIGHTS], *[out[n][1] for n in WEIGHTS],
            *[out[n][2] for n in WEIGHTS], *[out[n][3] for n in WEIGHTS])
```

```python
import functools
import math

import numpy as np
import jax
import jax.numpy as jnp
from jax import lax
from jax.experimental import pallas as pl
from jax.experimental.pallas import tpu as pltpu

F32 = jnp.float32
BF16 = jnp.bfloat16
MESH = pl.DeviceIdType.MESH

HEAD_DIM = 64
LANES = 128
QBLOCK = 128
FWD_UNROLL, BWD_UNROLL = 8, 8
N_BUCKETS = 32
MAX_DISTANCE = 1024
DILATED_CONFIGS = ((128, 1), (512, 4), (2048, 16))
SWA_RADIUS = 128
GROUP_B = 4
EPS = 1e-6
NEG = -1e30
Q_SCALE = HEAD_DIM ** -0.5
ADAM_LR, ADAM_B1, ADAM_B2, ADAM_EPS, ADAM_WD, ADAM_STEP = 0.001, 0.9, 0.999, 1e-08, 0.01, 10
VMEM_LIMIT = 56 * 2 ** 20
N_CHIPS = 4
N_DEV = 8

BIG = ("ffn1_w_in", "ffn1_w_out", "w_qkv", "w_o", "ffn2_w_in", "ffn2_w_out", "w_ple_gate", "w_ple_proj")
SMALL = ("rel_bias", "norm_ffn1", "norm_mix", "q_norm_a", "k_norm_a", "q_norm_b", "k_norm_b", "sink_b",
         "norm_ffn2", "norm_ple")
WEIGHTS = ("rel_bias", "norm_ffn1", "ffn1_w_in", "ffn1_w_out", "norm_mix", "w_qkv", "q_norm_a", "k_norm_a",
           "q_norm_b", "k_norm_b", "sink_b", "w_o", "norm_ffn2", "ffn2_w_in", "ffn2_w_out", "norm_ple",
           "w_ple_gate", "w_ple_proj")


HBM_SPEC = pl.BlockSpec(memory_space=pltpu.HBM)
ANY_SPEC = pl.BlockSpec(memory_space=pl.ANY)
SEM_SPEC = pl.BlockSpec(memory_space=pltpu.SEMAPHORE)


def _params(n_grid):
    return pltpu.CompilerParams(dimension_semantics=("arbitrary",) * n_grid, vmem_limit_bytes=VMEM_LIMIT)


def _pick_tm(rows, cap):
    t = (min(cap, rows) // 16) * 16
    while t >= 16:
        if rows % t == 0:
            return t
        t -= 16
    return rows


def _dot(a, b):
    return jnp.dot(a, b, preferred_element_type=F32)


def _dot_nt(a, b):
    return lax.dot_general(a, b, (((1,), (1,)), ((), ())), preferred_element_type=F32)


def _dot_tn(a, b):
    return lax.dot_general(a, b, (((0,), (0,)), ((), ())), preferred_element_type=F32)


def _sigmoid(z):
    return 1.0 / (1.0 + jnp.exp(-z))


def _lo_lanes(shape):
    return lax.broadcasted_iota(jnp.int32, shape, len(shape) - 1) % LANES < HEAD_DIM


def _seg_sum(blk):
    lo = _lo_lanes(blk.shape)
    s_lo = jnp.sum(jnp.where(lo, blk, 0.0), axis=1, keepdims=True)
    s_hi = jnp.sum(jnp.where(lo, 0.0, blk), axis=1, keepdims=True)
    return jnp.where(lo, s_lo, s_hi)


def _rms_bwd_tile(x, g, dh):
    r = lax.rsqrt(jnp.mean(x * x, axis=-1, keepdims=True) + EPS)
    xh = x * r
    dyg = dh * g
    dx = r * (dyg - xh * jnp.mean(dyg * xh, axis=-1, keepdims=True))
    return dx, jnp.sum(dh * xh, axis=0, keepdims=True)


def _ew(name, fn, ins, out_defs, cap=512):
    rows = ins[0].shape[0]
    tm = _pick_tm(rows, cap)
    n_in = len(ins)

    def body(*refs):
        vals = fn(*[r[...] for r in refs[:n_in]])
        if not isinstance(vals, tuple):
            vals = (vals,)
        for r, v in zip(refs[n_in:], vals):
            r[...] = v.astype(r.dtype)

    return pl.pallas_call(
        body, name=name, grid=(rows // tm,),
        in_specs=[pl.BlockSpec((tm, a.shape[1]), lambda i: (i, 0)) for a in ins],
        out_specs=[pl.BlockSpec((tm, c), lambda i: (i, 0)) for c, _ in out_defs],
        out_shape=[jax.ShapeDtypeStruct((rows, c), dt) for c, dt in out_defs],
        compiler_params=_params(1))(*ins)


def _rms_tile(xv, gv):
    r = lax.rsqrt(jnp.mean(xv * xv, axis=-1, keepdims=True) + EPS)
    return (xv * r * gv).astype(BF16)


def _ffn_up(x, g, win):
    T, D = x.shape
    wc = win.shape[2]
    tm = _pick_tm(T, 512)

    def body(x_ref, g_ref, wg_ref, wu_ref, h_ref, silu_ref, dgate_ref, act_ref, wcat):
        @pl.when(pl.program_id(1) == 0)
        def _():
            wcat[:, :wc] = wg_ref[...]
            wcat[:, wc:] = wu_ref[...]

        hv = _rms_tile(x_ref[...], g_ref[...])
        h_ref[...] = hv
        gu = _dot(hv, wcat[...])
        gte, u = gu[:, :wc], gu[:, wc:]
        sg = _sigmoid(gte)
        silu = gte * sg
        silu_ref[...] = silu.astype(BF16)
        dgate_ref[...] = ((sg + silu * (1.0 - sg)) * u).astype(BF16)
        act_ref[...] = (silu * u).astype(BF16)

    out = jax.ShapeDtypeStruct((T, 2 * wc), BF16)
    ospec = pl.BlockSpec((tm, wc), lambda j, i: (i, j))
    nt = T // tm
    h_spec = pl.BlockSpec((tm, D), lambda j, i: (jnp.where(j == 0, i, nt), 0))
    return pl.pallas_call(
        body, name="ffn_up", grid=(2, nt),
        in_specs=[pl.BlockSpec((tm, D), lambda j, i: (i, 0)), pl.BlockSpec((1, D), lambda j, i: (0, 0)),
                  pl.BlockSpec((None, D, wc), lambda j, i: (j, 0, 0)),
                  pl.BlockSpec((None, D, wc), lambda j, i: (j + 2, 0, 0))],
        out_specs=[h_spec] + [ospec] * 3, out_shape=[jax.ShapeDtypeStruct((T + tm, D), BF16)] + [out] * 3,
        scratch_shapes=[pltpu.VMEM((D, 2 * wc), BF16)], compiler_params=_params(2))(x, g, win, win)


def _mm_res(name, res, a_list, w, scale):
    T, N = res.shape
    n = len(a_list)
    widths = [a.shape[1] for a in a_list]
    tm = _pick_tm(T, 512)

    def body(*refs):
        r_ref, a_refs, w_refs, o_ref = refs[0], refs[1:1 + n], refs[1 + n:1 + 2 * n], refs[1 + 2 * n]
        acc = _dot(a_refs[0][...].astype(BF16), w_refs[0][...])
        for a_ref, w_ref in zip(a_refs[1:], w_refs[1:]):
            acc = acc + _dot(a_ref[...].astype(BF16), w_ref[...])
        o_ref[...] = r_ref[...] + scale * acc

    w_specs, off = [], 0
    for k in widths:
        w_specs.append(pl.BlockSpec((k, N), lambda i, blk=off // k: (blk, 0)))
        off += k
    return pl.pallas_call(
        body, name=name, grid=(T // tm,),
        in_specs=[pl.BlockSpec((tm, N), lambda i: (i, 0))]
        + [pl.BlockSpec((tm, k), lambda i: (i, 0)) for k in widths] + w_specs,
        out_specs=pl.BlockSpec((tm, N), lambda i: (i, 0)),
        out_shape=jax.ShapeDtypeStruct((T, N), F32), compiler_params=_params(1))(res, *a_list, *([w] * n))


def _mm_tn(name, a, a_w, b, b_w, n_slots, a_by_slot, b_by_slot, scale, tm_cap=512):
    T = b.shape[0]
    tm = _pick_tm(T, tm_cap)
    nt = T // tm

    def body(a_ref, b_ref, o_ref, acc):
        i = pl.program_id(1)

        @pl.when(i == 0)
        def _():
            acc[...] = jnp.zeros_like(acc)

        acc[...] += _dot_tn(a_ref[...].astype(BF16), b_ref[...].astype(BF16))

        @pl.when(i == nt - 1)
        def _():
            o_ref[...] = (acc[...] * scale).astype(BF16)

    return pl.pallas_call(
        body, name=name, grid=(n_slots, nt),
        in_specs=[pl.BlockSpec((tm, a_w), (lambda s, i: (i, s)) if a_by_slot else (lambda s, i: (i, 0))),
                  pl.BlockSpec((tm, b_w), (lambda s, i: (i, s)) if b_by_slot else (lambda s, i: (i, 0)))],
        out_specs=pl.BlockSpec((None, a_w, b_w), lambda s, i: (s, 0, 0)),
        out_shape=jax.ShapeDtypeStruct((n_slots, a_w, b_w), BF16),
        scratch_shapes=[pltpu.VMEM((a_w, b_w), F32)], compiler_params=_params(2))(a, b)


def _mm_tn_pairs(name, a, b, b_w, n_slots):
    T = b.shape[0]
    a_w = a.shape[1]
    tm = _pick_tm(T, 512)
    nt = T // tm

    def body(a_ref, b_ref, o_ref, acc):
        i = pl.program_id(1)

        @pl.when(i == 0)
        def _():
            acc[...] = jnp.zeros_like(acc)

        acc[...] += _dot_tn(a_ref[...], b_ref[...])

        @pl.when(i == nt - 1)
        def _():
            o_ref[0] = acc[:, :b_w].astype(BF16)
            o_ref[1] = acc[:, b_w:].astype(BF16)

    return pl.pallas_call(
        body, name=name, grid=(n_slots // 2, nt),
        in_specs=[pl.BlockSpec((tm, a_w), lambda s, i: (i, 0)), pl.BlockSpec((tm, 2 * b_w), lambda s, i: (i, s))],
        out_specs=pl.BlockSpec((2, a_w, b_w), lambda s, i: (s, 0, 0)),
        out_shape=jax.ShapeDtypeStruct((n_slots, a_w, b_w), BF16),
        scratch_shapes=[pltpu.VMEM((a_w, 2 * b_w), F32)], compiler_params=_params(2))(a, b)


def _ffn_bwd(dx, wout, silu, dgate, win, x, g, after):
    T, D = dx.shape
    F = silu.shape[1]
    n_slots, _, wc = win.shape
    tm = _pick_tm(T, 256)

    def body(dx_ref, wout_hbm, s_ref, dgt_ref, win_hbm, x_ref, g_ref, after_ref,
             dgu_ref, dxo_ref, dg_ref, wout_v, wcat, sem):
        @pl.when(pl.program_id(0) == 0)
        def _():
            copies = [pltpu.make_async_copy(wout_hbm, wout_v, sem.at[n_slots])]
            copies += [pltpu.make_async_copy(win_hbm.at[s], wcat.at[s // 2, :, pl.ds((s % 2) * wc, wc)], sem.at[s])
                       for s in range(n_slots)]
            for cp in copies:
                cp.start()
            for cp in copies:
                cp.wait()
            dg_ref[...] = jnp.zeros_like(dg_ref)

        dxv = dx_ref[...]
        dact = 0.5 * _dot_nt(dxv.astype(BF16), wout_v[...])
        d_gate = (dact * dgt_ref[...].astype(F32)).astype(BF16)
        d_up = (dact * s_ref[...].astype(F32)).astype(BF16)
        dgu_ref[:, :F] = d_gate
        dgu_ref[:, F:] = d_up
        dh = _dot_nt(d_gate, wcat[0]) + _dot_nt(d_up, wcat[1])
        dxn, dg = _rms_bwd_tile(x_ref[...], g_ref[...], dh)
        dxo_ref[...] = dxv + dxn
        dg_ref[...] += dg

    row = pl.BlockSpec((tm, D), lambda i: (i, 0))
    vec = pl.BlockSpec((1, D), lambda i: (0, 0))
    act_spec = pl.BlockSpec((tm, F), lambda i: (i, 0))
    return pl.pallas_call(
        body, name="ffn_bwd", grid=(T // tm,),
        in_specs=[row, ANY_SPEC, act_spec, act_spec, ANY_SPEC, row, vec, ANY_SPEC],
        out_specs=[pl.BlockSpec((tm, 2 * F), lambda i: (i, 0)), row, vec],
        out_shape=[jax.ShapeDtypeStruct((T, 2 * F), BF16), jax.ShapeDtypeStruct((T, D), F32),
                   jax.ShapeDtypeStruct((1, D), F32)],
        scratch_shapes=[pltpu.VMEM((F, D), BF16), pltpu.VMEM((n_slots // 2, D, 2 * wc), BF16),
                        pltpu.SemaphoreType.DMA((n_slots + 1,))],
        compiler_params=_params(1))(dx, wout, silu, dgate, win, x, g, after)


def _proj_bwd(name, d, w, h, x, g, dx_in, after):
    T, D = x.shape
    N = d.shape[1]
    tm = _pick_tm(T, 256)
    nt = T // tm

    def body(d_ref, w_ref, h_ref, x_ref, g_ref, dxi_ref, after_ref, dx_ref, dg_ref, dw_ref, acc):
        i = pl.program_id(0)

        @pl.when(i == 0)
        def _():
            acc[...] = jnp.zeros_like(acc)
            dg_ref[...] = jnp.zeros_like(dg_ref)

        dv = d_ref[...]
        dxn, dg = _rms_bwd_tile(x_ref[...], g_ref[...], _dot_nt(dv, w_ref[...]))
        dx_ref[...] = dxi_ref[...] + dxn
        dg_ref[...] += dg
        acc[...] += _dot_tn(h_ref[...], dv)

        @pl.when(i == nt - 1)
        def _():
            dw_ref[...] = acc[...].astype(BF16)

    row = pl.BlockSpec((tm, D), lambda i: (i, 0))
    vec = pl.BlockSpec((1, D), lambda i: (0, 0))
    whole = pl.BlockSpec((D, N), lambda i: (0, 0))
    return pl.pallas_call(
        body, name=name, grid=(nt,),
        in_specs=[pl.BlockSpec((tm, N), lambda i: (i, 0)), whole, row, row, vec, row, ANY_SPEC],
        out_specs=[row, vec, whole],
        out_shape=[jax.ShapeDtypeStruct((T, D), F32), jax.ShapeDtypeStruct((1, D), F32),
                   jax.ShapeDtypeStruct((D, N), BF16)],
        scratch_shapes=[pltpu.VMEM((D, N), F32)], compiler_params=_params(1))(d, w, h, x, g, dx_in, after)


def _ple_fwd(x, g, p, wg, wp, after):
    T, D = x.shape
    P = p.shape[1]
    tm = _pick_tm(T, 256)

    def body(x_ref, g_ref, p_ref, wg_ref, wp_ref, after_ref, hn_ref, xo_ref, gate_ref, pp_ref):
        xv = x_ref[...]
        hn = _rms_tile(xv, g_ref[...])
        hn_ref[...] = hn
        gate = _sigmoid(_dot(hn, wg_ref[...]))
        pp = _dot(p_ref[...].astype(BF16), wp_ref[...])
        gate_ref[...] = gate.astype(BF16)
        pp_ref[...] = pp.astype(BF16)
        xo_ref[...] = xv + gate * pp

    row = pl.BlockSpec((tm, D), lambda i: (i, 0))
    out = jax.ShapeDtypeStruct((T, D), F32)
    half = jax.ShapeDtypeStruct((T, D), BF16)
    return pl.pallas_call(
        body, name="ple_fwd", grid=(T // tm,),
        in_specs=[row, pl.BlockSpec((1, D), lambda i: (0, 0)), pl.BlockSpec((tm, P), lambda i: (i, 0)),
                  pl.BlockSpec((D, D), lambda i: (0, 0)), pl.BlockSpec((P, D), lambda i: (0, 0)), ANY_SPEC],
        out_specs=[row, row, row, row], out_shape=[half, out, half, half],
        compiler_params=_params(1))(x, g, p, wg, wp, after)


def _ple_bwd(dx, gate, pp, hn, p, x, g, wg, after):
    T, D = x.shape
    P = p.shape[1]
    tm = _pick_tm(T, 256)
    nt = T // tm

    def body(dx_ref, gate_ref, pp_ref, hn_ref, p_ref, x_ref, g_ref, wg_ref, after_ref,
             dxo_ref, dg_ref, dwg_ref, dwp_ref, acc_g, acc_p):
        i = pl.program_id(0)

        @pl.when(i == 0)
        def _():
            acc_g[...] = jnp.zeros_like(acc_g)
            acc_p[...] = jnp.zeros_like(acc_p)
            dg_ref[...] = jnp.zeros_like(dg_ref)

        dxv = dx_ref[...]
        gate = gate_ref[...].astype(F32)
        dz = (dxv * pp_ref[...].astype(F32) * gate * (1.0 - gate)).astype(BF16)
        dpp = (dxv * gate).astype(BF16)
        acc_g[...] += _dot_tn(hn_ref[...], dz)
        acc_p[...] += _dot_tn(p_ref[...].astype(BF16), dpp)
        dxn, dg = _rms_bwd_tile(x_ref[...], g_ref[...], _dot_nt(dz, wg_ref[...]))
        dxo_ref[...] = dxv + dxn
        dg_ref[...] += dg

        @pl.when(i == nt - 1)
        def _():
            dwg_ref[...] = acc_g[...].astype(BF16)
            dwp_ref[...] = acc_p[...].astype(BF16)

    row = pl.BlockSpec((tm, D), lambda i: (i, 0))
    vec = pl.BlockSpec((1, D), lambda i: (0, 0))
    return pl.pallas_call(
        body, name="ple_bwd", grid=(nt,),
        in_specs=[row, row, row, row, pl.BlockSpec((tm, P), lambda i: (i, 0)), row, vec,
                  pl.BlockSpec((D, D), lambda i: (0, 0)), ANY_SPEC],
        out_specs=[row, vec, pl.BlockSpec((D, D), lambda i: (0, 0)), pl.BlockSpec((P, D), lambda i: (0, 0))],
        out_shape=[jax.ShapeDtypeStruct((T, D), F32), jax.ShapeDtypeStruct((1, D), F32),
                   jax.ShapeDtypeStruct((D, D), BF16), jax.ShapeDtypeStruct((P, D), BF16)],
        scratch_shapes=[pltpu.VMEM((D, D), F32), pltpu.VMEM((P, D), F32)],
        compiler_params=_params(1))(dx, gate, pp, hn, p, x, g, wg, after)


def _loss_fwd_bwd(y, tgt):
    T, D = y.shape
    tm = _pick_tm(T, 512)

    def body(y_ref, t_ref, dy_ref, loss_ref):
        e = y_ref[...] - t_ref[...]
        dy_ref[...] = e / D

        @pl.when(pl.program_id(0) == 0)
        def _():
            loss_ref[...] = jnp.zeros_like(loss_ref)

        loss_ref[...] += 0.5 * jnp.sum(jnp.mean(e * e, axis=-1, keepdims=True), axis=0, keepdims=True)

    row = pl.BlockSpec((tm, D), lambda i: (i, 0))
    return pl.pallas_call(
        body, name="loss", grid=(T // tm,), in_specs=[row, row],
        out_specs=[row, pl.BlockSpec((8, LANES), lambda i: (0, 0))],
        out_shape=[jax.ShapeDtypeStruct((T, D), F32), jax.ShapeDtypeStruct((8, LANES), F32)],
        compiler_params=_params(1))(y, tgt)


def _qkv_layout(D):
    n_a = D // (2 * LANES)
    n_b = D // (2 * LANES)
    n_kv = max(1, (2 * n_b) // GROUP_B) * HEAD_DIM // LANES
    return n_a, n_b, n_kv


def _dup_half(xv, half):
    rolled = pltpu.roll(xv, HEAD_DIM, 1)
    lo = _lo_lanes(xv.shape)
    return jnp.where(lo, xv, rolled) if half == 0 else jnp.where(lo, rolled, xv)


def _qkv_proj(x, g_mix, w, gains):
    T, D = x.shape
    W = w.shape[1]
    n_a, n_b, n_kv = _qkv_layout(D)
    tm = _pick_tm(T, 256)
    o_qb = 3 * n_a

    def norm(xv, gv, scale):
        ms = _seg_sum(xv * xv) * (1.0 / HEAD_DIM)
        return xv * lax.rsqrt(ms + EPS) * gv * scale

    def body(x_ref, gm_ref, w_ref, g_ref, h_ref, raw_ref, a_ref, b_ref):
        hv = _rms_tile(x_ref[...], gm_ref[...])
        h_ref[...] = hv
        raw = _dot(hv, w_ref[...])
        raw_ref[...] = raw

        def blk(cb):
            return raw[:, cb * LANES:(cb + 1) * LANES]

        def gn(cb):
            return g_ref[:, cb * LANES:(cb + 1) * LANES]

        for cb in range(n_a):
            a_ref[:, cb * LANES:(cb + 1) * LANES] = norm(blk(cb), gn(cb), Q_SCALE)
            cbk = n_a + cb
            a_ref[:, cbk * LANES:(cbk + 1) * LANES] = norm(blk(cbk), gn(cbk), 1.0)
            cbv = 2 * n_a + cb
            a_ref[:, cbv * LANES:(cbv + 1) * LANES] = blk(cbv)
        for cb in range(n_b):
            src = o_qb + cb
            b_ref[:, cb * LANES:(cb + 1) * LANES] = norm(blk(src), gn(src), Q_SCALE)
        for e in range(n_b):
            kvh = (2 * e) // GROUP_B
            ck = o_qb + n_b + kvh // 2
            cv = ck + n_kv
            kn = norm(blk(ck), gn(ck), 1.0)
            b_ref[:, (n_b + e) * LANES:(n_b + e + 1) * LANES] = _dup_half(kn, kvh % 2)
            b_ref[:, (2 * n_b + e) * LANES:(2 * n_b + e + 1) * LANES] = _dup_half(blk(cv), kvh % 2)

    wa, wb = 3 * n_a * LANES, 3 * n_b * LANES

    def rows(width):
        return pl.BlockSpec((tm, width), lambda i: (i, 0))

    return pl.pallas_call(
        body, name="qkv_proj", grid=(T // tm,),
        in_specs=[rows(D), pl.BlockSpec((1, D), lambda i: (0, 0)), pl.BlockSpec((D, W), lambda i: (0, 0)),
                  pl.BlockSpec((1, W), lambda i: (0, 0))],
        out_specs=[rows(D), rows(W), rows(wa), rows(wb)],
        out_shape=[jax.ShapeDtypeStruct((T, D), BF16), jax.ShapeDtypeStruct((T, W), F32),
                   jax.ShapeDtypeStruct((T, wa), F32), jax.ShapeDtypeStruct((T, wb), F32)],
        compiler_params=_params(1))(x, g_mix, w, gains)


def _qkv_post_bwd(raw, gains, d_a, d_b):
    T, W = raw.shape
    n_a, n_b, n_kv = _qkv_layout(W * 4 // 9)
    tm = _pick_tm(T, 256)
    o_qb = 3 * n_a

    def body(raw_ref, g_ref, daq, dak, dav, dbq, dbk, dbv, o_ref, dg_ref):
        @pl.when(pl.program_id(0) == 0)
        def _():
            dg_ref[...] = jnp.zeros_like(dg_ref)

        def cols(ref, cb):
            return ref[:, cb * LANES:(cb + 1) * LANES]

        def norm_bwd(cb, dy, scale):
            xv = cols(raw_ref, cb)
            gv = cols(g_ref, cb)
            r = lax.rsqrt(_seg_sum(xv * xv) * (1.0 / HEAD_DIM) + EPS)
            xh = xv * r
            dys = dy * scale
            dyg = dys * gv
            dxv = r * (dyg - xh * (_seg_sum(dyg * xh) * (1.0 / HEAD_DIM)))
            o_ref[:, cb * LANES:(cb + 1) * LANES] = dxv.astype(BF16)
            dg_ref[:, cb * LANES:(cb + 1) * LANES] += jnp.sum(dys * xh, axis=0, keepdims=True)

        def fold(ref, kv_blk):
            halves = []
            for half in range(2):
                kvh = 2 * kv_blk + half
                blocks = [e for e in range(n_b) if (2 * e) // GROUP_B == kvh]
                s = cols(ref, blocks[0])
                for e in blocks[1:]:
                    s = s + cols(ref, e)
                halves.append(s + pltpu.roll(s, HEAD_DIM, 1))
            return jnp.where(_lo_lanes(halves[0].shape), halves[0], halves[1])

        for cb in range(n_a):
            norm_bwd(cb, cols(daq, cb), Q_SCALE)
            norm_bwd(n_a + cb, cols(dak, cb), 1.0)
            cbv = 2 * n_a + cb
            o_ref[:, cbv * LANES:(cbv + 1) * LANES] = cols(dav, cb).astype(BF16)
        for cb in range(n_b):
            norm_bwd(o_qb + cb, cols(dbq, cb), Q_SCALE)
        for kb in range(n_kv):
            ck = o_qb + n_b + kb
            cv = ck + n_kv
            norm_bwd(ck, fold(dbk, kb), 1.0)
            o_ref[:, cv * LANES:(cv + 1) * LANES] = fold(dbv, kb).astype(BF16)

    hw_a, hw_b = n_a * LANES, n_b * LANES
    return pl.pallas_call(
        body, name="qkv_post_bwd", grid=(T // tm,),
        in_specs=[pl.BlockSpec((tm, W), lambda i: (i, 0)), pl.BlockSpec((1, W), lambda i: (0, 0))]
        + [pl.BlockSpec((tm, hw_a), lambda i: (i, 0))] * 3 + [pl.BlockSpec((tm, hw_b), lambda i: (i, 0))] * 3,
        out_specs=[pl.BlockSpec((tm, W), lambda i: (i, 0)), pl.BlockSpec((1, W), lambda i: (0, 0))],
        out_shape=[jax.ShapeDtypeStruct((T, W), BF16), jax.ShapeDtypeStruct((1, W), F32)],
        compiler_params=_params(1))(raw, gains, *d_a, *d_b)


def _t5_bucket_np(rel):
    half = N_BUCKETS // 2
    max_exact = half // 2
    ret = np.where(rel > 0, half, 0)
    n = np.abs(rel)
    nf = np.maximum(n, 1).astype(np.float32)
    large = max_exact + (np.log(nf / np.float32(max_exact)) / np.float32(math.log(MAX_DISTANCE / max_exact))
                         * np.float32(half - max_exact)).astype(np.int32)
    large = np.minimum(large, half - 1)
    return ret + np.where(n < max_exact, n, large)


def _window_pad(radius):
    assert radius <= QBLOCK
    return HEAD_DIM if radius <= HEAD_DIM else QBLOCK


def _bucket_maps(configs):
    pad = _window_pad(configs[0][0])
    q = np.arange(QBLOCK)[:, None]
    kk = np.arange(QBLOCK + 2 * pad)[None, :]
    rel = kk - pad - q
    maps = [np.where(np.abs(rel) <= radius, _t5_bucket_np(rel * dil), -1) for radius, dil in configs]
    return np.stack(maps).astype(np.int32)


def _bias_build(rel_bias, bmaps, col0):
    n_sets, _, W = bmaps.shape
    n_heads = rel_bias.shape[1] // 2

    def body(rb_ref, bm_ref, o_ref):
        h = pl.program_id(1)
        bm = bm_ref[...]

        def step(n, acc):
            return jnp.where(bm == n, rb_ref[n, col0 + h], acc)

        o_ref[...] = lax.fori_loop(0, N_BUCKETS, step, jnp.where(bm < 0, NEG, 0.0).astype(F32))

    return pl.pallas_call(
        body, name="bias_build", grid=(n_sets, n_heads),
        in_specs=[pl.BlockSpec(memory_space=pltpu.SMEM), pl.BlockSpec((None, QBLOCK, W), lambda s, h: (s, 0, 0))],
        out_specs=pl.BlockSpec((None, None, QBLOCK, W), lambda s, h: (s, h, 0, 0)),
        out_shape=jax.ShapeDtypeStruct((n_sets, n_heads, QBLOCK, W), F32),
        compiler_params=_params(2))(rel_bias, bmaps)


def _bias_grad(dtiles, bmaps, col0):
    n_sets, _, W = bmaps.shape
    n_heads = dtiles[0].shape[1]
    n_l = len(dtiles)

    def body(*refs):
        bm_ref, o_ref = refs[0], refs[1 + n_l]
        s, h = pl.program_id(0), pl.program_id(1)

        @pl.when((s == 0) & (h == 0))
        def _():
            o_ref[...] = jnp.zeros_like(o_ref)

        d = refs[1][...]
        for r in refs[2:1 + n_l]:
            d = d + r[...]
        acc8 = d[0:8, :]
        for a in range(1, QBLOCK // 8):
            acc8 = acc8 + pltpu.roll(d[8 * a:8 * a + 8, :], W - 8 * a, 1)
        per_offset = acc8[0:1, :]
        for b in range(1, 8):
            per_offset = per_offset + pltpu.roll(acc8[b:b + 1, :], W - b, 1)
        bucket = lax.broadcasted_iota(jnp.int32, (N_BUCKETS, W), 0)
        hit = bucket == bm_ref[0:1, :]
        per_bucket = jnp.sum(jnp.where(hit, per_offset, 0.0), axis=1, keepdims=True)
        lanes = lax.broadcasted_iota(jnp.int32, o_ref.shape, 1)
        o_ref[...] += jnp.where(lanes == col0 + h, per_bucket, 0.0)

    tile = pl.BlockSpec((None, None, QBLOCK, W), lambda s, h: (s, h, 0, 0))
    return pl.pallas_call(
        body, name="bias_grad", grid=(n_sets, n_heads),
        in_specs=[pl.BlockSpec((None, QBLOCK, W), lambda s, h: (s, 0, 0))] + [tile] * n_l,
        out_specs=pl.BlockSpec((N_BUCKETS, LANES), lambda s, h: (0, 0)),
        out_shape=jax.ShapeDtypeStruct((N_BUCKETS, LANES), F32), compiler_params=_params(2))(bmaps, *dtiles)


def _rows(l_start, n, d, r):
    if d == 1:
        return pl.ds(pl.multiple_of(l_start, 8), n)
    return pl.ds(l_start * d + r, n, stride=d)


def _stack_heads(xv, lo):
    z = jnp.zeros_like(xv)
    return jnp.concatenate([jnp.where(lo, xv, z), jnp.where(lo, z, xv)], axis=0)


def _unstack_heads(xv, lo):
    return jnp.where(lo, xv[:QBLOCK], xv[QBLOCK:])


def _per_head_rows(v0, v1):
    if jnp.ndim(v0) == 0:
        return jnp.where(lax.broadcasted_iota(jnp.int32, (2 * QBLOCK, 1), 0) < QBLOCK, v0, v1)
    return jnp.concatenate([v0, v1], axis=0)


def _block_geometry(b, nb_sub, pad):
    r, lb = b // nb_sub, b % nb_sub
    l0 = lb * QBLOCK
    lp = jnp.maximum(l0 - pad, 0)
    ln = jnp.minimum(l0 + QBLOCK, nb_sub * QBLOCK - pad)
    return r, l0, lp, ln, (lb == 0).astype(jnp.int32) + 2 * (lb == nb_sub - 1).astype(jnp.int32)


def _edge_variants(bias, pad):
    n_br, _, _, W = bias.shape
    col = np.arange(W)
    left, right = col < pad, col >= pad + QBLOCK
    masked = jnp.asarray(np.stack([np.zeros(W, bool), left, right, left | right]))
    return jnp.where(masked[None, :, None, :], NEG, bias.reshape(n_br, 1, -1, W))


def _window(ref, l0, lp, ln, pad, d, r):
    return jnp.concatenate([ref[_rows(lp, pad, d, r), :], ref[_rows(l0, QBLOCK, d, r), :],
                            ref[_rows(ln, pad, d, r), :]], axis=0)


def _attn_fwd(qkv, bias, sink, dils, pad):
    T = qkv.shape[0]
    hw = qkv.shape[1] // 3
    ng = hw // LANES
    n_br = len(dils)
    n_blocks = T // QBLOCK
    W = QBLOCK + 2 * pad
    chunk = 256

    def body(sink_ref, q_ref, k_ref, v_ref, bias_ref, o_ref, lse_ref, *scratch):
        g = pl.program_id(0)
        lo = _lo_lanes((QBLOCK, LANES))
        snk = _per_head_rows(sink_ref[2 * g], sink_ref[2 * g + 1])
        for c, d in enumerate(dils):
            nb_sub = n_blocks // d
            o_dst = scratch[0].at[c] if n_br > 1 else o_ref
            l_dst = scratch[1].at[c] if n_br > 1 else lse_ref

            def block(b, carry, c=c, d=d, nb_sub=nb_sub, o_dst=o_dst, l_dst=l_dst):
                r, l0, lp, ln, edge = _block_geometry(b, nb_sub, pad)
                q = _stack_heads(q_ref[_rows(l0, QBLOCK, d, r), :].astype(BF16), lo)
                k = _window(k_ref, l0, lp, ln, pad, d, r).astype(BF16)
                v = _window(v_ref, l0, lp, ln, pad, d, r).astype(BF16)
                s = _dot_nt(q, k) + bias_ref[c, edge]
                m = jnp.maximum(jnp.max(s, axis=1, keepdims=True), snk)
                p = jnp.exp(s - m)
                den = jnp.sum(p, axis=1, keepdims=True) + jnp.exp(snk - m)
                o_dst[_rows(l0, QBLOCK, d, r), :] = _unstack_heads(_dot(p.astype(BF16), v) / den, lo)
                l_dst[_rows(l0, QBLOCK, d, r), :] = _unstack_heads(
                    jnp.broadcast_to(m + jnp.log(den), (2 * QBLOCK, LANES)), lo)
                return carry

            lax.fori_loop(0, n_blocks, block, 0, unroll=FWD_UNROLL)

        if n_br > 1:
            def merge(i, carry):
                rs = pl.ds(pl.multiple_of(i * chunk, chunk), chunk)
                ls = [scratch[1][c, rs, :] for c in range(n_br)]
                m = ls[0]
                for t in ls[1:]:
                    m = jnp.maximum(m, t)
                ws = [jnp.exp(t - m) for t in ls]
                z = ws[0]
                acc = ws[0] * scratch[0][0, rs, :]
                for c in range(1, n_br):
                    z = z + ws[c]
                    acc = acc + ws[c] * scratch[0][c, rs, :]
                o_ref[rs, :] = acc / z
                lse_ref[rs, :] = m + jnp.log(z)
                return carry

            lax.fori_loop(0, T // chunk, merge, 0)

    def col(base):
        return pl.BlockSpec((T, LANES), lambda g: (0, base + g))

    out = jax.ShapeDtypeStruct((T, hw), F32)
    scratch = [pltpu.VMEM((n_br, T, LANES), F32)] * 2 if n_br > 1 else []
    return pl.pallas_call(
        body, name="attn_fwd", grid=(ng,),
        in_specs=[pl.BlockSpec(memory_space=pltpu.SMEM), col(0), col(ng), col(2 * ng),
                  pl.BlockSpec((n_br, 4, 2 * QBLOCK, W), lambda g: (0, 0, g, 0))],
        out_specs=[col(0), col(0)], out_shape=[out, out], scratch_shapes=scratch,
        compiler_params=_params(1))(sink, qkv, qkv, qkv, bias)


def _attn_bwd(qkv, bias, sink, dils, pad, do, lse, dd, col_base, after):
    T = qkv.shape[0]
    hw = qkv.shape[1] // 3
    ng = hw // LANES
    n_br = len(dils)
    n_blocks = T // QBLOCK
    W = QBLOCK + 2 * pad

    def body(sink_ref, q_ref, k_ref, v_ref, bias_ref, do_ref, lse_ref, dd_ref, after_ref,
             dq_ref, dk_ref, dv_ref, dt_ref, ds_ref):
        g = pl.program_id(0)
        dq_ref[...] = jnp.zeros_like(dq_ref)
        dk_ref[...] = jnp.zeros_like(dk_ref)
        dv_ref[...] = jnp.zeros_like(dv_ref)
        dt_ref[...] = jnp.zeros_like(dt_ref)
        ds_ref[...] = jnp.zeros_like(ds_ref)
        lo = _lo_lanes((QBLOCK, LANES))
        snk = jnp.where(lo, sink_ref[2 * g], sink_ref[2 * g + 1])
        for c, d in enumerate(dils):
            nb_sub = n_blocks // d

            def block(b, carry, c=c, d=d, nb_sub=nb_sub):
                r, l0, lp, ln, edge = _block_geometry(b, nb_sub, pad)
                rows_q = _rows(l0, QBLOCK, d, r)
                q = _stack_heads(q_ref[rows_q, :].astype(BF16), lo)
                k = _window(k_ref, l0, lp, ln, pad, d, r).astype(BF16)
                v = _window(v_ref, l0, lp, ln, pad, d, r).astype(BF16)
                dob = _stack_heads(do_ref[rows_q, :].astype(BF16), lo)
                lse_b = lse_ref[rows_q, :]
                dd_b = dd_ref[rows_q, :]
                s = _dot_nt(q, k) + bias_ref[c, edge]
                p = jnp.exp(s - _per_head_rows(lse_b[:, 0:1], lse_b[:, HEAD_DIM:HEAD_DIM + 1]))
                ds = p * (_dot_nt(dob, v) - _per_head_rows(dd_b[:, 0:1], dd_b[:, HEAD_DIM:HEAD_DIM + 1]))
                dsb = ds.astype(BF16)
                dkw = _dot_tn(dsb, q)
                dvw = _dot_tn(p.astype(BF16), dob)
                dt_ref[c] += ds
                dq_ref[rows_q, :] += _unstack_heads(_dot(dsb, k), lo)
                ds_ref[0:1, :] += jnp.sum(-jnp.exp(snk - lse_b) * dd_b, axis=0, keepdims=True)
                for part, (start, n) in zip((0, pad, pad + QBLOCK), ((lp, pad), (l0, QBLOCK), (ln, pad))):
                    dk_ref[_rows(start, n, d, r), :] += dkw[part:part + n]
                    dv_ref[_rows(start, n, d, r), :] += dvw[part:part + n]
                return carry

            lax.fori_loop(0, n_blocks, block, 0, unroll=BWD_UNROLL)

    def col(base):
        return pl.BlockSpec((T, LANES), lambda g: (0, base + g))

    tile = pl.BlockSpec((n_br, 2 * QBLOCK, W), lambda g: (0, g, 0))
    full = jax.ShapeDtypeStruct((T, hw), F32)
    dq, dk, dv, dt, dsink = pl.pallas_call(
        body, name="attn_bwd", grid=(ng,),
        in_specs=[pl.BlockSpec(memory_space=pltpu.SMEM), col(0), col(ng), col(2 * ng),
                  pl.BlockSpec((n_br, 4, 2 * QBLOCK, W), lambda g: (0, 0, g, 0)),
                  col(col_base), col(0), col(col_base), ANY_SPEC],
        out_specs=[col(0), col(0), col(0), tile, pl.BlockSpec((None, 8, LANES), lambda g: (g, 0, 0))],
        out_shape=[full, full, full, jax.ShapeDtypeStruct((n_br, 2 * ng * QBLOCK, W), F32),
                   jax.ShapeDtypeStruct((ng, 8, LANES), F32)],
        compiler_params=_params(1))(sink, qkv, qkv, qkv, bias, do, lse, dd, after)
    return dq, dk, dv, dt.reshape(n_br, 2 * ng, QBLOCK, W), dsink


def _mix_bwd_in(dx, wo, o_list):
    T, D = dx.shape
    widths = [o.shape[1] for o in o_list]
    hw = sum(widths)
    n = len(o_list)
    tm = _pick_tm(T, 256)
    nt = T // tm

    def body(*refs):
        dx_ref, w_ref, o_refs = refs[0], refs[1], refs[2:2 + n]
        do_ref, dd_ref, dw_ref, acc = refs[2 + n:]
        i = pl.program_id(0)

        @pl.when(i == 0)
        def _():
            acc[...] = jnp.zeros_like(acc)

        dxb = dx_ref[...].astype(BF16)
        dov = _dot_nt(dxb, w_ref[...])
        do_ref[...] = dov
        off = 0
        for o_ref, k in zip(o_refs, widths):
            ov = o_ref[...]
            prod = dov[:, off:off + k] * ov
            for cb in range(k // LANES):
                dd_ref[:, off + cb * LANES:off + (cb + 1) * LANES] = _seg_sum(prod[:, cb * LANES:(cb + 1) * LANES])
            acc[off:off + k, :] += _dot_tn(ov.astype(BF16), dxb)
            off += k

        @pl.when(i == nt - 1)
        def _():
            dw_ref[...] = acc[...].astype(BF16)

    row = pl.BlockSpec((tm, hw), lambda i: (i, 0))
    out = jax.ShapeDtypeStruct((T, hw), F32)
    return pl.pallas_call(
        body, name="mix_bwd_in", grid=(nt,),
        in_specs=[pl.BlockSpec((tm, D), lambda i: (i, 0)), pl.BlockSpec((hw, D), lambda i: (0, 0))]
        + [pl.BlockSpec((tm, k), lambda i: (i, 0)) for k in widths],
        out_specs=[row, row, pl.BlockSpec((hw, D), lambda i: (0, 0))],
        out_shape=[out, out, jax.ShapeDtypeStruct((hw, D), BF16)],
        scratch_shapes=[pltpu.VMEM((hw, D), F32)], compiler_params=_params(1))(dx, wo, *o_list)


def _mesh_pos():
    return lax.axis_index("x"), lax.axis_index("y"), lax.axis_index("c")


def _my_chip():
    return 2 * lax.axis_index("x") + lax.axis_index("y")


def _other_chips(x, y):
    return [(1 - x, y), (x, 1 - y), (1 - x, 1 - y)]


def _half_rows(rows, cc):
    hr = rows // 2
    return pl.ds(pl.multiple_of(cc * hr, 16), hr)


def _cast_into_slot(w, l):
    _, R, C = w.shape
    tm = _pick_tm(R, 512)

    def body(w_ref, o_ref):
        o_ref[...] = w_ref[...].astype(BF16)

    return pl.pallas_call(
        body, name="cast_into_slot", grid=(R // tm,),
        in_specs=[pl.BlockSpec((None, tm, C), lambda i: (l, i, 0))],
        out_specs=pl.BlockSpec((None, tm, C), lambda i: (_my_chip(), i, 0)),
        out_shape=jax.ShapeDtypeStruct((N_CHIPS, R, C), BF16), compiler_params=_params(1))(w)


def _split_start(name, arrays, make_copies, n_sem, after):
    n = len(arrays)
    n_in = n + (0 if after is None else 1)

    def body(*refs):
        send_s, recv_s = refs[n_in], refs[n_in + 1]
        token = refs[n_in + 2 + n]
        for send, _ in make_copies(refs[:n], send_s, recv_s):
            send.start()
        token[...] = jnp.zeros_like(token)

    res = pl.pallas_call(
        body, name=name,
        out_shape=(pltpu.SemaphoreType.DMA((n_sem,)), pltpu.SemaphoreType.DMA((n_sem,)),
                   *[pltpu.HBM(a.shape, a.dtype) for a in arrays], jax.ShapeDtypeStruct((8, LANES), F32)),
        in_specs=[HBM_SPEC] * n + [ANY_SPEC] * (n_in - n),
        out_specs=(SEM_SPEC, SEM_SPEC, *([HBM_SPEC] * n), pl.BlockSpec(memory_space=pltpu.VMEM)),
        input_output_aliases={i: 2 + i for i in range(n)},
        compiler_params=pltpu.CompilerParams(has_side_effects=pltpu.SideEffectType.DATAFLOW_SIDE_EFFECTING),
    )(*[pltpu.with_memory_space_constraint(a, pltpu.HBM) for a in arrays], *([] if after is None else [after]))
    return res[0], res[1], list(res[2:2 + n]), res[2 + n]


def _split_wait(name, send_s, recv_s, arrays, make_copies, after):
    n = len(arrays)

    def body(*refs):
        for send, landed in make_copies(refs[:n], refs[n], refs[n + 1]):
            send.wait_send()
            landed.wait_recv()

    return list(pl.pallas_call(
        body, name=name, out_shape=[pltpu.HBM(a.shape, a.dtype) for a in arrays],
        in_specs=[HBM_SPEC] * n + [SEM_SPEC, SEM_SPEC, ANY_SPEC], out_specs=[HBM_SPEC] * n,
        input_output_aliases={i: i for i in range(n)},
        compiler_params=pltpu.CompilerParams(has_side_effects=pltpu.SideEffectType.DATAFLOW_SIDE_EFFECTING),
    )(*arrays, send_s, recv_s, after))


def _gather_copies(shapes):
    n = len(shapes)

    def make(refs, send_s, recv_s):
        x, y, c = _mesh_pos()
        my = 2 * x + y
        copies = []
        for w in range(n):
            for k, (px, py) in enumerate(_other_chips(x, y)):
                def part(slot, w=w):
                    return refs[w].at[slot, _half_rows(shapes[w][1], c), :]
                sems = dict(send_sem=send_s.at[k * n + w], recv_sem=recv_s.at[k * n + w],
                            device_id=(px, py, c), device_id_type=MESH)
                copies.append((pltpu.make_async_remote_copy(src_ref=part(my), dst_ref=part(my), **sems),
                               pltpu.make_async_remote_copy(src_ref=part(2 * px + py), dst_ref=part(2 * px + py), **sems)))
        return copies

    return make


def _forward_copies(shapes):
    n = len(shapes)

    def make(refs, send_s, recv_s):
        x, y, c = _mesh_pos()
        copies = []
        for w in range(n):
            for k, (px, py) in enumerate(_other_chips(x, y)):
                def part(cc, w=w, slot=2 * px + py):
                    return refs[w].at[slot, _half_rows(shapes[w][1], cc), :]
                sems = dict(send_sem=send_s.at[k * n + w], recv_sem=recv_s.at[k * n + w],
                            device_id=(x, y, 1 - c), device_id_type=MESH)
                copies.append((pltpu.make_async_remote_copy(src_ref=part(c), dst_ref=part(c), **sems),
                               pltpu.make_async_remote_copy(src_ref=part(1 - c), dst_ref=part(1 - c), **sems)))
        return copies

    return make


def _pair_forward(bufs):
    n = len(bufs)
    make = _forward_copies([b.shape for b in bufs])

    def body(*refs):
        copies = make(refs[n:2 * n], refs[2 * n], refs[2 * n + 1])
        for send, _ in copies:
            send.start()
        for _, landed in copies:
            landed.wait_recv()
        for send, _ in copies:
            send.wait_send()

    return list(pl.pallas_call(
        body, name="ag_pair_forward", in_specs=[HBM_SPEC] * n, out_specs=[HBM_SPEC] * n,
        out_shape=[jax.ShapeDtypeStruct(b.shape, b.dtype) for b in bufs],
        input_output_aliases={w: w for w in range(n)},
        scratch_shapes=[pltpu.SemaphoreType.DMA((3 * n,)), pltpu.SemaphoreType.DMA((3 * n,))],
    )(*bufs))


def _pair_exchange_copies(shapes):
    n = len(shapes)

    def make(refs, send_s, recv_s):
        x, y, c = _mesh_pos()
        copies = []
        for t in range(n):
            sems = dict(send_sem=send_s.at[t], recv_sem=recv_s.at[t], device_id=(x, y, 1 - c), device_id_type=MESH)
            land = refs[n + t]
            copies.append((pltpu.make_async_remote_copy(
                src_ref=refs[t].at[:, _half_rows(shapes[t][1], 1 - c), :], dst_ref=land, **sems),
                pltpu.make_async_remote_copy(src_ref=land, dst_ref=land, **sems)))
        return copies

    return make


def _pair_share_copies(shapes):
    n = len(shapes)

    def make(refs, send_s, recv_s):
        x, y, c = _mesh_pos()
        copies = []
        for t in range(n):
            def half(cc, t=t):
                return refs[t].at[_half_rows(shapes[t][0], cc), :]
            sems = dict(send_sem=send_s.at[t], recv_sem=recv_s.at[t], device_id=(x, y, 1 - c), device_id_type=MESH)
            copies.append((pltpu.make_async_remote_copy(src_ref=half(c), dst_ref=half(c), **sems),
                           pltpu.make_async_remote_copy(src_ref=half(1 - c), dst_ref=half(1 - c), **sems)))
        return copies

    return make


def _rs_add_pair(grad, recv):
    n_slot, hr, C = recv.shape
    tm = _pick_tm(hr, 192)
    nb = hr // tm

    def body(a_ref, b_ref, o_ref):
        o_ref[...] = (a_ref[...].astype(F32) + b_ref[...].astype(F32)).astype(BF16)

    blk = pl.BlockSpec((n_slot, tm, C), lambda i: (0, i, 0))
    return pl.pallas_call(
        body, name="rs_add_pair", grid=(nb,),
        in_specs=[pl.BlockSpec((n_slot, tm, C), lambda i: (0, lax.axis_index("c") * nb + i, 0)), blk],
        out_specs=blk, out_shape=jax.ShapeDtypeStruct(recv.shape, BF16), compiler_params=_params(1))(grad, recv)


def _scatter_copies(n):
    def make(refs, send_s, recv_s):
        x, y, c = _mesh_pos()
        copies = []
        for t in range(n):
            for k, (px, py) in enumerate(_other_chips(x, y)):
                sems = dict(send_sem=send_s.at[3 * t + k], recv_sem=recv_s.at[3 * t + k],
                            device_id=(px, py, c), device_id_type=MESH)
                land = refs[n + t].at[k]
                copies.append((pltpu.make_async_remote_copy(src_ref=refs[t].at[2 * px + py], dst_ref=land, **sems),
                               pltpu.make_async_remote_copy(src_ref=land, dst_ref=land, **sems)))
        return copies

    return make


def _rs_add_chips(part, recv):
    _, hr, C = part.shape
    tm = _pick_tm(hr, 256)
    nb = hr // tm

    def body(a_ref, r0, r1, r2, o_ref):
        o_ref[...] = ((a_ref[...].astype(F32) + r0[...].astype(F32)) + r1[...].astype(F32)) + r2[...].astype(F32)

    def rel(k):
        return pl.BlockSpec((None, tm, C), lambda i: (k, i, 0))

    return pl.pallas_call(
        body, name="rs_add_chips", grid=(nb,),
        in_specs=[pl.BlockSpec((None, tm, C), lambda i: (_my_chip(), i, 0)), rel(0), rel(1), rel(2)],
        out_specs=pl.BlockSpec((tm, C), lambda i: (lax.axis_index("c") * nb + i, 0)),
        out_shape=jax.ShapeDtypeStruct((2 * hr, C), F32), compiler_params=_params(1))(part, recv, recv, recv)


def _allreduce_small(v, after):
    rows = v.shape[0]

    def body(v_ref, after_ref, o_ref, buf, send_s, recv_s):
        x, y, c = _mesh_pos()
        me = 4 * x + 2 * y + c
        buf[me] = v_ref[...]
        copies = []
        for r in range(1, N_DEV):
            px = 1 - x if r & 4 else x
            py = 1 - y if r & 2 else y
            pc = 1 - c if r & 1 else c
            send = pltpu.make_async_remote_copy(
                src_ref=v_ref, dst_ref=buf.at[me], send_sem=send_s.at[r - 1], recv_sem=recv_s.at[r - 1],
                device_id=(px, py, pc), device_id_type=MESH)
            peer_slot = buf.at[4 * px + 2 * py + pc]
            landed = pltpu.make_async_remote_copy(
                src_ref=peer_slot, dst_ref=peer_slot, send_sem=send_s.at[r - 1], recv_sem=recv_s.at[r - 1],
                device_id=(px, py, pc), device_id_type=MESH)
            copies.append((send, landed))
        for send, _ in copies:
            send.start()
        for _, landed in copies:
            landed.wait_recv()
        for send, _ in copies:
            send.wait_send()
        acc = buf[0]
        for j in range(1, N_DEV):
            acc = acc + buf[j]
        o_ref[...] = acc

    vm = pl.BlockSpec(memory_space=pltpu.VMEM)
    return pl.pallas_call(
        body, name="allreduce_small", in_specs=[vm, ANY_SPEC], out_specs=vm,
        out_shape=jax.ShapeDtypeStruct((rows, LANES), F32),
        scratch_shapes=[pltpu.VMEM((N_DEV, rows, LANES), F32), pltpu.SemaphoreType.DMA((N_DEV - 1,)),
                        pltpu.SemaphoreType.DMA((N_DEV - 1,))],
    )(v, after)


def _adamw_fn(w, g, m, v):
    m2 = ADAM_B1 * m + (1.0 - ADAM_B1) * g
    v2 = ADAM_B2 * v + (1.0 - ADAM_B2) * (g * g)
    m_hat = m2 / (1.0 - ADAM_B1 ** ADAM_STEP)
    v_hat = v2 / (1.0 - ADAM_B2 ** ADAM_STEP)
    delta = -ADAM_LR * (m_hat / (jnp.sqrt(v_hat) + ADAM_EPS) + ADAM_WD * w)
    return g, delta, m2, v2


def _adamw_layer(w, g, m, v, l, prev, after):
    NL, R, C = w.shape
    tm = _pick_tm(R, 256)
    n_prev = 0 if prev is None else 4

    def body(w_ref, g_ref, m_ref, v_ref, after_ref, *rest):
        outs = rest[n_prev:]
        for o_ref, val in zip(outs, _adamw_fn(w_ref[...], g_ref[...], m_ref[...], v_ref[...])):
            o_ref[...] = val

    lay = pl.BlockSpec((None, tm, C), lambda i: (l, i, 0))
    shape = jax.ShapeDtypeStruct((NL, R, C), F32)
    return pl.pallas_call(
        body, name="adamw", grid=(R // tm,),
        in_specs=[lay, pl.BlockSpec((tm, C), lambda i: (i, 0)), lay, lay, ANY_SPEC] + [ANY_SPEC] * n_prev,
        out_specs=[lay] * 4, out_shape=[shape] * 4,
        input_output_aliases={5 + j: j for j in range(n_prev)},
        compiler_params=_params(1))(w, g, m, v, after, *(prev or []))


def _pack_small(parts):
    out = []
    for a in parts:
        flat = a.reshape(-1)
        n = -(-flat.shape[0] // (8 * LANES)) * 8 * LANES
        out.append(jnp.pad(flat, (0, n - flat.shape[0])).reshape(-1, LANES))
    return jnp.concatenate(out, axis=0)


def _unpack_small(packed, like):
    out, r = [], 0
    for a in like:
        size = int(np.prod(a.shape))
        rows = -(-size // (8 * LANES)) * 8
        out.append(packed[r:r + rows].reshape(-1)[:size].reshape(a.shape))
        r += rows
    return out


def _ffn_forward(x, g, win, wout):
    h, silu, dgate, act = _ffn_up(x, g, win)
    if callable(wout):
        wout = wout(act)
    return _mm_res("ffn_down", x, [act], wout, 0.5), (x, h, silu, dgate, act)


def _ffn_backward(dx, saved, g, win, wout, after):
    x, h, silu, dgate, act = saved
    D = x.shape[1]
    wc = win.shape[2]
    dgu, dx_in, dg = _ffn_bwd(dx, wout, silu, dgate, win, x, g, after)
    dwout = _mm_tn("dw_ffn_out", act, wc, dx, D, 2, True, False, 0.5)
    dwin = _mm_tn_pairs("dw_ffn_in", h, dgu, wc, 4)
    return dx_in, dg, dwin, dwout.reshape(N_CHIPS, -1, D)


GROUP_MID = ("w_o", "ffn2_w_in", "ffn2_w_out", "w_ple_gate", "w_ple_proj")
GROUP_LAST = ("w_qkv", "ffn1_w_in", "ffn1_w_out")
GATHER_L0 = (("a", ("ffn1_w_in",)), ("b", ("ffn1_w_out",)), ("c", ("w_qkv", "w_o")),
             ("d", ("ffn2_w_in", "ffn2_w_out", "w_ple_gate", "w_ple_proj")))


def _gather_start(tag, slotted, after):
    return _split_start("ag_start_" + tag, slotted, _gather_copies([a.shape for a in slotted]), 3 * len(slotted), after)


def _gather_finish(tag, started, after):
    send_s, recv_s, arrays, _ = started
    return _pair_forward(_split_wait("ag_wait_" + tag, send_s, recv_s, arrays,
                                     _gather_copies([a.shape for a in arrays]), after))


def _scatter_exchange(tag, grads):
    n = len(grads)
    land = [lax.empty((g_.shape[0], g_.shape[1] // 2, g_.shape[2]), g_.dtype) for g_ in grads]
    return _split_start("rs_px_start_" + tag, list(grads) + land, _pair_exchange_copies([g_.shape for g_ in grads]),
                        n, None)


def _scatter_chips(tag, started, after):
    send_s, recv_s, arrays, _ = started
    n = len(arrays) // 2
    arrays = _split_wait("rs_px_wait_" + tag, send_s, recv_s, arrays,
                         _pair_exchange_copies([a.shape for a in arrays[:n]]), after)
    part = [_rs_add_pair(g_, r_) for g_, r_ in zip(arrays[:n], arrays[n:])]
    land = [lax.empty((3,) + p_.shape[1:], p_.dtype) for p_ in part]
    return _split_start("rs_start_" + tag, part + land, _scatter_copies(n), 3 * n, None)


def _scatter_share(tag, started, after):
    send_s, recv_s, arrays, _ = started
    n = len(arrays) // 2
    arrays = _split_wait("rs_wait_" + tag, send_s, recv_s, arrays, _scatter_copies(n), after)
    halves = [_rs_add_chips(p_, r_) for p_, r_ in zip(arrays[:n], arrays[n:])]
    return _split_start("rs_ps_start_" + tag, halves, _pair_share_copies([h_.shape for h_ in halves]), n, None)


def _scatter_done(tag, started, after):
    send_s, recv_s, arrays, _ = started
    return _split_wait("rs_ps_wait_" + tag, send_s, recv_s, arrays, _pair_share_copies([a.shape for a in arrays]), after)


def kernel(x, p, rel_bias, norm_ffn1, ffn1_w_in, ffn1_w_out, norm_mix, w_qkv, q_norm_a, k_norm_a, q_norm_b, k_norm_b, sink_b, w_o, norm_ffn2, ffn2_w_in, ffn2_w_out, norm_ple, w_ple_gate, w_ple_proj, loss_target, m_rel_bias, m_norm_ffn1, m_ffn1_w_in, m_ffn1_w_out, m_norm_mix, m_w_qkv, m_q_norm_a, m_k_norm_a, m_q_norm_b, m_k_norm_b, m_sink_b, m_w_o, m_norm_ffn2, m_ffn2_w_in, m_ffn2_w_out, m_norm_ple, m_w_ple_gate, m_w_ple_proj, v_rel_bias, v_norm_ffn1, v_ffn1_w_in, v_ffn1_w_out, v_norm_mix, v_w_qkv, v_q_norm_a, v_k_norm_a, v_q_norm_b, v_k_norm_b, v_sink_b, v_w_o, v_norm_ffn2, v_ffn2_w_in, v_ffn2_w_out, v_norm_ple, v_w_ple_gate, v_w_ple_proj):
    given = dict(locals())
    T, D = x.shape[1], x.shape[2]
    NL = norm_ffn1.shape[0]
    x0 = x.reshape(T, D)
    tgt = loss_target.reshape(T, D)
    n_a, n_b, n_kv = _qkv_layout(D)

    assert NL == 2
    slot = [{name: _cast_into_slot(given[name], l) for name in BIG} for l in range(NL)]
    ag, token = {}, None
    for tag, names in GATHER_L0:
        ag[tag] = _gather_start(tag, [slot[0][n] for n in names], token)
        token = ag[tag][3]
    ag_1 = _gather_start("1", [slot[1][n] for n in BIG], token)

    def arrived(tag, after):
        return dict(zip(dict(GATHER_L0)[tag], _gather_finish(tag, ag[tag], after)))

    def by_rows(a):
        return a.reshape(-1, a.shape[-1])

    def by_cols(a):
        return a.transpose(1, 0, 2).reshape(a.shape[1], -1)

    QW = N_CHIPS * w_qkv.shape[2]

    dils = tuple(d for _, d in DILATED_CONFIGS)
    cfg_a = [(w // (2 * d), d) for w, d in DILATED_CONFIGS]
    pad_a, pad_b = _window_pad(cfg_a[0][0]), _window_pad(SWA_RADIUS)
    bmaps_a, bmaps_b = jnp.asarray(_bucket_maps(cfg_a)), jnp.asarray(_bucket_maps([(SWA_RADIUS, 1)]))
    n_heads = rel_bias.shape[1] // 2
    bias_a = _edge_variants(_bias_build(rel_bias, bmaps_a, 0), pad_a)
    bias_b = _edge_variants(_bias_build(rel_bias, bmaps_b, n_heads), pad_b)
    no_sink = jnp.full((n_heads,), NEG, F32)

    def gains_row(l):
        ones = jnp.ones((n_a * LANES,), F32)
        return jnp.concatenate([
            jnp.tile(q_norm_a[l], 2 * n_a), jnp.tile(k_norm_a[l], 2 * n_a), ones,
            jnp.tile(q_norm_b[l], 2 * n_b), jnp.tile(k_norm_b[l], 2 * n_kv), jnp.ones((n_kv * LANES,), F32)]).reshape(1, QW)

    saved, weights = [], []
    xc = x0
    pf_1 = None
    for l in range(NL):
        s, w = {}, {}
        if l == 0:
            w.update(arrived("a", ag_1[3]))

            def ffn1_w_out(act, w=w):
                w.update(arrived("b", act))
                w["ffn1_w_out"] = by_rows(w["ffn1_w_out"])
                return w["ffn1_w_out"]
        else:
            shapes = [a.shape for a in pf_1[2]]
            w.update(zip(BIG, _split_wait("ag_pf_wait_1", pf_1[0], pf_1[1], pf_1[2], _forward_copies(shapes), xc)))
            ffn1_w_out = w["ffn1_w_out"] = by_rows(w["ffn1_w_out"])
        xc, s["ffn1"] = _ffn_forward(xc, norm_ffn1[l:l + 1], w["ffn1_w_in"], ffn1_w_out)
        s["x1"] = xc
        if l == 0:
            w.update(arrived("c", xc))
        w["w_qkv"] = by_cols(w["w_qkv"])
        w["w_o"] = by_rows(w["w_o"])
        s["h2"], s["raw"], s["qkv_a"], s["qkv_b"] = _qkv_proj(xc, norm_mix[l:l + 1], w["w_qkv"], gains_row(l))
        s["o_a"], s["lse_a"] = _attn_fwd(s["qkv_a"], bias_a, no_sink, dils, pad_a)
        s["o_b"], s["lse_b"] = _attn_fwd(s["qkv_b"], bias_b, sink_b[l], (1,), pad_b)
        xc = _mm_res("attn_out", xc, [s["o_a"], s["o_b"]], w["w_o"], 1.0)
        if l == 0:
            w.update(arrived("d", xc))
        w["w_ple_proj"] = by_cols(w["w_ple_proj"])
        for name in ("ffn2_w_out", "w_ple_gate"):
            w[name] = by_rows(w[name])
        xc, s["ffn2"] = _ffn_forward(xc, norm_ffn2[l:l + 1], w["ffn2_w_in"], w["ffn2_w_out"])
        s["x3"] = xc
        if l == 0:
            landed = _split_wait("ag_wait_1", ag_1[0], ag_1[1], ag_1[2], _gather_copies([a.shape for a in ag_1[2]]), xc)
            pf_1 = _split_start("ag_pf_start_1", landed, _forward_copies([a.shape for a in landed]), 3 * len(landed),
                                None)
        s["p"] = p[l].reshape(T, -1)
        s["hn"], xc, s["gate"], s["pp"] = _ple_fwd(xc, norm_ple[l:l + 1], s["p"], w["w_ple_gate"], w["w_ple_proj"],
                                                   pf_1[3] if l == 0 else xc)
        saved.append(s)
        weights.append(w)

    dx, loss_blk = _loss_fwd_bwd(xc, tgt)
    loss = lax.psum(loss_blk[0, 0], ("x", "y", "c"))

    gs = {name: [None] * NL for name in SMALL if name != "rel_bias"}
    dt_a, dt_b = [], []

    def layer_backward(l, dx, hooks):
        def at(point, ready, *more):
            return hooks[point](ready, *more) if point in hooks else ready

        s, w, gw = saved[l], weights[l], {}
        dx, gs["norm_ple"][l], dwg, dwp = _ple_bwd(dx, s["gate"], s["pp"], s["hn"], s["p"], s["x3"],
                                                   norm_ple[l:l + 1], w["w_ple_gate"], at("start", dx))
        gw["w_ple_gate"] = dwg.reshape(N_CHIPS, -1, D)
        gw["w_ple_proj"] = dwp.reshape(dwp.shape[0], N_CHIPS, -1).transpose(1, 0, 2)
        dx, gs["norm_ffn2"][l], gw["ffn2_w_in"], gw["ffn2_w_out"] = _ffn_backward(
            dx, s["ffn2"], norm_ffn2[l:l + 1], w["ffn2_w_in"], w["ffn2_w_out"], at("after_ple", dx))
        do, dd, dwo = _mix_bwd_in(dx, w["w_o"], [s["o_a"], s["o_b"]])
        hwa = s["o_a"].shape[1]
        gw["w_o"] = dwo.reshape(N_CHIPS, -1, D)
        dqa, dka, dva, dt, _ = _attn_bwd(s["qkv_a"], bias_a, no_sink, dils, pad_a, do, s["lse_a"], dd, 0, do)
        dt_a.append(dt)
        dqb, dkb, dvb, dt, dsink = _attn_bwd(s["qkv_b"], bias_b, sink_b[l], (1,), pad_b, do, s["lse_b"], dd,
                                             hwa // LANES, at("after_attn_a", dqa))
        dt_b.append(dt)
        gs["sink_b"][l] = dsink[:, 0, ::HEAD_DIM].reshape(-1)
        draw, dgains = _qkv_post_bwd(s["raw"], gains_row(l), (dqa, dka, dva), (dqb, dkb, dvb))
        dgv = dgains.reshape(-1, HEAD_DIM)
        gs["q_norm_a"][l] = dgv[:2 * n_a].sum(0)
        gs["k_norm_a"][l] = dgv[2 * n_a:4 * n_a].sum(0)
        gs["q_norm_b"][l] = dgv[6 * n_a:6 * n_a + 2 * n_b].sum(0)
        gs["k_norm_b"][l] = dgv[6 * n_a + 2 * n_b:6 * n_a + 2 * n_b + 2 * n_kv].sum(0)
        dx, gs["norm_mix"][l], dwqkv = _proj_bwd("qkv_bwd", draw, w["w_qkv"], s["h2"], s["x1"], norm_mix[l:l + 1], dx,
                                                 at("before_qkv_bwd", draw, [gw[n] for n in GROUP_MID]))
        gw["w_qkv"] = dwqkv.reshape(D, N_CHIPS, -1).transpose(1, 0, 2)
        dx, gs["norm_ffn1"][l], gw["ffn1_w_in"], gw["ffn1_w_out"] = _ffn_backward(
            dx, s["ffn1"], norm_ffn1[l:l + 1], w["ffn1_w_in"], w["ffn1_w_out"], at("before_ffn1", dx))
        return dx, gw

    out = {}

    def adamw_group(names, l, grads, after):
        for name, g_ in zip(names, grads):
            out[name] = _adamw_layer(given[name], g_, given["m_" + name], given["v_" + name], l, out.get(name), after)
            after = out[name][0]
        return after

    dx, gw1 = layer_backward(NL - 1, dx, {})
    px_1 = _scatter_exchange("1", [gw1[n] for n in BIG])
    rs = {}

    def chips_1(ready):
        rs["chips_1"] = _scatter_chips("1", px_1, ready)
        return rs["chips_1"][3]

    def share_1(ready):
        rs["share_1"] = _scatter_share("1", rs["chips_1"], ready)
        return rs["share_1"][3]

    def exchange_0a(ready, grads):
        rs["px_0a"] = _scatter_exchange("0a", grads)
        return rs["px_0a"][3]

    def chips_0a(ready):
        rs["chips_0a"] = _scatter_chips("0a", rs["px_0a"], ready)
        return rs["chips_0a"][3]

    dx, gw0 = layer_backward(0, dx, {"start": lambda ready: px_1[3], "after_ple": chips_1, "after_attn_a": share_1,
                                     "before_qkv_bwd": exchange_0a, "before_ffn1": chips_0a})
    grad_x = dx.reshape(x.shape)

    px_0b = _scatter_exchange("0b", [gw0[n] for n in GROUP_LAST])
    d_rel_bias = (_bias_grad(dt_a, bmaps_a, 0) + _bias_grad(dt_b, bmaps_b, n_heads))[:, :rel_bias.shape[1]]
    small_g = [d_rel_bias] + [jnp.stack([t.reshape(-1) for t in gs[name]]) for name in SMALL[1:]]
    g_sum = _allreduce_small(_pack_small(small_g), px_0b[3])
    res = _ew("adamw_small", _adamw_fn,
              [_pack_small([given[n] for n in SMALL]), g_sum, _pack_small([given["m_" + n] for n in SMALL]),
               _pack_small([given["v_" + n] for n in SMALL])], [(LANES, F32)] * 4)
    like = [given[n] for n in SMALL]
    unpacked = [_unpack_small(r, like) for r in res]
    for i, name in enumerate(SMALL):
        out[name] = [u[i] for u in unpacked]

    share_0a = _scatter_share("0a", rs["chips_0a"], res[0])
    g_1 = _scatter_done("1", rs["share_1"], share_0a[3])
    chips_0b = _scatter_chips("0b", px_0b, g_1[0])
    ready = adamw_group(BIG, 1, g_1, chips_0b[3])
    ready = adamw_group(GROUP_MID, 0, _scatter_done("0a", share_0a, ready), ready)
    share_0b = _scatter_share("0b", chips_0b, ready)
    adamw_group(GROUP_LAST, 0, _scatter_done("0b", share_0b, share_0b[3]), share_0b[3])

    return (loss, grad_x, *[out[n][0] for n in WEIGHTS], *[out[n][1] for n in WEIGHTS],
            *[out[n][2] for n in WEIGHTS], *[out[n][3] for n in WEIGHTS])
```

```python
import functools
import math

import numpy as np
import jax
import jax.numpy as jnp
from jax import lax
from jax.experimental import pallas as pl
from jax.experimental.pallas import tpu as pltpu

F32 = jnp.float32
BF16 = jnp.bfloat16
MESH = pl.DeviceIdType.MESH

HEAD_DIM = 64
LANES = 128
QBLOCK = 128
FWD_UNROLL, BWD_UNROLL = 8, 8
N_BUCKETS = 32
MAX_DISTANCE = 1024
DILATED_CONFIGS = ((128, 1), (512, 4), (2048, 16))
SWA_RADIUS = 128
GROUP_B = 4
EPS = 1e-6
NEG = -1e30
Q_SCALE = HEAD_DIM ** -0.5
ADAM_LR, ADAM_B1, ADAM_B2, ADAM_EPS, ADAM_WD, ADAM_STEP = 0.001, 0.9, 0.999, 1e-08, 0.01, 10
VMEM_LIMIT = 56 * 2 ** 20
N_CHIPS = 4
N_DEV = 8

BIG = ("ffn1_w_in", "ffn1_w_out", "w_qkv", "w_o", "ffn2_w_in", "ffn2_w_out", "w_ple_gate", "w_ple_proj")
SMALL = ("rel_bias", "norm_ffn1", "norm_mix", "q_norm_a", "k_norm_a", "q_norm_b", "k_norm_b", "sink_b",
         "norm_ffn2", "norm_ple")
WEIGHTS = ("rel_bias", "norm_ffn1", "ffn1_w_in", "ffn1_w_out", "norm_mix", "w_qkv", "q_norm_a", "k_norm_a",
           "q_norm_b", "k_norm_b", "sink_b", "w_o", "norm_ffn2", "ffn2_w_in", "ffn2_w_out", "norm_ple",
           "w_ple_gate", "w_ple_proj")


HBM_SPEC = pl.BlockSpec(memory_space=pltpu.HBM)
ANY_SPEC = pl.BlockSpec(memory_space=pl.ANY)
SEM_SPEC = pl.BlockSpec(memory_space=pltpu.SEMAPHORE)


def _params(n_grid):
    return pltpu.CompilerParams(dimension_semantics=("arbitrary",) * n_grid, vmem_limit_bytes=VMEM_LIMIT)


def _pick_tm(rows, cap):
    t = (min(cap, rows) // 16) * 16
    while t >= 16:
        if rows % t == 0:
            return t
        t -= 16
    return rows


def _dot(a, b):
    return jnp.dot(a, b, preferred_element_type=F32)


def _dot_nt(a, b):
    return lax.dot_general(a, b, (((1,), (1,)), ((), ())), preferred_element_type=F32)


def _dot_tn(a, b):
    return lax.dot_general(a, b, (((0,), (0,)), ((), ())), preferred_element_type=F32)


def _sigmoid(z):
    return 1.0 / (1.0 + jnp.exp(-z))


def _lo_lanes(shape):
    return lax.broadcasted_iota(jnp.int32, shape, len(shape) - 1) % LANES < HEAD_DIM


def _seg_sum(blk):
    lo = _lo_lanes(blk.shape)
    s_lo = jnp.sum(jnp.where(lo, blk, 0.0), axis=1, keepdims=True)
    s_hi = jnp.sum(jnp.where(lo, 0.0, blk), axis=1, keepdims=True)
    return jnp.where(lo, s_lo, s_hi)


def _rms_bwd_tile(x, g, dh):
    r = lax.rsqrt(jnp.mean(x * x, axis=-1, keepdims=True) + EPS)
    xh = x * r
    dyg = dh * g
    dx = r * (dyg - xh * jnp.mean(dyg * xh, axis=-1, keepdims=True))
    return dx, jnp.sum(dh * xh, axis=0, keepdims=True)


def _ew(name, fn, ins, out_defs, cap=512):
    rows = ins[0].shape[0]
    tm = _pick_tm(rows, cap)
    n_in = len(ins)

    def body(*refs):
        vals = fn(*[r[...] for r in refs[:n_in]])
        if not isinstance(vals, tuple):
            vals = (vals,)
        for r, v in zip(refs[n_in:], vals):
            r[...] = v.astype(r.dtype)

    return pl.pallas_call(
        body, name=name, grid=(rows // tm,),
        in_specs=[pl.BlockSpec((tm, a.shape[1]), lambda i: (i, 0)) for a in ins],
        out_specs=[pl.BlockSpec((tm, c), lambda i: (i, 0)) for c, _ in out_defs],
        out_shape=[jax.ShapeDtypeStruct((rows, c), dt) for c, dt in out_defs],
        compiler_params=_params(1))(*ins)


def _rms_tile(xv, gv):
    r = lax.rsqrt(jnp.mean(xv * xv, axis=-1, keepdims=True) + EPS)
    return (xv * r * gv).astype(BF16)


def _ffn_up(x, g, win):
    T, D = x.shape
    wc = win.shape[2]
    tm = _pick_tm(T, 512)

    def body(x_ref, g_ref, wg_ref, wu_ref, h_ref, silu_ref, dgate_ref, act_ref, wcat):
        @pl.when(pl.program_id(1) == 0)
        def _():
            wcat[:, :wc] = wg_ref[...]
            wcat[:, wc:] = wu_ref[...]

        hv = _rms_tile(x_ref[...], g_ref[...])
        h_ref[...] = hv
        gu = _dot(hv, wcat[...])
        gte, u = gu[:, :wc], gu[:, wc:]
        sg = _sigmoid(gte)
        silu = gte * sg
        silu_ref[...] = silu.astype(BF16)
        dgate_ref[...] = ((sg + silu * (1.0 - sg)) * u).astype(BF16)
        act_ref[...] = (silu * u).astype(BF16)

    out = jax.ShapeDtypeStruct((T, 2 * wc), BF16)
    ospec = pl.BlockSpec((tm, wc), lambda j, i: (i, j))
    nt = T // tm
    h_spec = pl.BlockSpec((tm, D), lambda j, i: (jnp.where(j == 0, i, nt), 0))
    return pl.pallas_call(
        body, name="ffn_up", grid=(2, nt),
        in_specs=[pl.BlockSpec((tm, D), lambda j, i: (i, 0)), pl.BlockSpec((1, D), lambda j, i: (0, 0)),
                  pl.BlockSpec((None, D, wc), lambda j, i: (j, 0, 0)),
                  pl.BlockSpec((None, D, wc), lambda j, i: (j + 2, 0, 0))],
        out_specs=[h_spec] + [ospec] * 3, out_shape=[jax.ShapeDtypeStruct((T + tm, D), BF16)] + [out] * 3,
        scratch_shapes=[pltpu.VMEM((D, 2 * wc), BF16)], compiler_params=_params(2))(x, g, win, win)


def _mm_res(name, res, a_list, w, scale):
    T, N = res.shape
    n = len(a_list)
    widths = [a.shape[1] for a in a_list]
    tm = _pick_tm(T, 512)

    def body(*refs):
        r_ref, a_refs, w_refs, o_ref = refs[0], refs[1:1 + n], refs[1 + n:1 + 2 * n], refs[1 + 2 * n]
        acc = _dot(a_refs[0][...].astype(BF16), w_refs[0][...])
        for a_ref, w_ref in zip(a_refs[1:], w_refs[1:]):
            acc = acc + _dot(a_ref[...].astype(BF16), w_ref[...])
        o_ref[...] = r_ref[...] + scale * acc

    w_specs, off = [], 0
    for k in widths:
        w_specs.append(pl.BlockSpec((k, N), lambda i, blk=off // k: (blk, 0)))
        off += k
    return pl.pallas_call(
        body, name=name, grid=(T // tm,),
        in_specs=[pl.BlockSpec((tm, N), lambda i: (i, 0))]
        + [pl.BlockSpec((tm, k), lambda i: (i, 0)) for k in widths] + w_specs,
        out_specs=pl.BlockSpec((tm, N), lambda i: (i, 0)),
        out_shape=jax.ShapeDtypeStruct((T, N), F32), compiler_params=_params(1))(res, *a_list, *([w] * n))


def _mm_tn(name, a, a_w, b, b_w, n_slots, a_by_slot, b_by_slot, scale, tm_cap=512):
    T = b.shape[0]
    tm = _pick_tm(T, tm_cap)
    nt = T // tm

    def body(a_ref, b_ref, o_ref, acc):
        i = pl.program_id(1)

        @pl.when(i == 0)
        def _():
            acc[...] = jnp.zeros_like(acc)

        acc[...] += _dot_tn(a_ref[...].astype(BF16), b_ref[...].astype(BF16))

        @pl.when(i == nt - 1)
        def _():
            o_ref[...] = (acc[...] * scale).astype(BF16)

    return pl.pallas_call(
        body, name=name, grid=(n_slots, nt),
        in_specs=[pl.BlockSpec((tm, a_w), (lambda s, i: (i, s)) if a_by_slot else (lambda s, i: (i, 0))),
                  pl.BlockSpec((tm, b_w), (lambda s, i: (i, s)) if b_by_slot else (lambda s, i: (i, 0)))],
        out_specs=pl.BlockSpec((None, a_w, b_w), lambda s, i: (s, 0, 0)),
        out_shape=jax.ShapeDtypeStruct((n_slots, a_w, b_w), BF16),
        scratch_shapes=[pltpu.VMEM((a_w, b_w), F32)], compiler_params=_params(2))(a, b)


def _mm_tn_pairs(name, a, b, b_w, n_slots):
    T = b.shape[0]
    a_w = a.shape[1]
    tm = _pick_tm(T, 512)
    nt = T // tm

    def body(a_ref, b_ref, o_ref, acc):
        i = pl.program_id(1)

        @pl.when(i == 0)
        def _():
            acc[...] = jnp.zeros_like(acc)

        acc[...] += _dot_tn(a_ref[...], b_ref[...])

        @pl.when(i == nt - 1)
        def _():
            o_ref[0] = acc[:, :b_w].astype(BF16)
            o_ref[1] = acc[:, b_w:].astype(BF16)

    return pl.pallas_call(
        body, name=name, grid=(n_slots // 2, nt),
        in_specs=[pl.BlockSpec((tm, a_w), lambda s, i: (i, 0)), pl.BlockSpec((tm, 2 * b_w), lambda s, i: (i, s))],
        out_specs=pl.BlockSpec((2, a_w, b_w), lambda s, i: (s, 0, 0)),
        out_shape=jax.ShapeDtypeStruct((n_slots, a_w, b_w), BF16),
        scratch_shapes=[pltpu.VMEM((a_w, 2 * b_w), F32)], compiler_params=_params(2))(a, b)


def _ffn_bwd(dx, wout, silu, dgate, win, x, g, after):
    T, D = dx.shape
    F = silu.shape[1]
    n_slots, _, wc = win.shape
    tm = _pick_tm(T, 256)

    def body(dx_ref, wout_hbm, s_ref, dgt_ref, win_hbm, x_ref, g_ref, after_ref,
             dgu_ref, dxo_ref, dg_ref, wout_v, wcat, sem):
        @pl.when(pl.program_id(0) == 0)
        def _():
            copies = [pltpu.make_async_copy(wout_hbm, wout_v, sem.at[n_slots])]
            copies += [pltpu.make_async_copy(win_hbm.at[s], wcat.at[s // 2, :, pl.ds((s % 2) * wc, wc)], sem.at[s])
                       for s in range(n_slots)]
            for cp in copies:
                cp.start()
            for cp in copies:
                cp.wait()
            dg_ref[...] = jnp.zeros_like(dg_ref)

        dxv = dx_ref[...]
        dact = 0.5 * _dot_nt(dxv.astype(BF16), wout_v[...])
        d_gate = (dact * dgt_ref[...].astype(F32)).astype(BF16)
        d_up = (dact * s_ref[...].astype(F32)).astype(BF16)
        dgu_ref[:, :F] = d_gate
        dgu_ref[:, F:] = d_up
        dh = _dot_nt(d_gate, wcat[0]) + _dot_nt(d_up, wcat[1])
        dxn, dg = _rms_bwd_tile(x_ref[...], g_ref[...], dh)
        dxo_ref[...] = dxv + dxn
        dg_ref[...] += dg

    row = pl.BlockSpec((tm, D), lambda i: (i, 0))
    vec = pl.BlockSpec((1, D), lambda i: (0, 0))
    act_spec = pl.BlockSpec((tm, F), lambda i: (i, 0))
    return pl.pallas_call(
        body, name="ffn_bwd", grid=(T // tm,),
        in_specs=[row, ANY_SPEC, act_spec, act_spec, ANY_SPEC, row, vec, ANY_SPEC],
        out_specs=[pl.BlockSpec((tm, 2 * F), lambda i: (i, 0)), row, vec],
        out_shape=[jax.ShapeDtypeStruct((T, 2 * F), BF16), jax.ShapeDtypeStruct((T, D), F32),
                   jax.ShapeDtypeStruct((1, D), F32)],
        scratch_shapes=[pltpu.VMEM((F, D), BF16), pltpu.VMEM((n_slots // 2, D, 2 * wc), BF16),
                        pltpu.SemaphoreType.DMA((n_slots + 1,))],
        compiler_params=_params(1))(dx, wout, silu, dgate, win, x, g, after)


def _ple_fwd(x, g, p, wg, wp, after):
    T, D = x.shape
    P = p.shape[1]
    tm = _pick_tm(T, 256)

    def body(x_ref, g_ref, p_ref, wg_ref, wp_ref, after_ref, hn_ref, xo_ref, gate_ref, pp_ref):
        xv = x_ref[...]
        hn = _rms_tile(xv, g_ref[...])
        hn_ref[...] = hn
        gate = _sigmoid(_dot(hn, wg_ref[...]))
        pp = _dot(p_ref[...].astype(BF16), wp_ref[...])
        gate_ref[...] = gate.astype(BF16)
        pp_ref[...] = pp.astype(BF16)
        xo_ref[...] = xv + gate * pp

    row = pl.BlockSpec((tm, D), lambda i: (i, 0))
    out = jax.ShapeDtypeStruct((T, D), F32)
    half = jax.ShapeDtypeStruct((T, D), BF16)
    return pl.pallas_call(
        body, name="ple_fwd", grid=(T // tm,),
        in_specs=[row, pl.BlockSpec((1, D), lambda i: (0, 0)), pl.BlockSpec((tm, P), lambda i: (i, 0)),
                  pl.BlockSpec((D, D), lambda i: (0, 0)), pl.BlockSpec((P, D), lambda i: (0, 0)), ANY_SPEC],
        out_specs=[row, row, row, row], out_shape=[half, out, half, half],
        compiler_params=_params(1))(x, g, p, wg, wp, after)


def _ple_bwd(dx, gate, pp, hn, p, x, g, wg, after):
    T, D = x.shape
    P = p.shape[1]
    tm = _pick_tm(T, 256)
    nt = T // tm

    def body(dx_ref, gate_ref, pp_ref, hn_ref, p_ref, x_ref, g_ref, wg_ref, after_ref,
             dxo_ref, dg_ref, dwg_ref, dwp_ref, acc_g, acc_p):
        i = pl.program_id(0)

        @pl.when(i == 0)
        def _():
            acc_g[...] = jnp.zeros_like(acc_g)
            acc_p[...] = jnp.zeros_like(acc_p)
            dg_ref[...] = jnp.zeros_like(dg_ref)

        dxv = dx_ref[...]
        gate = gate_ref[...].astype(F32)
        dz = (dxv * pp_ref[...].astype(F32) * gate * (1.0 - gate)).astype(BF16)
        dpp = (dxv * gate).astype(BF16)
        acc_g[...] += _dot_tn(hn_ref[...], dz)
        acc_p[...] += _dot_tn(p_ref[...].astype(BF16), dpp)
        dxn, dg = _rms_bwd_tile(x_ref[...], g_ref[...], _dot_nt(dz, wg_ref[...]))
        dxo_ref[...] = dxv + dxn
        dg_ref[...] += dg

        @pl.when(i == nt - 1)
        def _():
            dwg_ref[...] = acc_g[...].astype(BF16)
            dwp_ref[...] = acc_p[...].astype(BF16)

    row = pl.BlockSpec((tm, D), lambda i: (i, 0))
    vec = pl.BlockSpec((1, D), lambda i: (0, 0))
    return pl.pallas_call(
        body, name="ple_bwd", grid=(nt,),
        in_specs=[row, row, row, row, pl.BlockSpec((tm, P), lambda i: (i, 0)), row, vec,
                  pl.BlockSpec((D, D), lambda i: (0, 0)), ANY_SPEC],
        out_specs=[row, vec, pl.BlockSpec((D, D), lambda i: (0, 0)), pl.BlockSpec((P, D), lambda i: (0, 0))],
        out_shape=[jax.ShapeDtypeStruct((T, D), F32), jax.ShapeDtypeStruct((1, D), F32),
                   jax.ShapeDtypeStruct((D, D), BF16), jax.ShapeDtypeStruct((P, D), BF16)],
        scratch_shapes=[pltpu.VMEM((D, D), F32), pltpu.VMEM((P, D), F32)],
        compiler_params=_params(1))(dx, gate, pp, hn, p, x, g, wg, after)


def _loss_fwd_bwd(y, tgt):
    T, D = y.shape
    tm = _pick_tm(T, 512)

    def body(y_ref, t_ref, dy_ref, loss_ref):
        e = y_ref[...] - t_ref[...]
        dy_ref[...] = e / D

        @pl.when(pl.program_id(0) == 0)
        def _():
            loss_ref[...] = jnp.zeros_like(loss_ref)

        loss_ref[...] += 0.5 * jnp.sum(jnp.mean(e * e, axis=-1, keepdims=True), axis=0, keepdims=True)

    row = pl.BlockSpec((tm, D), lambda i: (i, 0))
    return pl.pallas_call(
        body, name="loss", grid=(T // tm,), in_specs=[row, row],
        out_specs=[row, pl.BlockSpec((8, LANES), lambda i: (0, 0))],
        out_shape=[jax.ShapeDtypeStruct((T, D), F32), jax.ShapeDtypeStruct((8, LANES), F32)],
        compiler_params=_params(1))(y, tgt)


def _qkv_layout(D):
    n_a = D // (2 * LANES)
    n_b = D // (2 * LANES)
    n_kv = max(1, (2 * n_b) // GROUP_B) * HEAD_DIM // LANES
    return n_a, n_b, n_kv


def _dup_half(xv, half):
    rolled = pltpu.roll(xv, HEAD_DIM, 1)
    lo = _lo_lanes(xv.shape)
    return jnp.where(lo, xv, rolled) if half == 0 else jnp.where(lo, rolled, xv)


def _qkv_proj(x, g_mix, w, gains):
    T, D = x.shape
    W = w.shape[1]
    n_a, n_b, n_kv = _qkv_layout(D)
    tm = _pick_tm(T, 256)
    o_qb = 3 * n_a

    def norm(xv, gv, scale):
        ms = _seg_sum(xv * xv) * (1.0 / HEAD_DIM)
        return xv * lax.rsqrt(ms + EPS) * gv * scale

    def body(x_ref, gm_ref, w_ref, g_ref, h_ref, raw_ref, a_ref, b_ref):
        hv = _rms_tile(x_ref[...], gm_ref[...])
        h_ref[...] = hv
        raw = _dot(hv, w_ref[...])
        raw_ref[...] = raw

        def blk(cb):
            return raw[:, cb * LANES:(cb + 1) * LANES]

        def gn(cb):
            return g_ref[:, cb * LANES:(cb + 1) * LANES]

        for cb in range(n_a):
            a_ref[:, cb * LANES:(cb + 1) * LANES] = norm(blk(cb), gn(cb), Q_SCALE)
            cbk = n_a + cb
            a_ref[:, cbk * LANES:(cbk + 1) * LANES] = norm(blk(cbk), gn(cbk), 1.0)
            cbv = 2 * n_a + cb
            a_ref[:, cbv * LANES:(cbv + 1) * LANES] = blk(cbv)
        for cb in range(n_b):
            src = o_qb + cb
            b_ref[:, cb * LANES:(cb + 1) * LANES] = norm(blk(src), gn(src), Q_SCALE)
        for e in range(n_b):
            kvh = (2 * e) // GROUP_B
            ck = o_qb + n_b + kvh // 2
            cv = ck + n_kv
            kn = norm(blk(ck), gn(ck), 1.0)
            b_ref[:, (n_b + e) * LANES:(n_b + e + 1) * LANES] = _dup_half(kn, kvh % 2)
            b_ref[:, (2 * n_b + e) * LANES:(2 * n_b + e + 1) * LANES] = _dup_half(blk(cv), kvh % 2)

    wa, wb = 3 * n_a * LANES, 3 * n_b * LANES

    def rows(width):
        return pl.BlockSpec((tm, width), lambda i: (i, 0))

    return pl.pallas_call(
        body, name="qkv_proj", grid=(T // tm,),
        in_specs=[rows(D), pl.BlockSpec((1, D), lambda i: (0, 0)), pl.BlockSpec((D, W), lambda i: (0, 0)),
                  pl.BlockSpec((1, W), lambda i: (0, 0))],
        out_specs=[rows(D), rows(W), rows(wa), rows(wb)],
        out_shape=[jax.ShapeDtypeStruct((T, D), BF16), jax.ShapeDtypeStruct((T, W), F32),
                   jax.ShapeDtypeStruct((T, wa), F32), jax.ShapeDtypeStruct((T, wb), F32)],
        compiler_params=_params(1))(x, g_mix, w, gains)


def _qkv_bwd(raw, gains, d_a, d_b, w, h, x, g_mix, dx_in, after):
    T, W = raw.shape
    D = x.shape[1]
    n_a, n_b, n_kv = _qkv_layout(D)
    tm = _pick_tm(T, 256)
    nt = T // tm
    o_qb = 3 * n_a

    def body(raw_ref, g_ref, daq, dak, dav, dbq, dbk, dbv, w_ref, h_ref, x_ref, gm_ref, dxi_ref, after_ref,
             dx_ref, dgm_ref, dg_ref, dw_ref, o_ref, acc):
        i = pl.program_id(0)

        @pl.when(i == 0)
        def _():
            dg_ref[...] = jnp.zeros_like(dg_ref)
            dgm_ref[...] = jnp.zeros_like(dgm_ref)
            acc[...] = jnp.zeros_like(acc)

        def cols(ref, cb):
            return ref[:, cb * LANES:(cb + 1) * LANES]

        def norm_bwd(cb, dy, scale):
            xv = cols(raw_ref, cb)
            gv = cols(g_ref, cb)
            r = lax.rsqrt(_seg_sum(xv * xv) * (1.0 / HEAD_DIM) + EPS)
            xh = xv * r
            dys = dy * scale
            dyg = dys * gv
            dxv = r * (dyg - xh * (_seg_sum(dyg * xh) * (1.0 / HEAD_DIM)))
            o_ref[:, cb * LANES:(cb + 1) * LANES] = dxv.astype(BF16)
            dg_ref[:, cb * LANES:(cb + 1) * LANES] += jnp.sum(dys * xh, axis=0, keepdims=True)

        def fold(ref, kv_blk):
            halves = []
            for half in range(2):
                kvh = 2 * kv_blk + half
                blocks = [e for e in range(n_b) if (2 * e) // GROUP_B == kvh]
                s = cols(ref, blocks[0])
                for e in blocks[1:]:
                    s = s + cols(ref, e)
                halves.append(s + pltpu.roll(s, HEAD_DIM, 1))
            return jnp.where(_lo_lanes(halves[0].shape), halves[0], halves[1])

        for cb in range(n_a):
            norm_bwd(cb, cols(daq, cb), Q_SCALE)
            norm_bwd(n_a + cb, cols(dak, cb), 1.0)
            cbv = 2 * n_a + cb
            o_ref[:, cbv * LANES:(cbv + 1) * LANES] = cols(dav, cb).astype(BF16)
        for cb in range(n_b):
            norm_bwd(o_qb + cb, cols(dbq, cb), Q_SCALE)
        for kb in range(n_kv):
            ck = o_qb + n_b + kb
            cv = ck + n_kv
            norm_bwd(ck, fold(dbk, kb), 1.0)
            o_ref[:, cv * LANES:(cv + 1) * LANES] = fold(dbv, kb).astype(BF16)

        draw = o_ref[...]
        dxn, dgm = _rms_bwd_tile(x_ref[...], gm_ref[...], _dot_nt(draw, w_ref[...]))
        dx_ref[...] = dxi_ref[...] + dxn
        dgm_ref[...] += dgm
        acc[...] += _dot_tn(h_ref[...], draw)

        @pl.when(i == nt - 1)
        def _():
            dw_ref[...] = acc[...].astype(BF16)

    hw_a, hw_b = n_a * LANES, n_b * LANES

    def rows(width):
        return pl.BlockSpec((tm, width), lambda i: (i, 0))

    def fixed(r, c):
        return pl.BlockSpec((r, c), lambda i: (0, 0))

    return pl.pallas_call(
        body, name="qkv_bwd", grid=(nt,),
        in_specs=[rows(W), fixed(1, W)] + [rows(hw_a)] * 3 + [rows(hw_b)] * 3
        + [fixed(D, W), rows(D), rows(D), fixed(1, D), rows(D), ANY_SPEC],
        out_specs=[rows(D), fixed(1, D), fixed(1, W), fixed(D, W)],
        out_shape=[jax.ShapeDtypeStruct((T, D), F32), jax.ShapeDtypeStruct((1, D), F32),
                   jax.ShapeDtypeStruct((1, W), F32), jax.ShapeDtypeStruct((D, W), BF16)],
        scratch_shapes=[pltpu.VMEM((tm, W), BF16), pltpu.VMEM((D, W), F32)],
        compiler_params=_params(1))(raw, gains, *d_a, *d_b, w, h, x, g_mix, dx_in, after)


def _t5_bucket_np(rel):
    half = N_BUCKETS // 2
    max_exact = half // 2
    ret = np.where(rel > 0, half, 0)
    n = np.abs(rel)
    nf = np.maximum(n, 1).astype(np.float32)
    large = max_exact + (np.log(nf / np.float32(max_exact)) / np.float32(math.log(MAX_DISTANCE / max_exact))
                         * np.float32(half - max_exact)).astype(np.int32)
    large = np.minimum(large, half - 1)
    return ret + np.where(n < max_exact, n, large)


def _window_pad(radius):
    assert radius <= QBLOCK
    return HEAD_DIM if radius <= HEAD_DIM else QBLOCK


def _bucket_maps(configs):
    pad = _window_pad(configs[0][0])
    q = np.arange(QBLOCK)[:, None]
    kk = np.arange(QBLOCK + 2 * pad)[None, :]
    rel = kk - pad - q
    maps = [np.where(np.abs(rel) <= radius, _t5_bucket_np(rel * dil), -1) for radius, dil in configs]
    return np.stack(maps).astype(np.int32)


def _bias_build(rel_bias, bmaps, col0):
    n_sets, _, W = bmaps.shape
    n_heads = rel_bias.shape[1] // 2

    def body(rb_ref, bm_ref, o_ref):
        h = pl.program_id(1)
        bm = bm_ref[...]

        def step(n, acc):
            return jnp.where(bm == n, rb_ref[n, col0 + h], acc)

        o_ref[...] = lax.fori_loop(0, N_BUCKETS, step, jnp.where(bm < 0, NEG, 0.0).astype(F32))

    return pl.pallas_call(
        body, name="bias_build", grid=(n_sets, n_heads),
        in_specs=[pl.BlockSpec(memory_space=pltpu.SMEM), pl.BlockSpec((None, QBLOCK, W), lambda s, h: (s, 0, 0))],
        out_specs=pl.BlockSpec((None, None, QBLOCK, W), lambda s, h: (s, h, 0, 0)),
        out_shape=jax.ShapeDtypeStruct((n_sets, n_heads, QBLOCK, W), F32),
        compiler_params=_params(2))(rel_bias, bmaps)


def _bias_grad(dtiles, bmaps, col0):
    n_sets, _, W = bmaps.shape
    n_heads = dtiles[0].shape[1]
    n_l = len(dtiles)

    def body(*refs):
        bm_ref, o_ref = refs[0], refs[1 + n_l]
        s, h = pl.program_id(0), pl.program_id(1)

        @pl.when((s == 0) & (h == 0))
        def _():
            o_ref[...] = jnp.zeros_like(o_ref)

        d = refs[1][...]
        for r in refs[2:1 + n_l]:
            d = d + r[...]
        acc8 = d[0:8, :]
        for a in range(1, QBLOCK // 8):
            acc8 = acc8 + pltpu.roll(d[8 * a:8 * a + 8, :], W - 8 * a, 1)
        per_offset = acc8[0:1, :]
        for b in range(1, 8):
            per_offset = per_offset + pltpu.roll(acc8[b:b + 1, :], W - b, 1)
        bucket = lax.broadcasted_iota(jnp.int32, (N_BUCKETS, W), 0)
        hit = bucket == bm_ref[0:1, :]
        per_bucket = jnp.sum(jnp.where(hit, per_offset, 0.0), axis=1, keepdims=True)
        lanes = lax.broadcasted_iota(jnp.int32, o_ref.shape, 1)
        o_ref[...] += jnp.where(lanes == col0 + h, per_bucket, 0.0)

    tile = pl.BlockSpec((None, None, QBLOCK, W), lambda s, h: (s, h, 0, 0))
    return pl.pallas_call(
        body, name="bias_grad", grid=(n_sets, n_heads),
        in_specs=[pl.BlockSpec((None, QBLOCK, W), lambda s, h: (s, 0, 0))] + [tile] * n_l,
        out_specs=pl.BlockSpec((N_BUCKETS, LANES), lambda s, h: (0, 0)),
        out_shape=jax.ShapeDtypeStruct((N_BUCKETS, LANES), F32), compiler_params=_params(2))(bmaps, *dtiles)


def _rows(l_start, n, d, r):
    if d == 1:
        return pl.ds(pl.multiple_of(l_start, 8), n)
    return pl.ds(l_start * d + r, n, stride=d)


def _stack_heads(xv, lo):
    z = jnp.zeros_like(xv)
    return jnp.concatenate([jnp.where(lo, xv, z), jnp.where(lo, z, xv)], axis=0)


def _unstack_heads(xv, lo):
    return jnp.where(lo, xv[:QBLOCK], xv[QBLOCK:])


def _per_head_rows(v0, v1):
    if jnp.ndim(v0) == 0:
        return jnp.where(lax.broadcasted_iota(jnp.int32, (2 * QBLOCK, 1), 0) < QBLOCK, v0, v1)
    return jnp.concatenate([v0, v1], axis=0)


def _block_geometry(b, nb_sub, pad):
    r, lb = b // nb_sub, b % nb_sub
    l0 = lb * QBLOCK
    lp = jnp.maximum(l0 - pad, 0)
    ln = jnp.minimum(l0 + QBLOCK, nb_sub * QBLOCK - pad)
    return r, l0, lp, ln, (lb == 0).astype(jnp.int32) + 2 * (lb == nb_sub - 1).astype(jnp.int32)


def _edge_variants(bias, pad):
    n_br, _, _, W = bias.shape
    col = np.arange(W)
    left, right = col < pad, col >= pad + QBLOCK
    masked = jnp.asarray(np.stack([np.zeros(W, bool), left, right, left | right]))
    return jnp.where(masked[None, :, None, :], NEG, bias.reshape(n_br, 1, -1, W))


def _window(ref, l0, lp, ln, pad, d, r):
    return jnp.concatenate([ref[_rows(lp, pad, d, r), :], ref[_rows(l0, QBLOCK, d, r), :],
                            ref[_rows(ln, pad, d, r), :]], axis=0)


def _attn_fwd(qkv, bias, sink, dils, pad):
    T = qkv.shape[0]
    hw = qkv.shape[1] // 3
    ng = hw // LANES
    n_br = len(dils)
    n_blocks = T // QBLOCK
    W = QBLOCK + 2 * pad
    chunk = 256

    def body(sink_ref, q_ref, k_ref, v_ref, bias_ref, o_ref, lse_ref, *scratch):
        g = pl.program_id(0)
        lo = _lo_lanes((QBLOCK, LANES))
        snk = _per_head_rows(sink_ref[2 * g], sink_ref[2 * g + 1])
        for c, d in enumerate(dils):
            nb_sub = n_blocks // d
            o_dst = scratch[0].at[c] if n_br > 1 else o_ref
            l_dst = scratch[1].at[c] if n_br > 1 else lse_ref

            def block(b, carry, c=c, d=d, nb_sub=nb_sub, o_dst=o_dst, l_dst=l_dst):
                r, l0, lp, ln, edge = _block_geometry(b, nb_sub, pad)
                q = _stack_heads(q_ref[_rows(l0, QBLOCK, d, r), :].astype(BF16), lo)
                k = _window(k_ref, l0, lp, ln, pad, d, r).astype(BF16)
                v = _window(v_ref, l0, lp, ln, pad, d, r).astype(BF16)
                s = _dot_nt(q, k) + bias_ref[c, edge]
                m = jnp.maximum(jnp.max(s, axis=1, keepdims=True), snk)
                p = jnp.exp(s - m)
                den = jnp.sum(p, axis=1, keepdims=True) + jnp.exp(snk - m)
                o_dst[_rows(l0, QBLOCK, d, r), :] = _unstack_heads(_dot(p.astype(BF16), v) / den, lo)
                l_dst[_rows(l0, QBLOCK, d, r), :] = _unstack_heads(
                    jnp.broadcast_to(m + jnp.log(den), (2 * QBLOCK, LANES)), lo)
                return carry

            lax.fori_loop(0, n_blocks, block, 0, unroll=FWD_UNROLL)

        if n_br > 1:
            def merge(i, carry):
                rs = pl.ds(pl.multiple_of(i * chunk, chunk), chunk)
                ls = [scratch[1][c, rs, :] for c in range(n_br)]
                m = ls[0]
                for t in ls[1:]:
                    m = jnp.maximum(m, t)
                ws = [jnp.exp(t - m) for t in ls]
                z = ws[0]
                acc = ws[0] * scratch[0][0, rs, :]
                for c in range(1, n_br):
                    z = z + ws[c]
                    acc = acc + ws[c] * scratch[0][c, rs, :]
                o_ref[rs, :] = acc / z
                lse_ref[rs, :] = m + jnp.log(z)
                return carry

            lax.fori_loop(0, T // chunk, merge, 0)

    def col(base):
        return pl.BlockSpec((T, LANES), lambda g: (0, base + g))

    out = jax.ShapeDtypeStruct((T, hw), F32)
    scratch = [pltpu.VMEM((n_br, T, LANES), F32)] * 2 if n_br > 1 else []
    return pl.pallas_call(
        body, name="attn_fwd", grid=(ng,),
        in_specs=[pl.BlockSpec(memory_space=pltpu.SMEM), col(0), col(ng), col(2 * ng),
                  pl.BlockSpec((n_br, 4, 2 * QBLOCK, W), lambda g: (0, 0, g, 0))],
        out_specs=[col(0), col(0)], out_shape=[out, out], scratch_shapes=scratch,
        compiler_params=_params(1))(sink, qkv, qkv, qkv, bias)


def _attn_bwd(qkv, bias, sink, dils, pad, do, lse, dd, col_base, after):
    T = qkv.shape[0]
    hw = qkv.shape[1] // 3
    ng = hw // LANES
    n_br = len(dils)
    n_blocks = T // QBLOCK
    W = QBLOCK + 2 * pad

    def body(sink_ref, q_ref, k_ref, v_ref, bias_ref, do_ref, lse_ref, dd_ref, after_ref,
             dq_ref, dk_ref, dv_ref, dt_ref, ds_ref):
        g = pl.program_id(0)
        dq_ref[...] = jnp.zeros_like(dq_ref)
        dk_ref[...] = jnp.zeros_like(dk_ref)
        dv_ref[...] = jnp.zeros_like(dv_ref)
        dt_ref[...] = jnp.zeros_like(dt_ref)
        ds_ref[...] = jnp.zeros_like(ds_ref)
        lo = _lo_lanes((QBLOCK, LANES))
        snk = jnp.where(lo, sink_ref[2 * g], sink_ref[2 * g + 1])
        for c, d in enumerate(dils):
            nb_sub = n_blocks // d

            def block(b, carry, c=c, d=d, nb_sub=nb_sub):
                r, l0, lp, ln, edge = _block_geometry(b, nb_sub, pad)
                rows_q = _rows(l0, QBLOCK, d, r)
                q = _stack_heads(q_ref[rows_q, :].astype(BF16), lo)
                k = _window(k_ref, l0, lp, ln, pad, d, r).astype(BF16)
                v = _window(v_ref, l0, lp, ln, pad, d, r).astype(BF16)
                dob = _stack_heads(do_ref[rows_q, :].astype(BF16), lo)
                lse_b = lse_ref[rows_q, :]
                dd_b = dd_ref[rows_q, :]
                s = _dot_nt(q, k) + bias_ref[c, edge]
                p = jnp.exp(s - _per_head_rows(lse_b[:, 0:1], lse_b[:, HEAD_DIM:HEAD_DIM + 1]))
                ds = p * (_dot_nt(dob, v) - _per_head_rows(dd_b[:, 0:1], dd_b[:, HEAD_DIM:HEAD_DIM + 1]))
                dsb = ds.astype(BF16)
                dkw = _dot_tn(dsb, q)
                dvw = _dot_tn(p.astype(BF16), dob)
                dt_ref[c] += ds
                dq_ref[rows_q, :] += _unstack_heads(_dot(dsb, k), lo)
                ds_ref[0:1, :] += jnp.sum(-jnp.exp(snk - lse_b) * dd_b, axis=0, keepdims=True)
                for part, (start, n) in zip((0, pad, pad + QBLOCK), ((lp, pad), (l0, QBLOCK), (ln, pad))):
                    dk_ref[_rows(start, n, d, r), :] += dkw[part:part + n]
                    dv_ref[_rows(start, n, d, r), :] += dvw[part:part + n]
                return carry

            lax.fori_loop(0, n_blocks, block, 0, unroll=BWD_UNROLL)

    def col(base):
        return pl.BlockSpec((T, LANES), lambda g: (0, base + g))

    tile = pl.BlockSpec((n_br, 2 * QBLOCK, W), lambda g: (0, g, 0))
    full = jax.ShapeDtypeStruct((T, hw), F32)
    dq, dk, dv, dt, dsink = pl.pallas_call(
        body, name="attn_bwd", grid=(ng,),
        in_specs=[pl.BlockSpec(memory_space=pltpu.SMEM), col(0), col(ng), col(2 * ng),
                  pl.BlockSpec((n_br, 4, 2 * QBLOCK, W), lambda g: (0, 0, g, 0)),
                  col(col_base), col(0), col(col_base), ANY_SPEC],
        out_specs=[col(0), col(0), col(0), tile, pl.BlockSpec((None, 8, LANES), lambda g: (g, 0, 0))],
        out_shape=[full, full, full, jax.ShapeDtypeStruct((n_br, 2 * ng * QBLOCK, W), F32),
                   jax.ShapeDtypeStruct((ng, 8, LANES), F32)],
        compiler_params=_params(1))(sink, qkv, qkv, qkv, bias, do, lse, dd, after)
    return dq, dk, dv, dt.reshape(n_br, 2 * ng, QBLOCK, W), dsink


def _mix_bwd_in(dx, wo, o_list):
    T, D = dx.shape
    widths = [o.shape[1] for o in o_list]
    hw = sum(widths)
    n = len(o_list)
    tm = _pick_tm(T, 256)
    nt = T // tm

    def body(*refs):
        dx_ref, w_ref, o_refs = refs[0], refs[1], refs[2:2 + n]
        do_ref, dd_ref, dw_ref, acc = refs[2 + n:]
        i = pl.program_id(0)

        @pl.when(i == 0)
        def _():
            acc[...] = jnp.zeros_like(acc)

        dxb = dx_ref[...].astype(BF16)
        dov = _dot_nt(dxb, w_ref[...])
        do_ref[...] = dov
        off = 0
        for o_ref, k in zip(o_refs, widths):
            ov = o_ref[...]
            prod = dov[:, off:off + k] * ov
            for cb in range(k // LANES):
                dd_ref[:, off + cb * LANES:off + (cb + 1) * LANES] = _seg_sum(prod[:, cb * LANES:(cb + 1) * LANES])
            acc[off:off + k, :] += _dot_tn(ov.astype(BF16), dxb)
            off += k

        @pl.when(i == nt - 1)
        def _():
            dw_ref[...] = acc[...].astype(BF16)

    row = pl.BlockSpec((tm, hw), lambda i: (i, 0))
    out = jax.ShapeDtypeStruct((T, hw), F32)
    return pl.pallas_call(
        body, name="mix_bwd_in", grid=(nt,),
        in_specs=[pl.BlockSpec((tm, D), lambda i: (i, 0)), pl.BlockSpec((hw, D), lambda i: (0, 0))]
        + [pl.BlockSpec((tm, k), lambda i: (i, 0)) for k in widths],
        out_specs=[row, row, pl.BlockSpec((hw, D), lambda i: (0, 0))],
        out_shape=[out, out, jax.ShapeDtypeStruct((hw, D), BF16)],
        scratch_shapes=[pltpu.VMEM((hw, D), F32)], compiler_params=_params(1))(dx, wo, *o_list)


def _mesh_pos():
    return lax.axis_index("x"), lax.axis_index("y"), lax.axis_index("c")


def _my_chip():
    return 2 * lax.axis_index("x") + lax.axis_index("y")


def _other_chips(x, y):
    return [(1 - x, y), (x, 1 - y), (1 - x, 1 - y)]


def _half_rows(rows, cc):
    hr = rows // 2
    return pl.ds(pl.multiple_of(cc * hr, 16), hr)


def _cast_into_slot(w, l):
    _, R, C = w.shape
    tm = _pick_tm(R, 512)

    def body(w_ref, o_ref):
        o_ref[...] = w_ref[...].astype(BF16)

    return pl.pallas_call(
        body, name="cast_into_slot", grid=(R // tm,),
        in_specs=[pl.BlockSpec((None, tm, C), lambda i: (l, i, 0))],
        out_specs=pl.BlockSpec((None, tm, C), lambda i: (_my_chip(), i, 0)),
        out_shape=jax.ShapeDtypeStruct((N_CHIPS, R, C), BF16), compiler_params=_params(1))(w)


def _split_start(name, arrays, make_copies, n_sem, after):
    n = len(arrays)
    n_in = n + (0 if after is None else 1)

    def body(*refs):
        send_s, recv_s = refs[n_in], refs[n_in + 1]
        token = refs[n_in + 2 + n]
        for send, _ in make_copies(refs[:n], send_s, recv_s):
            send.start()
        token[...] = jnp.zeros_like(token)

    res = pl.pallas_call(
        body, name=name,
        out_shape=(pltpu.SemaphoreType.DMA((n_sem,)), pltpu.SemaphoreType.DMA((n_sem,)),
                   *[pltpu.HBM(a.shape, a.dtype) for a in arrays], jax.ShapeDtypeStruct((8, LANES), F32)),
        in_specs=[HBM_SPEC] * n + [ANY_SPEC] * (n_in - n),
        out_specs=(SEM_SPEC, SEM_SPEC, *([HBM_SPEC] * n), pl.BlockSpec(memory_space=pltpu.VMEM)),
        input_output_aliases={i: 2 + i for i in range(n)},
        compiler_params=pltpu.CompilerParams(has_side_effects=pltpu.SideEffectType.DATAFLOW_SIDE_EFFECTING),
    )(*[pltpu.with_memory_space_constraint(a, pltpu.HBM) for a in arrays], *([] if after is None else [after]))
    return res[0], res[1], list(res[2:2 + n]), res[2 + n]


def _split_wait(name, send_s, recv_s, arrays, make_copies, after):
    n = len(arrays)

    def body(*refs):
        for send, landed in make_copies(refs[:n], refs[n], refs[n + 1]):
            send.wait_send()
            landed.wait_recv()

    return list(pl.pallas_call(
        body, name=name, out_shape=[pltpu.HBM(a.shape, a.dtype) for a in arrays],
        in_specs=[HBM_SPEC] * n + [SEM_SPEC, SEM_SPEC, ANY_SPEC], out_specs=[HBM_SPEC] * n,
        input_output_aliases={i: i for i in range(n)},
        compiler_params=pltpu.CompilerParams(has_side_effects=pltpu.SideEffectType.DATAFLOW_SIDE_EFFECTING),
    )(*arrays, send_s, recv_s, after))


def _gather_copies(shapes):
    n = len(shapes)

    def make(refs, send_s, recv_s):
        x, y, c = _mesh_pos()
        my = 2 * x + y
        copies = []
        for w in range(n):
            for k, (px, py) in enumerate(_other_chips(x, y)):
                def part(slot, w=w):
                    return refs[w].at[slot, _half_rows(shapes[w][1], c), :]
                sems = dict(send_sem=send_s.at[k * n + w], recv_sem=recv_s.at[k * n + w],
                            device_id=(px, py, c), device_id_type=MESH)
                copies.append((pltpu.make_async_remote_copy(src_ref=part(my), dst_ref=part(my), **sems),
                               pltpu.make_async_remote_copy(src_ref=part(2 * px + py), dst_ref=part(2 * px + py), **sems)))
        return copies

    return make


def _forward_copies(shapes):
    n = len(shapes)

    def make(refs, send_s, recv_s):
        x, y, c = _mesh_pos()
        copies = []
        for w in range(n):
            for k, (px, py) in enumerate(_other_chips(x, y)):
                def part(cc, w=w, slot=2 * px + py):
                    return refs[w].at[slot, _half_rows(shapes[w][1], cc), :]
                sems = dict(send_sem=send_s.at[k * n + w], recv_sem=recv_s.at[k * n + w],
                            device_id=(x, y, 1 - c), device_id_type=MESH)
                copies.append((pltpu.make_async_remote_copy(src_ref=part(c), dst_ref=part(c), **sems),
                               pltpu.make_async_remote_copy(src_ref=part(1 - c), dst_ref=part(1 - c), **sems)))
        return copies

    return make


def _pair_forward(bufs):
    n = len(bufs)
    make = _forward_copies([b.shape for b in bufs])

    def body(*refs):
        copies = make(refs[n:2 * n], refs[2 * n], refs[2 * n + 1])
        for send, _ in copies:
            send.start()
        for _, landed in copies:
            landed.wait_recv()
        for send, _ in copies:
            send.wait_send()

    return list(pl.pallas_call(
        body, name="ag_pair_forward", in_specs=[HBM_SPEC] * n, out_specs=[HBM_SPEC] * n,
        out_shape=[jax.ShapeDtypeStruct(b.shape, b.dtype) for b in bufs],
        input_output_aliases={w: w for w in range(n)},
        scratch_shapes=[pltpu.SemaphoreType.DMA((3 * n,)), pltpu.SemaphoreType.DMA((3 * n,))],
    )(*bufs))


def _pair_exchange_copies(shapes):
    n = len(shapes)

    def make(refs, send_s, recv_s):
        x, y, c = _mesh_pos()
        copies = []
        for t in range(n):
            sems = dict(send_sem=send_s.at[t], recv_sem=recv_s.at[t], device_id=(x, y, 1 - c), device_id_type=MESH)
            land = refs[n + t]
            copies.append((pltpu.make_async_remote_copy(
                src_ref=refs[t].at[:, _half_rows(shapes[t][1], 1 - c), :], dst_ref=land, **sems),
                pltpu.make_async_remote_copy(src_ref=land, dst_ref=land, **sems)))
        return copies

    return make


def _pair_share_copies(shapes):
    n = len(shapes)

    def make(refs, send_s, recv_s):
        x, y, c = _mesh_pos()
        copies = []
        for t in range(n):
            def half(cc, t=t):
                return refs[t].at[_half_rows(shapes[t][0], cc), :]
            sems = dict(send_sem=send_s.at[t], recv_sem=recv_s.at[t], device_id=(x, y, 1 - c), device_id_type=MESH)
            copies.append((pltpu.make_async_remote_copy(src_ref=half(c), dst_ref=half(c), **sems),
                           pltpu.make_async_remote_copy(src_ref=half(1 - c), dst_ref=half(1 - c), **sems)))
        return copies

    return make


def _rs_add_pair(grad, recv):
    n_slot, hr, C = recv.shape
    tm = _pick_tm(hr, 192)
    nb = hr // tm

    def body(a_ref, b_ref, o_ref):
        o_ref[...] = (a_ref[...].astype(F32) + b_ref[...].astype(F32)).astype(BF16)

    blk = pl.BlockSpec((n_slot, tm, C), lambda i: (0, i, 0))
    return pl.pallas_call(
        body, name="rs_add_pair", grid=(nb,),
        in_specs=[pl.BlockSpec((n_slot, tm, C), lambda i: (0, lax.axis_index("c") * nb + i, 0)), blk],
        out_specs=blk, out_shape=jax.ShapeDtypeStruct(recv.shape, BF16), compiler_params=_params(1))(grad, recv)


def _scatter_copies(n):
    def make(refs, send_s, recv_s):
        x, y, c = _mesh_pos()
        copies = []
        for t in range(n):
            for k, (px, py) in enumerate(_other_chips(x, y)):
                sems = dict(send_sem=send_s.at[3 * t + k], recv_sem=recv_s.at[3 * t + k],
                            device_id=(px, py, c), device_id_type=MESH)
                land = refs[n + t].at[k]
                copies.append((pltpu.make_async_remote_copy(src_ref=refs[t].at[2 * px + py], dst_ref=land, **sems),
                               pltpu.make_async_remote_copy(src_ref=land, dst_ref=land, **sems)))
        return copies

    return make


def _rs_add_chips(part, recv):
    _, hr, C = part.shape
    tm = _pick_tm(hr, 256)
    nb = hr // tm

    def body(a_ref, r0, r1, r2, o_ref):
        o_ref[...] = ((a_ref[...].astype(F32) + r0[...].astype(F32)) + r1[...].astype(F32)) + r2[...].astype(F32)

    def rel(k):
        return pl.BlockSpec((None, tm, C), lambda i: (k, i, 0))

    return pl.pallas_call(
        body, name="rs_add_chips", grid=(nb,),
        in_specs=[pl.BlockSpec((None, tm, C), lambda i: (_my_chip(), i, 0)), rel(0), rel(1), rel(2)],
        out_specs=pl.BlockSpec((tm, C), lambda i: (lax.axis_index("c") * nb + i, 0)),
        out_shape=jax.ShapeDtypeStruct((2 * hr, C), F32), compiler_params=_params(1))(part, recv, recv, recv)


def _allreduce_small(v, after):
    rows = v.shape[0]

    def body(v_ref, after_ref, o_ref, buf, send_s, recv_s):
        x, y, c = _mesh_pos()
        me = 4 * x + 2 * y + c
        buf[me] = v_ref[...]
        copies = []
        for r in range(1, N_DEV):
            px = 1 - x if r & 4 else x
            py = 1 - y if r & 2 else y
            pc = 1 - c if r & 1 else c
            send = pltpu.make_async_remote_copy(
                src_ref=v_ref, dst_ref=buf.at[me], send_sem=send_s.at[r - 1], recv_sem=recv_s.at[r - 1],
                device_id=(px, py, pc), device_id_type=MESH)
            peer_slot = buf.at[4 * px + 2 * py + pc]
            landed = pltpu.make_async_remote_copy(
                src_ref=peer_slot, dst_ref=peer_slot, send_sem=send_s.at[r - 1], recv_sem=recv_s.at[r - 1],
                device_id=(px, py, pc), device_id_type=MESH)
            copies.append((send, landed))
        for send, _ in copies:
            send.start()
        for _, landed in copies:
            landed.wait_recv()
        for send, _ in copies:
            send.wait_send()
        acc = buf[0]
        for j in range(1, N_DEV):
            acc = acc + buf[j]
        o_ref[...] = acc

    vm = pl.BlockSpec(memory_space=pltpu.VMEM)
    return pl.pallas_call(
        body, name="allreduce_small", in_specs=[vm, ANY_SPEC], out_specs=vm,
        out_shape=jax.ShapeDtypeStruct((rows, LANES), F32),
        scratch_shapes=[pltpu.VMEM((N_DEV, rows, LANES), F32), pltpu.SemaphoreType.DMA((N_DEV - 1,)),
                        pltpu.SemaphoreType.DMA((N_DEV - 1,))],
    )(v, after)


def _adamw_fn(w, g, m, v):
    m2 = ADAM_B1 * m + (1.0 - ADAM_B1) * g
    v2 = ADAM_B2 * v + (1.0 - ADAM_B2) * (g * g)
    m_hat = m2 / (1.0 - ADAM_B1 ** ADAM_STEP)
    v_hat = v2 / (1.0 - ADAM_B2 ** ADAM_STEP)
    delta = -ADAM_LR * (m_hat / (jnp.sqrt(v_hat) + ADAM_EPS) + ADAM_WD * w)
    return g, delta, m2, v2


def _adamw_layer(w, g, m, v, l, prev, after):
    NL, R, C = w.shape
    tm = _pick_tm(R, 256)
    n_prev = 0 if prev is None else 4

    def body(w_ref, g_ref, m_ref, v_ref, after_ref, *rest):
        outs = rest[n_prev:]
        for o_ref, val in zip(outs, _adamw_fn(w_ref[...], g_ref[...], m_ref[...], v_ref[...])):
            o_ref[...] = val

    lay = pl.BlockSpec((None, tm, C), lambda i: (l, i, 0))
    shape = jax.ShapeDtypeStruct((NL, R, C), F32)
    return pl.pallas_call(
        body, name="adamw", grid=(R // tm,),
        in_specs=[lay, pl.BlockSpec((tm, C), lambda i: (i, 0)), lay, lay, ANY_SPEC] + [ANY_SPEC] * n_prev,
        out_specs=[lay] * 4, out_shape=[shape] * 4,
        input_output_aliases={5 + j: j for j in range(n_prev)},
        compiler_params=_params(1))(w, g, m, v, after, *(prev or []))


def _pack_small(parts):
    out = []
    for a in parts:
        flat = a.reshape(-1)
        n = -(-flat.shape[0] // (8 * LANES)) * 8 * LANES
        out.append(jnp.pad(flat, (0, n - flat.shape[0])).reshape(-1, LANES))
    return jnp.concatenate(out, axis=0)


def _unpack_small(packed, like):
    out, r = [], 0
    for a in like:
        size = int(np.prod(a.shape))
        rows = -(-size // (8 * LANES)) * 8
        out.append(packed[r:r + rows].reshape(-1)[:size].reshape(a.shape))
        r += rows
    return out


def _ffn_forward(x, g, win, wout):
    h, silu, dgate, act = _ffn_up(x, g, win)
    if callable(wout):
        wout = wout(act)
    return _mm_res("ffn_down", x, [act], wout, 0.5), (x, h, silu, dgate, act)


def _ffn_backward(dx, saved, g, win, wout, after):
    x, h, silu, dgate, act = saved
    D = x.shape[1]
    wc = win.shape[2]
    dgu, dx_in, dg = _ffn_bwd(dx, wout, silu, dgate, win, x, g, after)
    dwout = _mm_tn("dw_ffn_out", act, wc, dx, D, 2, True, False, 0.5)
    dwin = _mm_tn_pairs("dw_ffn_in", h, dgu, wc, 4)
    return dx_in, dg, dwin, dwout.reshape(N_CHIPS, -1, D)


GROUP_MID = ("w_o", "ffn2_w_in", "ffn2_w_out", "w_ple_gate", "w_ple_proj")
GROUP_LAST = ("w_qkv", "ffn1_w_in", "ffn1_w_out")
GATHER_L0 = (("a", ("ffn1_w_in",)), ("b", ("ffn1_w_out",)), ("c", ("w_qkv", "w_o")),
             ("d", ("ffn2_w_in", "ffn2_w_out", "w_ple_gate", "w_ple_proj")))


def _gather_start(tag, slotted, after):
    return _split_start("ag_start_" + tag, slotted, _gather_copies([a.shape for a in slotted]), 3 * len(slotted), after)


def _gather_finish(tag, started, after):
    send_s, recv_s, arrays, _ = started
    return _pair_forward(_split_wait("ag_wait_" + tag, send_s, recv_s, arrays,
                                     _gather_copies([a.shape for a in arrays]), after))


def _scatter_exchange(tag, grads):
    n = len(grads)
    land = [lax.empty((g_.shape[0], g_.shape[1] // 2, g_.shape[2]), g_.dtype) for g_ in grads]
    return _split_start("rs_px_start_" + tag, list(grads) + land, _pair_exchange_copies([g_.shape for g_ in grads]),
                        n, None)


def _scatter_chips(tag, started, after):
    send_s, recv_s, arrays, _ = started
    n = len(arrays) // 2
    arrays = _split_wait("rs_px_wait_" + tag, send_s, recv_s, arrays,
                         _pair_exchange_copies([a.shape for a in arrays[:n]]), after)
    part = [_rs_add_pair(g_, r_) for g_, r_ in zip(arrays[:n], arrays[n:])]
    land = [lax.empty((3,) + p_.shape[1:], p_.dtype) for p_ in part]
    return _split_start("rs_start_" + tag, part + land, _scatter_copies(n), 3 * n, None)


def _scatter_share(tag, started, after):
    send_s, recv_s, arrays, _ = started
    n = len(arrays) // 2
    arrays = _split_wait("rs_wait_" + tag, send_s, recv_s, arrays, _scatter_copies(n), after)
    halves = [_rs_add_chips(p_, r_) for p_, r_ in zip(arrays[:n], arrays[n:])]
    return _split_start("rs_ps_start_" + tag, halves, _pair_share_copies([h_.shape for h_ in halves]), n, None)


def _scatter_done(tag, started, after):
    send_s, recv_s, arrays, _ = started
    return _split_wait("rs_ps_wait_" + tag, send_s, recv_s, arrays, _pair_share_copies([a.shape for a in arrays]), after)


def kernel(x, p, rel_bias, norm_ffn1, ffn1_w_in, ffn1_w_out, norm_mix, w_qkv, q_norm_a, k_norm_a, q_norm_b, k_norm_b, sink_b, w_o, norm_ffn2, ffn2_w_in, ffn2_w_out, norm_ple, w_ple_gate, w_ple_proj, loss_target, m_rel_bias, m_norm_ffn1, m_ffn1_w_in, m_ffn1_w_out, m_norm_mix, m_w_qkv, m_q_norm_a, m_k_norm_a, m_q_norm_b, m_k_norm_b, m_sink_b, m_w_o, m_norm_ffn2, m_ffn2_w_in, m_ffn2_w_out, m_norm_ple, m_w_ple_gate, m_w_ple_proj, v_rel_bias, v_norm_ffn1, v_ffn1_w_in, v_ffn1_w_out, v_norm_mix, v_w_qkv, v_q_norm_a, v_k_norm_a, v_q_norm_b, v_k_norm_b, v_sink_b, v_w_o, v_norm_ffn2, v_ffn2_w_in, v_ffn2_w_out, v_norm_ple, v_w_ple_gate, v_w_ple_proj):
    given = dict(locals())
    T, D = x.shape[1], x.shape[2]
    NL = norm_ffn1.shape[0]
    x0 = x.reshape(T, D)
    tgt = loss_target.reshape(T, D)
    n_a, n_b, n_kv = _qkv_layout(D)

    assert NL == 2
    slot = [{name: _cast_into_slot(given[name], l) for name in BIG} for l in range(NL)]
    ag, token = {}, None
    for tag, names in GATHER_L0:
        ag[tag] = _gather_start(tag, [slot[0][n] for n in names], token)
        token = ag[tag][3]
    ag_1 = _gather_start("1", [slot[1][n] for n in BIG], token)

    def arrived(tag, after):
        return dict(zip(dict(GATHER_L0)[tag], _gather_finish(tag, ag[tag], after)))

    def by_rows(a):
        return a.reshape(-1, a.shape[-1])

    def by_cols(a):
        return a.transpose(1, 0, 2).reshape(a.shape[1], -1)

    QW = N_CHIPS * w_qkv.shape[2]

    dils = tuple(d for _, d in DILATED_CONFIGS)
    cfg_a = [(w // (2 * d), d) for w, d in DILATED_CONFIGS]
    pad_a, pad_b = _window_pad(cfg_a[0][0]), _window_pad(SWA_RADIUS)
    bmaps_a, bmaps_b = jnp.asarray(_bucket_maps(cfg_a)), jnp.asarray(_bucket_maps([(SWA_RADIUS, 1)]))
    n_heads = rel_bias.shape[1] // 2
    bias_a = _edge_variants(_bias_build(rel_bias, bmaps_a, 0), pad_a)
    bias_b = _edge_variants(_bias_build(rel_bias, bmaps_b, n_heads), pad_b)
    no_sink = jnp.full((n_heads,), NEG, F32)

    def gains_row(l):
        ones = jnp.ones((n_a * LANES,), F32)
        return jnp.concatenate([
            jnp.tile(q_norm_a[l], 2 * n_a), jnp.tile(k_norm_a[l], 2 * n_a), ones,
            jnp.tile(q_norm_b[l], 2 * n_b), jnp.tile(k_norm_b[l], 2 * n_kv), jnp.ones((n_kv * LANES,), F32)]).reshape(1, QW)

    saved, weights = [], []
    xc = x0
    pf_1 = None
    for l in range(NL):
        s, w = {}, {}
        if l == 0:
            w.update(arrived("a", ag_1[3]))

            def ffn1_w_out(act, w=w):
                w.update(arrived("b", act))
                w["ffn1_w_out"] = by_rows(w["ffn1_w_out"])
                return w["ffn1_w_out"]
        else:
            shapes = [a.shape for a in pf_1[2]]
            w.update(zip(BIG, _split_wait("ag_pf_wait_1", pf_1[0], pf_1[1], pf_1[2], _forward_copies(shapes), xc)))
            ffn1_w_out = w["ffn1_w_out"] = by_rows(w["ffn1_w_out"])
        xc, s["ffn1"] = _ffn_forward(xc, norm_ffn1[l:l + 1], w["ffn1_w_in"], ffn1_w_out)
        s["x1"] = xc
        if l == 0:
            w.update(arrived("c", xc))
        w["w_qkv"] = by_cols(w["w_qkv"])
        w["w_o"] = by_rows(w["w_o"])
        s["h2"], s["raw"], s["qkv_a"], s["qkv_b"] = _qkv_proj(xc, norm_mix[l:l + 1], w["w_qkv"], gains_row(l))
        s["o_a"], s["lse_a"] = _attn_fwd(s["qkv_a"], bias_a, no_sink, dils, pad_a)
        s["o_b"], s["lse_b"] = _attn_fwd(s["qkv_b"], bias_b, sink_b[l], (1,), pad_b)
        xc = _mm_res("attn_out", xc, [s["o_a"], s["o_b"]], w["w_o"], 1.0)
        if l == 0:
            w.update(arrived("d", xc))
        w["w_ple_proj"] = by_cols(w["w_ple_proj"])
        for name in ("ffn2_w_out", "w_ple_gate"):
            w[name] = by_rows(w[name])
        xc, s["ffn2"] = _ffn_forward(xc, norm_ffn2[l:l + 1], w["ffn2_w_in"], w["ffn2_w_out"])
        s["x3"] = xc
        if l == 0:
            landed = _split_wait("ag_wait_1", ag_1[0], ag_1[1], ag_1[2], _gather_copies([a.shape for a in ag_1[2]]), xc)
            pf_1 = _split_start("ag_pf_start_1", landed, _forward_copies([a.shape for a in landed]), 3 * len(landed),
                                None)
        s["p"] = p[l].reshape(T, -1)
        s["hn"], xc, s["gate"], s["pp"] = _ple_fwd(xc, norm_ple[l:l + 1], s["p"], w["w_ple_gate"], w["w_ple_proj"],
                                                   pf_1[3] if l == 0 else xc)
        saved.append(s)
        weights.append(w)

    dx, loss_blk = _loss_fwd_bwd(xc, tgt)
    loss = lax.psum(loss_blk[0, 0], ("x", "y", "c"))

    gs = {name: [None] * NL for name in SMALL if name != "rel_bias"}
    dt_a, dt_b = [], []

    def layer_backward(l, dx, hooks):
        def at(point, ready, *more):
            return hooks[point](ready, *more) if point in hooks else ready

        s, w, gw = saved[l], weights[l], {}
        dx, gs["norm_ple"][l], dwg, dwp = _ple_bwd(dx, s["gate"], s["pp"], s["hn"], s["p"], s["x3"],
                                                   norm_ple[l:l + 1], w["w_ple_gate"], at("start", dx))
        gw["w_ple_gate"] = dwg.reshape(N_CHIPS, -1, D)
        gw["w_ple_proj"] = dwp.reshape(dwp.shape[0], N_CHIPS, -1).transpose(1, 0, 2)
        dx, gs["norm_ffn2"][l], gw["ffn2_w_in"], gw["ffn2_w_out"] = _ffn_backward(
            dx, s["ffn2"], norm_ffn2[l:l + 1], w["ffn2_w_in"], w["ffn2_w_out"], at("after_ple", dx))
        do, dd, dwo = _mix_bwd_in(dx, w["w_o"], [s["o_a"], s["o_b"]])
        hwa = s["o_a"].shape[1]
        gw["w_o"] = dwo.reshape(N_CHIPS, -1, D)
        dqa, dka, dva, dt, _ = _attn_bwd(s["qkv_a"], bias_a, no_sink, dils, pad_a, do, s["lse_a"], dd, 0, do)
        dt_a.append(dt)
        dqb, dkb, dvb, dt, dsink = _attn_bwd(s["qkv_b"], bias_b, sink_b[l], (1,), pad_b, do, s["lse_b"], dd,
                                             hwa // LANES, at("after_attn_a", dqa))
        dt_b.append(dt)
        gs["sink_b"][l] = dsink[:, 0, ::HEAD_DIM].reshape(-1)
        dx, gs["norm_mix"][l], dgains, dwqkv = _qkv_bwd(
            s["raw"], gains_row(l), (dqa, dka, dva), (dqb, dkb, dvb), w["w_qkv"], s["h2"], s["x1"],
            norm_mix[l:l + 1], dx, at("before_qkv_bwd", dqb, [gw[n] for n in GROUP_MID]))
        dgv = dgains.reshape(-1, HEAD_DIM)
        gs["q_norm_a"][l] = dgv[:2 * n_a].sum(0)
        gs["k_norm_a"][l] = dgv[2 * n_a:4 * n_a].sum(0)
        gs["q_norm_b"][l] = dgv[6 * n_a:6 * n_a + 2 * n_b].sum(0)
        gs["k_norm_b"][l] = dgv[6 * n_a + 2 * n_b:6 * n_a + 2 * n_b + 2 * n_kv].sum(0)
        gw["w_qkv"] = dwqkv.reshape(D, N_CHIPS, -1).transpose(1, 0, 2)
        dx, gs["norm_ffn1"][l], gw["ffn1_w_in"], gw["ffn1_w_out"] = _ffn_backward(
            dx, s["ffn1"], norm_ffn1[l:l + 1], w["ffn1_w_in"], w["ffn1_w_out"], at("before_ffn1", dx))
        return dx, gw

    out = {}

    def adamw_group(names, l, grads, after):
        for name, g_ in zip(names, grads):
            out[name] = _adamw_layer(given[name], g_, given["m_" + name], given["v_" + name], l, out.get(name), after)
            after = out[name][0]
        return after

    dx, gw1 = layer_backward(NL - 1, dx, {})
    px_1 = _scatter_exchange("1", [gw1[n] for n in BIG])
    rs = {}

    def chips_1(ready):
        rs["chips_1"] = _scatter_chips("1", px_1, ready)
        return rs["chips_1"][3]

    def share_1(ready):
        rs["share_1"] = _scatter_share("1", rs["chips_1"], ready)
        return rs["share_1"][3]

    def exchange_0a(ready, grads):
        rs["px_0a"] = _scatter_exchange("0a", grads)
        return rs["px_0a"][3]

    def chips_0a(ready):
        rs["chips_0a"] = _scatter_chips("0a", rs["px_0a"], ready)
        return rs["chips_0a"][3]

    dx, gw0 = layer_backward(0, dx, {"start": lambda ready: px_1[3], "after_ple": chips_1, "after_attn_a": share_1,
                                     "before_qkv_bwd": exchange_0a, "before_ffn1": chips_0a})
    grad_x = dx.reshape(x.shape)

    px_0b = _scatter_exchange("0b", [gw0[n] for n in GROUP_LAST])
    d_rel_bias = (_bias_grad(dt_a, bmaps_a, 0) + _bias_grad(dt_b, bmaps_b, n_heads))[:, :rel_bias.shape[1]]
    small_g = [d_rel_bias] + [jnp.stack([t.reshape(-1) for t in gs[name]]) for name in SMALL[1:]]
    g_sum = _allreduce_small(_pack_small(small_g), px_0b[3])
    res = _ew("adamw_small", _adamw_fn,
              [_pack_small([given[n] for n in SMALL]), g_sum, _pack_small([given["m_" + n] for n in SMALL]),
               _pack_small([given["v_" + n] for n in SMALL])], [(LANES, F32)] * 4)
    like = [given[n] for n in SMALL]
    unpacked = [_unpack_small(r, like) for r in res]
    for i, name in enumerate(SMALL):
        out[name] = [u[i] for u in unpacked]

    share_0a = _scatter_share("0a", rs["chips_0a"], res[0])
    g_1 = _scatter_done("1", rs["share_1"], share_0a[3])
    chips_0b = _scatter_chips("0b", px_0b, g_1[0])
    ready = adamw_group(BIG, 1, g_1, chips_0b[3])
    ready = adamw_group(GROUP_MID, 0, _scatter_done("0a", share_0a, ready), ready)
    share_0b = _scatter_share("0b", chips_0b, ready)
    adamw_group(GROUP_LAST, 0, _scatter_done("0b", share_0b, share_0b[3]), share_0b[3])

    return (loss, grad_x, *[out[n][0] for n in WEIGHTS], *[out[n][1] for n in WEIGHTS],
            *[out[n][2] for n in WEIGHTS], *[out[n][3] for n in WEIGHTS])
```

```python
import functools
import math

import numpy as np
import jax
import jax.numpy as jnp
from jax import lax
from jax.experimental import pallas as pl
from jax.experimental.pallas import tpu as pltpu

F32 = jnp.float32
BF16 = jnp.bfloat16
MESH = pl.DeviceIdType.MESH

HEAD_DIM = 64
LANES = 128
QBLOCK = 128
FWD_UNROLL, BWD_UNROLL = 8, 8
QKV_CHUNK = 6
N_BUCKETS = 32
MAX_DISTANCE = 1024
DILATED_CONFIGS = ((128, 1), (512, 4), (2048, 16))
SWA_RADIUS = 128
GROUP_B = 4
EPS = 1e-6
NEG = -1e30
Q_SCALE = HEAD_DIM ** -0.5
ADAM_LR, ADAM_B1, ADAM_B2, ADAM_EPS, ADAM_WD, ADAM_STEP = 0.001, 0.9, 0.999, 1e-08, 0.01, 10
VMEM_LIMIT = 56 * 2 ** 20
N_CHIPS = 4
N_DEV = 8

BIG = ("ffn1_w_in", "ffn1_w_out", "w_qkv", "w_o", "ffn2_w_in", "ffn2_w_out", "w_ple_gate", "w_ple_proj")
SMALL = ("rel_bias", "norm_ffn1", "norm_mix", "q_norm_a", "k_norm_a", "q_norm_b", "k_norm_b", "sink_b",
         "norm_ffn2", "norm_ple")
WEIGHTS = ("rel_bias", "norm_ffn1", "ffn1_w_in", "ffn1_w_out", "norm_mix", "w_qkv", "q_norm_a", "k_norm_a",
           "q_norm_b", "k_norm_b", "sink_b", "w_o", "norm_ffn2", "ffn2_w_in", "ffn2_w_out", "norm_ple",
           "w_ple_gate", "w_ple_proj")


HBM_SPEC = pl.BlockSpec(memory_space=pltpu.HBM)
ANY_SPEC = pl.BlockSpec(memory_space=pl.ANY)
SEM_SPEC = pl.BlockSpec(memory_space=pltpu.SEMAPHORE)


def _params(n_grid):
    return pltpu.CompilerParams(dimension_semantics=("arbitrary",) * n_grid, vmem_limit_bytes=VMEM_LIMIT)


def _pick_tm(rows, cap):
    t = (min(cap, rows) // 16) * 16
    while t >= 16:
        if rows % t == 0:
            return t
        t -= 16
    return rows


def _dot(a, b):
    return jnp.dot(a, b, preferred_element_type=F32)


def _dot_nt(a, b):
    return lax.dot_general(a, b, (((1,), (1,)), ((), ())), preferred_element_type=F32)


def _dot_tn(a, b):
    return lax.dot_general(a, b, (((0,), (0,)), ((), ())), preferred_element_type=F32)


def _sigmoid(z):
    return 1.0 / (1.0 + jnp.exp(-z))


def _lo_lanes(shape):
    return lax.broadcasted_iota(jnp.int32, shape, len(shape) - 1) % LANES < HEAD_DIM


def _seg_sum(blk):
    lo = _lo_lanes(blk.shape)
    s_lo = jnp.sum(jnp.where(lo, blk, 0.0), axis=1, keepdims=True)
    s_hi = jnp.sum(jnp.where(lo, 0.0, blk), axis=1, keepdims=True)
    return jnp.where(lo, s_lo, s_hi)


def _rms_bwd_tile(x, g, dh):
    r = lax.rsqrt(jnp.mean(x * x, axis=-1, keepdims=True) + EPS)
    xh = x * r
    dyg = dh * g
    dx = r * (dyg - xh * jnp.mean(dyg * xh, axis=-1, keepdims=True))
    return dx, jnp.sum(dh * xh, axis=0, keepdims=True)


def _ew(name, fn, ins, out_defs, cap=512):
    rows = ins[0].shape[0]
    tm = _pick_tm(rows, cap)
    n_in = len(ins)

    def body(*refs):
        vals = fn(*[r[...] for r in refs[:n_in]])
        if not isinstance(vals, tuple):
            vals = (vals,)
        for r, v in zip(refs[n_in:], vals):
            r[...] = v.astype(r.dtype)

    return pl.pallas_call(
        body, name=name, grid=(rows // tm,),
        in_specs=[pl.BlockSpec((tm, a.shape[1]), lambda i: (i, 0)) for a in ins],
        out_specs=[pl.BlockSpec((tm, c), lambda i: (i, 0)) for c, _ in out_defs],
        out_shape=[jax.ShapeDtypeStruct((rows, c), dt) for c, dt in out_defs],
        compiler_params=_params(1))(*ins)


def _rms_tile(xv, gv):
    r = lax.rsqrt(jnp.mean(xv * xv, axis=-1, keepdims=True) + EPS)
    return (xv * r * gv).astype(BF16)


def _ffn_up(x, g, win):
    T, D = x.shape
    wc = win.shape[2]
    tm = _pick_tm(T, 512)

    def body(x_ref, g_ref, wg_ref, wu_ref, h_ref, silu_ref, dgate_ref, act_ref, wcat):
        @pl.when(pl.program_id(1) == 0)
        def _():
            wcat[:, :wc] = wg_ref[...]
            wcat[:, wc:] = wu_ref[...]

        hv = _rms_tile(x_ref[...], g_ref[...])
        h_ref[...] = hv
        gu = _dot(hv, wcat[...])
        gte, u = gu[:, :wc], gu[:, wc:]
        sg = _sigmoid(gte)
        silu = gte * sg
        silu_ref[...] = silu.astype(BF16)
        dgate_ref[...] = ((sg + silu * (1.0 - sg)) * u).astype(BF16)
        act_ref[...] = (silu * u).astype(BF16)

    out = jax.ShapeDtypeStruct((T, 2 * wc), BF16)
    ospec = pl.BlockSpec((tm, wc), lambda j, i: (i, j))
    nt = T // tm
    h_spec = pl.BlockSpec((tm, D), lambda j, i: (jnp.where(j == 0, i, nt), 0))
    return pl.pallas_call(
        body, name="ffn_up", grid=(2, nt),
        in_specs=[pl.BlockSpec((tm, D), lambda j, i: (i, 0)), pl.BlockSpec((1, D), lambda j, i: (0, 0)),
                  pl.BlockSpec((None, D, wc), lambda j, i: (j, 0, 0)),
                  pl.BlockSpec((None, D, wc), lambda j, i: (j + 2, 0, 0))],
        out_specs=[h_spec] + [ospec] * 3, out_shape=[jax.ShapeDtypeStruct((T + tm, D), BF16)] + [out] * 3,
        scratch_shapes=[pltpu.VMEM((D, 2 * wc), BF16)], compiler_params=_params(2))(x, g, win, win)


def _mm_res(name, res, a_list, w, scale):
    T, N = res.shape
    n = len(a_list)
    widths = [a.shape[1] for a in a_list]
    tm = _pick_tm(T, 512)

    def body(*refs):
        r_ref, a_refs, w_refs, o_ref = refs[0], refs[1:1 + n], refs[1 + n:1 + 2 * n], refs[1 + 2 * n]
        acc = _dot(a_refs[0][...].astype(BF16), w_refs[0][...])
        for a_ref, w_ref in zip(a_refs[1:], w_refs[1:]):
            acc = acc + _dot(a_ref[...].astype(BF16), w_ref[...])
        o_ref[...] = r_ref[...] + scale * acc

    w_specs, off = [], 0
    for k in widths:
        w_specs.append(pl.BlockSpec((k, N), lambda i, blk=off // k: (blk, 0)))
        off += k
    return pl.pallas_call(
        body, name=name, grid=(T // tm,),
        in_specs=[pl.BlockSpec((tm, N), lambda i: (i, 0))]
        + [pl.BlockSpec((tm, k), lambda i: (i, 0)) for k in widths] + w_specs,
        out_specs=pl.BlockSpec((tm, N), lambda i: (i, 0)),
        out_shape=jax.ShapeDtypeStruct((T, N), F32), compiler_params=_params(1))(res, *a_list, *([w] * n))


def _mm_tn(name, a, a_w, b, b_w, n_slots, a_by_slot, b_by_slot, scale, tm_cap=512):
    T = b.shape[0]
    tm = _pick_tm(T, tm_cap)
    nt = T // tm

    def body(a_ref, b_ref, o_ref, acc):
        i = pl.program_id(1)

        @pl.when(i == 0)
        def _():
            acc[...] = jnp.zeros_like(acc)

        acc[...] += _dot_tn(a_ref[...].astype(BF16), b_ref[...].astype(BF16))

        @pl.when(i == nt - 1)
        def _():
            o_ref[...] = (acc[...] * scale).astype(BF16)

    return pl.pallas_call(
        body, name=name, grid=(n_slots, nt),
        in_specs=[pl.BlockSpec((tm, a_w), (lambda s, i: (i, s)) if a_by_slot else (lambda s, i: (i, 0))),
                  pl.BlockSpec((tm, b_w), (lambda s, i: (i, s)) if b_by_slot else (lambda s, i: (i, 0)))],
        out_specs=pl.BlockSpec((None, a_w, b_w), lambda s, i: (s, 0, 0)),
        out_shape=jax.ShapeDtypeStruct((n_slots, a_w, b_w), BF16),
        scratch_shapes=[pltpu.VMEM((a_w, b_w), F32)], compiler_params=_params(2))(a, b)


def _mm_tn_pairs(name, a, b, b_w, n_slots):
    T = b.shape[0]
    a_w = a.shape[1]
    tm = _pick_tm(T, 512)
    nt = T // tm

    def body(a_ref, b_ref, o_ref, acc):
        i = pl.program_id(1)

        @pl.when(i == 0)
        def _():
            acc[...] = jnp.zeros_like(acc)

        acc[...] += _dot_tn(a_ref[...], b_ref[...])

        @pl.when(i == nt - 1)
        def _():
            o_ref[0] = acc[:, :b_w].astype(BF16)
            o_ref[1] = acc[:, b_w:].astype(BF16)

    return pl.pallas_call(
        body, name=name, grid=(n_slots // 2, nt),
        in_specs=[pl.BlockSpec((tm, a_w), lambda s, i: (i, 0)), pl.BlockSpec((tm, 2 * b_w), lambda s, i: (i, s))],
        out_specs=pl.BlockSpec((2, a_w, b_w), lambda s, i: (s, 0, 0)),
        out_shape=jax.ShapeDtypeStruct((n_slots, a_w, b_w), BF16),
        scratch_shapes=[pltpu.VMEM((a_w, 2 * b_w), F32)], compiler_params=_params(2))(a, b)


def _ffn_bwd(dx, wout, silu, dgate, win, x, g, after):
    T, D = dx.shape
    F = silu.shape[1]
    n_slots, _, wc = win.shape
    tm = _pick_tm(T, 256)

    def body(dx_ref, wout_hbm, s_ref, dgt_ref, win_hbm, x_ref, g_ref, after_ref,
             dgu_ref, dxo_ref, dg_ref, wout_v, wcat, sem):
        @pl.when(pl.program_id(0) == 0)
        def _():
            copies = [pltpu.make_async_copy(wout_hbm, wout_v, sem.at[n_slots])]
            copies += [pltpu.make_async_copy(win_hbm.at[s], wcat.at[s // 2, :, pl.ds((s % 2) * wc, wc)], sem.at[s])
                       for s in range(n_slots)]
            for cp in copies:
                cp.start()
            for cp in copies:
                cp.wait()
            dg_ref[...] = jnp.zeros_like(dg_ref)

        dxv = dx_ref[...]
        dact = 0.5 * _dot_nt(dxv.astype(BF16), wout_v[...])
        d_gate = (dact * dgt_ref[...].astype(F32)).astype(BF16)
        d_up = (dact * s_ref[...].astype(F32)).astype(BF16)
        dgu_ref[:, :F] = d_gate
        dgu_ref[:, F:] = d_up
        dh = _dot_nt(d_gate, wcat[0]) + _dot_nt(d_up, wcat[1])
        dxn, dg = _rms_bwd_tile(x_ref[...], g_ref[...], dh)
        dxo_ref[...] = dxv + dxn
        dg_ref[...] += dg

    row = pl.BlockSpec((tm, D), lambda i: (i, 0))
    vec = pl.BlockSpec((1, D), lambda i: (0, 0))
    act_spec = pl.BlockSpec((tm, F), lambda i: (i, 0))
    return pl.pallas_call(
        body, name="ffn_bwd", grid=(T // tm,),
        in_specs=[row, ANY_SPEC, act_spec, act_spec, ANY_SPEC, row, vec, ANY_SPEC],
        out_specs=[pl.BlockSpec((tm, 2 * F), lambda i: (i, 0)), row, vec],
        out_shape=[jax.ShapeDtypeStruct((T, 2 * F), BF16), jax.ShapeDtypeStruct((T, D), F32),
                   jax.ShapeDtypeStruct((1, D), F32)],
        scratch_shapes=[pltpu.VMEM((F, D), BF16), pltpu.VMEM((n_slots // 2, D, 2 * wc), BF16),
                        pltpu.SemaphoreType.DMA((n_slots + 1,))],
        compiler_params=_params(1))(dx, wout, silu, dgate, win, x, g, after)


def _ple_fwd(x, g, p, wg, wp, after):
    T, D = x.shape
    P = p.shape[1]
    tm = _pick_tm(T, 256)

    def body(x_ref, g_ref, p_ref, wg_ref, wp_ref, after_ref, hn_ref, xo_ref, gate_ref, pp_ref):
        xv = x_ref[...]
        hn = _rms_tile(xv, g_ref[...])
        hn_ref[...] = hn
        gate = _sigmoid(_dot(hn, wg_ref[...]))
        pp = _dot(p_ref[...].astype(BF16), wp_ref[...])
        gate_ref[...] = gate.astype(BF16)
        pp_ref[...] = pp.astype(BF16)
        xo_ref[...] = xv + gate * pp

    row = pl.BlockSpec((tm, D), lambda i: (i, 0))
    out = jax.ShapeDtypeStruct((T, D), F32)
    half = jax.ShapeDtypeStruct((T, D), BF16)
    return pl.pallas_call(
        body, name="ple_fwd", grid=(T // tm,),
        in_specs=[row, pl.BlockSpec((1, D), lambda i: (0, 0)), pl.BlockSpec((tm, P), lambda i: (i, 0)),
                  pl.BlockSpec((D, D), lambda i: (0, 0)), pl.BlockSpec((P, D), lambda i: (0, 0)), ANY_SPEC],
        out_specs=[row, row, row, row], out_shape=[half, out, half, half],
        compiler_params=_params(1))(x, g, p, wg, wp, after)


def _ple_bwd(dx, gate, pp, hn, p, x, g, wg, after):
    T, D = x.shape
    P = p.shape[1]
    tm = _pick_tm(T, 256)
    nt = T // tm

    def body(dx_ref, gate_ref, pp_ref, hn_ref, p_ref, x_ref, g_ref, wg_ref, after_ref,
             dxo_ref, dg_ref, dwg_ref, dwp_ref, acc_g, acc_p):
        i = pl.program_id(0)

        @pl.when(i == 0)
        def _():
            acc_g[...] = jnp.zeros_like(acc_g)
            acc_p[...] = jnp.zeros_like(acc_p)
            dg_ref[...] = jnp.zeros_like(dg_ref)

        dxv = dx_ref[...]
        gate = gate_ref[...].astype(F32)
        dz = (dxv * pp_ref[...].astype(F32) * gate * (1.0 - gate)).astype(BF16)
        dpp = (dxv * gate).astype(BF16)
        acc_g[...] += _dot_tn(hn_ref[...], dz)
        acc_p[...] += _dot_tn(p_ref[...].astype(BF16), dpp)
        dxn, dg = _rms_bwd_tile(x_ref[...], g_ref[...], _dot_nt(dz, wg_ref[...]))
        dxo_ref[...] = dxv + dxn
        dg_ref[...] += dg

        @pl.when(i == nt - 1)
        def _():
            dwg_ref[...] = acc_g[...].astype(BF16)
            dwp_ref[...] = acc_p[...].astype(BF16)

    row = pl.BlockSpec((tm, D), lambda i: (i, 0))
    vec = pl.BlockSpec((1, D), lambda i: (0, 0))
    return pl.pallas_call(
        body, name="ple_bwd", grid=(nt,),
        in_specs=[row, row, row, row, pl.BlockSpec((tm, P), lambda i: (i, 0)), row, vec,
                  pl.BlockSpec((D, D), lambda i: (0, 0)), ANY_SPEC],
        out_specs=[row, vec, pl.BlockSpec((D, D), lambda i: (0, 0)), pl.BlockSpec((P, D), lambda i: (0, 0))],
        out_shape=[jax.ShapeDtypeStruct((T, D), F32), jax.ShapeDtypeStruct((1, D), F32),
                   jax.ShapeDtypeStruct((D, D), BF16), jax.ShapeDtypeStruct((P, D), BF16)],
        scratch_shapes=[pltpu.VMEM((D, D), F32), pltpu.VMEM((P, D), F32)],
        compiler_params=_params(1))(dx, gate, pp, hn, p, x, g, wg, after)


def _loss_fwd_bwd(y, tgt):
    T, D = y.shape
    tm = _pick_tm(T, 512)

    def body(y_ref, t_ref, dy_ref, loss_ref):
        e = y_ref[...] - t_ref[...]
        dy_ref[...] = e / D

        @pl.when(pl.program_id(0) == 0)
        def _():
            loss_ref[...] = jnp.zeros_like(loss_ref)

        loss_ref[...] += 0.5 * jnp.sum(jnp.mean(e * e, axis=-1, keepdims=True), axis=0, keepdims=True)

    row = pl.BlockSpec((tm, D), lambda i: (i, 0))
    return pl.pallas_call(
        body, name="loss", grid=(T // tm,), in_specs=[row, row],
        out_specs=[row, pl.BlockSpec((8, LANES), lambda i: (0, 0))],
        out_shape=[jax.ShapeDtypeStruct((T, D), F32), jax.ShapeDtypeStruct((8, LANES), F32)],
        compiler_params=_params(1))(y, tgt)


def _qkv_layout(D):
    n_a = D // (2 * LANES)
    n_b = D // (2 * LANES)
    n_kv = max(1, (2 * n_b) // GROUP_B) * HEAD_DIM // LANES
    return n_a, n_b, n_kv


def _dup_half(xv, half):
    rolled = pltpu.roll(xv, HEAD_DIM, 1)
    lo = _lo_lanes(xv.shape)
    return jnp.where(lo, xv, rolled) if half == 0 else jnp.where(lo, rolled, xv)


def _qkv_proj(x, g_mix, w, gains):
    T, D = x.shape
    W = w.shape[1]
    n_a, n_b, n_kv = _qkv_layout(D)
    tm = _pick_tm(T, 256)
    o_qb = 3 * n_a

    def norm(xv, gv, scale):
        ms = _seg_sum(xv * xv) * (1.0 / HEAD_DIM)
        return xv * lax.rsqrt(ms + EPS) * gv * scale

    def body(x_ref, gm_ref, w_ref, g_ref, h_ref, raw_ref, a_ref, b_ref):
        hv = _rms_tile(x_ref[...], gm_ref[...])
        h_ref[...] = hv
        raw = _dot(hv, w_ref[...])
        raw_ref[...] = raw

        def blk(cb):
            return raw[:, cb * LANES:(cb + 1) * LANES]

        def gn(cb):
            return g_ref[:, cb * LANES:(cb + 1) * LANES]

        for cb in range(n_a):
            a_ref[:, cb * LANES:(cb + 1) * LANES] = norm(blk(cb), gn(cb), Q_SCALE)
            cbk = n_a + cb
            a_ref[:, cbk * LANES:(cbk + 1) * LANES] = norm(blk(cbk), gn(cbk), 1.0)
            cbv = 2 * n_a + cb
            a_ref[:, cbv * LANES:(cbv + 1) * LANES] = blk(cbv)
        for cb in range(n_b):
            src = o_qb + cb
            b_ref[:, cb * LANES:(cb + 1) * LANES] = norm(blk(src), gn(src), Q_SCALE)
        for e in range(n_b):
            kvh = (2 * e) // GROUP_B
            ck = o_qb + n_b + kvh // 2
            cv = ck + n_kv
            kn = norm(blk(ck), gn(ck), 1.0)
            b_ref[:, (n_b + e) * LANES:(n_b + e + 1) * LANES] = _dup_half(kn, kvh % 2)
            b_ref[:, (2 * n_b + e) * LANES:(2 * n_b + e + 1) * LANES] = _dup_half(blk(cv), kvh % 2)

    wa, wb = 3 * n_a * LANES, 3 * n_b * LANES

    def rows(width):
        return pl.BlockSpec((tm, width), lambda i: (i, 0))

    return pl.pallas_call(
        body, name="qkv_proj", grid=(T // tm,),
        in_specs=[rows(D), pl.BlockSpec((1, D), lambda i: (0, 0)), pl.BlockSpec((D, W), lambda i: (0, 0)),
                  pl.BlockSpec((1, W), lambda i: (0, 0))],
        out_specs=[rows(D), rows(W), rows(wa), rows(wb)],
        out_shape=[jax.ShapeDtypeStruct((T, D), BF16), jax.ShapeDtypeStruct((T, W), F32),
                   jax.ShapeDtypeStruct((T, wa), F32), jax.ShapeDtypeStruct((T, wb), F32)],
        compiler_params=_params(1))(x, g_mix, w, gains)


def _qkv_bwd(raw, gains, d_a, d_b, w, h, x, g_mix, dx_in, after):
    T, W = raw.shape
    D = x.shape[1]
    n_a, n_b, n_kv = _qkv_layout(D)
    tm = _pick_tm(T, 256)
    nt = T // tm
    o_qb = 3 * n_a

    def body(raw_ref, g_ref, daq, dak, dav, dbq, dbk, dbv, w_ref, h_ref, x_ref, gm_ref, dxi_ref, after_ref,
             dx_ref, dgm_ref, dg_ref, dw_ref, acc):
        i = pl.program_id(0)

        @pl.when(i == 0)
        def _():
            dg_ref[...] = jnp.zeros_like(dg_ref)
            dgm_ref[...] = jnp.zeros_like(dgm_ref)
            acc[...] = jnp.zeros_like(acc)

        def cols(ref, cb):
            return ref[:, cb * LANES:(cb + 1) * LANES]

        pieces = [None] * (W // LANES)

        def norm_bwd(cb, dy, scale):
            xv = cols(raw_ref, cb)
            gv = cols(g_ref, cb)
            r = lax.rsqrt(_seg_sum(xv * xv) * (1.0 / HEAD_DIM) + EPS)
            xh = xv * r
            dys = dy * scale
            dyg = dys * gv
            dxv = r * (dyg - xh * (_seg_sum(dyg * xh) * (1.0 / HEAD_DIM)))
            pieces[cb] = dxv.astype(BF16)
            dg_ref[:, cb * LANES:(cb + 1) * LANES] += jnp.sum(dys * xh, axis=0, keepdims=True)

        def fold(ref, kv_blk):
            halves = []
            for half in range(2):
                kvh = 2 * kv_blk + half
                blocks = [e for e in range(n_b) if (2 * e) // GROUP_B == kvh]
                s = cols(ref, blocks[0])
                for e in blocks[1:]:
                    s = s + cols(ref, e)
                halves.append(s + pltpu.roll(s, HEAD_DIM, 1))
            return jnp.where(_lo_lanes(halves[0].shape), halves[0], halves[1])

        for cb in range(n_a):
            norm_bwd(cb, cols(daq, cb), Q_SCALE)
        for cb in range(n_a):
            norm_bwd(n_a + cb, cols(dak, cb), 1.0)
        for cb in range(n_a):
            pieces[2 * n_a + cb] = cols(dav, cb).astype(BF16)
        for cb in range(n_b):
            norm_bwd(o_qb + cb, cols(dbq, cb), Q_SCALE)
        for kb in range(n_kv):
            norm_bwd(o_qb + n_b + kb, fold(dbk, kb), 1.0)
        for kb in range(n_kv):
            pieces[o_qb + n_b + n_kv + kb] = fold(dbv, kb).astype(BF16)

        hv = h_ref[...]
        dh = None
        for c0 in range(0, W // LANES, QKV_CHUNK):
            chunk = jnp.concatenate(pieces[c0:c0 + QKV_CHUNK], axis=1)
            lanes = slice(c0 * LANES, (c0 + QKV_CHUNK) * LANES)
            part = _dot_nt(chunk, w_ref[:, lanes])
            dh = part if dh is None else dh + part
            acc[:, lanes] += _dot_tn(hv, chunk)
        dxn, dgm = _rms_bwd_tile(x_ref[...], gm_ref[...], dh)
        dx_ref[...] = dxi_ref[...] + dxn
        dgm_ref[...] += dgm

        @pl.when(i == nt - 1)
        def _():
            dw_ref[...] = acc[...].astype(BF16)

    hw_a, hw_b = n_a * LANES, n_b * LANES

    def rows(width):
        return pl.BlockSpec((tm, width), lambda i: (i, 0))

    def fixed(r, c):
        return pl.BlockSpec((r, c), lambda i: (0, 0))

    return pl.pallas_call(
        body, name="qkv_bwd", grid=(nt,),
        in_specs=[rows(W), fixed(1, W)] + [rows(hw_a)] * 3 + [rows(hw_b)] * 3
        + [fixed(D, W), rows(D), rows(D), fixed(1, D), rows(D), ANY_SPEC],
        out_specs=[rows(D), fixed(1, D), fixed(1, W), fixed(D, W)],
        out_shape=[jax.ShapeDtypeStruct((T, D), F32), jax.ShapeDtypeStruct((1, D), F32),
                   jax.ShapeDtypeStruct((1, W), F32), jax.ShapeDtypeStruct((D, W), BF16)],
        scratch_shapes=[pltpu.VMEM((D, W), F32)],
        compiler_params=_params(1))(raw, gains, *d_a, *d_b, w, h, x, g_mix, dx_in, after)


def _t5_bucket_np(rel):
    half = N_BUCKETS // 2
    max_exact = half // 2
    ret = np.where(rel > 0, half, 0)
    n = np.abs(rel)
    nf = np.maximum(n, 1).astype(np.float32)
    large = max_exact + (np.log(nf / np.float32(max_exact)) / np.float32(math.log(MAX_DISTANCE / max_exact))
                         * np.float32(half - max_exact)).astype(np.int32)
    large = np.minimum(large, half - 1)
    return ret + np.where(n < max_exact, n, large)


def _window_pad(radius):
    assert radius <= QBLOCK
    return HEAD_DIM if radius <= HEAD_DIM else QBLOCK


def _bucket_maps(configs):
    pad = _window_pad(configs[0][0])
    q = np.arange(QBLOCK)[:, None]
    kk = np.arange(QBLOCK + 2 * pad)[None, :]
    rel = kk - pad - q
    maps = [np.where(np.abs(rel) <= radius, _t5_bucket_np(rel * dil), -1) for radius, dil in configs]
    return np.stack(maps).astype(np.int32)


def _bias_build(rel_bias, bmaps, col0):
    n_sets, _, W = bmaps.shape
    n_buckets, n_cols = rel_bias.shape
    n_heads = n_cols // 2

    def body(rb_ref, bm_ref, o_ref):
        h = pl.program_id(1)
        bm = bm_ref[0:1, :]
        head = lax.broadcasted_iota(jnp.int32, (n_buckets, n_cols), 1) == col0 + h
        per_bucket = jnp.sum(jnp.where(head, rb_ref[...], 0.0), axis=1, keepdims=True)
        hit = lax.broadcasted_iota(jnp.int32, (n_buckets, W), 0) == bm
        row0 = jnp.sum(jnp.where(hit, per_bucket, 0.0), axis=0, keepdims=True) + jnp.where(bm < 0, NEG, 0.0)
        o_ref[...] = pltpu.roll(jnp.broadcast_to(row0, (QBLOCK, W)), 0, 1, stride=1, stride_axis=0)

    return pl.pallas_call(
        body, name="bias_build", grid=(n_sets, n_heads),
        in_specs=[pl.BlockSpec((n_buckets, n_cols), lambda s, h: (0, 0)),
                  pl.BlockSpec((None, QBLOCK, W), lambda s, h: (s, 0, 0))],
        out_specs=pl.BlockSpec((None, None, QBLOCK, W), lambda s, h: (s, h, 0, 0)),
        out_shape=jax.ShapeDtypeStruct((n_sets, n_heads, QBLOCK, W), F32),
        compiler_params=_params(2))(rel_bias, bmaps)


def _bias_grad(dtiles, bmaps, col0):
    n_sets, _, W = bmaps.shape
    n_heads = dtiles[0].shape[1]
    n_l = len(dtiles)

    def body(*refs):
        bm_ref, o_ref = refs[0], refs[1 + n_l]
        s, h = pl.program_id(0), pl.program_id(1)

        @pl.when((s == 0) & (h == 0))
        def _():
            o_ref[...] = jnp.zeros_like(o_ref)

        d = refs[1][...]
        for r in refs[2:1 + n_l]:
            d = d + r[...]
        acc8 = d[0:8, :]
        for a in range(1, QBLOCK // 8):
            acc8 = acc8 + pltpu.roll(d[8 * a:8 * a + 8, :], W - 8 * a, 1)
        per_offset = acc8[0:1, :]
        for b in range(1, 8):
            per_offset = per_offset + pltpu.roll(acc8[b:b + 1, :], W - b, 1)
        bucket = lax.broadcasted_iota(jnp.int32, (N_BUCKETS, W), 0)
        hit = bucket == bm_ref[0:1, :]
        per_bucket = jnp.sum(jnp.where(hit, per_offset, 0.0), axis=1, keepdims=True)
        lanes = lax.broadcasted_iota(jnp.int32, o_ref.shape, 1)
        o_ref[...] += jnp.where(lanes == col0 + h, per_bucket, 0.0)

    tile = pl.BlockSpec((None, None, QBLOCK, W), lambda s, h: (s, h, 0, 0))
    return pl.pallas_call(
        body, name="bias_grad", grid=(n_sets, n_heads),
        in_specs=[pl.BlockSpec((None, QBLOCK, W), lambda s, h: (s, 0, 0))] + [tile] * n_l,
        out_specs=pl.BlockSpec((N_BUCKETS, LANES), lambda s, h: (0, 0)),
        out_shape=jax.ShapeDtypeStruct((N_BUCKETS, LANES), F32), compiler_params=_params(2))(bmaps, *dtiles)


def _rows(l_start, n, d, r):
    if d == 1:
        return pl.ds(pl.multiple_of(l_start, 8), n)
    return pl.ds(l_start * d + r, n, stride=d)


def _stack_heads(xv, lo):
    z = jnp.zeros_like(xv)
    return jnp.concatenate([jnp.where(lo, xv, z), jnp.where(lo, z, xv)], axis=0)


def _unstack_heads(xv, lo):
    return jnp.where(lo, xv[:QBLOCK], xv[QBLOCK:])


def _per_head_rows(v0, v1):
    if jnp.ndim(v0) == 0:
        return jnp.where(lax.broadcasted_iota(jnp.int32, (2 * QBLOCK, 1), 0) < QBLOCK, v0, v1)
    return jnp.concatenate([v0, v1], axis=0)


def _block_geometry(b, nb_sub, pad):
    r, lb = b // nb_sub, b % nb_sub
    l0 = lb * QBLOCK
    lp = jnp.maximum(l0 - pad, 0)
    ln = jnp.minimum(l0 + QBLOCK, nb_sub * QBLOCK - pad)
    return r, l0, lp, ln, (lb == 0).astype(jnp.int32) + 2 * (lb == nb_sub - 1).astype(jnp.int32)


def _edge_variants(bias, pad):
    n_br, _, _, W = bias.shape
    col = np.arange(W)
    left, right = col < pad, col >= pad + QBLOCK
    masked = jnp.asarray(np.stack([np.zeros(W, bool), left, right, left | right]))
    return jnp.where(masked[None, :, None, :], NEG, bias.reshape(n_br, 1, -1, W))


def _window(ref, l0, lp, ln, pad, d, r):
    return jnp.concatenate([ref[_rows(lp, pad, d, r), :], ref[_rows(l0, QBLOCK, d, r), :],
                            ref[_rows(ln, pad, d, r), :]], axis=0)


def _attn_fwd(qkv, bias, sink, dils, pad):
    T = qkv.shape[0]
    hw = qkv.shape[1] // 3
    ng = hw // LANES
    n_br = len(dils)
    n_blocks = T // QBLOCK
    W = QBLOCK + 2 * pad
    chunk = 256

    def body(sink_ref, q_ref, k_ref, v_ref, bias_ref, o_ref, lse_ref, *scratch):
        g = pl.program_id(0)
        lo = _lo_lanes((QBLOCK, LANES))
        snk = _per_head_rows(sink_ref[2 * g], sink_ref[2 * g + 1])
        for c, d in enumerate(dils):
            nb_sub = n_blocks // d
            o_dst = scratch[0].at[c] if n_br > 1 else o_ref
            l_dst = scratch[1].at[c] if n_br > 1 else lse_ref

            def block(b, carry, c=c, d=d, nb_sub=nb_sub, o_dst=o_dst, l_dst=l_dst):
                r, l0, lp, ln, edge = _block_geometry(b, nb_sub, pad)
                q = _stack_heads(q_ref[_rows(l0, QBLOCK, d, r), :].astype(BF16), lo)
                k = _window(k_ref, l0, lp, ln, pad, d, r).astype(BF16)
                v = _window(v_ref, l0, lp, ln, pad, d, r).astype(BF16)
                s = _dot_nt(q, k) + bias_ref[c, edge]
                m = jnp.maximum(jnp.max(s, axis=1, keepdims=True), snk)
                p = jnp.exp(s - m)
                den = jnp.sum(p, axis=1, keepdims=True) + jnp.exp(snk - m)
                o_dst[_rows(l0, QBLOCK, d, r), :] = _unstack_heads(_dot(p.astype(BF16), v) / den, lo)
                l_dst[_rows(l0, QBLOCK, d, r), :] = _unstack_heads(
                    jnp.broadcast_to(m + jnp.log(den), (2 * QBLOCK, LANES)), lo)
                return carry

            lax.fori_loop(0, n_blocks, block, 0, unroll=FWD_UNROLL)

        if n_br > 1:
            def merge(i, carry):
                rs = pl.ds(pl.multiple_of(i * chunk, chunk), chunk)
                ls = [scratch[1][c, rs, :] for c in range(n_br)]
                m = ls[0]
                for t in ls[1:]:
                    m = jnp.maximum(m, t)
                ws = [jnp.exp(t - m) for t in ls]
                z = ws[0]
                acc = ws[0] * scratch[0][0, rs, :]
                for c in range(1, n_br):
                    z = z + ws[c]
                    acc = acc + ws[c] * scratch[0][c, rs, :]
                o_ref[rs, :] = acc / z
                lse_ref[rs, :] = m + jnp.log(z)
                return carry

            lax.fori_loop(0, T // chunk, merge, 0)

    def col(base):
        return pl.BlockSpec((T, LANES), lambda g: (0, base + g))

    out = jax.ShapeDtypeStruct((T, hw), F32)
    scratch = [pltpu.VMEM((n_br, T, LANES), F32)] * 2 if n_br > 1 else []
    return pl.pallas_call(
        body, name="attn_fwd", grid=(ng,),
        in_specs=[pl.BlockSpec(memory_space=pltpu.SMEM), col(0), col(ng), col(2 * ng),
                  pl.BlockSpec((n_br, 4, 2 * QBLOCK, W), lambda g: (0, 0, g, 0))],
        out_specs=[col(0), col(0)], out_shape=[out, out], scratch_shapes=scratch,
        compiler_params=_params(1))(sink, qkv, qkv, qkv, bias)


def _attn_bwd(qkv, bias, sink, dils, pad, do, lse, dd, col_base, after):
    T = qkv.shape[0]
    hw = qkv.shape[1] // 3
    ng = hw // LANES
    n_br = len(dils)
    n_blocks = T // QBLOCK
    W = QBLOCK + 2 * pad

    def body(sink_ref, q_ref, k_ref, v_ref, bias_ref, do_ref, lse_ref, dd_ref, after_ref,
             dq_ref, dk_ref, dv_ref, dt_ref, ds_ref):
        g = pl.program_id(0)
        dq_ref[...] = jnp.zeros_like(dq_ref)
        dk_ref[...] = jnp.zeros_like(dk_ref)
        dv_ref[...] = jnp.zeros_like(dv_ref)
        dt_ref[...] = jnp.zeros_like(dt_ref)
        ds_ref[...] = jnp.zeros_like(ds_ref)
        lo = _lo_lanes((QBLOCK, LANES))
        snk = jnp.where(lo, sink_ref[2 * g], sink_ref[2 * g + 1])
        for c, d in enumerate(dils):
            nb_sub = n_blocks // d

            def block(b, carry, c=c, d=d, nb_sub=nb_sub):
                r, l0, lp, ln, edge = _block_geometry(b, nb_sub, pad)
                rows_q = _rows(l0, QBLOCK, d, r)
                q = _stack_heads(q_ref[rows_q, :].astype(BF16), lo)
                k = _window(k_ref, l0, lp, ln, pad, d, r).astype(BF16)
                v = _window(v_ref, l0, lp, ln, pad, d, r).astype(BF16)
                dob = _stack_heads(do_ref[rows_q, :].astype(BF16), lo)
                lse_b = lse_ref[rows_q, :]
                dd_b = dd_ref[rows_q, :]
                s = _dot_nt(q, k) + bias_ref[c, edge]
                p = jnp.exp(s - _per_head_rows(lse_b[:, 0:1], lse_b[:, HEAD_DIM:HEAD_DIM + 1]))
                ds = p * (_dot_nt(dob, v) - _per_head_rows(dd_b[:, 0:1], dd_b[:, HEAD_DIM:HEAD_DIM + 1]))
                dsb = ds.astype(BF16)
                dkw = _dot_tn(dsb, q)
                dvw = _dot_tn(p.astype(BF16), dob)
                dt_ref[c] += ds
                dq_ref[rows_q, :] += _unstack_heads(_dot(dsb, k), lo)
                ds_ref[0:1, :] += jnp.sum(-jnp.exp(snk - lse_b) * dd_b, axis=0, keepdims=True)
                for part, (start, n) in zip((0, pad, pad + QBLOCK), ((lp, pad), (l0, QBLOCK), (ln, pad))):
                    dk_ref[_rows(start, n, d, r), :] += dkw[part:part + n]
                    dv_ref[_rows(start, n, d, r), :] += dvw[part:part + n]
                return carry

            lax.fori_loop(0, n_blocks, block, 0, unroll=BWD_UNROLL)

    def col(base):
        return pl.BlockSpec((T, LANES), lambda g: (0, base + g))

    tile = pl.BlockSpec((n_br, 2 * QBLOCK, W), lambda g: (0, g, 0))
    full = jax.ShapeDtypeStruct((T, hw), F32)
    dq, dk, dv, dt, dsink = pl.pallas_call(
        body, name="attn_bwd", grid=(ng,),
        in_specs=[pl.BlockSpec(memory_space=pltpu.SMEM), col(0), col(ng), col(2 * ng),
                  pl.BlockSpec((n_br, 4, 2 * QBLOCK, W), lambda g: (0, 0, g, 0)),
                  col(col_base), col(0), col(col_base), ANY_SPEC],
        out_specs=[col(0), col(0), col(0), tile, pl.BlockSpec((None, 8, LANES), lambda g: (g, 0, 0))],
        out_shape=[full, full, full, jax.ShapeDtypeStruct((n_br, 2 * ng * QBLOCK, W), F32),
                   jax.ShapeDtypeStruct((ng, 8, LANES), F32)],
        compiler_params=_params(1))(sink, qkv, qkv, qkv, bias, do, lse, dd, after)
    return dq, dk, dv, dt.reshape(n_br, 2 * ng, QBLOCK, W), dsink


def _mix_bwd_in(dx, wo, o_list):
    T, D = dx.shape
    widths = [o.shape[1] for o in o_list]
    hw = sum(widths)
    n = len(o_list)
    tm = _pick_tm(T, 256)
    nt = T // tm

    def body(*refs):
        dx_ref, w_ref, o_refs = refs[0], refs[1], refs[2:2 + n]
        do_ref, dd_ref, dw_ref, acc = refs[2 + n:]
        i = pl.program_id(0)

        @pl.when(i == 0)
        def _():
            acc[...] = jnp.zeros_like(acc)

        dxb = dx_ref[...].astype(BF16)
        dov = _dot_nt(dxb, w_ref[...])
        do_ref[...] = dov
        off = 0
        for o_ref, k in zip(o_refs, widths):
            ov = o_ref[...]
            prod = dov[:, off:off + k] * ov
            for cb in range(k // LANES):
                dd_ref[:, off + cb * LANES:off + (cb + 1) * LANES] = _seg_sum(prod[:, cb * LANES:(cb + 1) * LANES])
            acc[off:off + k, :] += _dot_tn(ov.astype(BF16), dxb)
            off += k

        @pl.when(i == nt - 1)
        def _():
            dw_ref[...] = acc[...].astype(BF16)

    row = pl.BlockSpec((tm, hw), lambda i: (i, 0))
    out = jax.ShapeDtypeStruct((T, hw), F32)
    return pl.pallas_call(
        body, name="mix_bwd_in", grid=(nt,),
        in_specs=[pl.BlockSpec((tm, D), lambda i: (i, 0)), pl.BlockSpec((hw, D), lambda i: (0, 0))]
        + [pl.BlockSpec((tm, k), lambda i: (i, 0)) for k in widths],
        out_specs=[row, row, pl.BlockSpec((hw, D), lambda i: (0, 0))],
        out_shape=[out, out, jax.ShapeDtypeStruct((hw, D), BF16)],
        scratch_shapes=[pltpu.VMEM((hw, D), F32)], compiler_params=_params(1))(dx, wo, *o_list)


def _mesh_pos():
    return lax.axis_index("x"), lax.axis_index("y"), lax.axis_index("c")


def _my_chip():
    return 2 * lax.axis_index("x") + lax.axis_index("y")


def _other_chips(x, y):
    return [(1 - x, y), (x, 1 - y), (1 - x, 1 - y)]


def _half_rows(rows, cc):
    hr = rows // 2
    return pl.ds(pl.multiple_of(cc * hr, 16), hr)


def _cast_into_slot(w, l):
    _, R, C = w.shape
    tm = _pick_tm(R, 512)

    def body(w_ref, o_ref):
        o_ref[...] = w_ref[...].astype(BF16)

    return pl.pallas_call(
        body, name="cast_into_slot", grid=(R // tm,),
        in_specs=[pl.BlockSpec((None, tm, C), lambda i: (l, i, 0))],
        out_specs=pl.BlockSpec((None, tm, C), lambda i: (_my_chip(), i, 0)),
        out_shape=jax.ShapeDtypeStruct((N_CHIPS, R, C), BF16), compiler_params=_params(1))(w)


def _split_start(name, arrays, make_copies, n_sem, after):
    n = len(arrays)
    n_in = n + (0 if after is None else 1)

    def body(*refs):
        send_s, recv_s = refs[n_in], refs[n_in + 1]
        token = refs[n_in + 2 + n]
        for send, _ in make_copies(refs[:n], send_s, recv_s):
            send.start()
        token[...] = jnp.zeros_like(token)

    res = pl.pallas_call(
        body, name=name,
        out_shape=(pltpu.SemaphoreType.DMA((n_sem,)), pltpu.SemaphoreType.DMA((n_sem,)),
                   *[pltpu.HBM(a.shape, a.dtype) for a in arrays], jax.ShapeDtypeStruct((8, LANES), F32)),
        in_specs=[HBM_SPEC] * n + [ANY_SPEC] * (n_in - n),
        out_specs=(SEM_SPEC, SEM_SPEC, *([HBM_SPEC] * n), pl.BlockSpec(memory_space=pltpu.VMEM)),
        input_output_aliases={i: 2 + i for i in range(n)},
        compiler_params=pltpu.CompilerParams(has_side_effects=pltpu.SideEffectType.DATAFLOW_SIDE_EFFECTING),
    )(*[pltpu.with_memory_space_constraint(a, pltpu.HBM) for a in arrays], *([] if after is None else [after]))
    return res[0], res[1], list(res[2:2 + n]), res[2 + n]


def _split_wait(name, send_s, recv_s, arrays, make_copies, after):
    n = len(arrays)

    def body(*refs):
        for send, landed in make_copies(refs[:n], refs[n], refs[n + 1]):
            send.wait_send()
            landed.wait_recv()

    return list(pl.pallas_call(
        body, name=name, out_shape=[pltpu.HBM(a.shape, a.dtype) for a in arrays],
        in_specs=[HBM_SPEC] * n + [SEM_SPEC, SEM_SPEC, ANY_SPEC], out_specs=[HBM_SPEC] * n,
        input_output_aliases={i: i for i in range(n)},
        compiler_params=pltpu.CompilerParams(has_side_effects=pltpu.SideEffectType.DATAFLOW_SIDE_EFFECTING),
    )(*arrays, send_s, recv_s, after))


def _gather_copies(shapes):
    n = len(shapes)

    def make(refs, send_s, recv_s):
        x, y, c = _mesh_pos()
        my = 2 * x + y
        copies = []
        for w in range(n):
            for k, (px, py) in enumerate(_other_chips(x, y)):
                def part(slot, w=w):
                    return refs[w].at[slot, _half_rows(shapes[w][1], c), :]
                sems = dict(send_sem=send_s.at[k * n + w], recv_sem=recv_s.at[k * n + w],
                            device_id=(px, py, c), device_id_type=MESH)
                copies.append((pltpu.make_async_remote_copy(src_ref=part(my), dst_ref=part(my), **sems),
                               pltpu.make_async_remote_copy(src_ref=part(2 * px + py), dst_ref=part(2 * px + py), **sems)))
        return copies

    return make


def _forward_copies(shapes):
    n = len(shapes)

    def make(refs, send_s, recv_s):
        x, y, c = _mesh_pos()
        copies = []
        for w in range(n):
            for k, (px, py) in enumerate(_other_chips(x, y)):
                def part(cc, w=w, slot=2 * px + py):
                    return refs[w].at[slot, _half_rows(shapes[w][1], cc), :]
                sems = dict(send_sem=send_s.at[k * n + w], recv_sem=recv_s.at[k * n + w],
                            device_id=(x, y, 1 - c), device_id_type=MESH)
                copies.append((pltpu.make_async_remote_copy(src_ref=part(c), dst_ref=part(c), **sems),
                               pltpu.make_async_remote_copy(src_ref=part(1 - c), dst_ref=part(1 - c), **sems)))
        return copies

    return make


def _pair_forward(bufs):
    n = len(bufs)
    make = _forward_copies([b.shape for b in bufs])

    def body(*refs):
        copies = make(refs[n:2 * n], refs[2 * n], refs[2 * n + 1])
        for send, _ in copies:
            send.start()
        for _, landed in copies:
            landed.wait_recv()
        for send, _ in copies:
            send.wait_send()

    return list(pl.pallas_call(
        body, name="ag_pair_forward", in_specs=[HBM_SPEC] * n, out_specs=[HBM_SPEC] * n,
        out_shape=[jax.ShapeDtypeStruct(b.shape, b.dtype) for b in bufs],
        input_output_aliases={w: w for w in range(n)},
        scratch_shapes=[pltpu.SemaphoreType.DMA((3 * n,)), pltpu.SemaphoreType.DMA((3 * n,))],
    )(*bufs))


def _pair_exchange_copies(shapes):
    n = len(shapes)

    def make(refs, send_s, recv_s):
        x, y, c = _mesh_pos()
        copies = []
        for t in range(n):
            sems = dict(send_sem=send_s.at[t], recv_sem=recv_s.at[t], device_id=(x, y, 1 - c), device_id_type=MESH)
            land = refs[n + t]
            copies.append((pltpu.make_async_remote_copy(
                src_ref=refs[t].at[:, _half_rows(shapes[t][1], 1 - c), :], dst_ref=land, **sems),
                pltpu.make_async_remote_copy(src_ref=land, dst_ref=land, **sems)))
        return copies

    return make


def _pair_share_copies(shapes):
    n = len(shapes)

    def make(refs, send_s, recv_s):
        x, y, c = _mesh_pos()
        copies = []
        for t in range(n):
            def half(cc, t=t):
                return refs[t].at[_half_rows(shapes[t][0], cc), :]
            sems = dict(send_sem=send_s.at[t], recv_sem=recv_s.at[t], device_id=(x, y, 1 - c), device_id_type=MESH)
            copies.append((pltpu.make_async_remote_copy(src_ref=half(c), dst_ref=half(c), **sems),
                           pltpu.make_async_remote_copy(src_ref=half(1 - c), dst_ref=half(1 - c), **sems)))
        return copies

    return make


def _rs_add_pair(grad, recv):
    n_slot, hr, C = recv.shape
    tm = _pick_tm(hr, 192)
    nb = hr // tm

    def body(a_ref, b_ref, o_ref):
        o_ref[...] = (a_ref[...].astype(F32) + b_ref[...].astype(F32)).astype(BF16)

    blk = pl.BlockSpec((n_slot, tm, C), lambda i: (0, i, 0))
    return pl.pallas_call(
        body, name="rs_add_pair", grid=(nb,),
        in_specs=[pl.BlockSpec((n_slot, tm, C), lambda i: (0, lax.axis_index("c") * nb + i, 0)), blk],
        out_specs=blk, out_shape=jax.ShapeDtypeStruct(recv.shape, BF16), compiler_params=_params(1))(grad, recv)


def _scatter_copies(n):
    def make(refs, send_s, recv_s):
        x, y, c = _mesh_pos()
        copies = []
        for t in range(n):
            for k, (px, py) in enumerate(_other_chips(x, y)):
                sems = dict(send_sem=send_s.at[3 * t + k], recv_sem=recv_s.at[3 * t + k],
                            device_id=(px, py, c), device_id_type=MESH)
                land = refs[n + t].at[k]
                copies.append((pltpu.make_async_remote_copy(src_ref=refs[t].at[2 * px + py], dst_ref=land, **sems),
                               pltpu.make_async_remote_copy(src_ref=land, dst_ref=land, **sems)))
        return copies

    return make


def _rs_add_chips(part, recv):
    _, hr, C = part.shape
    tm = _pick_tm(hr, 256)
    nb = hr // tm

    def body(a_ref, r0, r1, r2, o_ref):
        o_ref[...] = ((a_ref[...].astype(F32) + r0[...].astype(F32)) + r1[...].astype(F32)) + r2[...].astype(F32)

    def rel(k):
        return pl.BlockSpec((None, tm, C), lambda i: (k, i, 0))

    return pl.pallas_call(
        body, name="rs_add_chips", grid=(nb,),
        in_specs=[pl.BlockSpec((None, tm, C), lambda i: (_my_chip(), i, 0)), rel(0), rel(1), rel(2)],
        out_specs=pl.BlockSpec((tm, C), lambda i: (lax.axis_index("c") * nb + i, 0)),
        out_shape=jax.ShapeDtypeStruct((2 * hr, C), F32), compiler_params=_params(1))(part, recv, recv, recv)


def _allreduce_small(v, after):
    rows = v.shape[0]

    def body(v_ref, after_ref, o_ref, buf, send_s, recv_s):
        x, y, c = _mesh_pos()
        me = 4 * x + 2 * y + c
        buf[me] = v_ref[...]
        copies = []
        for r in range(1, N_DEV):
            px = 1 - x if r & 4 else x
            py = 1 - y if r & 2 else y
            pc = 1 - c if r & 1 else c
            send = pltpu.make_async_remote_copy(
                src_ref=v_ref, dst_ref=buf.at[me], send_sem=send_s.at[r - 1], recv_sem=recv_s.at[r - 1],
                device_id=(px, py, pc), device_id_type=MESH)
            peer_slot = buf.at[4 * px + 2 * py + pc]
            landed = pltpu.make_async_remote_copy(
                src_ref=peer_slot, dst_ref=peer_slot, send_sem=send_s.at[r - 1], recv_sem=recv_s.at[r - 1],
                device_id=(px, py, pc), device_id_type=MESH)
            copies.append((send, landed))
        for send, _ in copies:
            send.start()
        for _, landed in copies:
            landed.wait_recv()
        for send, _ in copies:
            send.wait_send()
        acc = buf[0]
        for j in range(1, N_DEV):
            acc = acc + buf[j]
        o_ref[...] = acc

    vm = pl.BlockSpec(memory_space=pltpu.VMEM)
    return pl.pallas_call(
        body, name="allreduce_small", in_specs=[vm, ANY_SPEC], out_specs=vm,
        out_shape=jax.ShapeDtypeStruct((rows, LANES), F32),
        scratch_shapes=[pltpu.VMEM((N_DEV, rows, LANES), F32), pltpu.SemaphoreType.DMA((N_DEV - 1,)),
                        pltpu.SemaphoreType.DMA((N_DEV - 1,))],
    )(v, after)


def _adamw_fn(w, g, m, v):
    m2 = ADAM_B1 * m + (1.0 - ADAM_B1) * g
    v2 = ADAM_B2 * v + (1.0 - ADAM_B2) * (g * g)
    m_hat = m2 / (1.0 - ADAM_B1 ** ADAM_STEP)
    v_hat = v2 / (1.0 - ADAM_B2 ** ADAM_STEP)
    delta = -ADAM_LR * (m_hat / (jnp.sqrt(v_hat) + ADAM_EPS) + ADAM_WD * w)
    return g, delta, m2, v2


def _adamw_layer(w, g, m, v, l, prev, after):
    NL, R, C = w.shape
    tm = _pick_tm(R, 256)
    n_prev = 0 if prev is None else 4

    def body(w_ref, g_ref, m_ref, v_ref, after_ref, *rest):
        outs = rest[n_prev:]
        for o_ref, val in zip(outs, _adamw_fn(w_ref[...], g_ref[...], m_ref[...], v_ref[...])):
            o_ref[...] = val

    lay = pl.BlockSpec((None, tm, C), lambda i: (l, i, 0))
    shape = jax.ShapeDtypeStruct((NL, R, C), F32)
    return pl.pallas_call(
        body, name="adamw", grid=(R // tm,),
        in_specs=[lay, pl.BlockSpec((tm, C), lambda i: (i, 0)), lay, lay, ANY_SPEC] + [ANY_SPEC] * n_prev,
        out_specs=[lay] * 4, out_shape=[shape] * 4,
        input_output_aliases={5 + j: j for j in range(n_prev)},
        compiler_params=_params(1))(w, g, m, v, after, *(prev or []))


def _pack_small(parts):
    out = []
    for a in parts:
        flat = a.reshape(-1)
        n = -(-flat.shape[0] // (8 * LANES)) * 8 * LANES
        out.append(jnp.pad(flat, (0, n - flat.shape[0])).reshape(-1, LANES))
    return jnp.concatenate(out, axis=0)


def _unpack_small(packed, like):
    out, r = [], 0
    for a in like:
        size = int(np.prod(a.shape))
        rows = -(-size // (8 * LANES)) * 8
        out.append(packed[r:r + rows].reshape(-1)[:size].reshape(a.shape))
        r += rows
    return out


def _ffn_forward(x, g, win, wout):
    h, silu, dgate, act = _ffn_up(x, g, win)
    if callable(wout):
        wout = wout(act)
    return _mm_res("ffn_down", x, [act], wout, 0.5), (x, h, silu, dgate, act)


def _ffn_backward(dx, saved, g, win, wout, after):
    x, h, silu, dgate, act = saved
    D = x.shape[1]
    wc = win.shape[2]
    dgu, dx_in, dg = _ffn_bwd(dx, wout, silu, dgate, win, x, g, after)
    dwout = _mm_tn("dw_ffn_out", act, wc, dx, D, 2, True, False, 0.5)
    dwin = _mm_tn_pairs("dw_ffn_in", h, dgu, wc, 4)
    return dx_in, dg, dwin, dwout.reshape(N_CHIPS, -1, D)


GROUP_MID = ("w_o", "ffn2_w_in", "ffn2_w_out", "w_ple_gate", "w_ple_proj")
GROUP_LAST = ("w_qkv", "ffn1_w_in", "ffn1_w_out")
GATHER_L0 = (("a", ("ffn1_w_in",)), ("b", ("ffn1_w_out",)), ("c", ("w_qkv", "w_o")),
             ("d", ("ffn2_w_in", "ffn2_w_out", "w_ple_gate", "w_ple_proj")))


def _gather_start(tag, slotted, after):
    return _split_start("ag_start_" + tag, slotted, _gather_copies([a.shape for a in slotted]), 3 * len(slotted), after)


def _gather_finish(tag, started, after):
    send_s, recv_s, arrays, _ = started
    return _pair_forward(_split_wait("ag_wait_" + tag, send_s, recv_s, arrays,
                                     _gather_copies([a.shape for a in arrays]), after))


def _scatter_exchange(tag, grads):
    n = len(grads)
    land = [lax.empty((g_.shape[0], g_.shape[1] // 2, g_.shape[2]), g_.dtype) for g_ in grads]
    return _split_start("rs_px_start_" + tag, list(grads) + land, _pair_exchange_copies([g_.shape for g_ in grads]),
                        n, None)


def _scatter_chips(tag, started, after):
    send_s, recv_s, arrays, _ = started
    n = len(arrays) // 2
    arrays = _split_wait("rs_px_wait_" + tag, send_s, recv_s, arrays,
                         _pair_exchange_copies([a.shape for a in arrays[:n]]), after)
    part = [_rs_add_pair(g_, r_) for g_, r_ in zip(arrays[:n], arrays[n:])]
    land = [lax.empty((3,) + p_.shape[1:], p_.dtype) for p_ in part]
    return _split_start("rs_start_" + tag, part + land, _scatter_copies(n), 3 * n, None)


def _scatter_share(tag, started, after):
    send_s, recv_s, arrays, _ = started
    n = len(arrays) // 2
    arrays = _split_wait("rs_wait_" + tag, send_s, recv_s, arrays, _scatter_copies(n), after)
    halves = [_rs_add_chips(p_, r_) for p_, r_ in zip(arrays[:n], arrays[n:])]
    return _split_start("rs_ps_start_" + tag, halves, _pair_share_copies([h_.shape for h_ in halves]), n, None)


def _scatter_done(tag, started, after):
    send_s, recv_s, arrays, _ = started
    return _split_wait("rs_ps_wait_" + tag, send_s, recv_s, arrays, _pair_share_copies([a.shape for a in arrays]), after)


def kernel(x, p, rel_bias, norm_ffn1, ffn1_w_in, ffn1_w_out, norm_mix, w_qkv, q_norm_a, k_norm_a, q_norm_b, k_norm_b, sink_b, w_o, norm_ffn2, ffn2_w_in, ffn2_w_out, norm_ple, w_ple_gate, w_ple_proj, loss_target, m_rel_bias, m_norm_ffn1, m_ffn1_w_in, m_ffn1_w_out, m_norm_mix, m_w_qkv, m_q_norm_a, m_k_norm_a, m_q_norm_b, m_k_norm_b, m_sink_b, m_w_o, m_norm_ffn2, m_ffn2_w_in, m_ffn2_w_out, m_norm_ple, m_w_ple_gate, m_w_ple_proj, v_rel_bias, v_norm_ffn1, v_ffn1_w_in, v_ffn1_w_out, v_norm_mix, v_w_qkv, v_q_norm_a, v_k_norm_a, v_q_norm_b, v_k_norm_b, v_sink_b, v_w_o, v_norm_ffn2, v_ffn2_w_in, v_ffn2_w_out, v_norm_ple, v_w_ple_gate, v_w_ple_proj):
    given = dict(locals())
    T, D = x.shape[1], x.shape[2]
    NL = norm_ffn1.shape[0]
    x0 = x.reshape(T, D)
    tgt = loss_target.reshape(T, D)
    n_a, n_b, n_kv = _qkv_layout(D)

    assert NL == 2
    slot = [{name: _cast_into_slot(given[name], l) for name in BIG} for l in range(NL)]
    ag, token = {}, None
    for tag, names in GATHER_L0:
        ag[tag] = _gather_start(tag, [slot[0][n] for n in names], token)
        token = ag[tag][3]
    ag_1 = _gather_start("1", [slot[1][n] for n in BIG], token)

    def arrived(tag, after):
        return dict(zip(dict(GATHER_L0)[tag], _gather_finish(tag, ag[tag], after)))

    def by_rows(a):
        return a.reshape(-1, a.shape[-1])

    def by_cols(a):
        return a.transpose(1, 0, 2).reshape(a.shape[1], -1)

    QW = N_CHIPS * w_qkv.shape[2]

    dils = tuple(d for _, d in DILATED_CONFIGS)
    cfg_a = [(w // (2 * d), d) for w, d in DILATED_CONFIGS]
    pad_a, pad_b = _window_pad(cfg_a[0][0]), _window_pad(SWA_RADIUS)
    bmaps_a, bmaps_b = jnp.asarray(_bucket_maps(cfg_a)), jnp.asarray(_bucket_maps([(SWA_RADIUS, 1)]))
    n_heads = rel_bias.shape[1] // 2
    bias_a = _edge_variants(_bias_build(rel_bias, bmaps_a, 0), pad_a)
    bias_b = _edge_variants(_bias_build(rel_bias, bmaps_b, n_heads), pad_b)
    no_sink = jnp.full((n_heads,), NEG, F32)

    def gains_row(l):
        ones = jnp.ones((n_a * LANES,), F32)
        return jnp.concatenate([
            jnp.tile(q_norm_a[l], 2 * n_a), jnp.tile(k_norm_a[l], 2 * n_a), ones,
            jnp.tile(q_norm_b[l], 2 * n_b), jnp.tile(k_norm_b[l], 2 * n_kv), jnp.ones((n_kv * LANES,), F32)]).reshape(1, QW)

    saved, weights = [], []
    xc = x0
    pf_1 = None
    for l in range(NL):
        s, w = {}, {}
        if l == 0:
            w.update(arrived("a", ag_1[3]))

            def ffn1_w_out(act, w=w):
                w.update(arrived("b", act))
                w["ffn1_w_out"] = by_rows(w["ffn1_w_out"])
                return w["ffn1_w_out"]
        else:
            shapes = [a.shape for a in pf_1[2]]
            w.update(zip(BIG, _split_wait("ag_pf_wait_1", pf_1[0], pf_1[1], pf_1[2], _forward_copies(shapes), xc)))
            ffn1_w_out = w["ffn1_w_out"] = by_rows(w["ffn1_w_out"])
        xc, s["ffn1"] = _ffn_forward(xc, norm_ffn1[l:l + 1], w["ffn1_w_in"], ffn1_w_out)
        s["x1"] = xc
        if l == 0:
            w.update(arrived("c", xc))
        w["w_qkv"] = by_cols(w["w_qkv"])
        w["w_o"] = by_rows(w["w_o"])
        s["h2"], s["raw"], s["qkv_a"], s["qkv_b"] = _qkv_proj(xc, norm_mix[l:l + 1], w["w_qkv"], gains_row(l))
        s["o_a"], s["lse_a"] = _attn_fwd(s["qkv_a"], bias_a, no_sink, dils, pad_a)
        s["o_b"], s["lse_b"] = _attn_fwd(s["qkv_b"], bias_b, sink_b[l], (1,), pad_b)
        xc = _mm_res("attn_out", xc, [s["o_a"], s["o_b"]], w["w_o"], 1.0)
        if l == 0:
            w.update(arrived("d", xc))
        w["w_ple_proj"] = by_cols(w["w_ple_proj"])
        for name in ("ffn2_w_out", "w_ple_gate"):
            w[name] = by_rows(w[name])
        xc, s["ffn2"] = _ffn_forward(xc, norm_ffn2[l:l + 1], w["ffn2_w_in"], w["ffn2_w_out"])
        s["x3"] = xc
        if l == 0:
            landed = _split_wait("ag_wait_1", ag_1[0], ag_1[1], ag_1[2], _gather_copies([a.shape for a in ag_1[2]]), xc)
            pf_1 = _split_start("ag_pf_start_1", landed, _forward_copies([a.shape for a in landed]), 3 * len(landed),
                                None)
        s["p"] = p[l].reshape(T, -1)
        s["hn"], xc, s["gate"], s["pp"] = _ple_fwd(xc, norm_ple[l:l + 1], s["p"], w["w_ple_gate"], w["w_ple_proj"],
                                                   pf_1[3] if l == 0 else xc)
        saved.append(s)
        weights.append(w)

    dx, loss_blk = _loss_fwd_bwd(xc, tgt)
    loss = lax.psum(loss_blk[0, 0], ("x", "y", "c"))

    gs = {name: [None] * NL for name in SMALL if name != "rel_bias"}
    dt_a, dt_b = [], []

    def layer_backward(l, dx, hooks):
        def at(point, ready, *more):
            return hooks[point](ready, *more) if point in hooks else ready

        s, w, gw = saved[l], weights[l], {}
        dx, gs["norm_ple"][l], dwg, dwp = _ple_bwd(dx, s["gate"], s["pp"], s["hn"], s["p"], s["x3"],
                                                   norm_ple[l:l + 1], w["w_ple_gate"], at("start", dx))
        gw["w_ple_gate"] = dwg.reshape(N_CHIPS, -1, D)
        gw["w_ple_proj"] = dwp.reshape(dwp.shape[0], N_CHIPS, -1).transpose(1, 0, 2)
        dx, gs["norm_ffn2"][l], gw["ffn2_w_in"], gw["ffn2_w_out"] = _ffn_backward(
            dx, s["ffn2"], norm_ffn2[l:l + 1], w["ffn2_w_in"], w["ffn2_w_out"], at("after_ple", dx))
        do, dd, dwo = _mix_bwd_in(dx, w["w_o"], [s["o_a"], s["o_b"]])
        hwa = s["o_a"].shape[1]
        gw["w_o"] = dwo.reshape(N_CHIPS, -1, D)
        dqa, dka, dva, dt, _ = _attn_bwd(s["qkv_a"], bias_a, no_sink, dils, pad_a, do, s["lse_a"], dd, 0, do)
        dt_a.append(dt)
        dqb, dkb, dvb, dt, dsink = _attn_bwd(s["qkv_b"], bias_b, sink_b[l], (1,), pad_b, do, s["lse_b"], dd,
                                             hwa // LANES, at("after_attn_a", dqa))
        dt_b.append(dt)
        gs["sink_b"][l] = dsink[:, 0, ::HEAD_DIM].reshape(-1)
        dx, gs["norm_mix"][l], dgains, dwqkv = _qkv_bwd(
            s["raw"], gains_row(l), (dqa, dka, dva), (dqb, dkb, dvb), w["w_qkv"], s["h2"], s["x1"],
            norm_mix[l:l + 1], dx, at("before_qkv_bwd", dqb, [gw[n] for n in GROUP_MID]))
        dgv = dgains.reshape(-1, HEAD_DIM)
        gs["q_norm_a"][l] = dgv[:2 * n_a].sum(0)
        gs["k_norm_a"][l] = dgv[2 * n_a:4 * n_a].sum(0)
        gs["q_norm_b"][l] = dgv[6 * n_a:6 * n_a + 2 * n_b].sum(0)
        gs["k_norm_b"][l] = dgv[6 * n_a + 2 * n_b:6 * n_a + 2 * n_b + 2 * n_kv].sum(0)
        gw["w_qkv"] = dwqkv.reshape(D, N_CHIPS, -1).transpose(1, 0, 2)
        dx, gs["norm_ffn1"][l], gw["ffn1_w_in"], gw["ffn1_w_out"] = _ffn_backward(
            dx, s["ffn1"], norm_ffn1[l:l + 1], w["ffn1_w_in"], w["ffn1_w_out"], at("before_ffn1", dx))
        return dx, gw

    out = {}

    def adamw_group(names, l, grads, after):
        for name, g_ in zip(names, grads):
            out[name] = _adamw_layer(given[name], g_, given["m_" + name], given["v_" + name], l, out.get(name), after)
            after = out[name][0]
        return after

    dx, gw1 = layer_backward(NL - 1, dx, {})
    px_1 = _scatter_exchange("1", [gw1[n] for n in BIG])
    rs = {}

    def chips_1(ready):
        rs["chips_1"] = _scatter_chips("1", px_1, ready)
        return rs["chips_1"][3]

    def share_1(ready):
        rs["share_1"] = _scatter_share("1", rs["chips_1"], ready)
        return rs["share_1"][3]

    def exchange_0a(ready, grads):
        rs["px_0a"] = _scatter_exchange("0a", grads)
        return rs["px_0a"][3]

    def chips_0a(ready):
        rs["chips_0a"] = _scatter_chips("0a", rs["px_0a"], ready)
        return rs["chips_0a"][3]

    dx, gw0 = layer_backward(0, dx, {"start": lambda ready: px_1[3], "after_ple": chips_1, "after_attn_a": share_1,
                                     "before_qkv_bwd": exchange_0a, "before_ffn1": chips_0a})
    grad_x = dx.reshape(x.shape)

    px_0b = _scatter_exchange("0b", [gw0[n] for n in GROUP_LAST])
    d_rel_bias = (_bias_grad(dt_a, bmaps_a, 0) + _bias_grad(dt_b, bmaps_b, n_heads))[:, :rel_bias.shape[1]]
    small_g = [d_rel_bias] + [jnp.stack([t.reshape(-1) for t in gs[name]]) for name in SMALL[1:]]
    g_sum = _allreduce_small(_pack_small(small_g), px_0b[3])
    res = _ew("adamw_small", _adamw_fn,
              [_pack_small([given[n] for n in SMALL]), g_sum, _pack_small([given["m_" + n] for n in SMALL]),
               _pack_small([given["v_" + n] for n in SMALL])], [(LANES, F32)] * 4)
    like = [given[n] for n in SMALL]
    unpacked = [_unpack_small(r, like) for r in res]
    for i, name in enumerate(SMALL):
        out[name] = [u[i] for u in unpacked]

    share_0a = _scatter_share("0a", rs["chips_0a"], res[0])
    g_1 = _scatter_done("1", rs["share_1"], share_0a[3])
    chips_0b = _scatter_chips("0b", px_0b, g_1[0])
    ready = adamw_group(BIG, 1, g_1, chips_0b[3])
    ready = adamw_group(GROUP_MID, 0, _scatter_done("0a", share_0a, ready), ready)
    share_0b = _scatter_share("0b", chips_0b, ready)
    adamw_group(GROUP_LAST, 0, _scatter_done("0b", share_0b, share_0b[3]), share_0b[3])

    return (loss, grad_x, *[out[n][0] for n in WEIGHTS], *[out[n][1] for n in WEIGHTS],
            *[out[n][2] for n in WEIGHTS], *[out[n][3] for n in WEIGHTS])
```

```python
import functools
import math

import numpy as np
import jax
import jax.numpy as jnp
from jax import lax
from jax.experimental import pallas as pl
from jax.experimental.pallas import tpu as pltpu

F32 = jnp.float32
BF16 = jnp.bfloat16
MESH = pl.DeviceIdType.MESH

HEAD_DIM = 64
LANES = 128
QBLOCK = 128
FWD_UNROLL, BWD_UNROLL = 8, 8
QKV_CHUNK = 6
N_BUCKETS = 32
MAX_DISTANCE = 1024
DILATED_CONFIGS = ((128, 1), (512, 4), (2048, 16))
SWA_RADIUS = 128
GROUP_B = 4
EPS = 1e-6
NEG = -1e30
Q_SCALE = HEAD_DIM ** -0.5
ADAM_LR, ADAM_B1, ADAM_B2, ADAM_EPS, ADAM_WD, ADAM_STEP = 0.001, 0.9, 0.999, 1e-08, 0.01, 10
VMEM_LIMIT = 56 * 2 ** 20
N_CHIPS = 4
N_DEV = 8

BIG = ("ffn1_w_in", "ffn1_w_out", "w_qkv", "w_o", "ffn2_w_in", "ffn2_w_out", "w_ple_gate", "w_ple_proj")
SMALL = ("rel_bias", "norm_ffn1", "norm_mix", "q_norm_a", "k_norm_a", "q_norm_b", "k_norm_b", "sink_b",
         "norm_ffn2", "norm_ple")
WEIGHTS = ("rel_bias", "norm_ffn1", "ffn1_w_in", "ffn1_w_out", "norm_mix", "w_qkv", "q_norm_a", "k_norm_a",
           "q_norm_b", "k_norm_b", "sink_b", "w_o", "norm_ffn2", "ffn2_w_in", "ffn2_w_out", "norm_ple",
           "w_ple_gate", "w_ple_proj")


HBM_SPEC = pl.BlockSpec(memory_space=pltpu.HBM)
ANY_SPEC = pl.BlockSpec(memory_space=pl.ANY)
SEM_SPEC = pl.BlockSpec(memory_space=pltpu.SEMAPHORE)


def _params(n_grid):
    return pltpu.CompilerParams(dimension_semantics=("arbitrary",) * n_grid, vmem_limit_bytes=VMEM_LIMIT)


def _pick_tm(rows, cap):
    t = (min(cap, rows) // 16) * 16
    while t >= 16:
        if rows % t == 0:
            return t
        t -= 16
    return rows


def _dot(a, b):
    return jnp.dot(a, b, preferred_element_type=F32)


def _dot_nt(a, b):
    return lax.dot_general(a, b, (((1,), (1,)), ((), ())), preferred_element_type=F32)


def _dot_tn(a, b):
    return lax.dot_general(a, b, (((0,), (0,)), ((), ())), preferred_element_type=F32)


def _sigmoid(z):
    return 1.0 / (1.0 + jnp.exp(-z))


def _lo_lanes(shape):
    return lax.broadcasted_iota(jnp.int32, shape, len(shape) - 1) % LANES < HEAD_DIM


def _seg_sum(blk):
    lo = _lo_lanes(blk.shape)
    s_lo = jnp.sum(jnp.where(lo, blk, 0.0), axis=1, keepdims=True)
    s_hi = jnp.sum(jnp.where(lo, 0.0, blk), axis=1, keepdims=True)
    return jnp.where(lo, s_lo, s_hi)


def _rms_bwd_tile(x, g, dh):
    r = lax.rsqrt(jnp.mean(x * x, axis=-1, keepdims=True) + EPS)
    xh = x * r
    dyg = dh * g
    dx = r * (dyg - xh * jnp.mean(dyg * xh, axis=-1, keepdims=True))
    return dx, jnp.sum(dh * xh, axis=0, keepdims=True)


def _ew(name, fn, ins, out_defs, cap=512):
    rows = ins[0].shape[0]
    tm = _pick_tm(rows, cap)
    n_in = len(ins)

    def body(*refs):
        vals = fn(*[r[...] for r in refs[:n_in]])
        if not isinstance(vals, tuple):
            vals = (vals,)
        for r, v in zip(refs[n_in:], vals):
            r[...] = v.astype(r.dtype)

    return pl.pallas_call(
        body, name=name, grid=(rows // tm,),
        in_specs=[pl.BlockSpec((tm, a.shape[1]), lambda i: (i, 0)) for a in ins],
        out_specs=[pl.BlockSpec((tm, c), lambda i: (i, 0)) for c, _ in out_defs],
        out_shape=[jax.ShapeDtypeStruct((rows, c), dt) for c, dt in out_defs],
        compiler_params=_params(1))(*ins)


def _rms_tile(xv, gv):
    r = lax.rsqrt(jnp.mean(xv * xv, axis=-1, keepdims=True) + EPS)
    return (xv * r * gv).astype(BF16)


def _ffn_up(x, g, win):
    T, D = x.shape
    wc = win.shape[2]
    tm = _pick_tm(T, 512)

    def body(x_ref, g_ref, wg_ref, wu_ref, h_ref, silu_ref, dgate_ref, act_ref, wcat):
        @pl.when(pl.program_id(1) == 0)
        def _():
            wcat[:, :wc] = wg_ref[...]
            wcat[:, wc:] = wu_ref[...]

        hv = _rms_tile(x_ref[...], g_ref[...])
        h_ref[...] = hv
        gu = _dot(hv, wcat[...])
        gte, u = gu[:, :wc], gu[:, wc:]
        sg = _sigmoid(gte)
        silu = gte * sg
        silu_ref[...] = silu.astype(BF16)
        dgate_ref[...] = ((sg + silu * (1.0 - sg)) * u).astype(BF16)
        act_ref[...] = (silu * u).astype(BF16)

    out = jax.ShapeDtypeStruct((T, 2 * wc), BF16)
    ospec = pl.BlockSpec((tm, wc), lambda j, i: (i, j))
    nt = T // tm
    h_spec = pl.BlockSpec((tm, D), lambda j, i: (jnp.where(j == 0, i, nt), 0))
    return pl.pallas_call(
        body, name="ffn_up", grid=(2, nt),
        in_specs=[pl.BlockSpec((tm, D), lambda j, i: (i, 0)), pl.BlockSpec((1, D), lambda j, i: (0, 0)),
                  pl.BlockSpec((None, D, wc), lambda j, i: (j, 0, 0)),
                  pl.BlockSpec((None, D, wc), lambda j, i: (j + 2, 0, 0))],
        out_specs=[h_spec] + [ospec] * 3, out_shape=[jax.ShapeDtypeStruct((T + tm, D), BF16)] + [out] * 3,
        scratch_shapes=[pltpu.VMEM((D, 2 * wc), BF16)], compiler_params=_params(2))(x, g, win, win)


def _mm_res(name, res, a_list, w, scale, after):
    T, N = res.shape
    n = len(a_list)
    widths = [a.shape[1] for a in a_list]
    tm = _pick_tm(T, 512)

    def body(*refs):
        r_ref, a_refs, w_refs, o_ref = refs[0], refs[1:1 + n], refs[1 + n:1 + 2 * n], refs[2 + 2 * n]
        acc = _dot(a_refs[0][...].astype(BF16), w_refs[0][...])
        for a_ref, w_ref in zip(a_refs[1:], w_refs[1:]):
            acc = acc + _dot(a_ref[...].astype(BF16), w_ref[...])
        o_ref[...] = r_ref[...] + scale * acc

    w_specs, off = [], 0
    for k in widths:
        w_specs.append(pl.BlockSpec((k, N), lambda i, blk=off // k: (blk, 0)))
        off += k
    return pl.pallas_call(
        body, name=name, grid=(T // tm,),
        in_specs=[pl.BlockSpec((tm, N), lambda i: (i, 0))]
        + [pl.BlockSpec((tm, k), lambda i: (i, 0)) for k in widths] + w_specs + [ANY_SPEC],
        out_specs=pl.BlockSpec((tm, N), lambda i: (i, 0)),
        out_shape=jax.ShapeDtypeStruct((T, N), F32),
        compiler_params=_params(1))(res, *a_list, *([w] * n), after)


def _mm_tn(name, a, a_w, b, b_w, n_slots, a_by_slot, b_by_slot, scale, tm_cap=512):
    T = b.shape[0]
    tm = _pick_tm(T, tm_cap)
    nt = T // tm

    def body(a_ref, b_ref, o_ref, acc):
        i = pl.program_id(1)

        @pl.when(i == 0)
        def _():
            acc[...] = jnp.zeros_like(acc)

        acc[...] += _dot_tn(a_ref[...].astype(BF16), b_ref[...].astype(BF16))

        @pl.when(i == nt - 1)
        def _():
            o_ref[...] = (acc[...] * scale).astype(BF16)

    return pl.pallas_call(
        body, name=name, grid=(n_slots, nt),
        in_specs=[pl.BlockSpec((tm, a_w), (lambda s, i: (i, s)) if a_by_slot else (lambda s, i: (i, 0))),
                  pl.BlockSpec((tm, b_w), (lambda s, i: (i, s)) if b_by_slot else (lambda s, i: (i, 0)))],
        out_specs=pl.BlockSpec((None, a_w, b_w), lambda s, i: (s, 0, 0)),
        out_shape=jax.ShapeDtypeStruct((n_slots, a_w, b_w), BF16),
        scratch_shapes=[pltpu.VMEM((a_w, b_w), F32)], compiler_params=_params(2))(a, b)


def _mm_tn_pairs(name, a, b, b_w, n_slots):
    T = b.shape[0]
    a_w = a.shape[1]
    tm = _pick_tm(T, 512)
    nt = T // tm

    def body(a_ref, b_ref, o_ref, acc):
        i = pl.program_id(1)

        @pl.when(i == 0)
        def _():
            acc[...] = jnp.zeros_like(acc)

        acc[...] += _dot_tn(a_ref[...], b_ref[...])

        @pl.when(i == nt - 1)
        def _():
            o_ref[0] = acc[:, :b_w].astype(BF16)
            o_ref[1] = acc[:, b_w:].astype(BF16)

    return pl.pallas_call(
        body, name=name, grid=(n_slots // 2, nt),
        in_specs=[pl.BlockSpec((tm, a_w), lambda s, i: (i, 0)), pl.BlockSpec((tm, 2 * b_w), lambda s, i: (i, s))],
        out_specs=pl.BlockSpec((2, a_w, b_w), lambda s, i: (s, 0, 0)),
        out_shape=jax.ShapeDtypeStruct((n_slots, a_w, b_w), BF16),
        scratch_shapes=[pltpu.VMEM((a_w, 2 * b_w), F32)], compiler_params=_params(2))(a, b)


def _ffn_bwd(dx, wout, silu, dgate, win, x, g, after):
    T, D = dx.shape
    F = silu.shape[1]
    n_slots, _, wc = win.shape
    tm = _pick_tm(T, 256)

    def body(dx_ref, wout_hbm, s_ref, dgt_ref, win_hbm, x_ref, g_ref, after_ref,
             dgu_ref, dxo_ref, dg_ref, wout_v, wcat, sem):
        @pl.when(pl.program_id(0) == 0)
        def _():
            copies = [pltpu.make_async_copy(wout_hbm, wout_v, sem.at[n_slots])]
            copies += [pltpu.make_async_copy(win_hbm.at[s], wcat.at[s // 2, :, pl.ds((s % 2) * wc, wc)], sem.at[s])
                       for s in range(n_slots)]
            for cp in copies:
                cp.start()
            for cp in copies:
                cp.wait()
            dg_ref[...] = jnp.zeros_like(dg_ref)

        dxv = dx_ref[...]
        dact = 0.5 * _dot_nt(dxv.astype(BF16), wout_v[...])
        d_gate = (dact * dgt_ref[...].astype(F32)).astype(BF16)
        d_up = (dact * s_ref[...].astype(F32)).astype(BF16)
        dgu_ref[:, :F] = d_gate
        dgu_ref[:, F:] = d_up
        dh = _dot_nt(d_gate, wcat[0]) + _dot_nt(d_up, wcat[1])
        dxn, dg = _rms_bwd_tile(x_ref[...], g_ref[...], dh)
        dxo_ref[...] = dxv + dxn
        dg_ref[...] += dg

    row = pl.BlockSpec((tm, D), lambda i: (i, 0))
    vec = pl.BlockSpec((1, D), lambda i: (0, 0))
    act_spec = pl.BlockSpec((tm, F), lambda i: (i, 0))
    return pl.pallas_call(
        body, name="ffn_bwd", grid=(T // tm,),
        in_specs=[row, ANY_SPEC, act_spec, act_spec, ANY_SPEC, row, vec, ANY_SPEC],
        out_specs=[pl.BlockSpec((tm, 2 * F), lambda i: (i, 0)), row, vec],
        out_shape=[jax.ShapeDtypeStruct((T, 2 * F), BF16), jax.ShapeDtypeStruct((T, D), F32),
                   jax.ShapeDtypeStruct((1, D), F32)],
        scratch_shapes=[pltpu.VMEM((F, D), BF16), pltpu.VMEM((n_slots // 2, D, 2 * wc), BF16),
                        pltpu.SemaphoreType.DMA((n_slots + 1,))],
        compiler_params=_params(1))(dx, wout, silu, dgate, win, x, g, after)


def _ple_fwd(x, g, p, wg, wp, after):
    T, D = x.shape
    P = p.shape[1]
    tm = _pick_tm(T, 256)

    def body(x_ref, g_ref, p_ref, wg_ref, wp_ref, after_ref, hn_ref, xo_ref, gate_ref, pp_ref):
        xv = x_ref[...]
        hn = _rms_tile(xv, g_ref[...])
        hn_ref[...] = hn
        gate = _sigmoid(_dot(hn, wg_ref[...]))
        pp = _dot(p_ref[...].astype(BF16), wp_ref[...])
        gate_ref[...] = gate.astype(BF16)
        pp_ref[...] = pp.astype(BF16)
        xo_ref[...] = xv + gate * pp

    row = pl.BlockSpec((tm, D), lambda i: (i, 0))
    out = jax.ShapeDtypeStruct((T, D), F32)
    half = jax.ShapeDtypeStruct((T, D), BF16)
    return pl.pallas_call(
        body, name="ple_fwd", grid=(T // tm,),
        in_specs=[row, pl.BlockSpec((1, D), lambda i: (0, 0)), pl.BlockSpec((tm, P), lambda i: (i, 0)),
                  pl.BlockSpec((D, D), lambda i: (0, 0)), pl.BlockSpec((P, D), lambda i: (0, 0)), ANY_SPEC],
        out_specs=[row, row, row, row], out_shape=[half, out, half, half],
        compiler_params=_params(1))(x, g, p, wg, wp, after)


def _ple_bwd(dx, gate, pp, hn, p, x, g, wg, after):
    T, D = x.shape
    P = p.shape[1]
    tm = _pick_tm(T, 256)
    nt = T // tm

    def body(dx_ref, gate_ref, pp_ref, hn_ref, p_ref, x_ref, g_ref, wg_ref, after_ref,
             dxo_ref, dg_ref, dwg_ref, dwp_ref, acc_g, acc_p):
        i = pl.program_id(0)

        @pl.when(i == 0)
        def _():
            acc_g[...] = jnp.zeros_like(acc_g)
            acc_p[...] = jnp.zeros_like(acc_p)
            dg_ref[...] = jnp.zeros_like(dg_ref)

        dxv = dx_ref[...]
        gate = gate_ref[...].astype(F32)
        dz = (dxv * pp_ref[...].astype(F32) * gate * (1.0 - gate)).astype(BF16)
        dpp = (dxv * gate).astype(BF16)
        acc_g[...] += _dot_tn(hn_ref[...], dz)
        acc_p[...] += _dot_tn(p_ref[...].astype(BF16), dpp)
        dxn, dg = _rms_bwd_tile(x_ref[...], g_ref[...], _dot_nt(dz, wg_ref[...]))
        dxo_ref[...] = dxv + dxn
        dg_ref[...] += dg

        @pl.when(i == nt - 1)
        def _():
            dwg_ref[...] = acc_g[...].astype(BF16)
            dwp_ref[...] = acc_p[...].astype(BF16)

    row = pl.BlockSpec((tm, D), lambda i: (i, 0))
    vec = pl.BlockSpec((1, D), lambda i: (0, 0))
    return pl.pallas_call(
        body, name="ple_bwd", grid=(nt,),
        in_specs=[row, row, row, row, pl.BlockSpec((tm, P), lambda i: (i, 0)), row, vec,
                  pl.BlockSpec((D, D), lambda i: (0, 0)), ANY_SPEC],
        out_specs=[row, vec, pl.BlockSpec((D, D), lambda i: (0, 0)), pl.BlockSpec((P, D), lambda i: (0, 0))],
        out_shape=[jax.ShapeDtypeStruct((T, D), F32), jax.ShapeDtypeStruct((1, D), F32),
                   jax.ShapeDtypeStruct((D, D), BF16), jax.ShapeDtypeStruct((P, D), BF16)],
        scratch_shapes=[pltpu.VMEM((D, D), F32), pltpu.VMEM((P, D), F32)],
        compiler_params=_params(1))(dx, gate, pp, hn, p, x, g, wg, after)


def _loss_fwd_bwd(y, tgt):
    T, D = y.shape
    tm = _pick_tm(T, 512)

    def body(y_ref, t_ref, dy_ref, loss_ref):
        e = y_ref[...] - t_ref[...]
        dy_ref[...] = e / D

        @pl.when(pl.program_id(0) == 0)
        def _():
            loss_ref[...] = jnp.zeros_like(loss_ref)

        loss_ref[...] += 0.5 * jnp.sum(jnp.mean(e * e, axis=-1, keepdims=True), axis=0, keepdims=True)

    row = pl.BlockSpec((tm, D), lambda i: (i, 0))
    return pl.pallas_call(
        body, name="loss", grid=(T // tm,), in_specs=[row, row],
        out_specs=[row, pl.BlockSpec((8, LANES), lambda i: (0, 0))],
        out_shape=[jax.ShapeDtypeStruct((T, D), F32), jax.ShapeDtypeStruct((8, LANES), F32)],
        compiler_params=_params(1))(y, tgt)


def _qkv_layout(D):
    n_a = D // (2 * LANES)
    n_b = D // (2 * LANES)
    n_kv = max(1, (2 * n_b) // GROUP_B) * HEAD_DIM // LANES
    return n_a, n_b, n_kv


def _dup_half(xv, half):
    rolled = pltpu.roll(xv, HEAD_DIM, 1)
    lo = _lo_lanes(xv.shape)
    return jnp.where(lo, xv, rolled) if half == 0 else jnp.where(lo, rolled, xv)


def _qkv_proj(x, g_mix, w, gains):
    T, D = x.shape
    W = w.shape[1]
    n_a, n_b, n_kv = _qkv_layout(D)
    tm = _pick_tm(T, 256)
    o_qb = 3 * n_a

    def norm(xv, gv, scale):
        ms = _seg_sum(xv * xv) * (1.0 / HEAD_DIM)
        return xv * lax.rsqrt(ms + EPS) * gv * scale

    def body(x_ref, gm_ref, w_ref, g_ref, h_ref, raw_ref, a_ref, b_ref):
        hv = _rms_tile(x_ref[...], gm_ref[...])
        h_ref[...] = hv
        raw = _dot(hv, w_ref[...])
        raw_ref[...] = raw

        def blk(cb):
            return raw[:, cb * LANES:(cb + 1) * LANES]

        def gn(cb):
            return g_ref[:, cb * LANES:(cb + 1) * LANES]

        for cb in range(n_a):
            a_ref[:, cb * LANES:(cb + 1) * LANES] = norm(blk(cb), gn(cb), Q_SCALE)
            cbk = n_a + cb
            a_ref[:, cbk * LANES:(cbk + 1) * LANES] = norm(blk(cbk), gn(cbk), 1.0)
            cbv = 2 * n_a + cb
            a_ref[:, cbv * LANES:(cbv + 1) * LANES] = blk(cbv)
        for cb in range(n_b):
            src = o_qb + cb
            b_ref[:, cb * LANES:(cb + 1) * LANES] = norm(blk(src), gn(src), Q_SCALE)
        for e in range(n_b):
            kvh = (2 * e) // GROUP_B
            ck = o_qb + n_b + kvh // 2
            cv = ck + n_kv
            kn = norm(blk(ck), gn(ck), 1.0)
            b_ref[:, (n_b + e) * LANES:(n_b + e + 1) * LANES] = _dup_half(kn, kvh % 2)
            b_ref[:, (2 * n_b + e) * LANES:(2 * n_b + e + 1) * LANES] = _dup_half(blk(cv), kvh % 2)

    wa, wb = 3 * n_a * LANES, 3 * n_b * LANES

    def rows(width):
        return pl.BlockSpec((tm, width), lambda i: (i, 0))

    return pl.pallas_call(
        body, name="qkv_proj", grid=(T // tm,),
        in_specs=[rows(D), pl.BlockSpec((1, D), lambda i: (0, 0)), pl.BlockSpec((D, W), lambda i: (0, 0)),
                  pl.BlockSpec((1, W), lambda i: (0, 0))],
        out_specs=[rows(D), rows(W), rows(wa), rows(wb)],
        out_shape=[jax.ShapeDtypeStruct((T, D), BF16), jax.ShapeDtypeStruct((T, W), F32),
                   jax.ShapeDtypeStruct((T, wa), F32), jax.ShapeDtypeStruct((T, wb), F32)],
        compiler_params=_params(1))(x, g_mix, w, gains)


def _qkv_bwd(raw, gains, d_a, d_b, w, h, x, g_mix, dx_in, after):
    T, W = raw.shape
    D = x.shape[1]
    n_a, n_b, n_kv = _qkv_layout(D)
    tm = _pick_tm(T, 256)
    nt = T // tm
    o_qb = 3 * n_a

    def body(raw_ref, g_ref, daq, dak, dav, dbq, dbk, dbv, w_ref, h_ref, x_ref, gm_ref, dxi_ref, after_ref,
             dx_ref, dgm_ref, dg_ref, dw_ref, acc):
        i = pl.program_id(0)

        @pl.when(i == 0)
        def _():
            dg_ref[...] = jnp.zeros_like(dg_ref)
            dgm_ref[...] = jnp.zeros_like(dgm_ref)
            acc[...] = jnp.zeros_like(acc)

        def cols(ref, cb):
            return ref[:, cb * LANES:(cb + 1) * LANES]

        pieces = [None] * (W // LANES)

        def norm_bwd(cb, dy, scale):
            xv = cols(raw_ref, cb)
            gv = cols(g_ref, cb)
            r = lax.rsqrt(_seg_sum(xv * xv) * (1.0 / HEAD_DIM) + EPS)
            xh = xv * r
            dys = dy * scale
            dyg = dys * gv
            dxv = r * (dyg - xh * (_seg_sum(dyg * xh) * (1.0 / HEAD_DIM)))
            pieces[cb] = dxv.astype(BF16)
            dg_ref[:, cb * LANES:(cb + 1) * LANES] += jnp.sum(dys * xh, axis=0, keepdims=True)

        def fold(ref, kv_blk):
            halves = []
            for half in range(2):
                kvh = 2 * kv_blk + half
                blocks = [e for e in range(n_b) if (2 * e) // GROUP_B == kvh]
                s = cols(ref, blocks[0])
                for e in blocks[1:]:
                    s = s + cols(ref, e)
                halves.append(s + pltpu.roll(s, HEAD_DIM, 1))
            return jnp.where(_lo_lanes(halves[0].shape), halves[0], halves[1])

        for cb in range(n_a):
            norm_bwd(cb, cols(daq, cb), Q_SCALE)
        for cb in range(n_a):
            norm_bwd(n_a + cb, cols(dak, cb), 1.0)
        for cb in range(n_a):
            pieces[2 * n_a + cb] = cols(dav, cb).astype(BF16)
        for cb in range(n_b):
            norm_bwd(o_qb + cb, cols(dbq, cb), Q_SCALE)
        for kb in range(n_kv):
            norm_bwd(o_qb + n_b + kb, fold(dbk, kb), 1.0)
        for kb in range(n_kv):
            pieces[o_qb + n_b + n_kv + kb] = fold(dbv, kb).astype(BF16)

        hv = h_ref[...]
        dh = None
        for c0 in range(0, W // LANES, QKV_CHUNK):
            chunk = jnp.concatenate(pieces[c0:c0 + QKV_CHUNK], axis=1)
            lanes = slice(c0 * LANES, (c0 + QKV_CHUNK) * LANES)
            part = _dot_nt(chunk, w_ref[:, lanes])
            dh = part if dh is None else dh + part
            acc[:, lanes] += _dot_tn(hv, chunk)
        dxn, dgm = _rms_bwd_tile(x_ref[...], gm_ref[...], dh)
        dx_ref[...] = dxi_ref[...] + dxn
        dgm_ref[...] += dgm

        @pl.when(i == nt - 1)
        def _():
            dw_ref[...] = acc[...].astype(BF16)

    hw_a, hw_b = n_a * LANES, n_b * LANES

    def rows(width):
        return pl.BlockSpec((tm, width), lambda i: (i, 0))

    def fixed(r, c):
        return pl.BlockSpec((r, c), lambda i: (0, 0))

    return pl.pallas_call(
        body, name="qkv_bwd", grid=(nt,),
        in_specs=[rows(W), fixed(1, W)] + [rows(hw_a)] * 3 + [rows(hw_b)] * 3
        + [fixed(D, W), rows(D), rows(D), fixed(1, D), rows(D), ANY_SPEC],
        out_specs=[rows(D), fixed(1, D), fixed(1, W), fixed(D, W)],
        out_shape=[jax.ShapeDtypeStruct((T, D), F32), jax.ShapeDtypeStruct((1, D), F32),
                   jax.ShapeDtypeStruct((1, W), F32), jax.ShapeDtypeStruct((D, W), BF16)],
        scratch_shapes=[pltpu.VMEM((D, W), F32)],
        compiler_params=_params(1))(raw, gains, *d_a, *d_b, w, h, x, g_mix, dx_in, after)


def _t5_bucket_np(rel):
    half = N_BUCKETS // 2
    max_exact = half // 2
    ret = np.where(rel > 0, half, 0)
    n = np.abs(rel)
    nf = np.maximum(n, 1).astype(np.float32)
    large = max_exact + (np.log(nf / np.float32(max_exact)) / np.float32(math.log(MAX_DISTANCE / max_exact))
                         * np.float32(half - max_exact)).astype(np.int32)
    large = np.minimum(large, half - 1)
    return ret + np.where(n < max_exact, n, large)


def _window_pad(radius):
    assert radius <= QBLOCK
    return HEAD_DIM if radius <= HEAD_DIM else QBLOCK


def _bucket_maps(configs):
    pad = _window_pad(configs[0][0])
    q = np.arange(QBLOCK)[:, None]
    kk = np.arange(QBLOCK + 2 * pad)[None, :]
    rel = kk - pad - q
    maps = [np.where(np.abs(rel) <= radius, _t5_bucket_np(rel * dil), -1) for radius, dil in configs]
    return np.stack(maps).astype(np.int32)


def _bias_build(rel_bias, bmaps, col0):
    n_sets, _, W = bmaps.shape
    n_buckets, n_cols = rel_bias.shape
    n_heads = n_cols // 2

    def body(rb_ref, bm_ref, o_ref):
        h = pl.program_id(1)
        bm = bm_ref[0:1, :]
        head = lax.broadcasted_iota(jnp.int32, (n_buckets, n_cols), 1) == col0 + h
        per_bucket = jnp.sum(jnp.where(head, rb_ref[...], 0.0), axis=1, keepdims=True)
        hit = lax.broadcasted_iota(jnp.int32, (n_buckets, W), 0) == bm
        row0 = jnp.sum(jnp.where(hit, per_bucket, 0.0), axis=0, keepdims=True) + jnp.where(bm < 0, NEG, 0.0)
        o_ref[...] = pltpu.roll(jnp.broadcast_to(row0, (QBLOCK, W)), 0, 1, stride=1, stride_axis=0)

    return pl.pallas_call(
        body, name="bias_build", grid=(n_sets, n_heads),
        in_specs=[pl.BlockSpec((n_buckets, n_cols), lambda s, h: (0, 0)),
                  pl.BlockSpec((None, QBLOCK, W), lambda s, h: (s, 0, 0))],
        out_specs=pl.BlockSpec((None, None, QBLOCK, W), lambda s, h: (s, h, 0, 0)),
        out_shape=jax.ShapeDtypeStruct((n_sets, n_heads, QBLOCK, W), F32),
        compiler_params=_params(2))(rel_bias, bmaps)


def _bias_grad(dtiles, bmaps, col0):
    n_sets, _, W = bmaps.shape
    n_heads = dtiles[0].shape[1]
    n_l = len(dtiles)

    def body(*refs):
        bm_ref, o_ref = refs[0], refs[1 + n_l]
        s, h = pl.program_id(0), pl.program_id(1)

        @pl.when((s == 0) & (h == 0))
        def _():
            o_ref[...] = jnp.zeros_like(o_ref)

        d = refs[1][...]
        for r in refs[2:1 + n_l]:
            d = d + r[...]
        acc8 = d[0:8, :]
        for a in range(1, QBLOCK // 8):
            acc8 = acc8 + pltpu.roll(d[8 * a:8 * a + 8, :], W - 8 * a, 1)
        per_offset = acc8[0:1, :]
        for b in range(1, 8):
            per_offset = per_offset + pltpu.roll(acc8[b:b + 1, :], W - b, 1)
        bucket = lax.broadcasted_iota(jnp.int32, (N_BUCKETS, W), 0)
        hit = bucket == bm_ref[0:1, :]
        per_bucket = jnp.sum(jnp.where(hit, per_offset, 0.0), axis=1, keepdims=True)
        lanes = lax.broadcasted_iota(jnp.int32, o_ref.shape, 1)
        o_ref[...] += jnp.where(lanes == col0 + h, per_bucket, 0.0)

    tile = pl.BlockSpec((None, None, QBLOCK, W), lambda s, h: (s, h, 0, 0))
    return pl.pallas_call(
        body, name="bias_grad", grid=(n_sets, n_heads),
        in_specs=[pl.BlockSpec((None, QBLOCK, W), lambda s, h: (s, 0, 0))] + [tile] * n_l,
        out_specs=pl.BlockSpec((N_BUCKETS, LANES), lambda s, h: (0, 0)),
        out_shape=jax.ShapeDtypeStruct((N_BUCKETS, LANES), F32), compiler_params=_params(2))(bmaps, *dtiles)


def _rows(l_start, n, d, r):
    if d == 1:
        return pl.ds(pl.multiple_of(l_start, 8), n)
    return pl.ds(l_start * d + r, n, stride=d)


def _stack_heads(xv, lo):
    z = jnp.zeros_like(xv)
    return jnp.concatenate([jnp.where(lo, xv, z), jnp.where(lo, z, xv)], axis=0)


def _unstack_heads(xv, lo):
    return jnp.where(lo, xv[:QBLOCK], xv[QBLOCK:])


def _per_head_rows(v0, v1):
    if jnp.ndim(v0) == 0:
        return jnp.where(lax.broadcasted_iota(jnp.int32, (2 * QBLOCK, 1), 0) < QBLOCK, v0, v1)
    return jnp.concatenate([v0, v1], axis=0)


def _block_geometry(b, nb_sub, pad):
    r, lb = b // nb_sub, b % nb_sub
    l0 = lb * QBLOCK
    lp = jnp.maximum(l0 - pad, 0)
    ln = jnp.minimum(l0 + QBLOCK, nb_sub * QBLOCK - pad)
    return r, l0, lp, ln, (lb == 0).astype(jnp.int32) + 2 * (lb == nb_sub - 1).astype(jnp.int32)


def _edge_variants(bias, pad):
    n_br, _, _, W = bias.shape
    col = np.arange(W)
    left, right = col < pad, col >= pad + QBLOCK
    masked = jnp.asarray(np.stack([np.zeros(W, bool), left, right, left | right]))
    return jnp.where(masked[None, :, None, :], NEG, bias.reshape(n_br, 1, -1, W))


def _window(ref, l0, lp, ln, pad, d, r):
    return jnp.concatenate([ref[_rows(lp, pad, d, r), :], ref[_rows(l0, QBLOCK, d, r), :],
                            ref[_rows(ln, pad, d, r), :]], axis=0)


def _attn_fwd(qkv, bias, sink, dils, pad, after):
    T = qkv.shape[0]
    hw = qkv.shape[1] // 3
    ng = hw // LANES
    n_br = len(dils)
    n_blocks = T // QBLOCK
    W = QBLOCK + 2 * pad
    chunk = 256

    def body(sink_ref, q_ref, k_ref, v_ref, bias_ref, after_ref, o_ref, lse_ref, *scratch):
        g = pl.program_id(0)
        lo = _lo_lanes((QBLOCK, LANES))
        snk = _per_head_rows(sink_ref[2 * g], sink_ref[2 * g + 1])
        for c, d in enumerate(dils):
            nb_sub = n_blocks // d
            o_dst = scratch[0].at[c] if n_br > 1 else o_ref
            l_dst = scratch[1].at[c] if n_br > 1 else lse_ref

            def block(b, carry, c=c, d=d, nb_sub=nb_sub, o_dst=o_dst, l_dst=l_dst):
                r, l0, lp, ln, edge = _block_geometry(b, nb_sub, pad)
                q = _stack_heads(q_ref[_rows(l0, QBLOCK, d, r), :].astype(BF16), lo)
                k = _window(k_ref, l0, lp, ln, pad, d, r).astype(BF16)
                v = _window(v_ref, l0, lp, ln, pad, d, r).astype(BF16)
                s = _dot_nt(q, k) + bias_ref[c, edge]
                m = jnp.maximum(jnp.max(s, axis=1, keepdims=True), snk)
                p = jnp.exp(s - m)
                den = jnp.sum(p, axis=1, keepdims=True) + jnp.exp(snk - m)
                o_dst[_rows(l0, QBLOCK, d, r), :] = _unstack_heads(_dot(p.astype(BF16), v) / den, lo)
                l_dst[_rows(l0, QBLOCK, d, r), :] = _unstack_heads(
                    jnp.broadcast_to(m + jnp.log(den), (2 * QBLOCK, LANES)), lo)
                return carry

            lax.fori_loop(0, n_blocks, block, 0, unroll=FWD_UNROLL)

        if n_br > 1:
            def merge(i, carry):
                rs = pl.ds(pl.multiple_of(i * chunk, chunk), chunk)
                ls = [scratch[1][c, rs, :] for c in range(n_br)]
                m = ls[0]
                for t in ls[1:]:
                    m = jnp.maximum(m, t)
                ws = [jnp.exp(t - m) for t in ls]
                z = ws[0]
                acc = ws[0] * scratch[0][0, rs, :]
                for c in range(1, n_br):
                    z = z + ws[c]
                    acc = acc + ws[c] * scratch[0][c, rs, :]
                o_ref[rs, :] = acc / z
                lse_ref[rs, :] = m + jnp.log(z)
                return carry

            lax.fori_loop(0, T // chunk, merge, 0)

    def col(base):
        return pl.BlockSpec((T, LANES), lambda g: (0, base + g))

    out = jax.ShapeDtypeStruct((T, hw), F32)
    scratch = [pltpu.VMEM((n_br, T, LANES), F32)] * 2 if n_br > 1 else []
    return pl.pallas_call(
        body, name="attn_fwd", grid=(ng,),
        in_specs=[pl.BlockSpec(memory_space=pltpu.SMEM), col(0), col(ng), col(2 * ng),
                  pl.BlockSpec((n_br, 4, 2 * QBLOCK, W), lambda g: (0, 0, g, 0)), ANY_SPEC],
        out_specs=[col(0), col(0)], out_shape=[out, out], scratch_shapes=scratch,
        compiler_params=_params(1))(sink, qkv, qkv, qkv, bias, after)


def _attn_bwd(qkv, bias, sink, dils, pad, do, lse, dd, col_base, after):
    T = qkv.shape[0]
    hw = qkv.shape[1] // 3
    ng = hw // LANES
    n_br = len(dils)
    n_blocks = T // QBLOCK
    W = QBLOCK + 2 * pad

    def body(sink_ref, q_ref, k_ref, v_ref, bias_ref, do_ref, lse_ref, dd_ref, after_ref,
             dq_ref, dk_ref, dv_ref, dt_ref, ds_ref):
        g = pl.program_id(0)
        dq_ref[...] = jnp.zeros_like(dq_ref)
        dk_ref[...] = jnp.zeros_like(dk_ref)
        dv_ref[...] = jnp.zeros_like(dv_ref)
        dt_ref[...] = jnp.zeros_like(dt_ref)
        ds_ref[...] = jnp.zeros_like(ds_ref)
        lo = _lo_lanes((QBLOCK, LANES))
        snk = jnp.where(lo, sink_ref[2 * g], sink_ref[2 * g + 1])
        for c, d in enumerate(dils):
            nb_sub = n_blocks // d

            def block(b, carry, c=c, d=d, nb_sub=nb_sub):
                r, l0, lp, ln, edge = _block_geometry(b, nb_sub, pad)
                rows_q = _rows(l0, QBLOCK, d, r)
                q = _stack_heads(q_ref[rows_q, :].astype(BF16), lo)
                k = _window(k_ref, l0, lp, ln, pad, d, r).astype(BF16)
                v = _window(v_ref, l0, lp, ln, pad, d, r).astype(BF16)
                dob = _stack_heads(do_ref[rows_q, :].astype(BF16), lo)
                lse_b = lse_ref[rows_q, :]
                dd_b = dd_ref[rows_q, :]
                s = _dot_nt(q, k) + bias_ref[c, edge]
                p = jnp.exp(s - _per_head_rows(lse_b[:, 0:1], lse_b[:, HEAD_DIM:HEAD_DIM + 1]))
                ds = p * (_dot_nt(dob, v) - _per_head_rows(dd_b[:, 0:1], dd_b[:, HEAD_DIM:HEAD_DIM + 1]))
                dsb = ds.astype(BF16)
                dkw = _dot_tn(dsb, q)
                dvw = _dot_tn(p.astype(BF16), dob)
                dt_ref[c] += ds
                dq_ref[rows_q, :] += _unstack_heads(_dot(dsb, k), lo)
                ds_ref[0:1, :] += jnp.sum(-jnp.exp(snk - lse_b) * dd_b, axis=0, keepdims=True)
                for part, (start, n) in zip((0, pad, pad + QBLOCK), ((lp, pad), (l0, QBLOCK), (ln, pad))):
                    dk_ref[_rows(start, n, d, r), :] += dkw[part:part + n]
                    dv_ref[_rows(start, n, d, r), :] += dvw[part:part + n]
                return carry

            lax.fori_loop(0, n_blocks, block, 0, unroll=BWD_UNROLL)

    def col(base):
        return pl.BlockSpec((T, LANES), lambda g: (0, base + g))

    tile = pl.BlockSpec((n_br, 2 * QBLOCK, W), lambda g: (0, g, 0))
    full = jax.ShapeDtypeStruct((T, hw), F32)
    dq, dk, dv, dt, dsink = pl.pallas_call(
        body, name="attn_bwd", grid=(ng,),
        in_specs=[pl.BlockSpec(memory_space=pltpu.SMEM), col(0), col(ng), col(2 * ng),
                  pl.BlockSpec((n_br, 4, 2 * QBLOCK, W), lambda g: (0, 0, g, 0)),
                  col(col_base), col(0), col(col_base), ANY_SPEC],
        out_specs=[col(0), col(0), col(0), tile, pl.BlockSpec((None, 8, LANES), lambda g: (g, 0, 0))],
        out_shape=[full, full, full, jax.ShapeDtypeStruct((n_br, 2 * ng * QBLOCK, W), F32),
                   jax.ShapeDtypeStruct((ng, 8, LANES), F32)],
        compiler_params=_params(1))(sink, qkv, qkv, qkv, bias, do, lse, dd, after)
    return dq, dk, dv, dt.reshape(n_br, 2 * ng, QBLOCK, W), dsink


def _mix_bwd_in(dx, wo, o_list):
    T, D = dx.shape
    widths = [o.shape[1] for o in o_list]
    hw = sum(widths)
    n = len(o_list)
    tm = _pick_tm(T, 256)
    nt = T // tm

    def body(*refs):
        dx_ref, w_ref, o_refs = refs[0], refs[1], refs[2:2 + n]
        do_ref, dd_ref, dw_ref, acc = refs[2 + n:]
        i = pl.program_id(0)

        @pl.when(i == 0)
        def _():
            acc[...] = jnp.zeros_like(acc)

        dxb = dx_ref[...].astype(BF16)
        dov = _dot_nt(dxb, w_ref[...])
        do_ref[...] = dov
        off = 0
        for o_ref, k in zip(o_refs, widths):
            ov = o_ref[...]
            prod = dov[:, off:off + k] * ov
            for cb in range(k // LANES):
                dd_ref[:, off + cb * LANES:off + (cb + 1) * LANES] = _seg_sum(prod[:, cb * LANES:(cb + 1) * LANES])
            acc[off:off + k, :] += _dot_tn(ov.astype(BF16), dxb)
            off += k

        @pl.when(i == nt - 1)
        def _():
            dw_ref[...] = acc[...].astype(BF16)

    row = pl.BlockSpec((tm, hw), lambda i: (i, 0))
    out = jax.ShapeDtypeStruct((T, hw), F32)
    return pl.pallas_call(
        body, name="mix_bwd_in", grid=(nt,),
        in_specs=[pl.BlockSpec((tm, D), lambda i: (i, 0)), pl.BlockSpec((hw, D), lambda i: (0, 0))]
        + [pl.BlockSpec((tm, k), lambda i: (i, 0)) for k in widths],
        out_specs=[row, row, pl.BlockSpec((hw, D), lambda i: (0, 0))],
        out_shape=[out, out, jax.ShapeDtypeStruct((hw, D), BF16)],
        scratch_shapes=[pltpu.VMEM((hw, D), F32)], compiler_params=_params(1))(dx, wo, *o_list)


def _mesh_pos():
    return lax.axis_index("x"), lax.axis_index("y"), lax.axis_index("c")


def _my_chip():
    return 2 * lax.axis_index("x") + lax.axis_index("y")


def _other_chips(x, y):
    return [(1 - x, y), (x, 1 - y), (1 - x, 1 - y)]


def _half_rows(rows, cc):
    hr = rows // 2
    return pl.ds(pl.multiple_of(cc * hr, 16), hr)


def _cast_into_slot(w, l):
    _, R, C = w.shape
    tm = _pick_tm(R, 512)

    def body(w_ref, o_ref):
        o_ref[...] = w_ref[...].astype(BF16)

    return pl.pallas_call(
        body, name="cast_into_slot", grid=(R // tm,),
        in_specs=[pl.BlockSpec((None, tm, C), lambda i: (l, i, 0))],
        out_specs=pl.BlockSpec((None, tm, C), lambda i: (_my_chip(), i, 0)),
        out_shape=jax.ShapeDtypeStruct((N_CHIPS, R, C), BF16), compiler_params=_params(1))(w)


def _split_start(name, arrays, make_copies, n_sem, after):
    n = len(arrays)
    n_in = n + (0 if after is None else 1)

    def body(*refs):
        send_s, recv_s = refs[n_in], refs[n_in + 1]
        token = refs[n_in + 2 + n]
        for send, _ in make_copies(refs[:n], send_s, recv_s):
            send.start()
        token[...] = jnp.zeros_like(token)

    res = pl.pallas_call(
        body, name=name,
        out_shape=(pltpu.SemaphoreType.DMA((n_sem,)), pltpu.SemaphoreType.DMA((n_sem,)),
                   *[pltpu.HBM(a.shape, a.dtype) for a in arrays], jax.ShapeDtypeStruct((8, LANES), F32)),
        in_specs=[HBM_SPEC] * n + [ANY_SPEC] * (n_in - n),
        out_specs=(SEM_SPEC, SEM_SPEC, *([HBM_SPEC] * n), pl.BlockSpec(memory_space=pltpu.VMEM)),
        input_output_aliases={i: 2 + i for i in range(n)},
        compiler_params=pltpu.CompilerParams(has_side_effects=pltpu.SideEffectType.DATAFLOW_SIDE_EFFECTING),
    )(*[pltpu.with_memory_space_constraint(a, pltpu.HBM) for a in arrays], *([] if after is None else [after]))
    return res[0], res[1], list(res[2:2 + n]), res[2 + n]


def _split_wait(name, send_s, recv_s, arrays, make_copies, after):
    n = len(arrays)

    def body(*refs):
        for send, landed in make_copies(refs[:n], refs[n], refs[n + 1]):
            send.wait_send()
            landed.wait_recv()

    return list(pl.pallas_call(
        body, name=name, out_shape=[pltpu.HBM(a.shape, a.dtype) for a in arrays],
        in_specs=[HBM_SPEC] * n + [SEM_SPEC, SEM_SPEC, ANY_SPEC], out_specs=[HBM_SPEC] * n,
        input_output_aliases={i: i for i in range(n)},
        compiler_params=pltpu.CompilerParams(has_side_effects=pltpu.SideEffectType.DATAFLOW_SIDE_EFFECTING),
    )(*arrays, send_s, recv_s, after))


def _gather_copies(shapes):
    n = len(shapes)

    def make(refs, send_s, recv_s):
        x, y, c = _mesh_pos()
        my = 2 * x + y
        copies = []
        for w in range(n):
            for k, (px, py) in enumerate(_other_chips(x, y)):
                def part(slot, w=w):
                    return refs[w].at[slot, _half_rows(shapes[w][1], c), :]
                sems = dict(send_sem=send_s.at[k * n + w], recv_sem=recv_s.at[k * n + w],
                            device_id=(px, py, c), device_id_type=MESH)
                copies.append((pltpu.make_async_remote_copy(src_ref=part(my), dst_ref=part(my), **sems),
                               pltpu.make_async_remote_copy(src_ref=part(2 * px + py), dst_ref=part(2 * px + py), **sems)))
        return copies

    return make


def _forward_copies(shapes):
    n = len(shapes)

    def make(refs, send_s, recv_s):
        x, y, c = _mesh_pos()
        copies = []
        for w in range(n):
            for k, (px, py) in enumerate(_other_chips(x, y)):
                def part(cc, w=w, slot=2 * px + py):
                    return refs[w].at[slot, _half_rows(shapes[w][1], cc), :]
                sems = dict(send_sem=send_s.at[k * n + w], recv_sem=recv_s.at[k * n + w],
                            device_id=(x, y, 1 - c), device_id_type=MESH)
                copies.append((pltpu.make_async_remote_copy(src_ref=part(c), dst_ref=part(c), **sems),
                               pltpu.make_async_remote_copy(src_ref=part(1 - c), dst_ref=part(1 - c), **sems)))
        return copies

    return make


def _pair_forward(bufs):
    n = len(bufs)
    make = _forward_copies([b.shape for b in bufs])

    def body(*refs):
        copies = make(refs[n:2 * n], refs[2 * n], refs[2 * n + 1])
        for send, _ in copies:
            send.start()
        for _, landed in copies:
            landed.wait_recv()
        for send, _ in copies:
            send.wait_send()

    return list(pl.pallas_call(
        body, name="ag_pair_forward", in_specs=[HBM_SPEC] * n, out_specs=[HBM_SPEC] * n,
        out_shape=[jax.ShapeDtypeStruct(b.shape, b.dtype) for b in bufs],
        input_output_aliases={w: w for w in range(n)},
        scratch_shapes=[pltpu.SemaphoreType.DMA((3 * n,)), pltpu.SemaphoreType.DMA((3 * n,))],
    )(*bufs))


def _pair_exchange_copies(shapes):
    n = len(shapes)

    def make(refs, send_s, recv_s):
        x, y, c = _mesh_pos()
        copies = []
        for t in range(n):
            sems = dict(send_sem=send_s.at[t], recv_sem=recv_s.at[t], device_id=(x, y, 1 - c), device_id_type=MESH)
            land = refs[n + t]
            copies.append((pltpu.make_async_remote_copy(
                src_ref=refs[t].at[:, _half_rows(shapes[t][1], 1 - c), :], dst_ref=land, **sems),
                pltpu.make_async_remote_copy(src_ref=land, dst_ref=land, **sems)))
        return copies

    return make


def _pair_share_copies(shapes):
    n = len(shapes)

    def make(refs, send_s, recv_s):
        x, y, c = _mesh_pos()
        copies = []
        for t in range(n):
            def half(cc, t=t):
                return refs[t].at[_half_rows(shapes[t][0], cc), :]
            sems = dict(send_sem=send_s.at[t], recv_sem=recv_s.at[t], device_id=(x, y, 1 - c), device_id_type=MESH)
            copies.append((pltpu.make_async_remote_copy(src_ref=half(c), dst_ref=half(c), **sems),
                           pltpu.make_async_remote_copy(src_ref=half(1 - c), dst_ref=half(1 - c), **sems)))
        return copies

    return make


def _rs_add_pair(grad, recv):
    n_slot, hr, C = recv.shape
    tm = _pick_tm(hr, 192)
    nb = hr // tm

    def body(a_ref, b_ref, o_ref):
        o_ref[...] = (a_ref[...].astype(F32) + b_ref[...].astype(F32)).astype(BF16)

    blk = pl.BlockSpec((n_slot, tm, C), lambda i: (0, i, 0))
    return pl.pallas_call(
        body, name="rs_add_pair", grid=(nb,),
        in_specs=[pl.BlockSpec((n_slot, tm, C), lambda i: (0, lax.axis_index("c") * nb + i, 0)), blk],
        out_specs=blk, out_shape=jax.ShapeDtypeStruct(recv.shape, BF16), compiler_params=_params(1))(grad, recv)


def _scatter_copies(n):
    def make(refs, send_s, recv_s):
        x, y, c = _mesh_pos()
        copies = []
        for t in range(n):
            for k, (px, py) in enumerate(_other_chips(x, y)):
                sems = dict(send_sem=send_s.at[3 * t + k], recv_sem=recv_s.at[3 * t + k],
                            device_id=(px, py, c), device_id_type=MESH)
                land = refs[n + t].at[k]
                copies.append((pltpu.make_async_remote_copy(src_ref=refs[t].at[2 * px + py], dst_ref=land, **sems),
                               pltpu.make_async_remote_copy(src_ref=land, dst_ref=land, **sems)))
        return copies

    return make


def _rs_add_chips(part, recv):
    _, hr, C = part.shape
    tm = _pick_tm(hr, 256)
    nb = hr // tm

    def body(a_ref, r0, r1, r2, o_ref):
        o_ref[...] = ((a_ref[...].astype(F32) + r0[...].astype(F32)) + r1[...].astype(F32)) + r2[...].astype(F32)

    def rel(k):
        return pl.BlockSpec((None, tm, C), lambda i: (k, i, 0))

    return pl.pallas_call(
        body, name="rs_add_chips", grid=(nb,),
        in_specs=[pl.BlockSpec((None, tm, C), lambda i: (_my_chip(), i, 0)), rel(0), rel(1), rel(2)],
        out_specs=pl.BlockSpec((tm, C), lambda i: (lax.axis_index("c") * nb + i, 0)),
        out_shape=jax.ShapeDtypeStruct((2 * hr, C), F32), compiler_params=_params(1))(part, recv, recv, recv)


def _allreduce_small(v, after):
    rows = v.shape[0]

    def body(v_ref, after_ref, o_ref, buf, send_s, recv_s):
        x, y, c = _mesh_pos()
        me = 4 * x + 2 * y + c
        buf[me] = v_ref[...]
        copies = []
        for r in range(1, N_DEV):
            px = 1 - x if r & 4 else x
            py = 1 - y if r & 2 else y
            pc = 1 - c if r & 1 else c
            send = pltpu.make_async_remote_copy(
                src_ref=v_ref, dst_ref=buf.at[me], send_sem=send_s.at[r - 1], recv_sem=recv_s.at[r - 1],
                device_id=(px, py, pc), device_id_type=MESH)
            peer_slot = buf.at[4 * px + 2 * py + pc]
            landed = pltpu.make_async_remote_copy(
                src_ref=peer_slot, dst_ref=peer_slot, send_sem=send_s.at[r - 1], recv_sem=recv_s.at[r - 1],
                device_id=(px, py, pc), device_id_type=MESH)
            copies.append((send, landed))
        for send, _ in copies:
            send.start()
        for _, landed in copies:
            landed.wait_recv()
        for send, _ in copies:
            send.wait_send()
        acc = buf[0]
        for j in range(1, N_DEV):
            acc = acc + buf[j]
        o_ref[...] = acc

    vm = pl.BlockSpec(memory_space=pltpu.VMEM)
    return pl.pallas_call(
        body, name="allreduce_small", in_specs=[vm, ANY_SPEC], out_specs=vm,
        out_shape=jax.ShapeDtypeStruct((rows, LANES), F32),
        scratch_shapes=[pltpu.VMEM((N_DEV, rows, LANES), F32), pltpu.SemaphoreType.DMA((N_DEV - 1,)),
                        pltpu.SemaphoreType.DMA((N_DEV - 1,))],
    )(v, after)


def _adamw_fn(w, g, m, v):
    m2 = ADAM_B1 * m + (1.0 - ADAM_B1) * g
    v2 = ADAM_B2 * v + (1.0 - ADAM_B2) * (g * g)
    m_hat = m2 / (1.0 - ADAM_B1 ** ADAM_STEP)
    v_hat = v2 / (1.0 - ADAM_B2 ** ADAM_STEP)
    delta = -ADAM_LR * (m_hat / (jnp.sqrt(v_hat) + ADAM_EPS) + ADAM_WD * w)
    return g, delta, m2, v2


def _adamw_layer(w, g, m, v, l, prev, after):
    NL, R, C = w.shape
    tm = _pick_tm(R, 256)
    n_prev = 0 if prev is None else 4

    def body(w_ref, g_ref, m_ref, v_ref, after_ref, *rest):
        outs = rest[n_prev:]
        for o_ref, val in zip(outs, _adamw_fn(w_ref[...], g_ref[...], m_ref[...], v_ref[...])):
            o_ref[...] = val

    lay = pl.BlockSpec((None, tm, C), lambda i: (l, i, 0))
    shape = jax.ShapeDtypeStruct((NL, R, C), F32)
    return pl.pallas_call(
        body, name="adamw", grid=(R // tm,),
        in_specs=[lay, pl.BlockSpec((tm, C), lambda i: (i, 0)), lay, lay, ANY_SPEC] + [ANY_SPEC] * n_prev,
        out_specs=[lay] * 4, out_shape=[shape] * 4,
        input_output_aliases={5 + j: j for j in range(n_prev)},
        compiler_params=_params(1))(w, g, m, v, after, *(prev or []))


def _pack_small(parts):
    out = []
    for a in parts:
        flat = a.reshape(-1)
        n = -(-flat.shape[0] // (8 * LANES)) * 8 * LANES
        out.append(jnp.pad(flat, (0, n - flat.shape[0])).reshape(-1, LANES))
    return jnp.concatenate(out, axis=0)


def _unpack_small(packed, like):
    out, r = [], 0
    for a in like:
        size = int(np.prod(a.shape))
        rows = -(-size // (8 * LANES)) * 8
        out.append(packed[r:r + rows].reshape(-1)[:size].reshape(a.shape))
        r += rows
    return out


def _ffn_forward(x, g, win, wout):
    h, silu, dgate, act = _ffn_up(x, g, win)
    wout, after = wout(act) if callable(wout) else (wout, act)
    return _mm_res("ffn_down", x, [act], wout, 0.5, after), (x, h, silu, dgate, act)


def _ffn_backward(dx, saved, g, win, wout, after):
    x, h, silu, dgate, act = saved
    D = x.shape[1]
    wc = win.shape[2]
    dgu, dx_in, dg = _ffn_bwd(dx, wout, silu, dgate, win, x, g, after)
    dwout = _mm_tn("dw_ffn_out", act, wc, dx, D, 2, True, False, 0.5)
    dwin = _mm_tn_pairs("dw_ffn_in", h, dgu, wc, 4)
    return dx_in, dg, dwin, dwout.reshape(N_CHIPS, -1, D)


GROUP_MID = ("w_o", "ffn2_w_in", "ffn2_w_out", "w_ple_gate", "w_ple_proj")
GROUP_LAST = ("w_qkv", "ffn1_w_in", "ffn1_w_out")
GATHER_L0 = (("a", ("ffn1_w_in",)), ("b", ("ffn1_w_out",)), ("c", ("w_qkv", "w_o")),
             ("d", ("ffn2_w_in", "ffn2_w_out", "w_ple_gate", "w_ple_proj")))


def _gather_start(tag, slotted, after):
    return _split_start("ag_start_" + tag, slotted, _gather_copies([a.shape for a in slotted]), 3 * len(slotted), after)


def _gather_finish(tag, started, after):
    send_s, recv_s, arrays, _ = started
    return _pair_forward(_split_wait("ag_wait_" + tag, send_s, recv_s, arrays,
                                     _gather_copies([a.shape for a in arrays]), after))


def _scatter_exchange(tag, grads):
    n = len(grads)
    land = [lax.empty((g_.shape[0], g_.shape[1] // 2, g_.shape[2]), g_.dtype) for g_ in grads]
    return _split_start("rs_px_start_" + tag, list(grads) + land, _pair_exchange_copies([g_.shape for g_ in grads]),
                        n, None)


def _scatter_chips(tag, started, after):
    send_s, recv_s, arrays, _ = started
    n = len(arrays) // 2
    arrays = _split_wait("rs_px_wait_" + tag, send_s, recv_s, arrays,
                         _pair_exchange_copies([a.shape for a in arrays[:n]]), after)
    part = [_rs_add_pair(g_, r_) for g_, r_ in zip(arrays[:n], arrays[n:])]
    land = [lax.empty((3,) + p_.shape[1:], p_.dtype) for p_ in part]
    return _split_start("rs_start_" + tag, part + land, _scatter_copies(n), 3 * n, None)


def _scatter_share(tag, started, after):
    send_s, recv_s, arrays, _ = started
    n = len(arrays) // 2
    arrays = _split_wait("rs_wait_" + tag, send_s, recv_s, arrays, _scatter_copies(n), after)
    halves = [_rs_add_chips(p_, r_) for p_, r_ in zip(arrays[:n], arrays[n:])]
    return _split_start("rs_ps_start_" + tag, halves, _pair_share_copies([h_.shape for h_ in halves]), n, None)


def _scatter_done(tag, started, after):
    send_s, recv_s, arrays, _ = started
    return _split_wait("rs_ps_wait_" + tag, send_s, recv_s, arrays, _pair_share_copies([a.shape for a in arrays]), after)


def kernel(x, p, rel_bias, norm_ffn1, ffn1_w_in, ffn1_w_out, norm_mix, w_qkv, q_norm_a, k_norm_a, q_norm_b, k_norm_b, sink_b, w_o, norm_ffn2, ffn2_w_in, ffn2_w_out, norm_ple, w_ple_gate, w_ple_proj, loss_target, m_rel_bias, m_norm_ffn1, m_ffn1_w_in, m_ffn1_w_out, m_norm_mix, m_w_qkv, m_q_norm_a, m_k_norm_a, m_q_norm_b, m_k_norm_b, m_sink_b, m_w_o, m_norm_ffn2, m_ffn2_w_in, m_ffn2_w_out, m_norm_ple, m_w_ple_gate, m_w_ple_proj, v_rel_bias, v_norm_ffn1, v_ffn1_w_in, v_ffn1_w_out, v_norm_mix, v_w_qkv, v_q_norm_a, v_k_norm_a, v_q_norm_b, v_k_norm_b, v_sink_b, v_w_o, v_norm_ffn2, v_ffn2_w_in, v_ffn2_w_out, v_norm_ple, v_w_ple_gate, v_w_ple_proj):
    given = dict(locals())
    T, D = x.shape[1], x.shape[2]
    NL = norm_ffn1.shape[0]
    x0 = x.reshape(T, D)
    tgt = loss_target.reshape(T, D)
    n_a, n_b, n_kv = _qkv_layout(D)

    assert NL == 2
    slot = [{name: _cast_into_slot(given[name], l) for name in BIG} for l in range(NL)]
    ag, token = {}, None
    for tag, names in GATHER_L0:
        ag[tag] = _gather_start(tag, [slot[0][n] for n in names], token)
        token = ag[tag][3]
    ag_1 = _gather_start("1", [slot[1][n] for n in BIG], token)

    def arrived(tag, after):
        return dict(zip(dict(GATHER_L0)[tag], _gather_finish(tag, ag[tag], after)))

    def by_rows(a):
        return a.reshape(-1, a.shape[-1])

    def by_cols(a):
        return a.transpose(1, 0, 2).reshape(a.shape[1], -1)

    QW = N_CHIPS * w_qkv.shape[2]

    dils = tuple(d for _, d in DILATED_CONFIGS)
    cfg_a = [(w // (2 * d), d) for w, d in DILATED_CONFIGS]
    pad_a, pad_b = _window_pad(cfg_a[0][0]), _window_pad(SWA_RADIUS)
    bmaps_a, bmaps_b = jnp.asarray(_bucket_maps(cfg_a)), jnp.asarray(_bucket_maps([(SWA_RADIUS, 1)]))
    n_heads = rel_bias.shape[1] // 2
    bias_a = _edge_variants(_bias_build(rel_bias, bmaps_a, 0), pad_a)
    bias_b = _edge_variants(_bias_build(rel_bias, bmaps_b, n_heads), pad_b)
    no_sink = jnp.full((n_heads,), NEG, F32)

    def gains_row(l):
        ones = jnp.ones((n_a * LANES,), F32)
        return jnp.concatenate([
            jnp.tile(q_norm_a[l], 2 * n_a), jnp.tile(k_norm_a[l], 2 * n_a), ones,
            jnp.tile(q_norm_b[l], 2 * n_b), jnp.tile(k_norm_b[l], 2 * n_kv), jnp.ones((n_kv * LANES,), F32)]).reshape(1, QW)

    def forward_start(tag, state, after):
        landed = _split_wait("ag_wait_" + tag, state[0], state[1], state[2],
                             _gather_copies([a.shape for a in state[2]]), after)
        return _split_start("ag_pf_start_" + tag, landed, _forward_copies([a.shape for a in landed]), 3 * len(landed),
                            None)

    def forward_done(tag, names, pf, after):
        shapes = [a.shape for a in pf[2]]
        return dict(zip(names, _split_wait("ag_pf_wait_" + tag, pf[0], pf[1], pf[2], _forward_copies(shapes), after)))

    saved, weights = [], []
    xc = x0
    pf_1 = None
    for l in range(NL):
        s, w = {}, {}
        if l == 0:
            w.update(arrived("a", ag_1[3]))
            pf = {}

            def ffn1_w_out(act, w=w, pf=pf):
                w.update(arrived("b", act))
                w["ffn1_w_out"] = by_rows(w["ffn1_w_out"])
                pf["c"] = forward_start("c", ag["c"], w["ffn1_w_out"])
                return w["ffn1_w_out"], pf["c"][3]
        else:
            w.update(forward_done("1", BIG, pf_1, xc))
            ffn1_w_out = w["ffn1_w_out"] = by_rows(w["ffn1_w_out"])
        xc, s["ffn1"] = _ffn_forward(xc, norm_ffn1[l:l + 1], w["ffn1_w_in"], ffn1_w_out)
        s["x1"] = xc
        if l == 0:
            w.update(forward_done("c", dict(GATHER_L0)["c"], pf["c"], xc))
        w["w_qkv"] = by_cols(w["w_qkv"])
        w["w_o"] = by_rows(w["w_o"])
        s["h2"], s["raw"], s["qkv_a"], s["qkv_b"] = _qkv_proj(xc, norm_mix[l:l + 1], w["w_qkv"], gains_row(l))
        after_qkv = s["qkv_a"]
        if l == 0:
            pf["d"] = forward_start("d", ag["d"], s["qkv_a"])
            after_qkv = pf["d"][3]
        s["o_a"], s["lse_a"] = _attn_fwd(s["qkv_a"], bias_a, no_sink, dils, pad_a, after_qkv)
        s["o_b"], s["lse_b"] = _attn_fwd(s["qkv_b"], bias_b, sink_b[l], (1,), pad_b, s["qkv_b"])
        xc = _mm_res("attn_out", xc, [s["o_a"], s["o_b"]], w["w_o"], 1.0, s["o_b"])
        if l == 0:
            w.update(forward_done("d", dict(GATHER_L0)["d"], pf["d"], xc))
        w["w_ple_proj"] = by_cols(w["w_ple_proj"])
        for name in ("ffn2_w_out", "w_ple_gate"):
            w[name] = by_rows(w[name])
        xc, s["ffn2"] = _ffn_forward(xc, norm_ffn2[l:l + 1], w["ffn2_w_in"], w["ffn2_w_out"])
        s["x3"] = xc
        if l == 0:
            pf_1 = forward_start("1", ag_1, xc)
        s["p"] = p[l].reshape(T, -1)
        s["hn"], xc, s["gate"], s["pp"] = _ple_fwd(xc, norm_ple[l:l + 1], s["p"], w["w_ple_gate"], w["w_ple_proj"],
                                                   pf_1[3] if l == 0 else xc)
        saved.append(s)
        weights.append(w)

    dx, loss_blk = _loss_fwd_bwd(xc, tgt)
    loss = lax.psum(loss_blk[0, 0], ("x", "y", "c"))

    gs = {name: [None] * NL for name in SMALL if name != "rel_bias"}
    dt_a, dt_b = [], []

    def layer_backward(l, dx, hooks):
        def at(point, ready, *more):
            return hooks[point](ready, *more) if point in hooks else ready

        s, w, gw = saved[l], weights[l], {}
        dx, gs["norm_ple"][l], dwg, dwp = _ple_bwd(dx, s["gate"], s["pp"], s["hn"], s["p"], s["x3"],
                                                   norm_ple[l:l + 1], w["w_ple_gate"], at("start", dx))
        gw["w_ple_gate"] = dwg.reshape(N_CHIPS, -1, D)
        gw["w_ple_proj"] = dwp.reshape(dwp.shape[0], N_CHIPS, -1).transpose(1, 0, 2)
        dx, gs["norm_ffn2"][l], gw["ffn2_w_in"], gw["ffn2_w_out"] = _ffn_backward(
            dx, s["ffn2"], norm_ffn2[l:l + 1], w["ffn2_w_in"], w["ffn2_w_out"], at("after_ple", dx))
        do, dd, dwo = _mix_bwd_in(dx, w["w_o"], [s["o_a"], s["o_b"]])
        hwa = s["o_a"].shape[1]
        gw["w_o"] = dwo.reshape(N_CHIPS, -1, D)
        dqa, dka, dva, dt, _ = _attn_bwd(s["qkv_a"], bias_a, no_sink, dils, pad_a, do, s["lse_a"], dd, 0, do)
        dt_a.append(dt)
        dqb, dkb, dvb, dt, dsink = _attn_bwd(s["qkv_b"], bias_b, sink_b[l], (1,), pad_b, do, s["lse_b"], dd,
                                             hwa // LANES, at("after_attn_a", dqa))
        dt_b.append(dt)
        gs["sink_b"][l] = dsink[:, 0, ::HEAD_DIM].reshape(-1)
        dx, gs["norm_mix"][l], dgains, dwqkv = _qkv_bwd(
            s["raw"], gains_row(l), (dqa, dka, dva), (dqb, dkb, dvb), w["w_qkv"], s["h2"], s["x1"],
            norm_mix[l:l + 1], dx, at("before_qkv_bwd", dqb, [gw[n] for n in GROUP_MID]))
        dgv = dgains.reshape(-1, HEAD_DIM)
        gs["q_norm_a"][l] = dgv[:2 * n_a].sum(0)
        gs["k_norm_a"][l] = dgv[2 * n_a:4 * n_a].sum(0)
        gs["q_norm_b"][l] = dgv[6 * n_a:6 * n_a + 2 * n_b].sum(0)
        gs["k_norm_b"][l] = dgv[6 * n_a + 2 * n_b:6 * n_a + 2 * n_b + 2 * n_kv].sum(0)
        gw["w_qkv"] = dwqkv.reshape(D, N_CHIPS, -1).transpose(1, 0, 2)
        dx, gs["norm_ffn1"][l], gw["ffn1_w_in"], gw["ffn1_w_out"] = _ffn_backward(
            dx, s["ffn1"], norm_ffn1[l:l + 1], w["ffn1_w_in"], w["ffn1_w_out"], at("before_ffn1", dx))
        return dx, gw

    out = {}

    def adamw_group(names, l, grads, after):
        for name, g_ in zip(names, grads):
            out[name] = _adamw_layer(given[name], g_, given["m_" + name], given["v_" + name], l, out.get(name), after)
            after = out[name][0]
        return after

    dx, gw1 = layer_backward(NL - 1, dx, {})
    px_1 = _scatter_exchange("1", [gw1[n] for n in BIG])
    rs = {}

    def chips_1(ready):
        rs["chips_1"] = _scatter_chips("1", px_1, ready)
        return rs["chips_1"][3]

    def share_1(ready):
        rs["share_1"] = _scatter_share("1", rs["chips_1"], ready)
        return rs["share_1"][3]

    def exchange_0a(ready, grads):
        rs["px_0a"] = _scatter_exchange("0a", grads)
        return rs["px_0a"][3]

    def chips_0a(ready):
        rs["chips_0a"] = _scatter_chips("0a", rs["px_0a"], ready)
        return rs["chips_0a"][3]

    dx, gw0 = layer_backward(0, dx, {"start": lambda ready: px_1[3], "after_ple": chips_1, "after_attn_a": share_1,
                                     "before_qkv_bwd": exchange_0a, "before_ffn1": chips_0a})
    grad_x = dx.reshape(x.shape)

    px_0b = _scatter_exchange("0b", [gw0[n] for n in GROUP_LAST])
    d_rel_bias = (_bias_grad(dt_a, bmaps_a, 0) + _bias_grad(dt_b, bmaps_b, n_heads))[:, :rel_bias.shape[1]]
    small_g = [d_rel_bias] + [jnp.stack([t.reshape(-1) for t in gs[name]]) for name in SMALL[1:]]
    g_sum = _allreduce_small(_pack_small(small_g), px_0b[3])
    res = _ew("adamw_small", _adamw_fn,
              [_pack_small([given[n] for n in SMALL]), g_sum, _pack_small([given["m_" + n] for n in SMALL]),
               _pack_small([given["v_" + n] for n in SMALL])], [(LANES, F32)] * 4)
    like = [given[n] for n in SMALL]
    unpacked = [_unpack_small(r, like) for r in res]
    for i, name in enumerate(SMALL):
        out[name] = [u[i] for u in unpacked]

    share_0a = _scatter_share("0a", rs["chips_0a"], res[0])
    g_1 = _scatter_done("1", rs["share_1"], share_0a[3])
    chips_0b = _scatter_chips("0b", px_0b, g_1[0])
    ready = adamw_group(BIG, 1, g_1, chips_0b[3])
    ready = adamw_group(GROUP_MID, 0, _scatter_done("0a", share_0a, ready), ready)
    share_0b = _scatter_share("0b", chips_0b, ready)
    adamw_group(GROUP_LAST, 0, _scatter_done("0b", share_0b, share_0b[3]), share_0b[3])

    return (loss, grad_x, *[out[n][0] for n in WEIGHTS], *[out[n][1] for n in WEIGHTS],
            *[out[n][2] for n in WEIGHTS], *[out[n][3] for n in WEIGHTS])
```

```python
import functools
import math

import numpy as np
import jax
import jax.numpy as jnp
from jax import lax
from jax.experimental import pallas as pl
from jax.experimental.pallas import tpu as pltpu

F32 = jnp.float32
BF16 = jnp.bfloat16
MESH = pl.DeviceIdType.MESH

HEAD_DIM = 64
LANES = 128
QBLOCK = 128
FWD_UNROLL, BWD_UNROLL = 8, 8
QKV_CHUNK = 6
N_BUCKETS = 32
MAX_DISTANCE = 1024
DILATED_CONFIGS = ((128, 1), (512, 4), (2048, 16))
SWA_RADIUS = 128
GROUP_B = 4
EPS = 1e-6
NEG = -1e30
Q_SCALE = HEAD_DIM ** -0.5
ADAM_LR, ADAM_B1, ADAM_B2, ADAM_EPS, ADAM_WD, ADAM_STEP = 0.001, 0.9, 0.999, 1e-08, 0.01, 10
VMEM_LIMIT = 56 * 2 ** 20
N_CHIPS = 4
N_DEV = 8

BIG = ("ffn1_w_in", "ffn1_w_out", "w_qkv", "w_o", "ffn2_w_in", "ffn2_w_out", "w_ple_gate", "w_ple_proj")
SMALL = ("rel_bias", "norm_ffn1", "norm_mix", "q_norm_a", "k_norm_a", "q_norm_b", "k_norm_b", "sink_b",
         "norm_ffn2", "norm_ple")
WEIGHTS = ("rel_bias", "norm_ffn1", "ffn1_w_in", "ffn1_w_out", "norm_mix", "w_qkv", "q_norm_a", "k_norm_a",
           "q_norm_b", "k_norm_b", "sink_b", "w_o", "norm_ffn2", "ffn2_w_in", "ffn2_w_out", "norm_ple",
           "w_ple_gate", "w_ple_proj")


HBM_SPEC = pl.BlockSpec(memory_space=pltpu.HBM)
ANY_SPEC = pl.BlockSpec(memory_space=pl.ANY)
SEM_SPEC = pl.BlockSpec(memory_space=pltpu.SEMAPHORE)


def _params(n_grid):
    return pltpu.CompilerParams(dimension_semantics=("arbitrary",) * n_grid, vmem_limit_bytes=VMEM_LIMIT)


def _pick_tm(rows, cap):
    t = (min(cap, rows) // 16) * 16
    while t >= 16:
        if rows % t == 0:
            return t
        t -= 16
    return rows


def _dot(a, b):
    return jnp.dot(a, b, preferred_element_type=F32)


def _dot_nt(a, b):
    return lax.dot_general(a, b, (((1,), (1,)), ((), ())), preferred_element_type=F32)


def _dot_tn(a, b):
    return lax.dot_general(a, b, (((0,), (0,)), ((), ())), preferred_element_type=F32)


def _sigmoid(z):
    return 1.0 / (1.0 + jnp.exp(-z))


def _lo_lanes(shape):
    return lax.broadcasted_iota(jnp.int32, shape, len(shape) - 1) % LANES < HEAD_DIM


def _seg_sum(blk):
    lo = _lo_lanes(blk.shape)
    s_lo = jnp.sum(jnp.where(lo, blk, 0.0), axis=1, keepdims=True)
    s_hi = jnp.sum(jnp.where(lo, 0.0, blk), axis=1, keepdims=True)
    return jnp.where(lo, s_lo, s_hi)


def _rms_bwd_tile(x, g, dh):
    r = lax.rsqrt(jnp.mean(x * x, axis=-1, keepdims=True) + EPS)
    xh = x * r
    dyg = dh * g
    dx = r * (dyg - xh * jnp.mean(dyg * xh, axis=-1, keepdims=True))
    return dx, jnp.sum(dh * xh, axis=0, keepdims=True)


def _ew(name, fn, ins, out_defs, cap=512):
    rows = ins[0].shape[0]
    tm = _pick_tm(rows, cap)
    n_in = len(ins)

    def body(*refs):
        vals = fn(*[r[...] for r in refs[:n_in]])
        if not isinstance(vals, tuple):
            vals = (vals,)
        for r, v in zip(refs[n_in:], vals):
            r[...] = v.astype(r.dtype)

    return pl.pallas_call(
        body, name=name, grid=(rows // tm,),
        in_specs=[pl.BlockSpec((tm, a.shape[1]), lambda i: (i, 0)) for a in ins],
        out_specs=[pl.BlockSpec((tm, c), lambda i: (i, 0)) for c, _ in out_defs],
        out_shape=[jax.ShapeDtypeStruct((rows, c), dt) for c, dt in out_defs],
        compiler_params=_params(1))(*ins)


def _rms_tile(xv, gv):
    r = lax.rsqrt(jnp.mean(xv * xv, axis=-1, keepdims=True) + EPS)
    return (xv * r * gv).astype(BF16)


def _ffn_up(x, g, win):
    T, D = x.shape
    wc = win.shape[2]
    tm = _pick_tm(T, 512)

    def body(x_ref, g_ref, wg_ref, wu_ref, h_ref, silu_ref, dgate_ref, act_ref, wcat):
        @pl.when(pl.program_id(1) == 0)
        def _():
            wcat[:, :wc] = wg_ref[...]
            wcat[:, wc:] = wu_ref[...]

        hv = _rms_tile(x_ref[...], g_ref[...])
        h_ref[...] = hv
        gu = _dot(hv, wcat[...])
        gte, u = gu[:, :wc], gu[:, wc:]
        sg = _sigmoid(gte)
        silu = gte * sg
        silu_ref[...] = silu.astype(BF16)
        dgate_ref[...] = ((sg + silu * (1.0 - sg)) * u).astype(BF16)
        act_ref[...] = (silu * u).astype(BF16)

    out = jax.ShapeDtypeStruct((T, 2 * wc), BF16)
    ospec = pl.BlockSpec((tm, wc), lambda j, i: (i, j))
    nt = T // tm
    h_spec = pl.BlockSpec((tm, D), lambda j, i: (jnp.where(j == 0, i, nt), 0))
    return pl.pallas_call(
        body, name="ffn_up", grid=(2, nt),
        in_specs=[pl.BlockSpec((tm, D), lambda j, i: (i, 0)), pl.BlockSpec((1, D), lambda j, i: (0, 0)),
                  pl.BlockSpec((None, D, wc), lambda j, i: (j, 0, 0)),
                  pl.BlockSpec((None, D, wc), lambda j, i: (j + 2, 0, 0))],
        out_specs=[h_spec] + [ospec] * 3, out_shape=[jax.ShapeDtypeStruct((T + tm, D), BF16)] + [out] * 3,
        scratch_shapes=[pltpu.VMEM((D, 2 * wc), BF16)], compiler_params=_params(2))(x, g, win, win)


def _mm_res(name, res, a_list, w, scale, after):
    T, N = res.shape
    n = len(a_list)
    widths = [a.shape[1] for a in a_list]
    tm = _pick_tm(T, 512)

    def body(*refs):
        r_ref, a_refs, w_refs, o_ref = refs[0], refs[1:1 + n], refs[1 + n:1 + 2 * n], refs[2 + 2 * n]
        acc = _dot(a_refs[0][...].astype(BF16), w_refs[0][...])
        for a_ref, w_ref in zip(a_refs[1:], w_refs[1:]):
            acc = acc + _dot(a_ref[...].astype(BF16), w_ref[...])
        o_ref[...] = r_ref[...] + scale * acc

    w_specs, off = [], 0
    for k in widths:
        w_specs.append(pl.BlockSpec((k, N), lambda i, blk=off // k: (blk, 0)))
        off += k
    return pl.pallas_call(
        body, name=name, grid=(T // tm,),
        in_specs=[pl.BlockSpec((tm, N), lambda i: (i, 0))]
        + [pl.BlockSpec((tm, k), lambda i: (i, 0)) for k in widths] + w_specs + [ANY_SPEC],
        out_specs=pl.BlockSpec((tm, N), lambda i: (i, 0)),
        out_shape=jax.ShapeDtypeStruct((T, N), F32),
        compiler_params=_params(1))(res, *a_list, *([w] * n), after)


def _mm_tn(name, a, a_w, b, b_w, n_slots, a_by_slot, b_by_slot, scale, tm_cap=512):
    T = b.shape[0]
    tm = _pick_tm(T, tm_cap)
    nt = T // tm

    def body(a_ref, b_ref, o_ref, acc):
        i = pl.program_id(1)

        @pl.when(i == 0)
        def _():
            acc[...] = jnp.zeros_like(acc)

        acc[...] += _dot_tn(a_ref[...].astype(BF16), b_ref[...].astype(BF16))

        @pl.when(i == nt - 1)
        def _():
            o_ref[...] = (acc[...] * scale).astype(BF16)

    return pl.pallas_call(
        body, name=name, grid=(n_slots, nt),
        in_specs=[pl.BlockSpec((tm, a_w), (lambda s, i: (i, s)) if a_by_slot else (lambda s, i: (i, 0))),
                  pl.BlockSpec((tm, b_w), (lambda s, i: (i, s)) if b_by_slot else (lambda s, i: (i, 0)))],
        out_specs=pl.BlockSpec((None, a_w, b_w), lambda s, i: (s, 0, 0)),
        out_shape=jax.ShapeDtypeStruct((n_slots, a_w, b_w), BF16),
        scratch_shapes=[pltpu.VMEM((a_w, b_w), F32)], compiler_params=_params(2))(a, b)


def _mm_tn_pairs(name, a, b, b_w, n_slots):
    T = b.shape[0]
    a_w = a.shape[1]
    tm = _pick_tm(T, 512)
    nt = T // tm

    def body(a_ref, b_ref, o_ref, acc):
        i = pl.program_id(1)

        @pl.when(i == 0)
        def _():
            acc[...] = jnp.zeros_like(acc)

        acc[...] += _dot_tn(a_ref[...], b_ref[...])

        @pl.when(i == nt - 1)
        def _():
            o_ref[0] = acc[:, :b_w].astype(BF16)
            o_ref[1] = acc[:, b_w:].astype(BF16)

    return pl.pallas_call(
        body, name=name, grid=(n_slots // 2, nt),
        in_specs=[pl.BlockSpec((tm, a_w), lambda s, i: (i, 0)), pl.BlockSpec((tm, 2 * b_w), lambda s, i: (i, s))],
        out_specs=pl.BlockSpec((2, a_w, b_w), lambda s, i: (s, 0, 0)),
        out_shape=jax.ShapeDtypeStruct((n_slots, a_w, b_w), BF16),
        scratch_shapes=[pltpu.VMEM((a_w, 2 * b_w), F32)], compiler_params=_params(2))(a, b)


def _ffn_bwd(dx, wout, silu, dgate, win, x, g, after):
    T, D = dx.shape
    F = silu.shape[1]
    n_slots, _, wc = win.shape
    tm = _pick_tm(T, 256)

    def body(dx_ref, wout_hbm, s_ref, dgt_ref, win_hbm, x_ref, g_ref, after_ref,
             dgu_ref, dxo_ref, dg_ref, wout_v, wcat, sem):
        @pl.when(pl.program_id(0) == 0)
        def _():
            copies = [pltpu.make_async_copy(wout_hbm, wout_v, sem.at[n_slots])]
            copies += [pltpu.make_async_copy(win_hbm.at[s], wcat.at[s // 2, :, pl.ds((s % 2) * wc, wc)], sem.at[s])
                       for s in range(n_slots)]
            for cp in copies:
                cp.start()
            for cp in copies:
                cp.wait()
            dg_ref[...] = jnp.zeros_like(dg_ref)

        dxv = dx_ref[...]
        dact = 0.5 * _dot_nt(dxv.astype(BF16), wout_v[...])
        d_gate = (dact * dgt_ref[...].astype(F32)).astype(BF16)
        d_up = (dact * s_ref[...].astype(F32)).astype(BF16)
        dgu_ref[:, :F] = d_gate
        dgu_ref[:, F:] = d_up
        dh = _dot_nt(d_gate, wcat[0]) + _dot_nt(d_up, wcat[1])
        dxn, dg = _rms_bwd_tile(x_ref[...], g_ref[...], dh)
        dxo_ref[...] = dxv + dxn
        dg_ref[...] += dg

    row = pl.BlockSpec((tm, D), lambda i: (i, 0))
    vec = pl.BlockSpec((1, D), lambda i: (0, 0))
    act_spec = pl.BlockSpec((tm, F), lambda i: (i, 0))
    return pl.pallas_call(
        body, name="ffn_bwd", grid=(T // tm,),
        in_specs=[row, ANY_SPEC, act_spec, act_spec, ANY_SPEC, row, vec, ANY_SPEC],
        out_specs=[pl.BlockSpec((tm, 2 * F), lambda i: (i, 0)), row, vec],
        out_shape=[jax.ShapeDtypeStruct((T, 2 * F), BF16), jax.ShapeDtypeStruct((T, D), F32),
                   jax.ShapeDtypeStruct((1, D), F32)],
        scratch_shapes=[pltpu.VMEM((F, D), BF16), pltpu.VMEM((n_slots // 2, D, 2 * wc), BF16),
                        pltpu.SemaphoreType.DMA((n_slots + 1,))],
        compiler_params=_params(1))(dx, wout, silu, dgate, win, x, g, after)


def _ple_fwd(x, g, p, layer, wg, wp, after):
    T, D = x.shape
    P = p.shape[-1]
    tm = _pick_tm(T, 256)

    def body(x_ref, g_ref, p_ref, wg_ref, wp_ref, after_ref, hn_ref, xo_ref, gate_ref, pp_ref):
        xv = x_ref[...]
        hn = _rms_tile(xv, g_ref[...])
        hn_ref[...] = hn
        gate = _sigmoid(_dot(hn, wg_ref[...]))
        pp = _dot(p_ref[...].astype(BF16), wp_ref[...])
        gate_ref[...] = gate.astype(BF16)
        pp_ref[...] = pp.astype(BF16)
        xo_ref[...] = xv + gate * pp

    row = pl.BlockSpec((tm, D), lambda i: (i, 0))
    out = jax.ShapeDtypeStruct((T, D), F32)
    half = jax.ShapeDtypeStruct((T, D), BF16)
    return pl.pallas_call(
        body, name="ple_fwd", grid=(T // tm,),
        in_specs=[row, pl.BlockSpec((1, D), lambda i: (0, 0)), pl.BlockSpec((None, None, tm, P), lambda i: (layer, 0, i, 0)),
                  pl.BlockSpec((D, D), lambda i: (0, 0)), pl.BlockSpec((P, D), lambda i: (0, 0)), ANY_SPEC],
        out_specs=[row, row, row, row], out_shape=[half, out, half, half],
        compiler_params=_params(1))(x, g, p, wg, wp, after)


def _ple_bwd(dx, gate, pp, hn, p, layer, x, g, wg, after):
    T, D = x.shape
    P = p.shape[-1]
    tm = _pick_tm(T, 256)
    nt = T // tm

    def body(dx_ref, gate_ref, pp_ref, hn_ref, p_ref, x_ref, g_ref, wg_ref, after_ref,
             dxo_ref, dg_ref, dwg_ref, dwp_ref, acc_g, acc_p):
        i = pl.program_id(0)

        @pl.when(i == 0)
        def _():
            acc_g[...] = jnp.zeros_like(acc_g)
            acc_p[...] = jnp.zeros_like(acc_p)
            dg_ref[...] = jnp.zeros_like(dg_ref)

        dxv = dx_ref[...]
        gate = gate_ref[...].astype(F32)
        dz = (dxv * pp_ref[...].astype(F32) * gate * (1.0 - gate)).astype(BF16)
        dpp = (dxv * gate).astype(BF16)
        acc_g[...] += _dot_tn(hn_ref[...], dz)
        acc_p[...] += _dot_tn(p_ref[...].astype(BF16), dpp)
        dxn, dg = _rms_bwd_tile(x_ref[...], g_ref[...], _dot_nt(dz, wg_ref[...]))
        dxo_ref[...] = dxv + dxn
        dg_ref[...] += dg

        @pl.when(i == nt - 1)
        def _():
            dwg_ref[...] = acc_g[...].astype(BF16)
            dwp_ref[...] = acc_p[...].astype(BF16)

    row = pl.BlockSpec((tm, D), lambda i: (i, 0))
    vec = pl.BlockSpec((1, D), lambda i: (0, 0))
    return pl.pallas_call(
        body, name="ple_bwd", grid=(nt,),
        in_specs=[row, row, row, row, pl.BlockSpec((None, None, tm, P), lambda i: (layer, 0, i, 0)), row, vec,
                  pl.BlockSpec((D, D), lambda i: (0, 0)), ANY_SPEC],
        out_specs=[row, vec, pl.BlockSpec((D, D), lambda i: (0, 0)), pl.BlockSpec((P, D), lambda i: (0, 0))],
        out_shape=[jax.ShapeDtypeStruct((T, D), F32), jax.ShapeDtypeStruct((1, D), F32),
                   jax.ShapeDtypeStruct((D, D), BF16), jax.ShapeDtypeStruct((P, D), BF16)],
        scratch_shapes=[pltpu.VMEM((D, D), F32), pltpu.VMEM((P, D), F32)],
        compiler_params=_params(1))(dx, gate, pp, hn, p, x, g, wg, after)


def _loss_fwd_bwd(y, tgt):
    T, D = y.shape
    tm = _pick_tm(T, 512)

    def body(y_ref, t_ref, dy_ref, loss_ref):
        e = y_ref[...] - t_ref[...]
        dy_ref[...] = e / D

        @pl.when(pl.program_id(0) == 0)
        def _():
            loss_ref[...] = jnp.zeros_like(loss_ref)

        loss_ref[...] += 0.5 * jnp.sum(jnp.mean(e * e, axis=-1, keepdims=True), axis=0, keepdims=True)

    row = pl.BlockSpec((tm, D), lambda i: (i, 0))
    return pl.pallas_call(
        body, name="loss", grid=(T // tm,), in_specs=[row, row],
        out_specs=[row, pl.BlockSpec((8, LANES), lambda i: (0, 0))],
        out_shape=[jax.ShapeDtypeStruct((T, D), F32), jax.ShapeDtypeStruct((8, LANES), F32)],
        compiler_params=_params(1))(y, tgt)


def _qkv_layout(D):
    n_a = D // (2 * LANES)
    n_b = D // (2 * LANES)
    n_kv = max(1, (2 * n_b) // GROUP_B) * HEAD_DIM // LANES
    return n_a, n_b, n_kv


def _dup_half(xv, half):
    rolled = pltpu.roll(xv, HEAD_DIM, 1)
    lo = _lo_lanes(xv.shape)
    return jnp.where(lo, xv, rolled) if half == 0 else jnp.where(lo, rolled, xv)


def _qkv_proj(x, g_mix, w, gains):
    T, D = x.shape
    W = w.shape[1]
    n_a, n_b, n_kv = _qkv_layout(D)
    tm = _pick_tm(T, 256)
    o_qb = 3 * n_a

    def norm(xv, gv, scale):
        ms = _seg_sum(xv * xv) * (1.0 / HEAD_DIM)
        return xv * lax.rsqrt(ms + EPS) * gv * scale

    def body(x_ref, gm_ref, w_ref, g_ref, h_ref, raw_ref, a_ref, b_ref):
        hv = _rms_tile(x_ref[...], gm_ref[...])
        h_ref[...] = hv
        raw = _dot(hv, w_ref[...])
        raw_ref[...] = raw

        def blk(cb):
            return raw[:, cb * LANES:(cb + 1) * LANES]

        def gn(cb):
            return g_ref[:, cb * LANES:(cb + 1) * LANES]

        for cb in range(n_a):
            a_ref[:, cb * LANES:(cb + 1) * LANES] = norm(blk(cb), gn(cb), Q_SCALE)
            cbk = n_a + cb
            a_ref[:, cbk * LANES:(cbk + 1) * LANES] = norm(blk(cbk), gn(cbk), 1.0)
            cbv = 2 * n_a + cb
            a_ref[:, cbv * LANES:(cbv + 1) * LANES] = blk(cbv)
        for cb in range(n_b):
            src = o_qb + cb
            b_ref[:, cb * LANES:(cb + 1) * LANES] = norm(blk(src), gn(src), Q_SCALE)
        for e in range(n_b):
            kvh = (2 * e) // GROUP_B
            ck = o_qb + n_b + kvh // 2
            cv = ck + n_kv
            kn = norm(blk(ck), gn(ck), 1.0)
            b_ref[:, (n_b + e) * LANES:(n_b + e + 1) * LANES] = _dup_half(kn, kvh % 2)
            b_ref[:, (2 * n_b + e) * LANES:(2 * n_b + e + 1) * LANES] = _dup_half(blk(cv), kvh % 2)

    wa, wb = 3 * n_a * LANES, 3 * n_b * LANES

    def rows(width):
        return pl.BlockSpec((tm, width), lambda i: (i, 0))

    return pl.pallas_call(
        body, name="qkv_proj", grid=(T // tm,),
        in_specs=[rows(D), pl.BlockSpec((1, D), lambda i: (0, 0)), pl.BlockSpec((D, W), lambda i: (0, 0)),
                  pl.BlockSpec((1, W), lambda i: (0, 0))],
        out_specs=[rows(D), rows(W), rows(wa), rows(wb)],
        out_shape=[jax.ShapeDtypeStruct((T, D), BF16), jax.ShapeDtypeStruct((T, W), F32),
                   jax.ShapeDtypeStruct((T, wa), F32), jax.ShapeDtypeStruct((T, wb), F32)],
        compiler_params=_params(1))(x, g_mix, w, gains)


def _qkv_bwd(raw, gains, d_a, d_b, w, h, x, g_mix, dx_in, after):
    T, W = raw.shape
    D = x.shape[1]
    n_a, n_b, n_kv = _qkv_layout(D)
    tm = _pick_tm(T, 256)
    nt = T // tm
    o_qb = 3 * n_a

    def body(raw_ref, g_ref, daq, dak, dav, dbq, dbk, dbv, w_ref, h_ref, x_ref, gm_ref, dxi_ref, after_ref,
             dx_ref, dgm_ref, dg_ref, dw_ref, acc):
        i = pl.program_id(0)

        @pl.when(i == 0)
        def _():
            dg_ref[...] = jnp.zeros_like(dg_ref)
            dgm_ref[...] = jnp.zeros_like(dgm_ref)
            acc[...] = jnp.zeros_like(acc)

        def cols(ref, cb):
            return ref[:, cb * LANES:(cb + 1) * LANES]

        pieces = [None] * (W // LANES)

        def norm_bwd(cb, dy, scale):
            xv = cols(raw_ref, cb)
            gv = cols(g_ref, cb)
            r = lax.rsqrt(_seg_sum(xv * xv) * (1.0 / HEAD_DIM) + EPS)
            xh = xv * r
            dys = dy * scale
            dyg = dys * gv
            dxv = r * (dyg - xh * (_seg_sum(dyg * xh) * (1.0 / HEAD_DIM)))
            pieces[cb] = dxv.astype(BF16)
            dg_ref[:, cb * LANES:(cb + 1) * LANES] += jnp.sum(dys * xh, axis=0, keepdims=True)

        def fold(ref, kv_blk):
            halves = []
            for half in range(2):
                kvh = 2 * kv_blk + half
                blocks = [e for e in range(n_b) if (2 * e) // GROUP_B == kvh]
                s = cols(ref, blocks[0])
                for e in blocks[1:]:
                    s = s + cols(ref, e)
                halves.append(s + pltpu.roll(s, HEAD_DIM, 1))
            return jnp.where(_lo_lanes(halves[0].shape), halves[0], halves[1])

        for cb in range(n_a):
            norm_bwd(cb, cols(daq, cb), Q_SCALE)
        for cb in range(n_a):
            norm_bwd(n_a + cb, cols(dak, cb), 1.0)
        for cb in range(n_a):
            pieces[2 * n_a + cb] = cols(dav, cb).astype(BF16)
        for cb in range(n_b):
            norm_bwd(o_qb + cb, cols(dbq, cb), Q_SCALE)
        for kb in range(n_kv):
            norm_bwd(o_qb + n_b + kb, fold(dbk, kb), 1.0)
        for kb in range(n_kv):
            pieces[o_qb + n_b + n_kv + kb] = fold(dbv, kb).astype(BF16)

        hv = h_ref[...]
        dh = None
        for c0 in range(0, W // LANES, QKV_CHUNK):
            chunk = jnp.concatenate(pieces[c0:c0 + QKV_CHUNK], axis=1)
            lanes = slice(c0 * LANES, (c0 + QKV_CHUNK) * LANES)
            part = _dot_nt(chunk, w_ref[:, lanes])
            dh = part if dh is None else dh + part
            acc[:, lanes] += _dot_tn(hv, chunk)
        dxn, dgm = _rms_bwd_tile(x_ref[...], gm_ref[...], dh)
        dx_ref[...] = dxi_ref[...] + dxn
        dgm_ref[...] += dgm

        @pl.when(i == nt - 1)
        def _():
            dw_ref[...] = acc[...].astype(BF16)

    hw_a, hw_b = n_a * LANES, n_b * LANES

    def rows(width):
        return pl.BlockSpec((tm, width), lambda i: (i, 0))

    def fixed(r, c):
        return pl.BlockSpec((r, c), lambda i: (0, 0))

    return pl.pallas_call(
        body, name="qkv_bwd", grid=(nt,),
        in_specs=[rows(W), fixed(1, W)] + [rows(hw_a)] * 3 + [rows(hw_b)] * 3
        + [fixed(D, W), rows(D), rows(D), fixed(1, D), rows(D), ANY_SPEC],
        out_specs=[rows(D), fixed(1, D), fixed(1, W), fixed(D, W)],
        out_shape=[jax.ShapeDtypeStruct((T, D), F32), jax.ShapeDtypeStruct((1, D), F32),
                   jax.ShapeDtypeStruct((1, W), F32), jax.ShapeDtypeStruct((D, W), BF16)],
        scratch_shapes=[pltpu.VMEM((D, W), F32)],
        compiler_params=_params(1))(raw, gains, *d_a, *d_b, w, h, x, g_mix, dx_in, after)


def _t5_bucket_np(rel):
    half = N_BUCKETS // 2
    max_exact = half // 2
    ret = np.where(rel > 0, half, 0)
    n = np.abs(rel)
    nf = np.maximum(n, 1).astype(np.float32)
    large = max_exact + (np.log(nf / np.float32(max_exact)) / np.float32(math.log(MAX_DISTANCE / max_exact))
                         * np.float32(half - max_exact)).astype(np.int32)
    large = np.minimum(large, half - 1)
    return ret + np.where(n < max_exact, n, large)


def _window_pad(radius):
    assert radius <= QBLOCK
    return HEAD_DIM if radius <= HEAD_DIM else QBLOCK


def _bucket_maps(configs):
    pad = _window_pad(configs[0][0])
    q = np.arange(QBLOCK)[:, None]
    kk = np.arange(QBLOCK + 2 * pad)[None, :]
    rel = kk - pad - q
    maps = [np.where(np.abs(rel) <= radius, _t5_bucket_np(rel * dil), -1) for radius, dil in configs]
    return np.stack(maps).astype(np.int32)


def _bias_build(rel_bias, bmaps, col0, pad):
    n_sets, _, W = bmaps.shape
    n_buckets, n_cols = rel_bias.shape
    n_heads = n_cols // 2

    def body(rb_ref, bm_ref, o_ref):
        h = pl.program_id(1)
        bm = bm_ref[0:1, :]
        head = lax.broadcasted_iota(jnp.int32, (n_buckets, n_cols), 1) == col0 + h
        per_bucket = jnp.sum(jnp.where(head, rb_ref[...], 0.0), axis=1, keepdims=True)
        hit = lax.broadcasted_iota(jnp.int32, (n_buckets, W), 0) == bm
        row0 = jnp.sum(jnp.where(hit, per_bucket, 0.0), axis=0, keepdims=True) + jnp.where(bm < 0, NEG, 0.0)
        tile = pltpu.roll(jnp.broadcast_to(row0, (QBLOCK, W)), 0, 1, stride=1, stride_axis=0)
        col = lax.broadcasted_iota(jnp.int32, (QBLOCK, W), 1)
        left, right = col < pad, col >= pad + QBLOCK
        o_ref[0] = tile
        o_ref[1] = jnp.where(left, NEG, tile)
        o_ref[2] = jnp.where(right, NEG, tile)
        o_ref[3] = jnp.where(left | right, NEG, tile)

    return pl.pallas_call(
        body, name="bias_build", grid=(n_sets, n_heads),
        in_specs=[pl.BlockSpec((n_buckets, n_cols), lambda s, h: (0, 0)),
                  pl.BlockSpec((None, QBLOCK, W), lambda s, h: (s, 0, 0))],
        out_specs=pl.BlockSpec((None, 4, QBLOCK, W), lambda s, h: (s, 0, h, 0)),
        out_shape=jax.ShapeDtypeStruct((n_sets, 4, n_heads * QBLOCK, W), F32),
        compiler_params=_params(2))(rel_bias, bmaps)


def _bias_grad(dtiles, bmaps, col0):
    n_sets, _, W = bmaps.shape
    n_heads = dtiles[0].shape[1]
    n_l = len(dtiles)

    def body(*refs):
        bm_ref, o_ref = refs[0], refs[1 + n_l]
        s, h = pl.program_id(0), pl.program_id(1)

        @pl.when((s == 0) & (h == 0))
        def _():
            o_ref[...] = jnp.zeros_like(o_ref)

        d = refs[1][...]
        for r in refs[2:1 + n_l]:
            d = d + r[...]
        acc8 = d[0:8, :]
        for a in range(1, QBLOCK // 8):
            acc8 = acc8 + pltpu.roll(d[8 * a:8 * a + 8, :], W - 8 * a, 1)
        per_offset = acc8[0:1, :]
        for b in range(1, 8):
            per_offset = per_offset + pltpu.roll(acc8[b:b + 1, :], W - b, 1)
        bucket = lax.broadcasted_iota(jnp.int32, (N_BUCKETS, W), 0)
        hit = bucket == bm_ref[0:1, :]
        per_bucket = jnp.sum(jnp.where(hit, per_offset, 0.0), axis=1, keepdims=True)
        lanes = lax.broadcasted_iota(jnp.int32, o_ref.shape, 1)
        o_ref[...] += jnp.where(lanes == col0 + h, per_bucket, 0.0)

    tile = pl.BlockSpec((None, None, QBLOCK, W), lambda s, h: (s, h, 0, 0))
    return pl.pallas_call(
        body, name="bias_grad", grid=(n_sets, n_heads),
        in_specs=[pl.BlockSpec((None, QBLOCK, W), lambda s, h: (s, 0, 0))] + [tile] * n_l,
        out_specs=pl.BlockSpec((N_BUCKETS, LANES), lambda s, h: (0, 0)),
        out_shape=jax.ShapeDtypeStruct((N_BUCKETS, LANES), F32), compiler_params=_params(2))(bmaps, *dtiles)


def _rows(l_start, n, d, r):
    if d == 1:
        return pl.ds(pl.multiple_of(l_start, 8), n)
    return pl.ds(l_start * d + r, n, stride=d)


def _stack_heads(xv, lo):
    z = jnp.zeros_like(xv)
    return jnp.concatenate([jnp.where(lo, xv, z), jnp.where(lo, z, xv)], axis=0)


def _unstack_heads(xv, lo):
    return jnp.where(lo, xv[:QBLOCK], xv[QBLOCK:])


def _per_head_rows(v0, v1):
    if jnp.ndim(v0) == 0:
        return jnp.where(lax.broadcasted_iota(jnp.int32, (2 * QBLOCK, 1), 0) < QBLOCK, v0, v1)
    return jnp.concatenate([v0, v1], axis=0)


def _block_geometry(b, nb_sub, pad):
    r, lb = b // nb_sub, b % nb_sub
    l0 = lb * QBLOCK
    lp = jnp.maximum(l0 - pad, 0)
    ln = jnp.minimum(l0 + QBLOCK, nb_sub * QBLOCK - pad)
    return r, l0, lp, ln, (lb == 0).astype(jnp.int32) + 2 * (lb == nb_sub - 1).astype(jnp.int32)


def _window(ref, l0, lp, ln, pad, d, r):
    return jnp.concatenate([ref[_rows(lp, pad, d, r), :], ref[_rows(l0, QBLOCK, d, r), :],
                            ref[_rows(ln, pad, d, r), :]], axis=0)


def _attn_fwd(qkv, bias, sink, dils, pad, after):
    T = qkv.shape[0]
    hw = qkv.shape[1] // 3
    ng = hw // LANES
    n_br = len(dils)
    n_blocks = T // QBLOCK
    W = QBLOCK + 2 * pad
    chunk = 256

    def body(sink_ref, q_ref, k_ref, v_ref, bias_ref, after_ref, o_ref, lse_ref, *scratch):
        g = pl.program_id(0)
        lo = _lo_lanes((QBLOCK, LANES))
        snk = _per_head_rows(sink_ref[2 * g], sink_ref[2 * g + 1])
        for c, d in enumerate(dils):
            nb_sub = n_blocks // d
            o_dst = scratch[0].at[c] if n_br > 1 else o_ref
            l_dst = scratch[1].at[c] if n_br > 1 else lse_ref

            def block(b, carry, c=c, d=d, nb_sub=nb_sub, o_dst=o_dst, l_dst=l_dst):
                r, l0, lp, ln, edge = _block_geometry(b, nb_sub, pad)
                q = _stack_heads(q_ref[_rows(l0, QBLOCK, d, r), :].astype(BF16), lo)
                k = _window(k_ref, l0, lp, ln, pad, d, r).astype(BF16)
                v = _window(v_ref, l0, lp, ln, pad, d, r).astype(BF16)
                s = _dot_nt(q, k) + bias_ref[c, edge]
                m = jnp.maximum(jnp.max(s, axis=1, keepdims=True), snk)
                p = jnp.exp(s - m)
                den = jnp.sum(p, axis=1, keepdims=True) + jnp.exp(snk - m)
                o_dst[_rows(l0, QBLOCK, d, r), :] = _unstack_heads(_dot(p.astype(BF16), v) / den, lo)
                l_dst[_rows(l0, QBLOCK, d, r), :] = _unstack_heads(
                    jnp.broadcast_to(m + jnp.log(den), (2 * QBLOCK, LANES)), lo)
                return carry

            lax.fori_loop(0, n_blocks, block, 0, unroll=FWD_UNROLL)

        if n_br > 1:
            def merge(i, carry):
                rs = pl.ds(pl.multiple_of(i * chunk, chunk), chunk)
                ls = [scratch[1][c, rs, :] for c in range(n_br)]
                m = ls[0]
                for t in ls[1:]:
                    m = jnp.maximum(m, t)
                ws = [jnp.exp(t - m) for t in ls]
                z = ws[0]
                acc = ws[0] * scratch[0][0, rs, :]
                for c in range(1, n_br):
                    z = z + ws[c]
                    acc = acc + ws[c] * scratch[0][c, rs, :]
                o_ref[rs, :] = acc / z
                lse_ref[rs, :] = m + jnp.log(z)
                return carry

            lax.fori_loop(0, T // chunk, merge, 0)

    def col(base):
        return pl.BlockSpec((T, LANES), lambda g: (0, base + g))

    out = jax.ShapeDtypeStruct((T, hw), F32)
    scratch = [pltpu.VMEM((n_br, T, LANES), F32)] * 2 if n_br > 1 else []
    return pl.pallas_call(
        body, name="attn_fwd", grid=(ng,),
        in_specs=[pl.BlockSpec(memory_space=pltpu.SMEM), col(0), col(ng), col(2 * ng),
                  pl.BlockSpec((n_br, 4, 2 * QBLOCK, W), lambda g: (0, 0, g, 0)), ANY_SPEC],
        out_specs=[col(0), col(0)], out_shape=[out, out], scratch_shapes=scratch,
        compiler_params=_params(1))(sink, qkv, qkv, qkv, bias, after)


def _attn_bwd(qkv, bias, sink, dils, pad, do, lse, dd, col_base, after):
    T = qkv.shape[0]
    hw = qkv.shape[1] // 3
    ng = hw // LANES
    n_br = len(dils)
    n_blocks = T // QBLOCK
    W = QBLOCK + 2 * pad

    def body(sink_ref, q_ref, k_ref, v_ref, bias_ref, do_ref, lse_ref, dd_ref, after_ref,
             dq_ref, dk_ref, dv_ref, dt_ref, ds_ref):
        g = pl.program_id(0)
        dq_ref[...] = jnp.zeros_like(dq_ref)
        dk_ref[...] = jnp.zeros_like(dk_ref)
        dv_ref[...] = jnp.zeros_like(dv_ref)
        dt_ref[...] = jnp.zeros_like(dt_ref)
        ds_ref[...] = jnp.zeros_like(ds_ref)
        lo = _lo_lanes((QBLOCK, LANES))
        snk = jnp.where(lo, sink_ref[2 * g], sink_ref[2 * g + 1])
        for c, d in enumerate(dils):
            nb_sub = n_blocks // d

            def block(b, carry, c=c, d=d, nb_sub=nb_sub):
                r, l0, lp, ln, edge = _block_geometry(b, nb_sub, pad)
                rows_q = _rows(l0, QBLOCK, d, r)
                q = _stack_heads(q_ref[rows_q, :].astype(BF16), lo)
                k = _window(k_ref, l0, lp, ln, pad, d, r).astype(BF16)
                v = _window(v_ref, l0, lp, ln, pad, d, r).astype(BF16)
                dob = _stack_heads(do_ref[rows_q, :].astype(BF16), lo)
                lse_b = lse_ref[rows_q, :]
                dd_b = dd_ref[rows_q, :]
                s = _dot_nt(q, k) + bias_ref[c, edge]
                p = jnp.exp(s - _per_head_rows(lse_b[:, 0:1], lse_b[:, HEAD_DIM:HEAD_DIM + 1]))
                ds = p * (_dot_nt(dob, v) - _per_head_rows(dd_b[:, 0:1], dd_b[:, HEAD_DIM:HEAD_DIM + 1]))
                dsb = ds.astype(BF16)
                dkw = _dot_tn(dsb, q)
                dvw = _dot_tn(p.astype(BF16), dob)
                dt_ref[c] += ds
                dq_ref[rows_q, :] += _unstack_heads(_dot(dsb, k), lo)
                ds_ref[0:1, :] += jnp.sum(-jnp.exp(snk - lse_b) * dd_b, axis=0, keepdims=True)
                for part, (start, n) in zip((0, pad, pad + QBLOCK), ((lp, pad), (l0, QBLOCK), (ln, pad))):
                    dk_ref[_rows(start, n, d, r), :] += dkw[part:part + n]
                    dv_ref[_rows(start, n, d, r), :] += dvw[part:part + n]
                return carry

            lax.fori_loop(0, n_blocks, block, 0, unroll=BWD_UNROLL)

    def col(base):
        return pl.BlockSpec((T, LANES), lambda g: (0, base + g))

    tile = pl.BlockSpec((n_br, 2 * QBLOCK, W), lambda g: (0, g, 0))
    full = jax.ShapeDtypeStruct((T, hw), F32)
    dq, dk, dv, dt, dsink = pl.pallas_call(
        body, name="attn_bwd", grid=(ng,),
        in_specs=[pl.BlockSpec(memory_space=pltpu.SMEM), col(0), col(ng), col(2 * ng),
                  pl.BlockSpec((n_br, 4, 2 * QBLOCK, W), lambda g: (0, 0, g, 0)),
                  col(col_base), col(0), col(col_base), ANY_SPEC],
        out_specs=[col(0), col(0), col(0), tile, pl.BlockSpec((None, 8, LANES), lambda g: (g, 0, 0))],
        out_shape=[full, full, full, jax.ShapeDtypeStruct((n_br, 2 * ng * QBLOCK, W), F32),
                   jax.ShapeDtypeStruct((ng, 8, LANES), F32)],
        compiler_params=_params(1))(sink, qkv, qkv, qkv, bias, do, lse, dd, after)
    return dq, dk, dv, dt.reshape(n_br, 2 * ng, QBLOCK, W), dsink


def _mix_bwd_in(dx, wo, o_list):
    T, D = dx.shape
    widths = [o.shape[1] for o in o_list]
    hw = sum(widths)
    n = len(o_list)
    tm = _pick_tm(T, 256)
    nt = T // tm

    def body(*refs):
        dx_ref, w_ref, o_refs = refs[0], refs[1], refs[2:2 + n]
        do_ref, dd_ref, dw_ref, acc = refs[2 + n:]
        i = pl.program_id(0)

        @pl.when(i == 0)
        def _():
            acc[...] = jnp.zeros_like(acc)

        dxb = dx_ref[...].astype(BF16)
        dov = _dot_nt(dxb, w_ref[...])
        do_ref[...] = dov
        off = 0
        for o_ref, k in zip(o_refs, widths):
            ov = o_ref[...]
            prod = dov[:, off:off + k] * ov
            for cb in range(k // LANES):
                dd_ref[:, off + cb * LANES:off + (cb + 1) * LANES] = _seg_sum(prod[:, cb * LANES:(cb + 1) * LANES])
            acc[off:off + k, :] += _dot_tn(ov.astype(BF16), dxb)
            off += k

        @pl.when(i == nt - 1)
        def _():
            dw_ref[...] = acc[...].astype(BF16)

    row = pl.BlockSpec((tm, hw), lambda i: (i, 0))
    out = jax.ShapeDtypeStruct((T, hw), F32)
    return pl.pallas_call(
        body, name="mix_bwd_in", grid=(nt,),
        in_specs=[pl.BlockSpec((tm, D), lambda i: (i, 0)), pl.BlockSpec((hw, D), lambda i: (0, 0))]
        + [pl.BlockSpec((tm, k), lambda i: (i, 0)) for k in widths],
        out_specs=[row, row, pl.BlockSpec((hw, D), lambda i: (0, 0))],
        out_shape=[out, out, jax.ShapeDtypeStruct((hw, D), BF16)],
        scratch_shapes=[pltpu.VMEM((hw, D), F32)], compiler_params=_params(1))(dx, wo, *o_list)


def _mesh_pos():
    return lax.axis_index("x"), lax.axis_index("y"), lax.axis_index("c")


def _my_chip():
    return 2 * lax.axis_index("x") + lax.axis_index("y")


def _other_chips(x, y):
    return [(1 - x, y), (x, 1 - y), (1 - x, 1 - y)]


def _half_rows(rows, cc):
    hr = rows // 2
    return pl.ds(pl.multiple_of(cc * hr, 16), hr)


def _cast_into_slot(w, l):
    _, R, C = w.shape
    tm = _pick_tm(R, 512)

    def body(w_ref, o_ref):
        o_ref[...] = w_ref[...].astype(BF16)

    return pl.pallas_call(
        body, name="cast_into_slot", grid=(R // tm,),
        in_specs=[pl.BlockSpec((None, tm, C), lambda i: (l, i, 0))],
        out_specs=pl.BlockSpec((None, tm, C), lambda i: (_my_chip(), i, 0)),
        out_shape=jax.ShapeDtypeStruct((N_CHIPS, R, C), BF16), compiler_params=_params(1))(w)


def _split_start(name, arrays, make_copies, n_sem, after):
    n = len(arrays)
    n_in = n + (0 if after is None else 1)

    def body(*refs):
        send_s, recv_s = refs[n_in], refs[n_in + 1]
        token = refs[n_in + 2 + n]
        for send, _ in make_copies(refs[:n], send_s, recv_s):
            send.start()
        token[...] = jnp.zeros_like(token)

    res = pl.pallas_call(
        body, name=name,
        out_shape=(pltpu.SemaphoreType.DMA((n_sem,)), pltpu.SemaphoreType.DMA((n_sem,)),
                   *[pltpu.HBM(a.shape, a.dtype) for a in arrays], jax.ShapeDtypeStruct((8, LANES), F32)),
        in_specs=[HBM_SPEC] * n + [ANY_SPEC] * (n_in - n),
        out_specs=(SEM_SPEC, SEM_SPEC, *([HBM_SPEC] * n), pl.BlockSpec(memory_space=pltpu.VMEM)),
        input_output_aliases={i: 2 + i for i in range(n)},
        compiler_params=pltpu.CompilerParams(has_side_effects=pltpu.SideEffectType.DATAFLOW_SIDE_EFFECTING),
    )(*[pltpu.with_memory_space_constraint(a, pltpu.HBM) for a in arrays], *([] if after is None else [after]))
    return res[0], res[1], list(res[2:2 + n]), res[2 + n]


def _split_wait(name, send_s, recv_s, arrays, make_copies, after):
    n = len(arrays)

    def body(*refs):
        for send, landed in make_copies(refs[:n], refs[n], refs[n + 1]):
            send.wait_send()
            landed.wait_recv()

    return list(pl.pallas_call(
        body, name=name, out_shape=[pltpu.HBM(a.shape, a.dtype) for a in arrays],
        in_specs=[HBM_SPEC] * n + [SEM_SPEC, SEM_SPEC, ANY_SPEC], out_specs=[HBM_SPEC] * n,
        input_output_aliases={i: i for i in range(n)},
        compiler_params=pltpu.CompilerParams(has_side_effects=pltpu.SideEffectType.DATAFLOW_SIDE_EFFECTING),
    )(*arrays, send_s, recv_s, after))


def _gather_copies(shapes):
    n = len(shapes)

    def make(refs, send_s, recv_s):
        x, y, c = _mesh_pos()
        my = 2 * x + y
        copies = []
        for w in range(n):
            for k, (px, py) in enumerate(_other_chips(x, y)):
                def part(slot, w=w):
                    return refs[w].at[slot, _half_rows(shapes[w][1], c), :]
                sems = dict(send_sem=send_s.at[k * n + w], recv_sem=recv_s.at[k * n + w],
                            device_id=(px, py, c), device_id_type=MESH)
                copies.append((pltpu.make_async_remote_copy(src_ref=part(my), dst_ref=part(my), **sems),
                               pltpu.make_async_remote_copy(src_ref=part(2 * px + py), dst_ref=part(2 * px + py), **sems)))
        return copies

    return make


def _forward_copies(shapes):
    n = len(shapes)

    def make(refs, send_s, recv_s):
        x, y, c = _mesh_pos()
        copies = []
        for w in range(n):
            for k, (px, py) in enumerate(_other_chips(x, y)):
                def part(cc, w=w, slot=2 * px + py):
                    return refs[w].at[slot, _half_rows(shapes[w][1], cc), :]
                sems = dict(send_sem=send_s.at[k * n + w], recv_sem=recv_s.at[k * n + w],
                            device_id=(x, y, 1 - c), device_id_type=MESH)
                copies.append((pltpu.make_async_remote_copy(src_ref=part(c), dst_ref=part(c), **sems),
                               pltpu.make_async_remote_copy(src_ref=part(1 - c), dst_ref=part(1 - c), **sems)))
        return copies

    return make


def _pair_forward(bufs):
    n = len(bufs)
    make = _forward_copies([b.shape for b in bufs])

    def body(*refs):
        copies = make(refs[n:2 * n], refs[2 * n], refs[2 * n + 1])
        for send, _ in copies:
            send.start()
        for _, landed in copies:
            landed.wait_recv()
        for send, _ in copies:
            send.wait_send()

    return list(pl.pallas_call(
        body, name="ag_pair_forward", in_specs=[HBM_SPEC] * n, out_specs=[HBM_SPEC] * n,
        out_shape=[jax.ShapeDtypeStruct(b.shape, b.dtype) for b in bufs],
        input_output_aliases={w: w for w in range(n)},
        scratch_shapes=[pltpu.SemaphoreType.DMA((3 * n,)), pltpu.SemaphoreType.DMA((3 * n,))],
    )(*bufs))


def _pair_exchange_copies(shapes):
    n = len(shapes)

    def make(refs, send_s, recv_s):
        x, y, c = _mesh_pos()
        copies = []
        for t in range(n):
            sems = dict(send_sem=send_s.at[t], recv_sem=recv_s.at[t], device_id=(x, y, 1 - c), device_id_type=MESH)
            land = refs[n + t]
            copies.append((pltpu.make_async_remote_copy(
                src_ref=refs[t].at[:, _half_rows(shapes[t][1], 1 - c), :], dst_ref=land, **sems),
                pltpu.make_async_remote_copy(src_ref=land, dst_ref=land, **sems)))
        return copies

    return make


def _pair_share_copies(shapes):
    n = len(shapes)

    def make(refs, send_s, recv_s):
        x, y, c = _mesh_pos()
        copies = []
        for t in range(n):
            def half(cc, t=t):
                return refs[t].at[_half_rows(shapes[t][0], cc), :]
            sems = dict(send_sem=send_s.at[t], recv_sem=recv_s.at[t], device_id=(x, y, 1 - c), device_id_type=MESH)
            copies.append((pltpu.make_async_remote_copy(src_ref=half(c), dst_ref=half(c), **sems),
                           pltpu.make_async_remote_copy(src_ref=half(1 - c), dst_ref=half(1 - c), **sems)))
        return copies

    return make


def _rs_add_pair(grad, recv):
    n_slot, hr, C = recv.shape
    tm = _pick_tm(hr, 192)
    nb = hr // tm

    def body(a_ref, b_ref, o_ref):
        o_ref[...] = (a_ref[...].astype(F32) + b_ref[...].astype(F32)).astype(BF16)

    blk = pl.BlockSpec((n_slot, tm, C), lambda i: (0, i, 0))
    return pl.pallas_call(
        body, name="rs_add_pair", grid=(nb,),
        in_specs=[pl.BlockSpec((n_slot, tm, C), lambda i: (0, lax.axis_index("c") * nb + i, 0)), blk],
        out_specs=blk, out_shape=jax.ShapeDtypeStruct(recv.shape, BF16), compiler_params=_params(1))(grad, recv)


def _scatter_copies(n):
    def make(refs, send_s, recv_s):
        x, y, c = _mesh_pos()
        copies = []
        for t in range(n):
            for k, (px, py) in enumerate(_other_chips(x, y)):
                sems = dict(send_sem=send_s.at[3 * t + k], recv_sem=recv_s.at[3 * t + k],
                            device_id=(px, py, c), device_id_type=MESH)
                land = refs[n + t].at[k]
                copies.append((pltpu.make_async_remote_copy(src_ref=refs[t].at[2 * px + py], dst_ref=land, **sems),
                               pltpu.make_async_remote_copy(src_ref=land, dst_ref=land, **sems)))
        return copies

    return make


def _rs_add_chips(part, recv):
    _, hr, C = part.shape
    tm = _pick_tm(hr, 256)
    nb = hr // tm

    def body(a_ref, r0, r1, r2, o_ref):
        o_ref[...] = ((a_ref[...].astype(F32) + r0[...].astype(F32)) + r1[...].astype(F32)) + r2[...].astype(F32)

    def rel(k):
        return pl.BlockSpec((None, tm, C), lambda i: (k, i, 0))

    return pl.pallas_call(
        body, name="rs_add_chips", grid=(nb,),
        in_specs=[pl.BlockSpec((None, tm, C), lambda i: (_my_chip(), i, 0)), rel(0), rel(1), rel(2)],
        out_specs=pl.BlockSpec((tm, C), lambda i: (lax.axis_index("c") * nb + i, 0)),
        out_shape=jax.ShapeDtypeStruct((2 * hr, C), F32), compiler_params=_params(1))(part, recv, recv, recv)


def _allreduce_small(v, after):
    rows = v.shape[0]

    def body(v_ref, after_ref, o_ref, buf, send_s, recv_s):
        x, y, c = _mesh_pos()
        me = 4 * x + 2 * y + c
        buf[me] = v_ref[...]
        copies = []
        for r in range(1, N_DEV):
            px = 1 - x if r & 4 else x
            py = 1 - y if r & 2 else y
            pc = 1 - c if r & 1 else c
            send = pltpu.make_async_remote_copy(
                src_ref=v_ref, dst_ref=buf.at[me], send_sem=send_s.at[r - 1], recv_sem=recv_s.at[r - 1],
                device_id=(px, py, pc), device_id_type=MESH)
            peer_slot = buf.at[4 * px + 2 * py + pc]
            landed = pltpu.make_async_remote_copy(
                src_ref=peer_slot, dst_ref=peer_slot, send_sem=send_s.at[r - 1], recv_sem=recv_s.at[r - 1],
                device_id=(px, py, pc), device_id_type=MESH)
            copies.append((send, landed))
        for send, _ in copies:
            send.start()
        for _, landed in copies:
            landed.wait_recv()
        for send, _ in copies:
            send.wait_send()
        acc = buf[0]
        for j in range(1, N_DEV):
            acc = acc + buf[j]
        o_ref[...] = acc

    vm = pl.BlockSpec(memory_space=pltpu.VMEM)
    return pl.pallas_call(
        body, name="allreduce_small", in_specs=[vm, ANY_SPEC], out_specs=vm,
        out_shape=jax.ShapeDtypeStruct((rows, LANES), F32),
        scratch_shapes=[pltpu.VMEM((N_DEV, rows, LANES), F32), pltpu.SemaphoreType.DMA((N_DEV - 1,)),
                        pltpu.SemaphoreType.DMA((N_DEV - 1,))],
    )(v, after)


def _adamw_fn(w, g, m, v):
    m2 = ADAM_B1 * m + (1.0 - ADAM_B1) * g
    v2 = ADAM_B2 * v + (1.0 - ADAM_B2) * (g * g)
    m_hat = m2 / (1.0 - ADAM_B1 ** ADAM_STEP)
    v_hat = v2 / (1.0 - ADAM_B2 ** ADAM_STEP)
    delta = -ADAM_LR * (m_hat / (jnp.sqrt(v_hat) + ADAM_EPS) + ADAM_WD * w)
    return g, delta, m2, v2


def _adamw_layer(w, g, m, v, l, prev, after):
    NL, R, C = w.shape
    tm = _pick_tm(R, 256)
    n_prev = 0 if prev is None else 4

    def body(w_ref, g_ref, m_ref, v_ref, after_ref, *rest):
        outs = rest[n_prev:]
        for o_ref, val in zip(outs, _adamw_fn(w_ref[...], g_ref[...], m_ref[...], v_ref[...])):
            o_ref[...] = val

    lay = pl.BlockSpec((None, tm, C), lambda i: (l, i, 0))
    shape = jax.ShapeDtypeStruct((NL, R, C), F32)
    return pl.pallas_call(
        body, name="adamw", grid=(R // tm,),
        in_specs=[lay, pl.BlockSpec((tm, C), lambda i: (i, 0)), lay, lay, ANY_SPEC] + [ANY_SPEC] * n_prev,
        out_specs=[lay] * 4, out_shape=[shape] * 4,
        input_output_aliases={5 + j: j for j in range(n_prev)},
        compiler_params=_params(1))(w, g, m, v, after, *(prev or []))


def _pack_small(parts):
    out = []
    for a in parts:
        flat = a.reshape(-1)
        n = -(-flat.shape[0] // (8 * LANES)) * 8 * LANES
        out.append(jnp.pad(flat, (0, n - flat.shape[0])).reshape(-1, LANES))
    return jnp.concatenate(out, axis=0)


def _unpack_small(packed, like):
    out, r = [], 0
    for a in like:
        size = int(np.prod(a.shape))
        rows = -(-size // (8 * LANES)) * 8
        out.append(packed[r:r + rows].reshape(-1)[:size].reshape(a.shape))
        r += rows
    return out


def _ffn_forward(x, g, win, wout):
    h, silu, dgate, act = _ffn_up(x, g, win)
    wout, after = wout(act) if callable(wout) else (wout, act)
    return _mm_res("ffn_down", x, [act], wout, 0.5, after), (x, h, silu, dgate, act)


def _ffn_backward(dx, saved, g, win, wout, after):
    x, h, silu, dgate, act = saved
    D = x.shape[1]
    wc = win.shape[2]
    dgu, dx_in, dg = _ffn_bwd(dx, wout, silu, dgate, win, x, g, after)
    dwout = _mm_tn("dw_ffn_out", act, wc, dx, D, 2, True, False, 0.5)
    dwin = _mm_tn_pairs("dw_ffn_in", h, dgu, wc, 4)
    return dx_in, dg, dwin, dwout.reshape(N_CHIPS, -1, D)


GROUP_MID = ("w_o", "ffn2_w_in", "ffn2_w_out", "w_ple_gate", "w_ple_proj")
GROUP_LAST = ("w_qkv", "ffn1_w_in", "ffn1_w_out")
GATHER_L0 = (("a", ("ffn1_w_in",)), ("b", ("ffn1_w_out",)), ("c", ("w_qkv", "w_o")),
             ("d", ("ffn2_w_in", "ffn2_w_out", "w_ple_gate", "w_ple_proj")))


def _gather_start(tag, slotted, after):
    return _split_start("ag_start_" + tag, slotted, _gather_copies([a.shape for a in slotted]), 3 * len(slotted), after)


def _gather_finish(tag, started, after):
    send_s, recv_s, arrays, _ = started
    return _pair_forward(_split_wait("ag_wait_" + tag, send_s, recv_s, arrays,
                                     _gather_copies([a.shape for a in arrays]), after))


def _scatter_exchange(tag, grads):
    n = len(grads)
    land = [lax.empty((g_.shape[0], g_.shape[1] // 2, g_.shape[2]), g_.dtype) for g_ in grads]
    return _split_start("rs_px_start_" + tag, list(grads) + land, _pair_exchange_copies([g_.shape for g_ in grads]),
                        n, None)


def _scatter_chips(tag, started, after):
    send_s, recv_s, arrays, _ = started
    n = len(arrays) // 2
    arrays = _split_wait("rs_px_wait_" + tag, send_s, recv_s, arrays,
                         _pair_exchange_copies([a.shape for a in arrays[:n]]), after)
    part = [_rs_add_pair(g_, r_) for g_, r_ in zip(arrays[:n], arrays[n:])]
    land = [lax.empty((3,) + p_.shape[1:], p_.dtype) for p_ in part]
    return _split_start("rs_start_" + tag, part + land, _scatter_copies(n), 3 * n, None)


def _scatter_share(tag, started, after):
    send_s, recv_s, arrays, _ = started
    n = len(arrays) // 2
    arrays = _split_wait("rs_wait_" + tag, send_s, recv_s, arrays, _scatter_copies(n), after)
    halves = [_rs_add_chips(p_, r_) for p_, r_ in zip(arrays[:n], arrays[n:])]
    return _split_start("rs_ps_start_" + tag, halves, _pair_share_copies([h_.shape for h_ in halves]), n, None)


def _scatter_done(tag, started, after):
    send_s, recv_s, arrays, _ = started
    return _split_wait("rs_ps_wait_" + tag, send_s, recv_s, arrays, _pair_share_copies([a.shape for a in arrays]), after)


def kernel(x, p, rel_bias, norm_ffn1, ffn1_w_in, ffn1_w_out, norm_mix, w_qkv, q_norm_a, k_norm_a, q_norm_b, k_norm_b, sink_b, w_o, norm_ffn2, ffn2_w_in, ffn2_w_out, norm_ple, w_ple_gate, w_ple_proj, loss_target, m_rel_bias, m_norm_ffn1, m_ffn1_w_in, m_ffn1_w_out, m_norm_mix, m_w_qkv, m_q_norm_a, m_k_norm_a, m_q_norm_b, m_k_norm_b, m_sink_b, m_w_o, m_norm_ffn2, m_ffn2_w_in, m_ffn2_w_out, m_norm_ple, m_w_ple_gate, m_w_ple_proj, v_rel_bias, v_norm_ffn1, v_ffn1_w_in, v_ffn1_w_out, v_norm_mix, v_w_qkv, v_q_norm_a, v_k_norm_a, v_q_norm_b, v_k_norm_b, v_sink_b, v_w_o, v_norm_ffn2, v_ffn2_w_in, v_ffn2_w_out, v_norm_ple, v_w_ple_gate, v_w_ple_proj):
    given = dict(locals())
    T, D = x.shape[1], x.shape[2]
    NL = norm_ffn1.shape[0]
    x0 = x.reshape(T, D)
    tgt = loss_target.reshape(T, D)
    n_a, n_b, n_kv = _qkv_layout(D)

    assert NL == 2
    slot = [{name: _cast_into_slot(given[name], l) for name in BIG} for l in range(NL)]
    ag, token = {}, None
    for tag, names in GATHER_L0:
        ag[tag] = _gather_start(tag, [slot[0][n] for n in names], token)
        token = ag[tag][3]
    ag_1 = _gather_start("1", [slot[1][n] for n in BIG], token)

    def arrived(tag, after):
        return dict(zip(dict(GATHER_L0)[tag], _gather_finish(tag, ag[tag], after)))

    def by_rows(a):
        return a.reshape(-1, a.shape[-1])

    def by_cols(a):
        return a.transpose(1, 0, 2).reshape(a.shape[1], -1)

    QW = N_CHIPS * w_qkv.shape[2]

    dils = tuple(d for _, d in DILATED_CONFIGS)
    cfg_a = [(w // (2 * d), d) for w, d in DILATED_CONFIGS]
    pad_a, pad_b = _window_pad(cfg_a[0][0]), _window_pad(SWA_RADIUS)
    bmaps_a, bmaps_b = jnp.asarray(_bucket_maps(cfg_a)), jnp.asarray(_bucket_maps([(SWA_RADIUS, 1)]))
    n_heads = rel_bias.shape[1] // 2
    bias_a = _bias_build(rel_bias, bmaps_a, 0, pad_a)
    bias_b = _bias_build(rel_bias, bmaps_b, n_heads, pad_b)
    no_sink = jnp.full((n_heads,), NEG, F32)

    def gains_row(l):
        ones = jnp.ones((n_a * LANES,), F32)
        return jnp.concatenate([
            jnp.tile(q_norm_a[l], 2 * n_a), jnp.tile(k_norm_a[l], 2 * n_a), ones,
            jnp.tile(q_norm_b[l], 2 * n_b), jnp.tile(k_norm_b[l], 2 * n_kv), jnp.ones((n_kv * LANES,), F32)]).reshape(1, QW)

    def forward_start(tag, state, after):
        landed = _split_wait("ag_wait_" + tag, state[0], state[1], state[2],
                             _gather_copies([a.shape for a in state[2]]), after)
        return _split_start("ag_pf_start_" + tag, landed, _forward_copies([a.shape for a in landed]), 3 * len(landed),
                            None)

    def forward_done(tag, names, pf, after):
        shapes = [a.shape for a in pf[2]]
        return dict(zip(names, _split_wait("ag_pf_wait_" + tag, pf[0], pf[1], pf[2], _forward_copies(shapes), after)))

    saved, weights = [], []
    xc = x0
    pf_1 = None
    for l in range(NL):
        s, w = {}, {}
        if l == 0:
            w.update(arrived("a", ag_1[3]))
            pf = {}

            def ffn1_w_out(act, w=w, pf=pf):
                w.update(arrived("b", act))
                w["ffn1_w_out"] = by_rows(w["ffn1_w_out"])
                pf["c"] = forward_start("c", ag["c"], w["ffn1_w_out"])
                return w["ffn1_w_out"], pf["c"][3]
        else:
            w.update(forward_done("1", BIG, pf_1, xc))
            ffn1_w_out = w["ffn1_w_out"] = by_rows(w["ffn1_w_out"])
        xc, s["ffn1"] = _ffn_forward(xc, norm_ffn1[l:l + 1], w["ffn1_w_in"], ffn1_w_out)
        s["x1"] = xc
        if l == 0:
            w.update(forward_done("c", dict(GATHER_L0)["c"], pf["c"], xc))
        w["w_qkv"] = by_cols(w["w_qkv"])
        w["w_o"] = by_rows(w["w_o"])
        s["h2"], s["raw"], s["qkv_a"], s["qkv_b"] = _qkv_proj(xc, norm_mix[l:l + 1], w["w_qkv"], gains_row(l))
        after_qkv = s["qkv_a"]
        if l == 0:
            pf["d"] = forward_start("d", ag["d"], s["qkv_a"])
            after_qkv = pf["d"][3]
        s["o_a"], s["lse_a"] = _attn_fwd(s["qkv_a"], bias_a, no_sink, dils, pad_a, after_qkv)
        s["o_b"], s["lse_b"] = _attn_fwd(s["qkv_b"], bias_b, sink_b[l], (1,), pad_b, s["qkv_b"])
        xc = _mm_res("attn_out", xc, [s["o_a"], s["o_b"]], w["w_o"], 1.0, s["o_b"])
        if l == 0:
            w.update(forward_done("d", dict(GATHER_L0)["d"], pf["d"], xc))
        w["w_ple_proj"] = by_cols(w["w_ple_proj"])
        for name in ("ffn2_w_out", "w_ple_gate"):
            w[name] = by_rows(w[name])
        xc, s["ffn2"] = _ffn_forward(xc, norm_ffn2[l:l + 1], w["ffn2_w_in"], w["ffn2_w_out"])
        s["x3"] = xc
        if l == 0:
            pf_1 = forward_start("1", ag_1, xc)
        s["hn"], xc, s["gate"], s["pp"] = _ple_fwd(xc, norm_ple[l:l + 1], p, l, w["w_ple_gate"], w["w_ple_proj"],
                                                   pf_1[3] if l == 0 else xc)
        saved.append(s)
        weights.append(w)

    dx, loss_blk = _loss_fwd_bwd(xc, tgt)
    loss = lax.psum(loss_blk[0, 0], ("x", "y", "c"))

    gs = {name: [None] * NL for name in SMALL if name != "rel_bias"}
    dt_a, dt_b = [], []

    def layer_backward(l, dx, hooks):
        def at(point, ready, *more):
            return hooks[point](ready, *more) if point in hooks else ready

        s, w, gw = saved[l], weights[l], {}
        dx, gs["norm_ple"][l], dwg, dwp = _ple_bwd(dx, s["gate"], s["pp"], s["hn"], p, l, s["x3"],
                                                   norm_ple[l:l + 1], w["w_ple_gate"], at("start", dx))
        gw["w_ple_gate"] = dwg.reshape(N_CHIPS, -1, D)
        gw["w_ple_proj"] = dwp.reshape(dwp.shape[0], N_CHIPS, -1).transpose(1, 0, 2)
        dx, gs["norm_ffn2"][l], gw["ffn2_w_in"], gw["ffn2_w_out"] = _ffn_backward(
            dx, s["ffn2"], norm_ffn2[l:l + 1], w["ffn2_w_in"], w["ffn2_w_out"], at("after_ple", dx))
        do, dd, dwo = _mix_bwd_in(dx, w["w_o"], [s["o_a"], s["o_b"]])
        hwa = s["o_a"].shape[1]
        gw["w_o"] = dwo.reshape(N_CHIPS, -1, D)
        dqa, dka, dva, dt, _ = _attn_bwd(s["qkv_a"], bias_a, no_sink, dils, pad_a, do, s["lse_a"], dd, 0, do)
        dt_a.append(dt)
        dqb, dkb, dvb, dt, dsink = _attn_bwd(s["qkv_b"], bias_b, sink_b[l], (1,), pad_b, do, s["lse_b"], dd,
                                             hwa // LANES, at("after_attn_a", dqa))
        dt_b.append(dt)
        gs["sink_b"][l] = dsink[:, 0, ::HEAD_DIM].reshape(-1)
        dx, gs["norm_mix"][l], dgains, dwqkv = _qkv_bwd(
            s["raw"], gains_row(l), (dqa, dka, dva), (dqb, dkb, dvb), w["w_qkv"], s["h2"], s["x1"],
            norm_mix[l:l + 1], dx, at("before_qkv_bwd", dqb, [gw[n] for n in GROUP_MID]))
        dgv = dgains.reshape(-1, HEAD_DIM)
        gs["q_norm_a"][l] = dgv[:2 * n_a].sum(0)
        gs["k_norm_a"][l] = dgv[2 * n_a:4 * n_a].sum(0)
        gs["q_norm_b"][l] = dgv[6 * n_a:6 * n_a + 2 * n_b].sum(0)
        gs["k_norm_b"][l] = dgv[6 * n_a + 2 * n_b:6 * n_a + 2 * n_b + 2 * n_kv].sum(0)
        gw["w_qkv"] = dwqkv.reshape(D, N_CHIPS, -1).transpose(1, 0, 2)
        dx, gs["norm_ffn1"][l], gw["ffn1_w_in"], gw["ffn1_w_out"] = _ffn_backward(
            dx, s["ffn1"], norm_ffn1[l:l + 1], w["ffn1_w_in"], w["ffn1_w_out"], at("before_ffn1", dx))
        return dx, gw

    out = {}

    def adamw_group(names, l, grads, after):
        for name, g_ in zip(names, grads):
            out[name] = _adamw_layer(given[name], g_, given["m_" + name], given["v_" + name], l, out.get(name), after)
            after = out[name][0]
        return after

    dx, gw1 = layer_backward(NL - 1, dx, {})
    px_1 = _scatter_exchange("1", [gw1[n] for n in BIG])
    rs = {}

    def chips_1(ready):
        rs["chips_1"] = _scatter_chips("1", px_1, ready)
        return rs["chips_1"][3]

    def share_1(ready):
        rs["share_1"] = _scatter_share("1", rs["chips_1"], ready)
        return rs["share_1"][3]

    def exchange_0a(ready, grads):
        rs["px_0a"] = _scatter_exchange("0a", grads)
        return rs["px_0a"][3]

    def chips_0a(ready):
        rs["chips_0a"] = _scatter_chips("0a", rs["px_0a"], ready)
        return rs["chips_0a"][3]

    dx, gw0 = layer_backward(0, dx, {"start": lambda ready: px_1[3], "after_ple": chips_1, "after_attn_a": share_1,
                                     "before_qkv_bwd": exchange_0a, "before_ffn1": chips_0a})
    grad_x = dx.reshape(x.shape)

    px_0b = _scatter_exchange("0b", [gw0[n] for n in GROUP_LAST])
    d_rel_bias = (_bias_grad(dt_a, bmaps_a, 0) + _bias_grad(dt_b, bmaps_b, n_heads))[:, :rel_bias.shape[1]]
    small_g = [d_rel_bias] + [jnp.stack([t.reshape(-1) for t in gs[name]]) for name in SMALL[1:]]
    g_sum = _allreduce_small(_pack_small(small_g), px_0b[3])
    res = _ew("adamw_small", _adamw_fn,
              [_pack_small([given[n] for n in SMALL]), g_sum, _pack_small([given["m_" + n] for n in SMALL]),
               _pack_small([given["v_" + n] for n in SMALL])], [(LANES, F32)] * 4)
    like = [given[n] for n in SMALL]
    unpacked = [_unpack_small(r, like) for r in res]
    for i, name in enumerate(SMALL):
        out[name] = [u[i] for u in unpacked]

    share_0a = _scatter_share("0a", rs["chips_0a"], res[0])
    g_1 = _scatter_done("1", rs["share_1"], share_0a[3])
    chips_0b = _scatter_chips("0b", px_0b, g_1[0])
    ready = adamw_group(BIG, 1, g_1, chips_0b[3])
    ready = adamw_group(GROUP_MID, 0, _scatter_done("0a", share_0a, ready), ready)
    share_0b = _scatter_share("0b", chips_0b, ready)
    adamw_group(GROUP_LAST, 0, _scatter_done("0b", share_0b, share_0b[3]), share_0b[3])

    return (loss, grad_x, *[out[n][0] for n in WEIGHTS], *[out[n][1] for n in WEIGHTS],
            *[out[n][2] for n in WEIGHTS], *[out[n][3] for n in WEIGHTS])
```

```python
import functools
import math

import numpy as np
import jax
import jax.numpy as jnp
from jax import lax
from jax.experimental import pallas as pl
from jax.experimental.pallas import tpu as pltpu

F32 = jnp.float32
BF16 = jnp.bfloat16
MESH = pl.DeviceIdType.MESH

HEAD_DIM = 64
LANES = 128
QBLOCK = 128
FWD_UNROLL, BWD_UNROLL = 16, 16
QKV_CHUNK = 6
N_BUCKETS = 32
MAX_DISTANCE = 1024
DILATED_CONFIGS = ((128, 1), (512, 4), (2048, 16))
SWA_RADIUS = 128
GROUP_B = 4
EPS = 1e-6
NEG = -1e30
Q_SCALE = HEAD_DIM ** -0.5
ADAM_LR, ADAM_B1, ADAM_B2, ADAM_EPS, ADAM_WD, ADAM_STEP = 0.001, 0.9, 0.999, 1e-08, 0.01, 10
VMEM_LIMIT = 56 * 2 ** 20
N_CHIPS = 4
N_DEV = 8

BIG = ("ffn1_w_in", "ffn1_w_out", "w_qkv", "w_o", "ffn2_w_in", "ffn2_w_out", "w_ple_gate", "w_ple_proj")
SMALL = ("rel_bias", "norm_ffn1", "norm_mix", "q_norm_a", "k_norm_a", "q_norm_b", "k_norm_b", "sink_b",
         "norm_ffn2", "norm_ple")
WEIGHTS = ("rel_bias", "norm_ffn1", "ffn1_w_in", "ffn1_w_out", "norm_mix", "w_qkv", "q_norm_a", "k_norm_a",
           "q_norm_b", "k_norm_b", "sink_b", "w_o", "norm_ffn2", "ffn2_w_in", "ffn2_w_out", "norm_ple",
           "w_ple_gate", "w_ple_proj")


HBM_SPEC = pl.BlockSpec(memory_space=pltpu.HBM)
ANY_SPEC = pl.BlockSpec(memory_space=pl.ANY)
SEM_SPEC = pl.BlockSpec(memory_space=pltpu.SEMAPHORE)


def _params(n_grid):
    return pltpu.CompilerParams(dimension_semantics=("arbitrary",) * n_grid, vmem_limit_bytes=VMEM_LIMIT)


def _pick_tm(rows, cap):
    t = (min(cap, rows) // 16) * 16
    while t >= 16:
        if rows % t == 0:
            return t
        t -= 16
    return rows


def _dot(a, b):
    return jnp.dot(a, b, preferred_element_type=F32)


def _dot_nt(a, b):
    return lax.dot_general(a, b, (((1,), (1,)), ((), ())), preferred_element_type=F32)


def _dot_tn(a, b):
    return lax.dot_general(a, b, (((0,), (0,)), ((), ())), preferred_element_type=F32)


def _sigmoid(z):
    return 1.0 / (1.0 + jnp.exp(-z))


def _lo_lanes(shape):
    return lax.broadcasted_iota(jnp.int32, shape, len(shape) - 1) % LANES < HEAD_DIM


def _seg_sum(blk):
    lo = _lo_lanes(blk.shape)
    s_lo = jnp.sum(jnp.where(lo, blk, 0.0), axis=1, keepdims=True)
    s_hi = jnp.sum(jnp.where(lo, 0.0, blk), axis=1, keepdims=True)
    return jnp.where(lo, s_lo, s_hi)


def _rms_bwd_tile(x, g, dh):
    r = lax.rsqrt(jnp.mean(x * x, axis=-1, keepdims=True) + EPS)
    xh = x * r
    dyg = dh * g
    dx = r * (dyg - xh * jnp.mean(dyg * xh, axis=-1, keepdims=True))
    return dx, jnp.sum(dh * xh, axis=0, keepdims=True)


def _ew(name, fn, ins, out_defs, cap=512):
    rows = ins[0].shape[0]
    tm = _pick_tm(rows, cap)
    n_in = len(ins)

    def body(*refs):
        vals = fn(*[r[...] for r in refs[:n_in]])
        if not isinstance(vals, tuple):
            vals = (vals,)
        for r, v in zip(refs[n_in:], vals):
            r[...] = v.astype(r.dtype)

    return pl.pallas_call(
        body, name=name, grid=(rows // tm,),
        in_specs=[pl.BlockSpec((tm, a.shape[1]), lambda i: (i, 0)) for a in ins],
        out_specs=[pl.BlockSpec((tm, c), lambda i: (i, 0)) for c, _ in out_defs],
        out_shape=[jax.ShapeDtypeStruct((rows, c), dt) for c, dt in out_defs],
        compiler_params=_params(1))(*ins)


def _rms_tile(xv, gv):
    r = lax.rsqrt(jnp.mean(xv * xv, axis=-1, keepdims=True) + EPS)
    return (xv * r * gv).astype(BF16)


def _ffn_up(x, g, win):
    T, D = x.shape
    wc = win.shape[2]
    tm = _pick_tm(T, 512)

    def body(x_ref, g_ref, wg_ref, wu_ref, h_ref, silu_ref, dgate_ref, act_ref, wcat):
        @pl.when(pl.program_id(1) == 0)
        def _():
            wcat[:, :wc] = wg_ref[...]
            wcat[:, wc:] = wu_ref[...]

        hv = _rms_tile(x_ref[...], g_ref[...])
        h_ref[...] = hv
        gu = _dot(hv, wcat[...])
        gte, u = gu[:, :wc], gu[:, wc:]
        sg = _sigmoid(gte)
        silu = gte * sg
        silu_ref[...] = silu.astype(BF16)
        dgate_ref[...] = ((sg + silu * (1.0 - sg)) * u).astype(BF16)
        act_ref[...] = (silu * u).astype(BF16)

    out = jax.ShapeDtypeStruct((T, 2 * wc), BF16)
    ospec = pl.BlockSpec((tm, wc), lambda j, i: (i, j))
    nt = T // tm
    h_spec = pl.BlockSpec((tm, D), lambda j, i: (jnp.where(j == 0, i, nt), 0))
    return pl.pallas_call(
        body, name="ffn_up", grid=(2, nt),
        in_specs=[pl.BlockSpec((tm, D), lambda j, i: (i, 0)), pl.BlockSpec((1, D), lambda j, i: (0, 0)),
                  pl.BlockSpec((None, D, wc), lambda j, i: (j, 0, 0)),
                  pl.BlockSpec((None, D, wc), lambda j, i: (j + 2, 0, 0))],
        out_specs=[h_spec] + [ospec] * 3, out_shape=[jax.ShapeDtypeStruct((T + tm, D), BF16)] + [out] * 3,
        scratch_shapes=[pltpu.VMEM((D, 2 * wc), BF16)], compiler_params=_params(2))(x, g, win, win)


def _mm_res(name, res, a_list, w, scale, after):
    T, N = res.shape
    n = len(a_list)
    widths = [a.shape[1] for a in a_list]
    tm = _pick_tm(T, 512)

    def body(*refs):
        r_ref, a_refs, w_refs, o_ref = refs[0], refs[1:1 + n], refs[1 + n:1 + 2 * n], refs[2 + 2 * n]
        acc = _dot(a_refs[0][...].astype(BF16), w_refs[0][...])
        for a_ref, w_ref in zip(a_refs[1:], w_refs[1:]):
            acc = acc + _dot(a_ref[...].astype(BF16), w_ref[...])
        o_ref[...] = r_ref[...] + scale * acc

    w_specs, off = [], 0
    for k in widths:
        w_specs.append(pl.BlockSpec((k, N), lambda i, blk=off // k: (blk, 0)))
        off += k
    return pl.pallas_call(
        body, name=name, grid=(T // tm,),
        in_specs=[pl.BlockSpec((tm, N), lambda i: (i, 0))]
        + [pl.BlockSpec((tm, k), lambda i: (i, 0)) for k in widths] + w_specs + [ANY_SPEC],
        out_specs=pl.BlockSpec((tm, N), lambda i: (i, 0)),
        out_shape=jax.ShapeDtypeStruct((T, N), F32),
        compiler_params=_params(1))(res, *a_list, *([w] * n), after)


def _mm_tn(name, a, a_w, b, b_w, n_slots, a_by_slot, b_by_slot, scale, tm_cap=512):
    T = b.shape[0]
    tm = _pick_tm(T, tm_cap)
    nt = T // tm

    def body(a_ref, b_ref, o_ref, acc):
        i = pl.program_id(1)

        @pl.when(i == 0)
        def _():
            acc[...] = jnp.zeros_like(acc)

        acc[...] += _dot_tn(a_ref[...].astype(BF16), b_ref[...].astype(BF16))

        @pl.when(i == nt - 1)
        def _():
            o_ref[...] = (acc[...] * scale).astype(BF16)

    return pl.pallas_call(
        body, name=name, grid=(n_slots, nt),
        in_specs=[pl.BlockSpec((tm, a_w), (lambda s, i: (i, s)) if a_by_slot else (lambda s, i: (i, 0))),
                  pl.BlockSpec((tm, b_w), (lambda s, i: (i, s)) if b_by_slot else (lambda s, i: (i, 0)))],
        out_specs=pl.BlockSpec((None, a_w, b_w), lambda s, i: (s, 0, 0)),
        out_shape=jax.ShapeDtypeStruct((n_slots, a_w, b_w), BF16),
        scratch_shapes=[pltpu.VMEM((a_w, b_w), F32)], compiler_params=_params(2))(a, b)


def _mm_tn_pairs(name, a, b, b_w, n_slots):
    T = b.shape[0]
    a_w = a.shape[1]
    tm = _pick_tm(T, 512)
    nt = T // tm

    def body(a_ref, b_ref, o_ref, acc):
        i = pl.program_id(1)

        @pl.when(i == 0)
        def _():
            acc[...] = jnp.zeros_like(acc)

        acc[...] += _dot_tn(a_ref[...], b_ref[...])

        @pl.when(i == nt - 1)
        def _():
            o_ref[0] = acc[:, :b_w].astype(BF16)
            o_ref[1] = acc[:, b_w:].astype(BF16)

    return pl.pallas_call(
        body, name=name, grid=(n_slots // 2, nt),
        in_specs=[pl.BlockSpec((tm, a_w), lambda s, i: (i, 0)), pl.BlockSpec((tm, 2 * b_w), lambda s, i: (i, s))],
        out_specs=pl.BlockSpec((2, a_w, b_w), lambda s, i: (s, 0, 0)),
        out_shape=jax.ShapeDtypeStruct((n_slots, a_w, b_w), BF16),
        scratch_shapes=[pltpu.VMEM((a_w, 2 * b_w), F32)], compiler_params=_params(2))(a, b)


def _ffn_bwd(dx, wout, silu, dgate, win, x, g, after):
    T, D = dx.shape
    F = silu.shape[1]
    n_slots, _, wc = win.shape
    tm = _pick_tm(T, 256)

    def body(dx_ref, wout_hbm, s_ref, dgt_ref, win_hbm, x_ref, g_ref, after_ref,
             dgu_ref, dxo_ref, dg_ref, wout_v, wcat, sem):
        @pl.when(pl.program_id(0) == 0)
        def _():
            copies = [pltpu.make_async_copy(wout_hbm, wout_v, sem.at[n_slots])]
            copies += [pltpu.make_async_copy(win_hbm.at[s], wcat.at[s // 2, :, pl.ds((s % 2) * wc, wc)], sem.at[s])
                       for s in range(n_slots)]
            for cp in copies:
                cp.start()
            for cp in copies:
                cp.wait()
            dg_ref[...] = jnp.zeros_like(dg_ref)

        dxv = dx_ref[...]
        dact = 0.5 * _dot_nt(dxv.astype(BF16), wout_v[...])
        d_gate = (dact * dgt_ref[...].astype(F32)).astype(BF16)
        d_up = (dact * s_ref[...].astype(F32)).astype(BF16)
        dgu_ref[:, :F] = d_gate
        dgu_ref[:, F:] = d_up
        dh = _dot_nt(d_gate, wcat[0]) + _dot_nt(d_up, wcat[1])
        dxn, dg = _rms_bwd_tile(x_ref[...], g_ref[...], dh)
        dxo_ref[...] = dxv + dxn
        dg_ref[...] += dg

    row = pl.BlockSpec((tm, D), lambda i: (i, 0))
    vec = pl.BlockSpec((1, D), lambda i: (0, 0))
    act_spec = pl.BlockSpec((tm, F), lambda i: (i, 0))
    return pl.pallas_call(
        body, name="ffn_bwd", grid=(T // tm,),
        in_specs=[row, ANY_SPEC, act_spec, act_spec, ANY_SPEC, row, vec, ANY_SPEC],
        out_specs=[pl.BlockSpec((tm, 2 * F), lambda i: (i, 0)), row, vec],
        out_shape=[jax.ShapeDtypeStruct((T, 2 * F), BF16), jax.ShapeDtypeStruct((T, D), F32),
                   jax.ShapeDtypeStruct((1, D), F32)],
        scratch_shapes=[pltpu.VMEM((F, D), BF16), pltpu.VMEM((n_slots // 2, D, 2 * wc), BF16),
                        pltpu.SemaphoreType.DMA((n_slots + 1,))],
        compiler_params=_params(1))(dx, wout, silu, dgate, win, x, g, after)


def _ple_fwd(x, g, p, layer, wg, wp, after):
    T, D = x.shape
    P = p.shape[-1]
    tm = _pick_tm(T, 256)

    def body(x_ref, g_ref, p_ref, wg_ref, wp_ref, after_ref, hn_ref, xo_ref, gate_ref, pp_ref):
        xv = x_ref[...]
        hn = _rms_tile(xv, g_ref[...])
        hn_ref[...] = hn
        gate = _sigmoid(_dot(hn, wg_ref[...]))
        pp = _dot(p_ref[...].astype(BF16), wp_ref[...])
        gate_ref[...] = gate.astype(BF16)
        pp_ref[...] = pp.astype(BF16)
        xo_ref[...] = xv + gate * pp

    row = pl.BlockSpec((tm, D), lambda i: (i, 0))
    out = jax.ShapeDtypeStruct((T, D), F32)
    half = jax.ShapeDtypeStruct((T, D), BF16)
    return pl.pallas_call(
        body, name="ple_fwd", grid=(T // tm,),
        in_specs=[row, pl.BlockSpec((1, D), lambda i: (0, 0)), pl.BlockSpec((None, None, tm, P), lambda i: (layer, 0, i, 0)),
                  pl.BlockSpec((D, D), lambda i: (0, 0)), pl.BlockSpec((P, D), lambda i: (0, 0)), ANY_SPEC],
        out_specs=[row, row, row, row], out_shape=[half, out, half, half],
        compiler_params=_params(1))(x, g, p, wg, wp, after)


def _ple_bwd(dx, gate, pp, hn, p, layer, x, g, wg, after):
    T, D = x.shape
    P = p.shape[-1]
    tm = _pick_tm(T, 256)
    nt = T // tm

    def body(dx_ref, gate_ref, pp_ref, hn_ref, p_ref, x_ref, g_ref, wg_ref, after_ref,
             dxo_ref, dg_ref, dwg_ref, dwp_ref, acc_g, acc_p):
        i = pl.program_id(0)

        @pl.when(i == 0)
        def _():
            acc_g[...] = jnp.zeros_like(acc_g)
            acc_p[...] = jnp.zeros_like(acc_p)
            dg_ref[...] = jnp.zeros_like(dg_ref)

        dxv = dx_ref[...]
        gate = gate_ref[...].astype(F32)
        dz = (dxv * pp_ref[...].astype(F32) * gate * (1.0 - gate)).astype(BF16)
        dpp = (dxv * gate).astype(BF16)
        acc_g[...] += _dot_tn(hn_ref[...], dz)
        acc_p[...] += _dot_tn(p_ref[...].astype(BF16), dpp)
        dxn, dg = _rms_bwd_tile(x_ref[...], g_ref[...], _dot_nt(dz, wg_ref[...]))
        dxo_ref[...] = dxv + dxn
        dg_ref[...] += dg

        @pl.when(i == nt - 1)
        def _():
            dwg_ref[...] = acc_g[...].astype(BF16)
            dwp_ref[...] = acc_p[...].astype(BF16)

    row = pl.BlockSpec((tm, D), lambda i: (i, 0))
    vec = pl.BlockSpec((1, D), lambda i: (0, 0))
    return pl.pallas_call(
        body, name="ple_bwd", grid=(nt,),
        in_specs=[row, row, row, row, pl.BlockSpec((None, None, tm, P), lambda i: (layer, 0, i, 0)), row, vec,
                  pl.BlockSpec((D, D), lambda i: (0, 0)), ANY_SPEC],
        out_specs=[row, vec, pl.BlockSpec((D, D), lambda i: (0, 0)), pl.BlockSpec((P, D), lambda i: (0, 0))],
        out_shape=[jax.ShapeDtypeStruct((T, D), F32), jax.ShapeDtypeStruct((1, D), F32),
                   jax.ShapeDtypeStruct((D, D), BF16), jax.ShapeDtypeStruct((P, D), BF16)],
        scratch_shapes=[pltpu.VMEM((D, D), F32), pltpu.VMEM((P, D), F32)],
        compiler_params=_params(1))(dx, gate, pp, hn, p, x, g, wg, after)


def _loss_fwd_bwd(y, tgt):
    T, D = y.shape
    tm = _pick_tm(T, 512)

    def body(y_ref, t_ref, dy_ref, loss_ref):
        e = y_ref[...] - t_ref[...]
        dy_ref[...] = e / D

        @pl.when(pl.program_id(0) == 0)
        def _():
            loss_ref[...] = jnp.zeros_like(loss_ref)

        loss_ref[...] += 0.5 * jnp.sum(jnp.mean(e * e, axis=-1, keepdims=True), axis=0, keepdims=True)

    row = pl.BlockSpec((tm, D), lambda i: (i, 0))
    return pl.pallas_call(
        body, name="loss", grid=(T // tm,), in_specs=[row, row],
        out_specs=[row, pl.BlockSpec((8, LANES), lambda i: (0, 0))],
        out_shape=[jax.ShapeDtypeStruct((T, D), F32), jax.ShapeDtypeStruct((8, LANES), F32)],
        compiler_params=_params(1))(y, tgt)


def _qkv_layout(D):
    n_a = D // (2 * LANES)
    n_b = D // (2 * LANES)
    n_kv = max(1, (2 * n_b) // GROUP_B) * HEAD_DIM // LANES
    return n_a, n_b, n_kv


def _dup_half(xv, half):
    rolled = pltpu.roll(xv, HEAD_DIM, 1)
    lo = _lo_lanes(xv.shape)
    return jnp.where(lo, xv, rolled) if half == 0 else jnp.where(lo, rolled, xv)


def _qkv_proj(x, g_mix, w, gains):
    T, D = x.shape
    W = w.shape[1]
    n_a, n_b, n_kv = _qkv_layout(D)
    tm = _pick_tm(T, 256)
    o_qb = 3 * n_a

    def norm(xv, gv, scale):
        ms = _seg_sum(xv * xv) * (1.0 / HEAD_DIM)
        return xv * lax.rsqrt(ms + EPS) * gv * scale

    def body(x_ref, gm_ref, w_ref, g_ref, h_ref, raw_ref, a_ref, b_ref):
        hv = _rms_tile(x_ref[...], gm_ref[...])
        h_ref[...] = hv
        raw = _dot(hv, w_ref[...])
        raw_ref[...] = raw

        def blk(cb):
            return raw[:, cb * LANES:(cb + 1) * LANES]

        def gn(cb):
            return g_ref[:, cb * LANES:(cb + 1) * LANES]

        for cb in range(n_a):
            a_ref[:, cb * LANES:(cb + 1) * LANES] = norm(blk(cb), gn(cb), Q_SCALE)
            cbk = n_a + cb
            a_ref[:, cbk * LANES:(cbk + 1) * LANES] = norm(blk(cbk), gn(cbk), 1.0)
            cbv = 2 * n_a + cb
            a_ref[:, cbv * LANES:(cbv + 1) * LANES] = blk(cbv)
        for cb in range(n_b):
            src = o_qb + cb
            b_ref[:, cb * LANES:(cb + 1) * LANES] = norm(blk(src), gn(src), Q_SCALE)
        for e in range(n_b):
            kvh = (2 * e) // GROUP_B
            ck = o_qb + n_b + kvh // 2
            cv = ck + n_kv
            kn = norm(blk(ck), gn(ck), 1.0)
            b_ref[:, (n_b + e) * LANES:(n_b + e + 1) * LANES] = _dup_half(kn, kvh % 2)
            b_ref[:, (2 * n_b + e) * LANES:(2 * n_b + e + 1) * LANES] = _dup_half(blk(cv), kvh % 2)

    wa, wb = 3 * n_a * LANES, 3 * n_b * LANES

    def rows(width):
        return pl.BlockSpec((tm, width), lambda i: (i, 0))

    return pl.pallas_call(
        body, name="qkv_proj", grid=(T // tm,),
        in_specs=[rows(D), pl.BlockSpec((1, D), lambda i: (0, 0)), pl.BlockSpec((D, W), lambda i: (0, 0)),
                  pl.BlockSpec((1, W), lambda i: (0, 0))],
        out_specs=[rows(D), rows(W), rows(wa), rows(wb)],
        out_shape=[jax.ShapeDtypeStruct((T, D), BF16), jax.ShapeDtypeStruct((T, W), F32),
                   jax.ShapeDtypeStruct((T, wa), F32), jax.ShapeDtypeStruct((T, wb), F32)],
        compiler_params=_params(1))(x, g_mix, w, gains)


def _qkv_bwd(raw, gains, d_a, d_b, w, h, x, g_mix, dx_in, after):
    T, W = raw.shape
    D = x.shape[1]
    n_a, n_b, n_kv = _qkv_layout(D)
    tm = _pick_tm(T, 256)
    nt = T // tm
    o_qb = 3 * n_a

    def body(raw_ref, g_ref, daq, dak, dav, dbq, dbk, dbv, w_ref, h_ref, x_ref, gm_ref, dxi_ref, after_ref,
             dx_ref, dgm_ref, dg_ref, dw_ref, acc):
        i = pl.program_id(0)

        @pl.when(i == 0)
        def _():
            dg_ref[...] = jnp.zeros_like(dg_ref)
            dgm_ref[...] = jnp.zeros_like(dgm_ref)
            acc[...] = jnp.zeros_like(acc)

        def cols(ref, cb):
            return ref[:, cb * LANES:(cb + 1) * LANES]

        pieces = [None] * (W // LANES)

        def norm_bwd(cb, dy, scale):
            xv = cols(raw_ref, cb)
            gv = cols(g_ref, cb)
            r = lax.rsqrt(_seg_sum(xv * xv) * (1.0 / HEAD_DIM) + EPS)
            xh = xv * r
            dys = dy * scale
            dyg = dys * gv
            dxv = r * (dyg - xh * (_seg_sum(dyg * xh) * (1.0 / HEAD_DIM)))
            pieces[cb] = dxv.astype(BF16)
            dg_ref[:, cb * LANES:(cb + 1) * LANES] += jnp.sum(dys * xh, axis=0, keepdims=True)

        def fold(ref, kv_blk):
            halves = []
            for half in range(2):
                kvh = 2 * kv_blk + half
                blocks = [e for e in range(n_b) if (2 * e) // GROUP_B == kvh]
                s = cols(ref, blocks[0])
                for e in blocks[1:]:
                    s = s + cols(ref, e)
                halves.append(s + pltpu.roll(s, HEAD_DIM, 1))
            return jnp.where(_lo_lanes(halves[0].shape), halves[0], halves[1])

        for cb in range(n_a):
            norm_bwd(cb, cols(daq, cb), Q_SCALE)
        for cb in range(n_a):
            norm_bwd(n_a + cb, cols(dak, cb), 1.0)
        for cb in range(n_a):
            pieces[2 * n_a + cb] = cols(dav, cb).astype(BF16)
        for cb in range(n_b):
            norm_bwd(o_qb + cb, cols(dbq, cb), Q_SCALE)
        for kb in range(n_kv):
            norm_bwd(o_qb + n_b + kb, fold(dbk, kb), 1.0)
        for kb in range(n_kv):
            pieces[o_qb + n_b + n_kv + kb] = fold(dbv, kb).astype(BF16)

        hv = h_ref[...]
        dh = None
        for c0 in range(0, W // LANES, QKV_CHUNK):
            chunk = jnp.concatenate(pieces[c0:c0 + QKV_CHUNK], axis=1)
            lanes = slice(c0 * LANES, (c0 + QKV_CHUNK) * LANES)
            part = _dot_nt(chunk, w_ref[:, lanes])
            dh = part if dh is None else dh + part
            acc[:, lanes] += _dot_tn(hv, chunk)
        dxn, dgm = _rms_bwd_tile(x_ref[...], gm_ref[...], dh)
        dx_ref[...] = dxi_ref[...] + dxn
        dgm_ref[...] += dgm

        @pl.when(i == nt - 1)
        def _():
            dw_ref[...] = acc[...].astype(BF16)

    hw_a, hw_b = n_a * LANES, n_b * LANES

    def rows(width):
        return pl.BlockSpec((tm, width), lambda i: (i, 0))

    def fixed(r, c):
        return pl.BlockSpec((r, c), lambda i: (0, 0))

    return pl.pallas_call(
        body, name="qkv_bwd", grid=(nt,),
        in_specs=[rows(W), fixed(1, W)] + [rows(hw_a)] * 3 + [rows(hw_b)] * 3
        + [fixed(D, W), rows(D), rows(D), fixed(1, D), rows(D), ANY_SPEC],
        out_specs=[rows(D), fixed(1, D), fixed(1, W), fixed(D, W)],
        out_shape=[jax.ShapeDtypeStruct((T, D), F32), jax.ShapeDtypeStruct((1, D), F32),
                   jax.ShapeDtypeStruct((1, W), F32), jax.ShapeDtypeStruct((D, W), BF16)],
        scratch_shapes=[pltpu.VMEM((D, W), F32)],
        compiler_params=_params(1))(raw, gains, *d_a, *d_b, w, h, x, g_mix, dx_in, after)


def _t5_bucket_np(rel):
    half = N_BUCKETS // 2
    max_exact = half // 2
    ret = np.where(rel > 0, half, 0)
    n = np.abs(rel)
    nf = np.maximum(n, 1).astype(np.float32)
    large = max_exact + (np.log(nf / np.float32(max_exact)) / np.float32(math.log(MAX_DISTANCE / max_exact))
                         * np.float32(half - max_exact)).astype(np.int32)
    large = np.minimum(large, half - 1)
    return ret + np.where(n < max_exact, n, large)


def _window_pad(radius):
    assert radius <= QBLOCK
    return HEAD_DIM if radius <= HEAD_DIM else QBLOCK


def _bucket_maps(configs):
    pad = _window_pad(configs[0][0])
    q = np.arange(QBLOCK)[:, None]
    kk = np.arange(QBLOCK + 2 * pad)[None, :]
    rel = kk - pad - q
    maps = [np.where(np.abs(rel) <= radius, _t5_bucket_np(rel * dil), -1) for radius, dil in configs]
    return np.stack(maps).astype(np.int32)


def _bias_build(rel_bias, bmaps, col0, pad):
    n_sets, _, W = bmaps.shape
    n_buckets, n_cols = rel_bias.shape
    n_heads = n_cols // 2

    def body(rb_ref, bm_ref, o_ref):
        h = pl.program_id(1)
        bm = bm_ref[0:1, :]
        head = lax.broadcasted_iota(jnp.int32, (n_buckets, n_cols), 1) == col0 + h
        per_bucket = jnp.sum(jnp.where(head, rb_ref[...], 0.0), axis=1, keepdims=True)
        hit = lax.broadcasted_iota(jnp.int32, (n_buckets, W), 0) == bm
        row0 = jnp.sum(jnp.where(hit, per_bucket, 0.0), axis=0, keepdims=True) + jnp.where(bm < 0, NEG, 0.0)
        tile = pltpu.roll(jnp.broadcast_to(row0, (QBLOCK, W)), 0, 1, stride=1, stride_axis=0)
        col = lax.broadcasted_iota(jnp.int32, (QBLOCK, W), 1)
        left, right = col < pad, col >= pad + QBLOCK
        o_ref[0] = tile
        o_ref[1] = jnp.where(left, NEG, tile)
        o_ref[2] = jnp.where(right, NEG, tile)
        o_ref[3] = jnp.where(left | right, NEG, tile)

    return pl.pallas_call(
        body, name="bias_build", grid=(n_sets, n_heads),
        in_specs=[pl.BlockSpec((n_buckets, n_cols), lambda s, h: (0, 0)),
                  pl.BlockSpec((None, QBLOCK, W), lambda s, h: (s, 0, 0))],
        out_specs=pl.BlockSpec((None, 4, QBLOCK, W), lambda s, h: (s, 0, h, 0)),
        out_shape=jax.ShapeDtypeStruct((n_sets, 4, n_heads * QBLOCK, W), F32),
        compiler_params=_params(2))(rel_bias, bmaps)


def _bias_grad(dtiles, bmaps, col0):
    n_sets, _, W = bmaps.shape
    n_heads = dtiles[0].shape[1]
    n_l = len(dtiles)

    def body(*refs):
        bm_ref, o_ref = refs[0], refs[1 + n_l]
        s, h = pl.program_id(0), pl.program_id(1)

        @pl.when((s == 0) & (h == 0))
        def _():
            o_ref[...] = jnp.zeros_like(o_ref)

        d = refs[1][...]
        for r in refs[2:1 + n_l]:
            d = d + r[...]
        acc8 = d[0:8, :]
        for a in range(1, QBLOCK // 8):
            acc8 = acc8 + pltpu.roll(d[8 * a:8 * a + 8, :], W - 8 * a, 1)
        per_offset = acc8[0:1, :]
        for b in range(1, 8):
            per_offset = per_offset + pltpu.roll(acc8[b:b + 1, :], W - b, 1)
        bucket = lax.broadcasted_iota(jnp.int32, (N_BUCKETS, W), 0)
        hit = bucket == bm_ref[0:1, :]
        per_bucket = jnp.sum(jnp.where(hit, per_offset, 0.0), axis=1, keepdims=True)
        lanes = lax.broadcasted_iota(jnp.int32, o_ref.shape, 1)
        o_ref[...] += jnp.where(lanes == col0 + h, per_bucket, 0.0)

    tile = pl.BlockSpec((None, None, QBLOCK, W), lambda s, h: (s, h, 0, 0))
    return pl.pallas_call(
        body, name="bias_grad", grid=(n_sets, n_heads),
        in_specs=[pl.BlockSpec((None, QBLOCK, W), lambda s, h: (s, 0, 0))] + [tile] * n_l,
        out_specs=pl.BlockSpec((N_BUCKETS, LANES), lambda s, h: (0, 0)),
        out_shape=jax.ShapeDtypeStruct((N_BUCKETS, LANES), F32), compiler_params=_params(2))(bmaps, *dtiles)


def _rows(l_start, n, d, r):
    if d == 1:
        return pl.ds(pl.multiple_of(l_start, 8), n)
    return pl.ds(l_start * d + r, n, stride=d)


def _stack_heads(xv, lo):
    z = jnp.zeros_like(xv)
    return jnp.concatenate([jnp.where(lo, xv, z), jnp.where(lo, z, xv)], axis=0)


def _unstack_heads(xv, lo):
    return jnp.where(lo, xv[:QBLOCK], xv[QBLOCK:])


def _per_head_rows(v0, v1):
    if jnp.ndim(v0) == 0:
        return jnp.where(lax.broadcasted_iota(jnp.int32, (2 * QBLOCK, 1), 0) < QBLOCK, v0, v1)
    return jnp.concatenate([v0, v1], axis=0)


def _block_geometry(b, nb_sub, pad):
    r, lb = b // nb_sub, b % nb_sub
    l0 = lb * QBLOCK
    lp = jnp.maximum(l0 - pad, 0)
    ln = jnp.minimum(l0 + QBLOCK, nb_sub * QBLOCK - pad)
    return r, l0, lp, ln, (lb == 0).astype(jnp.int32) + 2 * (lb == nb_sub - 1).astype(jnp.int32)


def _window(ref, l0, lp, ln, pad, d, r):
    return jnp.concatenate([ref[_rows(lp, pad, d, r), :], ref[_rows(l0, QBLOCK, d, r), :],
                            ref[_rows(ln, pad, d, r), :]], axis=0)


def _attn_fwd(qkv, bias, sink, dils, pad, after):
    T = qkv.shape[0]
    hw = qkv.shape[1] // 3
    ng = hw // LANES
    n_br = len(dils)
    n_blocks = T // QBLOCK
    W = QBLOCK + 2 * pad
    chunk = 256

    def body(sink_ref, q_ref, k_ref, v_ref, bias_ref, after_ref, o_ref, lse_ref, *scratch):
        g = pl.program_id(0)
        lo = _lo_lanes((QBLOCK, LANES))
        snk = _per_head_rows(sink_ref[2 * g], sink_ref[2 * g + 1])
        for c, d in enumerate(dils):
            nb_sub = n_blocks // d
            o_dst = scratch[0].at[c] if n_br > 1 else o_ref
            l_dst = scratch[1].at[c] if n_br > 1 else lse_ref

            def block(b, carry, c=c, d=d, nb_sub=nb_sub, o_dst=o_dst, l_dst=l_dst):
                r, l0, lp, ln, edge = _block_geometry(b, nb_sub, pad)
                q = _stack_heads(q_ref[_rows(l0, QBLOCK, d, r), :].astype(BF16), lo)
                k = _window(k_ref, l0, lp, ln, pad, d, r).astype(BF16)
                v = _window(v_ref, l0, lp, ln, pad, d, r).astype(BF16)
                s = _dot_nt(q, k) + bias_ref[c, edge]
                m = jnp.maximum(jnp.max(s, axis=1, keepdims=True), snk)
                p = jnp.exp(s - m)
                den = jnp.sum(p, axis=1, keepdims=True) + jnp.exp(snk - m)
                o_dst[_rows(l0, QBLOCK, d, r), :] = _unstack_heads(_dot(p.astype(BF16), v) / den, lo)
                l_dst[_rows(l0, QBLOCK, d, r), :] = _unstack_heads(
                    jnp.broadcast_to(m + jnp.log(den), (2 * QBLOCK, LANES)), lo)
                return carry

            lax.fori_loop(0, n_blocks, block, 0, unroll=FWD_UNROLL)

        if n_br > 1:
            def merge(i, carry):
                rs = pl.ds(pl.multiple_of(i * chunk, chunk), chunk)
                ls = [scratch[1][c, rs, :] for c in range(n_br)]
                m = ls[0]
                for t in ls[1:]:
                    m = jnp.maximum(m, t)
                ws = [jnp.exp(t - m) for t in ls]
                z = ws[0]
                acc = ws[0] * scratch[0][0, rs, :]
                for c in range(1, n_br):
                    z = z + ws[c]
                    acc = acc + ws[c] * scratch[0][c, rs, :]
                o_ref[rs, :] = acc / z
                lse_ref[rs, :] = m + jnp.log(z)
                return carry

            lax.fori_loop(0, T // chunk, merge, 0)

    def col(base):
        return pl.BlockSpec((T, LANES), lambda g: (0, base + g))

    out = jax.ShapeDtypeStruct((T, hw), F32)
    scratch = [pltpu.VMEM((n_br, T, LANES), F32)] * 2 if n_br > 1 else []
    return pl.pallas_call(
        body, name="attn_fwd", grid=(ng,),
        in_specs=[pl.BlockSpec(memory_space=pltpu.SMEM), col(0), col(ng), col(2 * ng),
                  pl.BlockSpec((n_br, 4, 2 * QBLOCK, W), lambda g: (0, 0, g, 0)), ANY_SPEC],
        out_specs=[col(0), col(0)], out_shape=[out, out], scratch_shapes=scratch,
        compiler_params=_params(1))(sink, qkv, qkv, qkv, bias, after)


def _attn_bwd(qkv, bias, sink, dils, pad, do, lse, dd, col_base, after):
    T = qkv.shape[0]
    hw = qkv.shape[1] // 3
    ng = hw // LANES
    n_br = len(dils)
    n_blocks = T // QBLOCK
    W = QBLOCK + 2 * pad

    def body(sink_ref, q_ref, k_ref, v_ref, bias_ref, do_ref, lse_ref, dd_ref, after_ref,
             dq_ref, dk_ref, dv_ref, dt_ref, ds_ref):
        g = pl.program_id(0)
        dq_ref[...] = jnp.zeros_like(dq_ref)
        dk_ref[...] = jnp.zeros_like(dk_ref)
        dv_ref[...] = jnp.zeros_like(dv_ref)
        dt_ref[...] = jnp.zeros_like(dt_ref)
        ds_ref[...] = jnp.zeros_like(ds_ref)
        lo = _lo_lanes((QBLOCK, LANES))
        snk = jnp.where(lo, sink_ref[2 * g], sink_ref[2 * g + 1])
        for c, d in enumerate(dils):
            nb_sub = n_blocks // d

            def block(b, carry, c=c, d=d, nb_sub=nb_sub):
                r, l0, lp, ln, edge = _block_geometry(b, nb_sub, pad)
                rows_q = _rows(l0, QBLOCK, d, r)
                q = _stack_heads(q_ref[rows_q, :].astype(BF16), lo)
                k = _window(k_ref, l0, lp, ln, pad, d, r).astype(BF16)
                v = _window(v_ref, l0, lp, ln, pad, d, r).astype(BF16)
                dob = _stack_heads(do_ref[rows_q, :].astype(BF16), lo)
                lse_b = lse_ref[rows_q, :]
                dd_b = dd_ref[rows_q, :]
                s = _dot_nt(q, k) + bias_ref[c, edge]
                p = jnp.exp(s - _per_head_rows(lse_b[:, 0:1], lse_b[:, HEAD_DIM:HEAD_DIM + 1]))
                ds = p * (_dot_nt(dob, v) - _per_head_rows(dd_b[:, 0:1], dd_b[:, HEAD_DIM:HEAD_DIM + 1]))
                dsb = ds.astype(BF16)
                dkw = _dot_tn(dsb, q)
                dvw = _dot_tn(p.astype(BF16), dob)
                dt_ref[c] += ds
                dq_ref[rows_q, :] += _unstack_heads(_dot(dsb, k), lo)
                ds_ref[0:1, :] += jnp.sum(-jnp.exp(snk - lse_b) * dd_b, axis=0, keepdims=True)
                for part, (start, n) in zip((0, pad, pad + QBLOCK), ((lp, pad), (l0, QBLOCK), (ln, pad))):
                    dk_ref[_rows(start, n, d, r), :] += dkw[part:part + n]
                    dv_ref[_rows(start, n, d, r), :] += dvw[part:part + n]
                return carry

            lax.fori_loop(0, n_blocks, block, 0, unroll=BWD_UNROLL)

    def col(base):
        return pl.BlockSpec((T, LANES), lambda g: (0, base + g))

    tile = pl.BlockSpec((n_br, 2 * QBLOCK, W), lambda g: (0, g, 0))
    full = jax.ShapeDtypeStruct((T, hw), F32)
    dq, dk, dv, dt, dsink = pl.pallas_call(
        body, name="attn_bwd", grid=(ng,),
        in_specs=[pl.BlockSpec(memory_space=pltpu.SMEM), col(0), col(ng), col(2 * ng),
                  pl.BlockSpec((n_br, 4, 2 * QBLOCK, W), lambda g: (0, 0, g, 0)),
                  col(col_base), col(0), col(col_base), ANY_SPEC],
        out_specs=[col(0), col(0), col(0), tile, pl.BlockSpec((None, 8, LANES), lambda g: (g, 0, 0))],
        out_shape=[full, full, full, jax.ShapeDtypeStruct((n_br, 2 * ng * QBLOCK, W), F32),
                   jax.ShapeDtypeStruct((ng, 8, LANES), F32)],
        compiler_params=_params(1))(sink, qkv, qkv, qkv, bias, do, lse, dd, after)
    return dq, dk, dv, dt.reshape(n_br, 2 * ng, QBLOCK, W), dsink


def _mix_bwd_in(dx, wo, o_list):
    T, D = dx.shape
    widths = [o.shape[1] for o in o_list]
    hw = sum(widths)
    n = len(o_list)
    tm = _pick_tm(T, 256)
    nt = T // tm

    def body(*refs):
        dx_ref, w_ref, o_refs = refs[0], refs[1], refs[2:2 + n]
        do_ref, dd_ref, dw_ref, acc = refs[2 + n:]
        i = pl.program_id(0)

        @pl.when(i == 0)
        def _():
            acc[...] = jnp.zeros_like(acc)

        dxb = dx_ref[...].astype(BF16)
        dov = _dot_nt(dxb, w_ref[...])
        do_ref[...] = dov
        off = 0
        for o_ref, k in zip(o_refs, widths):
            ov = o_ref[...]
            prod = dov[:, off:off + k] * ov
            for cb in range(k // LANES):
                dd_ref[:, off + cb * LANES:off + (cb + 1) * LANES] = _seg_sum(prod[:, cb * LANES:(cb + 1) * LANES])
            acc[off:off + k, :] += _dot_tn(ov.astype(BF16), dxb)
            off += k

        @pl.when(i == nt - 1)
        def _():
            dw_ref[...] = acc[...].astype(BF16)

    row = pl.BlockSpec((tm, hw), lambda i: (i, 0))
    out = jax.ShapeDtypeStruct((T, hw), F32)
    return pl.pallas_call(
        body, name="mix_bwd_in", grid=(nt,),
        in_specs=[pl.BlockSpec((tm, D), lambda i: (i, 0)), pl.BlockSpec((hw, D), lambda i: (0, 0))]
        + [pl.BlockSpec((tm, k), lambda i: (i, 0)) for k in widths],
        out_specs=[row, row, pl.BlockSpec((hw, D), lambda i: (0, 0))],
        out_shape=[out, out, jax.ShapeDtypeStruct((hw, D), BF16)],
        scratch_shapes=[pltpu.VMEM((hw, D), F32)], compiler_params=_params(1))(dx, wo, *o_list)


def _mesh_pos():
    return lax.axis_index("x"), lax.axis_index("y"), lax.axis_index("c")


def _my_chip():
    return 2 * lax.axis_index("x") + lax.axis_index("y")


def _other_chips(x, y):
    return [(1 - x, y), (x, 1 - y), (1 - x, 1 - y)]


def _half_rows(rows, cc):
    hr = rows // 2
    return pl.ds(pl.multiple_of(cc * hr, 16), hr)


def _cast_into_slot(w, l):
    _, R, C = w.shape
    tm = _pick_tm(R, 512)

    def body(w_ref, o_ref):
        o_ref[...] = w_ref[...].astype(BF16)

    return pl.pallas_call(
        body, name="cast_into_slot", grid=(R // tm,),
        in_specs=[pl.BlockSpec((None, tm, C), lambda i: (l, i, 0))],
        out_specs=pl.BlockSpec((None, tm, C), lambda i: (_my_chip(), i, 0)),
        out_shape=jax.ShapeDtypeStruct((N_CHIPS, R, C), BF16), compiler_params=_params(1))(w)


def _split_start(name, arrays, make_copies, n_sem, after):
    n = len(arrays)
    n_in = n + (0 if after is None else 1)

    def body(*refs):
        send_s, recv_s = refs[n_in], refs[n_in + 1]
        token = refs[n_in + 2 + n]
        for send, _ in make_copies(refs[:n], send_s, recv_s):
            send.start()
        token[...] = jnp.zeros_like(token)

    res = pl.pallas_call(
        body, name=name,
        out_shape=(pltpu.SemaphoreType.DMA((n_sem,)), pltpu.SemaphoreType.DMA((n_sem,)),
                   *[pltpu.HBM(a.shape, a.dtype) for a in arrays], jax.ShapeDtypeStruct((8, LANES), F32)),
        in_specs=[HBM_SPEC] * n + [ANY_SPEC] * (n_in - n),
        out_specs=(SEM_SPEC, SEM_SPEC, *([HBM_SPEC] * n), pl.BlockSpec(memory_space=pltpu.VMEM)),
        input_output_aliases={i: 2 + i for i in range(n)},
        compiler_params=pltpu.CompilerParams(has_side_effects=pltpu.SideEffectType.DATAFLOW_SIDE_EFFECTING),
    )(*[pltpu.with_memory_space_constraint(a, pltpu.HBM) for a in arrays], *([] if after is None else [after]))
    return res[0], res[1], list(res[2:2 + n]), res[2 + n]


def _split_wait(name, send_s, recv_s, arrays, make_copies, after):
    n = len(arrays)

    def body(*refs):
        for send, landed in make_copies(refs[:n], refs[n], refs[n + 1]):
            send.wait_send()
            landed.wait_recv()

    return list(pl.pallas_call(
        body, name=name, out_shape=[pltpu.HBM(a.shape, a.dtype) for a in arrays],
        in_specs=[HBM_SPEC] * n + [SEM_SPEC, SEM_SPEC, ANY_SPEC], out_specs=[HBM_SPEC] * n,
        input_output_aliases={i: i for i in range(n)},
        compiler_params=pltpu.CompilerParams(has_side_effects=pltpu.SideEffectType.DATAFLOW_SIDE_EFFECTING),
    )(*arrays, send_s, recv_s, after))


def _gather_copies(shapes):
    n = len(shapes)

    def make(refs, send_s, recv_s):
        x, y, c = _mesh_pos()
        my = 2 * x + y
        copies = []
        for w in range(n):
            for k, (px, py) in enumerate(_other_chips(x, y)):
                def part(slot, w=w):
                    return refs[w].at[slot, _half_rows(shapes[w][1], c), :]
                sems = dict(send_sem=send_s.at[k * n + w], recv_sem=recv_s.at[k * n + w],
                            device_id=(px, py, c), device_id_type=MESH)
                copies.append((pltpu.make_async_remote_copy(src_ref=part(my), dst_ref=part(my), **sems),
                               pltpu.make_async_remote_copy(src_ref=part(2 * px + py), dst_ref=part(2 * px + py), **sems)))
        return copies

    return make


def _forward_copies(shapes):
    n = len(shapes)

    def make(refs, send_s, recv_s):
        x, y, c = _mesh_pos()
        copies = []
        for w in range(n):
            for k, (px, py) in enumerate(_other_chips(x, y)):
                def part(cc, w=w, slot=2 * px + py):
                    return refs[w].at[slot, _half_rows(shapes[w][1], cc), :]
                sems = dict(send_sem=send_s.at[k * n + w], recv_sem=recv_s.at[k * n + w],
                            device_id=(x, y, 1 - c), device_id_type=MESH)
                copies.append((pltpu.make_async_remote_copy(src_ref=part(c), dst_ref=part(c), **sems),
                               pltpu.make_async_remote_copy(src_ref=part(1 - c), dst_ref=part(1 - c), **sems)))
        return copies

    return make


def _pair_forward(bufs):
    n = len(bufs)
    make = _forward_copies([b.shape for b in bufs])

    def body(*refs):
        copies = make(refs[n:2 * n], refs[2 * n], refs[2 * n + 1])
        for send, _ in copies:
            send.start()
        for _, landed in copies:
            landed.wait_recv()
        for send, _ in copies:
            send.wait_send()

    return list(pl.pallas_call(
        body, name="ag_pair_forward", in_specs=[HBM_SPEC] * n, out_specs=[HBM_SPEC] * n,
        out_shape=[jax.ShapeDtypeStruct(b.shape, b.dtype) for b in bufs],
        input_output_aliases={w: w for w in range(n)},
        scratch_shapes=[pltpu.SemaphoreType.DMA((3 * n,)), pltpu.SemaphoreType.DMA((3 * n,))],
    )(*bufs))


def _pair_exchange_copies(shapes):
    n = len(shapes)

    def make(refs, send_s, recv_s):
        x, y, c = _mesh_pos()
        copies = []
        for t in range(n):
            sems = dict(send_sem=send_s.at[t], recv_sem=recv_s.at[t], device_id=(x, y, 1 - c), device_id_type=MESH)
            land = refs[n + t]
            copies.append((pltpu.make_async_remote_copy(
                src_ref=refs[t].at[:, _half_rows(shapes[t][1], 1 - c), :], dst_ref=land, **sems),
                pltpu.make_async_remote_copy(src_ref=land, dst_ref=land, **sems)))
        return copies

    return make


def _pair_share_copies(shapes):
    n = len(shapes)

    def make(refs, send_s, recv_s):
        x, y, c = _mesh_pos()
        copies = []
        for t in range(n):
            def half(cc, t=t):
                return refs[t].at[_half_rows(shapes[t][0], cc), :]
            sems = dict(send_sem=send_s.at[t], recv_sem=recv_s.at[t], device_id=(x, y, 1 - c), device_id_type=MESH)
            copies.append((pltpu.make_async_remote_copy(src_ref=half(c), dst_ref=half(c), **sems),
                           pltpu.make_async_remote_copy(src_ref=half(1 - c), dst_ref=half(1 - c), **sems)))
        return copies

    return make


def _rs_add_pair(grad, recv):
    n_slot, hr, C = recv.shape
    tm = _pick_tm(hr, 192)
    nb = hr // tm

    def body(a_ref, b_ref, o_ref):
        o_ref[...] = (a_ref[...].astype(F32) + b_ref[...].astype(F32)).astype(BF16)

    blk = pl.BlockSpec((n_slot, tm, C), lambda i: (0, i, 0))
    return pl.pallas_call(
        body, name="rs_add_pair", grid=(nb,),
        in_specs=[pl.BlockSpec((n_slot, tm, C), lambda i: (0, lax.axis_index("c") * nb + i, 0)), blk],
        out_specs=blk, out_shape=jax.ShapeDtypeStruct(recv.shape, BF16), compiler_params=_params(1))(grad, recv)


def _scatter_copies(n):
    def make(refs, send_s, recv_s):
        x, y, c = _mesh_pos()
        copies = []
        for t in range(n):
            for k, (px, py) in enumerate(_other_chips(x, y)):
                sems = dict(send_sem=send_s.at[3 * t + k], recv_sem=recv_s.at[3 * t + k],
                            device_id=(px, py, c), device_id_type=MESH)
                land = refs[n + t].at[k]
                copies.append((pltpu.make_async_remote_copy(src_ref=refs[t].at[2 * px + py], dst_ref=land, **sems),
                               pltpu.make_async_remote_copy(src_ref=land, dst_ref=land, **sems)))
        return copies

    return make


def _rs_add_chips(part, recv):
    _, hr, C = part.shape
    tm = _pick_tm(hr, 256)
    nb = hr // tm

    def body(a_ref, r0, r1, r2, o_ref):
        o_ref[...] = ((a_ref[...].astype(F32) + r0[...].astype(F32)) + r1[...].astype(F32)) + r2[...].astype(F32)

    def rel(k):
        return pl.BlockSpec((None, tm, C), lambda i: (k, i, 0))

    return pl.pallas_call(
        body, name="rs_add_chips", grid=(nb,),
        in_specs=[pl.BlockSpec((None, tm, C), lambda i: (_my_chip(), i, 0)), rel(0), rel(1), rel(2)],
        out_specs=pl.BlockSpec((tm, C), lambda i: (lax.axis_index("c") * nb + i, 0)),
        out_shape=jax.ShapeDtypeStruct((2 * hr, C), F32), compiler_params=_params(1))(part, recv, recv, recv)


def _allreduce_small(v, after):
    rows = v.shape[0]

    def body(v_ref, after_ref, o_ref, buf, send_s, recv_s):
        x, y, c = _mesh_pos()
        me = 4 * x + 2 * y + c
        buf[me] = v_ref[...]
        copies = []
        for r in range(1, N_DEV):
            px = 1 - x if r & 4 else x
            py = 1 - y if r & 2 else y
            pc = 1 - c if r & 1 else c
            send = pltpu.make_async_remote_copy(
                src_ref=v_ref, dst_ref=buf.at[me], send_sem=send_s.at[r - 1], recv_sem=recv_s.at[r - 1],
                device_id=(px, py, pc), device_id_type=MESH)
            peer_slot = buf.at[4 * px + 2 * py + pc]
            landed = pltpu.make_async_remote_copy(
                src_ref=peer_slot, dst_ref=peer_slot, send_sem=send_s.at[r - 1], recv_sem=recv_s.at[r - 1],
                device_id=(px, py, pc), device_id_type=MESH)
            copies.append((send, landed))
        for send, _ in copies:
            send.start()
        for _, landed in copies:
            landed.wait_recv()
        for send, _ in copies:
            send.wait_send()
        acc = buf[0]
        for j in range(1, N_DEV):
            acc = acc + buf[j]
        o_ref[...] = acc

    vm = pl.BlockSpec(memory_space=pltpu.VMEM)
    return pl.pallas_call(
        body, name="allreduce_small", in_specs=[vm, ANY_SPEC], out_specs=vm,
        out_shape=jax.ShapeDtypeStruct((rows, LANES), F32),
        scratch_shapes=[pltpu.VMEM((N_DEV, rows, LANES), F32), pltpu.SemaphoreType.DMA((N_DEV - 1,)),
                        pltpu.SemaphoreType.DMA((N_DEV - 1,))],
    )(v, after)


def _adamw_fn(w, g, m, v):
    m2 = ADAM_B1 * m + (1.0 - ADAM_B1) * g
    v2 = ADAM_B2 * v + (1.0 - ADAM_B2) * (g * g)
    m_hat = m2 / (1.0 - ADAM_B1 ** ADAM_STEP)
    v_hat = v2 / (1.0 - ADAM_B2 ** ADAM_STEP)
    delta = -ADAM_LR * (m_hat / (jnp.sqrt(v_hat) + ADAM_EPS) + ADAM_WD * w)
    return g, delta, m2, v2


def _adamw_layer(w, g, m, v, l, prev, after):
    NL, R, C = w.shape
    tm = _pick_tm(R, 256)
    n_prev = 0 if prev is None else 4

    def body(w_ref, g_ref, m_ref, v_ref, after_ref, *rest):
        outs = rest[n_prev:]
        for o_ref, val in zip(outs, _adamw_fn(w_ref[...], g_ref[...], m_ref[...], v_ref[...])):
            o_ref[...] = val

    lay = pl.BlockSpec((None, tm, C), lambda i: (l, i, 0))
    shape = jax.ShapeDtypeStruct((NL, R, C), F32)
    return pl.pallas_call(
        body, name="adamw", grid=(R // tm,),
        in_specs=[lay, pl.BlockSpec((tm, C), lambda i: (i, 0)), lay, lay, ANY_SPEC] + [ANY_SPEC] * n_prev,
        out_specs=[lay] * 4, out_shape=[shape] * 4,
        input_output_aliases={5 + j: j for j in range(n_prev)},
        compiler_params=_params(1))(w, g, m, v, after, *(prev or []))


def _pack_small(parts):
    out = []
    for a in parts:
        flat = a.reshape(-1)
        n = -(-flat.shape[0] // (8 * LANES)) * 8 * LANES
        out.append(jnp.pad(flat, (0, n - flat.shape[0])).reshape(-1, LANES))
    return jnp.concatenate(out, axis=0)


def _unpack_small(packed, like):
    out, r = [], 0
    for a in like:
        size = int(np.prod(a.shape))
        rows = -(-size // (8 * LANES)) * 8
        out.append(packed[r:r + rows].reshape(-1)[:size].reshape(a.shape))
        r += rows
    return out


def _ffn_forward(x, g, win, wout):
    h, silu, dgate, act = _ffn_up(x, g, win)
    wout, after = wout(act) if callable(wout) else (wout, act)
    return _mm_res("ffn_down", x, [act], wout, 0.5, after), (x, h, silu, dgate, act)


def _ffn_backward(dx, saved, g, win, wout, after):
    x, h, silu, dgate, act = saved
    D = x.shape[1]
    wc = win.shape[2]
    dgu, dx_in, dg = _ffn_bwd(dx, wout, silu, dgate, win, x, g, after)
    dwout = _mm_tn("dw_ffn_out", act, wc, dx, D, 2, True, False, 0.5)
    dwin = _mm_tn_pairs("dw_ffn_in", h, dgu, wc, 4)
    return dx_in, dg, dwin, dwout.reshape(N_CHIPS, -1, D)


GROUP_MID = ("w_o", "ffn2_w_in", "ffn2_w_out", "w_ple_gate", "w_ple_proj")
GROUP_LAST = ("w_qkv", "ffn1_w_in", "ffn1_w_out")
GATHER_L0 = (("a", ("ffn1_w_in",)), ("b", ("ffn1_w_out",)), ("c", ("w_qkv", "w_o")),
             ("d", ("ffn2_w_in", "ffn2_w_out", "w_ple_gate", "w_ple_proj")))


def _gather_start(tag, slotted, after):
    return _split_start("ag_start_" + tag, slotted, _gather_copies([a.shape for a in slotted]), 3 * len(slotted), after)


def _gather_finish(tag, started, after):
    send_s, recv_s, arrays, _ = started
    return _pair_forward(_split_wait("ag_wait_" + tag, send_s, recv_s, arrays,
                                     _gather_copies([a.shape for a in arrays]), after))


def _scatter_exchange(tag, grads):
    n = len(grads)
    land = [lax.empty((g_.shape[0], g_.shape[1] // 2, g_.shape[2]), g_.dtype) for g_ in grads]
    return _split_start("rs_px_start_" + tag, list(grads) + land, _pair_exchange_copies([g_.shape for g_ in grads]),
                        n, None)


def _scatter_chips(tag, started, after):
    send_s, recv_s, arrays, _ = started
    n = len(arrays) // 2
    arrays = _split_wait("rs_px_wait_" + tag, send_s, recv_s, arrays,
                         _pair_exchange_copies([a.shape for a in arrays[:n]]), after)
    part = [_rs_add_pair(g_, r_) for g_, r_ in zip(arrays[:n], arrays[n:])]
    land = [lax.empty((3,) + p_.shape[1:], p_.dtype) for p_ in part]
    return _split_start("rs_start_" + tag, part + land, _scatter_copies(n), 3 * n, None)


def _scatter_share(tag, started, after):
    send_s, recv_s, arrays, _ = started
    n = len(arrays) // 2
    arrays = _split_wait("rs_wait_" + tag, send_s, recv_s, arrays, _scatter_copies(n), after)
    halves = [_rs_add_chips(p_, r_) for p_, r_ in zip(arrays[:n], arrays[n:])]
    return _split_start("rs_ps_start_" + tag, halves, _pair_share_copies([h_.shape for h_ in halves]), n, None)


def _scatter_done(tag, started, after):
    send_s, recv_s, arrays, _ = started
    return _split_wait("rs_ps_wait_" + tag, send_s, recv_s, arrays, _pair_share_copies([a.shape for a in arrays]), after)


def kernel(x, p, rel_bias, norm_ffn1, ffn1_w_in, ffn1_w_out, norm_mix, w_qkv, q_norm_a, k_norm_a, q_norm_b, k_norm_b, sink_b, w_o, norm_ffn2, ffn2_w_in, ffn2_w_out, norm_ple, w_ple_gate, w_ple_proj, loss_target, m_rel_bias, m_norm_ffn1, m_ffn1_w_in, m_ffn1_w_out, m_norm_mix, m_w_qkv, m_q_norm_a, m_k_norm_a, m_q_norm_b, m_k_norm_b, m_sink_b, m_w_o, m_norm_ffn2, m_ffn2_w_in, m_ffn2_w_out, m_norm_ple, m_w_ple_gate, m_w_ple_proj, v_rel_bias, v_norm_ffn1, v_ffn1_w_in, v_ffn1_w_out, v_norm_mix, v_w_qkv, v_q_norm_a, v_k_norm_a, v_q_norm_b, v_k_norm_b, v_sink_b, v_w_o, v_norm_ffn2, v_ffn2_w_in, v_ffn2_w_out, v_norm_ple, v_w_ple_gate, v_w_ple_proj):
    given = dict(locals())
    T, D = x.shape[1], x.shape[2]
    NL = norm_ffn1.shape[0]
    x0 = x.reshape(T, D)
    tgt = loss_target.reshape(T, D)
    n_a, n_b, n_kv = _qkv_layout(D)

    assert NL == 2
    slot = [{name: _cast_into_slot(given[name], l) for name in BIG} for l in range(NL)]
    ag, token = {}, None
    for tag, names in GATHER_L0:
        ag[tag] = _gather_start(tag, [slot[0][n] for n in names], token)
        token = ag[tag][3]
    ag_1 = _gather_start("1", [slot[1][n] for n in BIG], token)

    def arrived(tag, after):
        return dict(zip(dict(GATHER_L0)[tag], _gather_finish(tag, ag[tag], after)))

    def by_rows(a):
        return a.reshape(-1, a.shape[-1])

    def by_cols(a):
        return a.transpose(1, 0, 2).reshape(a.shape[1], -1)

    QW = N_CHIPS * w_qkv.shape[2]

    dils = tuple(d for _, d in DILATED_CONFIGS)
    cfg_a = [(w // (2 * d), d) for w, d in DILATED_CONFIGS]
    pad_a, pad_b = _window_pad(cfg_a[0][0]), _window_pad(SWA_RADIUS)
    bmaps_a, bmaps_b = jnp.asarray(_bucket_maps(cfg_a)), jnp.asarray(_bucket_maps([(SWA_RADIUS, 1)]))
    n_heads = rel_bias.shape[1] // 2
    bias_a = _bias_build(rel_bias, bmaps_a, 0, pad_a)
    bias_b = _bias_build(rel_bias, bmaps_b, n_heads, pad_b)
    no_sink = jnp.full((n_heads,), NEG, F32)

    def gains_row(l):
        ones = jnp.ones((n_a * LANES,), F32)
        return jnp.concatenate([
            jnp.tile(q_norm_a[l], 2 * n_a), jnp.tile(k_norm_a[l], 2 * n_a), ones,
            jnp.tile(q_norm_b[l], 2 * n_b), jnp.tile(k_norm_b[l], 2 * n_kv), jnp.ones((n_kv * LANES,), F32)]).reshape(1, QW)

    def forward_start(tag, state, after):
        landed = _split_wait("ag_wait_" + tag, state[0], state[1], state[2],
                             _gather_copies([a.shape for a in state[2]]), after)
        return _split_start("ag_pf_start_" + tag, landed, _forward_copies([a.shape for a in landed]), 3 * len(landed),
                            None)

    def forward_done(tag, names, pf, after):
        shapes = [a.shape for a in pf[2]]
        return dict(zip(names, _split_wait("ag_pf_wait_" + tag, pf[0], pf[1], pf[2], _forward_copies(shapes), after)))

    saved, weights = [], []
    xc = x0
    pf_1 = None
    for l in range(NL):
        s, w = {}, {}
        if l == 0:
            w.update(arrived("a", ag_1[3]))
            pf = {}

            def ffn1_w_out(act, w=w, pf=pf):
                w.update(arrived("b", act))
                w["ffn1_w_out"] = by_rows(w["ffn1_w_out"])
                pf["c"] = forward_start("c", ag["c"], w["ffn1_w_out"])
                return w["ffn1_w_out"], pf["c"][3]
        else:
            w.update(forward_done("1", BIG, pf_1, xc))
            ffn1_w_out = w["ffn1_w_out"] = by_rows(w["ffn1_w_out"])
        xc, s["ffn1"] = _ffn_forward(xc, norm_ffn1[l:l + 1], w["ffn1_w_in"], ffn1_w_out)
        s["x1"] = xc
        if l == 0:
            w.update(forward_done("c", dict(GATHER_L0)["c"], pf["c"], xc))
        w["w_qkv"] = by_cols(w["w_qkv"])
        w["w_o"] = by_rows(w["w_o"])
        s["h2"], s["raw"], s["qkv_a"], s["qkv_b"] = _qkv_proj(xc, norm_mix[l:l + 1], w["w_qkv"], gains_row(l))
        after_qkv = s["qkv_a"]
        if l == 0:
            pf["d"] = forward_start("d", ag["d"], s["qkv_a"])
            after_qkv = pf["d"][3]
        s["o_a"], s["lse_a"] = _attn_fwd(s["qkv_a"], bias_a, no_sink, dils, pad_a, after_qkv)
        s["o_b"], s["lse_b"] = _attn_fwd(s["qkv_b"], bias_b, sink_b[l], (1,), pad_b, s["qkv_b"])
        xc = _mm_res("attn_out", xc, [s["o_a"], s["o_b"]], w["w_o"], 1.0, s["o_b"])
        if l == 0:
            w.update(forward_done("d", dict(GATHER_L0)["d"], pf["d"], xc))
        w["w_ple_proj"] = by_cols(w["w_ple_proj"])
        for name in ("ffn2_w_out", "w_ple_gate"):
            w[name] = by_rows(w[name])
        xc, s["ffn2"] = _ffn_forward(xc, norm_ffn2[l:l + 1], w["ffn2_w_in"], w["ffn2_w_out"])
        s["x3"] = xc
        if l == 0:
            pf_1 = forward_start("1", ag_1, xc)
        s["hn"], xc, s["gate"], s["pp"] = _ple_fwd(xc, norm_ple[l:l + 1], p, l, w["w_ple_gate"], w["w_ple_proj"],
                                                   pf_1[3] if l == 0 else xc)
        saved.append(s)
        weights.append(w)

    dx, loss_blk = _loss_fwd_bwd(xc, tgt)
    loss = lax.psum(loss_blk[0, 0], ("x", "y", "c"))

    gs = {name: [None] * NL for name in SMALL if name != "rel_bias"}
    dt_a, dt_b = [], []

    def layer_backward(l, dx, hooks):
        def at(point, ready, *more):
            return hooks[point](ready, *more) if point in hooks else ready

        s, w, gw = saved[l], weights[l], {}
        dx, gs["norm_ple"][l], dwg, dwp = _ple_bwd(dx, s["gate"], s["pp"], s["hn"], p, l, s["x3"],
                                                   norm_ple[l:l + 1], w["w_ple_gate"], at("start", dx))
        gw["w_ple_gate"] = dwg.reshape(N_CHIPS, -1, D)
        gw["w_ple_proj"] = dwp.reshape(dwp.shape[0], N_CHIPS, -1).transpose(1, 0, 2)
        dx, gs["norm_ffn2"][l], gw["ffn2_w_in"], gw["ffn2_w_out"] = _ffn_backward(
            dx, s["ffn2"], norm_ffn2[l:l + 1], w["ffn2_w_in"], w["ffn2_w_out"], at("after_ple", dx))
        do, dd, dwo = _mix_bwd_in(dx, w["w_o"], [s["o_a"], s["o_b"]])
        hwa = s["o_a"].shape[1]
        gw["w_o"] = dwo.reshape(N_CHIPS, -1, D)
        dqa, dka, dva, dt, _ = _attn_bwd(s["qkv_a"], bias_a, no_sink, dils, pad_a, do, s["lse_a"], dd, 0, do)
        dt_a.append(dt)
        dqb, dkb, dvb, dt, dsink = _attn_bwd(s["qkv_b"], bias_b, sink_b[l], (1,), pad_b, do, s["lse_b"], dd,
                                             hwa // LANES, at("after_attn_a", dqa))
        dt_b.append(dt)
        gs["sink_b"][l] = dsink[:, 0, ::HEAD_DIM].reshape(-1)
        dx, gs["norm_mix"][l], dgains, dwqkv = _qkv_bwd(
            s["raw"], gains_row(l), (dqa, dka, dva), (dqb, dkb, dvb), w["w_qkv"], s["h2"], s["x1"],
            norm_mix[l:l + 1], dx, at("before_qkv_bwd", dqb, [gw[n] for n in GROUP_MID]))
        dgv = dgains.reshape(-1, HEAD_DIM)
        gs["q_norm_a"][l] = dgv[:2 * n_a].sum(0)
        gs["k_norm_a"][l] = dgv[2 * n_a:4 * n_a].sum(0)
        gs["q_norm_b"][l] = dgv[6 * n_a:6 * n_a + 2 * n_b].sum(0)
        gs["k_norm_b"][l] = dgv[6 * n_a + 2 * n_b:6 * n_a + 2 * n_b + 2 * n_kv].sum(0)
        gw["w_qkv"] = dwqkv.reshape(D, N_CHIPS, -1).transpose(1, 0, 2)
        dx, gs["norm_ffn1"][l], gw["ffn1_w_in"], gw["ffn1_w_out"] = _ffn_backward(
            dx, s["ffn1"], norm_ffn1[l:l + 1], w["ffn1_w_in"], w["ffn1_w_out"], at("before_ffn1", dx))
        return dx, gw

    out = {}

    def adamw_group(names, l, grads, after):
        for name, g_ in zip(names, grads):
            out[name] = _adamw_layer(given[name], g_, given["m_" + name], given["v_" + name], l, out.get(name), after)
            after = out[name][0]
        return after

    dx, gw1 = layer_backward(NL - 1, dx, {})
    px_1 = _scatter_exchange("1", [gw1[n] for n in BIG])
    rs = {}

    def chips_1(ready):
        rs["chips_1"] = _scatter_chips("1", px_1, ready)
        return rs["chips_1"][3]

    def share_1(ready):
        rs["share_1"] = _scatter_share("1", rs["chips_1"], ready)
        return rs["share_1"][3]

    def exchange_0a(ready, grads):
        rs["px_0a"] = _scatter_exchange("0a", grads)
        return rs["px_0a"][3]

    def chips_0a(ready):
        rs["chips_0a"] = _scatter_chips("0a", rs["px_0a"], ready)
        return rs["chips_0a"][3]

    dx, gw0 = layer_backward(0, dx, {"start": lambda ready: px_1[3], "after_ple": chips_1, "after_attn_a": share_1,
                                     "before_qkv_bwd": exchange_0a, "before_ffn1": chips_0a})
    grad_x = dx.reshape(x.shape)

    px_0b = _scatter_exchange("0b", [gw0[n] for n in GROUP_LAST])
    d_rel_bias = (_bias_grad(dt_a, bmaps_a, 0) + _bias_grad(dt_b, bmaps_b, n_heads))[:, :rel_bias.shape[1]]
    small_g = [d_rel_bias] + [jnp.stack([t.reshape(-1) for t in gs[name]]) for name in SMALL[1:]]
    g_sum = _allreduce_small(_pack_small(small_g), px_0b[3])
    res = _ew("adamw_small", _adamw_fn,
              [_pack_small([given[n] for n in SMALL]), g_sum, _pack_small([given["m_" + n] for n in SMALL]),
               _pack_small([given["v_" + n] for n in SMALL])], [(LANES, F32)] * 4)
    like = [given[n] for n in SMALL]
    unpacked = [_unpack_small(r, like) for r in res]
    for i, name in enumerate(SMALL):
        out[name] = [u[i] for u in unpacked]

    share_0a = _scatter_share("0a", rs["chips_0a"], res[0])
    g_1 = _scatter_done("1", rs["share_1"], share_0a[3])
    chips_0b = _scatter_chips("0b", px_0b, g_1[0])
    ready = adamw_group(BIG, 1, g_1, chips_0b[3])
    ready = adamw_group(GROUP_MID, 0, _scatter_done("0a", share_0a, ready), ready)
    share_0b = _scatter_share("0b", chips_0b, ready)
    adamw_group(GROUP_LAST, 0, _scatter_done("0b", share_0b, share_0b[3]), share_0b[3])

    return (loss, grad_x, *[out[n][0] for n in WEIGHTS], *[out[n][1] for n in WEIGHTS],
            *[out[n][2] for n in WEIGHTS], *[out[n][3] for n in WEIGHTS])
```

```python
import functools
import math

import numpy as np
import jax
import jax.numpy as jnp
from jax import lax
from jax.experimental import pallas as pl
from jax.experimental.pallas import tpu as pltpu

F32 = jnp.float32
BF16 = jnp.bfloat16
MESH = pl.DeviceIdType.MESH

HEAD_DIM = 64
LANES = 128
QBLOCK = 128
FWD_UNROLL, BWD_UNROLL = 16, 16
QKV_CHUNK = 6
N_BUCKETS = 32
MAX_DISTANCE = 1024
DILATED_CONFIGS = ((128, 1), (512, 4), (2048, 16))
SWA_RADIUS = 128
GROUP_B = 4
EPS = 1e-6
NEG = -1e30
Q_SCALE = HEAD_DIM ** -0.5
ADAM_LR, ADAM_B1, ADAM_B2, ADAM_EPS, ADAM_WD, ADAM_STEP = 0.001, 0.9, 0.999, 1e-08, 0.01, 10
VMEM_LIMIT = 56 * 2 ** 20
N_CHIPS = 4
N_DEV = 8

BIG = ("ffn1_w_in", "ffn1_w_out", "w_qkv", "w_o", "ffn2_w_in", "ffn2_w_out", "w_ple_gate", "w_ple_proj")
SMALL = ("rel_bias", "norm_ffn1", "norm_mix", "q_norm_a", "k_norm_a", "q_norm_b", "k_norm_b", "sink_b",
         "norm_ffn2", "norm_ple")
WEIGHTS = ("rel_bias", "norm_ffn1", "ffn1_w_in", "ffn1_w_out", "norm_mix", "w_qkv", "q_norm_a", "k_norm_a",
           "q_norm_b", "k_norm_b", "sink_b", "w_o", "norm_ffn2", "ffn2_w_in", "ffn2_w_out", "norm_ple",
           "w_ple_gate", "w_ple_proj")


HBM_SPEC = pl.BlockSpec(memory_space=pltpu.HBM)
ANY_SPEC = pl.BlockSpec(memory_space=pl.ANY)
SEM_SPEC = pl.BlockSpec(memory_space=pltpu.SEMAPHORE)


def _params(n_grid):
    return pltpu.CompilerParams(dimension_semantics=("arbitrary",) * n_grid, vmem_limit_bytes=VMEM_LIMIT)


def _pick_tm(rows, cap):
    t = (min(cap, rows) // 16) * 16
    while t >= 16:
        if rows % t == 0:
            return t
        t -= 16
    return rows


def _dot(a, b):
    return jnp.dot(a, b, preferred_element_type=F32)


def _dot_nt(a, b):
    return lax.dot_general(a, b, (((1,), (1,)), ((), ())), preferred_element_type=F32)


def _dot_tn(a, b):
    return lax.dot_general(a, b, (((0,), (0,)), ((), ())), preferred_element_type=F32)


def _sigmoid(z):
    return 1.0 / (1.0 + jnp.exp(-z))


def _lo_lanes(shape):
    return lax.broadcasted_iota(jnp.int32, shape, len(shape) - 1) % LANES < HEAD_DIM


def _seg_sum(blk):
    lo = _lo_lanes(blk.shape)
    s_lo = jnp.sum(jnp.where(lo, blk, 0.0), axis=1, keepdims=True)
    s_hi = jnp.sum(jnp.where(lo, 0.0, blk), axis=1, keepdims=True)
    return jnp.where(lo, s_lo, s_hi)


def _rms_bwd_tile(x, g, dh):
    r = lax.rsqrt(jnp.mean(x * x, axis=-1, keepdims=True) + EPS)
    xh = x * r
    dyg = dh * g
    dx = r * (dyg - xh * jnp.mean(dyg * xh, axis=-1, keepdims=True))
    return dx, jnp.sum(dh * xh, axis=0, keepdims=True)


def _ew(name, fn, ins, out_defs, cap=512):
    rows = ins[0].shape[0]
    tm = _pick_tm(rows, cap)
    n_in = len(ins)

    def body(*refs):
        vals = fn(*[r[...] for r in refs[:n_in]])
        if not isinstance(vals, tuple):
            vals = (vals,)
        for r, v in zip(refs[n_in:], vals):
            r[...] = v.astype(r.dtype)

    return pl.pallas_call(
        body, name=name, grid=(rows // tm,),
        in_specs=[pl.BlockSpec((tm, a.shape[1]), lambda i: (i, 0)) for a in ins],
        out_specs=[pl.BlockSpec((tm, c), lambda i: (i, 0)) for c, _ in out_defs],
        out_shape=[jax.ShapeDtypeStruct((rows, c), dt) for c, dt in out_defs],
        compiler_params=_params(1))(*ins)


def _rms_tile(xv, gv):
    r = lax.rsqrt(jnp.mean(xv * xv, axis=-1, keepdims=True) + EPS)
    return (xv * r * gv).astype(BF16)


def _ffn_up(x, g, win):
    T, D = x.shape
    wc = win.shape[2]
    tm = _pick_tm(T, 512)

    def body(x_ref, g_ref, wg_ref, wu_ref, h_ref, silu_ref, dgate_ref, act_ref, wcat):
        @pl.when(pl.program_id(1) == 0)
        def _():
            wcat[:, :wc] = wg_ref[...]
            wcat[:, wc:] = wu_ref[...]

        hv = _rms_tile(x_ref[...], g_ref[...])
        h_ref[...] = hv
        gu = _dot(hv, wcat[...])
        gte, u = gu[:, :wc], gu[:, wc:]
        sg = _sigmoid(gte)
        silu = gte * sg
        silu_ref[...] = silu.astype(BF16)
        dgate_ref[...] = ((sg + silu * (1.0 - sg)) * u).astype(BF16)
        act_ref[...] = (silu * u).astype(BF16)

    out = jax.ShapeDtypeStruct((T, 2 * wc), BF16)
    ospec = pl.BlockSpec((tm, wc), lambda j, i: (i, j))
    nt = T // tm
    h_spec = pl.BlockSpec((tm, D), lambda j, i: (jnp.where(j == 0, i, nt), 0))
    return pl.pallas_call(
        body, name="ffn_up", grid=(2, nt),
        in_specs=[pl.BlockSpec((tm, D), lambda j, i: (i, 0)), pl.BlockSpec((1, D), lambda j, i: (0, 0)),
                  pl.BlockSpec((None, D, wc), lambda j, i: (j, 0, 0)),
                  pl.BlockSpec((None, D, wc), lambda j, i: (j + 2, 0, 0))],
        out_specs=[h_spec] + [ospec] * 3, out_shape=[jax.ShapeDtypeStruct((T + tm, D), BF16)] + [out] * 3,
        scratch_shapes=[pltpu.VMEM((D, 2 * wc), BF16)], compiler_params=_params(2))(x, g, win, win)


def _mm_res(name, res, a_list, w, scale, after):
    T, N = res.shape
    n = len(a_list)
    widths = [a.shape[1] for a in a_list]
    tm = _pick_tm(T, 512)

    def body(*refs):
        r_ref, a_refs, w_refs, o_ref = refs[0], refs[1:1 + n], refs[1 + n:1 + 2 * n], refs[2 + 2 * n]
        acc = _dot(a_refs[0][...].astype(BF16), w_refs[0][...])
        for a_ref, w_ref in zip(a_refs[1:], w_refs[1:]):
            acc = acc + _dot(a_ref[...].astype(BF16), w_ref[...])
        o_ref[...] = r_ref[...] + scale * acc

    w_specs, off = [], 0
    for k in widths:
        w_specs.append(pl.BlockSpec((k, N), lambda i, blk=off // k: (blk, 0)))
        off += k
    return pl.pallas_call(
        body, name=name, grid=(T // tm,),
        in_specs=[pl.BlockSpec((tm, N), lambda i: (i, 0))]
        + [pl.BlockSpec((tm, k), lambda i: (i, 0)) for k in widths] + w_specs + [ANY_SPEC],
        out_specs=pl.BlockSpec((tm, N), lambda i: (i, 0)),
        out_shape=jax.ShapeDtypeStruct((T, N), F32),
        compiler_params=_params(1))(res, *a_list, *([w] * n), after)


def _mm_tn(name, a, a_w, b, b_w, n_slots, a_by_slot, b_by_slot, scale, tm_cap=512):
    T = b.shape[0]
    tm = _pick_tm(T, tm_cap)
    nt = T // tm

    def body(a_ref, b_ref, o_ref, acc):
        i = pl.program_id(1)

        @pl.when(i == 0)
        def _():
            acc[...] = jnp.zeros_like(acc)

        acc[...] += _dot_tn(a_ref[...].astype(BF16), b_ref[...].astype(BF16))

        @pl.when(i == nt - 1)
        def _():
            o_ref[...] = (acc[...] * scale).astype(BF16)

    return pl.pallas_call(
        body, name=name, grid=(n_slots, nt),
        in_specs=[pl.BlockSpec((tm, a_w), (lambda s, i: (i, s)) if a_by_slot else (lambda s, i: (i, 0))),
                  pl.BlockSpec((tm, b_w), (lambda s, i: (i, s)) if b_by_slot else (lambda s, i: (i, 0)))],
        out_specs=pl.BlockSpec((None, a_w, b_w), lambda s, i: (s, 0, 0)),
        out_shape=jax.ShapeDtypeStruct((n_slots, a_w, b_w), BF16),
        scratch_shapes=[pltpu.VMEM((a_w, b_w), F32)], compiler_params=_params(2))(a, b)


def _mm_tn_pairs(name, a, b, b_w, n_slots):
    T = b.shape[0]
    a_w = a.shape[1]
    tm = _pick_tm(T, 512)
    nt = T // tm

    def body(a_ref, b_ref, o_ref, acc):
        i = pl.program_id(1)

        @pl.when(i == 0)
        def _():
            acc[...] = jnp.zeros_like(acc)

        acc[...] += _dot_tn(a_ref[...], b_ref[...])

        @pl.when(i == nt - 1)
        def _():
            o_ref[0] = acc[:, :b_w].astype(BF16)
            o_ref[1] = acc[:, b_w:].astype(BF16)

    return pl.pallas_call(
        body, name=name, grid=(n_slots // 2, nt),
        in_specs=[pl.BlockSpec((tm, a_w), lambda s, i: (i, 0)), pl.BlockSpec((tm, 2 * b_w), lambda s, i: (i, s))],
        out_specs=pl.BlockSpec((2, a_w, b_w), lambda s, i: (s, 0, 0)),
        out_shape=jax.ShapeDtypeStruct((n_slots, a_w, b_w), BF16),
        scratch_shapes=[pltpu.VMEM((a_w, 2 * b_w), F32)], compiler_params=_params(2))(a, b)


def _ffn_bwd(dx, wout, silu, dgate, win, x, g, after):
    T, D = dx.shape
    F = silu.shape[1]
    n_slots, _, wc = win.shape
    tm = _pick_tm(T, 256)

    def body(dx_ref, wout_hbm, s_ref, dgt_ref, win_hbm, x_ref, g_ref, after_ref,
             dgu_ref, dxo_ref, dg_ref, wout_v, wcat, sem):
        @pl.when(pl.program_id(0) == 0)
        def _():
            copies = [pltpu.make_async_copy(wout_hbm, wout_v, sem.at[n_slots])]
            copies += [pltpu.make_async_copy(win_hbm.at[s], wcat.at[s // 2, :, pl.ds((s % 2) * wc, wc)], sem.at[s])
                       for s in range(n_slots)]
            for cp in copies:
                cp.start()
            for cp in copies:
                cp.wait()
            dg_ref[...] = jnp.zeros_like(dg_ref)

        dxv = dx_ref[...]
        dact = 0.5 * _dot_nt(dxv.astype(BF16), wout_v[...])
        d_gate = (dact * dgt_ref[...].astype(F32)).astype(BF16)
        d_up = (dact * s_ref[...].astype(F32)).astype(BF16)
        dgu_ref[:, :F] = d_gate
        dgu_ref[:, F:] = d_up
        dh = _dot_nt(d_gate, wcat[0]) + _dot_nt(d_up, wcat[1])
        dxn, dg = _rms_bwd_tile(x_ref[...], g_ref[...], dh)
        dxo_ref[...] = dxv + dxn
        dg_ref[...] += dg

    row = pl.BlockSpec((tm, D), lambda i: (i, 0))
    vec = pl.BlockSpec((1, D), lambda i: (0, 0))
    act_spec = pl.BlockSpec((tm, F), lambda i: (i, 0))
    return pl.pallas_call(
        body, name="ffn_bwd", grid=(T // tm,),
        in_specs=[row, ANY_SPEC, act_spec, act_spec, ANY_SPEC, row, vec, ANY_SPEC],
        out_specs=[pl.BlockSpec((tm, 2 * F), lambda i: (i, 0)), row, vec],
        out_shape=[jax.ShapeDtypeStruct((T, 2 * F), BF16), jax.ShapeDtypeStruct((T, D), F32),
                   jax.ShapeDtypeStruct((1, D), F32)],
        scratch_shapes=[pltpu.VMEM((F, D), BF16), pltpu.VMEM((n_slots // 2, D, 2 * wc), BF16),
                        pltpu.SemaphoreType.DMA((n_slots + 1,))],
        compiler_params=_params(1))(dx, wout, silu, dgate, win, x, g, after)


def _ple_fwd(x, g, p, layer, wg, wp, after):
    T, D = x.shape
    P = p.shape[-1]
    tm = _pick_tm(T, 256)

    def body(x_ref, g_ref, p_ref, wg_ref, wp_ref, after_ref, hn_ref, xo_ref, gate_ref, pp_ref):
        xv = x_ref[...]
        hn = _rms_tile(xv, g_ref[...])
        hn_ref[...] = hn
        gate = _sigmoid(_dot(hn, wg_ref[...]))
        pp = _dot(p_ref[...].astype(BF16), wp_ref[...])
        gate_ref[...] = gate.astype(BF16)
        pp_ref[...] = pp.astype(BF16)
        xo_ref[...] = xv + gate * pp

    row = pl.BlockSpec((tm, D), lambda i: (i, 0))
    out = jax.ShapeDtypeStruct((T, D), F32)
    half = jax.ShapeDtypeStruct((T, D), BF16)
    return pl.pallas_call(
        body, name="ple_fwd", grid=(T // tm,),
        in_specs=[row, pl.BlockSpec((1, D), lambda i: (0, 0)), pl.BlockSpec((None, None, tm, P), lambda i: (layer, 0, i, 0)),
                  pl.BlockSpec((D, D), lambda i: (0, 0)), pl.BlockSpec((P, D), lambda i: (0, 0)), ANY_SPEC],
        out_specs=[row, row, row, row], out_shape=[half, out, half, half],
        compiler_params=_params(1))(x, g, p, wg, wp, after)


def _ple_fwd_loss(x, g, p, layer, wg, wp, tgt):
    T, D = x.shape
    P = p.shape[-1]
    tm = _pick_tm(T, 256)

    def body(x_ref, g_ref, p_ref, wg_ref, wp_ref, t_ref, hn_ref, dy_ref, gate_ref, pp_ref, loss_ref):
        xv = x_ref[...]
        hn = _rms_tile(xv, g_ref[...])
        hn_ref[...] = hn
        gate = _sigmoid(_dot(hn, wg_ref[...]))
        pp = _dot(p_ref[...].astype(BF16), wp_ref[...])
        gate_ref[...] = gate.astype(BF16)
        pp_ref[...] = pp.astype(BF16)
        e = (xv + gate * pp) - t_ref[...]
        dy_ref[...] = e / D

        @pl.when(pl.program_id(0) == 0)
        def _():
            loss_ref[...] = jnp.zeros_like(loss_ref)

        loss_ref[...] += 0.5 * jnp.sum(jnp.mean(e * e, axis=-1, keepdims=True), axis=0, keepdims=True)

    row = pl.BlockSpec((tm, D), lambda i: (i, 0))
    out = jax.ShapeDtypeStruct((T, D), F32)
    half = jax.ShapeDtypeStruct((T, D), BF16)
    return pl.pallas_call(
        body, name="ple_fwd_loss", grid=(T // tm,),
        in_specs=[row, pl.BlockSpec((1, D), lambda i: (0, 0)), pl.BlockSpec((None, None, tm, P), lambda i: (layer, 0, i, 0)),
                  pl.BlockSpec((D, D), lambda i: (0, 0)), pl.BlockSpec((P, D), lambda i: (0, 0)), row],
        out_specs=[row, row, row, row, pl.BlockSpec((8, LANES), lambda i: (0, 0))],
        out_shape=[half, out, half, half, jax.ShapeDtypeStruct((8, LANES), F32)],
        compiler_params=_params(1))(x, g, p, wg, wp, tgt)


def _ple_bwd(dx, gate, pp, hn, p, layer, x, g, wg, after):
    T, D = x.shape
    P = p.shape[-1]
    tm = _pick_tm(T, 256)
    nt = T // tm

    def body(dx_ref, gate_ref, pp_ref, hn_ref, p_ref, x_ref, g_ref, wg_ref, after_ref,
             dxo_ref, dg_ref, dwg_ref, dwp_ref, acc_g, acc_p):
        i = pl.program_id(0)

        @pl.when(i == 0)
        def _():
            acc_g[...] = jnp.zeros_like(acc_g)
            acc_p[...] = jnp.zeros_like(acc_p)
            dg_ref[...] = jnp.zeros_like(dg_ref)

        dxv = dx_ref[...]
        gate = gate_ref[...].astype(F32)
        dz = (dxv * pp_ref[...].astype(F32) * gate * (1.0 - gate)).astype(BF16)
        dpp = (dxv * gate).astype(BF16)
        acc_g[...] += _dot_tn(hn_ref[...], dz)
        acc_p[...] += _dot_tn(p_ref[...].astype(BF16), dpp)
        dxn, dg = _rms_bwd_tile(x_ref[...], g_ref[...], _dot_nt(dz, wg_ref[...]))
        dxo_ref[...] = dxv + dxn
        dg_ref[...] += dg

        @pl.when(i == nt - 1)
        def _():
            dwg_ref[...] = acc_g[...].astype(BF16)
            dwp_ref[...] = acc_p[...].astype(BF16)

    row = pl.BlockSpec((tm, D), lambda i: (i, 0))
    vec = pl.BlockSpec((1, D), lambda i: (0, 0))
    return pl.pallas_call(
        body, name="ple_bwd", grid=(nt,),
        in_specs=[row, row, row, row, pl.BlockSpec((None, None, tm, P), lambda i: (layer, 0, i, 0)), row, vec,
                  pl.BlockSpec((D, D), lambda i: (0, 0)), ANY_SPEC],
        out_specs=[row, vec, pl.BlockSpec((D, D), lambda i: (0, 0)), pl.BlockSpec((P, D), lambda i: (0, 0))],
        out_shape=[jax.ShapeDtypeStruct((T, D), F32), jax.ShapeDtypeStruct((1, D), F32),
                   jax.ShapeDtypeStruct((D, D), BF16), jax.ShapeDtypeStruct((P, D), BF16)],
        scratch_shapes=[pltpu.VMEM((D, D), F32), pltpu.VMEM((P, D), F32)],
        compiler_params=_params(1))(dx, gate, pp, hn, p, x, g, wg, after)


def _qkv_layout(D):
    n_a = D // (2 * LANES)
    n_b = D // (2 * LANES)
    n_kv = max(1, (2 * n_b) // GROUP_B) * HEAD_DIM // LANES
    return n_a, n_b, n_kv


def _dup_half(xv, half):
    rolled = pltpu.roll(xv, HEAD_DIM, 1)
    lo = _lo_lanes(xv.shape)
    return jnp.where(lo, xv, rolled) if half == 0 else jnp.where(lo, rolled, xv)


def _qkv_proj(x, g_mix, w, gains):
    T, D = x.shape
    W = w.shape[1]
    n_a, n_b, n_kv = _qkv_layout(D)
    tm = _pick_tm(T, 256)
    o_qb = 3 * n_a

    def norm(xv, gv, scale):
        ms = _seg_sum(xv * xv) * (1.0 / HEAD_DIM)
        return xv * lax.rsqrt(ms + EPS) * gv * scale

    def body(x_ref, gm_ref, w_ref, g_ref, h_ref, raw_ref, a_ref, b_ref):
        hv = _rms_tile(x_ref[...], gm_ref[...])
        h_ref[...] = hv
        raw = _dot(hv, w_ref[...])
        raw_ref[...] = raw

        def blk(cb):
            return raw[:, cb * LANES:(cb + 1) * LANES]

        def gn(cb):
            return g_ref[:, cb * LANES:(cb + 1) * LANES]

        for cb in range(n_a):
            a_ref[:, cb * LANES:(cb + 1) * LANES] = norm(blk(cb), gn(cb), Q_SCALE)
            cbk = n_a + cb
            a_ref[:, cbk * LANES:(cbk + 1) * LANES] = norm(blk(cbk), gn(cbk), 1.0)
            cbv = 2 * n_a + cb
            a_ref[:, cbv * LANES:(cbv + 1) * LANES] = blk(cbv)
        for cb in range(n_b):
            src = o_qb + cb
            b_ref[:, cb * LANES:(cb + 1) * LANES] = norm(blk(src), gn(src), Q_SCALE)
        for e in range(n_b):
            kvh = (2 * e) // GROUP_B
            ck = o_qb + n_b + kvh // 2
            cv = ck + n_kv
            kn = norm(blk(ck), gn(ck), 1.0)
            b_ref[:, (n_b + e) * LANES:(n_b + e + 1) * LANES] = _dup_half(kn, kvh % 2)
            b_ref[:, (2 * n_b + e) * LANES:(2 * n_b + e + 1) * LANES] = _dup_half(blk(cv), kvh % 2)

    wa, wb = 3 * n_a * LANES, 3 * n_b * LANES

    def rows(width):
        return pl.BlockSpec((tm, width), lambda i: (i, 0))

    return pl.pallas_call(
        body, name="qkv_proj", grid=(T // tm,),
        in_specs=[rows(D), pl.BlockSpec((1, D), lambda i: (0, 0)), pl.BlockSpec((D, W), lambda i: (0, 0)),
                  pl.BlockSpec((1, W), lambda i: (0, 0))],
        out_specs=[rows(D), rows(W), rows(wa), rows(wb)],
        out_shape=[jax.ShapeDtypeStruct((T, D), BF16), jax.ShapeDtypeStruct((T, W), F32),
                   jax.ShapeDtypeStruct((T, wa), F32), jax.ShapeDtypeStruct((T, wb), F32)],
        compiler_params=_params(1))(x, g_mix, w, gains)


def _qkv_bwd(raw, gains, d_a, d_b, w, h, x, g_mix, dx_in, after):
    T, W = raw.shape
    D = x.shape[1]
    n_a, n_b, n_kv = _qkv_layout(D)
    tm = _pick_tm(T, 256)
    nt = T // tm
    o_qb = 3 * n_a

    def body(raw_ref, g_ref, daq, dak, dav, dbq, dbk, dbv, w_ref, h_ref, x_ref, gm_ref, dxi_ref, after_ref,
             dx_ref, dgm_ref, dg_ref, dw_ref, acc):
        i = pl.program_id(0)

        @pl.when(i == 0)
        def _():
            dg_ref[...] = jnp.zeros_like(dg_ref)
            dgm_ref[...] = jnp.zeros_like(dgm_ref)
            acc[...] = jnp.zeros_like(acc)

        def cols(ref, cb):
            return ref[:, cb * LANES:(cb + 1) * LANES]

        pieces = [None] * (W // LANES)

        def norm_bwd(cb, dy, scale):
            xv = cols(raw_ref, cb)
            gv = cols(g_ref, cb)
            r = lax.rsqrt(_seg_sum(xv * xv) * (1.0 / HEAD_DIM) + EPS)
            xh = xv * r
            dys = dy * scale
            dyg = dys * gv
            dxv = r * (dyg - xh * (_seg_sum(dyg * xh) * (1.0 / HEAD_DIM)))
            pieces[cb] = dxv.astype(BF16)
            dg_ref[:, cb * LANES:(cb + 1) * LANES] += jnp.sum(dys * xh, axis=0, keepdims=True)

        def fold(ref, kv_blk):
            halves = []
            for half in range(2):
                kvh = 2 * kv_blk + half
                blocks = [e for e in range(n_b) if (2 * e) // GROUP_B == kvh]
                s = cols(ref, blocks[0])
                for e in blocks[1:]:
                    s = s + cols(ref, e)
                halves.append(s + pltpu.roll(s, HEAD_DIM, 1))
            return jnp.where(_lo_lanes(halves[0].shape), halves[0], halves[1])

        for cb in range(n_a):
            norm_bwd(cb, cols(daq, cb), Q_SCALE)
        for cb in range(n_a):
            norm_bwd(n_a + cb, cols(dak, cb), 1.0)
        for cb in range(n_a):
            pieces[2 * n_a + cb] = cols(dav, cb).astype(BF16)
        for cb in range(n_b):
            norm_bwd(o_qb + cb, cols(dbq, cb), Q_SCALE)
        for kb in range(n_kv):
            norm_bwd(o_qb + n_b + kb, fold(dbk, kb), 1.0)
        for kb in range(n_kv):
            pieces[o_qb + n_b + n_kv + kb] = fold(dbv, kb).astype(BF16)

        hv = h_ref[...]
        dh = None
        for c0 in range(0, W // LANES, QKV_CHUNK):
            chunk = jnp.concatenate(pieces[c0:c0 + QKV_CHUNK], axis=1)
            lanes = slice(c0 * LANES, (c0 + QKV_CHUNK) * LANES)
            part = _dot_nt(chunk, w_ref[:, lanes])
            dh = part if dh is None else dh + part
            acc[:, lanes] += _dot_tn(hv, chunk)
        dxn, dgm = _rms_bwd_tile(x_ref[...], gm_ref[...], dh)
        dx_ref[...] = dxi_ref[...] + dxn
        dgm_ref[...] += dgm

        @pl.when(i == nt - 1)
        def _():
            dw_ref[...] = acc[...].astype(BF16)

    hw_a, hw_b = n_a * LANES, n_b * LANES

    def rows(width):
        return pl.BlockSpec((tm, width), lambda i: (i, 0))

    def fixed(r, c):
        return pl.BlockSpec((r, c), lambda i: (0, 0))

    return pl.pallas_call(
        body, name="qkv_bwd", grid=(nt,),
        in_specs=[rows(W), fixed(1, W)] + [rows(hw_a)] * 3 + [rows(hw_b)] * 3
        + [fixed(D, W), rows(D), rows(D), fixed(1, D), rows(D), ANY_SPEC],
        out_specs=[rows(D), fixed(1, D), fixed(1, W), fixed(D, W)],
        out_shape=[jax.ShapeDtypeStruct((T, D), F32), jax.ShapeDtypeStruct((1, D), F32),
                   jax.ShapeDtypeStruct((1, W), F32), jax.ShapeDtypeStruct((D, W), BF16)],
        scratch_shapes=[pltpu.VMEM((D, W), F32)],
        compiler_params=_params(1))(raw, gains, *d_a, *d_b, w, h, x, g_mix, dx_in, after)


def _t5_bucket_np(rel):
    half = N_BUCKETS // 2
    max_exact = half // 2
    ret = np.where(rel > 0, half, 0)
    n = np.abs(rel)
    nf = np.maximum(n, 1).astype(np.float32)
    large = max_exact + (np.log(nf / np.float32(max_exact)) / np.float32(math.log(MAX_DISTANCE / max_exact))
                         * np.float32(half - max_exact)).astype(np.int32)
    large = np.minimum(large, half - 1)
    return ret + np.where(n < max_exact, n, large)


def _window_pad(radius):
    assert radius <= QBLOCK
    return HEAD_DIM if radius <= HEAD_DIM else QBLOCK


def _bucket_maps(configs):
    pad = _window_pad(configs[0][0])
    q = np.arange(QBLOCK)[:, None]
    kk = np.arange(QBLOCK + 2 * pad)[None, :]
    rel = kk - pad - q
    maps = [np.where(np.abs(rel) <= radius, _t5_bucket_np(rel * dil), -1) for radius, dil in configs]
    return np.stack(maps).astype(np.int32)


def _bias_build(rel_bias, bmaps, col0, pad):
    n_sets, _, W = bmaps.shape
    n_buckets, n_cols = rel_bias.shape
    n_heads = n_cols // 2

    def body(rb_ref, bm_ref, o_ref):
        h = pl.program_id(1)
        bm = bm_ref[0:1, :]
        head = lax.broadcasted_iota(jnp.int32, (n_buckets, n_cols), 1) == col0 + h
        per_bucket = jnp.sum(jnp.where(head, rb_ref[...], 0.0), axis=1, keepdims=True)
        hit = lax.broadcasted_iota(jnp.int32, (n_buckets, W), 0) == bm
        row0 = jnp.sum(jnp.where(hit, per_bucket, 0.0), axis=0, keepdims=True) + jnp.where(bm < 0, NEG, 0.0)
        tile = pltpu.roll(jnp.broadcast_to(row0, (QBLOCK, W)), 0, 1, stride=1, stride_axis=0)
        col = lax.broadcasted_iota(jnp.int32, (QBLOCK, W), 1)
        left, right = col < pad, col >= pad + QBLOCK
        o_ref[0] = tile
        o_ref[1] = jnp.where(left, NEG, tile)
        o_ref[2] = jnp.where(right, NEG, tile)
        o_ref[3] = jnp.where(left | right, NEG, tile)

    return pl.pallas_call(
        body, name="bias_build", grid=(n_sets, n_heads),
        in_specs=[pl.BlockSpec((n_buckets, n_cols), lambda s, h: (0, 0)),
                  pl.BlockSpec((None, QBLOCK, W), lambda s, h: (s, 0, 0))],
        out_specs=pl.BlockSpec((None, 4, QBLOCK, W), lambda s, h: (s, 0, h, 0)),
        out_shape=jax.ShapeDtypeStruct((n_sets, 4, n_heads * QBLOCK, W), F32),
        compiler_params=_params(2))(rel_bias, bmaps)


def _bias_grad(dtiles, bmaps, col0):
    n_sets, _, W = bmaps.shape
    n_heads = dtiles[0].shape[1]
    n_l = len(dtiles)

    def body(*refs):
        bm_ref, o_ref = refs[0], refs[1 + n_l]
        s, h = pl.program_id(0), pl.program_id(1)

        @pl.when((s == 0) & (h == 0))
        def _():
            o_ref[...] = jnp.zeros_like(o_ref)

        d = refs[1][...]
        for r in refs[2:1 + n_l]:
            d = d + r[...]
        acc8 = d[0:8, :]
        for a in range(1, QBLOCK // 8):
            acc8 = acc8 + pltpu.roll(d[8 * a:8 * a + 8, :], W - 8 * a, 1)
        per_offset = acc8[0:1, :]
        for b in range(1, 8):
            per_offset = per_offset + pltpu.roll(acc8[b:b + 1, :], W - b, 1)
        bucket = lax.broadcasted_iota(jnp.int32, (N_BUCKETS, W), 0)
        hit = bucket == bm_ref[0:1, :]
        per_bucket = jnp.sum(jnp.where(hit, per_offset, 0.0), axis=1, keepdims=True)
        lanes = lax.broadcasted_iota(jnp.int32, o_ref.shape, 1)
        o_ref[...] += jnp.where(lanes == col0 + h, per_bucket, 0.0)

    tile = pl.BlockSpec((None, None, QBLOCK, W), lambda s, h: (s, h, 0, 0))
    return pl.pallas_call(
        body, name="bias_grad", grid=(n_sets, n_heads),
        in_specs=[pl.BlockSpec((None, QBLOCK, W), lambda s, h: (s, 0, 0))] + [tile] * n_l,
        out_specs=pl.BlockSpec((N_BUCKETS, LANES), lambda s, h: (0, 0)),
        out_shape=jax.ShapeDtypeStruct((N_BUCKETS, LANES), F32), compiler_params=_params(2))(bmaps, *dtiles)


def _rows(l_start, n, d, r):
    if d == 1:
        return pl.ds(pl.multiple_of(l_start, 8), n)
    return pl.ds(l_start * d + r, n, stride=d)


def _stack_heads(xv, lo):
    z = jnp.zeros_like(xv)
    return jnp.concatenate([jnp.where(lo, xv, z), jnp.where(lo, z, xv)], axis=0)


def _unstack_heads(xv, lo):
    return jnp.where(lo, xv[:QBLOCK], xv[QBLOCK:])


def _per_head_rows(v0, v1):
    if jnp.ndim(v0) == 0:
        return jnp.where(lax.broadcasted_iota(jnp.int32, (2 * QBLOCK, 1), 0) < QBLOCK, v0, v1)
    return jnp.concatenate([v0, v1], axis=0)


def _block_geometry(b, nb_sub, pad):
    r, lb = b // nb_sub, b % nb_sub
    l0 = lb * QBLOCK
    lp = jnp.maximum(l0 - pad, 0)
    ln = jnp.minimum(l0 + QBLOCK, nb_sub * QBLOCK - pad)
    return r, l0, lp, ln, (lb == 0).astype(jnp.int32) + 2 * (lb == nb_sub - 1).astype(jnp.int32)


def _window(ref, l0, lp, ln, pad, d, r):
    return jnp.concatenate([ref[_rows(lp, pad, d, r), :], ref[_rows(l0, QBLOCK, d, r), :],
                            ref[_rows(ln, pad, d, r), :]], axis=0)


def _attn_fwd(qkv, bias, sink, dils, pad, after):
    T = qkv.shape[0]
    hw = qkv.shape[1] // 3
    ng = hw // LANES
    n_br = len(dils)
    n_blocks = T // QBLOCK
    W = QBLOCK + 2 * pad
    chunk = 256

    def body(sink_ref, q_ref, k_ref, v_ref, bias_ref, after_ref, o_ref, lse_ref, *scratch):
        g = pl.program_id(0)
        lo = _lo_lanes((QBLOCK, LANES))
        snk = _per_head_rows(sink_ref[2 * g], sink_ref[2 * g + 1])
        for c, d in enumerate(dils):
            nb_sub = n_blocks // d
            o_dst = scratch[0].at[c] if n_br > 1 else o_ref
            l_dst = scratch[1].at[c] if n_br > 1 else lse_ref

            def block(b, carry, c=c, d=d, nb_sub=nb_sub, o_dst=o_dst, l_dst=l_dst):
                r, l0, lp, ln, edge = _block_geometry(b, nb_sub, pad)
                q = _stack_heads(q_ref[_rows(l0, QBLOCK, d, r), :].astype(BF16), lo)
                k = _window(k_ref, l0, lp, ln, pad, d, r).astype(BF16)
                v = _window(v_ref, l0, lp, ln, pad, d, r).astype(BF16)
                s = _dot_nt(q, k) + bias_ref[c, edge]
                m = jnp.maximum(jnp.max(s, axis=1, keepdims=True), snk)
                p = jnp.exp(s - m)
                den = jnp.sum(p, axis=1, keepdims=True) + jnp.exp(snk - m)
                o_dst[_rows(l0, QBLOCK, d, r), :] = _unstack_heads(_dot(p.astype(BF16), v) / den, lo)
                l_dst[_rows(l0, QBLOCK, d, r), :] = _unstack_heads(
                    jnp.broadcast_to(m + jnp.log(den), (2 * QBLOCK, LANES)), lo)
                return carry

            lax.fori_loop(0, n_blocks, block, 0, unroll=FWD_UNROLL)

        if n_br > 1:
            def merge(i, carry):
                rs = pl.ds(pl.multiple_of(i * chunk, chunk), chunk)
                ls = [scratch[1][c, rs, :] for c in range(n_br)]
                m = ls[0]
                for t in ls[1:]:
                    m = jnp.maximum(m, t)
                ws = [jnp.exp(t - m) for t in ls]
                z = ws[0]
                acc = ws[0] * scratch[0][0, rs, :]
                for c in range(1, n_br):
                    z = z + ws[c]
                    acc = acc + ws[c] * scratch[0][c, rs, :]
                o_ref[rs, :] = acc / z
                lse_ref[rs, :] = m + jnp.log(z)
                return carry

            lax.fori_loop(0, T // chunk, merge, 0)

    def col(base):
        return pl.BlockSpec((T, LANES), lambda g: (0, base + g))

    out = jax.ShapeDtypeStruct((T, hw), F32)
    scratch = [pltpu.VMEM((n_br, T, LANES), F32)] * 2 if n_br > 1 else []
    return pl.pallas_call(
        body, name="attn_fwd", grid=(ng,),
        in_specs=[pl.BlockSpec(memory_space=pltpu.SMEM), col(0), col(ng), col(2 * ng),
                  pl.BlockSpec((n_br, 4, 2 * QBLOCK, W), lambda g: (0, 0, g, 0)), ANY_SPEC],
        out_specs=[col(0), col(0)], out_shape=[out, out], scratch_shapes=scratch,
        compiler_params=_params(1))(sink, qkv, qkv, qkv, bias, after)


def _attn_bwd(qkv, bias, sink, dils, pad, do, lse, dd, col_base, after):
    T = qkv.shape[0]
    hw = qkv.shape[1] // 3
    ng = hw // LANES
    n_br = len(dils)
    n_blocks = T // QBLOCK
    W = QBLOCK + 2 * pad

    def body(sink_ref, q_ref, k_ref, v_ref, bias_ref, do_ref, lse_ref, dd_ref, after_ref,
             dq_ref, dk_ref, dv_ref, dt_ref, ds_ref):
        g = pl.program_id(0)
        dq_ref[...] = jnp.zeros_like(dq_ref)
        dk_ref[...] = jnp.zeros_like(dk_ref)
        dv_ref[...] = jnp.zeros_like(dv_ref)
        dt_ref[...] = jnp.zeros_like(dt_ref)
        ds_ref[...] = jnp.zeros_like(ds_ref)
        lo = _lo_lanes((QBLOCK, LANES))
        snk = jnp.where(lo, sink_ref[2 * g], sink_ref[2 * g + 1])
        for c, d in enumerate(dils):
            nb_sub = n_blocks // d

            def block(b, carry, c=c, d=d, nb_sub=nb_sub):
                r, l0, lp, ln, edge = _block_geometry(b, nb_sub, pad)
                rows_q = _rows(l0, QBLOCK, d, r)
                q = _stack_heads(q_ref[rows_q, :].astype(BF16), lo)
                k = _window(k_ref, l0, lp, ln, pad, d, r).astype(BF16)
                v = _window(v_ref, l0, lp, ln, pad, d, r).astype(BF16)
                dob = _stack_heads(do_ref[rows_q, :].astype(BF16), lo)
                lse_b = lse_ref[rows_q, :]
                dd_b = dd_ref[rows_q, :]
                s = _dot_nt(q, k) + bias_ref[c, edge]
                p = jnp.exp(s - _per_head_rows(lse_b[:, 0:1], lse_b[:, HEAD_DIM:HEAD_DIM + 1]))
                ds = p * (_dot_nt(dob, v) - _per_head_rows(dd_b[:, 0:1], dd_b[:, HEAD_DIM:HEAD_DIM + 1]))
                dsb = ds.astype(BF16)
                dkw = _dot_tn(dsb, q)
                dvw = _dot_tn(p.astype(BF16), dob)
                dt_ref[c] += ds
                dq_ref[rows_q, :] += _unstack_heads(_dot(dsb, k), lo)
                ds_ref[0:1, :] += jnp.sum(-jnp.exp(snk - lse_b) * dd_b, axis=0, keepdims=True)
                for part, (start, n) in zip((0, pad, pad + QBLOCK), ((lp, pad), (l0, QBLOCK), (ln, pad))):
                    dk_ref[_rows(start, n, d, r), :] += dkw[part:part + n]
                    dv_ref[_rows(start, n, d, r), :] += dvw[part:part + n]
                return carry

            lax.fori_loop(0, n_blocks, block, 0, unroll=BWD_UNROLL)

    def col(base):
        return pl.BlockSpec((T, LANES), lambda g: (0, base + g))

    tile = pl.BlockSpec((n_br, 2 * QBLOCK, W), lambda g: (0, g, 0))
    full = jax.ShapeDtypeStruct((T, hw), F32)
    dq, dk, dv, dt, dsink = pl.pallas_call(
        body, name="attn_bwd", grid=(ng,),
        in_specs=[pl.BlockSpec(memory_space=pltpu.SMEM), col(0), col(ng), col(2 * ng),
                  pl.BlockSpec((n_br, 4, 2 * QBLOCK, W), lambda g: (0, 0, g, 0)),
                  col(col_base), col(0), col(col_base), ANY_SPEC],
        out_specs=[col(0), col(0), col(0), tile, pl.BlockSpec((None, 8, LANES), lambda g: (g, 0, 0))],
        out_shape=[full, full, full, jax.ShapeDtypeStruct((n_br, 2 * ng * QBLOCK, W), F32),
                   jax.ShapeDtypeStruct((ng, 8, LANES), F32)],
        compiler_params=_params(1))(sink, qkv, qkv, qkv, bias, do, lse, dd, after)
    return dq, dk, dv, dt.reshape(n_br, 2 * ng, QBLOCK, W), dsink


def _mix_bwd_in(dx, wo, o_list):
    T, D = dx.shape
    widths = [o.shape[1] for o in o_list]
    hw = sum(widths)
    n = len(o_list)
    tm = _pick_tm(T, 256)
    nt = T // tm

    def body(*refs):
        dx_ref, w_ref, o_refs = refs[0], refs[1], refs[2:2 + n]
        do_ref, dd_ref, dw_ref, acc = refs[2 + n:]
        i = pl.program_id(0)

        @pl.when(i == 0)
        def _():
            acc[...] = jnp.zeros_like(acc)

        dxb = dx_ref[...].astype(BF16)
        dov = _dot_nt(dxb, w_ref[...])
        do_ref[...] = dov
        off = 0
        for o_ref, k in zip(o_refs, widths):
            ov = o_ref[...]
            prod = dov[:, off:off + k] * ov
            for cb in range(k // LANES):
                dd_ref[:, off + cb * LANES:off + (cb + 1) * LANES] = _seg_sum(prod[:, cb * LANES:(cb + 1) * LANES])
            acc[off:off + k, :] += _dot_tn(ov.astype(BF16), dxb)
            off += k

        @pl.when(i == nt - 1)
        def _():
            dw_ref[...] = acc[...].astype(BF16)

    row = pl.BlockSpec((tm, hw), lambda i: (i, 0))
    out = jax.ShapeDtypeStruct((T, hw), F32)
    return pl.pallas_call(
        body, name="mix_bwd_in", grid=(nt,),
        in_specs=[pl.BlockSpec((tm, D), lambda i: (i, 0)), pl.BlockSpec((hw, D), lambda i: (0, 0))]
        + [pl.BlockSpec((tm, k), lambda i: (i, 0)) for k in widths],
        out_specs=[row, row, pl.BlockSpec((hw, D), lambda i: (0, 0))],
        out_shape=[out, out, jax.ShapeDtypeStruct((hw, D), BF16)],
        scratch_shapes=[pltpu.VMEM((hw, D), F32)], compiler_params=_params(1))(dx, wo, *o_list)


def _mesh_pos():
    return lax.axis_index("x"), lax.axis_index("y"), lax.axis_index("c")


def _my_chip():
    return 2 * lax.axis_index("x") + lax.axis_index("y")


def _other_chips(x, y):
    return [(1 - x, y), (x, 1 - y), (1 - x, 1 - y)]


def _half_rows(rows, cc):
    hr = rows // 2
    return pl.ds(pl.multiple_of(cc * hr, 16), hr)


def _cast_into_slot(w, l):
    _, R, C = w.shape
    tm = _pick_tm(R, 512)

    def body(w_ref, o_ref):
        o_ref[...] = w_ref[...].astype(BF16)

    return pl.pallas_call(
        body, name="cast_into_slot", grid=(R // tm,),
        in_specs=[pl.BlockSpec((None, tm, C), lambda i: (l, i, 0))],
        out_specs=pl.BlockSpec((None, tm, C), lambda i: (_my_chip(), i, 0)),
        out_shape=jax.ShapeDtypeStruct((N_CHIPS, R, C), BF16), compiler_params=_params(1))(w)


def _split_start(name, arrays, make_copies, n_sem, after):
    n = len(arrays)
    n_in = n + (0 if after is None else 1)

    def body(*refs):
        send_s, recv_s = refs[n_in], refs[n_in + 1]
        token = refs[n_in + 2 + n]
        for send, _ in make_copies(refs[:n], send_s, recv_s):
            send.start()
        token[...] = jnp.zeros_like(token)

    res = pl.pallas_call(
        body, name=name,
        out_shape=(pltpu.SemaphoreType.DMA((n_sem,)), pltpu.SemaphoreType.DMA((n_sem,)),
                   *[pltpu.HBM(a.shape, a.dtype) for a in arrays], jax.ShapeDtypeStruct((8, LANES), F32)),
        in_specs=[HBM_SPEC] * n + [ANY_SPEC] * (n_in - n),
        out_specs=(SEM_SPEC, SEM_SPEC, *([HBM_SPEC] * n), pl.BlockSpec(memory_space=pltpu.VMEM)),
        input_output_aliases={i: 2 + i for i in range(n)},
        compiler_params=pltpu.CompilerParams(has_side_effects=pltpu.SideEffectType.DATAFLOW_SIDE_EFFECTING),
    )(*[pltpu.with_memory_space_constraint(a, pltpu.HBM) for a in arrays], *([] if after is None else [after]))
    return res[0], res[1], list(res[2:2 + n]), res[2 + n]


def _split_wait(name, send_s, recv_s, arrays, make_copies, after):
    n = len(arrays)

    def body(*refs):
        for send, landed in make_copies(refs[:n], refs[n], refs[n + 1]):
            send.wait_send()
            landed.wait_recv()

    return list(pl.pallas_call(
        body, name=name, out_shape=[pltpu.HBM(a.shape, a.dtype) for a in arrays],
        in_specs=[HBM_SPEC] * n + [SEM_SPEC, SEM_SPEC, ANY_SPEC], out_specs=[HBM_SPEC] * n,
        input_output_aliases={i: i for i in range(n)},
        compiler_params=pltpu.CompilerParams(has_side_effects=pltpu.SideEffectType.DATAFLOW_SIDE_EFFECTING),
    )(*arrays, send_s, recv_s, after))


def _gather_copies(shapes):
    n = len(shapes)

    def make(refs, send_s, recv_s):
        x, y, c = _mesh_pos()
        my = 2 * x + y
        copies = []
        for w in range(n):
            for k, (px, py) in enumerate(_other_chips(x, y)):
                def part(slot, w=w):
                    return refs[w].at[slot, _half_rows(shapes[w][1], c), :]
                sems = dict(send_sem=send_s.at[k * n + w], recv_sem=recv_s.at[k * n + w],
                            device_id=(px, py, c), device_id_type=MESH)
                copies.append((pltpu.make_async_remote_copy(src_ref=part(my), dst_ref=part(my), **sems),
                               pltpu.make_async_remote_copy(src_ref=part(2 * px + py), dst_ref=part(2 * px + py), **sems)))
        return copies

    return make


def _forward_copies(shapes):
    n = len(shapes)

    def make(refs, send_s, recv_s):
        x, y, c = _mesh_pos()
        copies = []
        for w in range(n):
            for k, (px, py) in enumerate(_other_chips(x, y)):
                def part(cc, w=w, slot=2 * px + py):
                    return refs[w].at[slot, _half_rows(shapes[w][1], cc), :]
                sems = dict(send_sem=send_s.at[k * n + w], recv_sem=recv_s.at[k * n + w],
                            device_id=(x, y, 1 - c), device_id_type=MESH)
                copies.append((pltpu.make_async_remote_copy(src_ref=part(c), dst_ref=part(c), **sems),
                               pltpu.make_async_remote_copy(src_ref=part(1 - c), dst_ref=part(1 - c), **sems)))
        return copies

    return make


def _pair_forward(bufs):
    n = len(bufs)
    make = _forward_copies([b.shape for b in bufs])

    def body(*refs):
        copies = make(refs[n:2 * n], refs[2 * n], refs[2 * n + 1])
        for send, _ in copies:
            send.start()
        for _, landed in copies:
            landed.wait_recv()
        for send, _ in copies:
            send.wait_send()

    return list(pl.pallas_call(
        body, name="ag_pair_forward", in_specs=[HBM_SPEC] * n, out_specs=[HBM_SPEC] * n,
        out_shape=[jax.ShapeDtypeStruct(b.shape, b.dtype) for b in bufs],
        input_output_aliases={w: w for w in range(n)},
        scratch_shapes=[pltpu.SemaphoreType.DMA((3 * n,)), pltpu.SemaphoreType.DMA((3 * n,))],
    )(*bufs))


def _pair_exchange_copies(shapes):
    n = len(shapes)

    def make(refs, send_s, recv_s):
        x, y, c = _mesh_pos()
        copies = []
        for t in range(n):
            sems = dict(send_sem=send_s.at[t], recv_sem=recv_s.at[t], device_id=(x, y, 1 - c), device_id_type=MESH)
            land = refs[n + t]
            copies.append((pltpu.make_async_remote_copy(
                src_ref=refs[t].at[:, _half_rows(shapes[t][1], 1 - c), :], dst_ref=land, **sems),
                pltpu.make_async_remote_copy(src_ref=land, dst_ref=land, **sems)))
        return copies

    return make


def _pair_share_copies(shapes):
    n = len(shapes)

    def make(refs, send_s, recv_s):
        x, y, c = _mesh_pos()
        copies = []
        for t in range(n):
            def half(cc, t=t):
                return refs[t].at[_half_rows(shapes[t][0], cc), :]
            sems = dict(send_sem=send_s.at[t], recv_sem=recv_s.at[t], device_id=(x, y, 1 - c), device_id_type=MESH)
            copies.append((pltpu.make_async_remote_copy(src_ref=half(c), dst_ref=half(c), **sems),
                           pltpu.make_async_remote_copy(src_ref=half(1 - c), dst_ref=half(1 - c), **sems)))
        return copies

    return make


def _rs_add_pair(grad, recv):
    n_slot, hr, C = recv.shape
    tm = _pick_tm(hr, 192)
    nb = hr // tm

    def body(a_ref, b_ref, o_ref):
        o_ref[...] = (a_ref[...].astype(F32) + b_ref[...].astype(F32)).astype(BF16)

    blk = pl.BlockSpec((n_slot, tm, C), lambda i: (0, i, 0))
    return pl.pallas_call(
        body, name="rs_add_pair", grid=(nb,),
        in_specs=[pl.BlockSpec((n_slot, tm, C), lambda i: (0, lax.axis_index("c") * nb + i, 0)), blk],
        out_specs=blk, out_shape=jax.ShapeDtypeStruct(recv.shape, BF16), compiler_params=_params(1))(grad, recv)


def _scatter_copies(n):
    def make(refs, send_s, recv_s):
        x, y, c = _mesh_pos()
        copies = []
        for t in range(n):
            for k, (px, py) in enumerate(_other_chips(x, y)):
                sems = dict(send_sem=send_s.at[3 * t + k], recv_sem=recv_s.at[3 * t + k],
                            device_id=(px, py, c), device_id_type=MESH)
                land = refs[n + t].at[k]
                copies.append((pltpu.make_async_remote_copy(src_ref=refs[t].at[2 * px + py], dst_ref=land, **sems),
                               pltpu.make_async_remote_copy(src_ref=land, dst_ref=land, **sems)))
        return copies

    return make


def _rs_add_chips(part, recv):
    _, hr, C = part.shape
    tm = _pick_tm(hr, 256)
    nb = hr // tm

    def body(a_ref, r0, r1, r2, o_ref):
        o_ref[...] = ((a_ref[...].astype(F32) + r0[...].astype(F32)) + r1[...].astype(F32)) + r2[...].astype(F32)

    def rel(k):
        return pl.BlockSpec((None, tm, C), lambda i: (k, i, 0))

    return pl.pallas_call(
        body, name="rs_add_chips", grid=(nb,),
        in_specs=[pl.BlockSpec((None, tm, C), lambda i: (_my_chip(), i, 0)), rel(0), rel(1), rel(2)],
        out_specs=pl.BlockSpec((tm, C), lambda i: (lax.axis_index("c") * nb + i, 0)),
        out_shape=jax.ShapeDtypeStruct((2 * hr, C), F32), compiler_params=_params(1))(part, recv, recv, recv)


def _allreduce_small(v, after):
    rows = v.shape[0]

    def body(v_ref, after_ref, o_ref, buf, send_s, recv_s):
        x, y, c = _mesh_pos()
        me = 4 * x + 2 * y + c
        buf[me] = v_ref[...]
        copies = []
        for r in range(1, N_DEV):
            px = 1 - x if r & 4 else x
            py = 1 - y if r & 2 else y
            pc = 1 - c if r & 1 else c
            send = pltpu.make_async_remote_copy(
                src_ref=v_ref, dst_ref=buf.at[me], send_sem=send_s.at[r - 1], recv_sem=recv_s.at[r - 1],
                device_id=(px, py, pc), device_id_type=MESH)
            peer_slot = buf.at[4 * px + 2 * py + pc]
            landed = pltpu.make_async_remote_copy(
                src_ref=peer_slot, dst_ref=peer_slot, send_sem=send_s.at[r - 1], recv_sem=recv_s.at[r - 1],
                device_id=(px, py, pc), device_id_type=MESH)
            copies.append((send, landed))
        for send, _ in copies:
            send.start()
        for _, landed in copies:
            landed.wait_recv()
        for send, _ in copies:
            send.wait_send()
        acc = buf[0]
        for j in range(1, N_DEV):
            acc = acc + buf[j]
        o_ref[...] = acc

    vm = pl.BlockSpec(memory_space=pltpu.VMEM)
    return pl.pallas_call(
        body, name="allreduce_small", in_specs=[vm, ANY_SPEC], out_specs=vm,
        out_shape=jax.ShapeDtypeStruct((rows, LANES), F32),
        scratch_shapes=[pltpu.VMEM((N_DEV, rows, LANES), F32), pltpu.SemaphoreType.DMA((N_DEV - 1,)),
                        pltpu.SemaphoreType.DMA((N_DEV - 1,))],
    )(v, after)


def _adamw_fn(w, g, m, v):
    m2 = ADAM_B1 * m + (1.0 - ADAM_B1) * g
    v2 = ADAM_B2 * v + (1.0 - ADAM_B2) * (g * g)
    m_hat = m2 / (1.0 - ADAM_B1 ** ADAM_STEP)
    v_hat = v2 / (1.0 - ADAM_B2 ** ADAM_STEP)
    delta = -ADAM_LR * (m_hat / (jnp.sqrt(v_hat) + ADAM_EPS) + ADAM_WD * w)
    return g, delta, m2, v2


def _adamw_layer(w, g, m, v, l, prev, after):
    NL, R, C = w.shape
    tm = _pick_tm(R, 256)
    n_prev = 0 if prev is None else 4

    def body(w_ref, g_ref, m_ref, v_ref, after_ref, *rest):
        outs = rest[n_prev:]
        for o_ref, val in zip(outs, _adamw_fn(w_ref[...], g_ref[...], m_ref[...], v_ref[...])):
            o_ref[...] = val

    lay = pl.BlockSpec((None, tm, C), lambda i: (l, i, 0))
    shape = jax.ShapeDtypeStruct((NL, R, C), F32)
    return pl.pallas_call(
        body, name="adamw", grid=(R // tm,),
        in_specs=[lay, pl.BlockSpec((tm, C), lambda i: (i, 0)), lay, lay, ANY_SPEC] + [ANY_SPEC] * n_prev,
        out_specs=[lay] * 4, out_shape=[shape] * 4,
        input_output_aliases={5 + j: j for j in range(n_prev)},
        compiler_params=_params(1))(w, g, m, v, after, *(prev or []))


def _pack_small(parts):
    out = []
    for a in parts:
        flat = a.reshape(-1)
        n = -(-flat.shape[0] // (8 * LANES)) * 8 * LANES
        out.append(jnp.pad(flat, (0, n - flat.shape[0])).reshape(-1, LANES))
    return jnp.concatenate(out, axis=0)


def _unpack_small(packed, like):
    out, r = [], 0
    for a in like:
        size = int(np.prod(a.shape))
        rows = -(-size // (8 * LANES)) * 8
        out.append(packed[r:r + rows].reshape(-1)[:size].reshape(a.shape))
        r += rows
    return out


def _ffn_forward(x, g, win, wout):
    h, silu, dgate, act = _ffn_up(x, g, win)
    wout, after = wout(act) if callable(wout) else (wout, act)
    return _mm_res("ffn_down", x, [act], wout, 0.5, after), (x, h, silu, dgate, act)


def _ffn_backward(dx, saved, g, win, wout, after):
    x, h, silu, dgate, act = saved
    D = x.shape[1]
    wc = win.shape[2]
    dgu, dx_in, dg = _ffn_bwd(dx, wout, silu, dgate, win, x, g, after)
    dwout = _mm_tn("dw_ffn_out", act, wc, dx, D, 2, True, False, 0.5)
    dwin = _mm_tn_pairs("dw_ffn_in", h, dgu, wc, 4)
    return dx_in, dg, dwin, dwout.reshape(N_CHIPS, -1, D)


GROUP_MID = ("w_o", "ffn2_w_in", "ffn2_w_out", "w_ple_gate", "w_ple_proj")
GROUP_LAST = ("w_qkv", "ffn1_w_in", "ffn1_w_out")
GATHER_L0 = (("a", ("ffn1_w_in",)), ("b", ("ffn1_w_out",)), ("c", ("w_qkv", "w_o")),
             ("d", ("ffn2_w_in", "ffn2_w_out", "w_ple_gate", "w_ple_proj")))


def _gather_start(tag, slotted, after):
    return _split_start("ag_start_" + tag, slotted, _gather_copies([a.shape for a in slotted]), 3 * len(slotted), after)


def _gather_finish(tag, started, after):
    send_s, recv_s, arrays, _ = started
    return _pair_forward(_split_wait("ag_wait_" + tag, send_s, recv_s, arrays,
                                     _gather_copies([a.shape for a in arrays]), after))


def _scatter_exchange(tag, grads):
    n = len(grads)
    land = [lax.empty((g_.shape[0], g_.shape[1] // 2, g_.shape[2]), g_.dtype) for g_ in grads]
    return _split_start("rs_px_start_" + tag, list(grads) + land, _pair_exchange_copies([g_.shape for g_ in grads]),
                        n, None)


def _scatter_chips(tag, started, after):
    send_s, recv_s, arrays, _ = started
    n = len(arrays) // 2
    arrays = _split_wait("rs_px_wait_" + tag, send_s, recv_s, arrays,
                         _pair_exchange_copies([a.shape for a in arrays[:n]]), after)
    part = [_rs_add_pair(g_, r_) for g_, r_ in zip(arrays[:n], arrays[n:])]
    land = [lax.empty((3,) + p_.shape[1:], p_.dtype) for p_ in part]
    return _split_start("rs_start_" + tag, part + land, _scatter_copies(n), 3 * n, None)


def _scatter_share(tag, started, after):
    send_s, recv_s, arrays, _ = started
    n = len(arrays) // 2
    arrays = _split_wait("rs_wait_" + tag, send_s, recv_s, arrays, _scatter_copies(n), after)
    halves = [_rs_add_chips(p_, r_) for p_, r_ in zip(arrays[:n], arrays[n:])]
    return _split_start("rs_ps_start_" + tag, halves, _pair_share_copies([h_.shape for h_ in halves]), n, None)


def _scatter_done(tag, started, after):
    send_s, recv_s, arrays, _ = started
    return _split_wait("rs_ps_wait_" + tag, send_s, recv_s, arrays, _pair_share_copies([a.shape for a in arrays]), after)


def kernel(x, p, rel_bias, norm_ffn1, ffn1_w_in, ffn1_w_out, norm_mix, w_qkv, q_norm_a, k_norm_a, q_norm_b, k_norm_b, sink_b, w_o, norm_ffn2, ffn2_w_in, ffn2_w_out, norm_ple, w_ple_gate, w_ple_proj, loss_target, m_rel_bias, m_norm_ffn1, m_ffn1_w_in, m_ffn1_w_out, m_norm_mix, m_w_qkv, m_q_norm_a, m_k_norm_a, m_q_norm_b, m_k_norm_b, m_sink_b, m_w_o, m_norm_ffn2, m_ffn2_w_in, m_ffn2_w_out, m_norm_ple, m_w_ple_gate, m_w_ple_proj, v_rel_bias, v_norm_ffn1, v_ffn1_w_in, v_ffn1_w_out, v_norm_mix, v_w_qkv, v_q_norm_a, v_k_norm_a, v_q_norm_b, v_k_norm_b, v_sink_b, v_w_o, v_norm_ffn2, v_ffn2_w_in, v_ffn2_w_out, v_norm_ple, v_w_ple_gate, v_w_ple_proj):
    given = dict(locals())
    T, D = x.shape[1], x.shape[2]
    NL = norm_ffn1.shape[0]
    x0 = x.reshape(T, D)
    tgt = loss_target.reshape(T, D)
    n_a, n_b, n_kv = _qkv_layout(D)

    assert NL == 2
    slot = [{name: _cast_into_slot(given[name], l) for name in BIG} for l in range(NL)]
    ag, token = {}, None
    for tag, names in GATHER_L0:
        ag[tag] = _gather_start(tag, [slot[0][n] for n in names], token)
        token = ag[tag][3]
    ag_1 = _gather_start("1", [slot[1][n] for n in BIG], token)

    def arrived(tag, after):
        return dict(zip(dict(GATHER_L0)[tag], _gather_finish(tag, ag[tag], after)))

    def by_rows(a):
        return a.reshape(-1, a.shape[-1])

    def by_cols(a):
        return a.transpose(1, 0, 2).reshape(a.shape[1], -1)

    QW = N_CHIPS * w_qkv.shape[2]

    dils = tuple(d for _, d in DILATED_CONFIGS)
    cfg_a = [(w // (2 * d), d) for w, d in DILATED_CONFIGS]
    pad_a, pad_b = _window_pad(cfg_a[0][0]), _window_pad(SWA_RADIUS)
    bmaps_a, bmaps_b = jnp.asarray(_bucket_maps(cfg_a)), jnp.asarray(_bucket_maps([(SWA_RADIUS, 1)]))
    n_heads = rel_bias.shape[1] // 2
    bias_a = _bias_build(rel_bias, bmaps_a, 0, pad_a)
    bias_b = _bias_build(rel_bias, bmaps_b, n_heads, pad_b)
    no_sink = jnp.full((n_heads,), NEG, F32)

    def gains_row(l):
        ones = jnp.ones((n_a * LANES,), F32)
        return jnp.concatenate([
            jnp.tile(q_norm_a[l], 2 * n_a), jnp.tile(k_norm_a[l], 2 * n_a), ones,
            jnp.tile(q_norm_b[l], 2 * n_b), jnp.tile(k_norm_b[l], 2 * n_kv), jnp.ones((n_kv * LANES,), F32)]).reshape(1, QW)

    def forward_start(tag, state, after):
        landed = _split_wait("ag_wait_" + tag, state[0], state[1], state[2],
                             _gather_copies([a.shape for a in state[2]]), after)
        return _split_start("ag_pf_start_" + tag, landed, _forward_copies([a.shape for a in landed]), 3 * len(landed),
                            None)

    def forward_done(tag, names, pf, after):
        shapes = [a.shape for a in pf[2]]
        return dict(zip(names, _split_wait("ag_pf_wait_" + tag, pf[0], pf[1], pf[2], _forward_copies(shapes), after)))

    saved, weights = [], []
    xc = x0
    pf_1 = None
    for l in range(NL):
        s, w = {}, {}
        if l == 0:
            w.update(arrived("a", ag_1[3]))
            pf = {}

            def ffn1_w_out(act, w=w, pf=pf):
                w.update(arrived("b", act))
                w["ffn1_w_out"] = by_rows(w["ffn1_w_out"])
                pf["c"] = forward_start("c", ag["c"], w["ffn1_w_out"])
                return w["ffn1_w_out"], pf["c"][3]
        else:
            w.update(forward_done("1", BIG, pf_1, xc))
            ffn1_w_out = w["ffn1_w_out"] = by_rows(w["ffn1_w_out"])
        xc, s["ffn1"] = _ffn_forward(xc, norm_ffn1[l:l + 1], w["ffn1_w_in"], ffn1_w_out)
        s["x1"] = xc
        if l == 0:
            w.update(forward_done("c", dict(GATHER_L0)["c"], pf["c"], xc))
        w["w_qkv"] = by_cols(w["w_qkv"])
        w["w_o"] = by_rows(w["w_o"])
        s["h2"], s["raw"], s["qkv_a"], s["qkv_b"] = _qkv_proj(xc, norm_mix[l:l + 1], w["w_qkv"], gains_row(l))
        after_qkv = s["qkv_a"]
        if l == 0:
            pf["d"] = forward_start("d", ag["d"], s["qkv_a"])
            after_qkv = pf["d"][3]
        s["o_a"], s["lse_a"] = _attn_fwd(s["qkv_a"], bias_a, no_sink, dils, pad_a, after_qkv)
        s["o_b"], s["lse_b"] = _attn_fwd(s["qkv_b"], bias_b, sink_b[l], (1,), pad_b, s["qkv_b"])
        xc = _mm_res("attn_out", xc, [s["o_a"], s["o_b"]], w["w_o"], 1.0, s["o_b"])
        if l == 0:
            w.update(forward_done("d", dict(GATHER_L0)["d"], pf["d"], xc))
        w["w_ple_proj"] = by_cols(w["w_ple_proj"])
        for name in ("ffn2_w_out", "w_ple_gate"):
            w[name] = by_rows(w[name])
        xc, s["ffn2"] = _ffn_forward(xc, norm_ffn2[l:l + 1], w["ffn2_w_in"], w["ffn2_w_out"])
        s["x3"] = xc
        if l == 0:
            pf_1 = forward_start("1", ag_1, xc)
        if l < NL - 1:
            s["hn"], xc, s["gate"], s["pp"] = _ple_fwd(xc, norm_ple[l:l + 1], p, l, w["w_ple_gate"], w["w_ple_proj"],
                                                       pf_1[3] if l == 0 else xc)
        else:
            s["hn"], dx, s["gate"], s["pp"], loss_blk = _ple_fwd_loss(
                xc, norm_ple[l:l + 1], p, l, w["w_ple_gate"], w["w_ple_proj"], tgt)
        saved.append(s)
        weights.append(w)

    loss = lax.psum(loss_blk[0, 0], ("x", "y", "c"))

    gs = {name: [None] * NL for name in SMALL if name != "rel_bias"}
    dt_a, dt_b = [], []

    def layer_backward(l, dx, hooks):
        def at(point, ready, *more):
            return hooks[point](ready, *more) if point in hooks else ready

        s, w, gw = saved[l], weights[l], {}
        dx, gs["norm_ple"][l], dwg, dwp = _ple_bwd(dx, s["gate"], s["pp"], s["hn"], p, l, s["x3"],
                                                   norm_ple[l:l + 1], w["w_ple_gate"], at("start", dx))
        gw["w_ple_gate"] = dwg.reshape(N_CHIPS, -1, D)
        gw["w_ple_proj"] = dwp.reshape(dwp.shape[0], N_CHIPS, -1).transpose(1, 0, 2)
        dx, gs["norm_ffn2"][l], gw["ffn2_w_in"], gw["ffn2_w_out"] = _ffn_backward(
            dx, s["ffn2"], norm_ffn2[l:l + 1], w["ffn2_w_in"], w["ffn2_w_out"], at("after_ple", dx))
        do, dd, dwo = _mix_bwd_in(dx, w["w_o"], [s["o_a"], s["o_b"]])
        hwa = s["o_a"].shape[1]
        gw["w_o"] = dwo.reshape(N_CHIPS, -1, D)
        dqa, dka, dva, dt, _ = _attn_bwd(s["qkv_a"], bias_a, no_sink, dils, pad_a, do, s["lse_a"], dd, 0, do)
        dt_a.append(dt)
        dqb, dkb, dvb, dt, dsink = _attn_bwd(s["qkv_b"], bias_b, sink_b[l], (1,), pad_b, do, s["lse_b"], dd,
                                             hwa // LANES, at("after_attn_a", dqa))
        dt_b.append(dt)
        gs["sink_b"][l] = dsink[:, 0, ::HEAD_DIM].reshape(-1)
        dx, gs["norm_mix"][l], dgains, dwqkv = _qkv_bwd(
            s["raw"], gains_row(l), (dqa, dka, dva), (dqb, dkb, dvb), w["w_qkv"], s["h2"], s["x1"],
            norm_mix[l:l + 1], dx, at("before_qkv_bwd", dqb, [gw[n] for n in GROUP_MID]))
        dgv = dgains.reshape(-1, HEAD_DIM)
        gs["q_norm_a"][l] = dgv[:2 * n_a].sum(0)
        gs["k_norm_a"][l] = dgv[2 * n_a:4 * n_a].sum(0)
        gs["q_norm_b"][l] = dgv[6 * n_a:6 * n_a + 2 * n_b].sum(0)
        gs["k_norm_b"][l] = dgv[6 * n_a + 2 * n_b:6 * n_a + 2 * n_b + 2 * n_kv].sum(0)
        gw["w_qkv"] = dwqkv.reshape(D, N_CHIPS, -1).transpose(1, 0, 2)
        dx, gs["norm_ffn1"][l], gw["ffn1_w_in"], gw["ffn1_w_out"] = _ffn_backward(
            dx, s["ffn1"], norm_ffn1[l:l + 1], w["ffn1_w_in"], w["ffn1_w_out"], at("before_ffn1", dx))
        return dx, gw

    out = {}

    def adamw_group(names, l, grads, after):
        for name, g_ in zip(names, grads):
            out[name] = _adamw_layer(given[name], g_, given["m_" + name], given["v_" + name], l, out.get(name), after)
            after = out[name][0]
        return after

    dx, gw1 = layer_backward(NL - 1, dx, {})
    px_1 = _scatter_exchange("1", [gw1[n] for n in BIG])
    rs = {}

    def chips_1(ready):
        rs["chips_1"] = _scatter_chips("1", px_1, ready)
        return rs["chips_1"][3]

    def share_1(ready):
        rs["share_1"] = _scatter_share("1", rs["chips_1"], ready)
        return rs["share_1"][3]

    def exchange_0a(ready, grads):
        rs["px_0a"] = _scatter_exchange("0a", grads)
        return rs["px_0a"][3]

    def chips_0a(ready):
        rs["chips_0a"] = _scatter_chips("0a", rs["px_0a"], ready)
        return rs["chips_0a"][3]

    dx, gw0 = layer_backward(0, dx, {"start": lambda ready: px_1[3], "after_ple": chips_1, "after_attn_a": share_1,
                                     "before_qkv_bwd": exchange_0a, "before_ffn1": chips_0a})
    grad_x = dx.reshape(x.shape)

    px_0b = _scatter_exchange("0b", [gw0[n] for n in GROUP_LAST])
    d_rel_bias = (_bias_grad(dt_a, bmaps_a, 0) + _bias_grad(dt_b, bmaps_b, n_heads))[:, :rel_bias.shape[1]]
    small_g = [d_rel_bias] + [jnp.stack([t.reshape(-1) for t in gs[name]]) for name in SMALL[1:]]
    g_sum = _allreduce_small(_pack_small(small_g), px_0b[3])
    res = _ew("adamw_small", _adamw_fn,
              [_pack_small([given[n] for n in SMALL]), g_sum, _pack_small([given["m_" + n] for n in SMALL]),
               _pack_small([given["v_" + n] for n in SMALL])], [(LANES, F32)] * 4)
    like = [given[n] for n in SMALL]
    unpacked = [_unpack_small(r, like) for r in res]
    for i, name in enumerate(SMALL):
        out[name] = [u[i] for u in unpacked]

    share_0a = _scatter_share("0a", rs["chips_0a"], res[0])
    g_1 = _scatter_done("1", rs["share_1"], share_0a[3])
    chips_0b = _scatter_chips("0b", px_0b, g_1[0])
    ready = adamw_group(BIG, 1, g_1, chips_0b[3])
    ready = adamw_group(GROUP_MID, 0, _scatter_done("0a", share_0a, ready), ready)
    share_0b = _scatter_share("0b", chips_0b, ready)
    adamw_group(GROUP_LAST, 0, _scatter_done("0b", share_0b, share_0b[3]), share_0b[3])

    return (loss, grad_x, *[out[n][0] for n in WEIGHTS], *[out[n][1] for n in WEIGHTS],
            *[out[n][2] for n in WEIGHTS], *[out[n][3] for n in WEIGHTS])
```
